```python
import math
import jax
import jax.numpy as jnp
from jax import lax
import numpy as np

D_MODEL = 1024
BATCH = 8
SEQ = 2048
DEPTH = 4

N_MIXERS = 3
D_PLE = 256
NORM_EPS = 1e-6
Q_BLOCK = 128
NEG_INF = -1e30

MLA_HEADS = 16
MLA_Q_RANK = 384
MLA_KV_RANK = 256
MLA_NOPE = 64
MLA_ROPE = 32
MLA_V = 64
ROPE_THETA = 10000.0

DIL_PATTERNS = ((128, 1), (512, 4), (2048, 16))
DIL_GROUPS = len(DIL_PATTERNS)
DIL_HEADS = 16
DIL_HEAD_DIM = 64

REL_BUCKETS = 32
REL_MAX_DIST = 2048

FOX_HEADS = 16
FOX_HEAD_DIM = 64

D_FF = -(-8 * D_MODEL // (3 * 256)) * 256

kernel_name = 'hybrid_mla_dilated_fox_trunk'


def rms_norm(x, g):
    xf = x.astype(jnp.float32)
    y = xf * lax.rsqrt(jnp.mean(xf * xf, axis=-1, keepdims=True) + NORM_EPS)
    return (y * g.astype(jnp.float32)).astype(x.dtype)


def apply_rope(x, positions):
    half = x.shape[-1] // 2
    inv = ROPE_THETA ** (-jnp.arange(half, dtype=jnp.float32) / half)
    ang = positions.astype(jnp.float32)[:, :, None, None] * inv
    cos, sin = jnp.cos(ang), jnp.sin(ang)
    xf = x.astype(jnp.float32)
    x1, x2 = xf[..., :half], xf[..., half:]
    return jnp.concatenate([x1 * cos - x2 * sin, x2 * cos + x1 * sin], axis=-1).astype(x.dtype)


def causal_block_attention(q, k, v, scale, log_forget_cumsum=None):
    s_len = q.shape[1]
    outs = []
    for b0 in range(0, s_len, Q_BLOCK):
        b1 = b0 + Q_BLOCK
        qb, kb, vb = q[:, b0:b1], k[:, :b1], v[:, :b1]
        logits = jnp.einsum('bqhd,bkhd->bhqk', qb, kb).astype(jnp.float32) * scale
        if log_forget_cumsum is not None:
            c_q = jnp.transpose(log_forget_cumsum[:, b0:b1], (0, 2, 1))
            c_k = jnp.transpose(log_forget_cumsum[:, :b1], (0, 2, 1))
            logits = logits + (c_q[..., :, None] - c_k[..., None, :])
        q_idx = b0 + jnp.arange(Q_BLOCK)
        k_idx = jnp.arange(b1)
        causal = k_idx[None, :] <= q_idx[:, None]
        logits = jnp.where(causal, logits, NEG_INF)
        probs = jax.nn.softmax(logits, axis=-1).astype(v.dtype)
        outs.append(jnp.einsum('bhqk,bkhd->bqhd', probs, vb))
    return jnp.concatenate(outs, axis=1)


def t5_bucket(dist):
    max_exact = REL_BUCKETS // 2
    n = jnp.maximum(dist.astype(jnp.float32), 1.0)
    large = max_exact + (jnp.log(n / max_exact) / math.log(REL_MAX_DIST / max_exact)
                         * (REL_BUCKETS - max_exact)).astype(jnp.int32)
    large = jnp.minimum(large, REL_BUCKETS - 1)
    return jnp.where(dist < max_exact, dist, large)


def dilated_branch(q, k, v, dilation, span, bias_table):
    b, s_len, h, dh = q.shape
    sub_len = s_len // dilation
    n_blk = -(-sub_len // Q_BLOCK)
    pad_len = n_blk * Q_BLOCK

    def to_blocks(t):
        t = t.reshape(b, sub_len, dilation, h, dh).transpose(0, 2, 1, 3, 4)
        t = t.reshape(b * dilation, sub_len, h, dh)
        t = jnp.pad(t, ((0, 0), (0, pad_len - sub_len), (0, 0), (0, 0)))
        return t.reshape(b * dilation, n_blk, Q_BLOCK, h, dh)

    def with_previous(t):
        prev = jnp.concatenate([jnp.zeros_like(t[:, :1]), t[:, :-1]], axis=1)
        return jnp.concatenate([prev, t], axis=2)

    qs = to_blocks(q)
    kb = with_previous(to_blocks(k))
    vb = with_previous(to_blocks(v))

    logits = jnp.einsum('bnqhd,bnkhd->bnhqk', qs, kb).astype(jnp.float32) * (dh ** -0.5)
    i = jnp.arange(Q_BLOCK)
    j = jnp.arange(2 * Q_BLOCK)
    rel = Q_BLOCK + i[:, None] - j[None, :]
    bucket = t5_bucket(jnp.clip(rel, 0) * dilation)
    bias = jnp.transpose(bias_table[bucket].astype(jnp.float32), (2, 0, 1))
    key_pos = jnp.arange(n_blk)[:, None] * Q_BLOCK - Q_BLOCK + j[None, :]
    valid = ((rel >= 0) & (rel <= span))[None] & (key_pos >= 0)[:, None, :]
    logits = jnp.where(valid[None, :, None], logits + bias[None, None], NEG_INF)
    lse = jax.nn.logsumexp(logits, axis=-1, keepdims=True)
    probs = jnp.exp(logits - lse).astype(v.dtype)
    o = jnp.einsum('bnhqk,bnkhd->bnqhd', probs, vb)

    o = o.reshape(b * dilation, pad_len, h, dh)[:, :sub_len]
    o = o.reshape(b, dilation, sub_len, h, dh).transpose(0, 2, 1, 3, 4).reshape(b, s_len, h, dh)
    lse = jnp.transpose(lse[..., 0], (0, 1, 3, 2)).reshape(b * dilation, pad_len, h)[:, :sub_len]
    lse = lse.reshape(b, dilation, sub_len, h).transpose(0, 2, 1, 3).reshape(b, s_len, h)
    return o, lse


def mla_mixer(h, positions, w_a, q_norm, kv_norm, w_uq, w_ukv, w_o):
    b, s_len, _ = h.shape
    a = h @ w_a
    c_q = rms_norm(a[..., :MLA_Q_RANK], q_norm)
    c_kv = rms_norm(a[..., MLA_Q_RANK:MLA_Q_RANK + MLA_KV_RANK], kv_norm)
    k_rot = apply_rope(a[..., MLA_Q_RANK + MLA_KV_RANK:][:, :, None, :], positions)
    q = (c_q @ w_uq).reshape(b, s_len, MLA_HEADS, MLA_NOPE + MLA_ROPE)
    q = jnp.concatenate([q[..., :MLA_NOPE], apply_rope(q[..., MLA_NOPE:], positions)], axis=-1)
    kv = (c_kv @ w_ukv).reshape(b, s_len, MLA_HEADS, MLA_NOPE + MLA_V)
    k = jnp.concatenate([kv[..., :MLA_NOPE],
                         jnp.broadcast_to(k_rot, (b, s_len, MLA_HEADS, MLA_ROPE))], axis=-1)
    v = kv[..., MLA_NOPE:]
    o = causal_block_attention(q, k, v, (MLA_NOPE + MLA_ROPE) ** -0.5)
    return o.reshape(b, s_len, MLA_HEADS * MLA_V) @ w_o


def dilated_mixer(h, w_qkv, w_o, rel_bias):
    b, s_len, _ = h.shape
    qkv = (h @ w_qkv).reshape(b, s_len, DIL_GROUPS, 3, DIL_HEADS, DIL_HEAD_DIM)
    table = rel_bias.reshape(REL_BUCKETS, DIL_GROUPS, DIL_HEADS)
    outs, lses = [], []
    for g, (window, dilation) in enumerate(DIL_PATTERNS):
        o, lse = dilated_branch(qkv[:, :, g, 0], qkv[:, :, g, 1], qkv[:, :, g, 2],
                                dilation, window // dilation, table[:, g])
        outs.append(o)
        lses.append(lse)
    alpha = jax.nn.softmax(jnp.stack(lses, axis=0), axis=0)
    o = jnp.sum(alpha[..., None] * jnp.stack(outs, axis=0).astype(jnp.float32), axis=0).astype(h.dtype)
    return o.reshape(b, s_len, DIL_HEADS * DIL_HEAD_DIM) @ w_o


def fox_mixer(h, w_qkvf, b_f, w_o):
    b, s_len, _ = h.shape
    inner = FOX_HEADS * FOX_HEAD_DIM
    a = h @ w_qkvf
    qkv = a[..., :3 * inner].reshape(b, s_len, 3, FOX_HEADS, FOX_HEAD_DIM)
    log_f = jax.nn.log_sigmoid((a[..., 3 * inner:] + b_f).astype(jnp.float32))
    cum = jnp.cumsum(log_f, axis=1)
    o = causal_block_attention(qkv[:, :, 0], qkv[:, :, 1], qkv[:, :, 2], FOX_HEAD_DIM ** -0.5, cum)
    return o.reshape(b, s_len, inner) @ w_o


def swiglu(h, w_in, w_out):
    gu = h @ w_in
    return (jax.nn.silu(gu[..., :D_FF]) * gu[..., D_FF:]) @ w_out


def _fwd_setup_inputs(seed: int = 0) -> dict:
    key = jax.random.key(seed)
    ks = jax.random.split(key, 20)
    f32 = jnp.float32
    n_a, n_b, n_c = (len(range(m, DEPTH, N_MIXERS)) for m in range(N_MIXERS))

    def dense(k, shape):
        return jax.random.normal(k, shape, f32) * shape[-2] ** -0.5

    def gain(k, shape):
        return 1.0 + 0.1 * jax.random.normal(k, shape, f32)

    x = jax.random.normal(ks[0], (BATCH, SEQ, D_MODEL), f32)
    p = jax.random.normal(ks[1], (DEPTH, BATCH, SEQ, D_PLE), f32)
    offsets = jax.random.randint(ks[2], (BATCH, 1), 0, 4096, jnp.int32)
    positions = (offsets + jnp.arange(SEQ, dtype=jnp.int32)[None, :]).astype(jnp.int32)
    norm_g = gain(ks[3], (DEPTH, 4, D_MODEL))
    ffn_w_in = dense(ks[4], (DEPTH, D_MODEL, 2 * D_FF))
    ffn_w_out = dense(ks[5], (DEPTH, D_FF, D_MODEL))
    ple_w_proj = dense(ks[6], (DEPTH, D_PLE, D_MODEL))
    ple_w_gate = dense(ks[7], (DEPTH, D_MODEL, D_MODEL))
    rel_bias = 0.5 * jax.random.normal(ks[8], (REL_BUCKETS, DIL_GROUPS * DIL_HEADS), f32)
    mla_w_a = dense(ks[9], (n_a, D_MODEL, MLA_Q_RANK + MLA_KV_RANK + MLA_ROPE))
    mla_q_norm = gain(ks[10], (n_a, MLA_Q_RANK))
    mla_kv_norm = gain(ks[11], (n_a, MLA_KV_RANK))
    mla_w_uq = dense(ks[12], (n_a, MLA_Q_RANK, MLA_HEADS * (MLA_NOPE + MLA_ROPE)))
    mla_w_ukv = dense(ks[13], (n_a, MLA_KV_RANK, MLA_HEADS * (MLA_NOPE + MLA_V)))
    mla_w_o = dense(ks[14], (n_a, MLA_HEADS * MLA_V, D_MODEL))
    dil_w_qkv = dense(ks[15], (n_b, D_MODEL, DIL_GROUPS * 3 * DIL_HEADS * DIL_HEAD_DIM))
    dil_w_o = dense(ks[16], (n_b, DIL_HEADS * DIL_HEAD_DIM, D_MODEL))
    fox_w_qkvf = dense(ks[17], (n_c, D_MODEL, 3 * FOX_HEADS * FOX_HEAD_DIM + FOX_HEADS))
    fox_b_f = jax.random.uniform(ks[18], (n_c, FOX_HEADS), f32, 1.0, 5.0)
    fox_w_o = dense(ks[19], (n_c, FOX_HEADS * FOX_HEAD_DIM, D_MODEL))
    return {'x': x, 'p': p, 'positions': positions, 'norm_g': norm_g,
            'ffn_w_in': ffn_w_in, 'ffn_w_out': ffn_w_out,
            'ple_w_proj': ple_w_proj, 'ple_w_gate': ple_w_gate, 'rel_bias': rel_bias,
            'mla_w_a': mla_w_a, 'mla_q_norm': mla_q_norm, 'mla_kv_norm': mla_kv_norm,
            'mla_w_uq': mla_w_uq, 'mla_w_ukv': mla_w_ukv, 'mla_w_o': mla_w_o,
            'dil_w_qkv': dil_w_qkv, 'dil_w_o': dil_w_o,
            'fox_w_qkvf': fox_w_qkvf, 'fox_b_f': fox_b_f, 'fox_w_o': fox_w_o}


def _fwd_reference(x, p, positions, norm_g, ffn_w_in, ffn_w_out, ple_w_proj, ple_w_gate, rel_bias,
              mla_w_a, mla_q_norm, mla_kv_norm, mla_w_uq, mla_w_ukv, mla_w_o,
              dil_w_qkv, dil_w_o, fox_w_qkvf, fox_b_f, fox_w_o):
    h = x
    for i in range(DEPTH):
        mixer, j = i % N_MIXERS, i // N_MIXERS
        g = norm_g[i]
        hn = rms_norm(h, g[0])
        if mixer == 0:
            y = mla_mixer(hn, positions, mla_w_a[j], mla_q_norm[j], mla_kv_norm[j],
                          mla_w_uq[j], mla_w_ukv[j], mla_w_o[j])
        elif mixer == 1:
            y = dilated_mixer(hn, dil_w_qkv[j], dil_w_o[j], rel_bias)
        else:
            y = fox_mixer(hn, fox_w_qkvf[j], fox_b_f[j], fox_w_o[j])
        h = h + rms_norm(y, g[1])
        h = h + rms_norm(swiglu(rms_norm(h, g[2]), ffn_w_in[i], ffn_w_out[i]), g[3])
        h = h + (p[i] @ ple_w_proj[i]) * jax.nn.sigmoid(h @ ple_w_gate[i])
    return h


import jax as _jax
import jax.numpy as _jnp

TWIN_FORMAT = 'train_step'
FWD_PARAMS = ['x', 'p', 'positions', 'norm_g', 'ffn_w_in', 'ffn_w_out', 'ple_w_proj', 'ple_w_gate', 'rel_bias', 'mla_w_a', 'mla_q_norm', 'mla_kv_norm', 'mla_w_uq', 'mla_w_ukv', 'mla_w_o', 'dil_w_qkv', 'dil_w_o', 'fox_w_qkvf', 'fox_b_f', 'fox_w_o']
TWIN_WEIGHTS = ['norm_g', 'ffn_w_in', 'ffn_w_out', 'ple_w_proj', 'ple_w_gate', 'rel_bias', 'mla_w_a', 'mla_q_norm', 'mla_kv_norm', 'mla_w_uq', 'mla_w_ukv', 'mla_w_o', 'dil_w_qkv', 'dil_w_o', 'fox_w_qkvf', 'fox_b_f', 'fox_w_o']
TWIN_DIFF_INPUT = 'x'
TWIN_INPUTS = ['x', 'p', 'positions', 'norm_g', 'ffn_w_in', 'ffn_w_out', 'ple_w_proj', 'ple_w_gate', 'rel_bias', 'mla_w_a', 'mla_q_norm', 'mla_kv_norm', 'mla_w_uq', 'mla_w_ukv', 'mla_w_o', 'dil_w_qkv', 'dil_w_o', 'fox_w_qkvf', 'fox_b_f', 'fox_w_o', 'loss_target', 'm_norm_g', 'm_ffn_w_in', 'm_ffn_w_out', 'm_ple_w_proj', 'm_ple_w_gate', 'm_rel_bias', 'm_mla_w_a', 'm_mla_q_norm', 'm_mla_kv_norm', 'm_mla_w_uq', 'm_mla_w_ukv', 'm_mla_w_o', 'm_dil_w_qkv', 'm_dil_w_o', 'm_fox_w_qkvf', 'm_fox_b_f', 'm_fox_w_o', 'v_norm_g', 'v_ffn_w_in', 'v_ffn_w_out', 'v_ple_w_proj', 'v_ple_w_gate', 'v_rel_bias', 'v_mla_w_a', 'v_mla_q_norm', 'v_mla_kv_norm', 'v_mla_w_uq', 'v_mla_w_ukv', 'v_mla_w_o', 'v_dil_w_qkv', 'v_dil_w_o', 'v_fox_w_qkvf', 'v_fox_b_f', 'v_fox_w_o']
TWIN_OUTPUTS = ['loss', 'grad_x', 'grad_norm_g', 'grad_ffn_w_in', 'grad_ffn_w_out', 'grad_ple_w_proj', 'grad_ple_w_gate', 'grad_rel_bias', 'grad_mla_w_a', 'grad_mla_q_norm', 'grad_mla_kv_norm', 'grad_mla_w_uq', 'grad_mla_w_ukv', 'grad_mla_w_o', 'grad_dil_w_qkv', 'grad_dil_w_o', 'grad_fox_w_qkvf', 'grad_fox_b_f', 'grad_fox_w_o', 'delta_norm_g', 'delta_ffn_w_in', 'delta_ffn_w_out', 'delta_ple_w_proj', 'delta_ple_w_gate', 'delta_rel_bias', 'delta_mla_w_a', 'delta_mla_q_norm', 'delta_mla_kv_norm', 'delta_mla_w_uq', 'delta_mla_w_ukv', 'delta_mla_w_o', 'delta_dil_w_qkv', 'delta_dil_w_o', 'delta_fox_w_qkvf', 'delta_fox_b_f', 'delta_fox_w_o', 'new_m_norm_g', 'new_m_ffn_w_in', 'new_m_ffn_w_out', 'new_m_ple_w_proj', 'new_m_ple_w_gate', 'new_m_rel_bias', 'new_m_mla_w_a', 'new_m_mla_q_norm', 'new_m_mla_kv_norm', 'new_m_mla_w_uq', 'new_m_mla_w_ukv', 'new_m_mla_w_o', 'new_m_dil_w_qkv', 'new_m_dil_w_o', 'new_m_fox_w_qkvf', 'new_m_fox_b_f', 'new_m_fox_w_o', 'new_v_norm_g', 'new_v_ffn_w_in', 'new_v_ffn_w_out', 'new_v_ple_w_proj', 'new_v_ple_w_gate', 'new_v_rel_bias', 'new_v_mla_w_a', 'new_v_mla_q_norm', 'new_v_mla_kv_norm', 'new_v_mla_w_uq', 'new_v_mla_w_ukv', 'new_v_mla_w_o', 'new_v_dil_w_qkv', 'new_v_dil_w_o', 'new_v_fox_w_qkvf', 'new_v_fox_b_f', 'new_v_fox_w_o']
TWIN_LEAF_KINDS = {'loss': 'loss', 'grad_x': 'grad_x', 'grad_norm_g': 'grad_w', 'grad_ffn_w_in': 'grad_w', 'grad_ffn_w_out': 'grad_w', 'grad_ple_w_proj': 'grad_w', 'grad_ple_w_gate': 'grad_w', 'grad_rel_bias': 'grad_w', 'grad_mla_w_a': 'grad_w', 'grad_mla_q_norm': 'grad_w', 'grad_mla_kv_norm': 'grad_w', 'grad_mla_w_uq': 'grad_w', 'grad_mla_w_ukv': 'grad_w', 'grad_mla_w_o': 'grad_w', 'grad_dil_w_qkv': 'grad_w', 'grad_dil_w_o': 'grad_w', 'grad_fox_w_qkvf': 'grad_w', 'grad_fox_b_f': 'grad_w', 'grad_fox_w_o': 'grad_w', 'delta_norm_g': 'delta_w', 'delta_ffn_w_in': 'delta_w', 'delta_ffn_w_out': 'delta_w', 'delta_ple_w_proj': 'delta_w', 'delta_ple_w_gate': 'delta_w', 'delta_rel_bias': 'delta_w', 'delta_mla_w_a': 'delta_w', 'delta_mla_q_norm': 'delta_w', 'delta_mla_kv_norm': 'delta_w', 'delta_mla_w_uq': 'delta_w', 'delta_mla_w_ukv': 'delta_w', 'delta_mla_w_o': 'delta_w', 'delta_dil_w_qkv': 'delta_w', 'delta_dil_w_o': 'delta_w', 'delta_fox_w_qkvf': 'delta_w', 'delta_fox_b_f': 'delta_w', 'delta_fox_w_o': 'delta_w', 'new_m_norm_g': 'new_m', 'new_m_ffn_w_in': 'new_m', 'new_m_ffn_w_out': 'new_m', 'new_m_ple_w_proj': 'new_m', 'new_m_ple_w_gate': 'new_m', 'new_m_rel_bias': 'new_m', 'new_m_mla_w_a': 'new_m', 'new_m_mla_q_norm': 'new_m', 'new_m_mla_kv_norm': 'new_m', 'new_m_mla_w_uq': 'new_m', 'new_m_mla_w_ukv': 'new_m', 'new_m_mla_w_o': 'new_m', 'new_m_dil_w_qkv': 'new_m', 'new_m_dil_w_o': 'new_m', 'new_m_fox_w_qkvf': 'new_m', 'new_m_fox_b_f': 'new_m', 'new_m_fox_w_o': 'new_m', 'new_v_norm_g': 'new_v', 'new_v_ffn_w_in': 'new_v', 'new_v_ffn_w_out': 'new_v', 'new_v_ple_w_proj': 'new_v', 'new_v_ple_w_gate': 'new_v', 'new_v_rel_bias': 'new_v', 'new_v_mla_w_a': 'new_v', 'new_v_mla_q_norm': 'new_v', 'new_v_mla_kv_norm': 'new_v', 'new_v_mla_w_uq': 'new_v', 'new_v_mla_w_ukv': 'new_v', 'new_v_mla_w_o': 'new_v', 'new_v_dil_w_qkv': 'new_v', 'new_v_dil_w_o': 'new_v', 'new_v_fox_w_qkvf': 'new_v', 'new_v_fox_b_f': 'new_v', 'new_v_fox_w_o': 'new_v'}


def _forward(args):
    return _fwd_reference(*[args[k] for k in FWD_PARAMS])


def _output_shape():
    out = _jax.eval_shape(lambda: _forward(_fwd_setup_inputs(0)))
    return out.shape, out.dtype

N_MICROBATCH = 1
ADAM_LR = 0.001
ADAM_B1 = 0.9
ADAM_B2 = 0.999
ADAM_EPS = 1e-08
ADAM_WD = 0.01
ADAM_STEP = 10
PER_EXAMPLE_BATCH_AXIS = {'x': 0, 'p': 1, 'positions': 0, 'loss_target': 0}
SHARED_INPUTS = []
_WEIGHT_DTYPES = {'norm_g': _jnp.float32, 'ffn_w_in': _jnp.float32, 'ffn_w_out': _jnp.float32, 'ple_w_proj': _jnp.float32, 'ple_w_gate': _jnp.float32, 'rel_bias': _jnp.float32, 'mla_w_a': _jnp.float32, 'mla_q_norm': _jnp.float32, 'mla_kv_norm': _jnp.float32, 'mla_w_uq': _jnp.float32, 'mla_w_ukv': _jnp.float32, 'mla_w_o': _jnp.float32, 'dil_w_qkv': _jnp.float32, 'dil_w_o': _jnp.float32, 'fox_w_qkvf': _jnp.float32, 'fox_b_f': _jnp.float32, 'fox_w_o': _jnp.float32}
MOMENT_SCALE = {'norm_g': 1.426140e+01, 'ffn_w_in': 1.671527e+00, 'ffn_w_out': 3.172821e+00, 'ple_w_proj': 5.419986e-01, 'ple_w_gate': 4.832944e-01, 'rel_bias': 2.518852e+00, 'mla_w_a': 1.404080e+01, 'mla_q_norm': 2.542374e+00, 'mla_kv_norm': 1.952134e+01, 'mla_w_uq': 1.250644e+00, 'mla_w_ukv': 7.521164e+00, 'mla_w_o': 1.089552e+01, 'dil_w_qkv': 4.058192e+00, 'dil_w_o': 1.198480e+01, 'fox_w_qkvf': 5.167976e+00, 'fox_b_f': 5.700702e+00, 'fox_w_o': 9.028444e+00}


def _to_microbatches(a, axis):
    t = _jnp.moveaxis(a, axis, 0)
    t = t.reshape((N_MICROBATCH, t.shape[0] // N_MICROBATCH) + t.shape[1:])
    return _jnp.moveaxis(t, 1, axis + 1)


def setup_inputs(seed: int = 0) -> dict:
    inp = _fwd_setup_inputs(seed)
    key = _jax.random.fold_in(_jax.random.key(seed), 7919)
    shape, _ = _output_shape()
    out = dict(inp)
    out["loss_target"] = _jax.random.normal(_jax.random.fold_in(key, 0), shape, _jnp.float32)
    for i, name in enumerate(TWIN_WEIGHTS):
        w = inp[name].astype(_jnp.float32)
        if MOMENT_SCALE is None:
            s = _jnp.sqrt(_jnp.mean(_jnp.square(w)) + 1e-30)
        else:
            s = MOMENT_SCALE[name]
        km, kv = _jax.random.split(_jax.random.fold_in(key, i + 1))
        out[name] = w
        out["m_" + name] = s * _jax.random.normal(km, w.shape, _jnp.float32)
        out["v_" + name] = (s * s) * _jax.random.uniform(kv, w.shape, _jnp.float32, 0.5, 1.5)
    if N_MICROBATCH > 1:
        for name, axis in PER_EXAMPLE_BATCH_AXIS.items():
            out[name] = _to_microbatches(out[name], axis)
    return {'x': out['x'], 'p': out['p'], 'positions': out['positions'], 'norm_g': out['norm_g'], 'ffn_w_in': out['ffn_w_in'], 'ffn_w_out': out['ffn_w_out'], 'ple_w_proj': out['ple_w_proj'], 'ple_w_gate': out['ple_w_gate'], 'rel_bias': out['rel_bias'], 'mla_w_a': out['mla_w_a'], 'mla_q_norm': out['mla_q_norm'], 'mla_kv_norm': out['mla_kv_norm'], 'mla_w_uq': out['mla_w_uq'], 'mla_w_ukv': out['mla_w_ukv'], 'mla_w_o': out['mla_w_o'], 'dil_w_qkv': out['dil_w_qkv'], 'dil_w_o': out['dil_w_o'], 'fox_w_qkvf': out['fox_w_qkvf'], 'fox_b_f': out['fox_b_f'], 'fox_w_o': out['fox_w_o'], 'loss_target': out['loss_target'], 'm_norm_g': out['m_norm_g'], 'm_ffn_w_in': out['m_ffn_w_in'], 'm_ffn_w_out': out['m_ffn_w_out'], 'm_ple_w_proj': out['m_ple_w_proj'], 'm_ple_w_gate': out['m_ple_w_gate'], 'm_rel_bias': out['m_rel_bias'], 'm_mla_w_a': out['m_mla_w_a'], 'm_mla_q_norm': out['m_mla_q_norm'], 'm_mla_kv_norm': out['m_mla_kv_norm'], 'm_mla_w_uq': out['m_mla_w_uq'], 'm_mla_w_ukv': out['m_mla_w_ukv'], 'm_mla_w_o': out['m_mla_w_o'], 'm_dil_w_qkv': out['m_dil_w_qkv'], 'm_dil_w_o': out['m_dil_w_o'], 'm_fox_w_qkvf': out['m_fox_w_qkvf'], 'm_fox_b_f': out['m_fox_b_f'], 'm_fox_w_o': out['m_fox_w_o'], 'v_norm_g': out['v_norm_g'], 'v_ffn_w_in': out['v_ffn_w_in'], 'v_ffn_w_out': out['v_ffn_w_out'], 'v_ple_w_proj': out['v_ple_w_proj'], 'v_ple_w_gate': out['v_ple_w_gate'], 'v_rel_bias': out['v_rel_bias'], 'v_mla_w_a': out['v_mla_w_a'], 'v_mla_q_norm': out['v_mla_q_norm'], 'v_mla_kv_norm': out['v_mla_kv_norm'], 'v_mla_w_uq': out['v_mla_w_uq'], 'v_mla_w_ukv': out['v_mla_w_ukv'], 'v_mla_w_o': out['v_mla_w_o'], 'v_dil_w_qkv': out['v_dil_w_qkv'], 'v_dil_w_o': out['v_dil_w_o'], 'v_fox_w_qkvf': out['v_fox_w_qkvf'], 'v_fox_b_f': out['v_fox_b_f'], 'v_fox_w_o': out['v_fox_w_o']}


def _loss(weights, diff, rest, loss_target):
    with _jax.named_scope("forward"):
        args = {**rest, TWIN_DIFF_INPUT: diff, **{k: w.astype(_WEIGHT_DTYPES[k]) for k, w in weights.items()}}
        y = _forward(args)
    with _jax.named_scope("loss_head"):
        err = _jnp.square(y.astype(_jnp.float32) - loss_target)
        return 0.5 * _jnp.sum(_jnp.mean(err, axis=-1)) if err.ndim else 0.5 * err


def _adamw(w, g, m, v):
    m = ADAM_B1 * m + (1.0 - ADAM_B1) * g
    v = ADAM_B2 * v + (1.0 - ADAM_B2) * _jnp.square(g)
    m_hat = m / (1.0 - ADAM_B1 ** ADAM_STEP)
    v_hat = v / (1.0 - ADAM_B2 ** ADAM_STEP)
    delta = -ADAM_LR * (m_hat / (_jnp.sqrt(v_hat) + ADAM_EPS) + ADAM_WD * w)
    return delta, m, v


def reference(x, p, positions, norm_g, ffn_w_in, ffn_w_out, ple_w_proj, ple_w_gate, rel_bias, mla_w_a, mla_q_norm, mla_kv_norm, mla_w_uq, mla_w_ukv, mla_w_o, dil_w_qkv, dil_w_o, fox_w_qkvf, fox_b_f, fox_w_o, loss_target, m_norm_g, m_ffn_w_in, m_ffn_w_out, m_ple_w_proj, m_ple_w_gate, m_rel_bias, m_mla_w_a, m_mla_q_norm, m_mla_kv_norm, m_mla_w_uq, m_mla_w_ukv, m_mla_w_o, m_dil_w_qkv, m_dil_w_o, m_fox_w_qkvf, m_fox_b_f, m_fox_w_o, v_norm_g, v_ffn_w_in, v_ffn_w_out, v_ple_w_proj, v_ple_w_gate, v_rel_bias, v_mla_w_a, v_mla_q_norm, v_mla_kv_norm, v_mla_w_uq, v_mla_w_ukv, v_mla_w_o, v_dil_w_qkv, v_dil_w_o, v_fox_w_qkvf, v_fox_b_f, v_fox_w_o):
    given = dict(x=x, p=p, positions=positions, norm_g=norm_g, ffn_w_in=ffn_w_in, ffn_w_out=ffn_w_out, ple_w_proj=ple_w_proj, ple_w_gate=ple_w_gate, rel_bias=rel_bias, mla_w_a=mla_w_a, mla_q_norm=mla_q_norm, mla_kv_norm=mla_kv_norm, mla_w_uq=mla_w_uq, mla_w_ukv=mla_w_ukv, mla_w_o=mla_w_o, dil_w_qkv=dil_w_qkv, dil_w_o=dil_w_o, fox_w_qkvf=fox_w_qkvf, fox_b_f=fox_b_f, fox_w_o=fox_w_o, loss_target=loss_target, m_norm_g=m_norm_g, m_ffn_w_in=m_ffn_w_in, m_ffn_w_out=m_ffn_w_out, m_ple_w_proj=m_ple_w_proj, m_ple_w_gate=m_ple_w_gate, m_rel_bias=m_rel_bias, m_mla_w_a=m_mla_w_a, m_mla_q_norm=m_mla_q_norm, m_mla_kv_norm=m_mla_kv_norm, m_mla_w_uq=m_mla_w_uq, m_mla_w_ukv=m_mla_w_ukv, m_mla_w_o=m_mla_w_o, m_dil_w_qkv=m_dil_w_qkv, m_dil_w_o=m_dil_w_o, m_fox_w_qkvf=m_fox_w_qkvf, m_fox_b_f=m_fox_b_f, m_fox_w_o=m_fox_w_o, v_norm_g=v_norm_g, v_ffn_w_in=v_ffn_w_in, v_ffn_w_out=v_ffn_w_out, v_ple_w_proj=v_ple_w_proj, v_ple_w_gate=v_ple_w_gate, v_rel_bias=v_rel_bias, v_mla_w_a=v_mla_w_a, v_mla_q_norm=v_mla_q_norm, v_mla_kv_norm=v_mla_kv_norm, v_mla_w_uq=v_mla_w_uq, v_mla_w_ukv=v_mla_w_ukv, v_mla_w_o=v_mla_w_o, v_dil_w_qkv=v_dil_w_qkv, v_dil_w_o=v_dil_w_o, v_fox_w_qkvf=v_fox_w_qkvf, v_fox_b_f=v_fox_b_f, v_fox_w_o=v_fox_w_o)
    weights = {n: given[n] for n in TWIN_WEIGHTS}
    shared = {n: given[n] for n in SHARED_INPUTS}
    per_example = {n: given[n] for n in ['x', 'p', 'positions']}
    grad_fn = _jax.value_and_grad(_loss, argnums=(0, 1))

    def one_microbatch(ex, loss_target):
        ex = dict(ex)
        diff = ex.pop(TWIN_DIFF_INPUT)
        return grad_fn(weights, diff, {**shared, **ex}, loss_target)

    if N_MICROBATCH == 1:
        loss, (grad_w, grad_x) = one_microbatch(per_example, given["loss_target"])
    else:
        def body(carry, xs):
            loss_sum, grad_sum = carry
            l_k, (gw_k, gx_k) = one_microbatch(xs[0], xs[1])
            with _jax.named_scope("update"):
                return (loss_sum + l_k, _jax.tree.map(_jnp.add, grad_sum, gw_k)), gx_k

        init = (_jnp.zeros((), _jnp.float32), _jax.tree.map(_jnp.zeros_like, weights))
        (loss, grad_w), grad_x = _jax.lax.scan(body, init, (per_example, given["loss_target"]))
    with _jax.named_scope("update"):
        delta_w, new_m, new_v = {}, {}, {}
        for n in TWIN_WEIGHTS:
            delta_w[n], new_m[n], new_v[n] = _adamw(weights[n], grad_w[n], given["m_" + n], given["v_" + n])
    return (loss, grad_x, *[grad_w[n] for n in TWIN_WEIGHTS], *[delta_w[n] for n in TWIN_WEIGHTS],
            *[new_m[n] for n in TWIN_WEIGHTS], *[new_v[n] for n in TWIN_WEIGHTS])
```

```python
import functools
import math

import numpy as np
import jax
import jax.numpy as jnp
from jax import lax
from jax.experimental import pallas as pl
from jax.experimental.pallas import tpu as pltpu

F32 = jnp.float32
BF16 = jnp.bfloat16
MESH_ID = pl.DeviceIdType.MESH

N_DEV = 8
S = 2048
D = 1024
DEPTH = 4
D_FF = 2816
D_PLE = 256
EPS = 1e-6
NEG = -1e30
LANES = 128
HEADS = 16
PAIRS = 8
Q_RANK = 384
KV_RANK = 256
QBLK = 128
DIL = ((128, 1), (512, 4), (2048, 16))
REL_BUCKETS = 32
FOX_W = 3200
MLA_A_W = 768
VMEM_LIMIT = 56 * 1024 * 1024

ADAM_LR, ADAM_B1, ADAM_B2, ADAM_EPS, ADAM_WD, ADAM_STEP = 1e-3, 0.9, 0.999, 1e-8, 0.01, 10


BIG = (
    ("ffn_w_in", (4, 1024, 704), 2), ("ffn_w_out", (4, 352, 1024), 1),
    ("ple_w_proj", (4, 256, 128), 2), ("ple_w_gate", (4, 128, 1024), 1),
    ("mla_w_a", (2, 128, 672), 1), ("mla_w_uq", (2, 384, 192), 2),
    ("mla_w_ukv", (2, 256, 256), 2), ("mla_w_o", (2, 128, 1024), 1),
    ("dil_w_qkv", (1, 1024, 1152), 2), ("dil_w_o", (1, 128, 1024), 1),
    ("fox_w_qkvf", (1, 1024, 386), 2), ("fox_w_o", (1, 128, 1024), 1),
)
BIG_N = sum(int(np.prod(s)) for _, s, _ in BIG)
ADAM_ROWS = 2560
BIG_ROWS = -(-BIG_N // (LANES * ADAM_ROWS)) * ADAM_ROWS
SMALL_SHARDED = (("norm_g", (4, 4, 128), 2), ("mla_q_norm", (2, 48), 1), ("mla_kv_norm", (2, 32), 1))
SMALL_REPL = (("rel_bias", (32, 48)), ("fox_b_f", (1, 16)))
WEIGHTS = ("norm_g", "ffn_w_in", "ffn_w_out", "ple_w_proj", "ple_w_gate", "rel_bias", "mla_w_a", "mla_q_norm",
           "mla_kv_norm", "mla_w_uq", "mla_w_ukv", "mla_w_o", "dil_w_qkv", "dil_w_o", "fox_w_qkvf", "fox_b_f",
           "fox_w_o")


def _rows(n):
    return -(-n // (8 * LANES)) * 8


def _mla_q_cols():
    idx = -np.ones((HEADS, LANES), np.int64)
    for h in range(HEADS):
        idx[h, 0:16] = 96 * h + 64 + np.arange(16)
        idx[h, 16:48] = 96 * h + np.arange(32)
        idx[h, 64:80] = 96 * h + 80 + np.arange(16)
        idx[h, 80:112] = 96 * h + 32 + np.arange(32)
    return idx.reshape(-1)


def _mla_kv_cols():
    k = -np.ones((HEADS, LANES), np.int64)
    v = np.zeros((HEADS, 64), np.int64)
    for h in range(HEADS):
        k[h, 16:48] = 128 * h + np.arange(32)
        k[h, 80:112] = 128 * h + 32 + np.arange(32)
        v[h] = 128 * h + 64 + np.arange(64)
    return np.concatenate([k.reshape(-1), v.reshape(-1)])


def _mla_a_cols():
    idx = -np.ones((MLA_A_W,), np.int64)
    idx[:640] = np.arange(640)
    idx[640:656] = 640 + np.arange(16)
    idx[704:720] = 656 + np.arange(16)
    return idx


def _fox_cols():
    idx = -np.ones((FOX_W,), np.int64)
    idx[:3088] = np.arange(3088)
    return idx


def _inverse(idx, n):
    inv = np.zeros((n,), np.int64)
    pos = np.nonzero(idx >= 0)[0]
    inv[idx[pos]] = pos
    return inv


COLS = {"mla_w_uq": _mla_q_cols(), "mla_w_ukv": _mla_kv_cols(), "mla_w_a": _mla_a_cols(), "fox_w_qkvf": _fox_cols()}
COLS_INV = {"mla_w_uq": _inverse(COLS["mla_w_uq"], 1536), "mla_w_ukv": _inverse(COLS["mla_w_ukv"], 2048),
            "mla_w_a": _inverse(COLS["mla_w_a"], 672), "fox_w_qkvf": _inverse(COLS["fox_w_qkvf"], 3088)}


def _spread_cols(w, idx):
    wz = jnp.concatenate([w, jnp.zeros(w.shape[:-1] + (1,), w.dtype)], axis=-1)
    return jnp.take(wz, jnp.asarray(np.where(idx < 0, w.shape[-1], idx)), axis=-1)


def _t5_bucket_np(dist):
    max_exact = REL_BUCKETS // 2
    n = np.maximum(dist.astype(np.float32), np.float32(1.0))
    large = max_exact + (np.log(n / np.float32(max_exact)) / np.float32(math.log(2048 / max_exact))
                         * np.float32(REL_BUCKETS - max_exact)).astype(np.int32)
    large = np.minimum(large, REL_BUCKETS - 1)
    return np.where(dist < max_exact, dist, large)


def _bucket_onehot(dilation):
    i = np.arange(QBLK)[:, None]
    j = np.arange(2 * QBLK)[None, :]
    bucket = _t5_bucket_np(np.clip(QBLK + i - j, 0, None) * dilation).reshape(-1)
    return (np.arange(REL_BUCKETS)[:, None] == bucket[None, :]).astype(np.float32)


def _rope_inv_lanes():
    half = 16
    inv = (np.float32(10000.0) ** (-np.arange(half, dtype=np.float32) / np.float32(half))).astype(np.float32)
    t = np.zeros((1, LANES), np.float32)
    t[0, 0:16] = inv
    t[0, 64:80] = inv
    return t


def _params(sem=None):
    return pltpu.CompilerParams(dimension_semantics=sem, vmem_limit_bytes=VMEM_LIMIT)


def _tile(dim, target):
    if dim <= target:
        return dim
    t = (target // LANES) * LANES
    while dim % t:
        t -= LANES
    return t


_DIMS = {"nn": (((1,), (0,)), ((), ())), "nt": (((1,), (1,)), ((), ())), "tn": (((0,), (0,)), ((), ()))}


def mm(a, b, mode, name, out_dtype=F32, precise=False, add=None, tm=1024, tn=512, tk=512):
    if mode == "tn":
        (K, M), N = a.shape, b.shape[1]
    else:
        (M, K), N = a.shape, (b.shape[1] if mode == "nn" else b.shape[0])
    tm, tn, tk = _tile(M, tm), _tile(N, tn), _tile(K, tk)
    nk = K // tk
    a_spec = pl.BlockSpec((tk, tm), lambda i, j, k: (k, i)) if mode == "tn" else pl.BlockSpec((tm, tk), lambda i, j, k: (i, k))
    b_spec = pl.BlockSpec((tn, tk), lambda i, j, k: (j, k)) if mode == "nt" else pl.BlockSpec((tk, tn), lambda i, j, k: (k, j))

    def body(a_ref, b_ref, *rest):
        o_ref, acc = rest[-2], rest[-1]
        k = pl.program_id(2)

        @pl.when(k == 0)
        def _():
            acc[...] = jnp.zeros_like(acc)

        if precise:
            acc[...] += lax.dot_general(a_ref[...], b_ref[...], _DIMS[mode], precision=lax.Precision.HIGHEST,
                                        preferred_element_type=F32)
        else:
            acc[...] += lax.dot_general(a_ref[...].astype(BF16), b_ref[...].astype(BF16), _DIMS[mode],
                                        preferred_element_type=F32)

        @pl.when(k == nk - 1)
        def _():
            r = acc[...] + rest[0][...] if add is not None else acc[...]
            o_ref[...] = r.astype(o_ref.dtype)

    o_spec = pl.BlockSpec((tm, tn), lambda i, j, k: (i, j))
    return pl.pallas_call(
        body, name=name, grid=(M // tm, N // tn, nk), in_specs=[a_spec, b_spec] + ([o_spec] if add is not None else []),
        out_specs=o_spec, out_shape=jax.ShapeDtypeStruct((M, N), out_dtype),
        scratch_shapes=[pltpu.VMEM((tm, tn), F32)], compiler_params=_params(("parallel", "parallel", "arbitrary")),
    )(*([a, b] + ([add] if add is not None else [])))


def _rows_call(body, name, ins, outs, tr=256, acc_outs=()):
    n = ins[0].shape[0]
    tr = min(tr, n)
    in_specs = [pl.BlockSpec((tr, a.shape[1]), lambda i: (i, 0)) if a.shape[0] == n else
                pl.BlockSpec(a.shape, lambda i: (0, 0)) for a in ins]
    out_specs = [pl.BlockSpec((tr, w), lambda i: (i, 0)) for w, _ in outs] + \
                [pl.BlockSpec((1, w), lambda i: (0, 0)) for w in acc_outs]
    out_shape = [jax.ShapeDtypeStruct((n, w), dt) for w, dt in outs] + \
                [jax.ShapeDtypeStruct((1, w), F32) for w in acc_outs]
    res = pl.pallas_call(body, name=name, grid=(n // tr,), in_specs=in_specs, out_specs=out_specs,
                         out_shape=out_shape, compiler_params=_params(("arbitrary",)))(*ins)
    return res[0] if len(res) == 1 else res


def _acc(ref, val):
    @pl.when(pl.program_id(0) == 0)
    def _():
        ref[...] = jnp.zeros_like(ref)

    ref[...] += val


def rms_fwd(x, g, name, res=None):
    def body(*refs):
        x_ref, g_ref = refs[0], refs[1]
        o_ref = refs[-1]
        xv = x_ref[...]
        y = xv * lax.rsqrt(jnp.mean(xv * xv, axis=-1, keepdims=True) + EPS) * g_ref[...]
        o_ref[...] = y + refs[2][...] if res is not None else y

    ins = [x, g] + ([res] if res is not None else [])
    return _rows_call(body, name, ins, [(x.shape[1], F32)])


def rms_bwd(x, g, dy, name, res=None):
    def body(*refs):
        x_ref, g_ref, dy_ref = refs[:3]
        dx_ref, dg_ref = refs[-2], refs[-1]
        xv, dyv = x_ref[...], dy_ref[...]
        r = lax.rsqrt(jnp.mean(xv * xv, axis=-1, keepdims=True) + EPS)
        xh = xv * r
        dxh = dyv * g_ref[...]
        dx = r * (dxh - xh * jnp.mean(dxh * xh, axis=-1, keepdims=True))
        dx_ref[...] = dx + refs[3][...] if res is not None else dx
        _acc(dg_ref, jnp.sum(dyv * xh, axis=0, keepdims=True))

    ins = [x, g, dy] + ([res] if res is not None else [])
    return _rows_call(body, name, ins, [(x.shape[1], F32)], acc_outs=(x.shape[1],))


def _sigmoid(x):
    return 1.0 / (1.0 + jnp.exp(-x))


def swiglu_fwd(gu, name):
    def body(gu_ref, o_ref):
        gate, up = gu_ref[:, :D_FF], gu_ref[:, D_FF:]
        o_ref[...] = gate * _sigmoid(gate) * up

    return _rows_call(body, name, [gu], [(D_FF, F32)])


def swiglu_bwd(gu, dact, name):
    def body(gu_ref, d_ref, o_ref):
        gate, up, d = gu_ref[:, :D_FF], gu_ref[:, D_FF:], d_ref[...]
        sg = _sigmoid(gate)
        o_ref[:, :D_FF] = d * up * sg * (1.0 + gate * (1.0 - sg))
        o_ref[:, D_FF:] = d * gate * sg

    return _rows_call(body, name, [gu, dact], [(2 * D_FF, F32)])


def ple_fwd(h, pp, gt, name):
    def body(h_ref, p_ref, g_ref, o_ref):
        o_ref[...] = h_ref[...] + p_ref[...] * _sigmoid(g_ref[...])

    return _rows_call(body, name, [h, pp, gt], [(D, F32)])


def ple_bwd(dh, pp, gt, name):
    def body(d_ref, p_ref, g_ref, dp_ref, dg_ref):
        d, sg = d_ref[...], _sigmoid(g_ref[...])
        dp_ref[...] = d * sg
        dg_ref[...] = d * p_ref[...] * sg * (1.0 - sg)

    return _rows_call(body, name, [dh, pp, gt], [(D, F32), (D, F32)])


def loss_head(y, target, name):
    def body(y_ref, t_ref, d_ref, l_ref):
        e = y_ref[...] - t_ref[...]
        d_ref[...] = e * (1.0 / D)
        col = jnp.sum(e * e, axis=0, keepdims=True) * (0.5 / D)
        _acc(l_ref, sum(col[:, LANES * c:LANES * (c + 1)] for c in range(D // LANES)))

    return _rows_call(body, name, [y, target], [(D, F32)], acc_outs=(LANES,))


def rope_tables(pos_col, name):
    inv = jnp.asarray(_rope_inv_lanes())

    def body(p_ref, inv_ref, c_ref, s_ref):
        ang = p_ref[...].astype(F32) * inv_ref[...]
        lane = lax.broadcasted_iota(jnp.int32, ang.shape, 1)
        first, second = lane < 16, (lane >= 64) & (lane < 80)
        c_ref[...] = jnp.where(first | second, jnp.cos(ang), 1.0)
        sn = jnp.sin(ang)
        s_ref[...] = jnp.where(first, -sn, jnp.where(second, sn, 0.0))

    return _rows_call(body, name, [pos_col, inv], [(LANES, F32), (LANES, F32)])


def _rope(x, c, s):
    return x * c + pltpu.roll(x, 64, axis=1) * s


def _rope_t(d, c, s):
    return d * c + pltpu.roll(d * s, 64, axis=1)


def mla_qk_fwd(qp, kvp, kr, cos, sin, name):
    def body(q_ref, k_ref, kr_ref, c_ref, s_ref, qo_ref, ko_ref):
        c, s = c_ref[...], s_ref[...]
        kr_rot = _rope(kr_ref[...], c, s)
        for h in range(HEADS):
            sl = slice(LANES * h, LANES * (h + 1))
            qo_ref[:, sl] = _rope(q_ref[:, sl], c, s).astype(BF16)
            ko_ref[:, sl] = (k_ref[:, sl] + kr_rot).astype(BF16)

    n = qp.shape[0]
    tr = 256
    w = HEADS * LANES
    return pl.pallas_call(
        body, name=name, grid=(n // tr,),
        in_specs=[pl.BlockSpec((tr, w), lambda i: (i, 0)), pl.BlockSpec((tr, w), lambda i: (i, 0)),
                  pl.BlockSpec((tr, LANES), lambda i: (i, 0)), pl.BlockSpec((tr, LANES), lambda i: (i, 0)),
                  pl.BlockSpec((tr, LANES), lambda i: (i, 0))],
        out_specs=[pl.BlockSpec((tr, w), lambda i: (i, 0))] * 2,
        out_shape=[jax.ShapeDtypeStruct((n, w), BF16)] * 2, compiler_params=_params(("arbitrary",)),
    )(qp, kvp, kr, cos, sin)


def mla_qk_bwd(dq, dk, cos, sin, name):
    def body(dq_ref, dk_ref, c_ref, s_ref, dqp_ref, dkr_ref):
        c, s = c_ref[...], s_ref[...]
        tot = jnp.zeros(c.shape, F32)
        for h in range(HEADS):
            sl = slice(LANES * h, LANES * (h + 1))
            dqp_ref[:, sl] = _rope_t(dq_ref[:, sl], c, s)
            tot = tot + dk_ref[:, sl]
        dkr_ref[...] = _rope_t(tot, c, s)

    return _rows_call(body, name, [dq, dk, cos, sin], [(HEADS * LANES, F32), (LANES, F32)])


TQ = 256


def _pair_masks(shape):
    lane = lax.broadcasted_iota(jnp.int32, shape, 1)
    return (lane < 64, lane >= 64)


def _causal_probs(q_a, k_a, scale, b0, cq, ck):
    s = lax.dot_general(q_a, k_a, _DIMS["nt"], preferred_element_type=F32) * scale
    if cq is not None:
        s = s + (cq - ck)
    row = lax.broadcasted_iota(jnp.int32, s.shape, 0) + b0
    col = lax.broadcasted_iota(jnp.int32, s.shape, 1)
    s = jnp.where(col <= row, s, NEG)
    e = jnp.exp(s - jnp.max(s, axis=-1, keepdims=True))
    return e / jnp.sum(e, axis=-1, keepdims=True)


def attn_fwd(q, k, v, name, *, wide, scale, q_off=0, k_off=0, v_off=0, cum=None, cum_t=None):
    qw = 2 * LANES if wide else LANES
    forget = cum is not None

    def body(*refs):
        q_ref, k_ref, v_ref = refs[:3]
        o_ref = refs[-1]
        m0, m1 = _pair_masks((TQ, LANES))
        for qi in range(S // TQ):
            b0, b1 = qi * TQ, (qi + 1) * TQ
            outs = []
            for a, msk in enumerate((m0, m1)):
                if wide:
                    q_a, k_a = q_ref[b0:b1, LANES * a:LANES * (a + 1)], k_ref[:b1, LANES * a:LANES * (a + 1)]
                else:
                    q_a, k_a = jnp.where(msk, q_ref[b0:b1, :], jnp.zeros((), BF16)), k_ref[:b1, :]
                cq = refs[3][0, b0:b1, a:a + 1] if forget else None
                ck = refs[4][0, a:a + 1, :b1] if forget else None
                p = _causal_probs(q_a, k_a, scale, b0, cq, ck)
                outs.append(jnp.dot(p.astype(BF16), v_ref[:b1, :], preferred_element_type=F32))
            o_ref[b0:b1, :] = jnp.where(m0, outs[0], outs[1])

    in_specs = [pl.BlockSpec((S, qw), lambda h: (0, q_off * LANES // qw + h)),
                pl.BlockSpec((S, qw), lambda h: (0, k_off * LANES // qw + h)),
                pl.BlockSpec((S, LANES), lambda h: (0, v_off + h))]
    ins = [q, k, v]
    if forget:
        in_specs += [pl.BlockSpec((1, S, 2), lambda h: (h, 0, 0)), pl.BlockSpec((1, 2, S), lambda h: (h, 0, 0))]
        ins += [cum, cum_t]
    return pl.pallas_call(
        body, name=name, grid=(PAIRS,), in_specs=in_specs, out_specs=pl.BlockSpec((S, LANES), lambda h: (0, h)),
        out_shape=jax.ShapeDtypeStruct((S, PAIRS * LANES), F32), compiler_params=_params(("arbitrary",)),
    )(*ins)


def attn_bwd(q, k, v, o, do, name, *, wide, scale, q_off=0, k_off=0, v_off=0, cum=None, cum_t=None):
    qw = 2 * LANES if wide else LANES
    forget = cum is not None

    def body(*refs):
        q_ref, k_ref, v_ref, o_ref, do_ref = refs[:5]
        n_out = 5 if forget else 3
        outs = refs[-(n_out + 2):-2]
        dq_ref, dk_ref, dv_ref = outs[:3]
        dk_acc, dv_acc = refs[-2], refs[-1]
        dk_acc[...] = jnp.zeros_like(dk_acc)
        dv_acc[...] = jnp.zeros_like(dv_acc)
        if forget:
            dcq_ref, dck_ref = outs[3], outs[4]
            dck_ref[...] = jnp.zeros_like(dck_ref)
        m0, m1 = _pair_masks((TQ, LANES))
        for qi in range(S // TQ):
            b0, b1 = qi * TQ, (qi + 1) * TQ
            do2 = do_ref[b0:b1, :]
            dd = do2 * o_ref[b0:b1, :]
            do_b = do2.astype(BF16)
            mk0, mk1 = _pair_masks((b1, LANES))
            dqs = []
            for a, (msk, mk) in enumerate(((m0, mk0), (m1, mk1))):
                lanes = slice(LANES * a, LANES * (a + 1)) if wide else slice(0, LANES)
                if wide:
                    q_a, k_a = q_ref[b0:b1, lanes], k_ref[:b1, lanes]
                else:
                    q_a, k_a = jnp.where(msk, q_ref[b0:b1, :], jnp.zeros((), BF16)), k_ref[:b1, :]
                cq = refs[5][0, b0:b1, a:a + 1] if forget else None
                ck = refs[6][0, a:a + 1, :b1] if forget else None
                p = _causal_probs(q_a, k_a, scale, b0, cq, ck)
                dp = lax.dot_general(jnp.where(msk, do_b, jnp.zeros((), BF16)), v_ref[:b1, :], _DIMS["nt"],
                                     preferred_element_type=F32)
                delta = jnp.sum(jnp.where(msk, dd, 0.0), axis=-1, keepdims=True)
                ds = p * (dp - delta)
                if forget:
                    dcq_ref[0, b0:b1, a:a + 1] = jnp.sum(ds, axis=-1, keepdims=True)
                    dck_ref[0, a:a + 1, :b1] -= jnp.sum(ds, axis=0, keepdims=True)
                ds_b = (ds * scale).astype(BF16)
                dqs.append(jnp.dot(ds_b, k_a, preferred_element_type=F32))
                dk_acc[:b1, lanes] += lax.dot_general(ds_b, q_a, _DIMS["tn"], preferred_element_type=F32)
                dv_acc[:b1, :] += jnp.where(mk, lax.dot_general(p.astype(BF16), do_b, _DIMS["tn"],
                                                                 preferred_element_type=F32), 0.0)
            if wide:
                dq_ref[b0:b1, :LANES] = dqs[0]
                dq_ref[b0:b1, LANES:] = dqs[1]
            else:
                dq_ref[b0:b1, :] = jnp.where(m0, dqs[0], dqs[1])
        dk_ref[...] = dk_acc[...]
        dv_ref[...] = dv_acc[...]

    pair = pl.BlockSpec((S, LANES), lambda h: (0, h))
    qk_out = pl.BlockSpec((S, qw), lambda h: (0, h))
    in_specs = [pl.BlockSpec((S, qw), lambda h: (0, q_off * LANES // qw + h)),
                pl.BlockSpec((S, qw), lambda h: (0, k_off * LANES // qw + h)),
                pl.BlockSpec((S, LANES), lambda h: (0, v_off + h)), pair, pair]
    ins = [q, k, v, o, do]
    out_specs = [qk_out, qk_out, pair]
    out_shape = [jax.ShapeDtypeStruct((S, PAIRS * qw), F32)] * 2 + [jax.ShapeDtypeStruct((S, PAIRS * LANES), F32)]
    if forget:
        by_q, by_k = pl.BlockSpec((1, S, 2), lambda h: (h, 0, 0)), pl.BlockSpec((1, 2, S), lambda h: (h, 0, 0))
        in_specs += [by_q, by_k]
        ins += [cum, cum_t]
        out_specs += [by_q, by_k]
        out_shape += [jax.ShapeDtypeStruct((PAIRS, S, 2), F32), jax.ShapeDtypeStruct((PAIRS, 2, S), F32)]
    return pl.pallas_call(
        body, name=name, grid=(PAIRS,), in_specs=in_specs, out_specs=out_specs, out_shape=out_shape,
        scratch_shapes=[pltpu.VMEM((S, qw), F32), pltpu.VMEM((S, LANES), F32)],
        compiler_params=_params(("arbitrary",)),
    )(*ins)


def _tri(lower):
    r = lax.broadcasted_iota(jnp.int32, (QBLK, QBLK), 0)
    c = lax.broadcasted_iota(jnp.int32, (QBLK, QBLK), 1)
    return jnp.where((c <= r) if lower else (c >= r), 1.0, 0.0).astype(F32)


def _hi_dot(a, b):
    return jnp.dot(a, b, precision=lax.Precision.HIGHEST, preferred_element_type=F32)


def fox_gate_fwd(fl, bias, name):
    def body(f_ref, b_ref, o_ref):
        tri = _tri(True)
        carry = jnp.zeros((1, LANES), F32)
        for n in range(S // QBLK):
            x = f_ref[n * QBLK:(n + 1) * QBLK, :].astype(F32) + b_ref[...]
            lf = jnp.minimum(x, 0.0) - jnp.log(1.0 + jnp.exp(-jnp.abs(x)))
            c = _hi_dot(tri, lf) + carry
            o_ref[n * QBLK:(n + 1) * QBLK, :] = c
            carry = c[QBLK - 1:QBLK, :]

    return pl.pallas_call(body, name=name, out_shape=jax.ShapeDtypeStruct((S, LANES), F32),
                          compiler_params=_params())(fl, bias)


def fox_gate_bwd(fl, bias, dcq, dck, name):
    def body(f_ref, b_ref, dq_ref, dk_ref, o_ref, db_ref):
        tri = _tri(False)
        carry = jnp.zeros((1, LANES), F32)
        db = jnp.zeros((1, LANES), F32)
        for n in reversed(range(S // QBLK)):
            rows = slice(n * QBLK, (n + 1) * QBLK)
            dlf = _hi_dot(tri, dq_ref[rows, :] + dk_ref[rows, :]) + carry
            carry = dlf[0:1, :]
            x = f_ref[rows, :].astype(F32) + b_ref[...]
            dx = dlf * (1.0 - _sigmoid(x))
            o_ref[rows, :] = dx
            db = db + jnp.sum(dx, axis=0, keepdims=True)
        db_ref[...] = db

    return pl.pallas_call(body, name=name, out_shape=[jax.ShapeDtypeStruct((S, LANES), F32),
                                                      jax.ShapeDtypeStruct((1, LANES), F32)],
                          compiler_params=_params())(fl, bias, dcq, dck)


def _band_valid(first):
    w = QBLK if first else 2 * QBLK
    i = lax.broadcasted_iota(jnp.int32, (QBLK, w), 0)
    j = lax.broadcasted_iota(jnp.int32, (QBLK, w), 1)
    return (j <= i) if first else ((j >= i) & (j - QBLK <= i))


def _band_logits(q_a, kk, bias, first):
    s = lax.dot_general(q_a, kk, _DIMS["nt"], preferred_element_type=F32) * 0.125 + bias
    return jnp.where(_band_valid(first), s, NEG)


def dil_fwd(qkv, bias, g, name):
    d = DIL[g][1]
    ls = S // d
    view = qkv.reshape(ls, d * 9216)

    def body(q_ref, k_ref, v_ref, b_ref, o_ref, l_ref):
        m0, m1 = _pair_masks((QBLK, LANES))
        for n in range(ls // QBLK):
            rows = slice(n * QBLK, (n + 1) * QBLK)
            keys = rows if n == 0 else slice((n - 1) * QBLK, (n + 1) * QBLK)
            os_, ls_ = [], []
            for a, msk in enumerate((m0, m1)):
                q_a = jnp.where(msk, q_ref[rows, :], jnp.zeros((), BF16))
                bias_a = b_ref[a, :, QBLK:] if n == 0 else b_ref[a]
                s = _band_logits(q_a, k_ref[keys, :], bias_a, n == 0)
                mx = jnp.max(s, axis=-1, keepdims=True)
                e = jnp.exp(s - mx)
                l = jnp.sum(e, axis=-1, keepdims=True)
                os_.append(jnp.dot((e / l).astype(BF16), v_ref[keys, :], preferred_element_type=F32))
                ls_.append(mx + jnp.log(l))
            o_ref[rows, :] = jnp.where(m0, os_[0], os_[1])
            l_ref[rows, :] = jnp.where(m0, ls_[0], ls_[1])

    def col(j):
        return lambda h, r: (0, r * 72 + g * 24 + j * 8 + h)

    out = pl.BlockSpec((ls, LANES), lambda h, r: (0, r * 8 + h))
    o, lse = pl.pallas_call(
        body, name=name, grid=(PAIRS, d),
        in_specs=[pl.BlockSpec((ls, LANES), col(0)), pl.BlockSpec((ls, LANES), col(1)), pl.BlockSpec((ls, LANES), col(2)),
                  pl.BlockSpec((2, QBLK, 2 * QBLK), lambda h, r: (h, 0, 0))],
        out_specs=[out, out], out_shape=[jax.ShapeDtypeStruct((ls, d * D), F32)] * 2,
        compiler_params=_params(("arbitrary", "arbitrary")),
    )(view, view, view, bias)
    return o.reshape(S, D), lse.reshape(S, D)


def dil_merge(os_, lses, name):
    def body(o0, o1, o2, l0, l1, l2, o_ref, l_ref):
        ls_ = [l0[...], l1[...], l2[...]]
        mx = jnp.maximum(jnp.maximum(ls_[0], ls_[1]), ls_[2])
        tot = mx + jnp.log(sum(jnp.exp(l - mx) for l in ls_))
        o_ref[...] = sum(jnp.exp(l - tot) * o[...] for l, o in zip(ls_, (o0, o1, o2)))
        l_ref[...] = tot

    return _rows_call(body, name, list(os_) + list(lses), [(D, F32), (D, F32)])


def dil_bwd(qkv, bias, o, lse, do, g, name):
    d = DIL[g][1]
    ls = S // d
    view = qkv.reshape(ls, d * 9216)
    o, lse, do = (t.reshape(ls, d * D) for t in (o, lse, do))

    def body(q_ref, k_ref, v_ref, b_ref, o_ref, l_ref, do_ref, dq_ref, dk_ref, dv_ref, db_ref, dk_acc, dv_acc):
        @pl.when(pl.program_id(1) == 0)
        def _():
            db_ref[...] = jnp.zeros_like(db_ref)

        dk_acc[...] = jnp.zeros_like(dk_acc)
        dv_acc[...] = jnp.zeros_like(dv_acc)
        m0, m1 = _pair_masks((QBLK, LANES))
        for n in range(ls // QBLK):
            rows = slice(n * QBLK, (n + 1) * QBLK)
            keys = rows if n == 0 else slice((n - 1) * QBLK, (n + 1) * QBLK)
            nk = QBLK if n == 0 else 2 * QBLK
            do2, lse2 = do_ref[rows, :], l_ref[rows, :]
            dd = do2 * o_ref[rows, :]
            do_b = do2.astype(BF16)
            mk0, mk1 = _pair_masks((nk, LANES))
            dqs = []
            for a, (msk, mk) in enumerate(((m0, mk0), (m1, mk1))):
                q_a = jnp.where(msk, q_ref[rows, :], jnp.zeros((), BF16))
                kk = k_ref[keys, :]
                bias_a = b_ref[a, :, QBLK:] if n == 0 else b_ref[a]
                s = _band_logits(q_a, kk, bias_a, n == 0)
                lse_a = jnp.max(jnp.where(msk, lse2, -jnp.inf), axis=-1, keepdims=True)
                p = jnp.exp(s - lse_a)
                dp = lax.dot_general(jnp.where(msk, do_b, jnp.zeros((), BF16)), v_ref[keys, :], _DIMS["nt"],
                                     preferred_element_type=F32)
                delta = jnp.sum(jnp.where(msk, dd, 0.0), axis=-1, keepdims=True)
                ds = p * (dp - delta)
                if n == 0:
                    db_ref[a, :, QBLK:] += ds
                else:
                    db_ref[a] += ds
                ds_b = (ds * 0.125).astype(BF16)
                dqs.append(jnp.dot(ds_b, kk, preferred_element_type=F32))
                dk_acc[keys, :] += lax.dot_general(ds_b, q_a, _DIMS["tn"], preferred_element_type=F32)
                dv_acc[keys, :] += jnp.where(mk, lax.dot_general(p.astype(BF16), do_b, _DIMS["tn"],
                                                                 preferred_element_type=F32), 0.0)
            dq_ref[rows, :] = jnp.where(m0, dqs[0], dqs[1])
        dk_ref[...] = dk_acc[...]
        dv_ref[...] = dv_acc[...]

    def col(j):
        return lambda h, r: (0, r * 72 + g * 24 + j * 8 + h)

    nat = pl.BlockSpec((ls, LANES), lambda h, r: (0, r * 8 + h))
    b_spec = pl.BlockSpec((2, QBLK, 2 * QBLK), lambda h, r: (h, 0, 0))
    dq, dk, dv, db = pl.pallas_call(
        body, name=name, grid=(PAIRS, d),
        in_specs=[pl.BlockSpec((ls, LANES), col(0)), pl.BlockSpec((ls, LANES), col(1)), pl.BlockSpec((ls, LANES), col(2)),
                  b_spec, nat, nat, nat],
        out_specs=[nat, nat, nat, b_spec],
        out_shape=[jax.ShapeDtypeStruct((ls, d * D), F32)] * 3 + [jax.ShapeDtypeStruct((HEADS, QBLK, 2 * QBLK), F32)],
        scratch_shapes=[pltpu.VMEM((ls, LANES), F32), pltpu.VMEM((ls, LANES), F32)],
        compiler_params=_params(("arbitrary", "arbitrary")),
    )(view, view, view, bias, o, lse, do)
    return dq.reshape(S, D), dk.reshape(S, D), dv.reshape(S, D), db


def _place():
    x, y, c = lax.axis_index("x"), lax.axis_index("y"), lax.axis_index("c")
    return x, y, c


def all_gather(block, name, in_vmem):
    r = block.shape[0]

    def body(x_ref, out_ref, send_sems, recv_sems, local_sem):
        x, y, c = _place()
        me, sibling = (x, y, c), (x, y, 1 - c)
        chips = [(1 - x, y), (x, 1 - y), (1 - x, 1 - y)]

        def slot(px, py, pc):
            return out_ref.at[4 * px + 2 * py + pc]

        def copy(k, blk, to, src=None):
            return pltpu.make_async_remote_copy(
                src_ref=slot(*blk) if src is None else src, dst_ref=slot(*blk), send_sem=send_sems.at[k],
                recv_sem=recv_sems.at[k], device_id=to, device_id_type=MESH_ID)

        mine = pltpu.make_async_copy(x_ref, slot(*me), local_sem)
        mine.start()
        first = [copy(0, me, sibling, src=x_ref)]
        first += [copy(1 + j, me, (*chip, c), src=x_ref) for j, chip in enumerate(chips)]
        for cp in first:
            cp.start()
        passed = [copy(4 + j, (*chip, c), sibling) for j, chip in enumerate(chips)]
        for j, chip in enumerate(chips):
            copy(1 + j, (*chip, c), me).wait_recv()
            passed[j].start()
        copy(0, sibling, me).wait_recv()
        for j, chip in enumerate(chips):
            copy(4 + j, (*chip, 1 - c), me).wait_recv()
        for cp in first + passed:
            cp.wait_send()
        mine.wait()

    space = pltpu.VMEM if in_vmem else pl.ANY
    return pl.pallas_call(
        body, name=name, out_shape=jax.ShapeDtypeStruct((N_DEV, r, LANES), block.dtype),
        in_specs=[pl.BlockSpec(memory_space=space)], out_specs=pl.BlockSpec(memory_space=space),
        scratch_shapes=[pltpu.SemaphoreType.DMA((7,)), pltpu.SemaphoreType.DMA((7,)), pltpu.SemaphoreType.DMA],
        compiler_params=pltpu.CompilerParams(vmem_limit_bytes=VMEM_LIMIT),
    )(block)


def exchange(chunks, name):
    def body(x_ref, out_ref, send_sems, recv_sems, local_sem):
        x, y, c = _place()
        me = 4 * x + 2 * y + c
        mine = pltpu.make_async_copy(x_ref.at[me], out_ref.at[me], local_sem)
        mine.start()
        copies = []
        for k in range(1, N_DEV):
            px = 1 - x if k & 4 else x
            py = 1 - y if k & 2 else y
            pc = 1 - c if k & 1 else c
            peer = 4 * px + 2 * py + pc
            copies.append(pltpu.make_async_remote_copy(
                src_ref=x_ref.at[peer], dst_ref=out_ref.at[me], send_sem=send_sems.at[k - 1],
                recv_sem=recv_sems.at[k - 1], device_id=(px, py, pc), device_id_type=MESH_ID))
        for cp in copies:
            cp.start()
        for cp in copies:
            cp.wait_recv()
        for cp in copies:
            cp.wait_send()
        mine.wait()

    return pl.pallas_call(
        body, name=name, out_shape=jax.ShapeDtypeStruct(chunks.shape, chunks.dtype),
        in_specs=[pl.BlockSpec(memory_space=pl.ANY)], out_specs=pl.BlockSpec(memory_space=pl.ANY),
        scratch_shapes=[pltpu.SemaphoreType.DMA((7,)), pltpu.SemaphoreType.DMA((7,)), pltpu.SemaphoreType.DMA],
        compiler_params=pltpu.CompilerParams(vmem_limit_bytes=VMEM_LIMIT),
    )(chunks)


def adamw(w, m, v, parts, name):
    n_parts, r = parts.shape[0], w.shape[0]
    tr = min(r, ADAM_ROWS)
    assert r % tr == 0
    c1 = 1.0 / (1.0 - ADAM_B1 ** ADAM_STEP)
    c2 = 1.0 / (1.0 - ADAM_B2 ** ADAM_STEP)

    def body(w_ref, m_ref, v_ref, p_ref, g_ref, d_ref, nm_ref, nv_ref):
        g = p_ref[0].astype(F32)
        for j in range(1, n_parts):
            g = g + p_ref[j].astype(F32)
        nm = ADAM_B1 * m_ref[...] + (1.0 - ADAM_B1) * g
        nv = ADAM_B2 * v_ref[...] + (1.0 - ADAM_B2) * (g * g)
        g_ref[...] = g
        nm_ref[...] = nm
        nv_ref[...] = nv
        d_ref[...] = -ADAM_LR * ((nm * c1) / (jnp.sqrt(nv * c2) + ADAM_EPS) + ADAM_WD * w_ref[...])

    blk = pl.BlockSpec((tr, LANES), lambda i: (i, 0))
    return pl.pallas_call(
        body, name=name, grid=(r // tr,),
        in_specs=[blk, blk, blk, pl.BlockSpec((n_parts, tr, LANES), lambda i: (0, i, 0))],
        out_specs=[blk] * 4, out_shape=[jax.ShapeDtypeStruct((r, LANES), F32)] * 4,
        compiler_params=_params(("parallel",)),
    )(w, m, v, parts)


def sum_parts(parts, name):
    def body(p_ref, o_ref):
        g = p_ref[0]
        for j in range(1, parts.shape[0]):
            g = g + p_ref[j]
        o_ref[...] = g

    return pl.pallas_call(body, name=name, out_shape=jax.ShapeDtypeStruct(parts.shape[1:], F32),
                          compiler_params=_params())(parts)


def _pack(arrs, rows, dtype):
    flat = jnp.concatenate([a.reshape(-1).astype(dtype) for a in arrs])
    return jnp.pad(flat, (0, rows * LANES - flat.shape[0])).reshape(rows, LANES)


def _unpack(packed, shapes):
    flat, out, off = packed.reshape(-1), [], 0
    for shp in shapes:
        n = int(np.prod(shp))
        out.append(flat[off:off + n].reshape(shp))
        off += n
    return out


def _full_from_gathered(g, shard_shape, axis):
    g = jnp.moveaxis(g, 0, axis)
    shp = list(shard_shape)
    shp[axis] *= N_DEV
    return g.reshape(shp)


def _shards_from_full(full, shard_shape, axis):
    shp = list(shard_shape)
    shp[axis:axis + 1] = [N_DEV, shard_shape[axis]]
    return jnp.moveaxis(full.reshape(shp), axis, 0)


def _mixer_fwd(kind, tag, hn, W, aux):
    if kind == 0:
        a = mm(hn, W["w_a"], "nn", f"{tag}_a", tn=768)
        cq = rms_fwd(a[:, :Q_RANK], W["q_norm"], f"{tag}_cq")
        ckv = rms_fwd(a[:, Q_RANK:Q_RANK + KV_RANK], W["kv_norm"], f"{tag}_ckv")
        qp = mm(cq, W["w_uq"], "nn", f"{tag}_uq", tk=384)
        kvp = mm(ckv, W["w_ukv"], "nn", f"{tag}_ukv", tk=256)
        q, k = mla_qk_fwd(qp, kvp, a[:, 640:], aux["cos"], aux["sin"], f"{tag}_qk")
        v = kvp.astype(BF16)
        o = attn_fwd(q, k, v, f"{tag}_attn", wide=True, scale=96 ** -0.5, v_off=HEADS)
        y = mm(o, W["w_o"], "nn", f"{tag}_o")
        return y, (a, cq, ckv, q, k, v, o)
    if kind == 1:
        qkv = mm(hn, W["w_qkv"], "nn", f"{tag}_qkv", out_dtype=BF16)
        parts = [dil_fwd(qkv, aux["dil_bias"][g], g, f"{tag}_g{g}") for g in range(3)]
        o, lse = dil_merge([p_[0] for p_ in parts], [p_[1] for p_ in parts], f"{tag}_merge")
        y = mm(o, W["w_o"], "nn", f"{tag}_o")
        return y, (qkv, o, lse)
    a = mm(hn, W["w_qkvf"], "nn", f"{tag}_qkvf", tn=640)
    fl = a[:, 3072:]
    cum = fox_gate_fwd(fl, aux["fox_b"], f"{tag}_gate")[:, :HEADS]
    cum_q = cum.reshape(S, PAIRS, 2).transpose(1, 0, 2)
    cum_k = cum.T.reshape(PAIRS, 2, S)
    ab = a.astype(BF16)
    o = attn_fwd(ab, ab, ab, f"{tag}_attn", wide=False, scale=0.125, k_off=PAIRS, v_off=2 * PAIRS, cum=cum_q, cum_t=cum_k)
    y = mm(o, W["w_o"], "nn", f"{tag}_o")
    return y, (fl, ab, cum_q, cum_k, o)


def _mixer_bwd(kind, tag, hn, dy, W, aux, saved):
    gr = {}
    if kind == 0:
        a, cq, ckv, q, k, v, o = saved
        gr["w_o"] = mm(o, dy, "tn", f"{tag}_dwo", out_dtype=BF16)
        do = mm(dy, W["w_o"], "nt", f"{tag}_do")
        dq, dk, dv = attn_bwd(q, k, v, o, do, f"{tag}_attn_b", wide=True, scale=96 ** -0.5, v_off=HEADS)
        dqp, dkr = mla_qk_bwd(dq, dk, aux["cos"], aux["sin"], f"{tag}_qk_b")
        dkvp = jnp.concatenate([dk, dv], axis=1)
        gr["w_ukv"] = mm(ckv, dkvp, "tn", f"{tag}_dwukv", out_dtype=BF16, tm=256)
        dckv = mm(dkvp, W["w_ukv"], "nt", f"{tag}_dckv", tn=256)
        gr["w_uq"] = mm(cq, dqp, "tn", f"{tag}_dwuq", out_dtype=BF16, tm=384)
        dcq = mm(dqp, W["w_uq"], "nt", f"{tag}_dcq", tn=384)
        da_q, gr["q_norm"] = rms_bwd(a[:, :Q_RANK], W["q_norm"], dcq, f"{tag}_cq_b")
        da_kv, gr["kv_norm"] = rms_bwd(a[:, Q_RANK:Q_RANK + KV_RANK], W["kv_norm"], dckv, f"{tag}_ckv_b")
        da = jnp.concatenate([da_q, da_kv, dkr], axis=1)
        gr["w_a"] = mm(hn, da, "tn", f"{tag}_dwa", out_dtype=BF16, tn=768)
        return mm(da, W["w_a"], "nt", f"{tag}_dhn", tk=768), gr
    if kind == 1:
        qkv, o, lse = saved
        gr["w_o"] = mm(o, dy, "tn", f"{tag}_dwo", out_dtype=BF16)
        do = mm(dy, W["w_o"], "nt", f"{tag}_do")
        cols, dbs = [], []
        for g in range(3):
            dq, dk, dv, db = dil_bwd(qkv, aux["dil_bias"][g], o, lse, do, g, f"{tag}_g{g}_b")
            cols += [dq, dk, dv]
            dbs.append(db)
        dqkv = jnp.concatenate(cols, axis=1)
        gr["dil_dbias"] = dbs
        gr["w_qkv"] = mm(hn, dqkv, "tn", f"{tag}_dwqkv", out_dtype=BF16)
        return mm(dqkv, W["w_qkv"], "nt", f"{tag}_dhn"), gr
    fl, ab, cum_q, cum_k, o = saved
    gr["w_o"] = mm(o, dy, "tn", f"{tag}_dwo", out_dtype=BF16)
    do = mm(dy, W["w_o"], "nt", f"{tag}_do")
    dq, dk, dv, dcq, dck = attn_bwd(ab, ab, ab, o, do, f"{tag}_attn_b", wide=False, scale=0.125, k_off=PAIRS,
                                    v_off=2 * PAIRS, cum=cum_q, cum_t=cum_k)
    pad = ((0, 0), (0, LANES - HEADS))
    dcq = jnp.pad(dcq.transpose(1, 0, 2).reshape(S, HEADS), pad)
    dck = jnp.pad(dck.reshape(HEADS, S).T, pad)
    dfl, gr["b_f"] = fox_gate_bwd(fl, aux["fox_b"], dcq, dck, f"{tag}_gate_b")
    da = jnp.concatenate([dq, dk, dv, dfl], axis=1)
    gr["w_qkvf"] = mm(hn, da, "tn", f"{tag}_dwqkvf", out_dtype=BF16, tn=640)
    return mm(da, W["w_qkvf"], "nt", f"{tag}_dhn", tk=640), gr


def kernel(x, p, positions, norm_g, ffn_w_in, ffn_w_out, ple_w_proj, ple_w_gate, rel_bias, mla_w_a, mla_q_norm, mla_kv_norm, mla_w_uq, mla_w_ukv, mla_w_o, dil_w_qkv, dil_w_o, fox_w_qkvf, fox_b_f, fox_w_o, loss_target, m_norm_g, m_ffn_w_in, m_ffn_w_out, m_ple_w_proj, m_ple_w_gate, m_rel_bias, m_mla_w_a, m_mla_q_norm, m_mla_kv_norm, m_mla_w_uq, m_mla_w_ukv, m_mla_w_o, m_dil_w_qkv, m_dil_w_o, m_fox_w_qkvf, m_fox_b_f, m_fox_w_o, v_norm_g, v_ffn_w_in, v_ffn_w_out, v_ple_w_proj, v_ple_w_gate, v_rel_bias, v_mla_w_a, v_mla_q_norm, v_mla_kv_norm, v_mla_w_uq, v_mla_w_ukv, v_mla_w_o, v_dil_w_qkv, v_dil_w_o, v_fox_w_qkvf, v_fox_b_f, v_fox_w_o):
    given = dict(locals())
    me = 4 * lax.axis_index("x") + 2 * lax.axis_index("y") + lax.axis_index("c")

    w_pack = _pack([given[n] for n, _, _ in BIG], BIG_ROWS, F32)
    gathered = all_gather(w_pack.astype(BF16), "gather_weights", in_vmem=False).reshape(N_DEV, BIG_ROWS * LANES)
    full, off = {}, 0
    for n, shp, axis in BIG:
        cnt = int(np.prod(shp))
        full[n] = _full_from_gathered(gathered[:, off:off + cnt].reshape((N_DEV,) + shp), shp, axis)
        off += cnt
    for n, idx in COLS.items():
        full[n] = _spread_cols(full[n], idx)

    gain_rows = _rows(sum(int(np.prod(s)) for _, s, _ in SMALL_SHARDED))
    gains = all_gather(_pack([given[n] for n, _, _ in SMALL_SHARDED], gain_rows, F32), "gather_gains", in_vmem=True)
    gains = gains.reshape(N_DEV, gain_rows * LANES)
    off = 0
    for n, shp, axis in SMALL_SHARDED:
        cnt = int(np.prod(shp))
        full[n] = _full_from_gathered(gains[:, off:off + cnt].reshape((N_DEV,) + shp), shp, axis)
        off += cnt

    cos, sin = rope_tables(positions.reshape(S, 1), "rope_tables")
    dil_bias = [mm(rel_bias[:, HEADS * g:HEADS * (g + 1)], jnp.asarray(_bucket_onehot(DIL[g][1])), "tn",
                   f"dil_bias{g}", precise=True, tn=4096).reshape(HEADS, QBLK, 2 * QBLK) for g in range(3)]
    aux = {"cos": cos, "sin": sin, "dil_bias": dil_bias,
           "fox_b": jnp.pad(fox_b_f, ((0, 0), (0, LANES - HEADS)))}

    def layer_weights(i):
        kind, j = i % 3, i // 3
        W = {"g": [full["norm_g"][i, r][None, :] for r in range(4)], "w_in": full["ffn_w_in"][i],
             "w_out": full["ffn_w_out"][i], "w_proj": full["ple_w_proj"][i], "w_gate": full["ple_w_gate"][i]}
        if kind == 0:
            W.update(w_a=full["mla_w_a"][j], w_uq=full["mla_w_uq"][j], w_ukv=full["mla_w_ukv"][j],
                     w_o=full["mla_w_o"][j], q_norm=full["mla_q_norm"][j][None, :],
                     kv_norm=full["mla_kv_norm"][j][None, :])
        elif kind == 1:
            W.update(w_qkv=full["dil_w_qkv"][j], w_o=full["dil_w_o"][j])
        else:
            W.update(w_qkvf=full["fox_w_qkvf"][j], w_o=full["fox_w_o"][j])
        return kind, j, W

    h = x[0]
    saved = []
    for i in range(DEPTH):
        kind, j, W = layer_weights(i)
        t = f"l{i}"
        hn = rms_fwd(h, W["g"][0], f"{t}_n0")
        y, mix = _mixer_fwd(kind, f"{t}_mix", hn, W, aux)
        h1 = rms_fwd(y, W["g"][1], f"{t}_n1", res=h)
        fin = rms_fwd(h1, W["g"][2], f"{t}_n2")
        gu = mm(fin, W["w_in"], "nn", f"{t}_ffn_in")
        act = swiglu_fwd(gu, f"{t}_swiglu")
        f = mm(act, W["w_out"], "nn", f"{t}_ffn_out", tk=256)
        h2 = rms_fwd(f, W["g"][3], f"{t}_n3", res=h1)
        pp = mm(p[i, 0], W["w_proj"], "nn", f"{t}_ple_p", tk=256)
        gt = mm(h2, W["w_gate"], "nn", f"{t}_ple_g")
        h3 = ple_fwd(h2, pp, gt, f"{t}_ple")
        saved.append((h, hn, y, h1, fin, gu, act, f, h2, pp, gt, mix))
        h = h3

    dh, loss_lanes = loss_head(h, loss_target[0], "loss_head")

    grads = {n: [None] * s[0] for n, s, _ in BIG}
    g_norm = [[None] * 4 for _ in range(DEPTH)]
    g_qn, g_kvn = [None, None], [None, None]
    g_rel, g_bf = None, None
    for i in reversed(range(DEPTH)):
        kind, j, W = layer_weights(i)
        t = f"l{i}b"
        h0, hn, y, h1, fin, gu, act, f, h2, pp, gt, mix = saved[i]
        dpp, dgt = ple_bwd(dh, pp, gt, f"{t}_ple")
        grads["ple_w_proj"][i] = mm(p[i, 0], dpp, "tn", f"{t}_dwp", out_dtype=BF16, tm=256)
        grads["ple_w_gate"][i] = mm(h2, dgt, "tn", f"{t}_dwg", out_dtype=BF16)
        dh2 = mm(dgt, W["w_gate"], "nt", f"{t}_dh2", add=dh)
        df, g_norm[i][3] = rms_bwd(f, W["g"][3], dh2, f"{t}_n3")
        grads["ffn_w_out"][i] = mm(act, df, "tn", f"{t}_dwout", out_dtype=BF16, tm=256)
        dact = mm(df, W["w_out"], "nt", f"{t}_dact", tn=256)
        dgu = swiglu_bwd(gu, dact, f"{t}_swiglu")
        grads["ffn_w_in"][i] = mm(fin, dgu, "tn", f"{t}_dwin", out_dtype=BF16)
        dfin = mm(dgu, W["w_in"], "nt", f"{t}_dfin")
        dh1, g_norm[i][2] = rms_bwd(h1, W["g"][2], dfin, f"{t}_n2", res=dh2)
        dy, g_norm[i][1] = rms_bwd(y, W["g"][1], dh1, f"{t}_n1")
        dhn, gr = _mixer_bwd(kind, f"{t}_mix", hn, dy, W, aux, mix)
        dh, g_norm[i][0] = rms_bwd(h0, W["g"][0], dhn, f"{t}_n0", res=dh1)
        if kind == 0:
            grads["mla_w_a"][j], grads["mla_w_uq"][j] = gr["w_a"], gr["w_uq"]
            grads["mla_w_ukv"][j], grads["mla_w_o"][j] = gr["w_ukv"], gr["w_o"]
            g_qn[j], g_kvn[j] = gr["q_norm"], gr["kv_norm"]
        elif kind == 1:
            grads["dil_w_qkv"][j], grads["dil_w_o"][j] = gr["w_qkv"], gr["w_o"]
            g_rel = jnp.concatenate(
                [mm(jnp.asarray(_bucket_onehot(DIL[g][1])), gr["dil_dbias"][g].reshape(HEADS, -1), "nt",
                    f"{t}_drel{g}", precise=True, tk=4096) for g in range(3)], axis=1)
        else:
            grads["fox_w_qkvf"][j], grads["fox_w_o"][j] = gr["w_qkvf"], gr["w_o"]
            g_bf = gr["b_f"][:, :HEADS]
    grad_x = dh[None]

    chunks = []
    for n, shp, axis in BIG:
        gfull = jnp.stack(grads[n])
        if n in COLS_INV:
            gfull = jnp.take(gfull, jnp.asarray(COLS_INV[n]), axis=-1)
        chunks.append(_shards_from_full(gfull, shp, axis).reshape(N_DEV, -1))
    chunks = jnp.concatenate(chunks, axis=1)
    chunks = jnp.pad(chunks, ((0, 0), (0, BIG_ROWS * LANES - BIG_N))).reshape(N_DEV, BIG_ROWS, LANES)
    landed = exchange(chunks, "scatter_grads")

    m_pack = _pack([given["m_" + n] for n, _, _ in BIG], BIG_ROWS, F32)
    v_pack = _pack([given["v_" + n] for n, _, _ in BIG], BIG_ROWS, F32)
    big_out = adamw(w_pack, m_pack, v_pack, landed, "adamw")
    big_out = [_unpack(o_, [s for _, s, _ in BIG]) for o_ in big_out]

    small_full = [jnp.stack([jnp.concatenate(r, axis=0) for r in g_norm]).reshape(-1),
                  jnp.concatenate(g_qn, axis=0).reshape(-1), jnp.concatenate(g_kvn, axis=0).reshape(-1),
                  g_rel.reshape(-1), g_bf.reshape(-1), loss_lanes.reshape(-1)]
    small_n = sum(a.shape[0] for a in small_full)
    small_rows = _rows(small_n)
    parts = all_gather(_pack(small_full, small_rows, F32), "gather_small_grads", in_vmem=True)
    tot = _unpack(sum_parts(parts, "sum_small_grads"), [(4, 4, D), (2, Q_RANK), (2, KV_RANK), (32, 48), (1, 16), (LANES,)])
    loss = jnp.sum(tot[5])
    small_g = [lax.dynamic_slice_in_dim(tot[0], me * 128, 128, axis=2), lax.dynamic_slice_in_dim(tot[1], me * 48, 48, axis=1),
               lax.dynamic_slice_in_dim(tot[2], me * 32, 32, axis=1), tot[3], tot[4]]
    small_names = [n for n, _, _ in SMALL_SHARDED] + [n for n, _ in SMALL_REPL]
    small_shapes = [s for _, s, _ in SMALL_SHARDED] + [s for _, s in SMALL_REPL]
    s_rows = _rows(sum(int(np.prod(s)) for s in small_shapes))
    small_out = adamw(_pack([given[n] for n in small_names], s_rows, F32),
                      _pack([given["m_" + n] for n in small_names], s_rows, F32),
                      _pack([given["v_" + n] for n in small_names], s_rows, F32),
                      _pack(small_g, s_rows, F32)[None], "adamw_small")
    small_out = [_unpack(o_, small_shapes) for o_ in small_out]

    res = [{}, {}, {}, {}]
    for k in range(4):
        for idx, (n, _, _) in enumerate(BIG):
            res[k][n] = big_out[k][idx]
        for idx, n in enumerate(small_names):
            res[k][n] = small_out[k][idx]
    return (loss, grad_x, *[res[0][n] for n in WEIGHTS], *[res[1][n] for n in WEIGHTS],
            *[res[2][n] for n in WEIGHTS], *[res[3][n] for n in WEIGHTS])
```

```python
import math
from typing import NamedTuple

import numpy as np
import jax
import jax.numpy as jnp
from jax import lax
from jax.experimental import pallas as pl
from jax.experimental.pallas import tpu as pltpu

F32 = jnp.float32
BF16 = jnp.bfloat16
MESH_ID = pl.DeviceIdType.MESH

N_DEV = 8
S = 2048
D = 1024
DEPTH = 4
D_FF = 2816
D_PLE = 256
EPS = 1e-6
NEG = -1e30
LANES = 128
HEADS = 16
PAIRS = 8
Q_RANK = 384
KV_RANK = 256
QBLK = 128
DIL = ((128, 1), (512, 4), (2048, 16))
REL_BUCKETS = 32
FOX_W = 3200
VMEM_LIMIT = 56 * 1024 * 1024

ADAM_LR, ADAM_B1, ADAM_B2, ADAM_EPS, ADAM_WD, ADAM_STEP = 1e-3, 0.9, 0.999, 1e-8, 0.01, 10


BIG = (
    ("ffn_w_in", (4, 1024, 704), 2), ("ffn_w_out", (4, 352, 1024), 1),
    ("ple_w_proj", (4, 256, 128), 2), ("ple_w_gate", (4, 128, 1024), 1),
    ("mla_w_a", (2, 128, 672), 1), ("mla_w_uq", (2, 384, 192), 2),
    ("mla_w_ukv", (2, 256, 256), 2), ("mla_w_o", (2, 128, 1024), 1),
    ("dil_w_qkv", (1, 1024, 1152), 2), ("dil_w_o", (1, 128, 1024), 1),
    ("fox_w_qkvf", (1, 1024, 386), 2), ("fox_w_o", (1, 128, 1024), 1),
)
SMALL_SHARDED = (("norm_g", (4, 4, 128), 2), ("mla_q_norm", (2, 48), 1), ("mla_kv_norm", (2, 32), 1))
SMALL_REPL = (("rel_bias", (32, 48)), ("fox_b_f", (1, 16)))
WEIGHTS = ("norm_g", "ffn_w_in", "ffn_w_out", "ple_w_proj", "ple_w_gate", "rel_bias", "mla_w_a", "mla_q_norm",
           "mla_kv_norm", "mla_w_uq", "mla_w_ukv", "mla_w_o", "dil_w_qkv", "dil_w_o", "fox_w_qkvf", "fox_b_f",
           "fox_w_o")


def _rows(n):
    return -(-n // (8 * LANES)) * 8


def _t5_bucket_np(dist):
    max_exact = REL_BUCKETS // 2
    n = np.maximum(dist.astype(np.float32), np.float32(1.0))
    large = max_exact + (np.log(n / np.float32(max_exact)) / np.float32(math.log(2048 / max_exact))
                         * np.float32(REL_BUCKETS - max_exact)).astype(np.int32)
    large = np.minimum(large, REL_BUCKETS - 1)
    return np.where(dist < max_exact, dist, large)


def _bucket_onehot(dilation):
    i = np.arange(QBLK)[:, None]
    j = np.arange(2 * QBLK)[None, :]
    bucket = _t5_bucket_np(np.clip(QBLK + i - j, 0, None) * dilation).reshape(-1)
    return (np.arange(REL_BUCKETS)[:, None] == bucket[None, :]).astype(np.float32)


def _rope_inv_lanes():
    half = 16
    inv = (np.float32(10000.0) ** (-np.arange(half, dtype=np.float32) / np.float32(half))).astype(np.float32)
    t = np.zeros((1, LANES), np.float32)
    t[0, 0:16] = inv
    t[0, 64:80] = inv
    return t


def _params(sem=None):
    return pltpu.CompilerParams(dimension_semantics=sem, vmem_limit_bytes=VMEM_LIMIT)


def _tile(dim, target):
    if dim <= target or dim % target == 0:
        return min(dim, target)
    t = (target // LANES) * LANES
    while dim % t:
        t -= LANES
    return t


_DIMS = {"nn": (((1,), (0,)), ((), ())), "nt": (((1,), (1,)), ((), ())), "tn": (((0,), (0,)), ((), ()))}


class Lay(NamedTuple):
    arr: jax.Array
    l: int


class Dev(NamedTuple):
    arr: jax.Array
    l: int


def _lshape(op):
    if isinstance(op, Dev):
        g, _, r, w = op.arr.shape
        return r, g * w
    return op.arr.shape[1:] if isinstance(op, Lay) else op.shape


def _op_spec(op, rows_t, cols_t, row_ix, col_ix):
    if isinstance(op, Dev):
        w = op.arr.shape[3]
        assert w % cols_t == 0 and (cols_t % LANES == 0 or cols_t == w), (w, cols_t)
        nb, l = w // cols_t, op.l
        return pl.BlockSpec((1, 1, rows_t, cols_t),
                            lambda i, j, k: (col_ix(i, j, k) // nb, l, row_ix(i, j, k), col_ix(i, j, k) % nb))
    if isinstance(op, Lay):
        l = op.l
        return pl.BlockSpec((1, rows_t, cols_t), lambda i, j, k: (l, row_ix(i, j, k), col_ix(i, j, k)))
    return pl.BlockSpec((rows_t, cols_t), lambda i, j, k: (row_ix(i, j, k), col_ix(i, j, k)))


def _mat(ref):
    return ref[(0,) * (len(ref.shape) - 2)]


def mm(a, b, mode, name, out_dtype=F32, precise=False, add=None, out_dev=None, stack=None, tm=1024, tn=512, tk=512):
    (ar, ac), (br, bc) = _lshape(a), _lshape(b)
    M, K = (ac, ar) if mode == "tn" else (ar, ac)
    N = br if mode == "nt" else bc
    assert K == (bc if mode == "nt" else br)
    tm, tn, tk = _tile(M, tm), _tile(N, tn), _tile(K, tk)
    nk = K // tk
    ix_i, ix_j, ix_k = (lambda i, j, k: i), (lambda i, j, k: j), (lambda i, j, k: k)
    a_spec = _op_spec(a, tk, tm, ix_k, ix_i) if mode == "tn" else _op_spec(a, tm, tk, ix_i, ix_k)
    b_spec = _op_spec(b, tn, tk, ix_j, ix_k) if mode == "nt" else _op_spec(b, tk, tn, ix_k, ix_j)
    buf, n_l, l = stack if stack is not None else (None, 1, 0)
    if out_dev is not None:
        out = Dev(jax.ShapeDtypeStruct((N // out_dev, n_l, M, out_dev), out_dtype), l)
    elif stack is not None:
        out = Lay(jax.ShapeDtypeStruct((n_l, M, N), out_dtype), l)
    else:
        out = jax.ShapeDtypeStruct((M, N), out_dtype)
    o_spec = _op_spec(out, tm, tn, ix_i, ix_j)
    n_in = 3 if add is not None else 2

    def body(*refs):
        a_ref, b_ref = refs[0], refs[1]
        o_ref, acc = refs[-2], refs[-1]
        k = pl.program_id(2)

        @pl.when(k == 0)
        def _():
            acc[...] = jnp.zeros_like(acc)

        if precise:
            acc[...] += lax.dot_general(_mat(a_ref), _mat(b_ref), _DIMS[mode], precision=lax.Precision.HIGHEST,
                                        preferred_element_type=F32)
        else:
            acc[...] += lax.dot_general(_mat(a_ref).astype(BF16), _mat(b_ref).astype(BF16), _DIMS[mode],
                                        preferred_element_type=F32)

        @pl.when(k == nk - 1)
        def _():
            r = acc[...] + refs[2][...] if add is not None else acc[...]
            o_ref[...] = r.astype(o_ref.dtype).reshape(o_ref.shape)

    ins = [getattr(a, "arr", a), getattr(b, "arr", b)] + ([add] if add is not None else [])
    in_specs = [a_spec, b_spec] + ([o_spec] if add is not None else [])
    aliases = {}
    if buf is not None:
        ins.append(buf)
        in_specs.append(pl.BlockSpec(memory_space=pl.ANY))
        aliases = {n_in: 0}
    return pl.pallas_call(
        body, name=name, grid=(M // tm, N // tn, nk), in_specs=in_specs, out_specs=o_spec,
        out_shape=getattr(out, "arr", out), input_output_aliases=aliases,
        scratch_shapes=[pltpu.VMEM((tm, tn), F32)], compiler_params=_params(("parallel", "parallel", "arbitrary")),
    )(*ins)


def _rows_call(body, name, ins, outs, tr=256, acc_outs=()):
    n = ins[0].shape[0]
    tr = min(tr, n)
    in_specs = [pl.BlockSpec((tr, a.shape[1]), lambda i: (i, 0)) if a.shape[0] == n else
                pl.BlockSpec(a.shape, lambda i: (0, 0)) for a in ins]
    out_specs = [pl.BlockSpec((tr, w), lambda i: (i, 0)) for w, _ in outs] + \
                [pl.BlockSpec((1, w), lambda i: (0, 0)) for w in acc_outs]
    out_shape = [jax.ShapeDtypeStruct((n, w), dt) for w, dt in outs] + \
                [jax.ShapeDtypeStruct((1, w), F32) for w in acc_outs]
    res = pl.pallas_call(body, name=name, grid=(n // tr,), in_specs=in_specs, out_specs=out_specs,
                         out_shape=out_shape, compiler_params=_params(("arbitrary",)))(*ins)
    return res[0] if len(res) == 1 else res


def _acc(ref, val):
    @pl.when(pl.program_id(0) == 0)
    def _():
        ref[...] = jnp.zeros_like(ref)

    ref[...] += val


def rms_fwd(x, g, name, res=None):
    def body(*refs):
        x_ref, g_ref = refs[0], refs[1]
        o_ref = refs[-1]
        xv = x_ref[...]
        y = xv * lax.rsqrt(jnp.mean(xv * xv, axis=-1, keepdims=True) + EPS) * g_ref[...]
        o_ref[...] = y + refs[2][...] if res is not None else y

    ins = [x, g] + ([res] if res is not None else [])
    return _rows_call(body, name, ins, [(x.shape[1], F32)])


def rms_bwd(x, g, dy, name, res=None):
    def body(*refs):
        x_ref, g_ref, dy_ref = refs[:3]
        dx_ref, dg_ref = refs[-2], refs[-1]
        xv, dyv = x_ref[...], dy_ref[...]
        r = lax.rsqrt(jnp.mean(xv * xv, axis=-1, keepdims=True) + EPS)
        xh = xv * r
        dxh = dyv * g_ref[...]
        dx = r * (dxh - xh * jnp.mean(dxh * xh, axis=-1, keepdims=True))
        dx_ref[...] = dx + refs[3][...] if res is not None else dx
        _acc(dg_ref, jnp.sum(dyv * xh, axis=0, keepdims=True))

    ins = [x, g, dy] + ([res] if res is not None else [])
    return _rows_call(body, name, ins, [(x.shape[1], F32)], acc_outs=(x.shape[1],))


def _sigmoid(x):
    return 1.0 / (1.0 + jnp.exp(-x))


FF_W = 704
FF_TR = 512


def _ff_spec(shift):
    return pl.BlockSpec((1, 1, FF_TR, FF_W), lambda d, i: (d + shift, 0, i, 0))


def swiglu_fwd(gu, name):
    def body(g_ref, u_ref, o_ref):
        gate = g_ref[...]
        o_ref[...] = gate * _sigmoid(gate) * u_ref[...]

    return pl.pallas_call(body, name=name, grid=(4, S // FF_TR), in_specs=[_ff_spec(0), _ff_spec(4)],
                          out_specs=_ff_spec(0), out_shape=jax.ShapeDtypeStruct((4, 1, S, FF_W), F32),
                          compiler_params=_params(("parallel", "parallel")))(gu, gu)


def swiglu_bwd(gu, dact, name):
    def body(g_ref, u_ref, d_ref, o_ref):
        gate, d = g_ref[...], d_ref[...]
        sg = _sigmoid(gate)

        @pl.when(pl.program_id(0) < 4)
        def _():
            o_ref[...] = d * u_ref[...] * sg * (1.0 + gate * (1.0 - sg))

        @pl.when(pl.program_id(0) >= 4)
        def _():
            o_ref[...] = d * gate * sg

    def half(shift):
        return pl.BlockSpec((1, 1, FF_TR, FF_W), lambda d, i: (d % 4 + shift, 0, i, 0))

    return pl.pallas_call(body, name=name, grid=(8, S // FF_TR), in_specs=[half(0), half(4), half(0)],
                          out_specs=_ff_spec(0), out_shape=jax.ShapeDtypeStruct((8, 1, S, FF_W), F32),
                          compiler_params=_params(("parallel", "parallel")))(gu, gu, dact)


def ple_fwd(h, pp, gt, name):
    def body(h_ref, p_ref, g_ref, o_ref):
        o_ref[...] = h_ref[...] + p_ref[...] * _sigmoid(g_ref[...])

    return _rows_call(body, name, [h, pp, gt], [(D, F32)])


def ple_bwd(dh, pp, gt, name):
    def body(d_ref, p_ref, g_ref, dp_ref, dg_ref):
        d, sg = d_ref[...], _sigmoid(g_ref[...])
        dp_ref[...] = d * sg
        dg_ref[...] = d * p_ref[...] * sg * (1.0 - sg)

    return _rows_call(body, name, [dh, pp, gt], [(D, F32), (D, F32)])


def loss_head(y, target, name):
    def body(y_ref, t_ref, d_ref, l_ref):
        e = y_ref[...] - t_ref[...]
        d_ref[...] = e * (1.0 / D)
        col = jnp.sum(e * e, axis=0, keepdims=True) * (0.5 / D)
        _acc(l_ref, sum(col[:, LANES * c:LANES * (c + 1)] for c in range(D // LANES)))

    return _rows_call(body, name, [y, target], [(D, F32)], acc_outs=(LANES,))


def rope_tables(pos_col, name):
    inv = jnp.asarray(_rope_inv_lanes())

    def body(p_ref, inv_ref, c_ref, s_ref):
        ang = p_ref[...].astype(F32) * inv_ref[...]
        lane = lax.broadcasted_iota(jnp.int32, ang.shape, 1)
        first, second = lane < 16, (lane >= 64) & (lane < 80)
        c_ref[...] = jnp.where(first | second, jnp.cos(ang), 1.0)
        sn = jnp.sin(ang)
        s_ref[...] = jnp.where(first, -sn, jnp.where(second, sn, 0.0))

    return _rows_call(body, name, [pos_col, inv], [(LANES, F32), (LANES, F32)])


def _rope(x, c, s):
    return x * c + pltpu.roll(x, 64, axis=1) * s


def _rope_t(d, c, s):
    return d * c + pltpu.roll(d * s, 64, axis=1)


def mla_qk_fwd(qp, kvp, kr, cos, sin, name):
    def body(q_ref, k_ref, kr_ref, c_ref, s_ref, qo_ref, ko_ref):
        c, s = c_ref[...], s_ref[...]
        kr_rot = _rope(kr_ref[...], c, s)
        for h in range(HEADS):
            sl = slice(LANES * h, LANES * (h + 1))
            qo_ref[:, sl] = _rope(q_ref[:, sl], c, s).astype(BF16)
            ko_ref[:, sl] = (k_ref[:, sl] + kr_rot).astype(BF16)

    n = qp.shape[0]
    tr = 256
    w = HEADS * LANES
    return pl.pallas_call(
        body, name=name, grid=(n // tr,),
        in_specs=[pl.BlockSpec((tr, w), lambda i: (i, 0)), pl.BlockSpec((tr, w), lambda i: (i, 0)),
                  pl.BlockSpec((tr, LANES), lambda i: (i, 0)), pl.BlockSpec((tr, LANES), lambda i: (i, 0)),
                  pl.BlockSpec((tr, LANES), lambda i: (i, 0))],
        out_specs=[pl.BlockSpec((tr, w), lambda i: (i, 0))] * 2,
        out_shape=[jax.ShapeDtypeStruct((n, w), BF16)] * 2, compiler_params=_params(("arbitrary",)),
    )(qp, kvp, kr, cos, sin)


def mla_qk_bwd(dq, dk, cos, sin, name):
    def body(dq_ref, dk_ref, c_ref, s_ref, dqp_ref, dkr_ref):
        c, s = c_ref[...], s_ref[...]
        tot = jnp.zeros(c.shape, F32)
        for h in range(HEADS):
            sl = slice(LANES * h, LANES * (h + 1))
            dqp_ref[:, sl] = _rope_t(dq_ref[:, sl], c, s)
            tot = tot + dk_ref[:, sl]
        dkr_ref[...] = _rope_t(tot, c, s)

    return _rows_call(body, name, [dq, dk, cos, sin], [(HEADS * LANES, F32), (LANES, F32)])


TQ = 256


def _pair_masks(shape):
    lane = lax.broadcasted_iota(jnp.int32, shape, 1)
    return (lane < 64, lane >= 64)


def _causal_probs(q_a, k_a, scale, b0, cq, ck):
    s = lax.dot_general(q_a, k_a, _DIMS["nt"], preferred_element_type=F32) * scale
    if cq is not None:
        s = s + (cq - ck)
    row = lax.broadcasted_iota(jnp.int32, s.shape, 0) + b0
    col = lax.broadcasted_iota(jnp.int32, s.shape, 1)
    s = jnp.where(col <= row, s, NEG)
    e = jnp.exp(s - jnp.max(s, axis=-1, keepdims=True))
    return e / jnp.sum(e, axis=-1, keepdims=True)


def attn_fwd(q, k, v, name, *, wide, scale, q_off=0, k_off=0, v_off=0, cum=None, cum_t=None):
    qw = 2 * LANES if wide else LANES
    forget = cum is not None

    def body(*refs):
        q_ref, k_ref, v_ref = refs[:3]
        o_ref = refs[-1]
        m0, m1 = _pair_masks((TQ, LANES))
        for qi in range(S // TQ):
            b0, b1 = qi * TQ, (qi + 1) * TQ
            outs = []
            for a, msk in enumerate((m0, m1)):
                if wide:
                    q_a, k_a = q_ref[b0:b1, LANES * a:LANES * (a + 1)], k_ref[:b1, LANES * a:LANES * (a + 1)]
                else:
                    q_a, k_a = jnp.where(msk, q_ref[b0:b1, :], jnp.zeros((), BF16)), k_ref[:b1, :]
                cq = refs[3][0, b0:b1, a:a + 1] if forget else None
                ck = refs[4][0, a:a + 1, :b1] if forget else None
                p = _causal_probs(q_a, k_a, scale, b0, cq, ck)
                outs.append(jnp.dot(p.astype(BF16), v_ref[:b1, :], preferred_element_type=F32))
            o_ref[b0:b1, :] = jnp.where(m0, outs[0], outs[1])

    in_specs = [pl.BlockSpec((S, qw), lambda h: (0, q_off * LANES // qw + h)),
                pl.BlockSpec((S, qw), lambda h: (0, k_off * LANES // qw + h)),
                pl.BlockSpec((S, LANES), lambda h: (0, v_off + h))]
    ins = [q, k, v]
    if forget:
        in_specs += [pl.BlockSpec((1, S, 2), lambda h: (h, 0, 0)), pl.BlockSpec((1, 2, S), lambda h: (h, 0, 0))]
        ins += [cum, cum_t]
    return pl.pallas_call(
        body, name=name, grid=(PAIRS,), in_specs=in_specs, out_specs=pl.BlockSpec((S, LANES), lambda h: (0, h)),
        out_shape=jax.ShapeDtypeStruct((S, PAIRS * LANES), F32), compiler_params=_params(("arbitrary",)),
    )(*ins)


def attn_bwd(q, k, v, o, do, name, *, wide, scale, q_off=0, k_off=0, v_off=0, cum=None, cum_t=None):
    qw = 2 * LANES if wide else LANES
    forget = cum is not None

    def body(*refs):
        q_ref, k_ref, v_ref, o_ref, do_ref = refs[:5]
        n_out = 5 if forget else 3
        outs = refs[-(n_out + 2):-2]
        dq_ref, dk_ref, dv_ref = outs[:3]
        dk_acc, dv_acc = refs[-2], refs[-1]
        dk_acc[...] = jnp.zeros_like(dk_acc)
        dv_acc[...] = jnp.zeros_like(dv_acc)
        if forget:
            dcq_ref, dck_ref = outs[3], outs[4]
            dck_ref[...] = jnp.zeros_like(dck_ref)
        m0, m1 = _pair_masks((TQ, LANES))
        for qi in range(S // TQ):
            b0, b1 = qi * TQ, (qi + 1) * TQ
            do2 = do_ref[b0:b1, :]
            dd = do2 * o_ref[b0:b1, :]
            do_b = do2.astype(BF16)
            mk0, mk1 = _pair_masks((b1, LANES))
            dqs = []
            for a, (msk, mk) in enumerate(((m0, mk0), (m1, mk1))):
                lanes = slice(LANES * a, LANES * (a + 1)) if wide else slice(0, LANES)
                if wide:
                    q_a, k_a = q_ref[b0:b1, lanes], k_ref[:b1, lanes]
                else:
                    q_a, k_a = jnp.where(msk, q_ref[b0:b1, :], jnp.zeros((), BF16)), k_ref[:b1, :]
                cq = refs[5][0, b0:b1, a:a + 1] if forget else None
                ck = refs[6][0, a:a + 1, :b1] if forget else None
                p = _causal_probs(q_a, k_a, scale, b0, cq, ck)
                dp = lax.dot_general(jnp.where(msk, do_b, jnp.zeros((), BF16)), v_ref[:b1, :], _DIMS["nt"],
                                     preferred_element_type=F32)
                delta = jnp.sum(jnp.where(msk, dd, 0.0), axis=-1, keepdims=True)
                ds = p * (dp - delta)
                if forget:
                    dcq_ref[0, b0:b1, a:a + 1] = jnp.sum(ds, axis=-1, keepdims=True)
                    dck_ref[0, a:a + 1, :b1] -= jnp.sum(ds, axis=0, keepdims=True)
                ds_b = (ds * scale).astype(BF16)
                dqs.append(jnp.dot(ds_b, k_a, preferred_element_type=F32))
                dk_acc[:b1, lanes] += lax.dot_general(ds_b, q_a, _DIMS["tn"], preferred_element_type=F32)
                dv_acc[:b1, :] += jnp.where(mk, lax.dot_general(p.astype(BF16), do_b, _DIMS["tn"],
                                                                 preferred_element_type=F32), 0.0)
            if wide:
                dq_ref[b0:b1, :LANES] = dqs[0]
                dq_ref[b0:b1, LANES:] = dqs[1]
            else:
                dq_ref[b0:b1, :] = jnp.where(m0, dqs[0], dqs[1])
        dk_ref[...] = dk_acc[...]
        dv_ref[...] = dv_acc[...]

    pair = pl.BlockSpec((S, LANES), lambda h: (0, h))
    qk_out = pl.BlockSpec((S, qw), lambda h: (0, h))
    in_specs = [pl.BlockSpec((S, qw), lambda h: (0, q_off * LANES // qw + h)),
                pl.BlockSpec((S, qw), lambda h: (0, k_off * LANES // qw + h)),
                pl.BlockSpec((S, LANES), lambda h: (0, v_off + h)), pair, pair]
    ins = [q, k, v, o, do]
    out_specs = [qk_out, qk_out, pair]
    out_shape = [jax.ShapeDtypeStruct((S, PAIRS * qw), F32)] * 2 + [jax.ShapeDtypeStruct((S, PAIRS * LANES), F32)]
    if forget:
        by_q, by_k = pl.BlockSpec((1, S, 2), lambda h: (h, 0, 0)), pl.BlockSpec((1, 2, S), lambda h: (h, 0, 0))
        in_specs += [by_q, by_k]
        ins += [cum, cum_t]
        out_specs += [by_q, by_k]
        out_shape += [jax.ShapeDtypeStruct((PAIRS, S, 2), F32), jax.ShapeDtypeStruct((PAIRS, 2, S), F32)]
    return pl.pallas_call(
        body, name=name, grid=(PAIRS,), in_specs=in_specs, out_specs=out_specs, out_shape=out_shape,
        scratch_shapes=[pltpu.VMEM((S, qw), F32), pltpu.VMEM((S, LANES), F32)],
        compiler_params=_params(("arbitrary",)),
    )(*ins)


def _tri(lower):
    r = lax.broadcasted_iota(jnp.int32, (QBLK, QBLK), 0)
    c = lax.broadcasted_iota(jnp.int32, (QBLK, QBLK), 1)
    return jnp.where((c <= r) if lower else (c >= r), 1.0, 0.0).astype(F32)


def _hi_dot(a, b):
    return jnp.dot(a, b, precision=lax.Precision.HIGHEST, preferred_element_type=F32)


def fox_gate_fwd(fl, bias, name):
    def body(f_ref, b_ref, o_ref):
        tri = _tri(True)
        carry = jnp.zeros((1, LANES), F32)
        for n in range(S // QBLK):
            x = f_ref[n * QBLK:(n + 1) * QBLK, :].astype(F32) + b_ref[...]
            lf = jnp.minimum(x, 0.0) - jnp.log(1.0 + jnp.exp(-jnp.abs(x)))
            c = _hi_dot(tri, lf) + carry
            o_ref[n * QBLK:(n + 1) * QBLK, :] = c
            carry = c[QBLK - 1:QBLK, :]

    return pl.pallas_call(body, name=name, out_shape=jax.ShapeDtypeStruct((S, LANES), F32),
                          compiler_params=_params())(fl, bias)


def fox_gate_bwd(fl, bias, dcq, dck, name):
    def body(f_ref, b_ref, dq_ref, dk_ref, o_ref, db_ref):
        tri = _tri(False)
        carry = jnp.zeros((1, LANES), F32)
        db = jnp.zeros((1, LANES), F32)
        for n in reversed(range(S // QBLK)):
            rows = slice(n * QBLK, (n + 1) * QBLK)
            dlf = _hi_dot(tri, dq_ref[rows, :] + dk_ref[rows, :]) + carry
            carry = dlf[0:1, :]
            x = f_ref[rows, :].astype(F32) + b_ref[...]
            dx = dlf * (1.0 - _sigmoid(x))
            o_ref[rows, :] = dx
            db = db + jnp.sum(dx, axis=0, keepdims=True)
        db_ref[...] = db

    return pl.pallas_call(body, name=name, out_shape=[jax.ShapeDtypeStruct((S, LANES), F32),
                                                      jax.ShapeDtypeStruct((1, LANES), F32)],
                          compiler_params=_params())(fl, bias, dcq, dck)


def _band_valid(first):
    w = QBLK if first else 2 * QBLK
    i = lax.broadcasted_iota(jnp.int32, (QBLK, w), 0)
    j = lax.broadcasted_iota(jnp.int32, (QBLK, w), 1)
    return (j <= i) if first else ((j >= i) & (j - QBLK <= i))


def _band_logits(q_a, kk, bias, first):
    s = lax.dot_general(q_a, kk, _DIMS["nt"], preferred_element_type=F32) * 0.125 + bias
    return jnp.where(_band_valid(first), s, NEG)


def dil_fwd(qkv, bias, g, name):
    d = DIL[g][1]
    ls = S // d
    view = qkv.reshape(ls, d * 9216)

    def body(q_ref, k_ref, v_ref, b_ref, o_ref, l_ref):
        m0, m1 = _pair_masks((QBLK, LANES))
        for n in range(ls // QBLK):
            rows = slice(n * QBLK, (n + 1) * QBLK)
            keys = rows if n == 0 else slice((n - 1) * QBLK, (n + 1) * QBLK)
            os_, ls_ = [], []
            for a, msk in enumerate((m0, m1)):
                q_a = jnp.where(msk, q_ref[rows, :], jnp.zeros((), BF16))
                bias_a = b_ref[a, :, QBLK:] if n == 0 else b_ref[a]
                s = _band_logits(q_a, k_ref[keys, :], bias_a, n == 0)
                mx = jnp.max(s, axis=-1, keepdims=True)
                e = jnp.exp(s - mx)
                l = jnp.sum(e, axis=-1, keepdims=True)
                os_.append(jnp.dot((e / l).astype(BF16), v_ref[keys, :], preferred_element_type=F32))
                ls_.append(mx + jnp.log(l))
            o_ref[rows, :] = jnp.where(m0, os_[0], os_[1])
            l_ref[rows, :] = jnp.where(m0, ls_[0], ls_[1])

    def col(j):
        return lambda h, r: (0, r * 72 + g * 24 + j * 8 + h)

    out = pl.BlockSpec((ls, LANES), lambda h, r: (0, r * 8 + h))
    o, lse = pl.pallas_call(
        body, name=name, grid=(PAIRS, d),
        in_specs=[pl.BlockSpec((ls, LANES), col(0)), pl.BlockSpec((ls, LANES), col(1)), pl.BlockSpec((ls, LANES), col(2)),
                  pl.BlockSpec((2, QBLK, 2 * QBLK), lambda h, r: (h, 0, 0))],
        out_specs=[out, out], out_shape=[jax.ShapeDtypeStruct((ls, d * D), F32)] * 2,
        compiler_params=_params(("arbitrary", "arbitrary")),
    )(view, view, view, bias)
    return o.reshape(S, D), lse.reshape(S, D)


def dil_merge(os_, lses, name):
    def body(o0, o1, o2, l0, l1, l2, o_ref, l_ref):
        ls_ = [l0[...], l1[...], l2[...]]
        mx = jnp.maximum(jnp.maximum(ls_[0], ls_[1]), ls_[2])
        tot = mx + jnp.log(sum(jnp.exp(l - mx) for l in ls_))
        o_ref[...] = sum(jnp.exp(l - tot) * o[...] for l, o in zip(ls_, (o0, o1, o2)))
        l_ref[...] = tot

    return _rows_call(body, name, list(os_) + list(lses), [(D, F32), (D, F32)])


def dil_bwd(qkv, bias, o, lse, do, g, name):
    d = DIL[g][1]
    ls = S // d
    view = qkv.reshape(ls, d * 9216)
    o, lse, do = (t.reshape(ls, d * D) for t in (o, lse, do))

    def body(q_ref, k_ref, v_ref, b_ref, o_ref, l_ref, do_ref, dq_ref, dk_ref, dv_ref, db_ref, dk_acc, dv_acc):
        @pl.when(pl.program_id(1) == 0)
        def _():
            db_ref[...] = jnp.zeros_like(db_ref)

        dk_acc[...] = jnp.zeros_like(dk_acc)
        dv_acc[...] = jnp.zeros_like(dv_acc)
        m0, m1 = _pair_masks((QBLK, LANES))
        for n in range(ls // QBLK):
            rows = slice(n * QBLK, (n + 1) * QBLK)
            keys = rows if n == 0 else slice((n - 1) * QBLK, (n + 1) * QBLK)
            nk = QBLK if n == 0 else 2 * QBLK
            do2, lse2 = do_ref[rows, :], l_ref[rows, :]
            dd = do2 * o_ref[rows, :]
            do_b = do2.astype(BF16)
            mk0, mk1 = _pair_masks((nk, LANES))
            dqs = []
            for a, (msk, mk) in enumerate(((m0, mk0), (m1, mk1))):
                q_a = jnp.where(msk, q_ref[rows, :], jnp.zeros((), BF16))
                kk = k_ref[keys, :]
                bias_a = b_ref[a, :, QBLK:] if n == 0 else b_ref[a]
                s = _band_logits(q_a, kk, bias_a, n == 0)
                lse_a = jnp.max(jnp.where(msk, lse2, -jnp.inf), axis=-1, keepdims=True)
                p = jnp.exp(s - lse_a)
                dp = lax.dot_general(jnp.where(msk, do_b, jnp.zeros((), BF16)), v_ref[keys, :], _DIMS["nt"],
                                     preferred_element_type=F32)
                delta = jnp.sum(jnp.where(msk, dd, 0.0), axis=-1, keepdims=True)
                ds = p * (dp - delta)
                if n == 0:
                    db_ref[a, :, QBLK:] += ds
                else:
                    db_ref[a] += ds
                ds_b = (ds * 0.125).astype(BF16)
                dqs.append(jnp.dot(ds_b, kk, preferred_element_type=F32))
                dk_acc[keys, :] += lax.dot_general(ds_b, q_a, _DIMS["tn"], preferred_element_type=F32)
                dv_acc[keys, :] += jnp.where(mk, lax.dot_general(p.astype(BF16), do_b, _DIMS["tn"],
                                                                 preferred_element_type=F32), 0.0)
            dq_ref[rows, :] = jnp.where(m0, dqs[0], dqs[1])
        dk_ref[...] = dk_acc[...]
        dv_ref[...] = dv_acc[...]

    def col(j):
        return lambda h, r: (0, r * 72 + g * 24 + j * 8 + h)

    nat = pl.BlockSpec((ls, LANES), lambda h, r: (0, r * 8 + h))
    b_spec = pl.BlockSpec((2, QBLK, 2 * QBLK), lambda h, r: (h, 0, 0))
    dq, dk, dv, db = pl.pallas_call(
        body, name=name, grid=(PAIRS, d),
        in_specs=[pl.BlockSpec((ls, LANES), col(0)), pl.BlockSpec((ls, LANES), col(1)), pl.BlockSpec((ls, LANES), col(2)),
                  b_spec, nat, nat, nat],
        out_specs=[nat, nat, nat, b_spec],
        out_shape=[jax.ShapeDtypeStruct((ls, d * D), F32)] * 3 + [jax.ShapeDtypeStruct((HEADS, QBLK, 2 * QBLK), F32)],
        scratch_shapes=[pltpu.VMEM((ls, LANES), F32), pltpu.VMEM((ls, LANES), F32)],
        compiler_params=_params(("arbitrary", "arbitrary")),
    )(view, view, view, bias, o, lse, do)
    return dq.reshape(S, D), dk.reshape(S, D), dv.reshape(S, D), db


def _place():
    x, y, c = lax.axis_index("x"), lax.axis_index("y"), lax.axis_index("c")
    return x, y, c


def _dev_slot(ref, by_rows, dev):
    return ref.at[:, dev] if by_rows else ref.at[dev]


def all_gather(shards, by_rows, name, in_vmem=False):
    n = len(shards)

    def body(*refs):
        x_refs, out_refs = refs[:n], refs[n:2 * n]
        send_sems, recv_sems, local_sems = refs[2 * n:]
        x, y, c = _place()
        me, sibling = (x, y, c), (x, y, 1 - c)
        chips = [(1 - x, y), (x, 1 - y), (1 - x, 1 - y)]

        def slot(t, px, py, pc):
            return _dev_slot(out_refs[t], by_rows[t], 4 * px + 2 * py + pc)

        def copy(t, k, blk, to, src=None):
            return pltpu.make_async_remote_copy(
                src_ref=slot(t, *blk) if src is None else src, dst_ref=slot(t, *blk), send_sem=send_sems.at[7 * t + k],
                recv_sem=recv_sems.at[7 * t + k], device_id=to, device_id_type=MESH_ID)

        mine = [pltpu.make_async_copy(x_refs[t], slot(t, *me), local_sems.at[t]) for t in range(n)]
        for cp in mine:
            cp.start()
        first = []
        for t in range(n):
            first.append(copy(t, 0, me, sibling, src=x_refs[t]))
            first += [copy(t, 1 + j, me, (*chip, c), src=x_refs[t]) for j, chip in enumerate(chips)]
        for cp in first:
            cp.start()
        passed = []
        for j, chip in enumerate(chips):
            for t in range(n):
                copy(t, 1 + j, (*chip, c), me).wait_recv()
                passed.append(copy(t, 4 + j, (*chip, c), sibling))
                passed[-1].start()
        for t in range(n):
            copy(t, 0, sibling, me).wait_recv()
            for j, chip in enumerate(chips):
                copy(t, 4 + j, (*chip, 1 - c), me).wait_recv()
        for cp in first + passed:
            cp.wait_send()
        for cp in mine:
            cp.wait()

    def gathered(s, rows):
        shp = (s.shape[0], N_DEV) + s.shape[1:] if rows else (N_DEV,) + s.shape
        return jax.ShapeDtypeStruct(shp, s.dtype)

    space = pl.BlockSpec(memory_space=pltpu.VMEM if in_vmem else pl.ANY)
    return pl.pallas_call(
        body, name=name, out_shape=[gathered(s, r) for s, r in zip(shards, by_rows)],
        in_specs=[space] * n, out_specs=[space] * n,
        scratch_shapes=[pltpu.SemaphoreType.DMA((7 * n,)), pltpu.SemaphoreType.DMA((7 * n,)),
                        pltpu.SemaphoreType.DMA((n,))],
        compiler_params=pltpu.CompilerParams(vmem_limit_bytes=VMEM_LIMIT),
    )(*shards)


def exchange(grads, by_rows, name):
    n = len(grads)

    def body(*refs):
        x_refs, out_refs = refs[:n], refs[n:2 * n]
        send_sems, recv_sems, local_sems = refs[2 * n:]
        x, y, c = _place()
        me = 4 * x + 2 * y + c
        mine = [pltpu.make_async_copy(_dev_slot(x_refs[t], by_rows[t], me), out_refs[t].at[me], local_sems.at[t])
                for t in range(n)]
        for cp in mine:
            cp.start()
        copies = []
        for k in range(1, N_DEV):
            px = 1 - x if k & 4 else x
            py = 1 - y if k & 2 else y
            pc = 1 - c if k & 1 else c
            peer = 4 * px + 2 * py + pc
            for t in range(n):
                copies.append(pltpu.make_async_remote_copy(
                    src_ref=_dev_slot(x_refs[t], by_rows[t], peer), dst_ref=out_refs[t].at[me],
                    send_sem=send_sems.at[7 * t + k - 1], recv_sem=recv_sems.at[7 * t + k - 1],
                    device_id=(px, py, pc), device_id_type=MESH_ID))
        for cp in copies:
            cp.start()
        for cp in copies:
            cp.wait_recv()
        for cp in copies:
            cp.wait_send()
        for cp in mine:
            cp.wait()

    def landed(g, rows):
        shp = (N_DEV, g.shape[0]) + g.shape[2:] if rows else g.shape
        return jax.ShapeDtypeStruct(shp, g.dtype)

    space = pl.BlockSpec(memory_space=pl.ANY)
    return pl.pallas_call(
        body, name=name, out_shape=[landed(g, r) for g, r in zip(grads, by_rows)],
        in_specs=[space] * n, out_specs=[space] * n,
        scratch_shapes=[pltpu.SemaphoreType.DMA((7 * n,)), pltpu.SemaphoreType.DMA((7 * n,)),
                        pltpu.SemaphoreType.DMA((n,))],
        compiler_params=pltpu.CompilerParams(vmem_limit_bytes=VMEM_LIMIT),
    )(*grads)


ADAM_BLOCK_BYTES = 3 << 19


def adamw(w, m, v, parts, name):
    n_parts = parts.shape[0]
    n_l, r, c = w.shape
    lane_c = -(-c // LANES) * LANES
    fits = [t for t in range(16, r, 16) if r % t == 0 and t * lane_c * 4 <= ADAM_BLOCK_BYTES]
    tr = max(fits) if fits and r * lane_c * 4 > ADAM_BLOCK_BYTES else r
    c1 = 1.0 / (1.0 - ADAM_B1 ** ADAM_STEP)
    c2 = 1.0 / (1.0 - ADAM_B2 ** ADAM_STEP)

    def body(w_ref, m_ref, v_ref, p_ref, g_ref, d_ref, nm_ref, nv_ref):
        g = p_ref[0].astype(F32)
        for j in range(1, n_parts):
            g = g + p_ref[j].astype(F32)
        nm = ADAM_B1 * m_ref[...] + (1.0 - ADAM_B1) * g
        nv = ADAM_B2 * v_ref[...] + (1.0 - ADAM_B2) * (g * g)
        g_ref[...] = g
        nm_ref[...] = nm
        nv_ref[...] = nv
        d_ref[...] = -ADAM_LR * ((nm * c1) / (jnp.sqrt(nv * c2) + ADAM_EPS) + ADAM_WD * w_ref[...])

    blk = pl.BlockSpec((1, tr, c), lambda l, i: (l, i, 0))
    return pl.pallas_call(
        body, name=name, grid=(n_l, r // tr),
        in_specs=[blk, blk, blk, pl.BlockSpec((n_parts, 1, tr, c), lambda l, i: (0, l, i, 0))],
        out_specs=[blk] * 4, out_shape=[jax.ShapeDtypeStruct((n_l, r, c), F32)] * 4,
        compiler_params=_params(("parallel", "parallel")),
    )(w, m, v, parts)


def sum_parts(parts, name):
    def body(p_ref, o_ref):
        g = p_ref[0]
        for j in range(1, parts.shape[0]):
            g = g + p_ref[j]
        o_ref[...] = g

    return pl.pallas_call(body, name=name, out_shape=jax.ShapeDtypeStruct(parts.shape[1:], F32),
                          compiler_params=_params())(parts)


def _pack(arrs, rows, dtype):
    flat = jnp.concatenate([a.reshape(-1).astype(dtype) for a in arrs])
    return jnp.pad(flat, (0, rows * LANES - flat.shape[0])).reshape(rows, LANES)


def _unpack(packed, shapes):
    flat, out, off = packed.reshape(-1), [], 0
    for shp in shapes:
        n = int(np.prod(shp))
        out.append(flat[off:off + n].reshape(shp))
        off += n
    return out


def _cat(parts):
    return jnp.concatenate(parts, axis=1)


def _mla_layout(w_a, g_uq, g_ukv, j):
    def z(r, n):
        return jnp.zeros((r, n), BF16)

    wa = w_a[j]
    a = _cat([wa[:, :640], wa[:, 640:656], z(D, 48), wa[:, 656:672], z(D, 48)])
    q, k, v = [], [], []
    for h in range(HEADS):
        b = g_uq[h // 2, j][:, 96 * (h % 2):96 * (h % 2 + 1)]
        q += [b[:, 64:80], b[:, 0:32], z(Q_RANK, 16), b[:, 80:96], b[:, 32:64], z(Q_RANK, 16)]
        b = g_ukv[h // 2, j][:, LANES * (h % 2):LANES * (h % 2 + 1)]
        k += [z(KV_RANK, 16), b[:, 0:32], z(KV_RANK, 32), b[:, 32:64], z(KV_RANK, 16)]
        v.append(b[:, 64:128])
    return a, _cat(q), _cat(k + v)


def _mla_unlayout(d_a, d_uq, d_ukv):
    a = _cat([d_a[:, :640], d_a[:, 640:656], d_a[:, 704:720]])
    uq, ukv = [], []
    for dev in range(N_DEV):
        q, kv = [], []
        for h in (2 * dev, 2 * dev + 1):
            b = d_uq[:, LANES * h:LANES * (h + 1)]
            q += [b[:, 16:48], b[:, 80:112], b[:, 0:16], b[:, 64:80]]
            b = d_ukv[:, LANES * h:LANES * (h + 1)]
            kv += [b[:, 16:48], b[:, 80:112], d_ukv[:, HEADS * LANES + 64 * h:HEADS * LANES + 64 * (h + 1)]]
        uq.append(_cat(q))
        ukv.append(_cat(kv))
    return a, jnp.stack(uq), jnp.stack(ukv)


def _mixer_fwd(kind, tag, hn, W, aux):
    if kind == 0:
        a = mm(hn, W["w_a"], "nn", f"{tag}_a", tn=768)
        cq = rms_fwd(a[:, :Q_RANK], W["q_norm"], f"{tag}_cq")
        ckv = rms_fwd(a[:, Q_RANK:Q_RANK + KV_RANK], W["kv_norm"], f"{tag}_ckv")
        qp = mm(cq, W["w_uq"], "nn", f"{tag}_uq", tk=384)
        kvp = mm(ckv, W["w_ukv"], "nn", f"{tag}_ukv", tk=256)
        q, k = mla_qk_fwd(qp, kvp, a[:, 640:], aux["cos"], aux["sin"], f"{tag}_qk")
        v = kvp.astype(BF16)
        o = attn_fwd(q, k, v, f"{tag}_attn", wide=True, scale=96 ** -0.5, v_off=HEADS)
        y = mm(o, W["w_o"], "nn", f"{tag}_o")
        return y, (a, cq, ckv, q, k, v, o)
    if kind == 1:
        qkv = mm(hn, W["w_qkv"], "nn", f"{tag}_qkv", out_dtype=BF16, tn=384)
        parts = [dil_fwd(qkv, aux["dil_bias"][g], g, f"{tag}_g{g}") for g in range(3)]
        o, lse = dil_merge([p_[0] for p_ in parts], [p_[1] for p_ in parts], f"{tag}_merge")
        y = mm(o, W["w_o"], "nn", f"{tag}_o")
        return y, (qkv, o, lse)
    a = mm(hn, W["w_qkvf"], "nn", f"{tag}_qkvf", tn=640)
    fl = a[:, 3072:]
    cum = fox_gate_fwd(fl, aux["fox_b"], f"{tag}_gate")[:, :HEADS]
    cum_q = cum.reshape(S, PAIRS, 2).transpose(1, 0, 2)
    cum_k = cum.T.reshape(PAIRS, 2, S)
    ab = a.astype(BF16)
    o = attn_fwd(ab, ab, ab, f"{tag}_attn", wide=False, scale=0.125, k_off=PAIRS, v_off=2 * PAIRS, cum=cum_q, cum_t=cum_k)
    y = mm(o, W["w_o"], "nn", f"{tag}_o")
    return y, (fl, ab, cum_q, cum_k, o)


def _mixer_bwd(kind, tag, hn, dy, W, aux, saved):
    gr = {}
    if kind == 0:
        a, cq, ckv, q, k, v, o = saved
        gr["w_o"] = mm(o, dy, "tn", f"{tag}_dwo", out_dtype=BF16)
        do = mm(dy, W["w_o"], "nt", f"{tag}_do")
        dq, dk, dv = attn_bwd(q, k, v, o, do, f"{tag}_attn_b", wide=True, scale=96 ** -0.5, v_off=HEADS)
        dqp, dkr = mla_qk_bwd(dq, dk, aux["cos"], aux["sin"], f"{tag}_qk_b")
        dkvp = jnp.concatenate([dk, dv], axis=1)
        gr["w_ukv"] = mm(ckv, dkvp, "tn", f"{tag}_dwukv", out_dtype=BF16, tm=256)
        dckv = mm(dkvp, W["w_ukv"], "nt", f"{tag}_dckv", tn=256)
        gr["w_uq"] = mm(cq, dqp, "tn", f"{tag}_dwuq", out_dtype=BF16, tm=384)
        dcq = mm(dqp, W["w_uq"], "nt", f"{tag}_dcq", tn=384)
        da_q, gr["q_norm"] = rms_bwd(a[:, :Q_RANK], W["q_norm"], dcq, f"{tag}_cq_b")
        da_kv, gr["kv_norm"] = rms_bwd(a[:, Q_RANK:Q_RANK + KV_RANK], W["kv_norm"], dckv, f"{tag}_ckv_b")
        da = jnp.concatenate([da_q, da_kv, dkr], axis=1)
        gr["w_a"] = mm(hn, da, "tn", f"{tag}_dwa", out_dtype=BF16, tn=768)
        return mm(da, W["w_a"], "nt", f"{tag}_dhn", tk=768), gr
    if kind == 1:
        qkv, o, lse = saved
        gr["w_o"] = mm(o, dy, "tn", f"{tag}_dwo", out_dtype=BF16)
        do = mm(dy, W["w_o"], "nt", f"{tag}_do")
        cols, dbs = [], []
        for g in range(3):
            dq, dk, dv, db = dil_bwd(qkv, aux["dil_bias"][g], o, lse, do, g, f"{tag}_g{g}_b")
            cols += [dq, dk, dv]
            dbs.append(db)
        dqkv = jnp.concatenate(cols, axis=1)
        gr["dil_dbias"] = dbs
        gr["w_qkv"] = mm(hn, dqkv, "tn", f"{tag}_dwqkv", out_dtype=BF16, out_dev=1152, tn=384)
        return mm(dqkv, W["w_qkv"], "nt", f"{tag}_dhn", tk=384), gr
    fl, ab, cum_q, cum_k, o = saved
    gr["w_o"] = mm(o, dy, "tn", f"{tag}_dwo", out_dtype=BF16)
    do = mm(dy, W["w_o"], "nt", f"{tag}_do")
    dq, dk, dv, dcq, dck = attn_bwd(ab, ab, ab, o, do, f"{tag}_attn_b", wide=False, scale=0.125, k_off=PAIRS,
                                    v_off=2 * PAIRS, cum=cum_q, cum_t=cum_k)
    pad = ((0, 0), (0, LANES - HEADS))
    dcq = jnp.pad(dcq.transpose(1, 0, 2).reshape(S, HEADS), pad)
    dck = jnp.pad(dck.reshape(HEADS, S).T, pad)
    dfl, gr["b_f"] = fox_gate_bwd(fl, aux["fox_b"], dcq, dck, f"{tag}_gate_b")
    da = jnp.concatenate([dq, dk, dv, dfl], axis=1)
    gr["w_qkvf"] = mm(hn, da, "tn", f"{tag}_dwqkvf", out_dtype=BF16, tn=640)
    return mm(da, W["w_qkvf"], "nt", f"{tag}_dhn", tk=640), gr


def kernel(x, p, positions, norm_g, ffn_w_in, ffn_w_out, ple_w_proj, ple_w_gate, rel_bias, mla_w_a, mla_q_norm, mla_kv_norm, mla_w_uq, mla_w_ukv, mla_w_o, dil_w_qkv, dil_w_o, fox_w_qkvf, fox_b_f, fox_w_o, loss_target, m_norm_g, m_ffn_w_in, m_ffn_w_out, m_ple_w_proj, m_ple_w_gate, m_rel_bias, m_mla_w_a, m_mla_q_norm, m_mla_kv_norm, m_mla_w_uq, m_mla_w_ukv, m_mla_w_o, m_dil_w_qkv, m_dil_w_o, m_fox_w_qkvf, m_fox_b_f, m_fox_w_o, v_norm_g, v_ffn_w_in, v_ffn_w_out, v_ple_w_proj, v_ple_w_gate, v_rel_bias, v_mla_w_a, v_mla_q_norm, v_mla_kv_norm, v_mla_w_uq, v_mla_w_ukv, v_mla_w_o, v_dil_w_qkv, v_dil_w_o, v_fox_w_qkvf, v_fox_b_f, v_fox_w_o):
    given = dict(locals())
    me = 4 * lax.axis_index("x") + 2 * lax.axis_index("y") + lax.axis_index("c")

    by_rows = [axis == 1 for _, _, axis in BIG]
    gathered = all_gather([given[n].astype(BF16) for n, _, _ in BIG], by_rows, "gather_weights")
    full = {}
    for (n, shp, axis), g in zip(BIG, gathered):
        full[n] = g.reshape(shp[0], N_DEV * shp[1], shp[2]) if axis == 1 else g
    mla = [_mla_layout(full["mla_w_a"], full["mla_w_uq"], full["mla_w_ukv"], j) for j in range(2)]
    fox_w = jnp.pad(_cat([full["fox_w_qkvf"][dev, 0] for dev in range(N_DEV)]), ((0, 0), (0, FOX_W - 3088)))

    gain_rows = _rows(sum(int(np.prod(s)) for _, s, _ in SMALL_SHARDED))
    gains, = all_gather([_pack([given[n] for n, _, _ in SMALL_SHARDED], gain_rows, F32)], [False], "gather_gains",
                        in_vmem=True)
    gains = gains.reshape(N_DEV, gain_rows * LANES)
    off = 0
    for n, shp, axis in SMALL_SHARDED:
        cnt = int(np.prod(shp))
        g = jnp.moveaxis(gains[:, off:off + cnt].reshape((N_DEV,) + shp), 0, axis)
        full[n] = g.reshape(shp[:axis] + (N_DEV * shp[axis],))
        off += cnt

    cos, sin = rope_tables(positions.reshape(S, 1), "rope_tables")
    dil_bias = [mm(rel_bias[:, HEADS * g:HEADS * (g + 1)], jnp.asarray(_bucket_onehot(DIL[g][1])), "tn",
                   f"dil_bias{g}", precise=True, tn=4096).reshape(HEADS, QBLK, 2 * QBLK) for g in range(3)]
    aux = {"cos": cos, "sin": sin, "dil_bias": dil_bias,
           "fox_b": jnp.pad(fox_b_f, ((0, 0), (0, LANES - HEADS)))}

    def layer_weights(i):
        kind, j = i % 3, i // 3
        W = {"g": [full["norm_g"][i, r][None, :] for r in range(4)], "w_in": Dev(full["ffn_w_in"], i),
             "w_out": Lay(full["ffn_w_out"], i), "w_proj": Dev(full["ple_w_proj"], i),
             "w_gate": Lay(full["ple_w_gate"], i)}
        if kind == 0:
            W.update(w_a=mla[j][0], w_uq=mla[j][1], w_ukv=mla[j][2], w_o=Lay(full["mla_w_o"], j),
                     q_norm=full["mla_q_norm"][j][None, :], kv_norm=full["mla_kv_norm"][j][None, :])
        elif kind == 1:
            W.update(w_qkv=Dev(full["dil_w_qkv"], j), w_o=Lay(full["dil_w_o"], j))
        else:
            W.update(w_qkvf=fox_w, w_o=Lay(full["fox_w_o"], j))
        return kind, j, W

    h = x[0]
    saved = []
    for i in range(DEPTH):
        kind, j, W = layer_weights(i)
        t = f"l{i}"
        hn = rms_fwd(h, W["g"][0], f"{t}_n0")
        y, mix = _mixer_fwd(kind, f"{t}_mix", hn, W, aux)
        h1 = rms_fwd(y, W["g"][1], f"{t}_n1", res=h)
        fin = rms_fwd(h1, W["g"][2], f"{t}_n2")
        gu = mm(fin, W["w_in"], "nn", f"{t}_ffn_in", out_dev=FF_W, tn=FF_W)
        act = swiglu_fwd(gu, f"{t}_swiglu")
        f = mm(Dev(act, 0), W["w_out"], "nn", f"{t}_ffn_out", tk=FF_W)
        h2 = rms_fwd(f, W["g"][3], f"{t}_n3", res=h1)
        pp = mm(p[i, 0], W["w_proj"], "nn", f"{t}_ple_p", tn=LANES, tk=256)
        gt = mm(h2, W["w_gate"], "nn", f"{t}_ple_g")
        h3 = ple_fwd(h2, pp, gt, f"{t}_ple")
        saved.append((h, hn, y, h1, fin, gu, act, f, h2, pp, gt, mix))
        h = h3

    dh, loss_lanes = loss_head(h, loss_target[0], "loss_head")

    grads = {n: None for n, _, _ in BIG}
    mla_g = [None, None]
    g_norm = [[None] * 4 for _ in range(DEPTH)]
    g_qn, g_kvn = [None, None], [None, None]
    g_rel, g_bf = None, None
    for i in reversed(range(DEPTH)):
        kind, j, W = layer_weights(i)
        t = f"l{i}b"
        h0, hn, y, h1, fin, gu, act, f, h2, pp, gt, mix = saved[i]
        dpp, dgt = ple_bwd(dh, pp, gt, f"{t}_ple")
        grads["ple_w_proj"] = mm(p[i, 0], dpp, "tn", f"{t}_dwp", out_dtype=BF16, out_dev=LANES, tm=256, tn=LANES,
                                 stack=(grads["ple_w_proj"], DEPTH, i))
        grads["ple_w_gate"] = mm(h2, dgt, "tn", f"{t}_dwg", out_dtype=BF16, stack=(grads["ple_w_gate"], DEPTH, i))
        dh2 = mm(dgt, W["w_gate"], "nt", f"{t}_dh2", add=dh)
        df, g_norm[i][3] = rms_bwd(f, W["g"][3], dh2, f"{t}_n3")
        grads["ffn_w_out"] = mm(Dev(act, 0), df, "tn", f"{t}_dwout", out_dtype=BF16, tm=FF_W,
                                stack=(grads["ffn_w_out"], DEPTH, i))
        dact = mm(df, W["w_out"], "nt", f"{t}_dact", out_dev=FF_W, tn=FF_W)
        dgu = swiglu_bwd(gu, dact, f"{t}_swiglu")
        grads["ffn_w_in"] = mm(fin, Dev(dgu, 0), "tn", f"{t}_dwin", out_dtype=BF16, out_dev=FF_W, tn=FF_W,
                               stack=(grads["ffn_w_in"], DEPTH, i))
        dfin = mm(Dev(dgu, 0), W["w_in"], "nt", f"{t}_dfin", tk=FF_W)
        dh1, g_norm[i][2] = rms_bwd(h1, W["g"][2], dfin, f"{t}_n2", res=dh2)
        dy, g_norm[i][1] = rms_bwd(y, W["g"][1], dh1, f"{t}_n1")
        dhn, gr = _mixer_bwd(kind, f"{t}_mix", hn, dy, W, aux, mix)
        dh, g_norm[i][0] = rms_bwd(h0, W["g"][0], dhn, f"{t}_n0", res=dh1)
        if kind == 0:
            mla_g[j] = _mla_unlayout(gr["w_a"], gr["w_uq"], gr["w_ukv"]) + (gr["w_o"],)
            g_qn[j], g_kvn[j] = gr["q_norm"], gr["kv_norm"]
        elif kind == 1:
            grads["dil_w_qkv"], grads["dil_w_o"] = gr["w_qkv"], gr["w_o"][None]
            g_rel = jnp.concatenate(
                [mm(jnp.asarray(_bucket_onehot(DIL[g][1])), gr["dil_dbias"][g].reshape(HEADS, -1), "nt",
                    f"{t}_drel{g}", precise=True, tk=4096) for g in range(3)], axis=1)
        else:
            wide = gr["w_qkvf"]
            grads["fox_w_qkvf"] = jnp.stack([wide[:, 386 * dev:386 * (dev + 1)] for dev in range(N_DEV)])[:, None]
            grads["fox_w_o"] = gr["w_o"][None]
            g_bf = gr["b_f"][:, :HEADS]
    grad_x = dh[None]
    grads["mla_w_a"] = jnp.stack([mla_g[0][0], mla_g[1][0]])
    grads["mla_w_uq"] = jnp.stack([mla_g[0][1], mla_g[1][1]], axis=1)
    grads["mla_w_ukv"] = jnp.stack([mla_g[0][2], mla_g[1][2]], axis=1)
    grads["mla_w_o"] = jnp.stack([mla_g[0][3], mla_g[1][3]])

    send = [grads[n].reshape(shp[0], N_DEV, shp[1], shp[2]) if axis == 1 else grads[n] for n, shp, axis in BIG]
    landed = exchange(send, by_rows, "scatter_grads")
    big_out = [adamw(given[n], given["m_" + n], given["v_" + n], part, f"adamw_{n}")
               for (n, _, _), part in zip(BIG, landed)]

    small_full = [jnp.stack([jnp.concatenate(r, axis=0) for r in g_norm]).reshape(-1),
                  jnp.concatenate(g_qn, axis=0).reshape(-1), jnp.concatenate(g_kvn, axis=0).reshape(-1),
                  g_rel.reshape(-1), g_bf.reshape(-1), loss_lanes.reshape(-1)]
    small_n = sum(a.shape[0] for a in small_full)
    small_rows = _rows(small_n)
    parts, = all_gather([_pack(small_full, small_rows, F32)], [False], "gather_small_grads", in_vmem=True)
    tot = _unpack(sum_parts(parts, "sum_small_grads"), [(4, 4, D), (2, Q_RANK), (2, KV_RANK), (32, 48), (1, 16), (LANES,)])
    loss = jnp.sum(tot[5])
    small_g = [lax.dynamic_slice_in_dim(tot[0], me * 128, 128, axis=2), lax.dynamic_slice_in_dim(tot[1], me * 48, 48, axis=1),
               lax.dynamic_slice_in_dim(tot[2], me * 32, 32, axis=1), tot[3], tot[4]]
    small_names = [n for n, _, _ in SMALL_SHARDED] + [n for n, _ in SMALL_REPL]
    small_shapes = [s for _, s, _ in SMALL_SHARDED] + [s for _, s in SMALL_REPL]
    s_rows = _rows(sum(int(np.prod(s)) for s in small_shapes))
    small_out = adamw(_pack([given[n] for n in small_names], s_rows, F32)[None],
                      _pack([given["m_" + n] for n in small_names], s_rows, F32)[None],
                      _pack([given["v_" + n] for n in small_names], s_rows, F32)[None],
                      _pack(small_g, s_rows, F32)[None, None], "adamw_small")
    small_out = [_unpack(o_, small_shapes) for o_ in small_out]

    res = [{}, {}, {}, {}]
    for k in range(4):
        for idx, (n, _, _) in enumerate(BIG):
            res[k][n] = big_out[idx][k]
        for idx, n in enumerate(small_names):
            res[k][n] = small_out[k][idx]
    return (loss, grad_x, *[res[0][n] for n in WEIGHTS], *[res[1][n] for n in WEIGHTS],
            *[res[2][n] for n in WEIGHTS], *[res[3][n] for n in WEIGHTS])
```

```python
import math
from typing import NamedTuple

import numpy as np
import jax
import jax.numpy as jnp
from jax import lax
from jax.experimental import pallas as pl
from jax.experimental.pallas import tpu as pltpu

F32 = jnp.float32
BF16 = jnp.bfloat16
MESH_ID = pl.DeviceIdType.MESH

N_DEV = 8
S = 2048
D = 1024
DEPTH = 4
D_FF = 2816
D_PLE = 256
EPS = 1e-6
NEG = -1e30
LANES = 128
HEADS = 16
PAIRS = 8
Q_RANK = 384
KV_RANK = 256
QBLK = 128
DIL = ((128, 1), (512, 4), (2048, 16))
REL_BUCKETS = 32
FOX_W = 3200
VMEM_LIMIT = 56 * 1024 * 1024

ADAM_LR, ADAM_B1, ADAM_B2, ADAM_EPS, ADAM_WD, ADAM_STEP = 1e-3, 0.9, 0.999, 1e-8, 0.01, 10


BIG = (
    ("ffn_w_in", (4, 1024, 704), 2), ("ffn_w_out", (4, 352, 1024), 1),
    ("ple_w_proj", (4, 256, 128), 2), ("ple_w_gate", (4, 128, 1024), 1),
    ("mla_w_a", (2, 128, 672), 1), ("mla_w_uq", (2, 384, 192), 2),
    ("mla_w_ukv", (2, 256, 256), 2), ("mla_w_o", (2, 128, 1024), 1),
    ("dil_w_qkv", (1, 1024, 1152), 2), ("dil_w_o", (1, 128, 1024), 1),
    ("fox_w_qkvf", (1, 1024, 386), 2), ("fox_w_o", (1, 128, 1024), 1),
)
SMALL_SHARDED = (("norm_g", (4, 4, 128), 2), ("mla_q_norm", (2, 48), 1), ("mla_kv_norm", (2, 32), 1))
SMALL_REPL = (("rel_bias", (32, 48)), ("fox_b_f", (1, 16)))
WEIGHTS = ("norm_g", "ffn_w_in", "ffn_w_out", "ple_w_proj", "ple_w_gate", "rel_bias", "mla_w_a", "mla_q_norm",
           "mla_kv_norm", "mla_w_uq", "mla_w_ukv", "mla_w_o", "dil_w_qkv", "dil_w_o", "fox_w_qkvf", "fox_b_f",
           "fox_w_o")


def _rows(n):
    return -(-n // (8 * LANES)) * 8


def _t5_bucket_np(dist):
    max_exact = REL_BUCKETS // 2
    n = np.maximum(dist.astype(np.float32), np.float32(1.0))
    large = max_exact + (np.log(n / np.float32(max_exact)) / np.float32(math.log(2048 / max_exact))
                         * np.float32(REL_BUCKETS - max_exact)).astype(np.int32)
    large = np.minimum(large, REL_BUCKETS - 1)
    return np.where(dist < max_exact, dist, large)


def _bucket_onehot(dilation):
    i = np.arange(QBLK)[:, None]
    j = np.arange(2 * QBLK)[None, :]
    bucket = _t5_bucket_np(np.clip(QBLK + i - j, 0, None) * dilation).reshape(-1)
    return (np.arange(REL_BUCKETS)[:, None] == bucket[None, :]).astype(np.float32)


def _rope_inv_lanes():
    half = 16
    inv = (np.float32(10000.0) ** (-np.arange(half, dtype=np.float32) / np.float32(half))).astype(np.float32)
    t = np.zeros((1, LANES), np.float32)
    t[0, 0:16] = inv
    t[0, 64:80] = inv
    return t


def _params(sem=None):
    return pltpu.CompilerParams(dimension_semantics=sem, vmem_limit_bytes=VMEM_LIMIT)


def _tile(dim, target):
    if dim <= target or dim % target == 0:
        return min(dim, target)
    t = (target // LANES) * LANES
    while dim % t:
        t -= LANES
    return t


_DIMS = {"nn": (((1,), (0,)), ((), ())), "nt": (((1,), (1,)), ((), ())), "tn": (((0,), (0,)), ((), ()))}


class Lay(NamedTuple):
    arr: jax.Array
    l: int


class Dev(NamedTuple):
    arr: jax.Array
    l: int


def _lshape(op):
    if isinstance(op, Dev):
        g, _, r, w = op.arr.shape
        return r, g * w
    return op.arr.shape[1:] if isinstance(op, Lay) else op.shape


def _op_spec(op, rows_t, cols_t, row_ix, col_ix):
    if isinstance(op, Dev):
        w = op.arr.shape[3]
        assert w % cols_t == 0 and (cols_t % LANES == 0 or cols_t == w), (w, cols_t)
        nb, l = w // cols_t, op.l
        return pl.BlockSpec((1, 1, rows_t, cols_t),
                            lambda i, j, k: (col_ix(i, j, k) // nb, l, row_ix(i, j, k), col_ix(i, j, k) % nb))
    if isinstance(op, Lay):
        l = op.l
        return pl.BlockSpec((1, rows_t, cols_t), lambda i, j, k: (l, row_ix(i, j, k), col_ix(i, j, k)))
    return pl.BlockSpec((rows_t, cols_t), lambda i, j, k: (row_ix(i, j, k), col_ix(i, j, k)))


def _mat(ref):
    return ref[(0,) * (len(ref.shape) - 2)]


def mm(a, b, mode, name, out_dtype=F32, precise=False, add=None, out_dev=None, stack=None, tm=1024, tn=512, tk=512):
    (ar, ac), (br, bc) = _lshape(a), _lshape(b)
    M, K = (ac, ar) if mode == "tn" else (ar, ac)
    N = br if mode == "nt" else bc
    assert K == (bc if mode == "nt" else br)
    tm, tn, tk = _tile(M, tm), _tile(N, tn), _tile(K, tk)
    nk = K // tk
    ix_i, ix_j, ix_k = (lambda i, j, k: i), (lambda i, j, k: j), (lambda i, j, k: k)
    a_spec = _op_spec(a, tk, tm, ix_k, ix_i) if mode == "tn" else _op_spec(a, tm, tk, ix_i, ix_k)
    b_spec = _op_spec(b, tn, tk, ix_j, ix_k) if mode == "nt" else _op_spec(b, tk, tn, ix_k, ix_j)
    buf, n_l, l = stack if stack is not None else (None, 1, 0)
    if out_dev is not None:
        out = Dev(jax.ShapeDtypeStruct((N // out_dev, n_l, M, out_dev), out_dtype), l)
    elif stack is not None:
        out = Lay(jax.ShapeDtypeStruct((n_l, M, N), out_dtype), l)
    else:
        out = jax.ShapeDtypeStruct((M, N), out_dtype)
    o_spec = _op_spec(out, tm, tn, ix_i, ix_j)
    n_in = 3 if add is not None else 2

    def body(*refs):
        a_ref, b_ref = refs[0], refs[1]
        o_ref, acc = refs[-2], refs[-1]
        k = pl.program_id(2)

        @pl.when(k == 0)
        def _():
            acc[...] = jnp.zeros_like(acc)

        if precise:
            acc[...] += lax.dot_general(_mat(a_ref), _mat(b_ref), _DIMS[mode], precision=lax.Precision.HIGHEST,
                                        preferred_element_type=F32)
        else:
            acc[...] += lax.dot_general(_mat(a_ref).astype(BF16), _mat(b_ref).astype(BF16), _DIMS[mode],
                                        preferred_element_type=F32)

        @pl.when(k == nk - 1)
        def _():
            r = acc[...] + refs[2][...] if add is not None else acc[...]
            o_ref[...] = r.astype(o_ref.dtype).reshape(o_ref.shape)

    ins = [getattr(a, "arr", a), getattr(b, "arr", b)] + ([add] if add is not None else [])
    in_specs = [a_spec, b_spec] + ([o_spec] if add is not None else [])
    aliases = {}
    if buf is not None:
        ins.append(buf)
        in_specs.append(pl.BlockSpec(memory_space=pl.ANY))
        aliases = {n_in: 0}
    return pl.pallas_call(
        body, name=name, grid=(M // tm, N // tn, nk), in_specs=in_specs, out_specs=o_spec,
        out_shape=getattr(out, "arr", out), input_output_aliases=aliases,
        scratch_shapes=[pltpu.VMEM((tm, tn), F32)], compiler_params=_params(("parallel", "parallel", "arbitrary")),
    )(*ins)


def _rows_call(body, name, ins, outs, tr=256, acc_outs=()):
    n = ins[0].shape[0]
    tr = min(tr, n)
    in_specs = [pl.BlockSpec((tr, a.shape[1]), lambda i: (i, 0)) if a.shape[0] == n else
                pl.BlockSpec(a.shape, lambda i: (0, 0)) for a in ins]
    out_specs = [pl.BlockSpec((tr, w), lambda i: (i, 0)) for w, _ in outs] + \
                [pl.BlockSpec((1, w), lambda i: (0, 0)) for w in acc_outs]
    out_shape = [jax.ShapeDtypeStruct((n, w), dt) for w, dt in outs] + \
                [jax.ShapeDtypeStruct((1, w), F32) for w in acc_outs]
    res = pl.pallas_call(body, name=name, grid=(n // tr,), in_specs=in_specs, out_specs=out_specs,
                         out_shape=out_shape, compiler_params=_params(("arbitrary",)))(*ins)
    return res[0] if len(res) == 1 else res


def _acc(ref, val):
    @pl.when(pl.program_id(0) == 0)
    def _():
        ref[...] = jnp.zeros_like(ref)

    ref[...] += val


def rms_fwd(x, g, name, res=None):
    def body(*refs):
        x_ref, g_ref = refs[0], refs[1]
        o_ref = refs[-1]
        xv = x_ref[...]
        y = xv * lax.rsqrt(jnp.mean(xv * xv, axis=-1, keepdims=True) + EPS) * g_ref[...]
        o_ref[...] = y + refs[2][...] if res is not None else y

    ins = [x, g] + ([res] if res is not None else [])
    return _rows_call(body, name, ins, [(x.shape[1], F32)])


def rms_bwd(x, g, dy, name, res=None):
    def body(*refs):
        x_ref, g_ref, dy_ref = refs[:3]
        dx_ref, dg_ref = refs[-2], refs[-1]
        xv, dyv = x_ref[...], dy_ref[...]
        r = lax.rsqrt(jnp.mean(xv * xv, axis=-1, keepdims=True) + EPS)
        xh = xv * r
        dxh = dyv * g_ref[...]
        dx = r * (dxh - xh * jnp.mean(dxh * xh, axis=-1, keepdims=True))
        dx_ref[...] = dx + refs[3][...] if res is not None else dx
        _acc(dg_ref, jnp.sum(dyv * xh, axis=0, keepdims=True))

    ins = [x, g, dy] + ([res] if res is not None else [])
    return _rows_call(body, name, ins, [(x.shape[1], F32)], acc_outs=(x.shape[1],))


def _sigmoid(x):
    return 1.0 / (1.0 + jnp.exp(-x))


FF_W = 704
FF_TR = 512


def _ff_spec(shift):
    return pl.BlockSpec((1, 1, FF_TR, FF_W), lambda d, i: (d + shift, 0, i, 0))


def swiglu_fwd(gu, name):
    def body(g_ref, u_ref, o_ref):
        gate = g_ref[...]
        o_ref[...] = gate * _sigmoid(gate) * u_ref[...]

    return pl.pallas_call(body, name=name, grid=(4, S // FF_TR), in_specs=[_ff_spec(0), _ff_spec(4)],
                          out_specs=_ff_spec(0), out_shape=jax.ShapeDtypeStruct((4, 1, S, FF_W), F32),
                          compiler_params=_params(("parallel", "parallel")))(gu, gu)


def swiglu_bwd(gu, dact, name):
    def body(g_ref, u_ref, d_ref, o_ref):
        gate, d = g_ref[...], d_ref[...]
        sg = _sigmoid(gate)

        @pl.when(pl.program_id(0) < 4)
        def _():
            o_ref[...] = d * u_ref[...] * sg * (1.0 + gate * (1.0 - sg))

        @pl.when(pl.program_id(0) >= 4)
        def _():
            o_ref[...] = d * gate * sg

    def half(shift):
        return pl.BlockSpec((1, 1, FF_TR, FF_W), lambda d, i: (d % 4 + shift, 0, i, 0))

    return pl.pallas_call(body, name=name, grid=(8, S // FF_TR), in_specs=[half(0), half(4), half(0)],
                          out_specs=_ff_spec(0), out_shape=jax.ShapeDtypeStruct((8, 1, S, FF_W), F32),
                          compiler_params=_params(("parallel", "parallel")))(gu, gu, dact)


def ple_fwd(h, pp, gt, name):
    def body(h_ref, p_ref, g_ref, o_ref):
        o_ref[...] = h_ref[...] + p_ref[...] * _sigmoid(g_ref[...])

    return _rows_call(body, name, [h, pp, gt], [(D, F32)])


def ple_bwd(dh, pp, gt, name):
    def body(d_ref, p_ref, g_ref, dp_ref, dg_ref):
        d, sg = d_ref[...], _sigmoid(g_ref[...])
        dp_ref[...] = d * sg
        dg_ref[...] = d * p_ref[...] * sg * (1.0 - sg)

    return _rows_call(body, name, [dh, pp, gt], [(D, F32), (D, F32)])


def loss_head(y, target, name):
    def body(y_ref, t_ref, d_ref, l_ref):
        e = y_ref[...] - t_ref[...]
        d_ref[...] = e * (1.0 / D)
        col = jnp.sum(e * e, axis=0, keepdims=True) * (0.5 / D)
        _acc(l_ref, sum(col[:, LANES * c:LANES * (c + 1)] for c in range(D // LANES)))

    return _rows_call(body, name, [y, target], [(D, F32)], acc_outs=(LANES,))


def rope_tables(pos_col, name):
    inv = jnp.asarray(_rope_inv_lanes())

    def body(p_ref, inv_ref, c_ref, s_ref):
        ang = p_ref[...].astype(F32) * inv_ref[...]
        lane = lax.broadcasted_iota(jnp.int32, ang.shape, 1)
        first, second = lane < 16, (lane >= 64) & (lane < 80)
        c_ref[...] = jnp.where(first | second, jnp.cos(ang), 1.0)
        sn = jnp.sin(ang)
        s_ref[...] = jnp.where(first, -sn, jnp.where(second, sn, 0.0))

    return _rows_call(body, name, [pos_col, inv], [(LANES, F32), (LANES, F32)])


def _rope(x, c, s):
    return x * c + pltpu.roll(x, 64, axis=1) * s


def _rope_t(d, c, s):
    return d * c + pltpu.roll(d * s, 64, axis=1)


def mla_qk_fwd(qp, kvp, kr, cos, sin, name):
    def body(q_ref, k_ref, kr_ref, c_ref, s_ref, qo_ref, ko_ref):
        c, s = c_ref[...], s_ref[...]
        kr_rot = _rope(kr_ref[...], c, s)
        for h in range(HEADS):
            sl = slice(LANES * h, LANES * (h + 1))
            qo_ref[:, sl] = _rope(q_ref[:, sl], c, s).astype(BF16)
            ko_ref[:, sl] = (k_ref[:, sl] + kr_rot).astype(BF16)

    n = qp.shape[0]
    tr = 256
    w = HEADS * LANES
    return pl.pallas_call(
        body, name=name, grid=(n // tr,),
        in_specs=[pl.BlockSpec((tr, w), lambda i: (i, 0)), pl.BlockSpec((tr, w), lambda i: (i, 0)),
                  pl.BlockSpec((tr, LANES), lambda i: (i, 0)), pl.BlockSpec((tr, LANES), lambda i: (i, 0)),
                  pl.BlockSpec((tr, LANES), lambda i: (i, 0))],
        out_specs=[pl.BlockSpec((tr, w), lambda i: (i, 0))] * 2,
        out_shape=[jax.ShapeDtypeStruct((n, w), BF16)] * 2, compiler_params=_params(("arbitrary",)),
    )(qp, kvp, kr, cos, sin)


def mla_qk_bwd(dq, dk, cos, sin, name):
    def body(dq_ref, dk_ref, c_ref, s_ref, dqp_ref, dkr_ref):
        c, s = c_ref[...], s_ref[...]
        tot = jnp.zeros(c.shape, F32)
        for h in range(HEADS):
            sl = slice(LANES * h, LANES * (h + 1))
            dqp_ref[:, sl] = _rope_t(dq_ref[:, sl], c, s)
            tot = tot + dk_ref[:, sl]
        dkr_ref[...] = _rope_t(tot, c, s)

    return _rows_call(body, name, [dq, dk, cos, sin], [(HEADS * LANES, F32), (LANES, F32)])


TQ = 256


def _pair_masks(shape):
    lane = lax.broadcasted_iota(jnp.int32, shape, 1)
    return (lane < 64, lane >= 64)


def _causal_probs(q_a, k_a, scale, b0, cq, ck):
    s = lax.dot_general(q_a, k_a, _DIMS["nt"], preferred_element_type=F32) * scale
    if cq is not None:
        s = s + (cq - ck)
    row = lax.broadcasted_iota(jnp.int32, s.shape, 0) + b0
    col = lax.broadcasted_iota(jnp.int32, s.shape, 1)
    s = jnp.where(col <= row, s, NEG)
    e = jnp.exp(s - jnp.max(s, axis=-1, keepdims=True))
    return e / jnp.sum(e, axis=-1, keepdims=True)


def attn_fwd(q, k, v, name, *, wide, scale, q_off=0, k_off=0, v_off=0, cum=None, cum_t=None):
    qw = 2 * LANES if wide else LANES
    forget = cum is not None

    def body(*refs):
        q_ref, k_ref, v_ref = refs[:3]
        o_ref = refs[-1]
        m0, m1 = _pair_masks((TQ, LANES))
        for qi in range(S // TQ):
            b0, b1 = qi * TQ, (qi + 1) * TQ
            outs = []
            for a, msk in enumerate((m0, m1)):
                if wide:
                    q_a, k_a = q_ref[b0:b1, LANES * a:LANES * (a + 1)], k_ref[:b1, LANES * a:LANES * (a + 1)]
                else:
                    q_a, k_a = jnp.where(msk, q_ref[b0:b1, :], jnp.zeros((), BF16)), k_ref[:b1, :]
                cq = refs[3][0, b0:b1, a:a + 1] if forget else None
                ck = refs[4][0, a:a + 1, :b1] if forget else None
                p = _causal_probs(q_a, k_a, scale, b0, cq, ck)
                outs.append(jnp.dot(p.astype(BF16), v_ref[:b1, :], preferred_element_type=F32))
            o_ref[b0:b1, :] = jnp.where(m0, outs[0], outs[1])

    in_specs = [pl.BlockSpec((S, qw), lambda h: (0, q_off * LANES // qw + h)),
                pl.BlockSpec((S, qw), lambda h: (0, k_off * LANES // qw + h)),
                pl.BlockSpec((S, LANES), lambda h: (0, v_off + h))]
    ins = [q, k, v]
    if forget:
        in_specs += [pl.BlockSpec((1, S, 2), lambda h: (h, 0, 0)), pl.BlockSpec((1, 2, S), lambda h: (h, 0, 0))]
        ins += [cum, cum_t]
    return pl.pallas_call(
        body, name=name, grid=(PAIRS,), in_specs=in_specs, out_specs=pl.BlockSpec((S, LANES), lambda h: (0, h)),
        out_shape=jax.ShapeDtypeStruct((S, PAIRS * LANES), F32), compiler_params=_params(("arbitrary",)),
    )(*ins)


def attn_bwd(q, k, v, o, do, name, *, wide, scale, q_off=0, k_off=0, v_off=0, cum=None, cum_t=None):
    qw = 2 * LANES if wide else LANES
    forget = cum is not None

    def body(*refs):
        q_ref, k_ref, v_ref, o_ref, do_ref = refs[:5]
        n_out = 5 if forget else 3
        outs = refs[-(n_out + 2):-2]
        dq_ref, dk_ref, dv_ref = outs[:3]
        dk_acc, dv_acc = refs[-2], refs[-1]
        dk_acc[...] = jnp.zeros_like(dk_acc)
        dv_acc[...] = jnp.zeros_like(dv_acc)
        if forget:
            dcq_ref, dck_ref = outs[3], outs[4]
            dck_ref[...] = jnp.zeros_like(dck_ref)
        m0, m1 = _pair_masks((TQ, LANES))
        for qi in range(S // TQ):
            b0, b1 = qi * TQ, (qi + 1) * TQ
            do2 = do_ref[b0:b1, :]
            dd = do2 * o_ref[b0:b1, :]
            do_b = do2.astype(BF16)
            mk0, mk1 = _pair_masks((b1, LANES))
            dqs = []
            for a, (msk, mk) in enumerate(((m0, mk0), (m1, mk1))):
                lanes = slice(LANES * a, LANES * (a + 1)) if wide else slice(0, LANES)
                if wide:
                    q_a, k_a = q_ref[b0:b1, lanes], k_ref[:b1, lanes]
                else:
                    q_a, k_a = jnp.where(msk, q_ref[b0:b1, :], jnp.zeros((), BF16)), k_ref[:b1, :]
                cq = refs[5][0, b0:b1, a:a + 1] if forget else None
                ck = refs[6][0, a:a + 1, :b1] if forget else None
                p = _causal_probs(q_a, k_a, scale, b0, cq, ck)
                dp = lax.dot_general(jnp.where(msk, do_b, jnp.zeros((), BF16)), v_ref[:b1, :], _DIMS["nt"],
                                     preferred_element_type=F32)
                delta = jnp.sum(jnp.where(msk, dd, 0.0), axis=-1, keepdims=True)
                ds = p * (dp - delta)
                if forget:
                    dcq_ref[0, b0:b1, a:a + 1] = jnp.sum(ds, axis=-1, keepdims=True)
                    dck_ref[0, a:a + 1, :b1] -= jnp.sum(ds, axis=0, keepdims=True)
                ds_b = (ds * scale).astype(BF16)
                dqs.append(jnp.dot(ds_b, k_a, preferred_element_type=F32))
                dk_acc[:b1, lanes] += lax.dot_general(ds_b, q_a, _DIMS["tn"], preferred_element_type=F32)
                dv_acc[:b1, :] += jnp.where(mk, lax.dot_general(p.astype(BF16), do_b, _DIMS["tn"],
                                                                 preferred_element_type=F32), 0.0)
            if wide:
                dq_ref[b0:b1, :LANES] = dqs[0]
                dq_ref[b0:b1, LANES:] = dqs[1]
            else:
                dq_ref[b0:b1, :] = jnp.where(m0, dqs[0], dqs[1])
        dk_ref[...] = dk_acc[...]
        dv_ref[...] = dv_acc[...]

    pair = pl.BlockSpec((S, LANES), lambda h: (0, h))
    qk_out = pl.BlockSpec((S, qw), lambda h: (0, h))
    in_specs = [pl.BlockSpec((S, qw), lambda h: (0, q_off * LANES // qw + h)),
                pl.BlockSpec((S, qw), lambda h: (0, k_off * LANES // qw + h)),
                pl.BlockSpec((S, LANES), lambda h: (0, v_off + h)), pair, pair]
    ins = [q, k, v, o, do]
    out_specs = [qk_out, qk_out, pair]
    out_shape = [jax.ShapeDtypeStruct((S, PAIRS * qw), F32)] * 2 + [jax.ShapeDtypeStruct((S, PAIRS * LANES), F32)]
    if forget:
        by_q, by_k = pl.BlockSpec((1, S, 2), lambda h: (h, 0, 0)), pl.BlockSpec((1, 2, S), lambda h: (h, 0, 0))
        in_specs += [by_q, by_k]
        ins += [cum, cum_t]
        out_specs += [by_q, by_k]
        out_shape += [jax.ShapeDtypeStruct((PAIRS, S, 2), F32), jax.ShapeDtypeStruct((PAIRS, 2, S), F32)]
    return pl.pallas_call(
        body, name=name, grid=(PAIRS,), in_specs=in_specs, out_specs=out_specs, out_shape=out_shape,
        scratch_shapes=[pltpu.VMEM((S, qw), F32), pltpu.VMEM((S, LANES), F32)],
        compiler_params=_params(("arbitrary",)),
    )(*ins)


def _tri(lower):
    r = lax.broadcasted_iota(jnp.int32, (QBLK, QBLK), 0)
    c = lax.broadcasted_iota(jnp.int32, (QBLK, QBLK), 1)
    return jnp.where((c <= r) if lower else (c >= r), 1.0, 0.0).astype(F32)


def _hi_dot(a, b):
    return jnp.dot(a, b, precision=lax.Precision.HIGHEST, preferred_element_type=F32)


def fox_gate_fwd(fl, bias, name):
    def body(f_ref, b_ref, o_ref):
        tri = _tri(True)
        carry = jnp.zeros((1, LANES), F32)
        for n in range(S // QBLK):
            x = f_ref[n * QBLK:(n + 1) * QBLK, :].astype(F32) + b_ref[...]
            lf = jnp.minimum(x, 0.0) - jnp.log(1.0 + jnp.exp(-jnp.abs(x)))
            c = _hi_dot(tri, lf) + carry
            o_ref[n * QBLK:(n + 1) * QBLK, :] = c
            carry = c[QBLK - 1:QBLK, :]

    return pl.pallas_call(body, name=name, out_shape=jax.ShapeDtypeStruct((S, LANES), F32),
                          compiler_params=_params())(fl, bias)


def fox_gate_bwd(fl, bias, dcq, dck, name):
    def body(f_ref, b_ref, dq_ref, dk_ref, o_ref, db_ref):
        tri = _tri(False)
        carry = jnp.zeros((1, LANES), F32)
        db = jnp.zeros((1, LANES), F32)
        for n in reversed(range(S // QBLK)):
            rows = slice(n * QBLK, (n + 1) * QBLK)
            dlf = _hi_dot(tri, dq_ref[rows, :] + dk_ref[rows, :]) + carry
            carry = dlf[0:1, :]
            x = f_ref[rows, :].astype(F32) + b_ref[...]
            dx = dlf * (1.0 - _sigmoid(x))
            o_ref[rows, :] = dx
            db = db + jnp.sum(dx, axis=0, keepdims=True)
        db_ref[...] = db

    return pl.pallas_call(body, name=name, out_shape=[jax.ShapeDtypeStruct((S, LANES), F32),
                                                      jax.ShapeDtypeStruct((1, LANES), F32)],
                          compiler_params=_params())(fl, bias, dcq, dck)


def _band_valid(first):
    w = QBLK if first else 2 * QBLK
    i = lax.broadcasted_iota(jnp.int32, (QBLK, w), 0)
    j = lax.broadcasted_iota(jnp.int32, (QBLK, w), 1)
    return (j <= i) if first else ((j >= i) & (j - QBLK <= i))


def _band_logits(q_a, kk, bias, first):
    s = lax.dot_general(q_a, kk, _DIMS["nt"], preferred_element_type=F32) * 0.125 + bias
    return jnp.where(_band_valid(first), s, NEG)


def dil_fwd(qkv, bias, g, name):
    d = DIL[g][1]
    ls = S // d
    view = qkv.reshape(ls, d * 9216)

    def body(q_ref, k_ref, v_ref, b_ref, o_ref, l_ref):
        m0, m1 = _pair_masks((QBLK, LANES))
        for n in range(ls // QBLK):
            rows = slice(n * QBLK, (n + 1) * QBLK)
            keys = rows if n == 0 else slice((n - 1) * QBLK, (n + 1) * QBLK)
            os_, ls_ = [], []
            for a, msk in enumerate((m0, m1)):
                q_a = jnp.where(msk, q_ref[rows, :], jnp.zeros((), BF16))
                bias_a = b_ref[a, :, QBLK:] if n == 0 else b_ref[a]
                s = _band_logits(q_a, k_ref[keys, :], bias_a, n == 0)
                mx = jnp.max(s, axis=-1, keepdims=True)
                e = jnp.exp(s - mx)
                l = jnp.sum(e, axis=-1, keepdims=True)
                os_.append(jnp.dot((e / l).astype(BF16), v_ref[keys, :], preferred_element_type=F32))
                ls_.append(mx + jnp.log(l))
            o_ref[rows, :] = jnp.where(m0, os_[0], os_[1])
            l_ref[rows, :] = jnp.where(m0, ls_[0], ls_[1])

    def col(j):
        return lambda h, r: (0, r * 72 + g * 24 + j * 8 + h)

    out = pl.BlockSpec((ls, LANES), lambda h, r: (0, r * 8 + h))
    o, lse = pl.pallas_call(
        body, name=name, grid=(PAIRS, d),
        in_specs=[pl.BlockSpec((ls, LANES), col(0)), pl.BlockSpec((ls, LANES), col(1)), pl.BlockSpec((ls, LANES), col(2)),
                  pl.BlockSpec((2, QBLK, 2 * QBLK), lambda h, r: (h, 0, 0))],
        out_specs=[out, out], out_shape=[jax.ShapeDtypeStruct((ls, d * D), F32)] * 2,
        compiler_params=_params(("arbitrary", "arbitrary")),
    )(view, view, view, bias)
    return o.reshape(S, D), lse.reshape(S, D)


def dil_merge(os_, lses, name):
    def body(o0, o1, o2, l0, l1, l2, o_ref, l_ref):
        ls_ = [l0[...], l1[...], l2[...]]
        mx = jnp.maximum(jnp.maximum(ls_[0], ls_[1]), ls_[2])
        tot = mx + jnp.log(sum(jnp.exp(l - mx) for l in ls_))
        o_ref[...] = sum(jnp.exp(l - tot) * o[...] for l, o in zip(ls_, (o0, o1, o2)))
        l_ref[...] = tot

    return _rows_call(body, name, list(os_) + list(lses), [(D, F32), (D, F32)])


def dil_bwd(qkv, bias, o, lse, do, g, name):
    d = DIL[g][1]
    ls = S // d
    view = qkv.reshape(ls, d * 9216)
    o, lse, do = (t.reshape(ls, d * D) for t in (o, lse, do))

    def body(q_ref, k_ref, v_ref, b_ref, o_ref, l_ref, do_ref, dq_ref, dk_ref, dv_ref, db_ref, dk_acc, dv_acc):
        @pl.when(pl.program_id(1) == 0)
        def _():
            db_ref[...] = jnp.zeros_like(db_ref)

        dk_acc[...] = jnp.zeros_like(dk_acc)
        dv_acc[...] = jnp.zeros_like(dv_acc)
        m0, m1 = _pair_masks((QBLK, LANES))
        for n in range(ls // QBLK):
            rows = slice(n * QBLK, (n + 1) * QBLK)
            keys = rows if n == 0 else slice((n - 1) * QBLK, (n + 1) * QBLK)
            nk = QBLK if n == 0 else 2 * QBLK
            do2, lse2 = do_ref[rows, :], l_ref[rows, :]
            dd = do2 * o_ref[rows, :]
            do_b = do2.astype(BF16)
            mk0, mk1 = _pair_masks((nk, LANES))
            dqs = []
            for a, (msk, mk) in enumerate(((m0, mk0), (m1, mk1))):
                q_a = jnp.where(msk, q_ref[rows, :], jnp.zeros((), BF16))
                kk = k_ref[keys, :]
                bias_a = b_ref[a, :, QBLK:] if n == 0 else b_ref[a]
                s = _band_logits(q_a, kk, bias_a, n == 0)
                lse_a = jnp.max(jnp.where(msk, lse2, -jnp.inf), axis=-1, keepdims=True)
                p = jnp.exp(s - lse_a)
                dp = lax.dot_general(jnp.where(msk, do_b, jnp.zeros((), BF16)), v_ref[keys, :], _DIMS["nt"],
                                     preferred_element_type=F32)
                delta = jnp.sum(jnp.where(msk, dd, 0.0), axis=-1, keepdims=True)
                ds = p * (dp - delta)
                if n == 0:
                    db_ref[a, :, QBLK:] += ds
                else:
                    db_ref[a] += ds
                ds_b = (ds * 0.125).astype(BF16)
                dqs.append(jnp.dot(ds_b, kk, preferred_element_type=F32))
                dk_acc[keys, :] += lax.dot_general(ds_b, q_a, _DIMS["tn"], preferred_element_type=F32)
                dv_acc[keys, :] += jnp.where(mk, lax.dot_general(p.astype(BF16), do_b, _DIMS["tn"],
                                                                 preferred_element_type=F32), 0.0)
            dq_ref[rows, :] = jnp.where(m0, dqs[0], dqs[1])
        dk_ref[...] = dk_acc[...]
        dv_ref[...] = dv_acc[...]

    def col(j):
        return lambda h, r: (0, r * 72 + g * 24 + j * 8 + h)

    nat = pl.BlockSpec((ls, LANES), lambda h, r: (0, r * 8 + h))
    b_spec = pl.BlockSpec((2, QBLK, 2 * QBLK), lambda h, r: (h, 0, 0))
    dq, dk, dv, db = pl.pallas_call(
        body, name=name, grid=(PAIRS, d),
        in_specs=[pl.BlockSpec((ls, LANES), col(0)), pl.BlockSpec((ls, LANES), col(1)), pl.BlockSpec((ls, LANES), col(2)),
                  b_spec, nat, nat, nat],
        out_specs=[nat, nat, nat, b_spec],
        out_shape=[jax.ShapeDtypeStruct((ls, d * D), F32)] * 3 + [jax.ShapeDtypeStruct((HEADS, QBLK, 2 * QBLK), F32)],
        scratch_shapes=[pltpu.VMEM((ls, LANES), F32), pltpu.VMEM((ls, LANES), F32)],
        compiler_params=_params(("arbitrary", "arbitrary")),
    )(view, view, view, bias, o, lse, do)
    return dq.reshape(S, D), dk.reshape(S, D), dv.reshape(S, D), db


def _place():
    x, y, c = lax.axis_index("x"), lax.axis_index("y"), lax.axis_index("c")
    return x, y, c


def _dev_slot(ref, by_rows, dev):
    return ref.at[:, dev] if by_rows else ref.at[dev]


def all_gather(shards, by_rows, name, in_vmem=False):
    n = len(shards)

    def body(*refs):
        x_refs, out_refs = refs[:n], refs[n:2 * n]
        send_sems, recv_sems, local_sems = refs[2 * n:]
        x, y, c = _place()
        me, sibling = (x, y, c), (x, y, 1 - c)
        chips = [(1 - x, y), (x, 1 - y), (1 - x, 1 - y)]

        def slot(t, px, py, pc):
            return _dev_slot(out_refs[t], by_rows[t], 4 * px + 2 * py + pc)

        def copy(t, k, blk, to, src=None):
            return pltpu.make_async_remote_copy(
                src_ref=slot(t, *blk) if src is None else src, dst_ref=slot(t, *blk), send_sem=send_sems.at[7 * t + k],
                recv_sem=recv_sems.at[7 * t + k], device_id=to, device_id_type=MESH_ID)

        mine = [pltpu.make_async_copy(x_refs[t], slot(t, *me), local_sems.at[t]) for t in range(n)]
        for cp in mine:
            cp.start()
        first = []
        for t in range(n):
            first.append(copy(t, 0, me, sibling, src=x_refs[t]))
            first += [copy(t, 1 + j, me, (*chip, c), src=x_refs[t]) for j, chip in enumerate(chips)]
        for cp in first:
            cp.start()
        passed = []
        for j, chip in enumerate(chips):
            for t in range(n):
                copy(t, 1 + j, (*chip, c), me).wait_recv()
                passed.append(copy(t, 4 + j, (*chip, c), sibling))
                passed[-1].start()
        for t in range(n):
            copy(t, 0, sibling, me).wait_recv()
            for j, chip in enumerate(chips):
                copy(t, 4 + j, (*chip, 1 - c), me).wait_recv()
        for cp in first + passed:
            cp.wait_send()
        for cp in mine:
            cp.wait()

    def gathered(s, rows):
        shp = (s.shape[0], N_DEV) + s.shape[1:] if rows else (N_DEV,) + s.shape
        return jax.ShapeDtypeStruct(shp, s.dtype)

    space = pl.BlockSpec(memory_space=pltpu.VMEM if in_vmem else pl.ANY)
    return pl.pallas_call(
        body, name=name, out_shape=[gathered(s, r) for s, r in zip(shards, by_rows)],
        in_specs=[space] * n, out_specs=[space] * n,
        scratch_shapes=[pltpu.SemaphoreType.DMA((7 * n,)), pltpu.SemaphoreType.DMA((7 * n,)),
                        pltpu.SemaphoreType.DMA((n,))],
        compiler_params=pltpu.CompilerParams(vmem_limit_bytes=VMEM_LIMIT),
    )(*shards)


_HBM = pl.BlockSpec(memory_space=pltpu.HBM)
_SEM = pl.BlockSpec(memory_space=pltpu.SEMAPHORE)
_SPLIT = dict(has_side_effects=pltpu.SideEffectType.DATAFLOW_SIDE_EFFECTING)


def _hbm(a):
    return pltpu.with_memory_space_constraint(a, pltpu.HBM)


def _gathered_shape(s, rows):
    return (s.shape[0], N_DEV) + s.shape[1:] if rows else (N_DEV,) + s.shape


def _first_level(x, y, c):
    return [(x, y, 1 - c), (1 - x, y, c), (x, 1 - y, c), (1 - x, 1 - y, c)]


def gather_start(shards, by_rows, name):
    n = len(shards)

    def body(*refs):
        x_refs, land_refs = refs[:n], refs[n:2 * n]
        send_sems, recv_sems = refs[2 * n], refs[2 * n + 1]
        x, y, c = _place()
        me = 4 * x + 2 * y + c
        for t in range(n):
            for k, peer in enumerate(_first_level(x, y, c)):
                pltpu.make_async_remote_copy(
                    src_ref=x_refs[t], dst_ref=_dev_slot(land_refs[t], by_rows[t], me), send_sem=send_sems.at[4 * t + k],
                    recv_sem=recv_sems.at[4 * t + k], device_id=peer, device_id_type=MESH_ID).start()

    lands = [lax.empty(_gathered_shape(s, r), s.dtype) for s, r in zip(shards, by_rows)]
    sems = pltpu.SemaphoreType.DMA((4 * n,))
    res = pl.pallas_call(
        body, name=name,
        out_shape=(sems, sems) + tuple(pltpu.HBM(a.shape, a.dtype) for a in list(shards) + lands),
        in_specs=[_HBM] * (2 * n), out_specs=(_SEM, _SEM) + (_HBM,) * (2 * n),
        input_output_aliases={i: 2 + i for i in range(2 * n)},
        compiler_params=pltpu.CompilerParams(**_SPLIT),
    )(*[_hbm(a) for a in list(shards) + lands])
    return res[0], res[1], list(res[2:2 + n]), list(res[2 + n:])


def gather_wait(send_sems, recv_sems, first, shards, lands, by_rows, after, name):
    n = len(shards)

    def body(*refs):
        x_refs, land_refs = refs[:n], refs[n:2 * n]
        send_sems, recv_sems = refs[2 * n], refs[2 * n + 1]
        x, y, c = _place()
        for t in range(n):
            for k, (px, py, pc) in enumerate(_first_level(x, y, c)):
                cp = pltpu.make_async_remote_copy(
                    src_ref=x_refs[t], dst_ref=_dev_slot(land_refs[t], by_rows[t], 4 * px + 2 * py + pc),
                    send_sem=send_sems.at[4 * (first + t) + k], recv_sem=recv_sems.at[4 * (first + t) + k],
                    device_id=(px, py, pc), device_id_type=MESH_ID)
                cp.wait_send()
                cp.wait_recv()

    res = pl.pallas_call(
        body, name=name, out_shape=tuple(pltpu.HBM(a.shape, a.dtype) for a in list(shards) + list(lands)),
        in_specs=[_HBM] * (2 * n) + [_SEM, _SEM, pl.BlockSpec(memory_space=pl.ANY)], out_specs=(_HBM,) * (2 * n),
        input_output_aliases={i: i for i in range(2 * n)},
        compiler_params=pltpu.CompilerParams(**_SPLIT),
    )(*shards, *lands, send_sems, recv_sems, after)
    return list(res[n:])


def gather_relay(shards, lands, by_rows, name):
    n = len(shards)

    def body(*refs):
        x_refs, out_refs = refs[:n], refs[2 * n:3 * n]
        send_sems, recv_sems, local_sems = refs[3 * n:]
        x, y, c = _place()
        chips = [(1 - x, y), (x, 1 - y), (1 - x, 1 - y)]

        def copy(t, j, pc):
            blk = _dev_slot(out_refs[t], by_rows[t], 4 * chips[j][0] + 2 * chips[j][1] + pc)
            return pltpu.make_async_remote_copy(src_ref=blk, dst_ref=blk, send_sem=send_sems.at[3 * t + j],
                                                recv_sem=recv_sems.at[3 * t + j], device_id=(x, y, 1 - c),
                                                device_id_type=MESH_ID)

        mine = [pltpu.make_async_copy(x_refs[t], _dev_slot(out_refs[t], by_rows[t], 4 * x + 2 * y + c), local_sems.at[t])
                for t in range(n)]
        sends = [copy(t, j, c) for t in range(n) for j in range(3)]
        for cp in mine + sends:
            cp.start()
        for t in range(n):
            for j in range(3):
                copy(t, j, 1 - c).wait_recv()
        for cp in sends:
            cp.wait_send()
        for cp in mine:
            cp.wait()

    space = pl.BlockSpec(memory_space=pl.ANY)
    return pl.pallas_call(
        body, name=name, out_shape=[jax.ShapeDtypeStruct(a.shape, a.dtype) for a in lands],
        in_specs=[space] * (2 * n), out_specs=[space] * n, input_output_aliases={n + i: i for i in range(n)},
        scratch_shapes=[pltpu.SemaphoreType.DMA((3 * n,)), pltpu.SemaphoreType.DMA((3 * n,)),
                        pltpu.SemaphoreType.DMA((n,))],
        compiler_params=pltpu.CompilerParams(vmem_limit_bytes=VMEM_LIMIT),
    )(*shards, *lands)


def _peers(x, y, c):
    return [(1 - x if k & 4 else x, 1 - y if k & 2 else y, 1 - c if k & 1 else c) for k in range(1, N_DEV)]


def scatter_start(srcs, src_l, lands, land_l, by_rows, name):
    n = len(srcs)

    def body(*refs):
        x_refs, land_refs = refs[:n], refs[n:2 * n]
        send_sems, recv_sems = refs[2 * n], refs[2 * n + 1]
        x, y, c = _place()
        me = 4 * x + 2 * y + c
        for k, (px, py, pc) in enumerate(_peers(x, y, c)):
            for t in range(n):
                blk = _dev_slot(x_refs[t], by_rows[t], 4 * px + 2 * py + pc)
                pltpu.make_async_remote_copy(
                    src_ref=blk.at[src_l[t]], dst_ref=land_refs[t].at[me, land_l[t]], send_sem=send_sems.at[7 * t + k],
                    recv_sem=recv_sems.at[7 * t + k], device_id=(px, py, pc), device_id_type=MESH_ID).start()

    def landed_shape(s, rows):
        return (N_DEV, 1) + s.shape[2:] if rows else (N_DEV, 1) + s.shape[2:]

    lands = [lax.empty(landed_shape(s, r), s.dtype) if ld is None else ld for s, r, ld in zip(srcs, by_rows, lands)]
    sems = pltpu.SemaphoreType.DMA((7 * n,))
    res = pl.pallas_call(
        body, name=name,
        out_shape=(sems, sems) + tuple(pltpu.HBM(a.shape, a.dtype) for a in list(srcs) + lands),
        in_specs=[_HBM] * (2 * n), out_specs=(_SEM, _SEM) + (_HBM,) * (2 * n),
        input_output_aliases={i: 2 + i for i in range(2 * n)},
        compiler_params=pltpu.CompilerParams(**_SPLIT),
    )(*[_hbm(a) for a in list(srcs) + lands])
    return res[0], res[1], list(res[2:2 + n]), list(res[2 + n:])


def scatter_wait(send_sems, recv_sems, srcs, src_l, lands, land_l, by_rows, after, name):
    n = len(srcs)

    def body(*refs):
        x_refs, land_refs = refs[:n], refs[n:2 * n]
        send_sems, recv_sems = refs[2 * n], refs[2 * n + 1]
        x, y, c = _place()
        for k, (px, py, pc) in enumerate(_peers(x, y, c)):
            peer = 4 * px + 2 * py + pc
            for t in range(n):
                cp = pltpu.make_async_remote_copy(
                    src_ref=_dev_slot(x_refs[t], by_rows[t], peer).at[src_l[t]], dst_ref=land_refs[t].at[peer, land_l[t]],
                    send_sem=send_sems.at[7 * t + k], recv_sem=recv_sems.at[7 * t + k], device_id=(px, py, pc),
                    device_id_type=MESH_ID)
                cp.wait_send()
                cp.wait_recv()

    res = pl.pallas_call(
        body, name=name, out_shape=tuple(pltpu.HBM(a.shape, a.dtype) for a in list(srcs) + list(lands)),
        in_specs=[_HBM] * (2 * n) + [_SEM, _SEM, pl.BlockSpec(memory_space=pl.ANY)], out_specs=(_HBM,) * (2 * n),
        input_output_aliases={i: i for i in range(2 * n)},
        compiler_params=pltpu.CompilerParams(**_SPLIT),
    )(*srcs, *lands, send_sems, recv_sems, after)
    return list(res[:n]), list(res[n:])


ADAM_BLOCK_BYTES = 3 << 19


def adamw(w, m, v, parts, name):
    n_parts = parts.shape[0]
    n_l, r, c = w.shape
    lane_c = -(-c // LANES) * LANES
    fits = [t for t in range(16, r, 16) if r % t == 0 and t * lane_c * 4 <= ADAM_BLOCK_BYTES]
    tr = max(fits) if fits and r * lane_c * 4 > ADAM_BLOCK_BYTES else r
    c1 = 1.0 / (1.0 - ADAM_B1 ** ADAM_STEP)
    c2 = 1.0 / (1.0 - ADAM_B2 ** ADAM_STEP)

    def body(w_ref, m_ref, v_ref, p_ref, g_ref, d_ref, nm_ref, nv_ref):
        g = p_ref[0].astype(F32)
        for j in range(1, n_parts):
            g = g + p_ref[j].astype(F32)
        nm = ADAM_B1 * m_ref[...] + (1.0 - ADAM_B1) * g
        nv = ADAM_B2 * v_ref[...] + (1.0 - ADAM_B2) * (g * g)
        g_ref[...] = g
        nm_ref[...] = nm
        nv_ref[...] = nv
        d_ref[...] = -ADAM_LR * ((nm * c1) / (jnp.sqrt(nv * c2) + ADAM_EPS) + ADAM_WD * w_ref[...])

    blk = pl.BlockSpec((1, tr, c), lambda l, i: (l, i, 0))
    return pl.pallas_call(
        body, name=name, grid=(n_l, r // tr),
        in_specs=[blk, blk, blk, pl.BlockSpec((n_parts, 1, tr, c), lambda l, i: (0, l, i, 0))],
        out_specs=[blk] * 4, out_shape=[jax.ShapeDtypeStruct((n_l, r, c), F32)] * 4,
        compiler_params=_params(("parallel", "parallel")),
    )(w, m, v, parts)


def sum_parts(parts, name):
    def body(p_ref, o_ref):
        g = p_ref[0]
        for j in range(1, parts.shape[0]):
            g = g + p_ref[j]
        o_ref[...] = g

    return pl.pallas_call(body, name=name, out_shape=jax.ShapeDtypeStruct(parts.shape[1:], F32),
                          compiler_params=_params())(parts)


def _pack(arrs, rows, dtype):
    flat = jnp.concatenate([a.reshape(-1).astype(dtype) for a in arrs])
    return jnp.pad(flat, (0, rows * LANES - flat.shape[0])).reshape(rows, LANES)


def _unpack(packed, shapes):
    flat, out, off = packed.reshape(-1), [], 0
    for shp in shapes:
        n = int(np.prod(shp))
        out.append(flat[off:off + n].reshape(shp))
        off += n
    return out


def _cat(parts):
    return jnp.concatenate(parts, axis=1)


def _layer_list(i):
    kind, j = i % 3, i // 3
    mix = ([("mla_w_a", j), ("mla_w_uq", j), ("mla_w_ukv", j), ("mla_w_o", j)] if kind == 0 else
           [("dil_w_qkv", 0), ("dil_w_o", 0)] if kind == 1 else [("fox_w_qkvf", 0), ("fox_w_o", 0)])
    return mix + [("ffn_w_in", i), ("ffn_w_out", i), ("ple_w_proj", i), ("ple_w_gate", i)]


def _mla_layout(w_a, g_uq, g_ukv, j):
    def z(r, n):
        return jnp.zeros((r, n), BF16)

    wa = w_a[j]
    a = _cat([wa[:, :640], wa[:, 640:656], z(D, 48), wa[:, 656:672], z(D, 48)])
    q, k, v = [], [], []
    for h in range(HEADS):
        b = g_uq[h // 2, j][:, 96 * (h % 2):96 * (h % 2 + 1)]
        q += [b[:, 64:80], b[:, 0:32], z(Q_RANK, 16), b[:, 80:96], b[:, 32:64], z(Q_RANK, 16)]
        b = g_ukv[h // 2, j][:, LANES * (h % 2):LANES * (h % 2 + 1)]
        k += [z(KV_RANK, 16), b[:, 0:32], z(KV_RANK, 32), b[:, 32:64], z(KV_RANK, 16)]
        v.append(b[:, 64:128])
    return a, _cat(q), _cat(k + v)


def _mla_unlayout(d_a, d_uq, d_ukv):
    a = _cat([d_a[:, :640], d_a[:, 640:656], d_a[:, 704:720]])
    uq, ukv = [], []
    for dev in range(N_DEV):
        q, kv = [], []
        for h in (2 * dev, 2 * dev + 1):
            b = d_uq[:, LANES * h:LANES * (h + 1)]
            q += [b[:, 16:48], b[:, 80:112], b[:, 0:16], b[:, 64:80]]
            b = d_ukv[:, LANES * h:LANES * (h + 1)]
            kv += [b[:, 16:48], b[:, 80:112], d_ukv[:, HEADS * LANES + 64 * h:HEADS * LANES + 64 * (h + 1)]]
        uq.append(_cat(q))
        ukv.append(_cat(kv))
    return a, jnp.stack(uq), jnp.stack(ukv)


def _mixer_fwd(kind, tag, hn, W, aux):
    if kind == 0:
        a = mm(hn, W["w_a"], "nn", f"{tag}_a", tn=768)
        cq = rms_fwd(a[:, :Q_RANK], W["q_norm"], f"{tag}_cq")
        ckv = rms_fwd(a[:, Q_RANK:Q_RANK + KV_RANK], W["kv_norm"], f"{tag}_ckv")
        qp = mm(cq, W["w_uq"], "nn", f"{tag}_uq", tk=384)
        kvp = mm(ckv, W["w_ukv"], "nn", f"{tag}_ukv", tk=256)
        q, k = mla_qk_fwd(qp, kvp, a[:, 640:], aux["cos"], aux["sin"], f"{tag}_qk")
        v = kvp.astype(BF16)
        o = attn_fwd(q, k, v, f"{tag}_attn", wide=True, scale=96 ** -0.5, v_off=HEADS)
        y = mm(o, W["w_o"], "nn", f"{tag}_o")
        return y, (a, cq, ckv, q, k, v, o)
    if kind == 1:
        qkv = mm(hn, W["w_qkv"], "nn", f"{tag}_qkv", out_dtype=BF16, tn=384)
        parts = [dil_fwd(qkv, aux["dil_bias"][g], g, f"{tag}_g{g}") for g in range(3)]
        o, lse = dil_merge([p_[0] for p_ in parts], [p_[1] for p_ in parts], f"{tag}_merge")
        y = mm(o, W["w_o"], "nn", f"{tag}_o")
        return y, (qkv, o, lse)
    a = mm(hn, W["w_qkvf"], "nn", f"{tag}_qkvf", tn=640)
    fl = a[:, 3072:]
    cum = fox_gate_fwd(fl, aux["fox_b"], f"{tag}_gate")[:, :HEADS]
    cum_q = cum.reshape(S, PAIRS, 2).transpose(1, 0, 2)
    cum_k = cum.T.reshape(PAIRS, 2, S)
    ab = a.astype(BF16)
    o = attn_fwd(ab, ab, ab, f"{tag}_attn", wide=False, scale=0.125, k_off=PAIRS, v_off=2 * PAIRS, cum=cum_q, cum_t=cum_k)
    y = mm(o, W["w_o"], "nn", f"{tag}_o")
    return y, (fl, ab, cum_q, cum_k, o)


def _mixer_bwd(kind, tag, hn, dy, W, aux, saved):
    gr = {}
    if kind == 0:
        a, cq, ckv, q, k, v, o = saved
        gr["w_o"] = mm(o, dy, "tn", f"{tag}_dwo", out_dtype=BF16)
        do = mm(dy, W["w_o"], "nt", f"{tag}_do")
        dq, dk, dv = attn_bwd(q, k, v, o, do, f"{tag}_attn_b", wide=True, scale=96 ** -0.5, v_off=HEADS)
        dqp, dkr = mla_qk_bwd(dq, dk, aux["cos"], aux["sin"], f"{tag}_qk_b")
        dkvp = jnp.concatenate([dk, dv], axis=1)
        gr["w_ukv"] = mm(ckv, dkvp, "tn", f"{tag}_dwukv", out_dtype=BF16, tm=256)
        dckv = mm(dkvp, W["w_ukv"], "nt", f"{tag}_dckv", tn=256)
        gr["w_uq"] = mm(cq, dqp, "tn", f"{tag}_dwuq", out_dtype=BF16, tm=384)
        dcq = mm(dqp, W["w_uq"], "nt", f"{tag}_dcq", tn=384)
        da_q, gr["q_norm"] = rms_bwd(a[:, :Q_RANK], W["q_norm"], dcq, f"{tag}_cq_b")
        da_kv, gr["kv_norm"] = rms_bwd(a[:, Q_RANK:Q_RANK + KV_RANK], W["kv_norm"], dckv, f"{tag}_ckv_b")
        da = jnp.concatenate([da_q, da_kv, dkr], axis=1)
        gr["w_a"] = mm(hn, da, "tn", f"{tag}_dwa", out_dtype=BF16, tn=768)
        return mm(da, W["w_a"], "nt", f"{tag}_dhn", tk=768), gr
    if kind == 1:
        qkv, o, lse = saved
        gr["w_o"] = mm(o, dy, "tn", f"{tag}_dwo", out_dtype=BF16)
        do = mm(dy, W["w_o"], "nt", f"{tag}_do")
        cols, dbs = [], []
        for g in range(3):
            dq, dk, dv, db = dil_bwd(qkv, aux["dil_bias"][g], o, lse, do, g, f"{tag}_g{g}_b")
            cols += [dq, dk, dv]
            dbs.append(db)
        dqkv = jnp.concatenate(cols, axis=1)
        gr["dil_dbias"] = dbs
        gr["w_qkv"] = mm(hn, dqkv, "tn", f"{tag}_dwqkv", out_dtype=BF16, out_dev=1152, tn=384)
        return mm(dqkv, W["w_qkv"], "nt", f"{tag}_dhn", tk=384), gr
    fl, ab, cum_q, cum_k, o = saved
    gr["w_o"] = mm(o, dy, "tn", f"{tag}_dwo", out_dtype=BF16)
    do = mm(dy, W["w_o"], "nt", f"{tag}_do")
    dq, dk, dv, dcq, dck = attn_bwd(ab, ab, ab, o, do, f"{tag}_attn_b", wide=False, scale=0.125, k_off=PAIRS,
                                    v_off=2 * PAIRS, cum=cum_q, cum_t=cum_k)
    pad = ((0, 0), (0, LANES - HEADS))
    dcq = jnp.pad(dcq.transpose(1, 0, 2).reshape(S, HEADS), pad)
    dck = jnp.pad(dck.reshape(HEADS, S).T, pad)
    dfl, gr["b_f"] = fox_gate_bwd(fl, aux["fox_b"], dcq, dck, f"{tag}_gate_b")
    da = jnp.concatenate([dq, dk, dv, dfl], axis=1)
    gr["w_qkvf"] = mm(hn, da, "tn", f"{tag}_dwqkvf", out_dtype=BF16, tn=640)
    return mm(da, W["w_qkvf"], "nt", f"{tag}_dhn", tk=640), gr


def kernel(x, p, positions, norm_g, ffn_w_in, ffn_w_out, ple_w_proj, ple_w_gate, rel_bias, mla_w_a, mla_q_norm, mla_kv_norm, mla_w_uq, mla_w_ukv, mla_w_o, dil_w_qkv, dil_w_o, fox_w_qkvf, fox_b_f, fox_w_o, loss_target, m_norm_g, m_ffn_w_in, m_ffn_w_out, m_ple_w_proj, m_ple_w_gate, m_rel_bias, m_mla_w_a, m_mla_q_norm, m_mla_kv_norm, m_mla_w_uq, m_mla_w_ukv, m_mla_w_o, m_dil_w_qkv, m_dil_w_o, m_fox_w_qkvf, m_fox_b_f, m_fox_w_o, v_norm_g, v_ffn_w_in, v_ffn_w_out, v_ple_w_proj, v_ple_w_gate, v_rel_bias, v_mla_w_a, v_mla_q_norm, v_mla_kv_norm, v_mla_w_uq, v_mla_w_ukv, v_mla_w_o, v_dil_w_qkv, v_dil_w_o, v_fox_w_qkvf, v_fox_b_f, v_fox_w_o):
    given = dict(locals())
    me = 4 * lax.axis_index("x") + 2 * lax.axis_index("y") + lax.axis_index("c")

    rows_of = {n: axis == 1 for n, _, axis in BIG}
    shape_of = {n: shp for n, shp, _ in BIG}
    lists = [_layer_list(i) for i in range(DEPTH)]
    flat = [nl for ls in lists for nl in ls]
    flat_rows = [rows_of[n] for n, _ in flat]
    send_s, recv_s, shards, lands = gather_start([given[n][l:l + 1].astype(BF16) for n, l in flat], flat_rows,
                                                 "gather_start")
    full = {}

    gain_rows = _rows(sum(int(np.prod(s)) for _, s, _ in SMALL_SHARDED))
    gains, = all_gather([_pack([given[n] for n, _, _ in SMALL_SHARDED], gain_rows, F32)], [False], "gather_gains",
                        in_vmem=True)
    gains = gains.reshape(N_DEV, gain_rows * LANES)
    off = 0
    for n, shp, axis in SMALL_SHARDED:
        cnt = int(np.prod(shp))
        g = jnp.moveaxis(gains[:, off:off + cnt].reshape((N_DEV,) + shp), 0, axis)
        full[n] = g.reshape(shp[:axis] + (N_DEV * shp[axis],))
        off += cnt

    cos, sin = rope_tables(positions.reshape(S, 1), "rope_tables")
    dil_bias = [mm(rel_bias[:, HEADS * g:HEADS * (g + 1)], jnp.asarray(_bucket_onehot(DIL[g][1])), "tn",
                   f"dil_bias{g}", precise=True, tn=4096).reshape(HEADS, QBLK, 2 * QBLK) for g in range(3)]
    aux = {"cos": cos, "sin": sin, "dil_bias": dil_bias,
           "fox_b": jnp.pad(fox_b_f, ((0, 0), (0, LANES - HEADS)))}

    def layer_weights(i, h_before):
        kind, j = i % 3, i // 3
        sl = slice(sum(len(ls) for ls in lists[:i]), sum(len(ls) for ls in lists[:i + 1]))
        got = gather_wait(send_s, recv_s, sl.start, shards[sl], lands[sl], flat_rows[sl], h_before, f"gather_wait{i}")
        got = gather_relay(shards[sl], got, flat_rows[sl], f"gather_relay{i}")
        w = {n: g.reshape(1, N_DEV * shape_of[n][1], shape_of[n][2]) if rows_of[n] else g
             for (n, _), g in zip(lists[i], got)}
        W = {"g": [full["norm_g"][i, r][None, :] for r in range(4)], "w_in": Dev(w["ffn_w_in"], 0),
             "w_out": Lay(w["ffn_w_out"], 0), "w_proj": Dev(w["ple_w_proj"], 0), "w_gate": Lay(w["ple_w_gate"], 0)}
        if kind == 0:
            w_a, w_uq, w_ukv = _mla_layout(w["mla_w_a"], w["mla_w_uq"], w["mla_w_ukv"], 0)
            W.update(w_a=w_a, w_uq=w_uq, w_ukv=w_ukv, w_o=Lay(w["mla_w_o"], 0),
                     q_norm=full["mla_q_norm"][j][None, :], kv_norm=full["mla_kv_norm"][j][None, :])
        elif kind == 1:
            W.update(w_qkv=Dev(w["dil_w_qkv"], 0), w_o=Lay(w["dil_w_o"], 0))
        else:
            fox_w = jnp.pad(_cat([w["fox_w_qkvf"][dev, 0] for dev in range(N_DEV)]), ((0, 0), (0, FOX_W - 3088)))
            W.update(w_qkvf=fox_w, w_o=Lay(w["fox_w_o"], 0))
        return W

    h = x[0]
    saved, weights = [], []
    for i in range(DEPTH):
        kind, j, W = i % 3, i // 3, layer_weights(i, h)
        weights.append(W)
        t = f"l{i}"
        hn = rms_fwd(h, W["g"][0], f"{t}_n0")
        y, mix = _mixer_fwd(kind, f"{t}_mix", hn, W, aux)
        h1 = rms_fwd(y, W["g"][1], f"{t}_n1", res=h)
        fin = rms_fwd(h1, W["g"][2], f"{t}_n2")
        gu = mm(fin, W["w_in"], "nn", f"{t}_ffn_in", out_dev=FF_W, tn=FF_W)
        act = swiglu_fwd(gu, f"{t}_swiglu")
        f = mm(Dev(act, 0), W["w_out"], "nn", f"{t}_ffn_out", tk=FF_W)
        h2 = rms_fwd(f, W["g"][3], f"{t}_n3", res=h1)
        pp = mm(p[i, 0], W["w_proj"], "nn", f"{t}_ple_p", tn=LANES, tk=256)
        gt = mm(h2, W["w_gate"], "nn", f"{t}_ple_g")
        h3 = ple_fwd(h2, pp, gt, f"{t}_ple")
        saved.append((h, hn, y, h1, fin, gu, act, f, h2, pp, gt, mix))
        h = h3

    dh, loss_lanes = loss_head(h, loss_target[0], "loss_head")

    grads = {n: None for n, _, _ in BIG}
    landed = {n: (lax.empty((N_DEV,) + shp, BF16) if shp[0] > 1 else None) for n, shp, _ in BIG}
    in_flight = []
    g_norm = [[None] * 4 for _ in range(DEPTH)]
    g_qn, g_kvn = [None, None], [None, None]
    g_rel, g_bf = None, None

    def stacked(n):
        g = grads[n]
        return None if g is None else g.reshape(g.shape[0], N_DEV * g.shape[2], g.shape[3])

    def by_device(g):
        return g.reshape(g.shape[0], N_DEV, g.shape[1] // N_DEV, g.shape[2])

    for i in reversed(range(DEPTH)):
        kind, j, W = i % 3, i // 3, weights[i]
        t = f"l{i}b"
        h0, hn, y, h1, fin, gu, act, f, h2, pp, gt, mix = saved[i]
        dpp, dgt = ple_bwd(dh, pp, gt, f"{t}_ple")
        grads["ple_w_proj"] = mm(p[i, 0], dpp, "tn", f"{t}_dwp", out_dtype=BF16, out_dev=LANES, tm=256, tn=LANES,
                                 stack=(grads["ple_w_proj"], DEPTH, i))
        grads["ple_w_gate"] = by_device(mm(h2, dgt, "tn", f"{t}_dwg", out_dtype=BF16,
                                           stack=(stacked("ple_w_gate"), DEPTH, i)))
        dh2 = mm(dgt, W["w_gate"], "nt", f"{t}_dh2", add=dh)
        df, g_norm[i][3] = rms_bwd(f, W["g"][3], dh2, f"{t}_n3")
        grads["ffn_w_out"] = by_device(mm(Dev(act, 0), df, "tn", f"{t}_dwout", out_dtype=BF16, tm=FF_W,
                                          stack=(stacked("ffn_w_out"), DEPTH, i)))
        dact = mm(df, W["w_out"], "nt", f"{t}_dact", out_dev=FF_W, tn=FF_W)
        dgu = swiglu_bwd(gu, dact, f"{t}_swiglu")
        grads["ffn_w_in"] = mm(fin, Dev(dgu, 0), "tn", f"{t}_dwin", out_dtype=BF16, out_dev=FF_W, tn=FF_W,
                               stack=(grads["ffn_w_in"], DEPTH, i))
        dfin = mm(Dev(dgu, 0), W["w_in"], "nt", f"{t}_dfin", tk=FF_W)
        dh1, g_norm[i][2] = rms_bwd(h1, W["g"][2], dfin, f"{t}_n2", res=dh2)
        dy, g_norm[i][1] = rms_bwd(y, W["g"][1], dh1, f"{t}_n1")
        dhn, gr = _mixer_bwd(kind, f"{t}_mix", hn, dy, W, aux, mix)
        dh, g_norm[i][0] = rms_bwd(h0, W["g"][0], dhn, f"{t}_n0", res=dh1)
        if kind == 0:
            d_a, d_uq, d_ukv = _mla_unlayout(gr["w_a"], gr["w_uq"], gr["w_ukv"])
            mine = {"mla_w_a": by_device(d_a[None]), "mla_w_uq": d_uq[:, None], "mla_w_ukv": d_ukv[:, None],
                    "mla_w_o": by_device(gr["w_o"][None])}
            g_qn[j], g_kvn[j] = gr["q_norm"], gr["kv_norm"]
        elif kind == 1:
            mine = {"dil_w_qkv": gr["w_qkv"], "dil_w_o": by_device(gr["w_o"][None])}
            g_rel = jnp.concatenate(
                [mm(jnp.asarray(_bucket_onehot(DIL[g][1])), gr["dil_dbias"][g].reshape(HEADS, -1), "nt",
                    f"{t}_drel{g}", precise=True, tk=4096) for g in range(3)], axis=1)
        else:
            wide = gr["w_qkvf"]
            mine = {"fox_w_qkvf": jnp.stack([wide[:, 386 * dev:386 * (dev + 1)] for dev in range(N_DEV)])[:, None],
                    "fox_w_o": by_device(gr["w_o"][None])}
            g_bf = gr["b_f"][:, :HEADS]
        names = [n for n, _ in lists[i]]
        srcs = [mine[n] if n in mine else grads[n] for n in names]
        src_l = [0 if n in mine else i for n in names]
        land_l = [l if shape_of[n][0] > 1 else 0 for n, l in lists[i]]
        rows = [rows_of[n] for n in names]
        s_sem, r_sem, srcs, got = scatter_start(srcs, src_l, [landed[n] for n in names], land_l, rows, f"scatter_start{i}")
        for n, src, ld in zip(names, srcs, got):
            landed[n] = ld
            if n in mine:
                mine[n] = src
            else:
                grads[n] = src
        in_flight.append((s_sem, r_sem, names, mine, src_l, land_l, rows))
    grad_x = dh[None]

    own = {n: [] for n, _, _ in BIG}
    for s_sem, r_sem, names, mine, src_l, land_l, rows in in_flight:
        srcs = [mine[n] if n in mine else grads[n] for n in names]
        srcs, got = scatter_wait(s_sem, r_sem, srcs, src_l, [landed[n] for n in names], land_l, rows, dh,
                                 f"scatter_wait{len(own['ffn_w_in'])}")
        for n, src, ld, sl, ll, rw in zip(names, srcs, got, src_l, land_l, rows):
            landed[n] = ld
            if n not in mine:
                grads[n] = src
            blk = lax.dynamic_index_in_dim(src, me, axis=1 if rw else 0, keepdims=False)[sl]
            own[n].append((ll, blk))
    big_out = []
    for n, _, _ in BIG:
        part = landed[n]
        for ll, blk in own[n]:
            part = lax.dynamic_update_slice(part, blk[None, None], (me, ll, 0, 0))
        big_out.append(adamw(given[n], given["m_" + n], given["v_" + n], part, f"adamw_{n}"))

    small_full = [jnp.stack([jnp.concatenate(r, axis=0) for r in g_norm]).reshape(-1),
                  jnp.concatenate(g_qn, axis=0).reshape(-1), jnp.concatenate(g_kvn, axis=0).reshape(-1),
                  g_rel.reshape(-1), g_bf.reshape(-1), loss_lanes.reshape(-1)]
    small_n = sum(a.shape[0] for a in small_full)
    small_rows = _rows(small_n)
    parts, = all_gather([_pack(small_full, small_rows, F32)], [False], "gather_small_grads", in_vmem=True)
    tot = _unpack(sum_parts(parts, "sum_small_grads"), [(4, 4, D), (2, Q_RANK), (2, KV_RANK), (32, 48), (1, 16), (LANES,)])
    loss = jnp.sum(tot[5])
    small_g = [lax.dynamic_slice_in_dim(tot[0], me * 128, 128, axis=2), lax.dynamic_slice_in_dim(tot[1], me * 48, 48, axis=1),
               lax.dynamic_slice_in_dim(tot[2], me * 32, 32, axis=1), tot[3], tot[4]]
    small_names = [n for n, _, _ in SMALL_SHARDED] + [n for n, _ in SMALL_REPL]
    small_shapes = [s for _, s, _ in SMALL_SHARDED] + [s for _, s in SMALL_REPL]
    s_rows = _rows(sum(int(np.prod(s)) for s in small_shapes))
    small_out = adamw(_pack([given[n] for n in small_names], s_rows, F32)[None],
                      _pack([given["m_" + n] for n in small_names], s_rows, F32)[None],
                      _pack([given["v_" + n] for n in small_names], s_rows, F32)[None],
                      _pack(small_g, s_rows, F32)[None, None], "adamw_small")
    small_out = [_unpack(o_, small_shapes) for o_ in small_out]

    res = [{}, {}, {}, {}]
    for k in range(4):
        for idx, (n, _, _) in enumerate(BIG):
            res[k][n] = big_out[idx][k]
        for idx, n in enumerate(small_names):
            res[k][n] = small_out[k][idx]
    return (loss, grad_x, *[res[0][n] for n in WEIGHTS], *[res[1][n] for n in WEIGHTS],
            *[res[2][n] for n in WEIGHTS], *[res[3][n] for n in WEIGHTS])
```

```python
import math
from typing import NamedTuple

import numpy as np
import jax
import jax.numpy as jnp
from jax import lax
from jax.experimental import pallas as pl
from jax.experimental.pallas import tpu as pltpu

F32 = jnp.float32
BF16 = jnp.bfloat16
MESH_ID = pl.DeviceIdType.MESH

N_DEV = 8
S = 2048
D = 1024
DEPTH = 4
D_FF = 2816
D_PLE = 256
EPS = 1e-6
NEG = -1e30
LANES = 128
HEADS = 16
PAIRS = 8
Q_RANK = 384
KV_RANK = 256
QBLK = 128
DIL = ((128, 1), (512, 4), (2048, 16))
REL_BUCKETS = 32
FOX_W = 3200
VMEM_LIMIT = 56 * 1024 * 1024

ADAM_LR, ADAM_B1, ADAM_B2, ADAM_EPS, ADAM_WD, ADAM_STEP = 1e-3, 0.9, 0.999, 1e-8, 0.01, 10


BIG = (
    ("ffn_w_in", (4, 1024, 704), 2), ("ffn_w_out", (4, 352, 1024), 1),
    ("ple_w_proj", (4, 256, 128), 2), ("ple_w_gate", (4, 128, 1024), 1),
    ("mla_w_a", (2, 128, 672), 1), ("mla_w_uq", (2, 384, 192), 2),
    ("mla_w_ukv", (2, 256, 256), 2), ("mla_w_o", (2, 128, 1024), 1),
    ("dil_w_qkv", (1, 1024, 1152), 2), ("dil_w_o", (1, 128, 1024), 1),
    ("fox_w_qkvf", (1, 1024, 386), 2), ("fox_w_o", (1, 128, 1024), 1),
)
SMALL_SHARDED = (("norm_g", (4, 4, 128), 2), ("mla_q_norm", (2, 48), 1), ("mla_kv_norm", (2, 32), 1))
SMALL_REPL = (("rel_bias", (32, 48)), ("fox_b_f", (1, 16)))
WEIGHTS = ("norm_g", "ffn_w_in", "ffn_w_out", "ple_w_proj", "ple_w_gate", "rel_bias", "mla_w_a", "mla_q_norm",
           "mla_kv_norm", "mla_w_uq", "mla_w_ukv", "mla_w_o", "dil_w_qkv", "dil_w_o", "fox_w_qkvf", "fox_b_f",
           "fox_w_o")


def _rows(n):
    return -(-n // (8 * LANES)) * 8


def _t5_bucket_np(dist):
    max_exact = REL_BUCKETS // 2
    n = np.maximum(dist.astype(np.float32), np.float32(1.0))
    large = max_exact + (np.log(n / np.float32(max_exact)) / np.float32(math.log(2048 / max_exact))
                         * np.float32(REL_BUCKETS - max_exact)).astype(np.int32)
    large = np.minimum(large, REL_BUCKETS - 1)
    return np.where(dist < max_exact, dist, large)


def _bucket_onehot(dilation):
    i = np.arange(QBLK)[:, None]
    j = np.arange(2 * QBLK)[None, :]
    bucket = _t5_bucket_np(np.clip(QBLK + i - j, 0, None) * dilation).reshape(-1)
    return (np.arange(REL_BUCKETS)[:, None] == bucket[None, :]).astype(np.float32)


def _rope_inv_lanes():
    half = 16
    inv = (np.float32(10000.0) ** (-np.arange(half, dtype=np.float32) / np.float32(half))).astype(np.float32)
    t = np.zeros((1, LANES), np.float32)
    t[0, 0:16] = inv
    t[0, 64:80] = inv
    return t


def _params(sem=None):
    return pltpu.CompilerParams(dimension_semantics=sem, vmem_limit_bytes=VMEM_LIMIT)


def _tile(dim, target):
    if dim <= target or dim % target == 0:
        return min(dim, target)
    t = (target // LANES) * LANES
    while dim % t:
        t -= LANES
    return t


_DIMS = {"nn": (((1,), (0,)), ((), ())), "nt": (((1,), (1,)), ((), ())), "tn": (((0,), (0,)), ((), ()))}


class Lay(NamedTuple):
    arr: jax.Array
    l: int


class Dev(NamedTuple):
    arr: jax.Array
    l: int


def _lshape(op):
    if isinstance(op, Dev):
        g, _, r, w = op.arr.shape
        return r, g * w
    return op.arr.shape[1:] if isinstance(op, Lay) else op.shape


def _op_spec(op, rows_t, cols_t, row_ix, col_ix):
    if isinstance(op, Dev):
        w = op.arr.shape[3]
        assert w % cols_t == 0 and (cols_t % LANES == 0 or cols_t == w), (w, cols_t)
        nb, l = w // cols_t, op.l
        return pl.BlockSpec((1, 1, rows_t, cols_t),
                            lambda i, j, k: (col_ix(i, j, k) // nb, l, row_ix(i, j, k), col_ix(i, j, k) % nb))
    if isinstance(op, Lay):
        l = op.l
        return pl.BlockSpec((1, rows_t, cols_t), lambda i, j, k: (l, row_ix(i, j, k), col_ix(i, j, k)))
    return pl.BlockSpec((rows_t, cols_t), lambda i, j, k: (row_ix(i, j, k), col_ix(i, j, k)))


def _mat(ref):
    return ref[(0,) * (len(ref.shape) - 2)]


def mm(a, b, mode, name, out_dtype=F32, precise=False, add=None, out_dev=None, stack=None, tm=1024, tn=512, tk=2048):
    (ar, ac), (br, bc) = _lshape(a), _lshape(b)
    M, K = (ac, ar) if mode == "tn" else (ar, ac)
    N = br if mode == "nt" else bc
    assert K == (bc if mode == "nt" else br)
    tm, tn, tk = _tile(M, tm), _tile(N, tn), _tile(K, tk)
    nk = K // tk
    ix_i, ix_j, ix_k = (lambda i, j, k: i), (lambda i, j, k: j), (lambda i, j, k: k)
    a_spec = _op_spec(a, tk, tm, ix_k, ix_i) if mode == "tn" else _op_spec(a, tm, tk, ix_i, ix_k)
    b_spec = _op_spec(b, tn, tk, ix_j, ix_k) if mode == "nt" else _op_spec(b, tk, tn, ix_k, ix_j)
    buf, n_l, l = stack if stack is not None else (None, 1, 0)
    if out_dev is not None:
        out = Dev(jax.ShapeDtypeStruct((N // out_dev, n_l, M, out_dev), out_dtype), l)
    elif stack is not None:
        out = Lay(jax.ShapeDtypeStruct((n_l, M, N), out_dtype), l)
    else:
        out = jax.ShapeDtypeStruct((M, N), out_dtype)
    o_spec = _op_spec(out, tm, tn, ix_i, ix_j)
    n_in = 3 if add is not None else 2

    def body(*refs):
        a_ref, b_ref = refs[0], refs[1]
        o_ref = refs[n_in + (buf is not None)]
        if precise:
            part = lax.dot_general(_mat(a_ref), _mat(b_ref), _DIMS[mode], precision=lax.Precision.HIGHEST,
                                   preferred_element_type=F32)
        else:
            part = lax.dot_general(_mat(a_ref).astype(BF16), _mat(b_ref).astype(BF16), _DIMS[mode],
                                   preferred_element_type=F32)

        def finish(r):
            r = r + refs[2][...] if add is not None else r
            o_ref[...] = r.astype(o_ref.dtype).reshape(o_ref.shape)

        if nk == 1:
            finish(part)
            return
        acc, k = refs[-1], pl.program_id(2)

        @pl.when(k == 0)
        def _():
            acc[...] = part

        @pl.when(k > 0)
        def _():
            acc[...] += part

        @pl.when(k == nk - 1)
        def _():
            finish(acc[...])

    ins = [getattr(a, "arr", a), getattr(b, "arr", b)] + ([add] if add is not None else [])
    in_specs = [a_spec, b_spec] + ([o_spec] if add is not None else [])
    aliases = {}
    if buf is not None:
        ins.append(buf)
        in_specs.append(pl.BlockSpec(memory_space=pl.ANY))
        aliases = {n_in: 0}
    return pl.pallas_call(
        body, name=name, grid=(M // tm, N // tn, nk), in_specs=in_specs, out_specs=o_spec,
        out_shape=getattr(out, "arr", out), input_output_aliases=aliases,
        scratch_shapes=[pltpu.VMEM((tm, tn), F32)] if nk > 1 else [],
        compiler_params=_params(("parallel", "parallel", "arbitrary")),
    )(*ins)


def _rows_call(body, name, ins, outs, tr=256, acc_outs=()):
    n = ins[0].shape[0]
    tr = min(tr, n)
    in_specs = [pl.BlockSpec((tr, a.shape[1]), lambda i: (i, 0)) if a.shape[0] == n else
                pl.BlockSpec(a.shape, lambda i: (0, 0)) for a in ins]
    out_specs = [pl.BlockSpec((tr, w), lambda i: (i, 0)) for w, _ in outs] + \
                [pl.BlockSpec((1, w), lambda i: (0, 0)) for w in acc_outs]
    out_shape = [jax.ShapeDtypeStruct((n, w), dt) for w, dt in outs] + \
                [jax.ShapeDtypeStruct((1, w), F32) for w in acc_outs]
    res = pl.pallas_call(body, name=name, grid=(n // tr,), in_specs=in_specs, out_specs=out_specs,
                         out_shape=out_shape, compiler_params=_params(("arbitrary",)))(*ins)
    return res[0] if len(res) == 1 else res


def _acc(ref, val):
    @pl.when(pl.program_id(0) == 0)
    def _():
        ref[...] = jnp.zeros_like(ref)

    ref[...] += val


def rms_fwd(x, g, name, res=None, out_dtype=F32):
    def body(*refs):
        x_ref, g_ref = refs[0], refs[1]
        o_ref = refs[-1]
        xv = x_ref[...]
        y = xv * lax.rsqrt(jnp.mean(xv * xv, axis=-1, keepdims=True) + EPS) * g_ref[...]
        o_ref[...] = (y + refs[2][...] if res is not None else y).astype(o_ref.dtype)

    ins = [x, g] + ([res] if res is not None else [])
    return _rows_call(body, name, ins, [(x.shape[1], out_dtype)])


def rms_bwd(x, g, dy, name, res=None, out_dtype=F32):
    def body(*refs):
        x_ref, g_ref, dy_ref = refs[:3]
        dx_ref, dg_ref = refs[-2], refs[-1]
        xv, dyv = x_ref[...], dy_ref[...]
        r = lax.rsqrt(jnp.mean(xv * xv, axis=-1, keepdims=True) + EPS)
        xh = xv * r
        dxh = dyv * g_ref[...]
        dx = r * (dxh - xh * jnp.mean(dxh * xh, axis=-1, keepdims=True))
        dx_ref[...] = (dx + refs[3][...] if res is not None else dx).astype(dx_ref.dtype)
        _acc(dg_ref, jnp.sum(dyv * xh, axis=0, keepdims=True))

    ins = [x, g, dy] + ([res] if res is not None else [])
    return _rows_call(body, name, ins, [(x.shape[1], out_dtype)], acc_outs=(x.shape[1],))


def _sigmoid(x):
    return 0.5 * jnp.tanh(0.5 * x) + 0.5


FF_W = 704
FF_TR = 512


def _ff_spec(shift):
    return pl.BlockSpec((1, 1, FF_TR, FF_W), lambda d, i: (d + shift, 0, i, 0))


def swiglu_fwd(gu, name):
    def body(g_ref, u_ref, o_ref):
        gate = g_ref[...]
        o_ref[...] = (gate * _sigmoid(gate) * u_ref[...]).astype(BF16)

    return pl.pallas_call(body, name=name, grid=(4, S // FF_TR), in_specs=[_ff_spec(0), _ff_spec(4)],
                          out_specs=_ff_spec(0), out_shape=jax.ShapeDtypeStruct((4, 1, S, FF_W), BF16),
                          compiler_params=_params(("parallel", "parallel")))(gu, gu)


def swiglu_bwd(gu, dact, name):
    def body(g_ref, u_ref, d_ref, o_ref):
        gate, d = g_ref[...], d_ref[...]
        sg = _sigmoid(gate)

        @pl.when(pl.program_id(0) < 4)
        def _():
            o_ref[...] = (d * u_ref[...] * sg * (1.0 + gate * (1.0 - sg))).astype(BF16)

        @pl.when(pl.program_id(0) >= 4)
        def _():
            o_ref[...] = (d * gate * sg).astype(BF16)

    def half(shift):
        return pl.BlockSpec((1, 1, FF_TR, FF_W), lambda d, i: (d % 4 + shift, 0, i, 0))

    return pl.pallas_call(body, name=name, grid=(8, S // FF_TR), in_specs=[half(0), half(4), half(0)],
                          out_specs=_ff_spec(0), out_shape=jax.ShapeDtypeStruct((8, 1, S, FF_W), BF16),
                          compiler_params=_params(("parallel", "parallel")))(gu, gu, dact)


def ple_fwd(h, pp, gt, name):
    def body(h_ref, p_ref, g_ref, o_ref):
        o_ref[...] = h_ref[...] + p_ref[...] * _sigmoid(g_ref[...])

    return _rows_call(body, name, [h, pp, gt], [(D, F32)])


def ple_bwd(dh, pp, gt, name):
    def body(d_ref, p_ref, g_ref, dp_ref, dg_ref):
        d, sg = d_ref[...], _sigmoid(g_ref[...])
        dp_ref[...] = (d * sg).astype(BF16)
        dg_ref[...] = (d * p_ref[...] * sg * (1.0 - sg)).astype(BF16)

    return _rows_call(body, name, [dh, pp, gt], [(D, BF16), (D, BF16)])


def loss_head(y, target, name):
    def body(y_ref, t_ref, d_ref, l_ref):
        e = y_ref[...] - t_ref[...]
        d_ref[...] = e * (1.0 / D)
        col = jnp.sum(e * e, axis=0, keepdims=True) * (0.5 / D)
        _acc(l_ref, sum(col[:, LANES * c:LANES * (c + 1)] for c in range(D // LANES)))

    return _rows_call(body, name, [y, target], [(D, F32)], acc_outs=(LANES,))


def rope_tables(pos_col, name):
    inv = jnp.asarray(_rope_inv_lanes())

    def body(p_ref, inv_ref, c_ref, s_ref):
        ang = p_ref[...].astype(F32) * inv_ref[...]
        lane = lax.broadcasted_iota(jnp.int32, ang.shape, 1)
        first, second = lane < 16, (lane >= 64) & (lane < 80)
        c_ref[...] = jnp.where(first | second, jnp.cos(ang), 1.0)
        sn = jnp.sin(ang)
        s_ref[...] = jnp.where(first, -sn, jnp.where(second, sn, 0.0))

    return _rows_call(body, name, [pos_col, inv], [(LANES, F32), (LANES, F32)])


def _rope(x, c, s):
    return x * c + pltpu.roll(x, 64, axis=1) * s


def _rope_t(d, c, s):
    return d * c + pltpu.roll(d * s, 64, axis=1)


def mla_qk_fwd(qp, kvp, kr, cos, sin, name):
    def body(q_ref, k_ref, kr_ref, c_ref, s_ref, qo_ref, ko_ref):
        c, s = c_ref[...], s_ref[...]
        kr_rot = _rope(kr_ref[...], c, s)
        for h in range(HEADS):
            sl = slice(LANES * h, LANES * (h + 1))
            qo_ref[:, sl] = _rope(q_ref[:, sl], c, s).astype(BF16)
            ko_ref[:, sl] = (k_ref[:, sl] + kr_rot).astype(BF16)

    n = qp.shape[0]
    tr = 256
    w = HEADS * LANES
    return pl.pallas_call(
        body, name=name, grid=(n // tr,),
        in_specs=[pl.BlockSpec((tr, w), lambda i: (i, 0)), pl.BlockSpec((tr, w), lambda i: (i, 0)),
                  pl.BlockSpec((tr, LANES), lambda i: (i, 0)), pl.BlockSpec((tr, LANES), lambda i: (i, 0)),
                  pl.BlockSpec((tr, LANES), lambda i: (i, 0))],
        out_specs=[pl.BlockSpec((tr, w), lambda i: (i, 0))] * 2,
        out_shape=[jax.ShapeDtypeStruct((n, w), BF16)] * 2, compiler_params=_params(("arbitrary",)),
    )(qp, kvp, kr, cos, sin)


def mla_qk_bwd(dq, dk, cos, sin, name):
    def body(dq_ref, dk_ref, c_ref, s_ref, dqp_ref, dkr_ref):
        c, s = c_ref[...], s_ref[...]
        tot = jnp.zeros(c.shape, F32)
        for h in range(HEADS):
            sl = slice(LANES * h, LANES * (h + 1))
            dqp_ref[:, sl] = _rope_t(dq_ref[:, sl], c, s).astype(BF16)
            tot = tot + dk_ref[:, sl]
        dkr_ref[...] = _rope_t(tot, c, s).astype(BF16)

    return _rows_call(body, name, [dq, dk, cos, sin], [(HEADS * LANES, BF16), (LANES, BF16)])


TQ = 256


def _pair_masks(shape):
    lane = lax.broadcasted_iota(jnp.int32, shape, 1)
    return (lane < 64, lane >= 64)


def _causal_probs(q_a, k_a, scale, b0, cq, ck):
    s = lax.dot_general(q_a, k_a, _DIMS["nt"], preferred_element_type=F32) * scale
    if cq is not None:
        s = s + (cq - ck)
    row = lax.broadcasted_iota(jnp.int32, s.shape, 0) + b0
    col = lax.broadcasted_iota(jnp.int32, s.shape, 1)
    s = jnp.where(col <= row, s, NEG)
    e = jnp.exp(s - jnp.max(s, axis=-1, keepdims=True))
    return e / jnp.sum(e, axis=-1, keepdims=True)


def attn_fwd(q, k, v, name, *, wide, scale, q_off=0, k_off=0, v_off=0, cum=None, cum_t=None):
    qw = 2 * LANES if wide else LANES
    forget = cum is not None

    def body(*refs):
        q_ref, k_ref, v_ref = refs[:3]
        o_ref = refs[-1]
        m0, m1 = _pair_masks((TQ, LANES))
        for qi in range(S // TQ):
            b0, b1 = qi * TQ, (qi + 1) * TQ
            outs = []
            for a, msk in enumerate((m0, m1)):
                if wide:
                    q_a, k_a = q_ref[b0:b1, LANES * a:LANES * (a + 1)], k_ref[:b1, LANES * a:LANES * (a + 1)]
                else:
                    q_a, k_a = jnp.where(msk, q_ref[b0:b1, :], jnp.zeros((), BF16)), k_ref[:b1, :]
                cq = refs[3][0, b0:b1, a:a + 1] if forget else None
                ck = refs[4][0, a:a + 1, :b1] if forget else None
                p = _causal_probs(q_a, k_a, scale, b0, cq, ck)
                outs.append(jnp.dot(p.astype(BF16), v_ref[:b1, :], preferred_element_type=F32))
            o_ref[b0:b1, :] = jnp.where(m0, outs[0], outs[1])

    in_specs = [pl.BlockSpec((S, qw), lambda h: (0, q_off * LANES // qw + h)),
                pl.BlockSpec((S, qw), lambda h: (0, k_off * LANES // qw + h)),
                pl.BlockSpec((S, LANES), lambda h: (0, v_off + h))]
    ins = [q, k, v]
    if forget:
        in_specs += [pl.BlockSpec((1, S, 2), lambda h: (h, 0, 0)), pl.BlockSpec((1, 2, S), lambda h: (h, 0, 0))]
        ins += [cum, cum_t]
    return pl.pallas_call(
        body, name=name, grid=(PAIRS,), in_specs=in_specs, out_specs=pl.BlockSpec((S, LANES), lambda h: (0, h)),
        out_shape=jax.ShapeDtypeStruct((S, PAIRS * LANES), F32), compiler_params=_params(("arbitrary",)),
    )(*ins)


def attn_bwd(q, k, v, o, do, name, *, wide, scale, out_dtype=F32, q_off=0, k_off=0, v_off=0, cum=None, cum_t=None):
    qw = 2 * LANES if wide else LANES
    forget = cum is not None

    def body(*refs):
        q_ref, k_ref, v_ref, o_ref, do_ref = refs[:5]
        n_out = 5 if forget else 3
        outs = refs[-(n_out + 2):-2]
        dq_ref, dk_ref, dv_ref = outs[:3]
        dk_acc, dv_acc = refs[-2], refs[-1]
        dk_acc[...] = jnp.zeros_like(dk_acc)
        dv_acc[...] = jnp.zeros_like(dv_acc)
        if forget:
            dcq_ref, dck_ref = outs[3], outs[4]
            dck_ref[...] = jnp.zeros_like(dck_ref)
        m0, m1 = _pair_masks((TQ, LANES))
        for qi in range(S // TQ):
            b0, b1 = qi * TQ, (qi + 1) * TQ
            do2 = do_ref[b0:b1, :]
            dd = do2 * o_ref[b0:b1, :]
            do_b = do2.astype(BF16)
            mk0, mk1 = _pair_masks((b1, LANES))
            dqs = []
            for a, (msk, mk) in enumerate(((m0, mk0), (m1, mk1))):
                lanes = slice(LANES * a, LANES * (a + 1)) if wide else slice(0, LANES)
                if wide:
                    q_a, k_a = q_ref[b0:b1, lanes], k_ref[:b1, lanes]
                else:
                    q_a, k_a = jnp.where(msk, q_ref[b0:b1, :], jnp.zeros((), BF16)), k_ref[:b1, :]
                cq = refs[5][0, b0:b1, a:a + 1] if forget else None
                ck = refs[6][0, a:a + 1, :b1] if forget else None
                p = _causal_probs(q_a, k_a, scale, b0, cq, ck)
                dp = lax.dot_general(jnp.where(msk, do_b, jnp.zeros((), BF16)), v_ref[:b1, :], _DIMS["nt"],
                                     preferred_element_type=F32)
                delta = jnp.sum(jnp.where(msk, dd, 0.0), axis=-1, keepdims=True)
                ds = p * (dp - delta)
                if forget:
                    dcq_ref[0, b0:b1, a:a + 1] = jnp.sum(ds, axis=-1, keepdims=True)
                    dck_ref[0, a:a + 1, :b1] -= jnp.sum(ds, axis=0, keepdims=True)
                ds_b = (ds * scale).astype(BF16)
                dqs.append(jnp.dot(ds_b, k_a, preferred_element_type=F32))
                dk_acc[:b1, lanes] += lax.dot_general(ds_b, q_a, _DIMS["tn"], preferred_element_type=F32)
                dv_acc[:b1, :] += jnp.where(mk, lax.dot_general(p.astype(BF16), do_b, _DIMS["tn"],
                                                                 preferred_element_type=F32), 0.0)
            if wide:
                dq_ref[b0:b1, :LANES] = dqs[0].astype(out_dtype)
                dq_ref[b0:b1, LANES:] = dqs[1].astype(out_dtype)
            else:
                dq_ref[b0:b1, :] = jnp.where(m0, dqs[0], dqs[1]).astype(out_dtype)
        dk_ref[...] = dk_acc[...].astype(out_dtype)
        dv_ref[...] = dv_acc[...].astype(out_dtype)

    pair = pl.BlockSpec((S, LANES), lambda h: (0, h))
    qk_out = pl.BlockSpec((S, qw), lambda h: (0, h))
    in_specs = [pl.BlockSpec((S, qw), lambda h: (0, q_off * LANES // qw + h)),
                pl.BlockSpec((S, qw), lambda h: (0, k_off * LANES // qw + h)),
                pl.BlockSpec((S, LANES), lambda h: (0, v_off + h)), pair, pair]
    ins = [q, k, v, o, do]
    out_specs = [qk_out, qk_out, pair]
    out_shape = [jax.ShapeDtypeStruct((S, PAIRS * qw), out_dtype)] * 2 + \
                [jax.ShapeDtypeStruct((S, PAIRS * LANES), out_dtype)]
    if forget:
        by_q, by_k = pl.BlockSpec((1, S, 2), lambda h: (h, 0, 0)), pl.BlockSpec((1, 2, S), lambda h: (h, 0, 0))
        in_specs += [by_q, by_k]
        ins += [cum, cum_t]
        out_specs += [by_q, by_k]
        out_shape += [jax.ShapeDtypeStruct((PAIRS, S, 2), F32), jax.ShapeDtypeStruct((PAIRS, 2, S), F32)]
    return pl.pallas_call(
        body, name=name, grid=(PAIRS,), in_specs=in_specs, out_specs=out_specs, out_shape=out_shape,
        scratch_shapes=[pltpu.VMEM((S, qw), F32), pltpu.VMEM((S, LANES), F32)],
        compiler_params=_params(("arbitrary",)),
    )(*ins)


def _tri(lower):
    r = lax.broadcasted_iota(jnp.int32, (QBLK, QBLK), 0)
    c = lax.broadcasted_iota(jnp.int32, (QBLK, QBLK), 1)
    return jnp.where((c <= r) if lower else (c >= r), 1.0, 0.0).astype(F32)


def _hi_dot(a, b):
    return jnp.dot(a, b, precision=lax.Precision.HIGHEST, preferred_element_type=F32)


def fox_gate_fwd(fl, bias, name):
    def body(f_ref, b_ref, o_ref):
        tri = _tri(True)
        carry = jnp.zeros((1, LANES), F32)
        for n in range(S // QBLK):
            x = f_ref[n * QBLK:(n + 1) * QBLK, :].astype(F32) + b_ref[...]
            lf = jnp.minimum(x, 0.0) - jnp.log(1.0 + jnp.exp(-jnp.abs(x)))
            c = _hi_dot(tri, lf) + carry
            o_ref[n * QBLK:(n + 1) * QBLK, :] = c
            carry = c[QBLK - 1:QBLK, :]

    return pl.pallas_call(body, name=name, out_shape=jax.ShapeDtypeStruct((S, LANES), F32),
                          compiler_params=_params())(fl, bias)


def fox_gate_bwd(fl, bias, dcq, dck, name):
    def body(f_ref, b_ref, dq_ref, dk_ref, o_ref, db_ref):
        tri = _tri(False)
        carry = jnp.zeros((1, LANES), F32)
        db = jnp.zeros((1, LANES), F32)
        for n in reversed(range(S // QBLK)):
            rows = slice(n * QBLK, (n + 1) * QBLK)
            dlf = _hi_dot(tri, dq_ref[rows, :] + dk_ref[rows, :]) + carry
            carry = dlf[0:1, :]
            x = f_ref[rows, :].astype(F32) + b_ref[...]
            dx = dlf * (1.0 - _sigmoid(x))
            o_ref[rows, :] = dx
            db = db + jnp.sum(dx, axis=0, keepdims=True)
        db_ref[...] = db

    return pl.pallas_call(body, name=name, out_shape=[jax.ShapeDtypeStruct((S, LANES), F32),
                                                      jax.ShapeDtypeStruct((1, LANES), F32)],
                          compiler_params=_params())(fl, bias, dcq, dck)


def _band_valid(first):
    w = QBLK if first else 2 * QBLK
    i = lax.broadcasted_iota(jnp.int32, (QBLK, w), 0)
    j = lax.broadcasted_iota(jnp.int32, (QBLK, w), 1)
    return (j <= i) if first else ((j >= i) & (j - QBLK <= i))


def _band_logits(q_a, kk, bias, first):
    s = lax.dot_general(q_a, kk, _DIMS["nt"], preferred_element_type=F32) * 0.125 + bias
    return jnp.where(_band_valid(first), s, NEG)


def dil_fwd(qkv, bias, g, name):
    d = DIL[g][1]
    ls = S // d
    view = qkv.reshape(ls, d * 9216)

    def body(q_ref, k_ref, v_ref, b_ref, o_ref, l_ref):
        m0, m1 = _pair_masks((QBLK, LANES))
        for n in range(ls // QBLK):
            rows = slice(n * QBLK, (n + 1) * QBLK)
            keys = rows if n == 0 else slice((n - 1) * QBLK, (n + 1) * QBLK)
            os_, ls_ = [], []
            for a, msk in enumerate((m0, m1)):
                q_a = jnp.where(msk, q_ref[rows, :], jnp.zeros((), BF16))
                bias_a = b_ref[a, :, QBLK:] if n == 0 else b_ref[a]
                s = _band_logits(q_a, k_ref[keys, :], bias_a, n == 0)
                mx = jnp.max(s, axis=-1, keepdims=True)
                e = jnp.exp(s - mx)
                l = jnp.sum(e, axis=-1, keepdims=True)
                os_.append(jnp.dot((e / l).astype(BF16), v_ref[keys, :], preferred_element_type=F32))
                ls_.append(mx + jnp.log(l))
            o_ref[rows, :] = jnp.where(m0, os_[0], os_[1])
            l_ref[rows, :] = jnp.where(m0, ls_[0], ls_[1])

    def col(j):
        return lambda h, r: (0, r * 72 + g * 24 + j * 8 + h)

    out = pl.BlockSpec((ls, LANES), lambda h, r: (0, r * 8 + h))
    o, lse = pl.pallas_call(
        body, name=name, grid=(PAIRS, d),
        in_specs=[pl.BlockSpec((ls, LANES), col(0)), pl.BlockSpec((ls, LANES), col(1)), pl.BlockSpec((ls, LANES), col(2)),
                  pl.BlockSpec((2, QBLK, 2 * QBLK), lambda h, r: (h, 0, 0))],
        out_specs=[out, out], out_shape=[jax.ShapeDtypeStruct((ls, d * D), F32)] * 2,
        compiler_params=_params(("arbitrary", "arbitrary")),
    )(view, view, view, bias)
    return o.reshape(S, D), lse.reshape(S, D)


def dil_merge(os_, lses, name):
    def body(o0, o1, o2, l0, l1, l2, o_ref, l_ref):
        ls_ = [l0[...], l1[...], l2[...]]
        mx = jnp.maximum(jnp.maximum(ls_[0], ls_[1]), ls_[2])
        tot = mx + jnp.log(sum(jnp.exp(l - mx) for l in ls_))
        o_ref[...] = sum(jnp.exp(l - tot) * o[...] for l, o in zip(ls_, (o0, o1, o2)))
        l_ref[...] = tot

    return _rows_call(body, name, list(os_) + list(lses), [(D, F32), (D, F32)])


def dil_bwd(qkv, bias, o, lse, do, g, name):
    d = DIL[g][1]
    ls = S // d
    view = qkv.reshape(ls, d * 9216)
    o, lse, do = (t.reshape(ls, d * D) for t in (o, lse, do))

    def body(q_ref, k_ref, v_ref, b_ref, o_ref, l_ref, do_ref, dq_ref, dk_ref, dv_ref, db_ref, dk_acc, dv_acc):
        @pl.when(pl.program_id(1) == 0)
        def _():
            db_ref[...] = jnp.zeros_like(db_ref)

        dk_acc[...] = jnp.zeros_like(dk_acc)
        dv_acc[...] = jnp.zeros_like(dv_acc)
        m0, m1 = _pair_masks((QBLK, LANES))
        for n in range(ls // QBLK):
            rows = slice(n * QBLK, (n + 1) * QBLK)
            keys = rows if n == 0 else slice((n - 1) * QBLK, (n + 1) * QBLK)
            nk = QBLK if n == 0 else 2 * QBLK
            do2, lse2 = do_ref[rows, :], l_ref[rows, :]
            dd = do2 * o_ref[rows, :]
            do_b = do2.astype(BF16)
            mk0, mk1 = _pair_masks((nk, LANES))
            dqs = []
            for a, (msk, mk) in enumerate(((m0, mk0), (m1, mk1))):
                q_a = jnp.where(msk, q_ref[rows, :], jnp.zeros((), BF16))
                kk = k_ref[keys, :]
                bias_a = b_ref[a, :, QBLK:] if n == 0 else b_ref[a]
                s = _band_logits(q_a, kk, bias_a, n == 0)
                lse_a = jnp.max(jnp.where(msk, lse2, -jnp.inf), axis=-1, keepdims=True)
                p = jnp.exp(s - lse_a)
                dp = lax.dot_general(jnp.where(msk, do_b, jnp.zeros((), BF16)), v_ref[keys, :], _DIMS["nt"],
                                     preferred_element_type=F32)
                delta = jnp.sum(jnp.where(msk, dd, 0.0), axis=-1, keepdims=True)
                ds = p * (dp - delta)
                if n == 0:
                    db_ref[a, :, QBLK:] += ds
                else:
                    db_ref[a] += ds
                ds_b = (ds * 0.125).astype(BF16)
                dqs.append(jnp.dot(ds_b, kk, preferred_element_type=F32))
                dk_acc[keys, :] += lax.dot_general(ds_b, q_a, _DIMS["tn"], preferred_element_type=F32)
                dv_acc[keys, :] += jnp.where(mk, lax.dot_general(p.astype(BF16), do_b, _DIMS["tn"],
                                                                 preferred_element_type=F32), 0.0)
            dq_ref[rows, :] = jnp.where(m0, dqs[0], dqs[1]).astype(BF16)
        dk_ref[...] = dk_acc[...].astype(BF16)
        dv_ref[...] = dv_acc[...].astype(BF16)

    def col(j):
        return lambda h, r: (0, r * 72 + g * 24 + j * 8 + h)

    nat = pl.BlockSpec((ls, LANES), lambda h, r: (0, r * 8 + h))
    b_spec = pl.BlockSpec((2, QBLK, 2 * QBLK), lambda h, r: (h, 0, 0))
    dq, dk, dv, db = pl.pallas_call(
        body, name=name, grid=(PAIRS, d),
        in_specs=[pl.BlockSpec((ls, LANES), col(0)), pl.BlockSpec((ls, LANES), col(1)), pl.BlockSpec((ls, LANES), col(2)),
                  b_spec, nat, nat, nat],
        out_specs=[nat, nat, nat, b_spec],
        out_shape=[jax.ShapeDtypeStruct((ls, d * D), BF16)] * 3 + [jax.ShapeDtypeStruct((HEADS, QBLK, 2 * QBLK), F32)],
        scratch_shapes=[pltpu.VMEM((ls, LANES), F32), pltpu.VMEM((ls, LANES), F32)],
        compiler_params=_params(("arbitrary", "arbitrary")),
    )(view, view, view, bias, o, lse, do)
    return dq.reshape(S, D), dk.reshape(S, D), dv.reshape(S, D), db


def _place():
    x, y, c = lax.axis_index("x"), lax.axis_index("y"), lax.axis_index("c")
    return x, y, c


def _dev_slot(ref, by_rows, dev):
    return ref.at[:, dev] if by_rows else ref.at[dev]


def all_gather(shards, by_rows, name, in_vmem=False):
    n = len(shards)

    def body(*refs):
        x_refs, out_refs = refs[:n], refs[n:2 * n]
        send_sems, recv_sems, local_sems = refs[2 * n:]
        x, y, c = _place()
        me, sibling = (x, y, c), (x, y, 1 - c)
        chips = [(1 - x, y), (x, 1 - y), (1 - x, 1 - y)]

        def slot(t, px, py, pc):
            return _dev_slot(out_refs[t], by_rows[t], 4 * px + 2 * py + pc)

        def copy(t, k, blk, to, src=None):
            return pltpu.make_async_remote_copy(
                src_ref=slot(t, *blk) if src is None else src, dst_ref=slot(t, *blk), send_sem=send_sems.at[7 * t + k],
                recv_sem=recv_sems.at[7 * t + k], device_id=to, device_id_type=MESH_ID)

        mine = [pltpu.make_async_copy(x_refs[t], slot(t, *me), local_sems.at[t]) for t in range(n)]
        for cp in mine:
            cp.start()
        first = []
        for t in range(n):
            first.append(copy(t, 0, me, sibling, src=x_refs[t]))
            first += [copy(t, 1 + j, me, (*chip, c), src=x_refs[t]) for j, chip in enumerate(chips)]
        for cp in first:
            cp.start()
        passed = []
        for j, chip in enumerate(chips):
            for t in range(n):
                copy(t, 1 + j, (*chip, c), me).wait_recv()
                passed.append(copy(t, 4 + j, (*chip, c), sibling))
                passed[-1].start()
        for t in range(n):
            copy(t, 0, sibling, me).wait_recv()
            for j, chip in enumerate(chips):
                copy(t, 4 + j, (*chip, 1 - c), me).wait_recv()
        for cp in first + passed:
            cp.wait_send()
        for cp in mine:
            cp.wait()

    def gathered(s, rows):
        shp = (s.shape[0], N_DEV) + s.shape[1:] if rows else (N_DEV,) + s.shape
        return jax.ShapeDtypeStruct(shp, s.dtype)

    space = pl.BlockSpec(memory_space=pltpu.VMEM if in_vmem else pl.ANY)
    return pl.pallas_call(
        body, name=name, out_shape=[gathered(s, r) for s, r in zip(shards, by_rows)],
        in_specs=[space] * n, out_specs=[space] * n,
        scratch_shapes=[pltpu.SemaphoreType.DMA((7 * n,)), pltpu.SemaphoreType.DMA((7 * n,)),
                        pltpu.SemaphoreType.DMA((n,))],
        compiler_params=pltpu.CompilerParams(vmem_limit_bytes=VMEM_LIMIT),
    )(*shards)


_HBM = pl.BlockSpec(memory_space=pltpu.HBM)
_SEM = pl.BlockSpec(memory_space=pltpu.SEMAPHORE)
_SPLIT = dict(has_side_effects=pltpu.SideEffectType.DATAFLOW_SIDE_EFFECTING)


def _hbm(a):
    return pltpu.with_memory_space_constraint(a, pltpu.HBM)


def _gathered_shape(s, rows):
    return (s.shape[0], N_DEV) + s.shape[1:] if rows else (N_DEV,) + s.shape


def _first_level(x, y, c):
    return [(x, y, 1 - c), (1 - x, y, c), (x, 1 - y, c), (1 - x, 1 - y, c)]


def gather_start(shards, by_rows, name):
    n = len(shards)

    def body(*refs):
        x_refs, land_refs = refs[:n], refs[n:2 * n]
        send_sems, recv_sems = refs[2 * n], refs[2 * n + 1]
        x, y, c = _place()
        me = 4 * x + 2 * y + c
        for t in range(n):
            for k, peer in enumerate(_first_level(x, y, c)):
                pltpu.make_async_remote_copy(
                    src_ref=x_refs[t], dst_ref=_dev_slot(land_refs[t], by_rows[t], me), send_sem=send_sems.at[4 * t + k],
                    recv_sem=recv_sems.at[4 * t + k], device_id=peer, device_id_type=MESH_ID).start()

    lands = [lax.empty(_gathered_shape(s, r), s.dtype) for s, r in zip(shards, by_rows)]
    sems = pltpu.SemaphoreType.DMA((4 * n,))
    res = pl.pallas_call(
        body, name=name,
        out_shape=(sems, sems) + tuple(pltpu.HBM(a.shape, a.dtype) for a in list(shards) + lands),
        in_specs=[_HBM] * (2 * n), out_specs=(_SEM, _SEM) + (_HBM,) * (2 * n),
        input_output_aliases={i: 2 + i for i in range(2 * n)},
        compiler_params=pltpu.CompilerParams(**_SPLIT),
    )(*[_hbm(a) for a in list(shards) + lands])
    return res[0], res[1], list(res[2:2 + n]), list(res[2 + n:])


def gather_wait(send_sems, recv_sems, first, shards, lands, by_rows, after, name):
    n = len(shards)

    def body(*refs):
        x_refs, land_refs = refs[:n], refs[n:2 * n]
        send_sems, recv_sems = refs[2 * n], refs[2 * n + 1]
        x, y, c = _place()
        for t in range(n):
            for k, (px, py, pc) in enumerate(_first_level(x, y, c)):
                cp = pltpu.make_async_remote_copy(
                    src_ref=x_refs[t], dst_ref=_dev_slot(land_refs[t], by_rows[t], 4 * px + 2 * py + pc),
                    send_sem=send_sems.at[4 * (first + t) + k], recv_sem=recv_sems.at[4 * (first + t) + k],
                    device_id=(px, py, pc), device_id_type=MESH_ID)
                cp.wait_send()
                cp.wait_recv()

    res = pl.pallas_call(
        body, name=name, out_shape=tuple(pltpu.HBM(a.shape, a.dtype) for a in list(shards) + list(lands)),
        in_specs=[_HBM] * (2 * n) + [_SEM, _SEM, pl.BlockSpec(memory_space=pl.ANY)], out_specs=(_HBM,) * (2 * n),
        input_output_aliases={i: i for i in range(2 * n)},
        compiler_params=pltpu.CompilerParams(**_SPLIT),
    )(*shards, *lands, send_sems, recv_sems, after)
    return list(res[n:])


def gather_relay(shards, lands, by_rows, name):
    n = len(shards)

    def body(*refs):
        x_refs, out_refs = refs[:n], refs[2 * n:3 * n]
        send_sems, recv_sems, local_sems = refs[3 * n:]
        x, y, c = _place()
        chips = [(1 - x, y), (x, 1 - y), (1 - x, 1 - y)]

        def copy(t, j, pc):
            blk = _dev_slot(out_refs[t], by_rows[t], 4 * chips[j][0] + 2 * chips[j][1] + pc)
            return pltpu.make_async_remote_copy(src_ref=blk, dst_ref=blk, send_sem=send_sems.at[3 * t + j],
                                                recv_sem=recv_sems.at[3 * t + j], device_id=(x, y, 1 - c),
                                                device_id_type=MESH_ID)

        mine = [pltpu.make_async_copy(x_refs[t], _dev_slot(out_refs[t], by_rows[t], 4 * x + 2 * y + c), local_sems.at[t])
                for t in range(n)]
        sends = [copy(t, j, c) for t in range(n) for j in range(3)]
        for cp in mine + sends:
            cp.start()
        for t in range(n):
            for j in range(3):
                copy(t, j, 1 - c).wait_recv()
        for cp in sends:
            cp.wait_send()
        for cp in mine:
            cp.wait()

    space = pl.BlockSpec(memory_space=pl.ANY)
    return pl.pallas_call(
        body, name=name, out_shape=[jax.ShapeDtypeStruct(a.shape, a.dtype) for a in lands],
        in_specs=[space] * (2 * n), out_specs=[space] * n, input_output_aliases={n + i: i for i in range(n)},
        scratch_shapes=[pltpu.SemaphoreType.DMA((3 * n,)), pltpu.SemaphoreType.DMA((3 * n,)),
                        pltpu.SemaphoreType.DMA((n,))],
        compiler_params=pltpu.CompilerParams(vmem_limit_bytes=VMEM_LIMIT),
    )(*shards, *lands)


def _peers(x, y, c):
    return [(1 - x if k & 4 else x, 1 - y if k & 2 else y, 1 - c if k & 1 else c) for k in range(1, N_DEV)]


def scatter_start(srcs, src_l, lands, land_l, by_rows, name):
    n = len(srcs)

    def body(*refs):
        x_refs, land_refs = refs[:n], refs[n:2 * n]
        send_sems, recv_sems = refs[2 * n], refs[2 * n + 1]
        x, y, c = _place()
        me = 4 * x + 2 * y + c
        for k, (px, py, pc) in enumerate(_peers(x, y, c)):
            for t in range(n):
                blk = _dev_slot(x_refs[t], by_rows[t], 4 * px + 2 * py + pc)
                pltpu.make_async_remote_copy(
                    src_ref=blk.at[src_l[t]], dst_ref=land_refs[t].at[me, land_l[t]], send_sem=send_sems.at[7 * t + k],
                    recv_sem=recv_sems.at[7 * t + k], device_id=(px, py, pc), device_id_type=MESH_ID).start()

    def landed_shape(s, rows):
        return (N_DEV, 1) + s.shape[2:] if rows else (N_DEV, 1) + s.shape[2:]

    lands = [lax.empty(landed_shape(s, r), s.dtype) if ld is None else ld for s, r, ld in zip(srcs, by_rows, lands)]
    sems = pltpu.SemaphoreType.DMA((7 * n,))
    res = pl.pallas_call(
        body, name=name,
        out_shape=(sems, sems) + tuple(pltpu.HBM(a.shape, a.dtype) for a in list(srcs) + lands),
        in_specs=[_HBM] * (2 * n), out_specs=(_SEM, _SEM) + (_HBM,) * (2 * n),
        input_output_aliases={i: 2 + i for i in range(2 * n)},
        compiler_params=pltpu.CompilerParams(**_SPLIT),
    )(*[_hbm(a) for a in list(srcs) + lands])
    return res[0], res[1], list(res[2:2 + n]), list(res[2 + n:])


def scatter_wait(send_sems, recv_sems, srcs, src_l, lands, land_l, by_rows, after, name):
    n = len(srcs)

    def body(*refs):
        x_refs, land_refs = refs[:n], refs[n:2 * n]
        send_sems, recv_sems = refs[2 * n], refs[2 * n + 1]
        x, y, c = _place()
        for k, (px, py, pc) in enumerate(_peers(x, y, c)):
            peer = 4 * px + 2 * py + pc
            for t in range(n):
                cp = pltpu.make_async_remote_copy(
                    src_ref=_dev_slot(x_refs[t], by_rows[t], peer).at[src_l[t]], dst_ref=land_refs[t].at[peer, land_l[t]],
                    send_sem=send_sems.at[7 * t + k], recv_sem=recv_sems.at[7 * t + k], device_id=(px, py, pc),
                    device_id_type=MESH_ID)
                cp.wait_send()
                cp.wait_recv()

    res = pl.pallas_call(
        body, name=name, out_shape=tuple(pltpu.HBM(a.shape, a.dtype) for a in list(srcs) + list(lands)),
        in_specs=[_HBM] * (2 * n) + [_SEM, _SEM, pl.BlockSpec(memory_space=pl.ANY)], out_specs=(_HBM,) * (2 * n),
        input_output_aliases={i: i for i in range(2 * n)},
        compiler_params=pltpu.CompilerParams(**_SPLIT),
    )(*srcs, *lands, send_sems, recv_sems, after)
    return list(res[:n]), list(res[n:])


ADAM_BLOCK_BYTES = 3 << 19


def adamw(w, m, v, parts, name):
    n_parts = parts.shape[0]
    n_l, r, c = w.shape
    lane_c = -(-c // LANES) * LANES
    fits = [t for t in range(16, r, 16) if r % t == 0 and t * lane_c * 4 <= ADAM_BLOCK_BYTES]
    tr = max(fits) if fits and r * lane_c * 4 > ADAM_BLOCK_BYTES else r
    c1 = 1.0 / (1.0 - ADAM_B1 ** ADAM_STEP)
    c2 = 1.0 / (1.0 - ADAM_B2 ** ADAM_STEP)

    def body(w_ref, m_ref, v_ref, p_ref, g_ref, d_ref, nm_ref, nv_ref):
        g = p_ref[0].astype(F32)
        for j in range(1, n_parts):
            g = g + p_ref[j].astype(F32)
        nm = ADAM_B1 * m_ref[...] + (1.0 - ADAM_B1) * g
        nv = ADAM_B2 * v_ref[...] + (1.0 - ADAM_B2) * (g * g)
        g_ref[...] = g
        nm_ref[...] = nm
        nv_ref[...] = nv
        d_ref[...] = -ADAM_LR * ((nm * c1) / (jnp.sqrt(nv * c2) + ADAM_EPS) + ADAM_WD * w_ref[...])

    blk = pl.BlockSpec((1, tr, c), lambda l, i: (l, i, 0))
    return pl.pallas_call(
        body, name=name, grid=(n_l, r // tr),
        in_specs=[blk, blk, blk, pl.BlockSpec((n_parts, 1, tr, c), lambda l, i: (0, l, i, 0))],
        out_specs=[blk] * 4, out_shape=[jax.ShapeDtypeStruct((n_l, r, c), F32)] * 4,
        compiler_params=_params(("parallel", "parallel")),
    )(w, m, v, parts)


def sum_parts(parts, name):
    def body(p_ref, o_ref):
        g = p_ref[0]
        for j in range(1, parts.shape[0]):
            g = g + p_ref[j]
        o_ref[...] = g

    return pl.pallas_call(body, name=name, out_shape=jax.ShapeDtypeStruct(parts.shape[1:], F32),
                          compiler_params=_params())(parts)


def _pack(arrs, rows, dtype):
    flat = jnp.concatenate([a.reshape(-1).astype(dtype) for a in arrs])
    return jnp.pad(flat, (0, rows * LANES - flat.shape[0])).reshape(rows, LANES)


def _unpack(packed, shapes):
    flat, out, off = packed.reshape(-1), [], 0
    for shp in shapes:
        n = int(np.prod(shp))
        out.append(flat[off:off + n].reshape(shp))
        off += n
    return out


def _cat(parts):
    return jnp.concatenate(parts, axis=1)


def _layer_list(i):
    kind, j = i % 3, i // 3
    mix = ([("mla_w_a", j), ("mla_w_uq", j), ("mla_w_ukv", j), ("mla_w_o", j)] if kind == 0 else
           [("dil_w_qkv", 0), ("dil_w_o", 0)] if kind == 1 else [("fox_w_qkvf", 0), ("fox_w_o", 0)])
    return mix + [("ffn_w_in", i), ("ffn_w_out", i), ("ple_w_proj", i), ("ple_w_gate", i)]


def _mla_layout(w_a, g_uq, g_ukv, j):
    def z(r, n):
        return jnp.zeros((r, n), BF16)

    wa = w_a[j]
    a = _cat([wa[:, :640], wa[:, 640:656], z(D, 48), wa[:, 656:672], z(D, 48)])
    q, k, v = [], [], []
    for h in range(HEADS):
        b = g_uq[h // 2, j][:, 96 * (h % 2):96 * (h % 2 + 1)]
        q += [b[:, 64:80], b[:, 0:32], z(Q_RANK, 16), b[:, 80:96], b[:, 32:64], z(Q_RANK, 16)]
        b = g_ukv[h // 2, j][:, LANES * (h % 2):LANES * (h % 2 + 1)]
        k += [z(KV_RANK, 16), b[:, 0:32], z(KV_RANK, 32), b[:, 32:64], z(KV_RANK, 16)]
        v.append(b[:, 64:128])
    return a, _cat(q), _cat(k + v)


def _mla_unlayout(d_a, d_uq, d_ukv):
    a = _cat([d_a[:, :640], d_a[:, 640:656], d_a[:, 704:720]])
    uq, ukv = [], []
    for dev in range(N_DEV):
        q, kv = [], []
        for h in (2 * dev, 2 * dev + 1):
            b = d_uq[:, LANES * h:LANES * (h + 1)]
            q += [b[:, 16:48], b[:, 80:112], b[:, 0:16], b[:, 64:80]]
            b = d_ukv[:, LANES * h:LANES * (h + 1)]
            kv += [b[:, 16:48], b[:, 80:112], d_ukv[:, HEADS * LANES + 64 * h:HEADS * LANES + 64 * (h + 1)]]
        uq.append(_cat(q))
        ukv.append(_cat(kv))
    return a, jnp.stack(uq), jnp.stack(ukv)


def _mixer_fwd(kind, tag, hn, W, aux):
    if kind == 0:
        a = mm(hn, W["w_a"], "nn", f"{tag}_a", tn=768)
        cq = rms_fwd(a[:, :Q_RANK], W["q_norm"], f"{tag}_cq", out_dtype=BF16)
        ckv = rms_fwd(a[:, Q_RANK:Q_RANK + KV_RANK], W["kv_norm"], f"{tag}_ckv", out_dtype=BF16)
        qp = mm(cq, W["w_uq"], "nn", f"{tag}_uq", tk=384)
        kvp = mm(ckv, W["w_ukv"], "nn", f"{tag}_ukv", tk=256)
        q, k = mla_qk_fwd(qp, kvp, a[:, 640:], aux["cos"], aux["sin"], f"{tag}_qk")
        v = kvp.astype(BF16)
        o = attn_fwd(q, k, v, f"{tag}_attn", wide=True, scale=96 ** -0.5, v_off=HEADS)
        y = mm(o, W["w_o"], "nn", f"{tag}_o")
        return y, (a, cq, ckv, q, k, v, o)
    if kind == 1:
        qkv = mm(hn, W["w_qkv"], "nn", f"{tag}_qkv", out_dtype=BF16, tn=1152)
        parts = [dil_fwd(qkv, aux["dil_bias"][g], g, f"{tag}_g{g}") for g in range(3)]
        o, lse = dil_merge([p_[0] for p_ in parts], [p_[1] for p_ in parts], f"{tag}_merge")
        y = mm(o, W["w_o"], "nn", f"{tag}_o")
        return y, (qkv, o, lse)
    a = mm(hn, W["w_qkvf"], "nn", f"{tag}_qkvf", tn=640)
    fl = a[:, 3072:]
    cum = fox_gate_fwd(fl, aux["fox_b"], f"{tag}_gate")[:, :HEADS]
    cum_q = cum.reshape(S, PAIRS, 2).transpose(1, 0, 2)
    cum_k = cum.T.reshape(PAIRS, 2, S)
    ab = a.astype(BF16)
    o = attn_fwd(ab, ab, ab, f"{tag}_attn", wide=False, scale=0.125, k_off=PAIRS, v_off=2 * PAIRS, cum=cum_q, cum_t=cum_k)
    y = mm(o, W["w_o"], "nn", f"{tag}_o")
    return y, (fl, ab, cum_q, cum_k, o)


def _mixer_bwd(kind, tag, hn, dy, W, aux, saved):
    gr = {}
    if kind == 0:
        a, cq, ckv, q, k, v, o = saved
        gr["w_o"] = mm(o, dy, "tn", f"{tag}_dwo", out_dtype=BF16)
        do = mm(dy, W["w_o"], "nt", f"{tag}_do")
        dq, dk, dv = attn_bwd(q, k, v, o, do, f"{tag}_attn_b", wide=True, scale=96 ** -0.5, v_off=HEADS)
        dqp, dkr = mla_qk_bwd(dq, dk, aux["cos"], aux["sin"], f"{tag}_qk_b")
        dkvp = jnp.concatenate([dk, dv], axis=1)
        gr["w_ukv"] = mm(ckv, dkvp, "tn", f"{tag}_dwukv", out_dtype=BF16, tm=256)
        dckv = mm(dkvp, W["w_ukv"], "nt", f"{tag}_dckv", tn=256)
        gr["w_uq"] = mm(cq, dqp, "tn", f"{tag}_dwuq", out_dtype=BF16, tm=384)
        dcq = mm(dqp, W["w_uq"], "nt", f"{tag}_dcq", tn=384)
        da_q, gr["q_norm"] = rms_bwd(a[:, :Q_RANK], W["q_norm"], dcq, f"{tag}_cq_b", out_dtype=BF16)
        da_kv, gr["kv_norm"] = rms_bwd(a[:, Q_RANK:Q_RANK + KV_RANK], W["kv_norm"], dckv, f"{tag}_ckv_b",
                                       out_dtype=BF16)
        da = jnp.concatenate([da_q, da_kv, dkr], axis=1)
        gr["w_a"] = mm(hn, da, "tn", f"{tag}_dwa", out_dtype=BF16, tn=768)
        return mm(da, W["w_a"], "nt", f"{tag}_dhn", tk=768), gr
    if kind == 1:
        qkv, o, lse = saved
        gr["w_o"] = mm(o, dy, "tn", f"{tag}_dwo", out_dtype=BF16)
        do = mm(dy, W["w_o"], "nt", f"{tag}_do")
        cols, dbs = [], []
        for g in range(3):
            dq, dk, dv, db = dil_bwd(qkv, aux["dil_bias"][g], o, lse, do, g, f"{tag}_g{g}_b")
            cols += [dq, dk, dv]
            dbs.append(db)
        dqkv = jnp.concatenate(cols, axis=1)
        gr["dil_dbias"] = dbs
        gr["w_qkv"] = mm(hn, dqkv, "tn", f"{tag}_dwqkv", out_dtype=BF16, out_dev=1152, tn=1152)
        return mm(dqkv, W["w_qkv"], "nt", f"{tag}_dhn", tk=1152), gr
    fl, ab, cum_q, cum_k, o = saved
    gr["w_o"] = mm(o, dy, "tn", f"{tag}_dwo", out_dtype=BF16)
    do = mm(dy, W["w_o"], "nt", f"{tag}_do")
    dq, dk, dv, dcq, dck = attn_bwd(ab, ab, ab, o, do, f"{tag}_attn_b", wide=False, scale=0.125, out_dtype=BF16,
                                    k_off=PAIRS, v_off=2 * PAIRS, cum=cum_q, cum_t=cum_k)
    pad = ((0, 0), (0, LANES - HEADS))
    dcq = jnp.pad(dcq.transpose(1, 0, 2).reshape(S, HEADS), pad)
    dck = jnp.pad(dck.reshape(HEADS, S).T, pad)
    dfl, gr["b_f"] = fox_gate_bwd(fl, aux["fox_b"], dcq, dck, f"{tag}_gate_b")
    da = jnp.concatenate([dq, dk, dv, dfl.astype(BF16)], axis=1)
    gr["w_qkvf"] = mm(hn, da, "tn", f"{tag}_dwqkvf", out_dtype=BF16, tn=640)
    return mm(da, W["w_qkvf"], "nt", f"{tag}_dhn", tk=640), gr


def kernel(x, p, positions, norm_g, ffn_w_in, ffn_w_out, ple_w_proj, ple_w_gate, rel_bias, mla_w_a, mla_q_norm, mla_kv_norm, mla_w_uq, mla_w_ukv, mla_w_o, dil_w_qkv, dil_w_o, fox_w_qkvf, fox_b_f, fox_w_o, loss_target, m_norm_g, m_ffn_w_in, m_ffn_w_out, m_ple_w_proj, m_ple_w_gate, m_rel_bias, m_mla_w_a, m_mla_q_norm, m_mla_kv_norm, m_mla_w_uq, m_mla_w_ukv, m_mla_w_o, m_dil_w_qkv, m_dil_w_o, m_fox_w_qkvf, m_fox_b_f, m_fox_w_o, v_norm_g, v_ffn_w_in, v_ffn_w_out, v_ple_w_proj, v_ple_w_gate, v_rel_bias, v_mla_w_a, v_mla_q_norm, v_mla_kv_norm, v_mla_w_uq, v_mla_w_ukv, v_mla_w_o, v_dil_w_qkv, v_dil_w_o, v_fox_w_qkvf, v_fox_b_f, v_fox_w_o):
    given = dict(locals())
    me = 4 * lax.axis_index("x") + 2 * lax.axis_index("y") + lax.axis_index("c")

    rows_of = {n: axis == 1 for n, _, axis in BIG}
    shape_of = {n: shp for n, shp, _ in BIG}
    lists = [_layer_list(i) for i in range(DEPTH)]
    flat = [nl for ls in lists for nl in ls]
    flat_rows = [rows_of[n] for n, _ in flat]
    send_s, recv_s, shards, lands = gather_start([given[n][l:l + 1].astype(BF16) for n, l in flat], flat_rows,
                                                 "gather_start")
    full = {}

    gain_rows = _rows(sum(int(np.prod(s)) for _, s, _ in SMALL_SHARDED))
    gains, = all_gather([_pack([given[n] for n, _, _ in SMALL_SHARDED], gain_rows, F32)], [False], "gather_gains",
                        in_vmem=True)
    gains = gains.reshape(N_DEV, gain_rows * LANES)
    off = 0
    for n, shp, axis in SMALL_SHARDED:
        cnt = int(np.prod(shp))
        g = jnp.moveaxis(gains[:, off:off + cnt].reshape((N_DEV,) + shp), 0, axis)
        full[n] = g.reshape(shp[:axis] + (N_DEV * shp[axis],))
        off += cnt

    cos, sin = rope_tables(positions.reshape(S, 1), "rope_tables")
    dil_bias = [mm(rel_bias[:, HEADS * g:HEADS * (g + 1)], jnp.asarray(_bucket_onehot(DIL[g][1])), "tn",
                   f"dil_bias{g}", precise=True, tn=4096).reshape(HEADS, QBLK, 2 * QBLK) for g in range(3)]
    aux = {"cos": cos, "sin": sin, "dil_bias": dil_bias,
           "fox_b": jnp.pad(fox_b_f, ((0, 0), (0, LANES - HEADS)))}

    def layer_weights(i, h_before):
        kind, j = i % 3, i // 3
        sl = slice(sum(len(ls) for ls in lists[:i]), sum(len(ls) for ls in lists[:i + 1]))
        got = gather_wait(send_s, recv_s, sl.start, shards[sl], lands[sl], flat_rows[sl], h_before, f"gather_wait{i}")
        got = gather_relay(shards[sl], got, flat_rows[sl], f"gather_relay{i}")
        w = {n: g.reshape(1, N_DEV * shape_of[n][1], shape_of[n][2]) if rows_of[n] else g
             for (n, _), g in zip(lists[i], got)}
        W = {"g": [full["norm_g"][i, r][None, :] for r in range(4)], "w_in": Dev(w["ffn_w_in"], 0),
             "w_out": Lay(w["ffn_w_out"], 0), "w_proj": Dev(w["ple_w_proj"], 0), "w_gate": Lay(w["ple_w_gate"], 0)}
        if kind == 0:
            w_a, w_uq, w_ukv = _mla_layout(w["mla_w_a"], w["mla_w_uq"], w["mla_w_ukv"], 0)
            W.update(w_a=w_a, w_uq=w_uq, w_ukv=w_ukv, w_o=Lay(w["mla_w_o"], 0),
                     q_norm=full["mla_q_norm"][j][None, :], kv_norm=full["mla_kv_norm"][j][None, :])
        elif kind == 1:
            W.update(w_qkv=Dev(w["dil_w_qkv"], 0), w_o=Lay(w["dil_w_o"], 0))
        else:
            fox_w = jnp.pad(_cat([w["fox_w_qkvf"][dev, 0] for dev in range(N_DEV)]), ((0, 0), (0, FOX_W - 3088)))
            W.update(w_qkvf=fox_w, w_o=Lay(w["fox_w_o"], 0))
        return W

    h = x[0]
    saved, weights = [], []
    for i in range(DEPTH):
        kind, j, W = i % 3, i // 3, layer_weights(i, h)
        weights.append(W)
        t = f"l{i}"
        hn = rms_fwd(h, W["g"][0], f"{t}_n0", out_dtype=BF16)
        y, mix = _mixer_fwd(kind, f"{t}_mix", hn, W, aux)
        h1 = rms_fwd(y, W["g"][1], f"{t}_n1", res=h)
        fin = rms_fwd(h1, W["g"][2], f"{t}_n2", out_dtype=BF16)
        gu = mm(fin, W["w_in"], "nn", f"{t}_ffn_in", out_dev=FF_W, tn=FF_W)
        act = swiglu_fwd(gu, f"{t}_swiglu")
        f = mm(Dev(act, 0), W["w_out"], "nn", f"{t}_ffn_out", tk=FF_W)
        h2 = rms_fwd(f, W["g"][3], f"{t}_n3", res=h1)
        pp = mm(p[i, 0], W["w_proj"], "nn", f"{t}_ple_p", tn=LANES, tk=256)
        gt = mm(h2, W["w_gate"], "nn", f"{t}_ple_g")
        h3 = ple_fwd(h2, pp, gt, f"{t}_ple")
        saved.append((h, hn, y, h1, fin, gu, act, f, h2, pp, gt, mix))
        h = h3

    dh, loss_lanes = loss_head(h, loss_target[0], "loss_head")

    grads = {n: None for n, _, _ in BIG}
    landed = {n: (lax.empty((N_DEV,) + shp, BF16) if shp[0] > 1 else None) for n, shp, _ in BIG}
    in_flight = []
    g_norm = [[None] * 4 for _ in range(DEPTH)]
    g_qn, g_kvn = [None, None], [None, None]
    g_rel, g_bf = None, None

    def stacked(n):
        g = grads[n]
        return None if g is None else g.reshape(g.shape[0], N_DEV * g.shape[2], g.shape[3])

    def by_device(g):
        return g.reshape(g.shape[0], N_DEV, g.shape[1] // N_DEV, g.shape[2])

    def start(i, entries, mine, tag):
        names = [n for n, _ in entries]
        srcs = [mine[n] if n in mine else grads[n] for n in names]
        src_l = [0 if n in mine else i for n in names]
        land_l = [l if shape_of[n][0] > 1 else 0 for n, l in entries]
        rows = [rows_of[n] for n in names]
        s_sem, r_sem, srcs, got = scatter_start(srcs, src_l, [landed[n] for n in names], land_l, rows, tag)
        for n, src, ld in zip(names, srcs, got):
            landed[n] = ld
            if n in mine:
                mine[n] = src
            else:
                grads[n] = src
        in_flight.append((s_sem, r_sem, names, mine, src_l, land_l, rows))

    for i in reversed(range(DEPTH)):
        kind, j, W = i % 3, i // 3, weights[i]
        t = f"l{i}b"
        h0, hn, y, h1, fin, gu, act, f, h2, pp, gt, mix = saved[i]
        dpp, dgt = ple_bwd(dh, pp, gt, f"{t}_ple")
        grads["ple_w_proj"] = mm(p[i, 0], dpp, "tn", f"{t}_dwp", out_dtype=BF16, out_dev=LANES, tm=256, tn=LANES,
                                 stack=(grads["ple_w_proj"], DEPTH, i))
        grads["ple_w_gate"] = by_device(mm(h2, dgt, "tn", f"{t}_dwg", out_dtype=BF16,
                                           stack=(stacked("ple_w_gate"), DEPTH, i)))
        dh2 = mm(dgt, W["w_gate"], "nt", f"{t}_dh2", add=dh)
        df, g_norm[i][3] = rms_bwd(f, W["g"][3], dh2, f"{t}_n3", out_dtype=BF16)
        grads["ffn_w_out"] = by_device(mm(Dev(act, 0), df, "tn", f"{t}_dwout", out_dtype=BF16, tm=FF_W,
                                          stack=(stacked("ffn_w_out"), DEPTH, i)))
        dact = mm(df, W["w_out"], "nt", f"{t}_dact", out_dev=FF_W, tn=FF_W)
        dgu = swiglu_bwd(gu, dact, f"{t}_swiglu")
        grads["ffn_w_in"] = mm(fin, Dev(dgu, 0), "tn", f"{t}_dwin", out_dtype=BF16, out_dev=FF_W, tn=FF_W,
                               stack=(grads["ffn_w_in"], DEPTH, i))
        start(i, lists[i][-4:], {}, f"scatter_ffn{i}")
        dfin = mm(Dev(dgu, 0), W["w_in"], "nt", f"{t}_dfin", tk=FF_W)
        dh1, g_norm[i][2] = rms_bwd(h1, W["g"][2], dfin, f"{t}_n2", res=dh2)
        dy, g_norm[i][1] = rms_bwd(y, W["g"][1], dh1, f"{t}_n1", out_dtype=BF16)
        dhn, gr = _mixer_bwd(kind, f"{t}_mix", hn, dy, W, aux, mix)
        dh, g_norm[i][0] = rms_bwd(h0, W["g"][0], dhn, f"{t}_n0", res=dh1)
        if kind == 0:
            d_a, d_uq, d_ukv = _mla_unlayout(gr["w_a"], gr["w_uq"], gr["w_ukv"])
            mine = {"mla_w_a": by_device(d_a[None]), "mla_w_uq": d_uq[:, None], "mla_w_ukv": d_ukv[:, None],
                    "mla_w_o": by_device(gr["w_o"][None])}
            g_qn[j], g_kvn[j] = gr["q_norm"], gr["kv_norm"]
        elif kind == 1:
            mine = {"dil_w_qkv": gr["w_qkv"], "dil_w_o": by_device(gr["w_o"][None])}
            g_rel = jnp.concatenate(
                [mm(jnp.asarray(_bucket_onehot(DIL[g][1])), gr["dil_dbias"][g].reshape(HEADS, -1), "nt",
                    f"{t}_drel{g}", precise=True, tk=4096) for g in range(3)], axis=1)
        else:
            wide = gr["w_qkvf"]
            mine = {"fox_w_qkvf": jnp.stack([wide[:, 386 * dev:386 * (dev + 1)] for dev in range(N_DEV)])[:, None],
                    "fox_w_o": by_device(gr["w_o"][None])}
            g_bf = gr["b_f"][:, :HEADS]
        start(i, lists[i][:-4], mine, f"scatter_mix{i}")
    grad_x = dh[None]

    own = {n: [] for n, _, _ in BIG}
    for idx, (s_sem, r_sem, names, mine, src_l, land_l, rows) in enumerate(in_flight):
        srcs = [mine[n] if n in mine else grads[n] for n in names]
        srcs, got = scatter_wait(s_sem, r_sem, srcs, src_l, [landed[n] for n in names], land_l, rows, dh,
                                 f"scatter_wait{idx}")
        for n, src, ld, sl, ll, rw in zip(names, srcs, got, src_l, land_l, rows):
            landed[n] = ld
            if n not in mine:
                grads[n] = src
            blk = lax.dynamic_index_in_dim(src, me, axis=1 if rw else 0, keepdims=False)[sl]
            own[n].append((ll, blk))
    big_out = []
    for n, _, _ in BIG:
        part = landed[n]
        for ll, blk in own[n]:
            part = lax.dynamic_update_slice(part, blk[None, None], (me, ll, 0, 0))
        big_out.append(adamw(given[n], given["m_" + n], given["v_" + n], part, f"adamw_{n}"))

    small_full = [jnp.stack([jnp.concatenate(r, axis=0) for r in g_norm]).reshape(-1),
                  jnp.concatenate(g_qn, axis=0).reshape(-1), jnp.concatenate(g_kvn, axis=0).reshape(-1),
                  g_rel.reshape(-1), g_bf.reshape(-1), loss_lanes.reshape(-1)]
    small_n = sum(a.shape[0] for a in small_full)
    small_rows = _rows(small_n)
    parts, = all_gather([_pack(small_full, small_rows, F32)], [False], "gather_small_grads", in_vmem=True)
    tot = _unpack(sum_parts(parts, "sum_small_grads"), [(4, 4, D), (2, Q_RANK), (2, KV_RANK), (32, 48), (1, 16), (LANES,)])
    loss = jnp.sum(tot[5])
    small_g = [lax.dynamic_slice_in_dim(tot[0], me * 128, 128, axis=2), lax.dynamic_slice_in_dim(tot[1], me * 48, 48, axis=1),
               lax.dynamic_slice_in_dim(tot[2], me * 32, 32, axis=1), tot[3], tot[4]]
    small_names = [n for n, _, _ in SMALL_SHARDED] + [n for n, _ in SMALL_REPL]
    small_shapes = [s for _, s, _ in SMALL_SHARDED] + [s for _, s in SMALL_REPL]
    s_rows = _rows(sum(int(np.prod(s)) for s in small_shapes))
    small_out = adamw(_pack([given[n] for n in small_names], s_rows, F32)[None],
                      _pack([given["m_" + n] for n in small_names], s_rows, F32)[None],
                      _pack([given["v_" + n] for n in small_names], s_rows, F32)[None],
                      _pack(small_g, s_rows, F32)[None, None], "adamw_small")
    small_out = [_unpack(o_, small_shapes) for o_ in small_out]

    res = [{}, {}, {}, {}]
    for k in range(4):
        for idx, (n, _, _) in enumerate(BIG):
            res[k][n] = big_out[idx][k]
        for idx, n in enumerate(small_names):
            res[k][n] = small_out[k][idx]
    return (loss, grad_x, *[res[0][n] for n in WEIGHTS], *[res[1][n] for n in WEIGHTS],
            *[res[2][n] for n in WEIGHTS], *[res[3][n] for n in WEIGHTS])
```

```python
import math
from typing import NamedTuple

import numpy as np
import jax
import jax.numpy as jnp
from jax import lax
from jax.experimental import pallas as pl
from jax.experimental.pallas import tpu as pltpu

F32 = jnp.float32
BF16 = jnp.bfloat16
MESH_ID = pl.DeviceIdType.MESH

N_DEV = 8
S = 2048
D = 1024
DEPTH = 4
D_FF = 2816
D_PLE = 256
EPS = 1e-6
NEG = -1e30
LANES = 128
HEADS = 16
PAIRS = 8
Q_RANK = 384
KV_RANK = 256
QBLK = 128
DIL = ((128, 1), (512, 4), (2048, 16))
REL_BUCKETS = 32
FOX_W = 3200
VMEM_LIMIT = 56 * 1024 * 1024

ADAM_LR, ADAM_B1, ADAM_B2, ADAM_EPS, ADAM_WD, ADAM_STEP = 1e-3, 0.9, 0.999, 1e-8, 0.01, 10


BIG = (
    ("ffn_w_in", (4, 1024, 704), 2), ("ffn_w_out", (4, 352, 1024), 1),
    ("ple_w_proj", (4, 256, 128), 2), ("ple_w_gate", (4, 128, 1024), 1),
    ("mla_w_a", (2, 128, 672), 1), ("mla_w_uq", (2, 384, 192), 2),
    ("mla_w_ukv", (2, 256, 256), 2), ("mla_w_o", (2, 128, 1024), 1),
    ("dil_w_qkv", (1, 1024, 1152), 2), ("dil_w_o", (1, 128, 1024), 1),
    ("fox_w_qkvf", (1, 1024, 386), 2), ("fox_w_o", (1, 128, 1024), 1),
)
SMALL_SHARDED = (("norm_g", (4, 4, 128), 2), ("mla_q_norm", (2, 48), 1), ("mla_kv_norm", (2, 32), 1))
SMALL_REPL = (("rel_bias", (32, 48)), ("fox_b_f", (1, 16)))
WEIGHTS = ("norm_g", "ffn_w_in", "ffn_w_out", "ple_w_proj", "ple_w_gate", "rel_bias", "mla_w_a", "mla_q_norm",
           "mla_kv_norm", "mla_w_uq", "mla_w_ukv", "mla_w_o", "dil_w_qkv", "dil_w_o", "fox_w_qkvf", "fox_b_f",
           "fox_w_o")


def _rows(n):
    return -(-n // (8 * LANES)) * 8


def _t5_bucket_np(dist):
    max_exact = REL_BUCKETS // 2
    n = np.maximum(dist.astype(np.float32), np.float32(1.0))
    large = max_exact + (np.log(n / np.float32(max_exact)) / np.float32(math.log(2048 / max_exact))
                         * np.float32(REL_BUCKETS - max_exact)).astype(np.int32)
    large = np.minimum(large, REL_BUCKETS - 1)
    return np.where(dist < max_exact, dist, large)


def _bucket_onehot(dilation):
    i = np.arange(QBLK)[:, None]
    j = np.arange(2 * QBLK)[None, :]
    bucket = _t5_bucket_np(np.clip(QBLK + i - j, 0, None) * dilation).reshape(-1)
    return (np.arange(REL_BUCKETS)[:, None] == bucket[None, :]).astype(np.float32)


def _rope_inv_lanes():
    half = 16
    inv = (np.float32(10000.0) ** (-np.arange(half, dtype=np.float32) / np.float32(half))).astype(np.float32)
    t = np.zeros((1, LANES), np.float32)
    t[0, 0:16] = inv
    t[0, 64:80] = inv
    return t


def _params(sem=None):
    return pltpu.CompilerParams(dimension_semantics=sem, vmem_limit_bytes=VMEM_LIMIT)


def _tile(dim, target):
    if dim <= target or dim % target == 0:
        return min(dim, target)
    t = (target // LANES) * LANES
    while dim % t:
        t -= LANES
    return t


_DIMS = {"nn": (((1,), (0,)), ((), ())), "nt": (((1,), (1,)), ((), ())), "tn": (((0,), (0,)), ((), ()))}


class Lay(NamedTuple):
    arr: jax.Array
    l: int


class Dev(NamedTuple):
    arr: jax.Array
    l: int


def _lshape(op):
    if isinstance(op, Dev):
        g, _, r, w = op.arr.shape
        return r, g * w
    return op.arr.shape[1:] if isinstance(op, Lay) else op.shape


def _op_spec(op, rows_t, cols_t, row_ix, col_ix):
    if isinstance(op, Dev):
        w = op.arr.shape[3]
        assert w % cols_t == 0 and (cols_t % LANES == 0 or cols_t == w), (w, cols_t)
        nb, l = w // cols_t, op.l
        return pl.BlockSpec((1, 1, rows_t, cols_t),
                            lambda i, j, k: (col_ix(i, j, k) // nb, l, row_ix(i, j, k), col_ix(i, j, k) % nb))
    if isinstance(op, Lay):
        l = op.l
        return pl.BlockSpec((1, rows_t, cols_t), lambda i, j, k: (l, row_ix(i, j, k), col_ix(i, j, k)))
    return pl.BlockSpec((rows_t, cols_t), lambda i, j, k: (row_ix(i, j, k), col_ix(i, j, k)))


def _mat(ref):
    return ref[(0,) * (len(ref.shape) - 2)]


def mm(a, b, mode, name, out_dtype=F32, precise=False, add=None, out_dev=None, stack=None, after=None,
       tm=1024, tn=512, tk=2048):
    (ar, ac), (br, bc) = _lshape(a), _lshape(b)
    M, K = (ac, ar) if mode == "tn" else (ar, ac)
    N = br if mode == "nt" else bc
    assert K == (bc if mode == "nt" else br)
    tm, tn, tk = _tile(M, tm), _tile(N, tn), _tile(K, tk)
    nk = K // tk
    ix_i, ix_j, ix_k = (lambda i, j, k: i), (lambda i, j, k: j), (lambda i, j, k: k)
    a_spec = _op_spec(a, tk, tm, ix_k, ix_i) if mode == "tn" else _op_spec(a, tm, tk, ix_i, ix_k)
    b_spec = _op_spec(b, tn, tk, ix_j, ix_k) if mode == "nt" else _op_spec(b, tk, tn, ix_k, ix_j)
    buf, n_l, l = stack if stack is not None else (None, 1, 0)
    if out_dev is not None:
        out = Dev(jax.ShapeDtypeStruct((N // out_dev, n_l, M, out_dev), out_dtype), l)
    elif stack is not None:
        out = Lay(jax.ShapeDtypeStruct((n_l, M, N), out_dtype), l)
    else:
        out = jax.ShapeDtypeStruct((M, N), out_dtype)
    o_spec = _op_spec(out, tm, tn, ix_i, ix_j)
    n_in = 3 if add is not None else 2

    def body(*refs):
        a_ref, b_ref = refs[0], refs[1]
        o_ref = refs[n_in + (buf is not None) + (after is not None)]
        if precise:
            part = lax.dot_general(_mat(a_ref), _mat(b_ref), _DIMS[mode], precision=lax.Precision.HIGHEST,
                                   preferred_element_type=F32)
        else:
            part = lax.dot_general(_mat(a_ref).astype(BF16), _mat(b_ref).astype(BF16), _DIMS[mode],
                                   preferred_element_type=F32)

        def finish(r):
            r = r + refs[2][...] if add is not None else r
            o_ref[...] = r.astype(o_ref.dtype).reshape(o_ref.shape)

        if nk == 1:
            finish(part)
            return
        acc, k = refs[-1], pl.program_id(2)

        @pl.when(k == 0)
        def _():
            acc[...] = part

        @pl.when(k > 0)
        def _():
            acc[...] += part

        @pl.when(k == nk - 1)
        def _():
            finish(acc[...])

    ins = [getattr(a, "arr", a), getattr(b, "arr", b)] + ([add] if add is not None else [])
    in_specs = [a_spec, b_spec] + ([o_spec] if add is not None else [])
    aliases = {}
    if buf is not None:
        ins.append(buf)
        in_specs.append(pl.BlockSpec(memory_space=pl.ANY))
        aliases = {n_in: 0}
    if after is not None:
        ins.append(after)
        in_specs.append(pl.BlockSpec(memory_space=pl.ANY))
    return pl.pallas_call(
        body, name=name, grid=(M // tm, N // tn, nk), in_specs=in_specs, out_specs=o_spec,
        out_shape=getattr(out, "arr", out), input_output_aliases=aliases,
        scratch_shapes=[pltpu.VMEM((tm, tn), F32)] if nk > 1 else [],
        compiler_params=_params(("parallel", "parallel", "arbitrary")),
    )(*ins)


def _rows_call(body, name, ins, outs, tr=256, acc_outs=()):
    n = ins[0].shape[0]
    tr = min(tr, n)
    in_specs = [pl.BlockSpec((tr, a.shape[1]), lambda i: (i, 0)) if a.shape[0] == n else
                pl.BlockSpec(a.shape, lambda i: (0, 0)) for a in ins]
    out_specs = [pl.BlockSpec((tr, w), lambda i: (i, 0)) for w, _ in outs] + \
                [pl.BlockSpec((1, w), lambda i: (0, 0)) for w in acc_outs]
    out_shape = [jax.ShapeDtypeStruct((n, w), dt) for w, dt in outs] + \
                [jax.ShapeDtypeStruct((1, w), F32) for w in acc_outs]
    res = pl.pallas_call(body, name=name, grid=(n // tr,), in_specs=in_specs, out_specs=out_specs,
                         out_shape=out_shape, compiler_params=_params(("arbitrary",)))(*ins)
    return res[0] if len(res) == 1 else res


def _acc(ref, val):
    @pl.when(pl.program_id(0) == 0)
    def _():
        ref[...] = jnp.zeros_like(ref)

    ref[...] += val


def rms_fwd(x, g, name, res=None, out_dtype=F32):
    def body(*refs):
        x_ref, g_ref = refs[0], refs[1]
        o_ref = refs[-1]
        xv = x_ref[...]
        y = xv * lax.rsqrt(jnp.mean(xv * xv, axis=-1, keepdims=True) + EPS) * g_ref[...]
        o_ref[...] = (y + refs[2][...] if res is not None else y).astype(o_ref.dtype)

    ins = [x, g] + ([res] if res is not None else [])
    return _rows_call(body, name, ins, [(x.shape[1], out_dtype)])


def rms_bwd(x, g, dy, name, res=None, out_dtype=F32):
    def body(*refs):
        x_ref, g_ref, dy_ref = refs[:3]
        dx_ref, dg_ref = refs[-2], refs[-1]
        xv, dyv = x_ref[...], dy_ref[...]
        r = lax.rsqrt(jnp.mean(xv * xv, axis=-1, keepdims=True) + EPS)
        xh = xv * r
        dxh = dyv * g_ref[...]
        dx = r * (dxh - xh * jnp.mean(dxh * xh, axis=-1, keepdims=True))
        dx_ref[...] = (dx + refs[3][...] if res is not None else dx).astype(dx_ref.dtype)
        _acc(dg_ref, jnp.sum(dyv * xh, axis=0, keepdims=True))

    ins = [x, g, dy] + ([res] if res is not None else [])
    return _rows_call(body, name, ins, [(x.shape[1], out_dtype)], acc_outs=(x.shape[1],))


def _sigmoid(x):
    return 0.5 * jnp.tanh(0.5 * x) + 0.5


FF_W = 704
FF_TR = 512


def _ff_spec(shift):
    return pl.BlockSpec((1, 1, FF_TR, FF_W), lambda d, i: (d + shift, 0, i, 0))


def swiglu_fwd(gu, name):
    def body(g_ref, u_ref, o_ref):
        gate = g_ref[...]
        o_ref[...] = (gate * _sigmoid(gate) * u_ref[...]).astype(BF16)

    return pl.pallas_call(body, name=name, grid=(4, S // FF_TR), in_specs=[_ff_spec(0), _ff_spec(4)],
                          out_specs=_ff_spec(0), out_shape=jax.ShapeDtypeStruct((4, 1, S, FF_W), BF16),
                          compiler_params=_params(("parallel", "parallel")))(gu, gu)


def swiglu_bwd(gu, dact, name):
    def body(g_ref, u_ref, d_ref, o_ref):
        gate, d = g_ref[...], d_ref[...]
        sg = _sigmoid(gate)

        @pl.when(pl.program_id(0) < 4)
        def _():
            o_ref[...] = (d * u_ref[...] * sg * (1.0 + gate * (1.0 - sg))).astype(BF16)

        @pl.when(pl.program_id(0) >= 4)
        def _():
            o_ref[...] = (d * gate * sg).astype(BF16)

    def half(shift):
        return pl.BlockSpec((1, 1, FF_TR, FF_W), lambda d, i: (d % 4 + shift, 0, i, 0))

    return pl.pallas_call(body, name=name, grid=(8, S // FF_TR), in_specs=[half(0), half(4), half(0)],
                          out_specs=_ff_spec(0), out_shape=jax.ShapeDtypeStruct((8, 1, S, FF_W), BF16),
                          compiler_params=_params(("parallel", "parallel")))(gu, gu, dact)


def ple_fwd(h, pp, gt, name):
    def body(h_ref, p_ref, g_ref, o_ref):
        o_ref[...] = h_ref[...] + p_ref[...] * _sigmoid(g_ref[...])

    return _rows_call(body, name, [h, pp, gt], [(D, F32)])


def ple_bwd(dh, pp, gt, name):
    def body(d_ref, p_ref, g_ref, dp_ref, dg_ref):
        d, sg = d_ref[...], _sigmoid(g_ref[...])
        dp_ref[...] = (d * sg).astype(BF16)
        dg_ref[...] = (d * p_ref[...] * sg * (1.0 - sg)).astype(BF16)

    return _rows_call(body, name, [dh, pp, gt], [(D, BF16), (D, BF16)])


def loss_head(y, target, name):
    def body(y_ref, t_ref, d_ref, l_ref):
        e = y_ref[...] - t_ref[...]
        d_ref[...] = e * (1.0 / D)
        col = jnp.sum(e * e, axis=0, keepdims=True) * (0.5 / D)
        _acc(l_ref, sum(col[:, LANES * c:LANES * (c + 1)] for c in range(D // LANES)))

    return _rows_call(body, name, [y, target], [(D, F32)], acc_outs=(LANES,))


def rope_tables(pos_col, name):
    inv = jnp.asarray(_rope_inv_lanes())

    def body(p_ref, inv_ref, c_ref, s_ref):
        ang = p_ref[...].astype(F32) * inv_ref[...]
        lane = lax.broadcasted_iota(jnp.int32, ang.shape, 1)
        first, second = lane < 16, (lane >= 64) & (lane < 80)
        c_ref[...] = jnp.where(first | second, jnp.cos(ang), 1.0)
        sn = jnp.sin(ang)
        s_ref[...] = jnp.where(first, -sn, jnp.where(second, sn, 0.0))

    return _rows_call(body, name, [pos_col, inv], [(LANES, F32), (LANES, F32)])


def _rope(x, c, s):
    return x * c + pltpu.roll(x, 64, axis=1) * s


def _rope_t(d, c, s):
    return d * c + pltpu.roll(d * s, 64, axis=1)


def mla_qk_fwd(qp, kvp, kr, cos, sin, name):
    def body(q_ref, k_ref, kr_ref, c_ref, s_ref, qo_ref, ko_ref):
        c, s = c_ref[...], s_ref[...]
        kr_rot = _rope(kr_ref[...], c, s)
        for h in range(HEADS):
            sl = slice(LANES * h, LANES * (h + 1))
            qo_ref[:, sl] = _rope(q_ref[:, sl], c, s).astype(BF16)
            ko_ref[:, sl] = (k_ref[:, sl] + kr_rot).astype(BF16)

    n = qp.shape[0]
    tr = 256
    w = HEADS * LANES
    return pl.pallas_call(
        body, name=name, grid=(n // tr,),
        in_specs=[pl.BlockSpec((tr, w), lambda i: (i, 0)), pl.BlockSpec((tr, w), lambda i: (i, 0)),
                  pl.BlockSpec((tr, LANES), lambda i: (i, 0)), pl.BlockSpec((tr, LANES), lambda i: (i, 0)),
                  pl.BlockSpec((tr, LANES), lambda i: (i, 0))],
        out_specs=[pl.BlockSpec((tr, w), lambda i: (i, 0))] * 2,
        out_shape=[jax.ShapeDtypeStruct((n, w), BF16)] * 2, compiler_params=_params(("arbitrary",)),
    )(qp, kvp, kr, cos, sin)


def mla_qk_bwd(dq, dk, cos, sin, name):
    def body(dq_ref, dk_ref, c_ref, s_ref, dqp_ref, dkr_ref):
        c, s = c_ref[...], s_ref[...]
        tot = jnp.zeros(c.shape, F32)
        for h in range(HEADS):
            sl = slice(LANES * h, LANES * (h + 1))
            dqp_ref[:, sl] = _rope_t(dq_ref[:, sl], c, s).astype(BF16)
            tot = tot + dk_ref[:, sl]
        dkr_ref[...] = _rope_t(tot, c, s).astype(BF16)

    return _rows_call(body, name, [dq, dk, cos, sin], [(HEADS * LANES, BF16), (LANES, BF16)])


TQ = 256


def _pair_masks(shape):
    lane = lax.broadcasted_iota(jnp.int32, shape, 1)
    return (lane < 64, lane >= 64)


def _causal_probs(q_a, k_a, scale, b0, cq, ck):
    s = lax.dot_general(q_a, k_a, _DIMS["nt"], preferred_element_type=F32) * scale
    if cq is not None:
        s = s + (cq - ck)
    row = lax.broadcasted_iota(jnp.int32, s.shape, 0) + b0
    col = lax.broadcasted_iota(jnp.int32, s.shape, 1)
    s = jnp.where(col <= row, s, NEG)
    e = jnp.exp(s - jnp.max(s, axis=-1, keepdims=True))
    return e / jnp.sum(e, axis=-1, keepdims=True)


def attn_fwd(q, k, v, name, *, wide, scale, q_off=0, k_off=0, v_off=0, cum=None, cum_t=None):
    qw = 2 * LANES if wide else LANES
    forget = cum is not None

    def body(*refs):
        q_ref, k_ref, v_ref = refs[:3]
        o_ref = refs[-1]
        m0, m1 = _pair_masks((TQ, LANES))
        for qi in range(S // TQ):
            b0, b1 = qi * TQ, (qi + 1) * TQ
            outs = []
            for a, msk in enumerate((m0, m1)):
                if wide:
                    q_a, k_a = q_ref[b0:b1, LANES * a:LANES * (a + 1)], k_ref[:b1, LANES * a:LANES * (a + 1)]
                else:
                    q_a, k_a = jnp.where(msk, q_ref[b0:b1, :], jnp.zeros((), BF16)), k_ref[:b1, :]
                cq = refs[3][0, b0:b1, a:a + 1] if forget else None
                ck = refs[4][0, a:a + 1, :b1] if forget else None
                p = _causal_probs(q_a, k_a, scale, b0, cq, ck)
                outs.append(jnp.dot(p.astype(BF16), v_ref[:b1, :], preferred_element_type=F32))
            o_ref[b0:b1, :] = jnp.where(m0, outs[0], outs[1])

    in_specs = [pl.BlockSpec((S, qw), lambda h: (0, q_off * LANES // qw + h)),
                pl.BlockSpec((S, qw), lambda h: (0, k_off * LANES // qw + h)),
                pl.BlockSpec((S, LANES), lambda h: (0, v_off + h))]
    ins = [q, k, v]
    if forget:
        in_specs += [pl.BlockSpec((1, S, 2), lambda h: (h, 0, 0)), pl.BlockSpec((1, 2, S), lambda h: (h, 0, 0))]
        ins += [cum, cum_t]
    return pl.pallas_call(
        body, name=name, grid=(PAIRS,), in_specs=in_specs, out_specs=pl.BlockSpec((S, LANES), lambda h: (0, h)),
        out_shape=jax.ShapeDtypeStruct((S, PAIRS * LANES), F32), compiler_params=_params(("arbitrary",)),
    )(*ins)


def attn_bwd(q, k, v, o, do, name, *, wide, scale, out_dtype=F32, q_off=0, k_off=0, v_off=0, cum=None, cum_t=None):
    qw = 2 * LANES if wide else LANES
    forget = cum is not None

    def body(*refs):
        q_ref, k_ref, v_ref, o_ref, do_ref = refs[:5]
        n_out = 5 if forget else 3
        outs = refs[-(n_out + 2):-2]
        dq_ref, dk_ref, dv_ref = outs[:3]
        dk_acc, dv_acc = refs[-2], refs[-1]
        dk_acc[...] = jnp.zeros_like(dk_acc)
        dv_acc[...] = jnp.zeros_like(dv_acc)
        if forget:
            dcq_ref, dck_ref = outs[3], outs[4]
            dck_ref[...] = jnp.zeros_like(dck_ref)
        m0, m1 = _pair_masks((TQ, LANES))
        for qi in range(S // TQ):
            b0, b1 = qi * TQ, (qi + 1) * TQ
            do2 = do_ref[b0:b1, :]
            dd = do2 * o_ref[b0:b1, :]
            do_b = do2.astype(BF16)
            mk0, mk1 = _pair_masks((b1, LANES))
            dqs = []
            for a, (msk, mk) in enumerate(((m0, mk0), (m1, mk1))):
                lanes = slice(LANES * a, LANES * (a + 1)) if wide else slice(0, LANES)
                if wide:
                    q_a, k_a = q_ref[b0:b1, lanes], k_ref[:b1, lanes]
                else:
                    q_a, k_a = jnp.where(msk, q_ref[b0:b1, :], jnp.zeros((), BF16)), k_ref[:b1, :]
                cq = refs[5][0, b0:b1, a:a + 1] if forget else None
                ck = refs[6][0, a:a + 1, :b1] if forget else None
                p = _causal_probs(q_a, k_a, scale, b0, cq, ck)
                dp = lax.dot_general(jnp.where(msk, do_b, jnp.zeros((), BF16)), v_ref[:b1, :], _DIMS["nt"],
                                     preferred_element_type=F32)
                delta = jnp.sum(jnp.where(msk, dd, 0.0), axis=-1, keepdims=True)
                ds = p * (dp - delta)
                if forget:
                    dcq_ref[0, b0:b1, a:a + 1] = jnp.sum(ds, axis=-1, keepdims=True)
                    dck_ref[0, a:a + 1, :b1] -= jnp.sum(ds, axis=0, keepdims=True)
                ds_b = (ds * scale).astype(BF16)
                dqs.append(jnp.dot(ds_b, k_a, preferred_element_type=F32))
                dk_acc[:b1, lanes] += lax.dot_general(ds_b, q_a, _DIMS["tn"], preferred_element_type=F32)
                dv_acc[:b1, :] += jnp.where(mk, lax.dot_general(p.astype(BF16), do_b, _DIMS["tn"],
                                                                 preferred_element_type=F32), 0.0)
            if wide:
                dq_ref[b0:b1, :LANES] = dqs[0].astype(out_dtype)
                dq_ref[b0:b1, LANES:] = dqs[1].astype(out_dtype)
            else:
                dq_ref[b0:b1, :] = jnp.where(m0, dqs[0], dqs[1]).astype(out_dtype)
        dk_ref[...] = dk_acc[...].astype(out_dtype)
        dv_ref[...] = dv_acc[...].astype(out_dtype)

    pair = pl.BlockSpec((S, LANES), lambda h: (0, h))
    qk_out = pl.BlockSpec((S, qw), lambda h: (0, h))
    in_specs = [pl.BlockSpec((S, qw), lambda h: (0, q_off * LANES // qw + h)),
                pl.BlockSpec((S, qw), lambda h: (0, k_off * LANES // qw + h)),
                pl.BlockSpec((S, LANES), lambda h: (0, v_off + h)), pair, pair]
    ins = [q, k, v, o, do]
    out_specs = [qk_out, qk_out, pair]
    out_shape = [jax.ShapeDtypeStruct((S, PAIRS * qw), out_dtype)] * 2 + \
                [jax.ShapeDtypeStruct((S, PAIRS * LANES), out_dtype)]
    if forget:
        by_q, by_k = pl.BlockSpec((1, S, 2), lambda h: (h, 0, 0)), pl.BlockSpec((1, 2, S), lambda h: (h, 0, 0))
        in_specs += [by_q, by_k]
        ins += [cum, cum_t]
        out_specs += [by_q, by_k]
        out_shape += [jax.ShapeDtypeStruct((PAIRS, S, 2), F32), jax.ShapeDtypeStruct((PAIRS, 2, S), F32)]
    return pl.pallas_call(
        body, name=name, grid=(PAIRS,), in_specs=in_specs, out_specs=out_specs, out_shape=out_shape,
        scratch_shapes=[pltpu.VMEM((S, qw), F32), pltpu.VMEM((S, LANES), F32)],
        compiler_params=_params(("arbitrary",)),
    )(*ins)


def _tri(lower):
    r = lax.broadcasted_iota(jnp.int32, (QBLK, QBLK), 0)
    c = lax.broadcasted_iota(jnp.int32, (QBLK, QBLK), 1)
    return jnp.where((c <= r) if lower else (c >= r), 1.0, 0.0).astype(F32)


def _hi_dot(a, b):
    return jnp.dot(a, b, precision=lax.Precision.HIGHEST, preferred_element_type=F32)


def fox_gate_fwd(fl, bias, name):
    def body(f_ref, b_ref, o_ref):
        tri = _tri(True)
        carry = jnp.zeros((1, LANES), F32)
        for n in range(S // QBLK):
            x = f_ref[n * QBLK:(n + 1) * QBLK, :].astype(F32) + b_ref[...]
            lf = jnp.minimum(x, 0.0) - jnp.log(1.0 + jnp.exp(-jnp.abs(x)))
            c = _hi_dot(tri, lf) + carry
            o_ref[n * QBLK:(n + 1) * QBLK, :] = c
            carry = c[QBLK - 1:QBLK, :]

    return pl.pallas_call(body, name=name, out_shape=jax.ShapeDtypeStruct((S, LANES), F32),
                          compiler_params=_params())(fl, bias)


def fox_gate_bwd(fl, bias, dcq, dck, name):
    def body(f_ref, b_ref, dq_ref, dk_ref, o_ref, db_ref):
        tri = _tri(False)
        carry = jnp.zeros((1, LANES), F32)
        db = jnp.zeros((1, LANES), F32)
        for n in reversed(range(S // QBLK)):
            rows = slice(n * QBLK, (n + 1) * QBLK)
            dlf = _hi_dot(tri, dq_ref[rows, :] + dk_ref[rows, :]) + carry
            carry = dlf[0:1, :]
            x = f_ref[rows, :].astype(F32) + b_ref[...]
            dx = dlf * (1.0 - _sigmoid(x))
            o_ref[rows, :] = dx
            db = db + jnp.sum(dx, axis=0, keepdims=True)
        db_ref[...] = db

    return pl.pallas_call(body, name=name, out_shape=[jax.ShapeDtypeStruct((S, LANES), F32),
                                                      jax.ShapeDtypeStruct((1, LANES), F32)],
                          compiler_params=_params())(fl, bias, dcq, dck)


def _band_valid(first):
    w = QBLK if first else 2 * QBLK
    i = lax.broadcasted_iota(jnp.int32, (QBLK, w), 0)
    j = lax.broadcasted_iota(jnp.int32, (QBLK, w), 1)
    return (j <= i) if first else ((j >= i) & (j - QBLK <= i))


def _band_logits(q_a, kk, bias, first):
    s = lax.dot_general(q_a, kk, _DIMS["nt"], preferred_element_type=F32) * 0.125 + bias
    return jnp.where(_band_valid(first), s, NEG)


def dil_fwd(qkv, bias, g, name):
    d = DIL[g][1]
    ls = S // d
    view = qkv.reshape(ls, d * 9216)

    def body(q_ref, k_ref, v_ref, b_ref, o_ref, l_ref):
        m0, m1 = _pair_masks((QBLK, LANES))
        for n in range(ls // QBLK):
            rows = slice(n * QBLK, (n + 1) * QBLK)
            keys = rows if n == 0 else slice((n - 1) * QBLK, (n + 1) * QBLK)
            os_, ls_ = [], []
            for a, msk in enumerate((m0, m1)):
                q_a = jnp.where(msk, q_ref[rows, :], jnp.zeros((), BF16))
                bias_a = b_ref[a, :, QBLK:] if n == 0 else b_ref[a]
                s = _band_logits(q_a, k_ref[keys, :], bias_a, n == 0)
                mx = jnp.max(s, axis=-1, keepdims=True)
                e = jnp.exp(s - mx)
                l = jnp.sum(e, axis=-1, keepdims=True)
                os_.append(jnp.dot((e / l).astype(BF16), v_ref[keys, :], preferred_element_type=F32))
                ls_.append(mx + jnp.log(l))
            o_ref[rows, :] = jnp.where(m0, os_[0], os_[1])
            l_ref[rows, :] = jnp.where(m0, ls_[0], ls_[1])

    def col(j):
        return lambda h, r: (0, r * 72 + g * 24 + j * 8 + h)

    out = pl.BlockSpec((ls, LANES), lambda h, r: (0, r * 8 + h))
    o, lse = pl.pallas_call(
        body, name=name, grid=(PAIRS, d),
        in_specs=[pl.BlockSpec((ls, LANES), col(0)), pl.BlockSpec((ls, LANES), col(1)), pl.BlockSpec((ls, LANES), col(2)),
                  pl.BlockSpec((2, QBLK, 2 * QBLK), lambda h, r: (h, 0, 0))],
        out_specs=[out, out], out_shape=[jax.ShapeDtypeStruct((ls, d * D), F32)] * 2,
        compiler_params=_params(("arbitrary", "arbitrary")),
    )(view, view, view, bias)
    return o.reshape(S, D), lse.reshape(S, D)


def dil_merge(os_, lses, name):
    def body(o0, o1, o2, l0, l1, l2, o_ref, l_ref):
        ls_ = [l0[...], l1[...], l2[...]]
        mx = jnp.maximum(jnp.maximum(ls_[0], ls_[1]), ls_[2])
        tot = mx + jnp.log(sum(jnp.exp(l - mx) for l in ls_))
        o_ref[...] = sum(jnp.exp(l - tot) * o[...] for l, o in zip(ls_, (o0, o1, o2)))
        l_ref[...] = tot

    return _rows_call(body, name, list(os_) + list(lses), [(D, F32), (D, F32)])


def dil_bwd(qkv, bias, o, lse, do, g, name):
    d = DIL[g][1]
    ls = S // d
    view = qkv.reshape(ls, d * 9216)
    o, lse, do = (t.reshape(ls, d * D) for t in (o, lse, do))

    def body(q_ref, k_ref, v_ref, b_ref, o_ref, l_ref, do_ref, dq_ref, dk_ref, dv_ref, db_ref, dk_acc, dv_acc):
        @pl.when(pl.program_id(1) == 0)
        def _():
            db_ref[...] = jnp.zeros_like(db_ref)

        dk_acc[...] = jnp.zeros_like(dk_acc)
        dv_acc[...] = jnp.zeros_like(dv_acc)
        m0, m1 = _pair_masks((QBLK, LANES))
        for n in range(ls // QBLK):
            rows = slice(n * QBLK, (n + 1) * QBLK)
            keys = rows if n == 0 else slice((n - 1) * QBLK, (n + 1) * QBLK)
            nk = QBLK if n == 0 else 2 * QBLK
            do2, lse2 = do_ref[rows, :], l_ref[rows, :]
            dd = do2 * o_ref[rows, :]
            do_b = do2.astype(BF16)
            mk0, mk1 = _pair_masks((nk, LANES))
            dqs = []
            for a, (msk, mk) in enumerate(((m0, mk0), (m1, mk1))):
                q_a = jnp.where(msk, q_ref[rows, :], jnp.zeros((), BF16))
                kk = k_ref[keys, :]
                bias_a = b_ref[a, :, QBLK:] if n == 0 else b_ref[a]
                s = _band_logits(q_a, kk, bias_a, n == 0)
                lse_a = jnp.max(jnp.where(msk, lse2, -jnp.inf), axis=-1, keepdims=True)
                p = jnp.exp(s - lse_a)
                dp = lax.dot_general(jnp.where(msk, do_b, jnp.zeros((), BF16)), v_ref[keys, :], _DIMS["nt"],
                                     preferred_element_type=F32)
                delta = jnp.sum(jnp.where(msk, dd, 0.0), axis=-1, keepdims=True)
                ds = p * (dp - delta)
                if n == 0:
                    db_ref[a, :, QBLK:] += ds
                else:
                    db_ref[a] += ds
                ds_b = (ds * 0.125).astype(BF16)
                dqs.append(jnp.dot(ds_b, kk, preferred_element_type=F32))
                dk_acc[keys, :] += lax.dot_general(ds_b, q_a, _DIMS["tn"], preferred_element_type=F32)
                dv_acc[keys, :] += jnp.where(mk, lax.dot_general(p.astype(BF16), do_b, _DIMS["tn"],
                                                                 preferred_element_type=F32), 0.0)
            dq_ref[rows, :] = jnp.where(m0, dqs[0], dqs[1]).astype(BF16)
        dk_ref[...] = dk_acc[...].astype(BF16)
        dv_ref[...] = dv_acc[...].astype(BF16)

    def col(j):
        return lambda h, r: (0, r * 72 + g * 24 + j * 8 + h)

    nat = pl.BlockSpec((ls, LANES), lambda h, r: (0, r * 8 + h))
    b_spec = pl.BlockSpec((2, QBLK, 2 * QBLK), lambda h, r: (h, 0, 0))
    dq, dk, dv, db = pl.pallas_call(
        body, name=name, grid=(PAIRS, d),
        in_specs=[pl.BlockSpec((ls, LANES), col(0)), pl.BlockSpec((ls, LANES), col(1)), pl.BlockSpec((ls, LANES), col(2)),
                  b_spec, nat, nat, nat],
        out_specs=[nat, nat, nat, b_spec],
        out_shape=[jax.ShapeDtypeStruct((ls, d * D), BF16)] * 3 + [jax.ShapeDtypeStruct((HEADS, QBLK, 2 * QBLK), F32)],
        scratch_shapes=[pltpu.VMEM((ls, LANES), F32), pltpu.VMEM((ls, LANES), F32)],
        compiler_params=_params(("arbitrary", "arbitrary")),
    )(view, view, view, bias, o, lse, do)
    return dq.reshape(S, D), dk.reshape(S, D), dv.reshape(S, D), db


def _place():
    x, y, c = lax.axis_index("x"), lax.axis_index("y"), lax.axis_index("c")
    return x, y, c


def _dev_slot(ref, by_rows, dev):
    return ref.at[:, dev] if by_rows else ref.at[dev]


def all_gather(shards, by_rows, name, in_vmem=False):
    n = len(shards)

    def body(*refs):
        x_refs, out_refs = refs[:n], refs[n:2 * n]
        send_sems, recv_sems, local_sems = refs[2 * n:]
        x, y, c = _place()
        me, sibling = (x, y, c), (x, y, 1 - c)
        chips = [(1 - x, y), (x, 1 - y), (1 - x, 1 - y)]

        def slot(t, px, py, pc):
            return _dev_slot(out_refs[t], by_rows[t], 4 * px + 2 * py + pc)

        def copy(t, k, blk, to, src=None):
            return pltpu.make_async_remote_copy(
                src_ref=slot(t, *blk) if src is None else src, dst_ref=slot(t, *blk), send_sem=send_sems.at[7 * t + k],
                recv_sem=recv_sems.at[7 * t + k], device_id=to, device_id_type=MESH_ID)

        mine = [pltpu.make_async_copy(x_refs[t], slot(t, *me), local_sems.at[t]) for t in range(n)]
        for cp in mine:
            cp.start()
        first = []
        for t in range(n):
            first.append(copy(t, 0, me, sibling, src=x_refs[t]))
            first += [copy(t, 1 + j, me, (*chip, c), src=x_refs[t]) for j, chip in enumerate(chips)]
        for cp in first:
            cp.start()
        passed = []
        for j, chip in enumerate(chips):
            for t in range(n):
                copy(t, 1 + j, (*chip, c), me).wait_recv()
                passed.append(copy(t, 4 + j, (*chip, c), sibling))
                passed[-1].start()
        for t in range(n):
            copy(t, 0, sibling, me).wait_recv()
            for j, chip in enumerate(chips):
                copy(t, 4 + j, (*chip, 1 - c), me).wait_recv()
        for cp in first + passed:
            cp.wait_send()
        for cp in mine:
            cp.wait()

    def gathered(s, rows):
        shp = (s.shape[0], N_DEV) + s.shape[1:] if rows else (N_DEV,) + s.shape
        return jax.ShapeDtypeStruct(shp, s.dtype)

    space = pl.BlockSpec(memory_space=pltpu.VMEM if in_vmem else pl.ANY)
    return pl.pallas_call(
        body, name=name, out_shape=[gathered(s, r) for s, r in zip(shards, by_rows)],
        in_specs=[space] * n, out_specs=[space] * n,
        scratch_shapes=[pltpu.SemaphoreType.DMA((7 * n,)), pltpu.SemaphoreType.DMA((7 * n,)),
                        pltpu.SemaphoreType.DMA((n,))],
        compiler_params=pltpu.CompilerParams(vmem_limit_bytes=VMEM_LIMIT),
    )(*shards)


_HBM = pl.BlockSpec(memory_space=pltpu.HBM)
_SEM = pl.BlockSpec(memory_space=pltpu.SEMAPHORE)
_SPLIT = dict(has_side_effects=pltpu.SideEffectType.DATAFLOW_SIDE_EFFECTING)


def _hbm(a):
    return pltpu.with_memory_space_constraint(a, pltpu.HBM)


def _gathered_shape(s, rows):
    return (s.shape[0], N_DEV) + s.shape[1:] if rows else (N_DEV,) + s.shape


def _first_level(x, y, c):
    return [(x, y, 1 - c), (1 - x, y, c), (x, 1 - y, c), (1 - x, 1 - y, c)]


def gather_start(shards, by_rows, name):
    n = len(shards)

    def body(*refs):
        x_refs, land_refs = refs[:n], refs[n:2 * n]
        send_sems, recv_sems = refs[2 * n], refs[2 * n + 1]
        x, y, c = _place()
        me = 4 * x + 2 * y + c
        for t in range(n):
            for k, peer in enumerate(_first_level(x, y, c)):
                pltpu.make_async_remote_copy(
                    src_ref=x_refs[t], dst_ref=_dev_slot(land_refs[t], by_rows[t], me), send_sem=send_sems.at[4 * t + k],
                    recv_sem=recv_sems.at[4 * t + k], device_id=peer, device_id_type=MESH_ID).start()

    lands = [lax.empty(_gathered_shape(s, r), s.dtype) for s, r in zip(shards, by_rows)]
    sems = pltpu.SemaphoreType.DMA((4 * n,))
    res = pl.pallas_call(
        body, name=name,
        out_shape=(sems, sems) + tuple(pltpu.HBM(a.shape, a.dtype) for a in list(shards) + lands),
        in_specs=[_HBM] * (2 * n), out_specs=(_SEM, _SEM) + (_HBM,) * (2 * n),
        input_output_aliases={i: 2 + i for i in range(2 * n)},
        compiler_params=pltpu.CompilerParams(**_SPLIT),
    )(*[_hbm(a) for a in list(shards) + lands])
    return res[0], res[1], list(res[2:2 + n]), list(res[2 + n:])


def gather_wait(send_sems, recv_sems, first, shards, lands, by_rows, after, name):
    n = len(shards)

    def body(*refs):
        x_refs, land_refs = refs[:n], refs[n:2 * n]
        send_sems, recv_sems = refs[2 * n], refs[2 * n + 1]
        x, y, c = _place()
        for t in range(n):
            for k, (px, py, pc) in enumerate(_first_level(x, y, c)):
                cp = pltpu.make_async_remote_copy(
                    src_ref=x_refs[t], dst_ref=_dev_slot(land_refs[t], by_rows[t], 4 * px + 2 * py + pc),
                    send_sem=send_sems.at[4 * (first + t) + k], recv_sem=recv_sems.at[4 * (first + t) + k],
                    device_id=(px, py, pc), device_id_type=MESH_ID)
                cp.wait_send()
                cp.wait_recv()

    res = pl.pallas_call(
        body, name=name, out_shape=tuple(pltpu.HBM(a.shape, a.dtype) for a in list(shards) + list(lands)),
        in_specs=[_HBM] * (2 * n) + [_SEM, _SEM, pl.BlockSpec(memory_space=pl.ANY)], out_specs=(_HBM,) * (2 * n),
        input_output_aliases={i: i for i in range(2 * n)},
        compiler_params=pltpu.CompilerParams(**_SPLIT),
    )(*shards, *lands, send_sems, recv_sems, after)
    return list(res[n:])


def gather_relay(shards, lands, by_rows, name):
    n = len(shards)

    def body(*refs):
        x_refs, out_refs = refs[:n], refs[2 * n:3 * n]
        send_sems, recv_sems, local_sems = refs[3 * n:]
        x, y, c = _place()
        chips = [(1 - x, y), (x, 1 - y), (1 - x, 1 - y)]

        def copy(t, j, pc):
            blk = _dev_slot(out_refs[t], by_rows[t], 4 * chips[j][0] + 2 * chips[j][1] + pc)
            return pltpu.make_async_remote_copy(src_ref=blk, dst_ref=blk, send_sem=send_sems.at[3 * t + j],
                                                recv_sem=recv_sems.at[3 * t + j], device_id=(x, y, 1 - c),
                                                device_id_type=MESH_ID)

        mine = [pltpu.make_async_copy(x_refs[t], _dev_slot(out_refs[t], by_rows[t], 4 * x + 2 * y + c), local_sems.at[t])
                for t in range(n)]
        sends = [copy(t, j, c) for t in range(n) for j in range(3)]
        for cp in mine + sends:
            cp.start()
        for t in range(n):
            for j in range(3):
                copy(t, j, 1 - c).wait_recv()
        for cp in sends:
            cp.wait_send()
        for cp in mine:
            cp.wait()

    space = pl.BlockSpec(memory_space=pl.ANY)
    return pl.pallas_call(
        body, name=name, out_shape=[jax.ShapeDtypeStruct(a.shape, a.dtype) for a in lands],
        in_specs=[space] * (2 * n), out_specs=[space] * n, input_output_aliases={n + i: i for i in range(n)},
        scratch_shapes=[pltpu.SemaphoreType.DMA((3 * n,)), pltpu.SemaphoreType.DMA((3 * n,)),
                        pltpu.SemaphoreType.DMA((n,))],
        compiler_params=pltpu.CompilerParams(vmem_limit_bytes=VMEM_LIMIT),
    )(*shards, *lands)


def _peers(x, y, c):
    return [(1 - x if k & 4 else x, 1 - y if k & 2 else y, 1 - c if k & 1 else c) for k in range(1, N_DEV)]


def scatter_start(srcs, src_l, lands, land_l, by_rows, name):
    n = len(srcs)

    def body(*refs):
        x_refs, land_refs = refs[:n], refs[n:2 * n]
        send_sems, recv_sems, token = refs[2 * n], refs[2 * n + 1], refs[-1]
        x, y, c = _place()
        me = 4 * x + 2 * y + c
        for k, (px, py, pc) in enumerate(_peers(x, y, c)):
            for t in range(n):
                blk = _dev_slot(x_refs[t], by_rows[t], 4 * px + 2 * py + pc)
                pltpu.make_async_remote_copy(
                    src_ref=blk.at[src_l[t]], dst_ref=land_refs[t].at[me, land_l[t]], send_sem=send_sems.at[7 * t + k],
                    recv_sem=recv_sems.at[7 * t + k], device_id=(px, py, pc), device_id_type=MESH_ID).start()
        token[...] = jnp.zeros_like(token)

    lands = [lax.empty((N_DEV, 1) + s.shape[2:], s.dtype) if ld is None else ld for s, ld in zip(srcs, lands)]
    sems = pltpu.SemaphoreType.DMA((7 * n,))
    res = pl.pallas_call(
        body, name=name,
        out_shape=(sems, sems) + tuple(pltpu.HBM(a.shape, a.dtype) for a in list(srcs) + lands)
        + (jax.ShapeDtypeStruct((8, LANES), F32),),
        in_specs=[_HBM] * (2 * n),
        out_specs=(_SEM, _SEM) + (_HBM,) * (2 * n) + (pl.BlockSpec(memory_space=pltpu.VMEM),),
        input_output_aliases={i: 2 + i for i in range(2 * n)},
        compiler_params=pltpu.CompilerParams(**_SPLIT),
    )(*[_hbm(a) for a in list(srcs) + lands])
    return res[0], res[1], list(res[2:2 + n]), list(res[2 + n:2 + 2 * n]), res[-1]


def scatter_wait(send_sems, recv_sems, srcs, src_l, lands, land_l, by_rows, after, name):
    n = len(srcs)

    def body(*refs):
        x_refs, land_refs = refs[:n], refs[n:2 * n]
        send_sems, recv_sems = refs[2 * n], refs[2 * n + 1]
        x, y, c = _place()
        for k, (px, py, pc) in enumerate(_peers(x, y, c)):
            peer = 4 * px + 2 * py + pc
            for t in range(n):
                cp = pltpu.make_async_remote_copy(
                    src_ref=_dev_slot(x_refs[t], by_rows[t], peer).at[src_l[t]], dst_ref=land_refs[t].at[peer, land_l[t]],
                    send_sem=send_sems.at[7 * t + k], recv_sem=recv_sems.at[7 * t + k], device_id=(px, py, pc),
                    device_id_type=MESH_ID)
                cp.wait_send()
                cp.wait_recv()

    res = pl.pallas_call(
        body, name=name, out_shape=tuple(pltpu.HBM(a.shape, a.dtype) for a in list(srcs) + list(lands)),
        in_specs=[_HBM] * (2 * n) + [_SEM, _SEM, pl.BlockSpec(memory_space=pl.ANY)], out_specs=(_HBM,) * (2 * n),
        input_output_aliases={i: i for i in range(2 * n)},
        compiler_params=pltpu.CompilerParams(**_SPLIT),
    )(*srcs, *lands, send_sems, recv_sems, after)
    return list(res[:n]), list(res[n:])


ADAM_BLOCK_BYTES = 3 << 19


def adamw(w, m, v, parts, name):
    n_parts = parts.shape[0]
    n_l, r, c = w.shape
    lane_c = -(-c // LANES) * LANES
    fits = [t for t in range(16, r, 16) if r % t == 0 and t * lane_c * 4 <= ADAM_BLOCK_BYTES]
    tr = max(fits) if fits and r * lane_c * 4 > ADAM_BLOCK_BYTES else r
    c1 = 1.0 / (1.0 - ADAM_B1 ** ADAM_STEP)
    c2 = 1.0 / (1.0 - ADAM_B2 ** ADAM_STEP)

    def body(w_ref, m_ref, v_ref, p_ref, g_ref, d_ref, nm_ref, nv_ref):
        g = p_ref[0].astype(F32)
        for j in range(1, n_parts):
            g = g + p_ref[j].astype(F32)
        nm = ADAM_B1 * m_ref[...] + (1.0 - ADAM_B1) * g
        nv = ADAM_B2 * v_ref[...] + (1.0 - ADAM_B2) * (g * g)
        g_ref[...] = g
        nm_ref[...] = nm
        nv_ref[...] = nv
        d_ref[...] = -ADAM_LR * ((nm * c1) / (jnp.sqrt(nv * c2) + ADAM_EPS) + ADAM_WD * w_ref[...])

    blk = pl.BlockSpec((1, tr, c), lambda l, i: (l, i, 0))
    return pl.pallas_call(
        body, name=name, grid=(n_l, r // tr),
        in_specs=[blk, blk, blk, pl.BlockSpec((n_parts, 1, tr, c), lambda l, i: (0, l, i, 0))],
        out_specs=[blk] * 4, out_shape=[jax.ShapeDtypeStruct((n_l, r, c), F32)] * 4,
        compiler_params=_params(("parallel", "parallel")),
    )(w, m, v, parts)


def sum_parts(parts, name):
    def body(p_ref, o_ref):
        g = p_ref[0]
        for j in range(1, parts.shape[0]):
            g = g + p_ref[j]
        o_ref[...] = g

    return pl.pallas_call(body, name=name, out_shape=jax.ShapeDtypeStruct(parts.shape[1:], F32),
                          compiler_params=_params())(parts)


def _pack(arrs, rows, dtype):
    flat = jnp.concatenate([a.reshape(-1).astype(dtype) for a in arrs])
    return jnp.pad(flat, (0, rows * LANES - flat.shape[0])).reshape(rows, LANES)


def _unpack(packed, shapes):
    flat, out, off = packed.reshape(-1), [], 0
    for shp in shapes:
        n = int(np.prod(shp))
        out.append(flat[off:off + n].reshape(shp))
        off += n
    return out


def _cat(parts):
    return jnp.concatenate(parts, axis=1)


def _layer_list(i):
    kind, j = i % 3, i // 3
    mix = ([("mla_w_a", j), ("mla_w_uq", j), ("mla_w_ukv", j), ("mla_w_o", j)] if kind == 0 else
           [("dil_w_qkv", 0), ("dil_w_o", 0)] if kind == 1 else [("fox_w_qkvf", 0), ("fox_w_o", 0)])
    return mix + [("ffn_w_in", i), ("ffn_w_out", i), ("ple_w_proj", i), ("ple_w_gate", i)]


def _mla_layout(w_a, g_uq, g_ukv, j):
    def z(r, n):
        return jnp.zeros((r, n), BF16)

    wa = w_a[j]
    a = _cat([wa[:, :640], wa[:, 640:656], z(D, 48), wa[:, 656:672], z(D, 48)])
    q, k, v = [], [], []
    for h in range(HEADS):
        b = g_uq[h // 2, j][:, 96 * (h % 2):96 * (h % 2 + 1)]
        q += [b[:, 64:80], b[:, 0:32], z(Q_RANK, 16), b[:, 80:96], b[:, 32:64], z(Q_RANK, 16)]
        b = g_ukv[h // 2, j][:, LANES * (h % 2):LANES * (h % 2 + 1)]
        k += [z(KV_RANK, 16), b[:, 0:32], z(KV_RANK, 32), b[:, 32:64], z(KV_RANK, 16)]
        v.append(b[:, 64:128])
    return a, _cat(q), _cat(k + v)


def _mla_unlayout(d_a, d_uq, d_ukv):
    a = _cat([d_a[:, :640], d_a[:, 640:656], d_a[:, 704:720]])
    uq, ukv = [], []
    for dev in range(N_DEV):
        q, kv = [], []
        for h in (2 * dev, 2 * dev + 1):
            b = d_uq[:, LANES * h:LANES * (h + 1)]
            q += [b[:, 16:48], b[:, 80:112], b[:, 0:16], b[:, 64:80]]
            b = d_ukv[:, LANES * h:LANES * (h + 1)]
            kv += [b[:, 16:48], b[:, 80:112], d_ukv[:, HEADS * LANES + 64 * h:HEADS * LANES + 64 * (h + 1)]]
        uq.append(_cat(q))
        ukv.append(_cat(kv))
    return a, jnp.stack(uq), jnp.stack(ukv)


def _mixer_fwd(kind, tag, hn, W, aux):
    if kind == 0:
        a = mm(hn, W["w_a"], "nn", f"{tag}_a", tn=768)
        cq = rms_fwd(a[:, :Q_RANK], W["q_norm"], f"{tag}_cq", out_dtype=BF16)
        ckv = rms_fwd(a[:, Q_RANK:Q_RANK + KV_RANK], W["kv_norm"], f"{tag}_ckv", out_dtype=BF16)
        qp = mm(cq, W["w_uq"], "nn", f"{tag}_uq", tk=384)
        kvp = mm(ckv, W["w_ukv"], "nn", f"{tag}_ukv", tk=256)
        q, k = mla_qk_fwd(qp, kvp, a[:, 640:], aux["cos"], aux["sin"], f"{tag}_qk")
        v = kvp.astype(BF16)
        o = attn_fwd(q, k, v, f"{tag}_attn", wide=True, scale=96 ** -0.5, v_off=HEADS)
        y = mm(o, W["w_o"], "nn", f"{tag}_o")
        return y, (a, cq, ckv, q, k, v, o)
    if kind == 1:
        qkv = mm(hn, W["w_qkv"], "nn", f"{tag}_qkv", out_dtype=BF16, tn=1152)
        parts = [dil_fwd(qkv, aux["dil_bias"][g], g, f"{tag}_g{g}") for g in range(3)]
        o, lse = dil_merge([p_[0] for p_ in parts], [p_[1] for p_ in parts], f"{tag}_merge")
        y = mm(o, W["w_o"], "nn", f"{tag}_o")
        return y, (qkv, o, lse)
    a = mm(hn, W["w_qkvf"], "nn", f"{tag}_qkvf", tn=640)
    fl = a[:, 3072:]
    cum = fox_gate_fwd(fl, aux["fox_b"], f"{tag}_gate")[:, :HEADS]
    cum_q = cum.reshape(S, PAIRS, 2).transpose(1, 0, 2)
    cum_k = cum.T.reshape(PAIRS, 2, S)
    ab = a.astype(BF16)
    o = attn_fwd(ab, ab, ab, f"{tag}_attn", wide=False, scale=0.125, k_off=PAIRS, v_off=2 * PAIRS, cum=cum_q, cum_t=cum_k)
    y = mm(o, W["w_o"], "nn", f"{tag}_o")
    return y, (fl, ab, cum_q, cum_k, o)


def _mixer_bwd(kind, tag, hn, dy, W, aux, saved):
    gr = {}
    if kind == 0:
        a, cq, ckv, q, k, v, o = saved
        gr["w_o"] = mm(o, dy, "tn", f"{tag}_dwo", out_dtype=BF16)
        do = mm(dy, W["w_o"], "nt", f"{tag}_do")
        dq, dk, dv = attn_bwd(q, k, v, o, do, f"{tag}_attn_b", wide=True, scale=96 ** -0.5, v_off=HEADS)
        dqp, dkr = mla_qk_bwd(dq, dk, aux["cos"], aux["sin"], f"{tag}_qk_b")
        dkvp = jnp.concatenate([dk, dv], axis=1)
        gr["w_ukv"] = mm(ckv, dkvp, "tn", f"{tag}_dwukv", out_dtype=BF16, tm=256)
        dckv = mm(dkvp, W["w_ukv"], "nt", f"{tag}_dckv", tn=256)
        gr["w_uq"] = mm(cq, dqp, "tn", f"{tag}_dwuq", out_dtype=BF16, tm=384)
        dcq = mm(dqp, W["w_uq"], "nt", f"{tag}_dcq", tn=384)
        da_q, gr["q_norm"] = rms_bwd(a[:, :Q_RANK], W["q_norm"], dcq, f"{tag}_cq_b", out_dtype=BF16)
        da_kv, gr["kv_norm"] = rms_bwd(a[:, Q_RANK:Q_RANK + KV_RANK], W["kv_norm"], dckv, f"{tag}_ckv_b",
                                       out_dtype=BF16)
        da = jnp.concatenate([da_q, da_kv, dkr], axis=1)
        gr["w_a"] = mm(hn, da, "tn", f"{tag}_dwa", out_dtype=BF16, tn=768)
        return mm(da, W["w_a"], "nt", f"{tag}_dhn", tk=768), gr
    if kind == 1:
        qkv, o, lse = saved
        gr["w_o"] = mm(o, dy, "tn", f"{tag}_dwo", out_dtype=BF16)
        do = mm(dy, W["w_o"], "nt", f"{tag}_do")
        cols, dbs = [], []
        for g in range(3):
            dq, dk, dv, db = dil_bwd(qkv, aux["dil_bias"][g], o, lse, do, g, f"{tag}_g{g}_b")
            cols += [dq, dk, dv]
            dbs.append(db)
        dqkv = jnp.concatenate(cols, axis=1)
        gr["dil_dbias"] = dbs
        gr["w_qkv"] = mm(hn, dqkv, "tn", f"{tag}_dwqkv", out_dtype=BF16, out_dev=1152, tn=1152)
        return mm(dqkv, W["w_qkv"], "nt", f"{tag}_dhn", tk=1152), gr
    fl, ab, cum_q, cum_k, o = saved
    gr["w_o"] = mm(o, dy, "tn", f"{tag}_dwo", out_dtype=BF16)
    do = mm(dy, W["w_o"], "nt", f"{tag}_do")
    dq, dk, dv, dcq, dck = attn_bwd(ab, ab, ab, o, do, f"{tag}_attn_b", wide=False, scale=0.125, out_dtype=BF16,
                                    k_off=PAIRS, v_off=2 * PAIRS, cum=cum_q, cum_t=cum_k)
    pad = ((0, 0), (0, LANES - HEADS))
    dcq = jnp.pad(dcq.transpose(1, 0, 2).reshape(S, HEADS), pad)
    dck = jnp.pad(dck.reshape(HEADS, S).T, pad)
    dfl, gr["b_f"] = fox_gate_bwd(fl, aux["fox_b"], dcq, dck, f"{tag}_gate_b")
    da = jnp.concatenate([dq, dk, dv, dfl.astype(BF16)], axis=1)
    gr["w_qkvf"] = mm(hn, da, "tn", f"{tag}_dwqkvf", out_dtype=BF16, tn=640)
    return mm(da, W["w_qkvf"], "nt", f"{tag}_dhn", tk=640), gr


def kernel(x, p, positions, norm_g, ffn_w_in, ffn_w_out, ple_w_proj, ple_w_gate, rel_bias, mla_w_a, mla_q_norm, mla_kv_norm, mla_w_uq, mla_w_ukv, mla_w_o, dil_w_qkv, dil_w_o, fox_w_qkvf, fox_b_f, fox_w_o, loss_target, m_norm_g, m_ffn_w_in, m_ffn_w_out, m_ple_w_proj, m_ple_w_gate, m_rel_bias, m_mla_w_a, m_mla_q_norm, m_mla_kv_norm, m_mla_w_uq, m_mla_w_ukv, m_mla_w_o, m_dil_w_qkv, m_dil_w_o, m_fox_w_qkvf, m_fox_b_f, m_fox_w_o, v_norm_g, v_ffn_w_in, v_ffn_w_out, v_ple_w_proj, v_ple_w_gate, v_rel_bias, v_mla_w_a, v_mla_q_norm, v_mla_kv_norm, v_mla_w_uq, v_mla_w_ukv, v_mla_w_o, v_dil_w_qkv, v_dil_w_o, v_fox_w_qkvf, v_fox_b_f, v_fox_w_o):
    given = dict(locals())
    me = 4 * lax.axis_index("x") + 2 * lax.axis_index("y") + lax.axis_index("c")

    rows_of = {n: axis == 1 for n, _, axis in BIG}
    shape_of = {n: shp for n, shp, _ in BIG}
    lists = [_layer_list(i) for i in range(DEPTH)]
    flat = [nl for ls in lists for nl in ls]
    flat_rows = [rows_of[n] for n, _ in flat]
    send_s, recv_s, shards, lands = gather_start([given[n][l:l + 1].astype(BF16) for n, l in flat], flat_rows,
                                                 "gather_start")
    full = {}

    gain_rows = _rows(sum(int(np.prod(s)) for _, s, _ in SMALL_SHARDED))
    gains, = all_gather([_pack([given[n] for n, _, _ in SMALL_SHARDED], gain_rows, F32)], [False], "gather_gains",
                        in_vmem=True)
    gains = gains.reshape(N_DEV, gain_rows * LANES)
    off = 0
    for n, shp, axis in SMALL_SHARDED:
        cnt = int(np.prod(shp))
        g = jnp.moveaxis(gains[:, off:off + cnt].reshape((N_DEV,) + shp), 0, axis)
        full[n] = g.reshape(shp[:axis] + (N_DEV * shp[axis],))
        off += cnt

    cos, sin = rope_tables(positions.reshape(S, 1), "rope_tables")
    dil_bias = [mm(rel_bias[:, HEADS * g:HEADS * (g + 1)], jnp.asarray(_bucket_onehot(DIL[g][1])), "tn",
                   f"dil_bias{g}", precise=True, tn=4096).reshape(HEADS, QBLK, 2 * QBLK) for g in range(3)]
    aux = {"cos": cos, "sin": sin, "dil_bias": dil_bias,
           "fox_b": jnp.pad(fox_b_f, ((0, 0), (0, LANES - HEADS)))}

    def layer_weights(i, h_before):
        kind, j = i % 3, i // 3
        sl = slice(sum(len(ls) for ls in lists[:i]), sum(len(ls) for ls in lists[:i + 1]))
        got = gather_wait(send_s, recv_s, sl.start, shards[sl], lands[sl], flat_rows[sl], h_before, f"gather_wait{i}")
        got = gather_relay(shards[sl], got, flat_rows[sl], f"gather_relay{i}")
        w = {n: g.reshape(1, N_DEV * shape_of[n][1], shape_of[n][2]) if rows_of[n] else g
             for (n, _), g in zip(lists[i], got)}
        W = {"g": [full["norm_g"][i, r][None, :] for r in range(4)], "w_in": Dev(w["ffn_w_in"], 0),
             "w_out": Lay(w["ffn_w_out"], 0), "w_proj": Dev(w["ple_w_proj"], 0), "w_gate": Lay(w["ple_w_gate"], 0)}
        if kind == 0:
            w_a, w_uq, w_ukv = _mla_layout(w["mla_w_a"], w["mla_w_uq"], w["mla_w_ukv"], 0)
            W.update(w_a=w_a, w_uq=w_uq, w_ukv=w_ukv, w_o=Lay(w["mla_w_o"], 0),
                     q_norm=full["mla_q_norm"][j][None, :], kv_norm=full["mla_kv_norm"][j][None, :])
        elif kind == 1:
            W.update(w_qkv=Dev(w["dil_w_qkv"], 0), w_o=Lay(w["dil_w_o"], 0))
        else:
            fox_w = jnp.pad(_cat([w["fox_w_qkvf"][dev, 0] for dev in range(N_DEV)]), ((0, 0), (0, FOX_W - 3088)))
            W.update(w_qkvf=fox_w, w_o=Lay(w["fox_w_o"], 0))
        return W

    h = x[0]
    saved, weights = [], []
    for i in range(DEPTH):
        kind, j, W = i % 3, i // 3, layer_weights(i, h)
        weights.append(W)
        t = f"l{i}"
        hn = rms_fwd(h, W["g"][0], f"{t}_n0", out_dtype=BF16)
        y, mix = _mixer_fwd(kind, f"{t}_mix", hn, W, aux)
        h1 = rms_fwd(y, W["g"][1], f"{t}_n1", res=h)
        fin = rms_fwd(h1, W["g"][2], f"{t}_n2", out_dtype=BF16)
        gu = mm(fin, W["w_in"], "nn", f"{t}_ffn_in", out_dev=FF_W, tn=FF_W)
        act = swiglu_fwd(gu, f"{t}_swiglu")
        f = mm(Dev(act, 0), W["w_out"], "nn", f"{t}_ffn_out", tk=FF_W)
        h2 = rms_fwd(f, W["g"][3], f"{t}_n3", res=h1)
        pp = mm(p[i, 0], W["w_proj"], "nn", f"{t}_ple_p", tn=LANES, tk=256)
        gt = mm(h2, W["w_gate"], "nn", f"{t}_ple_g")
        h3 = ple_fwd(h2, pp, gt, f"{t}_ple")
        saved.append((h, hn, y, h1, fin, gu, act, f, h2, pp, gt, mix))
        h = h3

    dh, loss_lanes = loss_head(h, loss_target[0], "loss_head")

    grads = {n: None for n, _, _ in BIG}
    landed = {n: (lax.empty((N_DEV,) + shp, BF16) if shp[0] > 1 else None) for n, shp, _ in BIG}
    in_flight = []
    g_norm = [[None] * 4 for _ in range(DEPTH)]
    g_qn, g_kvn = [None, None], [None, None]
    g_rel, g_bf = None, None

    def stacked(n):
        g = grads[n]
        return None if g is None else g.reshape(g.shape[0], N_DEV * g.shape[2], g.shape[3])

    def by_device(g):
        return g.reshape(g.shape[0], N_DEV, g.shape[1] // N_DEV, g.shape[2])

    def start(i, entries, mine, tag):
        names = [n for n, _ in entries]
        srcs = [mine[n] if n in mine else grads[n] for n in names]
        src_l = [0 if n in mine else i for n in names]
        land_l = [l if shape_of[n][0] > 1 else 0 for n, l in entries]
        rows = [rows_of[n] for n in names]
        s_sem, r_sem, srcs, got, token = scatter_start(srcs, src_l, [landed[n] for n in names], land_l, rows, tag)
        for n, src, ld in zip(names, srcs, got):
            landed[n] = ld
            if n in mine:
                mine[n] = src
            else:
                grads[n] = src
        in_flight.append((s_sem, r_sem, names, mine, src_l, land_l, rows))
        return token

    token = None
    for i in reversed(range(DEPTH)):
        kind, j, W = i % 3, i // 3, weights[i]
        t = f"l{i}b"
        h0, hn, y, h1, fin, gu, act, f, h2, pp, gt, mix = saved[i]
        dpp, dgt = ple_bwd(dh, pp, gt, f"{t}_ple")
        grads["ple_w_proj"] = mm(p[i, 0], dpp, "tn", f"{t}_dwp", out_dtype=BF16, out_dev=LANES, tm=256, tn=LANES,
                                 stack=(grads["ple_w_proj"], DEPTH, i), after=token)
        grads["ple_w_gate"] = by_device(mm(h2, dgt, "tn", f"{t}_dwg", out_dtype=BF16,
                                           stack=(stacked("ple_w_gate"), DEPTH, i)))
        dh2 = mm(dgt, W["w_gate"], "nt", f"{t}_dh2", add=dh)
        df, g_norm[i][3] = rms_bwd(f, W["g"][3], dh2, f"{t}_n3", out_dtype=BF16)
        grads["ffn_w_out"] = by_device(mm(Dev(act, 0), df, "tn", f"{t}_dwout", out_dtype=BF16, tm=FF_W,
                                          stack=(stacked("ffn_w_out"), DEPTH, i)))
        dact = mm(df, W["w_out"], "nt", f"{t}_dact", out_dev=FF_W, tn=FF_W)
        dgu = swiglu_bwd(gu, dact, f"{t}_swiglu")
        grads["ffn_w_in"] = mm(fin, Dev(dgu, 0), "tn", f"{t}_dwin", out_dtype=BF16, out_dev=FF_W, tn=FF_W,
                               stack=(grads["ffn_w_in"], DEPTH, i))
        token = start(i, lists[i][-4:], {}, f"scatter_ffn{i}")
        dfin = mm(Dev(dgu, 0), W["w_in"], "nt", f"{t}_dfin", after=token, tk=FF_W)
        dh1, g_norm[i][2] = rms_bwd(h1, W["g"][2], dfin, f"{t}_n2", res=dh2)
        dy, g_norm[i][1] = rms_bwd(y, W["g"][1], dh1, f"{t}_n1", out_dtype=BF16)
        dhn, gr = _mixer_bwd(kind, f"{t}_mix", hn, dy, W, aux, mix)
        dh, g_norm[i][0] = rms_bwd(h0, W["g"][0], dhn, f"{t}_n0", res=dh1)
        if kind == 0:
            d_a, d_uq, d_ukv = _mla_unlayout(gr["w_a"], gr["w_uq"], gr["w_ukv"])
            mine = {"mla_w_a": by_device(d_a[None]), "mla_w_uq": d_uq[:, None], "mla_w_ukv": d_ukv[:, None],
                    "mla_w_o": by_device(gr["w_o"][None])}
            g_qn[j], g_kvn[j] = gr["q_norm"], gr["kv_norm"]
        elif kind == 1:
            mine = {"dil_w_qkv": gr["w_qkv"], "dil_w_o": by_device(gr["w_o"][None])}
            g_rel = jnp.concatenate(
                [mm(jnp.asarray(_bucket_onehot(DIL[g][1])), gr["dil_dbias"][g].reshape(HEADS, -1), "nt",
                    f"{t}_drel{g}", precise=True, tk=4096) for g in range(3)], axis=1)
        else:
            wide = gr["w_qkvf"]
            mine = {"fox_w_qkvf": jnp.stack([wide[:, 386 * dev:386 * (dev + 1)] for dev in range(N_DEV)])[:, None],
                    "fox_w_o": by_device(gr["w_o"][None])}
            g_bf = gr["b_f"][:, :HEADS]
        token = start(i, lists[i][:-4], mine, f"scatter_mix{i}")
    grad_x = dh[None]

    own = {n: [] for n, _, _ in BIG}
    for idx, (s_sem, r_sem, names, mine, src_l, land_l, rows) in enumerate(in_flight):
        srcs = [mine[n] if n in mine else grads[n] for n in names]
        srcs, got = scatter_wait(s_sem, r_sem, srcs, src_l, [landed[n] for n in names], land_l, rows, dh,
                                 f"scatter_wait{idx}")
        for n, src, ld, sl, ll, rw in zip(names, srcs, got, src_l, land_l, rows):
            landed[n] = ld
            if n not in mine:
                grads[n] = src
            blk = lax.dynamic_index_in_dim(src, me, axis=1 if rw else 0, keepdims=False)[sl]
            own[n].append((ll, blk))
    big_out = []
    for n, _, _ in BIG:
        part = landed[n]
        for ll, blk in own[n]:
            part = lax.dynamic_update_slice(part, blk[None, None], (me, ll, 0, 0))
        big_out.append(adamw(given[n], given["m_" + n], given["v_" + n], part, f"adamw_{n}"))

    small_full = [jnp.stack([jnp.concatenate(r, axis=0) for r in g_norm]).reshape(-1),
                  jnp.concatenate(g_qn, axis=0).reshape(-1), jnp.concatenate(g_kvn, axis=0).reshape(-1),
                  g_rel.reshape(-1), g_bf.reshape(-1), loss_lanes.reshape(-1)]
    small_n = sum(a.shape[0] for a in small_full)
    small_rows = _rows(small_n)
    parts, = all_gather([_pack(small_full, small_rows, F32)], [False], "gather_small_grads", in_vmem=True)
    tot = _unpack(sum_parts(parts, "sum_small_grads"), [(4, 4, D), (2, Q_RANK), (2, KV_RANK), (32, 48), (1, 16), (LANES,)])
    loss = jnp.sum(tot[5])
    small_g = [lax.dynamic_slice_in_dim(tot[0], me * 128, 128, axis=2), lax.dynamic_slice_in_dim(tot[1], me * 48, 48, axis=1),
               lax.dynamic_slice_in_dim(tot[2], me * 32, 32, axis=1), tot[3], tot[4]]
    small_names = [n for n, _, _ in SMALL_SHARDED] + [n for n, _ in SMALL_REPL]
    small_shapes = [s for _, s, _ in SMALL_SHARDED] + [s for _, s in SMALL_REPL]
    s_rows = _rows(sum(int(np.prod(s)) for s in small_shapes))
    small_out = adamw(_pack([given[n] for n in small_names], s_rows, F32)[None],
                      _pack([given["m_" + n] for n in small_names], s_rows, F32)[None],
                      _pack([given["v_" + n] for n in small_names], s_rows, F32)[None],
                      _pack(small_g, s_rows, F32)[None, None], "adamw_small")
    small_out = [_unpack(o_, small_shapes) for o_ in small_out]

    res = [{}, {}, {}, {}]
    for k in range(4):
        for idx, (n, _, _) in enumerate(BIG):
            res[k][n] = big_out[idx][k]
        for idx, n in enumerate(small_names):
            res[k][n] = small_out[k][idx]
    return (loss, grad_x, *[res[0][n] for n in WEIGHTS], *[res[1][n] for n in WEIGHTS],
            *[res[2][n] for n in WEIGHTS], *[res[3][n] for n in WEIGHTS])
```

```python
import math
from typing import NamedTuple

import numpy as np
import jax
import jax.numpy as jnp
from jax import lax
from jax.experimental import pallas as pl
from jax.experimental.pallas import tpu as pltpu

F32 = jnp.float32
BF16 = jnp.bfloat16
MESH_ID = pl.DeviceIdType.MESH

N_DEV = 8
S = 2048
D = 1024
DEPTH = 4
D_FF = 2816
D_PLE = 256
EPS = 1e-6
NEG = -1e30
LANES = 128
HEADS = 16
PAIRS = 8
Q_RANK = 384
KV_RANK = 256
QBLK = 128
DIL = ((128, 1), (512, 4), (2048, 16))
REL_BUCKETS = 32
FOX_W = 3200
VMEM_LIMIT = 56 * 1024 * 1024

ADAM_LR, ADAM_B1, ADAM_B2, ADAM_EPS, ADAM_WD, ADAM_STEP = 1e-3, 0.9, 0.999, 1e-8, 0.01, 10


BIG = (
    ("ffn_w_in", (4, 1024, 704), 2), ("ffn_w_out", (4, 352, 1024), 1),
    ("ple_w_proj", (4, 256, 128), 2), ("ple_w_gate", (4, 128, 1024), 1),
    ("mla_w_a", (2, 128, 672), 1), ("mla_w_uq", (2, 384, 192), 2),
    ("mla_w_ukv", (2, 256, 256), 2), ("mla_w_o", (2, 128, 1024), 1),
    ("dil_w_qkv", (1, 1024, 1152), 2), ("dil_w_o", (1, 128, 1024), 1),
    ("fox_w_qkvf", (1, 1024, 386), 2), ("fox_w_o", (1, 128, 1024), 1),
)
SMALL_SHARDED = (("norm_g", (4, 4, 128), 2), ("mla_q_norm", (2, 48), 1), ("mla_kv_norm", (2, 32), 1))
SMALL_REPL = (("rel_bias", (32, 48)), ("fox_b_f", (1, 16)))
WEIGHTS = ("norm_g", "ffn_w_in", "ffn_w_out", "ple_w_proj", "ple_w_gate", "rel_bias", "mla_w_a", "mla_q_norm",
           "mla_kv_norm", "mla_w_uq", "mla_w_ukv", "mla_w_o", "dil_w_qkv", "dil_w_o", "fox_w_qkvf", "fox_b_f",
           "fox_w_o")


def _rows(n):
    return -(-n // (8 * LANES)) * 8


def _t5_bucket_np(dist):
    max_exact = REL_BUCKETS // 2
    n = np.maximum(dist.astype(np.float32), np.float32(1.0))
    large = max_exact + (np.log(n / np.float32(max_exact)) / np.float32(math.log(2048 / max_exact))
                         * np.float32(REL_BUCKETS - max_exact)).astype(np.int32)
    large = np.minimum(large, REL_BUCKETS - 1)
    return np.where(dist < max_exact, dist, large)


def _bucket_onehot(dilation):
    i = np.arange(QBLK)[:, None]
    j = np.arange(2 * QBLK)[None, :]
    bucket = _t5_bucket_np(np.clip(QBLK + i - j, 0, None) * dilation).reshape(-1)
    return (np.arange(REL_BUCKETS)[:, None] == bucket[None, :]).astype(np.float32)


def _rope_inv_lanes():
    half = 16
    inv = (np.float32(10000.0) ** (-np.arange(half, dtype=np.float32) / np.float32(half))).astype(np.float32)
    t = np.zeros((1, LANES), np.float32)
    t[0, 0:16] = inv
    t[0, 64:80] = inv
    return t


def _params(sem=None):
    return pltpu.CompilerParams(dimension_semantics=sem, vmem_limit_bytes=VMEM_LIMIT)


def _tile(dim, target):
    if dim <= target or dim % target == 0:
        return min(dim, target)
    t = (target // LANES) * LANES
    while dim % t:
        t -= LANES
    return t


_DIMS = {"nn": (((1,), (0,)), ((), ())), "nt": (((1,), (1,)), ((), ())), "tn": (((0,), (0,)), ((), ()))}


class Lay(NamedTuple):
    arr: jax.Array
    l: int


class Dev(NamedTuple):
    arr: jax.Array
    l: int


def _lshape(op):
    if isinstance(op, Dev):
        g, _, r, w = op.arr.shape
        return r, g * w
    return op.arr.shape[1:] if isinstance(op, Lay) else op.shape


def _op_spec(op, rows_t, cols_t, row_ix, col_ix):
    if isinstance(op, Dev):
        w = op.arr.shape[3]
        assert w % cols_t == 0 and (cols_t % LANES == 0 or cols_t == w), (w, cols_t)
        nb, l = w // cols_t, op.l
        return pl.BlockSpec((1, 1, rows_t, cols_t),
                            lambda i, j, k: (col_ix(i, j, k) // nb, l, row_ix(i, j, k), col_ix(i, j, k) % nb))
    if isinstance(op, Lay):
        l = op.l
        return pl.BlockSpec((1, rows_t, cols_t), lambda i, j, k: (l, row_ix(i, j, k), col_ix(i, j, k)))
    return pl.BlockSpec((rows_t, cols_t), lambda i, j, k: (row_ix(i, j, k), col_ix(i, j, k)))


def _mat(ref):
    return ref[(0,) * (len(ref.shape) - 2)]


def mm(a, b, mode, name, out_dtype=F32, precise=False, add=None, out_dev=None, stack=None, after=None,
       tm=1024, tn=512, tk=2048):
    (ar, ac), (br, bc) = _lshape(a), _lshape(b)
    M, K = (ac, ar) if mode == "tn" else (ar, ac)
    N = br if mode == "nt" else bc
    assert K == (bc if mode == "nt" else br)
    tm, tn, tk = _tile(M, tm), _tile(N, tn), _tile(K, tk)
    nk = K // tk
    ix_i, ix_j, ix_k = (lambda i, j, k: i), (lambda i, j, k: j), (lambda i, j, k: k)
    a_spec = _op_spec(a, tk, tm, ix_k, ix_i) if mode == "tn" else _op_spec(a, tm, tk, ix_i, ix_k)
    b_spec = _op_spec(b, tn, tk, ix_j, ix_k) if mode == "nt" else _op_spec(b, tk, tn, ix_k, ix_j)
    buf, n_l, l = stack if stack is not None else (None, 1, 0)
    if out_dev is not None:
        out = Dev(jax.ShapeDtypeStruct((N // out_dev, n_l, M, out_dev), out_dtype), l)
    elif stack is not None:
        out = Lay(jax.ShapeDtypeStruct((n_l, M, N), out_dtype), l)
    else:
        out = jax.ShapeDtypeStruct((M, N), out_dtype)
    o_spec = _op_spec(out, tm, tn, ix_i, ix_j)
    n_in = 3 if add is not None else 2

    def body(*refs):
        a_ref, b_ref = refs[0], refs[1]
        o_ref = refs[n_in + (buf is not None) + (after is not None)]
        if precise:
            part = lax.dot_general(_mat(a_ref), _mat(b_ref), _DIMS[mode], precision=lax.Precision.HIGHEST,
                                   preferred_element_type=F32)
        else:
            part = lax.dot_general(_mat(a_ref).astype(BF16), _mat(b_ref).astype(BF16), _DIMS[mode],
                                   preferred_element_type=F32)

        def finish(r):
            r = r + refs[2][...] if add is not None else r
            o_ref[...] = r.astype(o_ref.dtype).reshape(o_ref.shape)

        if nk == 1:
            finish(part)
            return
        acc, k = refs[-1], pl.program_id(2)

        @pl.when(k == 0)
        def _():
            acc[...] = part

        @pl.when(k > 0)
        def _():
            acc[...] += part

        @pl.when(k == nk - 1)
        def _():
            finish(acc[...])

    ins = [getattr(a, "arr", a), getattr(b, "arr", b)] + ([add] if add is not None else [])
    in_specs = [a_spec, b_spec] + ([o_spec] if add is not None else [])
    aliases = {}
    if buf is not None:
        ins.append(buf)
        in_specs.append(pl.BlockSpec(memory_space=pl.ANY))
        aliases = {n_in: 0}
    if after is not None:
        ins.append(after)
        in_specs.append(pl.BlockSpec(memory_space=pl.ANY))
    return pl.pallas_call(
        body, name=name, grid=(M // tm, N // tn, nk), in_specs=in_specs, out_specs=o_spec,
        out_shape=getattr(out, "arr", out), input_output_aliases=aliases,
        scratch_shapes=[pltpu.VMEM((tm, tn), F32)] if nk > 1 else [],
        compiler_params=_params(("parallel", "parallel", "arbitrary")),
    )(*ins)


def _rows_call(body, name, ins, outs, tr=256, acc_outs=()):
    n = ins[0].shape[0]
    tr = min(tr, n)
    in_specs = [pl.BlockSpec((tr, a.shape[1]), lambda i: (i, 0)) if a.shape[0] == n else
                pl.BlockSpec(a.shape, lambda i: (0, 0)) for a in ins]
    out_specs = [pl.BlockSpec((tr, w), lambda i: (i, 0)) for w, _ in outs] + \
                [pl.BlockSpec((1, w), lambda i: (0, 0)) for w in acc_outs]
    out_shape = [jax.ShapeDtypeStruct((n, w), dt) for w, dt in outs] + \
                [jax.ShapeDtypeStruct((1, w), F32) for w in acc_outs]
    res = pl.pallas_call(body, name=name, grid=(n // tr,), in_specs=in_specs, out_specs=out_specs,
                         out_shape=out_shape, compiler_params=_params(("arbitrary",)))(*ins)
    return res[0] if len(res) == 1 else res


def _acc(ref, val):
    @pl.when(pl.program_id(0) == 0)
    def _():
        ref[...] = jnp.zeros_like(ref)

    ref[...] += val


def rms_fwd(x, g, name, res=None, out_dtype=F32):
    def body(*refs):
        x_ref, g_ref = refs[0], refs[1]
        o_ref = refs[-1]
        xv = x_ref[...]
        y = xv * lax.rsqrt(jnp.mean(xv * xv, axis=-1, keepdims=True) + EPS) * g_ref[...]
        o_ref[...] = (y + refs[2][...] if res is not None else y).astype(o_ref.dtype)

    ins = [x, g] + ([res] if res is not None else [])
    return _rows_call(body, name, ins, [(x.shape[1], out_dtype)])


def rms_bwd(x, g, dy, name, res=None, out_dtype=F32, after=None):
    def body(*refs):
        x_ref, g_ref, dy_ref = refs[:3]
        dx_ref, dg_ref = refs[-2], refs[-1]
        xv, dyv = x_ref[...], dy_ref[...]
        r = lax.rsqrt(jnp.mean(xv * xv, axis=-1, keepdims=True) + EPS)
        xh = xv * r
        dxh = dyv * g_ref[...]
        dx = r * (dxh - xh * jnp.mean(dxh * xh, axis=-1, keepdims=True))
        dx_ref[...] = (dx + refs[3][...] if res is not None else dx).astype(dx_ref.dtype)
        _acc(dg_ref, jnp.sum(dyv * xh, axis=0, keepdims=True))

    ins = [x, g, dy] + ([res] if res is not None else []) + ([after] if after is not None else [])
    return _rows_call(body, name, ins, [(x.shape[1], out_dtype)], acc_outs=(x.shape[1],))


def _sigmoid(x):
    return 0.5 * jnp.tanh(0.5 * x) + 0.5


FF_W = 704
FF_TR = 512


def _ff_spec(shift):
    return pl.BlockSpec((1, 1, FF_TR, FF_W), lambda d, i: (d + shift, 0, i, 0))


def swiglu_fwd(gu, name):
    def body(g_ref, u_ref, o_ref):
        gate = g_ref[...]
        o_ref[...] = (gate * _sigmoid(gate) * u_ref[...]).astype(BF16)

    return pl.pallas_call(body, name=name, grid=(4, S // FF_TR), in_specs=[_ff_spec(0), _ff_spec(4)],
                          out_specs=_ff_spec(0), out_shape=jax.ShapeDtypeStruct((4, 1, S, FF_W), BF16),
                          compiler_params=_params(("parallel", "parallel")))(gu, gu)


def swiglu_bwd(gu, dact, name):
    def body(g_ref, u_ref, d_ref, o_ref):
        gate, d = g_ref[...], d_ref[...]
        sg = _sigmoid(gate)

        @pl.when(pl.program_id(0) < 4)
        def _():
            o_ref[...] = (d * u_ref[...] * sg * (1.0 + gate * (1.0 - sg))).astype(BF16)

        @pl.when(pl.program_id(0) >= 4)
        def _():
            o_ref[...] = (d * gate * sg).astype(BF16)

    def half(shift):
        return pl.BlockSpec((1, 1, FF_TR, FF_W), lambda d, i: (d % 4 + shift, 0, i, 0))

    return pl.pallas_call(body, name=name, grid=(8, S // FF_TR), in_specs=[half(0), half(4), half(0)],
                          out_specs=_ff_spec(0), out_shape=jax.ShapeDtypeStruct((8, 1, S, FF_W), BF16),
                          compiler_params=_params(("parallel", "parallel")))(gu, gu, dact)


def ple_fwd(h, pp, gt, name):
    def body(h_ref, p_ref, g_ref, o_ref):
        o_ref[...] = h_ref[...] + p_ref[...] * _sigmoid(g_ref[...])

    return _rows_call(body, name, [h, pp, gt], [(D, F32)])


def ple_bwd(dh, pp, gt, name):
    def body(d_ref, p_ref, g_ref, dp_ref, dg_ref):
        d, sg = d_ref[...], _sigmoid(g_ref[...])
        dp_ref[...] = (d * sg).astype(BF16)
        dg_ref[...] = (d * p_ref[...] * sg * (1.0 - sg)).astype(BF16)

    return _rows_call(body, name, [dh, pp, gt], [(D, BF16), (D, BF16)])


def loss_head(y, target, name):
    def body(y_ref, t_ref, d_ref, l_ref):
        e = y_ref[...] - t_ref[...]
        d_ref[...] = e * (1.0 / D)
        col = jnp.sum(e * e, axis=0, keepdims=True) * (0.5 / D)
        _acc(l_ref, sum(col[:, LANES * c:LANES * (c + 1)] for c in range(D // LANES)))

    return _rows_call(body, name, [y, target], [(D, F32)], acc_outs=(LANES,))


def rope_tables(pos_col, name):
    inv = jnp.asarray(_rope_inv_lanes())

    def body(p_ref, inv_ref, c_ref, s_ref):
        ang = p_ref[...].astype(F32) * inv_ref[...]
        lane = lax.broadcasted_iota(jnp.int32, ang.shape, 1)
        first, second = lane < 16, (lane >= 64) & (lane < 80)
        c_ref[...] = jnp.where(first | second, jnp.cos(ang), 1.0)
        sn = jnp.sin(ang)
        s_ref[...] = jnp.where(first, -sn, jnp.where(second, sn, 0.0))

    return _rows_call(body, name, [pos_col, inv], [(LANES, F32), (LANES, F32)])


def _rope(x, c, s):
    return x * c + pltpu.roll(x, 64, axis=1) * s


def _rope_t(d, c, s):
    return d * c + pltpu.roll(d * s, 64, axis=1)


def mla_qk_fwd(qp, kvp, kr, cos, sin, name):
    def body(q_ref, k_ref, kr_ref, c_ref, s_ref, qo_ref, ko_ref):
        c, s = c_ref[...], s_ref[...]
        kr_rot = _rope(kr_ref[...], c, s)
        for h in range(HEADS):
            sl = slice(LANES * h, LANES * (h + 1))
            qo_ref[:, sl] = _rope(q_ref[:, sl], c, s).astype(BF16)
            ko_ref[:, sl] = (k_ref[:, sl] + kr_rot).astype(BF16)

    n = qp.shape[0]
    tr = 256
    w = HEADS * LANES
    return pl.pallas_call(
        body, name=name, grid=(n // tr,),
        in_specs=[pl.BlockSpec((tr, w), lambda i: (i, 0)), pl.BlockSpec((tr, w), lambda i: (i, 0)),
                  pl.BlockSpec((tr, LANES), lambda i: (i, 0)), pl.BlockSpec((tr, LANES), lambda i: (i, 0)),
                  pl.BlockSpec((tr, LANES), lambda i: (i, 0))],
        out_specs=[pl.BlockSpec((tr, w), lambda i: (i, 0))] * 2,
        out_shape=[jax.ShapeDtypeStruct((n, w), BF16)] * 2, compiler_params=_params(("arbitrary",)),
    )(qp, kvp, kr, cos, sin)


def mla_qk_bwd(dq, dk, cos, sin, name):
    def body(dq_ref, dk_ref, c_ref, s_ref, dqp_ref, dkr_ref):
        c, s = c_ref[...], s_ref[...]
        tot = jnp.zeros(c.shape, F32)
        for h in range(HEADS):
            sl = slice(LANES * h, LANES * (h + 1))
            dqp_ref[:, sl] = _rope_t(dq_ref[:, sl], c, s).astype(BF16)
            tot = tot + dk_ref[:, sl]
        dkr_ref[...] = _rope_t(tot, c, s).astype(BF16)

    return _rows_call(body, name, [dq, dk, cos, sin], [(HEADS * LANES, BF16), (LANES, BF16)])


TQ = 256


def _pair_masks(shape):
    lane = lax.broadcasted_iota(jnp.int32, shape, 1)
    return (lane < 64, lane >= 64)


def _causal_probs(q_a, k_a, scale, b0, cq, ck):
    s = lax.dot_general(q_a, k_a, _DIMS["nt"], preferred_element_type=F32) * scale
    if cq is not None:
        s = s + (cq - ck)
    row = lax.broadcasted_iota(jnp.int32, s.shape, 0) + b0
    col = lax.broadcasted_iota(jnp.int32, s.shape, 1)
    s = jnp.where(col <= row, s, NEG)
    e = jnp.exp(s - jnp.max(s, axis=-1, keepdims=True))
    return e / jnp.sum(e, axis=-1, keepdims=True)


def attn_fwd(q, k, v, name, *, wide, scale, q_off=0, k_off=0, v_off=0, cum=None, cum_t=None):
    qw = 2 * LANES if wide else LANES
    forget = cum is not None

    def body(*refs):
        q_ref, k_ref, v_ref = refs[:3]
        o_ref = refs[-1]
        m0, m1 = _pair_masks((TQ, LANES))
        for qi in range(S // TQ):
            b0, b1 = qi * TQ, (qi + 1) * TQ
            outs = []
            for a, msk in enumerate((m0, m1)):
                if wide:
                    q_a, k_a = q_ref[b0:b1, LANES * a:LANES * (a + 1)], k_ref[:b1, LANES * a:LANES * (a + 1)]
                else:
                    q_a, k_a = jnp.where(msk, q_ref[b0:b1, :], jnp.zeros((), BF16)), k_ref[:b1, :]
                cq = refs[3][0, b0:b1, a:a + 1] if forget else None
                ck = refs[4][0, a:a + 1, :b1] if forget else None
                p = _causal_probs(q_a, k_a, scale, b0, cq, ck)
                outs.append(jnp.dot(p.astype(BF16), v_ref[:b1, :], preferred_element_type=F32))
            o_ref[b0:b1, :] = jnp.where(m0, outs[0], outs[1])

    in_specs = [pl.BlockSpec((S, qw), lambda h: (0, q_off * LANES // qw + h)),
                pl.BlockSpec((S, qw), lambda h: (0, k_off * LANES // qw + h)),
                pl.BlockSpec((S, LANES), lambda h: (0, v_off + h))]
    ins = [q, k, v]
    if forget:
        in_specs += [pl.BlockSpec((1, S, 2), lambda h: (h, 0, 0)), pl.BlockSpec((1, 2, S), lambda h: (h, 0, 0))]
        ins += [cum, cum_t]
    return pl.pallas_call(
        body, name=name, grid=(PAIRS,), in_specs=in_specs, out_specs=pl.BlockSpec((S, LANES), lambda h: (0, h)),
        out_shape=jax.ShapeDtypeStruct((S, PAIRS * LANES), F32), compiler_params=_params(("arbitrary",)),
    )(*ins)


def attn_bwd(q, k, v, o, do, name, *, wide, scale, out_dtype=F32, q_off=0, k_off=0, v_off=0, cum=None, cum_t=None):
    qw = 2 * LANES if wide else LANES
    forget = cum is not None

    def body(*refs):
        q_ref, k_ref, v_ref, o_ref, do_ref = refs[:5]
        n_out = 5 if forget else 3
        outs = refs[-(n_out + 2):-2]
        dq_ref, dk_ref, dv_ref = outs[:3]
        dk_acc, dv_acc = refs[-2], refs[-1]
        dk_acc[...] = jnp.zeros_like(dk_acc)
        dv_acc[...] = jnp.zeros_like(dv_acc)
        if forget:
            dcq_ref, dck_ref = outs[3], outs[4]
            dck_ref[...] = jnp.zeros_like(dck_ref)
        m0, m1 = _pair_masks((TQ, LANES))
        for qi in range(S // TQ):
            b0, b1 = qi * TQ, (qi + 1) * TQ
            do2 = do_ref[b0:b1, :]
            dd = do2 * o_ref[b0:b1, :]
            do_b = do2.astype(BF16)
            mk0, mk1 = _pair_masks((b1, LANES))
            dqs = []
            for a, (msk, mk) in enumerate(((m0, mk0), (m1, mk1))):
                lanes = slice(LANES * a, LANES * (a + 1)) if wide else slice(0, LANES)
                if wide:
                    q_a, k_a = q_ref[b0:b1, lanes], k_ref[:b1, lanes]
                else:
                    q_a, k_a = jnp.where(msk, q_ref[b0:b1, :], jnp.zeros((), BF16)), k_ref[:b1, :]
                cq = refs[5][0, b0:b1, a:a + 1] if forget else None
                ck = refs[6][0, a:a + 1, :b1] if forget else None
                p = _causal_probs(q_a, k_a, scale, b0, cq, ck)
                dp = lax.dot_general(jnp.where(msk, do_b, jnp.zeros((), BF16)), v_ref[:b1, :], _DIMS["nt"],
                                     preferred_element_type=F32)
                delta = jnp.sum(jnp.where(msk, dd, 0.0), axis=-1, keepdims=True)
                ds = p * (dp - delta)
                if forget:
                    dcq_ref[0, b0:b1, a:a + 1] = jnp.sum(ds, axis=-1, keepdims=True)
                    dck_ref[0, a:a + 1, :b1] -= jnp.sum(ds, axis=0, keepdims=True)
                ds_b = (ds * scale).astype(BF16)
                dqs.append(jnp.dot(ds_b, k_a, preferred_element_type=F32))
                dk_acc[:b1, lanes] += lax.dot_general(ds_b, q_a, _DIMS["tn"], preferred_element_type=F32)
                dv_acc[:b1, :] += jnp.where(mk, lax.dot_general(p.astype(BF16), do_b, _DIMS["tn"],
                                                                 preferred_element_type=F32), 0.0)
            if wide:
                dq_ref[b0:b1, :LANES] = dqs[0].astype(out_dtype)
                dq_ref[b0:b1, LANES:] = dqs[1].astype(out_dtype)
            else:
                dq_ref[b0:b1, :] = jnp.where(m0, dqs[0], dqs[1]).astype(out_dtype)
        dk_ref[...] = dk_acc[...].astype(out_dtype)
        dv_ref[...] = dv_acc[...].astype(out_dtype)

    pair = pl.BlockSpec((S, LANES), lambda h: (0, h))
    qk_out = pl.BlockSpec((S, qw), lambda h: (0, h))
    in_specs = [pl.BlockSpec((S, qw), lambda h: (0, q_off * LANES // qw + h)),
                pl.BlockSpec((S, qw), lambda h: (0, k_off * LANES // qw + h)),
                pl.BlockSpec((S, LANES), lambda h: (0, v_off + h)), pair, pair]
    ins = [q, k, v, o, do]
    out_specs = [qk_out, qk_out, pair]
    out_shape = [jax.ShapeDtypeStruct((S, PAIRS * qw), out_dtype)] * 2 + \
                [jax.ShapeDtypeStruct((S, PAIRS * LANES), out_dtype)]
    if forget:
        by_q, by_k = pl.BlockSpec((1, S, 2), lambda h: (h, 0, 0)), pl.BlockSpec((1, 2, S), lambda h: (h, 0, 0))
        in_specs += [by_q, by_k]
        ins += [cum, cum_t]
        out_specs += [by_q, by_k]
        out_shape += [jax.ShapeDtypeStruct((PAIRS, S, 2), F32), jax.ShapeDtypeStruct((PAIRS, 2, S), F32)]
    return pl.pallas_call(
        body, name=name, grid=(PAIRS,), in_specs=in_specs, out_specs=out_specs, out_shape=out_shape,
        scratch_shapes=[pltpu.VMEM((S, qw), F32), pltpu.VMEM((S, LANES), F32)],
        compiler_params=_params(("arbitrary",)),
    )(*ins)


def _tri(lower):
    r = lax.broadcasted_iota(jnp.int32, (QBLK, QBLK), 0)
    c = lax.broadcasted_iota(jnp.int32, (QBLK, QBLK), 1)
    return jnp.where((c <= r) if lower else (c >= r), 1.0, 0.0).astype(F32)


def _hi_dot(a, b):
    return jnp.dot(a, b, precision=lax.Precision.HIGHEST, preferred_element_type=F32)


def fox_gate_fwd(fl, bias, name):
    def body(f_ref, b_ref, o_ref):
        tri = _tri(True)
        carry = jnp.zeros((1, LANES), F32)
        for n in range(S // QBLK):
            x = f_ref[n * QBLK:(n + 1) * QBLK, :].astype(F32) + b_ref[...]
            lf = jnp.minimum(x, 0.0) - jnp.log(1.0 + jnp.exp(-jnp.abs(x)))
            c = _hi_dot(tri, lf) + carry
            o_ref[n * QBLK:(n + 1) * QBLK, :] = c
            carry = c[QBLK - 1:QBLK, :]

    return pl.pallas_call(body, name=name, out_shape=jax.ShapeDtypeStruct((S, LANES), F32),
                          compiler_params=_params())(fl, bias)


def fox_gate_bwd(fl, bias, dcq, dck, name):
    def body(f_ref, b_ref, dq_ref, dk_ref, o_ref, db_ref):
        tri = _tri(False)
        carry = jnp.zeros((1, LANES), F32)
        db = jnp.zeros((1, LANES), F32)
        for n in reversed(range(S // QBLK)):
            rows = slice(n * QBLK, (n + 1) * QBLK)
            dlf = _hi_dot(tri, dq_ref[rows, :] + dk_ref[rows, :]) + carry
            carry = dlf[0:1, :]
            x = f_ref[rows, :].astype(F32) + b_ref[...]
            dx = dlf * (1.0 - _sigmoid(x))
            o_ref[rows, :] = dx
            db = db + jnp.sum(dx, axis=0, keepdims=True)
        db_ref[...] = db

    return pl.pallas_call(body, name=name, out_shape=[jax.ShapeDtypeStruct((S, LANES), F32),
                                                      jax.ShapeDtypeStruct((1, LANES), F32)],
                          compiler_params=_params())(fl, bias, dcq, dck)


def _band_valid(first):
    w = QBLK if first else 2 * QBLK
    i = lax.broadcasted_iota(jnp.int32, (QBLK, w), 0)
    j = lax.broadcasted_iota(jnp.int32, (QBLK, w), 1)
    return (j <= i) if first else ((j >= i) & (j - QBLK <= i))


def _band_logits(q_a, kk, bias, first):
    s = lax.dot_general(q_a, kk, _DIMS["nt"], preferred_element_type=F32) * 0.125 + bias
    return jnp.where(_band_valid(first), s, NEG)


def dil_fwd(qkv, bias, g, name):
    d = DIL[g][1]
    ls = S // d
    view = qkv.reshape(ls, d * 9216)

    def body(q_ref, k_ref, v_ref, b_ref, o_ref, l_ref):
        m0, m1 = _pair_masks((QBLK, LANES))
        for n in range(ls // QBLK):
            rows = slice(n * QBLK, (n + 1) * QBLK)
            keys = rows if n == 0 else slice((n - 1) * QBLK, (n + 1) * QBLK)
            os_, ls_ = [], []
            for a, msk in enumerate((m0, m1)):
                q_a = jnp.where(msk, q_ref[rows, :], jnp.zeros((), BF16))
                bias_a = b_ref[a, :, QBLK:] if n == 0 else b_ref[a]
                s = _band_logits(q_a, k_ref[keys, :], bias_a, n == 0)
                mx = jnp.max(s, axis=-1, keepdims=True)
                e = jnp.exp(s - mx)
                l = jnp.sum(e, axis=-1, keepdims=True)
                os_.append(jnp.dot((e / l).astype(BF16), v_ref[keys, :], preferred_element_type=F32))
                ls_.append(mx + jnp.log(l))
            o_ref[rows, :] = jnp.where(m0, os_[0], os_[1])
            l_ref[rows, :] = jnp.where(m0, ls_[0], ls_[1])

    def col(j):
        return lambda h, r: (0, r * 72 + g * 24 + j * 8 + h)

    out = pl.BlockSpec((ls, LANES), lambda h, r: (0, r * 8 + h))
    o, lse = pl.pallas_call(
        body, name=name, grid=(PAIRS, d),
        in_specs=[pl.BlockSpec((ls, LANES), col(0)), pl.BlockSpec((ls, LANES), col(1)), pl.BlockSpec((ls, LANES), col(2)),
                  pl.BlockSpec((2, QBLK, 2 * QBLK), lambda h, r: (h, 0, 0))],
        out_specs=[out, out], out_shape=[jax.ShapeDtypeStruct((ls, d * D), F32)] * 2,
        compiler_params=_params(("arbitrary", "arbitrary")),
    )(view, view, view, bias)
    return o.reshape(S, D), lse.reshape(S, D)


def dil_merge(os_, lses, name):
    def body(o0, o1, o2, l0, l1, l2, o_ref, l_ref):
        ls_ = [l0[...], l1[...], l2[...]]
        mx = jnp.maximum(jnp.maximum(ls_[0], ls_[1]), ls_[2])
        tot = mx + jnp.log(sum(jnp.exp(l - mx) for l in ls_))
        o_ref[...] = sum(jnp.exp(l - tot) * o[...] for l, o in zip(ls_, (o0, o1, o2)))
        l_ref[...] = tot

    return _rows_call(body, name, list(os_) + list(lses), [(D, F32), (D, F32)])


def dil_bwd(qkv, bias, o, lse, do, g, name):
    d = DIL[g][1]
    ls = S // d
    view = qkv.reshape(ls, d * 9216)
    o, lse, do = (t.reshape(ls, d * D) for t in (o, lse, do))

    def body(q_ref, k_ref, v_ref, b_ref, o_ref, l_ref, do_ref, dq_ref, dk_ref, dv_ref, db_ref, dk_acc, dv_acc):
        @pl.when(pl.program_id(1) == 0)
        def _():
            db_ref[...] = jnp.zeros_like(db_ref)

        dk_acc[...] = jnp.zeros_like(dk_acc)
        dv_acc[...] = jnp.zeros_like(dv_acc)
        m0, m1 = _pair_masks((QBLK, LANES))
        for n in range(ls // QBLK):
            rows = slice(n * QBLK, (n + 1) * QBLK)
            keys = rows if n == 0 else slice((n - 1) * QBLK, (n + 1) * QBLK)
            nk = QBLK if n == 0 else 2 * QBLK
            do2, lse2 = do_ref[rows, :], l_ref[rows, :]
            dd = do2 * o_ref[rows, :]
            do_b = do2.astype(BF16)
            mk0, mk1 = _pair_masks((nk, LANES))
            dqs = []
            for a, (msk, mk) in enumerate(((m0, mk0), (m1, mk1))):
                q_a = jnp.where(msk, q_ref[rows, :], jnp.zeros((), BF16))
                kk = k_ref[keys, :]
                bias_a = b_ref[a, :, QBLK:] if n == 0 else b_ref[a]
                s = _band_logits(q_a, kk, bias_a, n == 0)
                lse_a = jnp.max(jnp.where(msk, lse2, -jnp.inf), axis=-1, keepdims=True)
                p = jnp.exp(s - lse_a)
                dp = lax.dot_general(jnp.where(msk, do_b, jnp.zeros((), BF16)), v_ref[keys, :], _DIMS["nt"],
                                     preferred_element_type=F32)
                delta = jnp.sum(jnp.where(msk, dd, 0.0), axis=-1, keepdims=True)
                ds = p * (dp - delta)
                if n == 0:
                    db_ref[a, :, QBLK:] += ds
                else:
                    db_ref[a] += ds
                ds_b = (ds * 0.125).astype(BF16)
                dqs.append(jnp.dot(ds_b, kk, preferred_element_type=F32))
                dk_acc[keys, :] += lax.dot_general(ds_b, q_a, _DIMS["tn"], preferred_element_type=F32)
                dv_acc[keys, :] += jnp.where(mk, lax.dot_general(p.astype(BF16), do_b, _DIMS["tn"],
                                                                 preferred_element_type=F32), 0.0)
            dq_ref[rows, :] = jnp.where(m0, dqs[0], dqs[1]).astype(BF16)
        dk_ref[...] = dk_acc[...].astype(BF16)
        dv_ref[...] = dv_acc[...].astype(BF16)

    def col(j):
        return lambda h, r: (0, r * 72 + g * 24 + j * 8 + h)

    nat = pl.BlockSpec((ls, LANES), lambda h, r: (0, r * 8 + h))
    b_spec = pl.BlockSpec((2, QBLK, 2 * QBLK), lambda h, r: (h, 0, 0))
    dq, dk, dv, db = pl.pallas_call(
        body, name=name, grid=(PAIRS, d),
        in_specs=[pl.BlockSpec((ls, LANES), col(0)), pl.BlockSpec((ls, LANES), col(1)), pl.BlockSpec((ls, LANES), col(2)),
                  b_spec, nat, nat, nat],
        out_specs=[nat, nat, nat, b_spec],
        out_shape=[jax.ShapeDtypeStruct((ls, d * D), BF16)] * 3 + [jax.ShapeDtypeStruct((HEADS, QBLK, 2 * QBLK), F32)],
        scratch_shapes=[pltpu.VMEM((ls, LANES), F32), pltpu.VMEM((ls, LANES), F32)],
        compiler_params=_params(("arbitrary", "arbitrary")),
    )(view, view, view, bias, o, lse, do)
    return dq.reshape(S, D), dk.reshape(S, D), dv.reshape(S, D), db


def _place():
    x, y, c = lax.axis_index("x"), lax.axis_index("y"), lax.axis_index("c")
    return x, y, c


def _dev_slot(ref, by_rows, dev):
    return ref.at[:, dev] if by_rows else ref.at[dev]


def all_gather(shards, by_rows, name, in_vmem=False):
    n = len(shards)

    def body(*refs):
        x_refs, out_refs = refs[:n], refs[n:2 * n]
        send_sems, recv_sems, local_sems = refs[2 * n:]
        x, y, c = _place()
        me, sibling = (x, y, c), (x, y, 1 - c)
        chips = [(1 - x, y), (x, 1 - y), (1 - x, 1 - y)]

        def slot(t, px, py, pc):
            return _dev_slot(out_refs[t], by_rows[t], 4 * px + 2 * py + pc)

        def copy(t, k, blk, to, src=None):
            return pltpu.make_async_remote_copy(
                src_ref=slot(t, *blk) if src is None else src, dst_ref=slot(t, *blk), send_sem=send_sems.at[7 * t + k],
                recv_sem=recv_sems.at[7 * t + k], device_id=to, device_id_type=MESH_ID)

        mine = [pltpu.make_async_copy(x_refs[t], slot(t, *me), local_sems.at[t]) for t in range(n)]
        for cp in mine:
            cp.start()
        first = []
        for t in range(n):
            first.append(copy(t, 0, me, sibling, src=x_refs[t]))
            first += [copy(t, 1 + j, me, (*chip, c), src=x_refs[t]) for j, chip in enumerate(chips)]
        for cp in first:
            cp.start()
        passed = []
        for j, chip in enumerate(chips):
            for t in range(n):
                copy(t, 1 + j, (*chip, c), me).wait_recv()
                passed.append(copy(t, 4 + j, (*chip, c), sibling))
                passed[-1].start()
        for t in range(n):
            copy(t, 0, sibling, me).wait_recv()
            for j, chip in enumerate(chips):
                copy(t, 4 + j, (*chip, 1 - c), me).wait_recv()
        for cp in first + passed:
            cp.wait_send()
        for cp in mine:
            cp.wait()

    def gathered(s, rows):
        shp = (s.shape[0], N_DEV) + s.shape[1:] if rows else (N_DEV,) + s.shape
        return jax.ShapeDtypeStruct(shp, s.dtype)

    space = pl.BlockSpec(memory_space=pltpu.VMEM if in_vmem else pl.ANY)
    return pl.pallas_call(
        body, name=name, out_shape=[gathered(s, r) for s, r in zip(shards, by_rows)],
        in_specs=[space] * n, out_specs=[space] * n,
        scratch_shapes=[pltpu.SemaphoreType.DMA((7 * n,)), pltpu.SemaphoreType.DMA((7 * n,)),
                        pltpu.SemaphoreType.DMA((n,))],
        compiler_params=pltpu.CompilerParams(vmem_limit_bytes=VMEM_LIMIT),
    )(*shards)


_HBM = pl.BlockSpec(memory_space=pltpu.HBM)
_SEM = pl.BlockSpec(memory_space=pltpu.SEMAPHORE)
_SPLIT = dict(has_side_effects=pltpu.SideEffectType.DATAFLOW_SIDE_EFFECTING)


def _hbm(a):
    return pltpu.with_memory_space_constraint(a, pltpu.HBM)


def _gathered_shape(s, rows):
    return (s.shape[0], N_DEV) + s.shape[1:] if rows else (N_DEV,) + s.shape


def _peers(x, y, c):
    return [(1 - x if k & 4 else x, 1 - y if k & 2 else y, 1 - c if k & 1 else c) for k in range(1, N_DEV)]


def gather_start(shards, by_rows, name):
    n = len(shards)

    def body(*refs):
        x_refs, land_refs = refs[:n], refs[n:2 * n]
        send_sems, recv_sems = refs[2 * n], refs[2 * n + 1]
        x, y, c = _place()
        me = 4 * x + 2 * y + c
        for t in range(n):
            for k, peer in enumerate(_peers(x, y, c)):
                pltpu.make_async_remote_copy(
                    src_ref=x_refs[t], dst_ref=_dev_slot(land_refs[t], by_rows[t], me), send_sem=send_sems.at[7 * t + k],
                    recv_sem=recv_sems.at[7 * t + k], device_id=peer, device_id_type=MESH_ID).start()

    lands = [lax.empty(_gathered_shape(s, r), s.dtype) for s, r in zip(shards, by_rows)]
    sems = pltpu.SemaphoreType.DMA((7 * n,))
    res = pl.pallas_call(
        body, name=name,
        out_shape=(sems, sems) + tuple(pltpu.HBM(a.shape, a.dtype) for a in list(shards) + lands),
        in_specs=[_HBM] * (2 * n), out_specs=(_SEM, _SEM) + (_HBM,) * (2 * n),
        input_output_aliases={i: 2 + i for i in range(2 * n)},
        compiler_params=pltpu.CompilerParams(**_SPLIT),
    )(*[_hbm(a) for a in list(shards) + lands])
    return res[0], res[1], list(res[2:2 + n]), list(res[2 + n:])


def gather_wait(send_sems, recv_sems, first, shards, lands, by_rows, after, name):
    n = len(shards)

    def body(*refs):
        x_refs, land_refs = refs[:n], refs[n:2 * n]
        send_sems, recv_sems = refs[2 * n], refs[2 * n + 1]
        x, y, c = _place()
        for t in range(n):
            for k, (px, py, pc) in enumerate(_peers(x, y, c)):
                cp = pltpu.make_async_remote_copy(
                    src_ref=x_refs[t], dst_ref=_dev_slot(land_refs[t], by_rows[t], 4 * px + 2 * py + pc),
                    send_sem=send_sems.at[7 * (first + t) + k], recv_sem=recv_sems.at[7 * (first + t) + k],
                    device_id=(px, py, pc), device_id_type=MESH_ID)
                cp.wait_send()
                cp.wait_recv()

    res = pl.pallas_call(
        body, name=name, out_shape=tuple(pltpu.HBM(a.shape, a.dtype) for a in list(shards) + list(lands)),
        in_specs=[_HBM] * (2 * n) + [_SEM, _SEM, pl.BlockSpec(memory_space=pl.ANY)], out_specs=(_HBM,) * (2 * n),
        input_output_aliases={i: i for i in range(2 * n)},
        compiler_params=pltpu.CompilerParams(**_SPLIT),
    )(*shards, *lands, send_sems, recv_sems, after)
    return list(res[n:])


def place_own(shards, lands, by_rows, name):
    n = len(shards)

    def body(*refs):
        x_refs, out_refs, sems = refs[:n], refs[2 * n:3 * n], refs[3 * n]
        x, y, c = _place()
        mine = [pltpu.make_async_copy(x_refs[t], _dev_slot(out_refs[t], by_rows[t], 4 * x + 2 * y + c), sems.at[t])
                for t in range(n)]
        for cp in mine:
            cp.start()
        for cp in mine:
            cp.wait()

    space = pl.BlockSpec(memory_space=pl.ANY)
    return pl.pallas_call(
        body, name=name, out_shape=[jax.ShapeDtypeStruct(a.shape, a.dtype) for a in lands],
        in_specs=[space] * (2 * n), out_specs=[space] * n, input_output_aliases={n + i: i for i in range(n)},
        scratch_shapes=[pltpu.SemaphoreType.DMA((n,))],
        compiler_params=pltpu.CompilerParams(vmem_limit_bytes=VMEM_LIMIT),
    )(*shards, *lands)


def scatter_start(srcs, src_l, lands, land_l, by_rows, name):
    n = len(srcs)

    def body(*refs):
        x_refs, land_refs = refs[:n], refs[n:2 * n]
        send_sems, recv_sems, token = refs[2 * n], refs[2 * n + 1], refs[-1]
        x, y, c = _place()
        me = 4 * x + 2 * y + c
        for k, (px, py, pc) in enumerate(_peers(x, y, c)):
            for t in range(n):
                blk = _dev_slot(x_refs[t], by_rows[t], 4 * px + 2 * py + pc)
                pltpu.make_async_remote_copy(
                    src_ref=blk.at[src_l[t]], dst_ref=land_refs[t].at[me, land_l[t]], send_sem=send_sems.at[7 * t + k],
                    recv_sem=recv_sems.at[7 * t + k], device_id=(px, py, pc), device_id_type=MESH_ID).start()
        token[...] = jnp.zeros_like(token)

    lands = [lax.empty((N_DEV, 1) + s.shape[2:], s.dtype) if ld is None else ld for s, ld in zip(srcs, lands)]
    sems = pltpu.SemaphoreType.DMA((7 * n,))
    res = pl.pallas_call(
        body, name=name,
        out_shape=(sems, sems) + tuple(pltpu.HBM(a.shape, a.dtype) for a in list(srcs) + lands)
        + (jax.ShapeDtypeStruct((8, LANES), F32),),
        in_specs=[_HBM] * (2 * n),
        out_specs=(_SEM, _SEM) + (_HBM,) * (2 * n) + (pl.BlockSpec(memory_space=pltpu.VMEM),),
        input_output_aliases={i: 2 + i for i in range(2 * n)},
        compiler_params=pltpu.CompilerParams(**_SPLIT),
    )(*[_hbm(a) for a in list(srcs) + lands])
    return res[0], res[1], list(res[2:2 + n]), list(res[2 + n:2 + 2 * n]), res[-1]


def scatter_wait(send_sems, recv_sems, srcs, src_l, lands, land_l, by_rows, after, name):
    n = len(srcs)

    def body(*refs):
        x_refs, land_refs = refs[:n], refs[n:2 * n]
        send_sems, recv_sems = refs[2 * n], refs[2 * n + 1]
        x, y, c = _place()
        for k, (px, py, pc) in enumerate(_peers(x, y, c)):
            peer = 4 * px + 2 * py + pc
            for t in range(n):
                cp = pltpu.make_async_remote_copy(
                    src_ref=_dev_slot(x_refs[t], by_rows[t], peer).at[src_l[t]], dst_ref=land_refs[t].at[peer, land_l[t]],
                    send_sem=send_sems.at[7 * t + k], recv_sem=recv_sems.at[7 * t + k], device_id=(px, py, pc),
                    device_id_type=MESH_ID)
                cp.wait_send()
                cp.wait_recv()

    res = pl.pallas_call(
        body, name=name, out_shape=tuple(pltpu.HBM(a.shape, a.dtype) for a in list(srcs) + list(lands)),
        in_specs=[_HBM] * (2 * n) + [_SEM, _SEM, pl.BlockSpec(memory_space=pl.ANY)], out_specs=(_HBM,) * (2 * n),
        input_output_aliases={i: i for i in range(2 * n)},
        compiler_params=pltpu.CompilerParams(**_SPLIT),
    )(*srcs, *lands, send_sems, recv_sems, after)
    return list(res[:n]), list(res[n:])


ADAM_BLOCK_BYTES = 3 << 19


def adamw(w, m, v, parts, name):
    n_parts = parts.shape[0]
    n_l, r, c = w.shape
    lane_c = -(-c // LANES) * LANES
    fits = [t for t in range(16, r, 16) if r % t == 0 and t * lane_c * 4 <= ADAM_BLOCK_BYTES]
    tr = max(fits) if fits and r * lane_c * 4 > ADAM_BLOCK_BYTES else r
    c1 = 1.0 / (1.0 - ADAM_B1 ** ADAM_STEP)
    c2 = 1.0 / (1.0 - ADAM_B2 ** ADAM_STEP)

    def body(w_ref, m_ref, v_ref, p_ref, g_ref, d_ref, nm_ref, nv_ref):
        g = p_ref[0].astype(F32)
        for j in range(1, n_parts):
            g = g + p_ref[j].astype(F32)
        nm = ADAM_B1 * m_ref[...] + (1.0 - ADAM_B1) * g
        nv = ADAM_B2 * v_ref[...] + (1.0 - ADAM_B2) * (g * g)
        g_ref[...] = g
        nm_ref[...] = nm
        nv_ref[...] = nv
        d_ref[...] = -ADAM_LR * ((nm * c1) / (jnp.sqrt(nv * c2) + ADAM_EPS) + ADAM_WD * w_ref[...])

    blk = pl.BlockSpec((1, tr, c), lambda l, i: (l, i, 0))
    return pl.pallas_call(
        body, name=name, grid=(n_l, r // tr),
        in_specs=[blk, blk, blk, pl.BlockSpec((n_parts, 1, tr, c), lambda l, i: (0, l, i, 0))],
        out_specs=[blk] * 4, out_shape=[jax.ShapeDtypeStruct((n_l, r, c), F32)] * 4,
        compiler_params=_params(("parallel", "parallel")),
    )(w, m, v, parts)


def sum_parts(parts, name):
    def body(p_ref, o_ref):
        g = p_ref[0]
        for j in range(1, parts.shape[0]):
            g = g + p_ref[j]
        o_ref[...] = g

    return pl.pallas_call(body, name=name, out_shape=jax.ShapeDtypeStruct(parts.shape[1:], F32),
                          compiler_params=_params())(parts)


def _pack(arrs, rows, dtype):
    flat = jnp.concatenate([a.reshape(-1).astype(dtype) for a in arrs])
    return jnp.pad(flat, (0, rows * LANES - flat.shape[0])).reshape(rows, LANES)


def _unpack(packed, shapes):
    flat, out, off = packed.reshape(-1), [], 0
    for shp in shapes:
        n = int(np.prod(shp))
        out.append(flat[off:off + n].reshape(shp))
        off += n
    return out


def _cat(parts):
    return jnp.concatenate(parts, axis=1)


def _layer_list(i):
    kind, j = i % 3, i // 3
    mix = ([("mla_w_a", j), ("mla_w_uq", j), ("mla_w_ukv", j), ("mla_w_o", j)] if kind == 0 else
           [("dil_w_qkv", 0), ("dil_w_o", 0)] if kind == 1 else [("fox_w_qkvf", 0), ("fox_w_o", 0)])
    return mix + [("ffn_w_in", i), ("ffn_w_out", i), ("ple_w_proj", i), ("ple_w_gate", i)]


def _mla_layout(w_a, g_uq, g_ukv, j):
    def z(r, n):
        return jnp.zeros((r, n), BF16)

    wa = w_a[j]
    a = _cat([wa[:, :640], wa[:, 640:656], z(D, 48), wa[:, 656:672], z(D, 48)])
    q, k, v = [], [], []
    for h in range(HEADS):
        b = g_uq[h // 2, j][:, 96 * (h % 2):96 * (h % 2 + 1)]
        q += [b[:, 64:80], b[:, 0:32], z(Q_RANK, 16), b[:, 80:96], b[:, 32:64], z(Q_RANK, 16)]
        b = g_ukv[h // 2, j][:, LANES * (h % 2):LANES * (h % 2 + 1)]
        k += [z(KV_RANK, 16), b[:, 0:32], z(KV_RANK, 32), b[:, 32:64], z(KV_RANK, 16)]
        v.append(b[:, 64:128])
    return a, _cat(q), _cat(k + v)


def _mla_unlayout(d_a, d_uq, d_ukv):
    a = _cat([d_a[:, :640], d_a[:, 640:656], d_a[:, 704:720]])
    uq, ukv = [], []
    for dev in range(N_DEV):
        q, kv = [], []
        for h in (2 * dev, 2 * dev + 1):
            b = d_uq[:, LANES * h:LANES * (h + 1)]
            q += [b[:, 16:48], b[:, 80:112], b[:, 0:16], b[:, 64:80]]
            b = d_ukv[:, LANES * h:LANES * (h + 1)]
            kv += [b[:, 16:48], b[:, 80:112], d_ukv[:, HEADS * LANES + 64 * h:HEADS * LANES + 64 * (h + 1)]]
        uq.append(_cat(q))
        ukv.append(_cat(kv))
    return a, jnp.stack(uq), jnp.stack(ukv)


def _mixer_fwd(kind, tag, hn, W, aux):
    if kind == 0:
        a = mm(hn, W["w_a"], "nn", f"{tag}_a", tn=768)
        cq = rms_fwd(a[:, :Q_RANK], W["q_norm"], f"{tag}_cq", out_dtype=BF16)
        ckv = rms_fwd(a[:, Q_RANK:Q_RANK + KV_RANK], W["kv_norm"], f"{tag}_ckv", out_dtype=BF16)
        qp = mm(cq, W["w_uq"], "nn", f"{tag}_uq", tk=384)
        kvp = mm(ckv, W["w_ukv"], "nn", f"{tag}_ukv", tk=256)
        q, k = mla_qk_fwd(qp, kvp, a[:, 640:], aux["cos"], aux["sin"], f"{tag}_qk")
        v = kvp.astype(BF16)
        o = attn_fwd(q, k, v, f"{tag}_attn", wide=True, scale=96 ** -0.5, v_off=HEADS)
        y = mm(o, W["w_o"], "nn", f"{tag}_o")
        return y, (a, cq, ckv, q, k, v, o)
    if kind == 1:
        qkv = mm(hn, W["w_qkv"], "nn", f"{tag}_qkv", out_dtype=BF16, tn=1152)
        parts = [dil_fwd(qkv, aux["dil_bias"][g], g, f"{tag}_g{g}") for g in range(3)]
        o, lse = dil_merge([p_[0] for p_ in parts], [p_[1] for p_ in parts], f"{tag}_merge")
        y = mm(o, W["w_o"], "nn", f"{tag}_o")
        return y, (qkv, o, lse)
    a = mm(hn, W["w_qkvf"], "nn", f"{tag}_qkvf", tn=640)
    fl = a[:, 3072:]
    cum = fox_gate_fwd(fl, aux["fox_b"], f"{tag}_gate")[:, :HEADS]
    cum_q = cum.reshape(S, PAIRS, 2).transpose(1, 0, 2)
    cum_k = cum.T.reshape(PAIRS, 2, S)
    ab = a.astype(BF16)
    o = attn_fwd(ab, ab, ab, f"{tag}_attn", wide=False, scale=0.125, k_off=PAIRS, v_off=2 * PAIRS, cum=cum_q, cum_t=cum_k)
    y = mm(o, W["w_o"], "nn", f"{tag}_o")
    return y, (fl, ab, cum_q, cum_k, o)


def _mixer_bwd(kind, tag, hn, dy, W, aux, saved):
    gr = {}
    if kind == 0:
        a, cq, ckv, q, k, v, o = saved
        gr["w_o"] = mm(o, dy, "tn", f"{tag}_dwo", out_dtype=BF16)
        do = mm(dy, W["w_o"], "nt", f"{tag}_do")
        dq, dk, dv = attn_bwd(q, k, v, o, do, f"{tag}_attn_b", wide=True, scale=96 ** -0.5, v_off=HEADS)
        dqp, dkr = mla_qk_bwd(dq, dk, aux["cos"], aux["sin"], f"{tag}_qk_b")
        dkvp = jnp.concatenate([dk, dv], axis=1)
        gr["w_ukv"] = mm(ckv, dkvp, "tn", f"{tag}_dwukv", out_dtype=BF16, tm=256)
        dckv = mm(dkvp, W["w_ukv"], "nt", f"{tag}_dckv", tn=256)
        gr["w_uq"] = mm(cq, dqp, "tn", f"{tag}_dwuq", out_dtype=BF16, tm=384)
        dcq = mm(dqp, W["w_uq"], "nt", f"{tag}_dcq", tn=384)
        da_q, gr["q_norm"] = rms_bwd(a[:, :Q_RANK], W["q_norm"], dcq, f"{tag}_cq_b", out_dtype=BF16)
        da_kv, gr["kv_norm"] = rms_bwd(a[:, Q_RANK:Q_RANK + KV_RANK], W["kv_norm"], dckv, f"{tag}_ckv_b",
                                       out_dtype=BF16)
        da = jnp.concatenate([da_q, da_kv, dkr], axis=1)
        gr["w_a"] = mm(hn, da, "tn", f"{tag}_dwa", out_dtype=BF16, tn=768)
        return mm(da, W["w_a"], "nt", f"{tag}_dhn", tk=768), gr
    if kind == 1:
        qkv, o, lse = saved
        gr["w_o"] = mm(o, dy, "tn", f"{tag}_dwo", out_dtype=BF16)
        do = mm(dy, W["w_o"], "nt", f"{tag}_do")
        cols, dbs = [], []
        for g in range(3):
            dq, dk, dv, db = dil_bwd(qkv, aux["dil_bias"][g], o, lse, do, g, f"{tag}_g{g}_b")
            cols += [dq, dk, dv]
            dbs.append(db)
        dqkv = jnp.concatenate(cols, axis=1)
        gr["dil_dbias"] = dbs
        gr["w_qkv"] = mm(hn, dqkv, "tn", f"{tag}_dwqkv", out_dtype=BF16, out_dev=1152, tn=1152)
        return mm(dqkv, W["w_qkv"], "nt", f"{tag}_dhn", tk=1152), gr
    fl, ab, cum_q, cum_k, o = saved
    gr["w_o"] = mm(o, dy, "tn", f"{tag}_dwo", out_dtype=BF16)
    do = mm(dy, W["w_o"], "nt", f"{tag}_do")
    dq, dk, dv, dcq, dck = attn_bwd(ab, ab, ab, o, do, f"{tag}_attn_b", wide=False, scale=0.125, out_dtype=BF16,
                                    k_off=PAIRS, v_off=2 * PAIRS, cum=cum_q, cum_t=cum_k)
    pad = ((0, 0), (0, LANES - HEADS))
    dcq = jnp.pad(dcq.transpose(1, 0, 2).reshape(S, HEADS), pad)
    dck = jnp.pad(dck.reshape(HEADS, S).T, pad)
    dfl, gr["b_f"] = fox_gate_bwd(fl, aux["fox_b"], dcq, dck, f"{tag}_gate_b")
    da = jnp.concatenate([dq, dk, dv, dfl.astype(BF16)], axis=1)
    gr["w_qkvf"] = mm(hn, da, "tn", f"{tag}_dwqkvf", out_dtype=BF16, tn=640)
    return mm(da, W["w_qkvf"], "nt", f"{tag}_dhn", tk=640), gr


def kernel(x, p, positions, norm_g, ffn_w_in, ffn_w_out, ple_w_proj, ple_w_gate, rel_bias, mla_w_a, mla_q_norm, mla_kv_norm, mla_w_uq, mla_w_ukv, mla_w_o, dil_w_qkv, dil_w_o, fox_w_qkvf, fox_b_f, fox_w_o, loss_target, m_norm_g, m_ffn_w_in, m_ffn_w_out, m_ple_w_proj, m_ple_w_gate, m_rel_bias, m_mla_w_a, m_mla_q_norm, m_mla_kv_norm, m_mla_w_uq, m_mla_w_ukv, m_mla_w_o, m_dil_w_qkv, m_dil_w_o, m_fox_w_qkvf, m_fox_b_f, m_fox_w_o, v_norm_g, v_ffn_w_in, v_ffn_w_out, v_ple_w_proj, v_ple_w_gate, v_rel_bias, v_mla_w_a, v_mla_q_norm, v_mla_kv_norm, v_mla_w_uq, v_mla_w_ukv, v_mla_w_o, v_dil_w_qkv, v_dil_w_o, v_fox_w_qkvf, v_fox_b_f, v_fox_w_o):
    given = dict(locals())
    me = 4 * lax.axis_index("x") + 2 * lax.axis_index("y") + lax.axis_index("c")

    rows_of = {n: axis == 1 for n, _, axis in BIG}
    shape_of = {n: shp for n, shp, _ in BIG}
    lists = [_layer_list(i) for i in range(DEPTH)]
    flat = [nl for ls in lists for nl in ls]
    flat_rows = [rows_of[n] for n, _ in flat]
    send_s, recv_s, shards, lands = gather_start([given[n][l:l + 1].astype(BF16) for n, l in flat], flat_rows,
                                                 "gather_start")
    lands = place_own(shards, lands, flat_rows, "place_own")
    full = {}

    gain_rows = _rows(sum(int(np.prod(s)) for _, s, _ in SMALL_SHARDED))
    gains, = all_gather([_pack([given[n] for n, _, _ in SMALL_SHARDED], gain_rows, F32)], [False], "gather_gains",
                        in_vmem=True)
    gains = gains.reshape(N_DEV, gain_rows * LANES)
    off = 0
    for n, shp, axis in SMALL_SHARDED:
        cnt = int(np.prod(shp))
        g = jnp.moveaxis(gains[:, off:off + cnt].reshape((N_DEV,) + shp), 0, axis)
        full[n] = g.reshape(shp[:axis] + (N_DEV * shp[axis],))
        off += cnt

    cos, sin = rope_tables(positions.reshape(S, 1), "rope_tables")
    dil_bias = [mm(rel_bias[:, HEADS * g:HEADS * (g + 1)], jnp.asarray(_bucket_onehot(DIL[g][1])), "tn",
                   f"dil_bias{g}", precise=True, tn=4096).reshape(HEADS, QBLK, 2 * QBLK) for g in range(3)]
    aux = {"cos": cos, "sin": sin, "dil_bias": dil_bias,
           "fox_b": jnp.pad(fox_b_f, ((0, 0), (0, LANES - HEADS)))}

    def arrived(i, part, behind):
        n_mix = len(lists[i]) - 4
        first = sum(len(ls) for ls in lists[:i]) + (n_mix if part else 0)
        sl = slice(first, first + (4 if part else n_mix))
        got = gather_wait(send_s, recv_s, first, shards[sl], lands[sl], flat_rows[sl], behind, f"gather_wait{i}_{part}")
        return {n: g.reshape(1, N_DEV * shape_of[n][1], shape_of[n][2]) if rows_of[n] else g
                for (n, _), g in zip(flat[sl], got)}

    def mixer_weights(i, behind):
        kind, j = i % 3, i // 3
        w = arrived(i, 0, behind)
        W = {"g": [full["norm_g"][i, r][None, :] for r in range(4)]}
        if kind == 0:
            w_a, w_uq, w_ukv = _mla_layout(w["mla_w_a"], w["mla_w_uq"], w["mla_w_ukv"], 0)
            W.update(w_a=w_a, w_uq=w_uq, w_ukv=w_ukv, w_o=Lay(w["mla_w_o"], 0),
                     q_norm=full["mla_q_norm"][j][None, :], kv_norm=full["mla_kv_norm"][j][None, :])
        elif kind == 1:
            W.update(w_qkv=Dev(w["dil_w_qkv"], 0), w_o=Lay(w["dil_w_o"], 0))
        else:
            fox_w = jnp.pad(_cat([w["fox_w_qkvf"][dev, 0] for dev in range(N_DEV)]), ((0, 0), (0, FOX_W - 3088)))
            W.update(w_qkvf=fox_w, w_o=Lay(w["fox_w_o"], 0))
        return W

    def ffn_weights(i, behind):
        w = arrived(i, 1, behind)
        return {"w_in": Dev(w["ffn_w_in"], 0), "w_out": Lay(w["ffn_w_out"], 0), "w_proj": Dev(w["ple_w_proj"], 0),
                "w_gate": Lay(w["ple_w_gate"], 0)}

    h = x[0]
    saved, weights = [], []
    for i in range(DEPTH):
        kind, j, W = i % 3, i // 3, mixer_weights(i, h)
        weights.append(W)
        t = f"l{i}"
        hn = rms_fwd(h, W["g"][0], f"{t}_n0", out_dtype=BF16)
        y, mix = _mixer_fwd(kind, f"{t}_mix", hn, W, aux)
        W.update(ffn_weights(i, y))
        h1 = rms_fwd(y, W["g"][1], f"{t}_n1", res=h)
        fin = rms_fwd(h1, W["g"][2], f"{t}_n2", out_dtype=BF16)
        gu = mm(fin, W["w_in"], "nn", f"{t}_ffn_in", out_dev=FF_W, tn=FF_W)
        act = swiglu_fwd(gu, f"{t}_swiglu")
        f = mm(Dev(act, 0), W["w_out"], "nn", f"{t}_ffn_out", tk=FF_W)
        h2 = rms_fwd(f, W["g"][3], f"{t}_n3", res=h1)
        pp = mm(p[i, 0], W["w_proj"], "nn", f"{t}_ple_p", tn=LANES, tk=256)
        gt = mm(h2, W["w_gate"], "nn", f"{t}_ple_g")
        h3 = ple_fwd(h2, pp, gt, f"{t}_ple")
        saved.append((h, hn, y, h1, fin, gu, act, f, h2, pp, gt, mix))
        h = h3

    dh, loss_lanes = loss_head(h, loss_target[0], "loss_head")

    grads = {n: None for n, _, _ in BIG}
    landed = {n: (lax.empty((N_DEV,) + shp, BF16) if shp[0] > 1 else None) for n, shp, _ in BIG}
    in_flight = []
    g_norm = [[None] * 4 for _ in range(DEPTH)]
    g_qn, g_kvn = [None, None], [None, None]
    g_rel, g_bf = None, None

    def stacked(n):
        g = grads[n]
        return None if g is None else g.reshape(g.shape[0], N_DEV * g.shape[2], g.shape[3])

    def by_device(g):
        return g.reshape(g.shape[0], N_DEV, g.shape[1] // N_DEV, g.shape[2])

    def start(i, entries, mine, tag):
        names = [n for n, _ in entries]
        srcs = [mine[n] if n in mine else grads[n] for n in names]
        src_l = [0 if n in mine else i for n in names]
        land_l = [l if shape_of[n][0] > 1 else 0 for n, l in entries]
        rows = [rows_of[n] for n in names]
        s_sem, r_sem, srcs, got, token = scatter_start(srcs, src_l, [landed[n] for n in names], land_l, rows, tag)
        for n, src, ld in zip(names, srcs, got):
            landed[n] = ld
            if n in mine:
                mine[n] = src
            else:
                grads[n] = src
        in_flight.append((s_sem, r_sem, names, mine, src_l, land_l, rows))
        return token

    token = None
    for i in reversed(range(DEPTH)):
        kind, j, W = i % 3, i // 3, weights[i]
        t = f"l{i}b"
        h0, hn, y, h1, fin, gu, act, f, h2, pp, gt, mix = saved[i]
        dpp, dgt = ple_bwd(dh, pp, gt, f"{t}_ple")
        grads["ple_w_proj"] = mm(p[i, 0], dpp, "tn", f"{t}_dwp", out_dtype=BF16, out_dev=LANES, tm=256, tn=LANES,
                                 stack=(grads["ple_w_proj"], DEPTH, i), after=token)
        grads["ple_w_gate"] = by_device(mm(h2, dgt, "tn", f"{t}_dwg", out_dtype=BF16,
                                           stack=(stacked("ple_w_gate"), DEPTH, i)))
        dh2 = mm(dgt, W["w_gate"], "nt", f"{t}_dh2", add=dh)
        df, g_norm[i][3] = rms_bwd(f, W["g"][3], dh2, f"{t}_n3", out_dtype=BF16)
        grads["ffn_w_out"] = by_device(mm(Dev(act, 0), df, "tn", f"{t}_dwout", out_dtype=BF16, tm=FF_W,
                                          stack=(stacked("ffn_w_out"), DEPTH, i)))
        dact = mm(df, W["w_out"], "nt", f"{t}_dact", out_dev=FF_W, tn=FF_W)
        dgu = swiglu_bwd(gu, dact, f"{t}_swiglu")
        grads["ffn_w_in"] = mm(fin, Dev(dgu, 0), "tn", f"{t}_dwin", out_dtype=BF16, out_dev=FF_W, tn=FF_W,
                               stack=(grads["ffn_w_in"], DEPTH, i))
        token = start(i, lists[i][-4:], {}, f"scatter_ffn{i}")
        dfin = mm(Dev(dgu, 0), W["w_in"], "nt", f"{t}_dfin", after=token, tk=FF_W)
        dh1, g_norm[i][2] = rms_bwd(h1, W["g"][2], dfin, f"{t}_n2", res=dh2)
        dy, g_norm[i][1] = rms_bwd(y, W["g"][1], dh1, f"{t}_n1", out_dtype=BF16)
        dhn, gr = _mixer_bwd(kind, f"{t}_mix", hn, dy, W, aux, mix)
        if kind == 0:
            d_a, d_uq, d_ukv = _mla_unlayout(gr["w_a"], gr["w_uq"], gr["w_ukv"])
            mine = {"mla_w_a": by_device(d_a[None]), "mla_w_uq": d_uq[:, None], "mla_w_ukv": d_ukv[:, None],
                    "mla_w_o": by_device(gr["w_o"][None])}
            g_qn[j], g_kvn[j] = gr["q_norm"], gr["kv_norm"]
        elif kind == 1:
            mine = {"dil_w_qkv": gr["w_qkv"], "dil_w_o": by_device(gr["w_o"][None])}
            g_rel = jnp.concatenate(
                [mm(jnp.asarray(_bucket_onehot(DIL[g][1])), gr["dil_dbias"][g].reshape(HEADS, -1), "nt",
                    f"{t}_drel{g}", precise=True, tk=4096) for g in range(3)], axis=1)
        else:
            wide = gr["w_qkvf"]
            mine = {"fox_w_qkvf": jnp.stack([wide[:, 386 * dev:386 * (dev + 1)] for dev in range(N_DEV)])[:, None],
                    "fox_w_o": by_device(gr["w_o"][None])}
            g_bf = gr["b_f"][:, :HEADS]
        token = start(i, lists[i][:-4], mine, f"scatter_mix{i}")
        dh, g_norm[i][0] = rms_bwd(h0, W["g"][0], dhn, f"{t}_n0", res=dh1, after=token)
    grad_x = dh[None]

    own = {n: [] for n, _, _ in BIG}
    for idx, (s_sem, r_sem, names, mine, src_l, land_l, rows) in enumerate(in_flight):
        srcs = [mine[n] if n in mine else grads[n] for n in names]
        srcs, got = scatter_wait(s_sem, r_sem, srcs, src_l, [landed[n] for n in names], land_l, rows, dh,
                                 f"scatter_wait{idx}")
        for n, src, ld, sl, ll, rw in zip(names, srcs, got, src_l, land_l, rows):
            landed[n] = ld
            if n not in mine:
                grads[n] = src
            blk = lax.dynamic_index_in_dim(src, me, axis=1 if rw else 0, keepdims=False)[sl]
            own[n].append((ll, blk))
    big_out = []
    for n, _, _ in BIG:
        part = landed[n]
        for ll, blk in own[n]:
            part = lax.dynamic_update_slice(part, blk[None, None], (me, ll, 0, 0))
        big_out.append(adamw(given[n], given["m_" + n], given["v_" + n], part, f"adamw_{n}"))

    small_full = [jnp.stack([jnp.concatenate(r, axis=0) for r in g_norm]).reshape(-1),
                  jnp.concatenate(g_qn, axis=0).reshape(-1), jnp.concatenate(g_kvn, axis=0).reshape(-1),
                  g_rel.reshape(-1), g_bf.reshape(-1), loss_lanes.reshape(-1)]
    small_n = sum(a.shape[0] for a in small_full)
    small_rows = _rows(small_n)
    parts, = all_gather([_pack(small_full, small_rows, F32)], [False], "gather_small_grads", in_vmem=True)
    tot = _unpack(sum_parts(parts, "sum_small_grads"), [(4, 4, D), (2, Q_RANK), (2, KV_RANK), (32, 48), (1, 16), (LANES,)])
    loss = jnp.sum(tot[5])
    small_g = [lax.dynamic_slice_in_dim(tot[0], me * 128, 128, axis=2), lax.dynamic_slice_in_dim(tot[1], me * 48, 48, axis=1),
               lax.dynamic_slice_in_dim(tot[2], me * 32, 32, axis=1), tot[3], tot[4]]
    small_names = [n for n, _, _ in SMALL_SHARDED] + [n for n, _ in SMALL_REPL]
    small_shapes = [s for _, s, _ in SMALL_SHARDED] + [s for _, s in SMALL_REPL]
    s_rows = _rows(sum(int(np.prod(s)) for s in small_shapes))
    small_out = adamw(_pack([given[n] for n in small_names], s_rows, F32)[None],
                      _pack([given["m_" + n] for n in small_names], s_rows, F32)[None],
                      _pack([given["v_" + n] for n in small_names], s_rows, F32)[None],
                      _pack(small_g, s_rows, F32)[None, None], "adamw_small")
    small_out = [_unpack(o_, small_shapes) for o_ in small_out]

    res = [{}, {}, {}, {}]
    for k in range(4):
        for idx, (n, _, _) in enumerate(BIG):
            res[k][n] = big_out[idx][k]
        for idx, n in enumerate(small_names):
            res[k][n] = small_out[k][idx]
    return (loss, grad_x, *[res[0][n] for n in WEIGHTS], *[res[1][n] for n in WEIGHTS],
            *[res[2][n] for n in WEIGHTS], *[res[3][n] for n in WEIGHTS])
```

```python
import math
from typing import NamedTuple

import numpy as np
import jax
import jax.numpy as jnp
from jax import lax
from jax.experimental import pallas as pl
from jax.experimental.pallas import tpu as pltpu

F32 = jnp.float32
BF16 = jnp.bfloat16
MESH_ID = pl.DeviceIdType.MESH

N_DEV = 8
S = 2048
D = 1024
DEPTH = 4
D_FF = 2816
D_PLE = 256
EPS = 1e-6
NEG = -1e30
LANES = 128
HEADS = 16
PAIRS = 8
Q_RANK = 384
KV_RANK = 256
QBLK = 128
DIL = ((128, 1), (512, 4), (2048, 16))
REL_BUCKETS = 32
FOX_W = 3200
VMEM_LIMIT = 56 * 1024 * 1024

ADAM_LR, ADAM_B1, ADAM_B2, ADAM_EPS, ADAM_WD, ADAM_STEP = 1e-3, 0.9, 0.999, 1e-8, 0.01, 10


BIG = (
    ("ffn_w_in", (4, 1024, 704), 2), ("ffn_w_out", (4, 352, 1024), 1),
    ("ple_w_proj", (4, 256, 128), 2), ("ple_w_gate", (4, 128, 1024), 1),
    ("mla_w_a", (2, 128, 672), 1), ("mla_w_uq", (2, 384, 192), 2),
    ("mla_w_ukv", (2, 256, 256), 2), ("mla_w_o", (2, 128, 1024), 1),
    ("dil_w_qkv", (1, 1024, 1152), 2), ("dil_w_o", (1, 128, 1024), 1),
    ("fox_w_qkvf", (1, 1024, 386), 2), ("fox_w_o", (1, 128, 1024), 1),
)
SMALL_SHARDED = (("norm_g", (4, 4, 128), 2), ("mla_q_norm", (2, 48), 1), ("mla_kv_norm", (2, 32), 1))
SMALL_REPL = (("rel_bias", (32, 48)), ("fox_b_f", (1, 16)))
WEIGHTS = ("norm_g", "ffn_w_in", "ffn_w_out", "ple_w_proj", "ple_w_gate", "rel_bias", "mla_w_a", "mla_q_norm",
           "mla_kv_norm", "mla_w_uq", "mla_w_ukv", "mla_w_o", "dil_w_qkv", "dil_w_o", "fox_w_qkvf", "fox_b_f",
           "fox_w_o")


def _rows(n):
    return -(-n // (8 * LANES)) * 8


def _t5_bucket_np(dist):
    max_exact = REL_BUCKETS // 2
    n = np.maximum(dist.astype(np.float32), np.float32(1.0))
    large = max_exact + (np.log(n / np.float32(max_exact)) / np.float32(math.log(2048 / max_exact))
                         * np.float32(REL_BUCKETS - max_exact)).astype(np.int32)
    large = np.minimum(large, REL_BUCKETS - 1)
    return np.where(dist < max_exact, dist, large)


def _bucket_onehot(dilation):
    i = np.arange(QBLK)[:, None]
    j = np.arange(2 * QBLK)[None, :]
    bucket = _t5_bucket_np(np.clip(QBLK + i - j, 0, None) * dilation).reshape(-1)
    return (np.arange(REL_BUCKETS)[:, None] == bucket[None, :]).astype(np.float32)


def _rope_inv_lanes():
    half = 16
    inv = (np.float32(10000.0) ** (-np.arange(half, dtype=np.float32) / np.float32(half))).astype(np.float32)
    t = np.zeros((1, LANES), np.float32)
    t[0, 0:16] = inv
    t[0, 64:80] = inv
    return t


def _params(sem=None):
    return pltpu.CompilerParams(dimension_semantics=sem, vmem_limit_bytes=VMEM_LIMIT)


def _tile(dim, target):
    if dim <= target or dim % target == 0:
        return min(dim, target)
    t = (target // LANES) * LANES
    while dim % t:
        t -= LANES
    return t


_DIMS = {"nn": (((1,), (0,)), ((), ())), "nt": (((1,), (1,)), ((), ())), "tn": (((0,), (0,)), ((), ()))}


class Lay(NamedTuple):
    arr: jax.Array
    l: int


class Dev(NamedTuple):
    arr: jax.Array
    l: int


def _lshape(op):
    if isinstance(op, Dev):
        g, _, r, w = op.arr.shape
        return r, g * w
    return op.arr.shape[1:] if isinstance(op, Lay) else op.shape


def _op_spec(op, rows_t, cols_t, row_ix, col_ix):
    if isinstance(op, Dev):
        w = op.arr.shape[3]
        assert w % cols_t == 0 and (cols_t % LANES == 0 or cols_t == w), (w, cols_t)
        nb, l = w // cols_t, op.l
        return pl.BlockSpec((1, 1, rows_t, cols_t),
                            lambda i, j, k: (col_ix(i, j, k) // nb, l, row_ix(i, j, k), col_ix(i, j, k) % nb))
    if isinstance(op, Lay):
        l = op.l
        return pl.BlockSpec((1, rows_t, cols_t), lambda i, j, k: (l, row_ix(i, j, k), col_ix(i, j, k)))
    return pl.BlockSpec((rows_t, cols_t), lambda i, j, k: (row_ix(i, j, k), col_ix(i, j, k)))


def _mat(ref):
    return ref[(0,) * (len(ref.shape) - 2)]


def mm(a, b, mode, name, out_dtype=F32, precise=False, add=None, out_dev=None, stack=None, after=None,
       tm=1024, tn=512, tk=2048):
    (ar, ac), (br, bc) = _lshape(a), _lshape(b)
    M, K = (ac, ar) if mode == "tn" else (ar, ac)
    N = br if mode == "nt" else bc
    assert K == (bc if mode == "nt" else br)
    tm, tn, tk = _tile(M, tm), _tile(N, tn), _tile(K, tk)
    nk = K // tk
    ix_i, ix_j, ix_k = (lambda i, j, k: i), (lambda i, j, k: j), (lambda i, j, k: k)
    a_spec = _op_spec(a, tk, tm, ix_k, ix_i) if mode == "tn" else _op_spec(a, tm, tk, ix_i, ix_k)
    b_spec = _op_spec(b, tn, tk, ix_j, ix_k) if mode == "nt" else _op_spec(b, tk, tn, ix_k, ix_j)
    buf, n_l, l = stack if stack is not None else (None, 1, 0)
    if out_dev is not None:
        out = Dev(jax.ShapeDtypeStruct((N // out_dev, n_l, M, out_dev), out_dtype), l)
    elif stack is not None:
        out = Lay(jax.ShapeDtypeStruct((n_l, M, N), out_dtype), l)
    else:
        out = jax.ShapeDtypeStruct((M, N), out_dtype)
    o_spec = _op_spec(out, tm, tn, ix_i, ix_j)
    n_in = 3 if add is not None else 2

    def body(*refs):
        a_ref, b_ref = refs[0], refs[1]
        o_ref = refs[n_in + (buf is not None) + (after is not None)]
        if precise:
            part = lax.dot_general(_mat(a_ref), _mat(b_ref), _DIMS[mode], precision=lax.Precision.HIGHEST,
                                   preferred_element_type=F32)
        else:
            part = lax.dot_general(_mat(a_ref).astype(BF16), _mat(b_ref).astype(BF16), _DIMS[mode],
                                   preferred_element_type=F32)

        def finish(r):
            r = r + refs[2][...] if add is not None else r
            o_ref[...] = r.astype(o_ref.dtype).reshape(o_ref.shape)

        if nk == 1:
            finish(part)
            return
        acc, k = refs[-1], pl.program_id(2)

        @pl.when(k == 0)
        def _():
            acc[...] = part

        @pl.when(k > 0)
        def _():
            acc[...] += part

        @pl.when(k == nk - 1)
        def _():
            finish(acc[...])

    ins = [getattr(a, "arr", a), getattr(b, "arr", b)] + ([add] if add is not None else [])
    in_specs = [a_spec, b_spec] + ([o_spec] if add is not None else [])
    aliases = {}
    if buf is not None:
        ins.append(buf)
        in_specs.append(pl.BlockSpec(memory_space=pl.ANY))
        aliases = {n_in: 0}
    if after is not None:
        ins.append(after)
        in_specs.append(pl.BlockSpec(memory_space=pl.ANY))
    return pl.pallas_call(
        body, name=name, grid=(M // tm, N // tn, nk), in_specs=in_specs, out_specs=o_spec,
        out_shape=getattr(out, "arr", out), input_output_aliases=aliases,
        scratch_shapes=[pltpu.VMEM((tm, tn), F32)] if nk > 1 else [],
        compiler_params=_params(("parallel", "parallel", "arbitrary")),
    )(*ins)


def _rows_call(body, name, ins, outs, tr=256, acc_outs=()):
    n = ins[0].shape[0]
    tr = min(tr, n)
    in_specs = [pl.BlockSpec((tr, a.shape[1]), lambda i: (i, 0)) if a.shape[0] == n else
                pl.BlockSpec(a.shape, lambda i: (0, 0)) for a in ins]
    out_specs = [pl.BlockSpec((tr, w), lambda i: (i, 0)) for w, _ in outs] + \
                [pl.BlockSpec((1, w), lambda i: (0, 0)) for w in acc_outs]
    out_shape = [jax.ShapeDtypeStruct((n, w), dt) for w, dt in outs] + \
                [jax.ShapeDtypeStruct((1, w), F32) for w in acc_outs]
    res = pl.pallas_call(body, name=name, grid=(n // tr,), in_specs=in_specs, out_specs=out_specs,
                         out_shape=out_shape, compiler_params=_params(("arbitrary",)))(*ins)
    return res[0] if len(res) == 1 else res


def _acc(ref, val):
    @pl.when(pl.program_id(0) == 0)
    def _():
        ref[...] = jnp.zeros_like(ref)

    ref[...] += val


def rms_fwd(x, g, name, res=None, out_dtype=F32):
    def body(*refs):
        x_ref, g_ref = refs[0], refs[1]
        o_ref = refs[-1]
        xv = x_ref[...]
        y = xv * lax.rsqrt(jnp.mean(xv * xv, axis=-1, keepdims=True) + EPS) * g_ref[...]
        o_ref[...] = (y + refs[2][...] if res is not None else y).astype(o_ref.dtype)

    ins = [x, g] + ([res] if res is not None else [])
    return _rows_call(body, name, ins, [(x.shape[1], out_dtype)])


def rms_bwd(x, g, dy, name, res=None, out_dtype=F32, after=None):
    def body(*refs):
        x_ref, g_ref, dy_ref = refs[:3]
        dx_ref, dg_ref = refs[-2], refs[-1]
        xv, dyv = x_ref[...], dy_ref[...]
        r = lax.rsqrt(jnp.mean(xv * xv, axis=-1, keepdims=True) + EPS)
        xh = xv * r
        dxh = dyv * g_ref[...]
        dx = r * (dxh - xh * jnp.mean(dxh * xh, axis=-1, keepdims=True))
        dx_ref[...] = (dx + refs[3][...] if res is not None else dx).astype(dx_ref.dtype)
        _acc(dg_ref, jnp.sum(dyv * xh, axis=0, keepdims=True))

    ins = [x, g, dy] + ([res] if res is not None else []) + ([after] if after is not None else [])
    return _rows_call(body, name, ins, [(x.shape[1], out_dtype)], acc_outs=(x.shape[1],))


def _sigmoid(x):
    return 0.5 * jnp.tanh(0.5 * x) + 0.5


FF_W = 704
FF_TR = 512


def _ff_spec(shift):
    return pl.BlockSpec((1, 1, FF_TR, FF_W), lambda d, i: (d + shift, 0, i, 0))


def swiglu_fwd(gu, name):
    def body(g_ref, u_ref, o_ref):
        gate = g_ref[...]
        o_ref[...] = (gate * _sigmoid(gate) * u_ref[...]).astype(BF16)

    return pl.pallas_call(body, name=name, grid=(4, S // FF_TR), in_specs=[_ff_spec(0), _ff_spec(4)],
                          out_specs=_ff_spec(0), out_shape=jax.ShapeDtypeStruct((4, 1, S, FF_W), BF16),
                          compiler_params=_params(("parallel", "parallel")))(gu, gu)


def swiglu_bwd(gu, dact, name):
    def body(g_ref, u_ref, d_ref, o_ref):
        gate, d = g_ref[...], d_ref[...]
        sg = _sigmoid(gate)

        @pl.when(pl.program_id(0) < 4)
        def _():
            o_ref[...] = (d * u_ref[...] * sg * (1.0 + gate * (1.0 - sg))).astype(BF16)

        @pl.when(pl.program_id(0) >= 4)
        def _():
            o_ref[...] = (d * gate * sg).astype(BF16)

    def half(shift):
        return pl.BlockSpec((1, 1, FF_TR, FF_W), lambda d, i: (d % 4 + shift, 0, i, 0))

    return pl.pallas_call(body, name=name, grid=(8, S // FF_TR), in_specs=[half(0), half(4), half(0)],
                          out_specs=_ff_spec(0), out_shape=jax.ShapeDtypeStruct((8, 1, S, FF_W), BF16),
                          compiler_params=_params(("parallel", "parallel")))(gu, gu, dact)


def ple_fwd(h, pp, gt, name):
    def body(h_ref, p_ref, g_ref, o_ref):
        o_ref[...] = h_ref[...] + p_ref[...] * _sigmoid(g_ref[...])

    return _rows_call(body, name, [h, pp, gt], [(D, F32)])


def ple_bwd(dh, pp, gt, name):
    def body(d_ref, p_ref, g_ref, dp_ref, dg_ref):
        d, sg = d_ref[...], _sigmoid(g_ref[...])
        dp_ref[...] = (d * sg).astype(BF16)
        dg_ref[...] = (d * p_ref[...] * sg * (1.0 - sg)).astype(BF16)

    return _rows_call(body, name, [dh, pp, gt], [(D, BF16), (D, BF16)])


def loss_head(y, target, name):
    def body(y_ref, t_ref, d_ref, l_ref):
        e = y_ref[...] - t_ref[...]
        d_ref[...] = e * (1.0 / D)
        col = jnp.sum(e * e, axis=0, keepdims=True) * (0.5 / D)
        _acc(l_ref, sum(col[:, LANES * c:LANES * (c + 1)] for c in range(D // LANES)))

    return _rows_call(body, name, [y, target], [(D, F32)], acc_outs=(LANES,))


def rope_tables(pos_col, name):
    inv = jnp.asarray(_rope_inv_lanes())

    def body(p_ref, inv_ref, c_ref, s_ref):
        ang = p_ref[...].astype(F32) * inv_ref[...]
        lane = lax.broadcasted_iota(jnp.int32, ang.shape, 1)
        first, second = lane < 16, (lane >= 64) & (lane < 80)
        c_ref[...] = jnp.where(first | second, jnp.cos(ang), 1.0)
        sn = jnp.sin(ang)
        s_ref[...] = jnp.where(first, -sn, jnp.where(second, sn, 0.0))

    return _rows_call(body, name, [pos_col, inv], [(LANES, F32), (LANES, F32)])


def _rope(x, c, s):
    return x * c + pltpu.roll(x, 64, axis=1) * s


def _rope_t(d, c, s):
    return d * c + pltpu.roll(d * s, 64, axis=1)


def mla_qk_fwd(qp, kvp, kr, cos, sin, name):
    def body(q_ref, k_ref, kr_ref, c_ref, s_ref, qo_ref, ko_ref):
        c, s = c_ref[...], s_ref[...]
        kr_rot = _rope(kr_ref[...], c, s)
        for h in range(HEADS):
            sl = slice(LANES * h, LANES * (h + 1))
            qo_ref[:, sl] = _rope(q_ref[:, sl], c, s).astype(BF16)
            ko_ref[:, sl] = (k_ref[:, sl] + kr_rot).astype(BF16)

    n = qp.shape[0]
    tr = 256
    w = HEADS * LANES
    return pl.pallas_call(
        body, name=name, grid=(n // tr,),
        in_specs=[pl.BlockSpec((tr, w), lambda i: (i, 0)), pl.BlockSpec((tr, w), lambda i: (i, 0)),
                  pl.BlockSpec((tr, LANES), lambda i: (i, 0)), pl.BlockSpec((tr, LANES), lambda i: (i, 0)),
                  pl.BlockSpec((tr, LANES), lambda i: (i, 0))],
        out_specs=[pl.BlockSpec((tr, w), lambda i: (i, 0))] * 2,
        out_shape=[jax.ShapeDtypeStruct((n, w), BF16)] * 2, compiler_params=_params(("arbitrary",)),
    )(qp, kvp, kr, cos, sin)


def mla_qk_bwd(dq, dk, cos, sin, name):
    def body(dq_ref, dk_ref, c_ref, s_ref, dqp_ref, dkr_ref):
        c, s = c_ref[...], s_ref[...]
        tot = jnp.zeros(c.shape, F32)
        for h in range(HEADS):
            sl = slice(LANES * h, LANES * (h + 1))
            dqp_ref[:, sl] = _rope_t(dq_ref[:, sl], c, s).astype(BF16)
            tot = tot + dk_ref[:, sl]
        dkr_ref[...] = _rope_t(tot, c, s).astype(BF16)

    return _rows_call(body, name, [dq, dk, cos, sin], [(HEADS * LANES, BF16), (LANES, BF16)])


TQ = 256


def _pair_masks(shape):
    lane = lax.broadcasted_iota(jnp.int32, shape, 1)
    return (lane < 64, lane >= 64)


def _causal_probs(q_a, k_a, scale, b0, cq, ck):
    s = lax.dot_general(q_a, k_a, _DIMS["nt"], preferred_element_type=F32) * scale
    if cq is not None:
        s = s + (cq - ck)
    row = lax.broadcasted_iota(jnp.int32, s.shape, 0) + b0
    col = lax.broadcasted_iota(jnp.int32, s.shape, 1)
    s = jnp.where(col <= row, s, NEG)
    e = jnp.exp(s - jnp.max(s, axis=-1, keepdims=True))
    return e / jnp.sum(e, axis=-1, keepdims=True)


def attn_fwd(q, k, v, name, *, wide, scale, q_off=0, k_off=0, v_off=0, cum=None, cum_t=None):
    qw = 2 * LANES if wide else LANES
    forget = cum is not None

    def body(*refs):
        q_ref, k_ref, v_ref = refs[:3]
        o_ref = refs[-1]
        m0, m1 = _pair_masks((TQ, LANES))
        for qi in range(S // TQ):
            b0, b1 = qi * TQ, (qi + 1) * TQ
            outs = []
            for a, msk in enumerate((m0, m1)):
                if wide:
                    q_a, k_a = q_ref[b0:b1, LANES * a:LANES * (a + 1)], k_ref[:b1, LANES * a:LANES * (a + 1)]
                else:
                    q_a, k_a = jnp.where(msk, q_ref[b0:b1, :], jnp.zeros((), BF16)), k_ref[:b1, :]
                cq = refs[3][0, b0:b1, a:a + 1] if forget else None
                ck = refs[4][0, a:a + 1, :b1] if forget else None
                p = _causal_probs(q_a, k_a, scale, b0, cq, ck)
                outs.append(jnp.dot(p.astype(BF16), v_ref[:b1, :], preferred_element_type=F32))
            o_ref[b0:b1, :] = jnp.where(m0, outs[0], outs[1])

    in_specs = [pl.BlockSpec((S, qw), lambda h: (0, q_off * LANES // qw + h)),
                pl.BlockSpec((S, qw), lambda h: (0, k_off * LANES // qw + h)),
                pl.BlockSpec((S, LANES), lambda h: (0, v_off + h))]
    ins = [q, k, v]
    if forget:
        in_specs += [pl.BlockSpec((1, S, 2), lambda h: (h, 0, 0)), pl.BlockSpec((1, 2, S), lambda h: (h, 0, 0))]
        ins += [cum, cum_t]
    return pl.pallas_call(
        body, name=name, grid=(PAIRS,), in_specs=in_specs, out_specs=pl.BlockSpec((S, LANES), lambda h: (0, h)),
        out_shape=jax.ShapeDtypeStruct((S, PAIRS * LANES), F32), compiler_params=_params(("arbitrary",)),
    )(*ins)


def attn_bwd(q, k, v, o, do, name, *, wide, scale, out_dtype=F32, q_off=0, k_off=0, v_off=0, cum=None, cum_t=None):
    qw = 2 * LANES if wide else LANES
    forget = cum is not None

    def body(*refs):
        q_ref, k_ref, v_ref, o_ref, do_ref = refs[:5]
        n_out = 5 if forget else 3
        outs = refs[-(n_out + 2):-2]
        dq_ref, dk_ref, dv_ref = outs[:3]
        dk_acc, dv_acc = refs[-2], refs[-1]
        dk_acc[...] = jnp.zeros_like(dk_acc)
        dv_acc[...] = jnp.zeros_like(dv_acc)
        if forget:
            dcq_ref, dck_ref = outs[3], outs[4]
            dck_ref[...] = jnp.zeros_like(dck_ref)
        m0, m1 = _pair_masks((TQ, LANES))
        for qi in range(S // TQ):
            b0, b1 = qi * TQ, (qi + 1) * TQ
            do2 = do_ref[b0:b1, :]
            dd = do2 * o_ref[b0:b1, :]
            do_b = do2.astype(BF16)
            mk0, mk1 = _pair_masks((b1, LANES))
            dqs = []
            for a, (msk, mk) in enumerate(((m0, mk0), (m1, mk1))):
                lanes = slice(LANES * a, LANES * (a + 1)) if wide else slice(0, LANES)
                if wide:
                    q_a, k_a = q_ref[b0:b1, lanes], k_ref[:b1, lanes]
                else:
                    q_a, k_a = jnp.where(msk, q_ref[b0:b1, :], jnp.zeros((), BF16)), k_ref[:b1, :]
                cq = refs[5][0, b0:b1, a:a + 1] if forget else None
                ck = refs[6][0, a:a + 1, :b1] if forget else None
                p = _causal_probs(q_a, k_a, scale, b0, cq, ck)
                dp = lax.dot_general(jnp.where(msk, do_b, jnp.zeros((), BF16)), v_ref[:b1, :], _DIMS["nt"],
                                     preferred_element_type=F32)
                delta = jnp.sum(jnp.where(msk, dd, 0.0), axis=-1, keepdims=True)
                ds = p * (dp - delta)
                if forget:
                    dcq_ref[0, b0:b1, a:a + 1] = jnp.sum(ds, axis=-1, keepdims=True)
                    dck_ref[0, a:a + 1, :b1] -= jnp.sum(ds, axis=0, keepdims=True)
                ds_b = (ds * scale).astype(BF16)
                dqs.append(jnp.dot(ds_b, k_a, preferred_element_type=F32))
                dk_acc[:b1, lanes] += lax.dot_general(ds_b, q_a, _DIMS["tn"], preferred_element_type=F32)
                dv_acc[:b1, :] += jnp.where(mk, lax.dot_general(p.astype(BF16), do_b, _DIMS["tn"],
                                                                 preferred_element_type=F32), 0.0)
            if wide:
                dq_ref[b0:b1, :LANES] = dqs[0].astype(out_dtype)
                dq_ref[b0:b1, LANES:] = dqs[1].astype(out_dtype)
            else:
                dq_ref[b0:b1, :] = jnp.where(m0, dqs[0], dqs[1]).astype(out_dtype)
        dk_ref[...] = dk_acc[...].astype(out_dtype)
        dv_ref[...] = dv_acc[...].astype(out_dtype)

    pair = pl.BlockSpec((S, LANES), lambda h: (0, h))
    qk_out = pl.BlockSpec((S, qw), lambda h: (0, h))
    in_specs = [pl.BlockSpec((S, qw), lambda h: (0, q_off * LANES // qw + h)),
                pl.BlockSpec((S, qw), lambda h: (0, k_off * LANES // qw + h)),
                pl.BlockSpec((S, LANES), lambda h: (0, v_off + h)), pair, pair]
    ins = [q, k, v, o, do]
    out_specs = [qk_out, qk_out, pair]
    out_shape = [jax.ShapeDtypeStruct((S, PAIRS * qw), out_dtype)] * 2 + \
                [jax.ShapeDtypeStruct((S, PAIRS * LANES), out_dtype)]
    if forget:
        by_q, by_k = pl.BlockSpec((1, S, 2), lambda h: (h, 0, 0)), pl.BlockSpec((1, 2, S), lambda h: (h, 0, 0))
        in_specs += [by_q, by_k]
        ins += [cum, cum_t]
        out_specs += [by_q, by_k]
        out_shape += [jax.ShapeDtypeStruct((PAIRS, S, 2), F32), jax.ShapeDtypeStruct((PAIRS, 2, S), F32)]
    return pl.pallas_call(
        body, name=name, grid=(PAIRS,), in_specs=in_specs, out_specs=out_specs, out_shape=out_shape,
        scratch_shapes=[pltpu.VMEM((S, qw), F32), pltpu.VMEM((S, LANES), F32)],
        compiler_params=_params(("arbitrary",)),
    )(*ins)


def _tri(lower):
    r = lax.broadcasted_iota(jnp.int32, (QBLK, QBLK), 0)
    c = lax.broadcasted_iota(jnp.int32, (QBLK, QBLK), 1)
    return jnp.where((c <= r) if lower else (c >= r), 1.0, 0.0).astype(F32)


def _hi_dot(a, b):
    return jnp.dot(a, b, precision=lax.Precision.HIGHEST, preferred_element_type=F32)


def fox_gate_fwd(fl, bias, name):
    def body(f_ref, b_ref, o_ref):
        tri = _tri(True)
        carry = jnp.zeros((1, LANES), F32)
        for n in range(S // QBLK):
            x = f_ref[n * QBLK:(n + 1) * QBLK, :].astype(F32) + b_ref[...]
            lf = jnp.minimum(x, 0.0) - jnp.log(1.0 + jnp.exp(-jnp.abs(x)))
            c = _hi_dot(tri, lf) + carry
            o_ref[n * QBLK:(n + 1) * QBLK, :] = c
            carry = c[QBLK - 1:QBLK, :]

    return pl.pallas_call(body, name=name, out_shape=jax.ShapeDtypeStruct((S, LANES), F32),
                          compiler_params=_params())(fl, bias)


def fox_gate_bwd(fl, bias, dcq, dck, name):
    def body(f_ref, b_ref, dq_ref, dk_ref, o_ref, db_ref):
        tri = _tri(False)
        carry = jnp.zeros((1, LANES), F32)
        db = jnp.zeros((1, LANES), F32)
        for n in reversed(range(S // QBLK)):
            rows = slice(n * QBLK, (n + 1) * QBLK)
            dlf = _hi_dot(tri, dq_ref[rows, :] + dk_ref[rows, :]) + carry
            carry = dlf[0:1, :]
            x = f_ref[rows, :].astype(F32) + b_ref[...]
            dx = dlf * (1.0 - _sigmoid(x))
            o_ref[rows, :] = dx
            db = db + jnp.sum(dx, axis=0, keepdims=True)
        db_ref[...] = db

    return pl.pallas_call(body, name=name, out_shape=[jax.ShapeDtypeStruct((S, LANES), F32),
                                                      jax.ShapeDtypeStruct((1, LANES), F32)],
                          compiler_params=_params())(fl, bias, dcq, dck)


def _band_valid(first):
    w = QBLK if first else 2 * QBLK
    i = lax.broadcasted_iota(jnp.int32, (QBLK, w), 0)
    j = lax.broadcasted_iota(jnp.int32, (QBLK, w), 1)
    return (j <= i) if first else ((j >= i) & (j - QBLK <= i))


def _band_logits(q_a, kk, bias, first):
    s = lax.dot_general(q_a, kk, _DIMS["nt"], preferred_element_type=F32) * 0.125 + bias
    return jnp.where(_band_valid(first), s, NEG)


def dil_fwd(qkv, bias, g, name):
    d = DIL[g][1]
    ls = S // d
    view = qkv.reshape(ls, d * 9216)

    def body(q_ref, k_ref, v_ref, b_ref, o_ref, l_ref):
        m0, m1 = _pair_masks((QBLK, LANES))
        for n in range(ls // QBLK):
            rows = slice(n * QBLK, (n + 1) * QBLK)
            keys = rows if n == 0 else slice((n - 1) * QBLK, (n + 1) * QBLK)
            os_, ls_ = [], []
            for a, msk in enumerate((m0, m1)):
                q_a = jnp.where(msk, q_ref[rows, :], jnp.zeros((), BF16))
                bias_a = b_ref[a, :, QBLK:] if n == 0 else b_ref[a]
                s = _band_logits(q_a, k_ref[keys, :], bias_a, n == 0)
                mx = jnp.max(s, axis=-1, keepdims=True)
                e = jnp.exp(s - mx)
                l = jnp.sum(e, axis=-1, keepdims=True)
                os_.append(jnp.dot((e / l).astype(BF16), v_ref[keys, :], preferred_element_type=F32))
                ls_.append(mx + jnp.log(l))
            o_ref[rows, :] = jnp.where(m0, os_[0], os_[1])
            l_ref[rows, :] = jnp.where(m0, ls_[0], ls_[1])

    def col(j):
        return lambda h, r: (0, r * 72 + g * 24 + j * 8 + h)

    out = pl.BlockSpec((ls, LANES), lambda h, r: (0, r * 8 + h))
    o, lse = pl.pallas_call(
        body, name=name, grid=(PAIRS, d),
        in_specs=[pl.BlockSpec((ls, LANES), col(0)), pl.BlockSpec((ls, LANES), col(1)), pl.BlockSpec((ls, LANES), col(2)),
                  pl.BlockSpec((2, QBLK, 2 * QBLK), lambda h, r: (h, 0, 0))],
        out_specs=[out, out], out_shape=[jax.ShapeDtypeStruct((ls, d * D), F32)] * 2,
        compiler_params=_params(("arbitrary", "arbitrary")),
    )(view, view, view, bias)
    return o.reshape(S, D), lse.reshape(S, D)


def dil_merge(os_, lses, name):
    def body(o0, o1, o2, l0, l1, l2, o_ref, l_ref):
        ls_ = [l0[...], l1[...], l2[...]]
        mx = jnp.maximum(jnp.maximum(ls_[0], ls_[1]), ls_[2])
        tot = mx + jnp.log(sum(jnp.exp(l - mx) for l in ls_))
        o_ref[...] = sum(jnp.exp(l - tot) * o[...] for l, o in zip(ls_, (o0, o1, o2)))
        l_ref[...] = tot

    return _rows_call(body, name, list(os_) + list(lses), [(D, F32), (D, F32)])


def dil_bwd(qkv, bias, o, lse, do, g, name):
    d = DIL[g][1]
    ls = S // d
    view = qkv.reshape(ls, d * 9216)
    o, lse, do = (t.reshape(ls, d * D) for t in (o, lse, do))

    def body(q_ref, k_ref, v_ref, b_ref, o_ref, l_ref, do_ref, dq_ref, dk_ref, dv_ref, db_ref, dk_acc, dv_acc):
        @pl.when(pl.program_id(1) == 0)
        def _():
            db_ref[...] = jnp.zeros_like(db_ref)

        dk_acc[...] = jnp.zeros_like(dk_acc)
        dv_acc[...] = jnp.zeros_like(dv_acc)
        m0, m1 = _pair_masks((QBLK, LANES))
        for n in range(ls // QBLK):
            rows = slice(n * QBLK, (n + 1) * QBLK)
            keys = rows if n == 0 else slice((n - 1) * QBLK, (n + 1) * QBLK)
            nk = QBLK if n == 0 else 2 * QBLK
            do2, lse2 = do_ref[rows, :], l_ref[rows, :]
            dd = do2 * o_ref[rows, :]
            do_b = do2.astype(BF16)
            mk0, mk1 = _pair_masks((nk, LANES))
            dqs = []
            for a, (msk, mk) in enumerate(((m0, mk0), (m1, mk1))):
                q_a = jnp.where(msk, q_ref[rows, :], jnp.zeros((), BF16))
                kk = k_ref[keys, :]
                bias_a = b_ref[a, :, QBLK:] if n == 0 else b_ref[a]
                s = _band_logits(q_a, kk, bias_a, n == 0)
                lse_a = jnp.max(jnp.where(msk, lse2, -jnp.inf), axis=-1, keepdims=True)
                p = jnp.exp(s - lse_a)
                dp = lax.dot_general(jnp.where(msk, do_b, jnp.zeros((), BF16)), v_ref[keys, :], _DIMS["nt"],
                                     preferred_element_type=F32)
                delta = jnp.sum(jnp.where(msk, dd, 0.0), axis=-1, keepdims=True)
                ds = p * (dp - delta)
                if n == 0:
                    db_ref[a, :, QBLK:] += ds
                else:
                    db_ref[a] += ds
                ds_b = (ds * 0.125).astype(BF16)
                dqs.append(jnp.dot(ds_b, kk, preferred_element_type=F32))
                dk_acc[keys, :] += lax.dot_general(ds_b, q_a, _DIMS["tn"], preferred_element_type=F32)
                dv_acc[keys, :] += jnp.where(mk, lax.dot_general(p.astype(BF16), do_b, _DIMS["tn"],
                                                                 preferred_element_type=F32), 0.0)
            dq_ref[rows, :] = jnp.where(m0, dqs[0], dqs[1]).astype(BF16)
        dk_ref[...] = dk_acc[...].astype(BF16)
        dv_ref[...] = dv_acc[...].astype(BF16)

    def col(j):
        return lambda h, r: (0, r * 72 + g * 24 + j * 8 + h)

    nat = pl.BlockSpec((ls, LANES), lambda h, r: (0, r * 8 + h))
    b_spec = pl.BlockSpec((2, QBLK, 2 * QBLK), lambda h, r: (h, 0, 0))
    dq, dk, dv, db = pl.pallas_call(
        body, name=name, grid=(PAIRS, d),
        in_specs=[pl.BlockSpec((ls, LANES), col(0)), pl.BlockSpec((ls, LANES), col(1)), pl.BlockSpec((ls, LANES), col(2)),
                  b_spec, nat, nat, nat],
        out_specs=[nat, nat, nat, b_spec],
        out_shape=[jax.ShapeDtypeStruct((ls, d * D), BF16)] * 3 + [jax.ShapeDtypeStruct((HEADS, QBLK, 2 * QBLK), F32)],
        scratch_shapes=[pltpu.VMEM((ls, LANES), F32), pltpu.VMEM((ls, LANES), F32)],
        compiler_params=_params(("arbitrary", "arbitrary")),
    )(view, view, view, bias, o, lse, do)
    return dq.reshape(S, D), dk.reshape(S, D), dv.reshape(S, D), db


def _place():
    x, y, c = lax.axis_index("x"), lax.axis_index("y"), lax.axis_index("c")
    return x, y, c


def _dev_slot(ref, by_rows, dev):
    return ref.at[:, dev] if by_rows else ref.at[dev]


def all_gather(shards, by_rows, name, in_vmem=False):
    n = len(shards)

    def body(*refs):
        x_refs, out_refs = refs[:n], refs[n:2 * n]
        send_sems, recv_sems, local_sems = refs[2 * n:]
        x, y, c = _place()
        me, sibling = (x, y, c), (x, y, 1 - c)
        chips = [(1 - x, y), (x, 1 - y), (1 - x, 1 - y)]

        def slot(t, px, py, pc):
            return _dev_slot(out_refs[t], by_rows[t], 4 * px + 2 * py + pc)

        def copy(t, k, blk, to, src=None):
            return pltpu.make_async_remote_copy(
                src_ref=slot(t, *blk) if src is None else src, dst_ref=slot(t, *blk), send_sem=send_sems.at[7 * t + k],
                recv_sem=recv_sems.at[7 * t + k], device_id=to, device_id_type=MESH_ID)

        mine = [pltpu.make_async_copy(x_refs[t], slot(t, *me), local_sems.at[t]) for t in range(n)]
        for cp in mine:
            cp.start()
        first = []
        for t in range(n):
            first.append(copy(t, 0, me, sibling, src=x_refs[t]))
            first += [copy(t, 1 + j, me, (*chip, c), src=x_refs[t]) for j, chip in enumerate(chips)]
        for cp in first:
            cp.start()
        passed = []
        for j, chip in enumerate(chips):
            for t in range(n):
                copy(t, 1 + j, (*chip, c), me).wait_recv()
                passed.append(copy(t, 4 + j, (*chip, c), sibling))
                passed[-1].start()
        for t in range(n):
            copy(t, 0, sibling, me).wait_recv()
            for j, chip in enumerate(chips):
                copy(t, 4 + j, (*chip, 1 - c), me).wait_recv()
        for cp in first + passed:
            cp.wait_send()
        for cp in mine:
            cp.wait()

    def gathered(s, rows):
        shp = (s.shape[0], N_DEV) + s.shape[1:] if rows else (N_DEV,) + s.shape
        return jax.ShapeDtypeStruct(shp, s.dtype)

    space = pl.BlockSpec(memory_space=pltpu.VMEM if in_vmem else pl.ANY)
    return pl.pallas_call(
        body, name=name, out_shape=[gathered(s, r) for s, r in zip(shards, by_rows)],
        in_specs=[space] * n, out_specs=[space] * n,
        scratch_shapes=[pltpu.SemaphoreType.DMA((7 * n,)), pltpu.SemaphoreType.DMA((7 * n,)),
                        pltpu.SemaphoreType.DMA((n,))],
        compiler_params=pltpu.CompilerParams(vmem_limit_bytes=VMEM_LIMIT),
    )(*shards)


_HBM = pl.BlockSpec(memory_space=pltpu.HBM)
_SEM = pl.BlockSpec(memory_space=pltpu.SEMAPHORE)
_SPLIT = dict(has_side_effects=pltpu.SideEffectType.DATAFLOW_SIDE_EFFECTING)


def _hbm(a):
    return pltpu.with_memory_space_constraint(a, pltpu.HBM)


def _gathered_shape(s, rows):
    return (s.shape[0], N_DEV) + s.shape[1:] if rows else (N_DEV,) + s.shape


def _peers(x, y, c):
    return [(1 - x if k & 4 else x, 1 - y if k & 2 else y, 1 - c if k & 1 else c) for k in range(1, N_DEV)]


def gather_start(shards, lands, by_rows, name):
    n = len(shards)

    def body(*refs):
        x_refs, land_refs = refs[:n], refs[n:2 * n]
        send_sems, recv_sems = refs[2 * n], refs[2 * n + 1]
        x, y, c = _place()
        me = 4 * x + 2 * y + c
        for t in range(n):
            for k, peer in enumerate(_peers(x, y, c)):
                pltpu.make_async_remote_copy(
                    src_ref=x_refs[t], dst_ref=_dev_slot(land_refs[t], by_rows[t], me), send_sem=send_sems.at[7 * t + k],
                    recv_sem=recv_sems.at[7 * t + k], device_id=peer, device_id_type=MESH_ID).start()

    sems = pltpu.SemaphoreType.DMA((7 * n,))
    res = pl.pallas_call(
        body, name=name,
        out_shape=(sems, sems) + tuple(pltpu.HBM(a.shape, a.dtype) for a in list(shards) + list(lands)),
        in_specs=[_HBM] * (2 * n), out_specs=(_SEM, _SEM) + (_HBM,) * (2 * n),
        input_output_aliases={i: 2 + i for i in range(2 * n)},
        compiler_params=pltpu.CompilerParams(**_SPLIT),
    )(*[_hbm(a) for a in list(shards) + list(lands)])
    return res[0], res[1], list(res[2:2 + n]), list(res[2 + n:])


def gather_wait(send_sems, recv_sems, first, shards, lands, by_rows, after, name):
    n = len(shards)

    def body(*refs):
        x_refs, land_refs = refs[:n], refs[n:2 * n]
        send_sems, recv_sems = refs[2 * n], refs[2 * n + 1]
        x, y, c = _place()
        for t in range(n):
            for k, (px, py, pc) in enumerate(_peers(x, y, c)):
                cp = pltpu.make_async_remote_copy(
                    src_ref=x_refs[t], dst_ref=_dev_slot(land_refs[t], by_rows[t], 4 * px + 2 * py + pc),
                    send_sem=send_sems.at[7 * (first + t) + k], recv_sem=recv_sems.at[7 * (first + t) + k],
                    device_id=(px, py, pc), device_id_type=MESH_ID)
                cp.wait_send()
                cp.wait_recv()

    res = pl.pallas_call(
        body, name=name, out_shape=tuple(pltpu.HBM(a.shape, a.dtype) for a in list(shards) + list(lands)),
        in_specs=[_HBM] * (2 * n) + [_SEM, _SEM, pl.BlockSpec(memory_space=pl.ANY)], out_specs=(_HBM,) * (2 * n),
        input_output_aliases={i: i for i in range(2 * n)},
        compiler_params=pltpu.CompilerParams(**_SPLIT),
    )(*shards, *lands, send_sems, recv_sems, after)
    return list(res[n:])


def place_own(shards, by_rows, name):
    n = len(shards)

    def body(*refs):
        x_refs, out_refs, sems = refs[:n], refs[n:2 * n], refs[2 * n]
        x, y, c = _place()
        mine = [pltpu.make_async_copy(x_refs[t], _dev_slot(out_refs[t], by_rows[t], 4 * x + 2 * y + c), sems.at[t])
                for t in range(n)]
        for cp in mine:
            cp.start()
        for cp in mine:
            cp.wait()

    space = pl.BlockSpec(memory_space=pl.ANY)
    return pl.pallas_call(
        body, name=name, out_shape=[jax.ShapeDtypeStruct(_gathered_shape(s, r), s.dtype) for s, r in zip(shards, by_rows)],
        in_specs=[space] * n, out_specs=[space] * n, scratch_shapes=[pltpu.SemaphoreType.DMA((n,))],
        compiler_params=pltpu.CompilerParams(vmem_limit_bytes=VMEM_LIMIT),
    )(*shards)


def scatter_start(srcs, src_l, lands, land_l, by_rows, name):
    n = len(srcs)

    def body(*refs):
        x_refs, land_refs = refs[:n], refs[n:2 * n]
        send_sems, recv_sems, token = refs[2 * n], refs[2 * n + 1], refs[-1]
        x, y, c = _place()
        me = 4 * x + 2 * y + c
        for k, (px, py, pc) in enumerate(_peers(x, y, c)):
            for t in range(n):
                blk = _dev_slot(x_refs[t], by_rows[t], 4 * px + 2 * py + pc)
                pltpu.make_async_remote_copy(
                    src_ref=blk.at[src_l[t]], dst_ref=land_refs[t].at[me, land_l[t]], send_sem=send_sems.at[7 * t + k],
                    recv_sem=recv_sems.at[7 * t + k], device_id=(px, py, pc), device_id_type=MESH_ID).start()
        token[...] = jnp.zeros_like(token)

    lands = [lax.empty((N_DEV, 1) + s.shape[2:], s.dtype) if ld is None else ld for s, ld in zip(srcs, lands)]
    sems = pltpu.SemaphoreType.DMA((7 * n,))
    res = pl.pallas_call(
        body, name=name,
        out_shape=(sems, sems) + tuple(pltpu.HBM(a.shape, a.dtype) for a in list(srcs) + lands)
        + (jax.ShapeDtypeStruct((8, LANES), F32),),
        in_specs=[_HBM] * (2 * n),
        out_specs=(_SEM, _SEM) + (_HBM,) * (2 * n) + (pl.BlockSpec(memory_space=pltpu.VMEM),),
        input_output_aliases={i: 2 + i for i in range(2 * n)},
        compiler_params=pltpu.CompilerParams(**_SPLIT),
    )(*[_hbm(a) for a in list(srcs) + lands])
    return res[0], res[1], list(res[2:2 + n]), list(res[2 + n:2 + 2 * n]), res[-1]


def scatter_wait(send_sems, recv_sems, srcs, src_l, lands, land_l, by_rows, after, name):
    n = len(srcs)

    def body(*refs):
        x_refs, land_refs = refs[:n], refs[n:2 * n]
        send_sems, recv_sems = refs[2 * n], refs[2 * n + 1]
        x, y, c = _place()
        for k, (px, py, pc) in enumerate(_peers(x, y, c)):
            peer = 4 * px + 2 * py + pc
            for t in range(n):
                cp = pltpu.make_async_remote_copy(
                    src_ref=_dev_slot(x_refs[t], by_rows[t], peer).at[src_l[t]], dst_ref=land_refs[t].at[peer, land_l[t]],
                    send_sem=send_sems.at[7 * t + k], recv_sem=recv_sems.at[7 * t + k], device_id=(px, py, pc),
                    device_id_type=MESH_ID)
                cp.wait_send()
                cp.wait_recv()

    res = pl.pallas_call(
        body, name=name, out_shape=tuple(pltpu.HBM(a.shape, a.dtype) for a in list(srcs) + list(lands)),
        in_specs=[_HBM] * (2 * n) + [_SEM, _SEM, pl.BlockSpec(memory_space=pl.ANY)], out_specs=(_HBM,) * (2 * n),
        input_output_aliases={i: i for i in range(2 * n)},
        compiler_params=pltpu.CompilerParams(**_SPLIT),
    )(*srcs, *lands, send_sems, recv_sems, after)
    return list(res[:n]), list(res[n:])


ADAM_BLOCK_BYTES = 3 << 19


def adamw(w, m, v, parts, name):
    n_parts = parts.shape[0]
    n_l, r, c = w.shape
    lane_c = -(-c // LANES) * LANES
    fits = [t for t in range(16, r, 16) if r % t == 0 and t * lane_c * 4 <= ADAM_BLOCK_BYTES]
    tr = max(fits) if fits and r * lane_c * 4 > ADAM_BLOCK_BYTES else r
    c1 = 1.0 / (1.0 - ADAM_B1 ** ADAM_STEP)
    c2 = 1.0 / (1.0 - ADAM_B2 ** ADAM_STEP)

    def body(w_ref, m_ref, v_ref, p_ref, g_ref, d_ref, nm_ref, nv_ref):
        g = p_ref[0].astype(F32)
        for j in range(1, n_parts):
            g = g + p_ref[j].astype(F32)
        nm = ADAM_B1 * m_ref[...] + (1.0 - ADAM_B1) * g
        nv = ADAM_B2 * v_ref[...] + (1.0 - ADAM_B2) * (g * g)
        g_ref[...] = g
        nm_ref[...] = nm
        nv_ref[...] = nv
        d_ref[...] = -ADAM_LR * ((nm * c1) / (jnp.sqrt(nv * c2) + ADAM_EPS) + ADAM_WD * w_ref[...])

    blk = pl.BlockSpec((1, tr, c), lambda l, i: (l, i, 0))
    return pl.pallas_call(
        body, name=name, grid=(n_l, r // tr),
        in_specs=[blk, blk, blk, pl.BlockSpec((n_parts, 1, tr, c), lambda l, i: (0, l, i, 0))],
        out_specs=[blk] * 4, out_shape=[jax.ShapeDtypeStruct((n_l, r, c), F32)] * 4,
        compiler_params=_params(("parallel", "parallel")),
    )(w, m, v, parts)


def sum_parts(parts, name):
    def body(p_ref, o_ref):
        g = p_ref[0]
        for j in range(1, parts.shape[0]):
            g = g + p_ref[j]
        o_ref[...] = g

    return pl.pallas_call(body, name=name, out_shape=jax.ShapeDtypeStruct(parts.shape[1:], F32),
                          compiler_params=_params())(parts)


def _pack(arrs, rows, dtype):
    flat = jnp.concatenate([a.reshape(-1).astype(dtype) for a in arrs])
    return jnp.pad(flat, (0, rows * LANES - flat.shape[0])).reshape(rows, LANES)


def _unpack(packed, shapes):
    flat, out, off = packed.reshape(-1), [], 0
    for shp in shapes:
        n = int(np.prod(shp))
        out.append(flat[off:off + n].reshape(shp))
        off += n
    return out


def _cat(parts):
    return jnp.concatenate(parts, axis=1)


def _layer_list(i):
    kind, j = i % 3, i // 3
    mix = ([("mla_w_a", j), ("mla_w_uq", j), ("mla_w_ukv", j), ("mla_w_o", j)] if kind == 0 else
           [("dil_w_qkv", 0), ("dil_w_o", 0)] if kind == 1 else [("fox_w_qkvf", 0), ("fox_w_o", 0)])
    return mix + [("ffn_w_in", i), ("ffn_w_out", i), ("ple_w_proj", i), ("ple_w_gate", i)]


def _mla_layout(w_a, g_uq, g_ukv, j):
    def z(r, n):
        return jnp.zeros((r, n), BF16)

    wa = w_a[j]
    a = _cat([wa[:, :640], wa[:, 640:656], z(D, 48), wa[:, 656:672], z(D, 48)])
    q, k, v = [], [], []
    for h in range(HEADS):
        b = g_uq[h // 2, j][:, 96 * (h % 2):96 * (h % 2 + 1)]
        q += [b[:, 64:80], b[:, 0:32], z(Q_RANK, 16), b[:, 80:96], b[:, 32:64], z(Q_RANK, 16)]
        b = g_ukv[h // 2, j][:, LANES * (h % 2):LANES * (h % 2 + 1)]
        k += [z(KV_RANK, 16), b[:, 0:32], z(KV_RANK, 32), b[:, 32:64], z(KV_RANK, 16)]
        v.append(b[:, 64:128])
    return a, _cat(q), _cat(k + v)


def _mla_unlayout(d_a, d_uq, d_ukv):
    a = _cat([d_a[:, :640], d_a[:, 640:656], d_a[:, 704:720]])
    uq, ukv = [], []
    for dev in range(N_DEV):
        q, kv = [], []
        for h in (2 * dev, 2 * dev + 1):
            b = d_uq[:, LANES * h:LANES * (h + 1)]
            q += [b[:, 16:48], b[:, 80:112], b[:, 0:16], b[:, 64:80]]
            b = d_ukv[:, LANES * h:LANES * (h + 1)]
            kv += [b[:, 16:48], b[:, 80:112], d_ukv[:, HEADS * LANES + 64 * h:HEADS * LANES + 64 * (h + 1)]]
        uq.append(_cat(q))
        ukv.append(_cat(kv))
    return a, jnp.stack(uq), jnp.stack(ukv)


def _mixer_fwd(kind, tag, hn, W, aux):
    if kind == 0:
        a = mm(hn, W["w_a"], "nn", f"{tag}_a", tn=768)
        cq = rms_fwd(a[:, :Q_RANK], W["q_norm"], f"{tag}_cq", out_dtype=BF16)
        ckv = rms_fwd(a[:, Q_RANK:Q_RANK + KV_RANK], W["kv_norm"], f"{tag}_ckv", out_dtype=BF16)
        qp = mm(cq, W["w_uq"], "nn", f"{tag}_uq", tk=384)
        kvp = mm(ckv, W["w_ukv"], "nn", f"{tag}_ukv", tk=256)
        q, k = mla_qk_fwd(qp, kvp, a[:, 640:], aux["cos"], aux["sin"], f"{tag}_qk")
        v = kvp.astype(BF16)
        o = attn_fwd(q, k, v, f"{tag}_attn", wide=True, scale=96 ** -0.5, v_off=HEADS)
        y = mm(o, W["w_o"], "nn", f"{tag}_o")
        return y, (a, cq, ckv, q, k, v, o)
    if kind == 1:
        qkv = mm(hn, W["w_qkv"], "nn", f"{tag}_qkv", out_dtype=BF16, tn=1152)
        parts = [dil_fwd(qkv, aux["dil_bias"][g], g, f"{tag}_g{g}") for g in range(3)]
        o, lse = dil_merge([p_[0] for p_ in parts], [p_[1] for p_ in parts], f"{tag}_merge")
        y = mm(o, W["w_o"], "nn", f"{tag}_o")
        return y, (qkv, o, lse)
    a = mm(hn, W["w_qkvf"], "nn", f"{tag}_qkvf", tn=640)
    fl = a[:, 3072:]
    cum = fox_gate_fwd(fl, aux["fox_b"], f"{tag}_gate")[:, :HEADS]
    cum_q = cum.reshape(S, PAIRS, 2).transpose(1, 0, 2)
    cum_k = cum.T.reshape(PAIRS, 2, S)
    ab = a.astype(BF16)
    o = attn_fwd(ab, ab, ab, f"{tag}_attn", wide=False, scale=0.125, k_off=PAIRS, v_off=2 * PAIRS, cum=cum_q, cum_t=cum_k)
    y = mm(o, W["w_o"], "nn", f"{tag}_o")
    return y, (fl, ab, cum_q, cum_k, o)


def _mixer_bwd(kind, tag, hn, dy, W, aux, saved):
    gr = {}
    if kind == 0:
        a, cq, ckv, q, k, v, o = saved
        gr["w_o"] = mm(o, dy, "tn", f"{tag}_dwo", out_dtype=BF16)
        do = mm(dy, W["w_o"], "nt", f"{tag}_do")
        dq, dk, dv = attn_bwd(q, k, v, o, do, f"{tag}_attn_b", wide=True, scale=96 ** -0.5, v_off=HEADS)
        dqp, dkr = mla_qk_bwd(dq, dk, aux["cos"], aux["sin"], f"{tag}_qk_b")
        dkvp = jnp.concatenate([dk, dv], axis=1)
        gr["w_ukv"] = mm(ckv, dkvp, "tn", f"{tag}_dwukv", out_dtype=BF16, tm=256)
        dckv = mm(dkvp, W["w_ukv"], "nt", f"{tag}_dckv", tn=256)
        gr["w_uq"] = mm(cq, dqp, "tn", f"{tag}_dwuq", out_dtype=BF16, tm=384)
        dcq = mm(dqp, W["w_uq"], "nt", f"{tag}_dcq", tn=384)
        da_q, gr["q_norm"] = rms_bwd(a[:, :Q_RANK], W["q_norm"], dcq, f"{tag}_cq_b", out_dtype=BF16)
        da_kv, gr["kv_norm"] = rms_bwd(a[:, Q_RANK:Q_RANK + KV_RANK], W["kv_norm"], dckv, f"{tag}_ckv_b",
                                       out_dtype=BF16)
        da = jnp.concatenate([da_q, da_kv, dkr], axis=1)
        gr["w_a"] = mm(hn, da, "tn", f"{tag}_dwa", out_dtype=BF16, tn=768)
        return mm(da, W["w_a"], "nt", f"{tag}_dhn", tk=768), gr
    if kind == 1:
        qkv, o, lse = saved
        gr["w_o"] = mm(o, dy, "tn", f"{tag}_dwo", out_dtype=BF16)
        do = mm(dy, W["w_o"], "nt", f"{tag}_do")
        cols, dbs = [], []
        for g in range(3):
            dq, dk, dv, db = dil_bwd(qkv, aux["dil_bias"][g], o, lse, do, g, f"{tag}_g{g}_b")
            cols += [dq, dk, dv]
            dbs.append(db)
        dqkv = jnp.concatenate(cols, axis=1)
        gr["dil_dbias"] = dbs
        gr["w_qkv"] = mm(hn, dqkv, "tn", f"{tag}_dwqkv", out_dtype=BF16, out_dev=1152, tn=1152)
        return mm(dqkv, W["w_qkv"], "nt", f"{tag}_dhn", tk=1152), gr
    fl, ab, cum_q, cum_k, o = saved
    gr["w_o"] = mm(o, dy, "tn", f"{tag}_dwo", out_dtype=BF16)
    do = mm(dy, W["w_o"], "nt", f"{tag}_do")
    dq, dk, dv, dcq, dck = attn_bwd(ab, ab, ab, o, do, f"{tag}_attn_b", wide=False, scale=0.125, out_dtype=BF16,
                                    k_off=PAIRS, v_off=2 * PAIRS, cum=cum_q, cum_t=cum_k)
    pad = ((0, 0), (0, LANES - HEADS))
    dcq = jnp.pad(dcq.transpose(1, 0, 2).reshape(S, HEADS), pad)
    dck = jnp.pad(dck.reshape(HEADS, S).T, pad)
    dfl, gr["b_f"] = fox_gate_bwd(fl, aux["fox_b"], dcq, dck, f"{tag}_gate_b")
    da = jnp.concatenate([dq, dk, dv, dfl.astype(BF16)], axis=1)
    gr["w_qkvf"] = mm(hn, da, "tn", f"{tag}_dwqkvf", out_dtype=BF16, tn=640)
    return mm(da, W["w_qkvf"], "nt", f"{tag}_dhn", tk=640), gr


def kernel(x, p, positions, norm_g, ffn_w_in, ffn_w_out, ple_w_proj, ple_w_gate, rel_bias, mla_w_a, mla_q_norm, mla_kv_norm, mla_w_uq, mla_w_ukv, mla_w_o, dil_w_qkv, dil_w_o, fox_w_qkvf, fox_b_f, fox_w_o, loss_target, m_norm_g, m_ffn_w_in, m_ffn_w_out, m_ple_w_proj, m_ple_w_gate, m_rel_bias, m_mla_w_a, m_mla_q_norm, m_mla_kv_norm, m_mla_w_uq, m_mla_w_ukv, m_mla_w_o, m_dil_w_qkv, m_dil_w_o, m_fox_w_qkvf, m_fox_b_f, m_fox_w_o, v_norm_g, v_ffn_w_in, v_ffn_w_out, v_ple_w_proj, v_ple_w_gate, v_rel_bias, v_mla_w_a, v_mla_q_norm, v_mla_kv_norm, v_mla_w_uq, v_mla_w_ukv, v_mla_w_o, v_dil_w_qkv, v_dil_w_o, v_fox_w_qkvf, v_fox_b_f, v_fox_w_o):
    given = dict(locals())
    me = 4 * lax.axis_index("x") + 2 * lax.axis_index("y") + lax.axis_index("c")

    rows_of = {n: axis == 1 for n, _, axis in BIG}
    shape_of = {n: shp for n, shp, _ in BIG}
    lists = [_layer_list(i) for i in range(DEPTH)]
    flat = [nl for ls in lists for nl in ls]
    flat_rows = [rows_of[n] for n, _ in flat]
    shards = [given[n][l:l + 1].astype(BF16) for n, l in flat]
    lands = place_own(shards, flat_rows, "place_own")
    send_s, recv_s, shards, lands = gather_start(shards, lands, flat_rows, "gather_start")
    full = {}

    gain_rows = _rows(sum(int(np.prod(s)) for _, s, _ in SMALL_SHARDED))
    gains, = all_gather([_pack([given[n] for n, _, _ in SMALL_SHARDED], gain_rows, F32)], [False], "gather_gains",
                        in_vmem=True)
    gains = gains.reshape(N_DEV, gain_rows * LANES)
    off = 0
    for n, shp, axis in SMALL_SHARDED:
        cnt = int(np.prod(shp))
        g = jnp.moveaxis(gains[:, off:off + cnt].reshape((N_DEV,) + shp), 0, axis)
        full[n] = g.reshape(shp[:axis] + (N_DEV * shp[axis],))
        off += cnt

    cos, sin = rope_tables(positions.reshape(S, 1), "rope_tables")
    dil_bias = [mm(rel_bias[:, HEADS * g:HEADS * (g + 1)], jnp.asarray(_bucket_onehot(DIL[g][1])), "tn",
                   f"dil_bias{g}", precise=True, tn=4096).reshape(HEADS, QBLK, 2 * QBLK) for g in range(3)]
    aux = {"cos": cos, "sin": sin, "dil_bias": dil_bias,
           "fox_b": jnp.pad(fox_b_f, ((0, 0), (0, LANES - HEADS)))}

    def arrived(i, part, behind):
        n_mix = len(lists[i]) - 4
        first = sum(len(ls) for ls in lists[:i]) + (n_mix if part else 0)
        sl = slice(first, first + (4 if part else n_mix))
        got = gather_wait(send_s, recv_s, first, shards[sl], lands[sl], flat_rows[sl], behind, f"gather_wait{i}_{part}")
        return {n: g.reshape(1, N_DEV * shape_of[n][1], shape_of[n][2]) if rows_of[n] else g
                for (n, _), g in zip(flat[sl], got)}

    def mixer_weights(i, behind):
        kind, j = i % 3, i // 3
        w = arrived(i, 0, behind)
        W = {"g": [full["norm_g"][i, r][None, :] for r in range(4)]}
        if kind == 0:
            w_a, w_uq, w_ukv = _mla_layout(w["mla_w_a"], w["mla_w_uq"], w["mla_w_ukv"], 0)
            W.update(w_a=w_a, w_uq=w_uq, w_ukv=w_ukv, w_o=Lay(w["mla_w_o"], 0),
                     q_norm=full["mla_q_norm"][j][None, :], kv_norm=full["mla_kv_norm"][j][None, :])
        elif kind == 1:
            W.update(w_qkv=Dev(w["dil_w_qkv"], 0), w_o=Lay(w["dil_w_o"], 0))
        else:
            fox_w = jnp.pad(_cat([w["fox_w_qkvf"][dev, 0] for dev in range(N_DEV)]), ((0, 0), (0, FOX_W - 3088)))
            W.update(w_qkvf=fox_w, w_o=Lay(w["fox_w_o"], 0))
        return W

    def ffn_weights(i, behind):
        w = arrived(i, 1, behind)
        return {"w_in": Dev(w["ffn_w_in"], 0), "w_out": Lay(w["ffn_w_out"], 0), "w_proj": Dev(w["ple_w_proj"], 0),
                "w_gate": Lay(w["ple_w_gate"], 0)}

    h = x[0]
    saved, weights = [], []
    for i in range(DEPTH):
        kind, j, W = i % 3, i // 3, mixer_weights(i, h)
        weights.append(W)
        t = f"l{i}"
        hn = rms_fwd(h, W["g"][0], f"{t}_n0", out_dtype=BF16)
        y, mix = _mixer_fwd(kind, f"{t}_mix", hn, W, aux)
        W.update(ffn_weights(i, y))
        h1 = rms_fwd(y, W["g"][1], f"{t}_n1", res=h)
        fin = rms_fwd(h1, W["g"][2], f"{t}_n2", out_dtype=BF16)
        gu = mm(fin, W["w_in"], "nn", f"{t}_ffn_in", out_dev=FF_W, tn=FF_W)
        act = swiglu_fwd(gu, f"{t}_swiglu")
        f = mm(Dev(act, 0), W["w_out"], "nn", f"{t}_ffn_out", tk=FF_W)
        h2 = rms_fwd(f, W["g"][3], f"{t}_n3", res=h1)
        pp = mm(p[i, 0], W["w_proj"], "nn", f"{t}_ple_p", tn=LANES, tk=256)
        gt = mm(h2, W["w_gate"], "nn", f"{t}_ple_g")
        h3 = ple_fwd(h2, pp, gt, f"{t}_ple")
        saved.append((h, hn, y, h1, fin, gu, act, f, h2, pp, gt, mix))
        h = h3

    dh, loss_lanes = loss_head(h, loss_target[0], "loss_head")

    grads = {n: None for n, _, _ in BIG}
    landed = {n: (lax.empty((N_DEV,) + shp, BF16) if shp[0] > 1 else None) for n, shp, _ in BIG}
    in_flight = []
    g_norm = [[None] * 4 for _ in range(DEPTH)]
    g_qn, g_kvn = [None, None], [None, None]
    g_rel, g_bf = None, None

    def stacked(n):
        g = grads[n]
        return None if g is None else g.reshape(g.shape[0], N_DEV * g.shape[2], g.shape[3])

    def by_device(g):
        return g.reshape(g.shape[0], N_DEV, g.shape[1] // N_DEV, g.shape[2])

    def start(i, entries, mine, tag):
        names = [n for n, _ in entries]
        srcs = [mine[n] if n in mine else grads[n] for n in names]
        src_l = [0 if n in mine else i for n in names]
        land_l = [l if shape_of[n][0] > 1 else 0 for n, l in entries]
        rows = [rows_of[n] for n in names]
        s_sem, r_sem, srcs, got, token = scatter_start(srcs, src_l, [landed[n] for n in names], land_l, rows, tag)
        for n, src, ld in zip(names, srcs, got):
            landed[n] = ld
            if n in mine:
                mine[n] = src
            else:
                grads[n] = src
        in_flight.append((s_sem, r_sem, names, mine, src_l, land_l, rows))
        return token

    token = None
    for i in reversed(range(DEPTH)):
        kind, j, W = i % 3, i // 3, weights[i]
        t = f"l{i}b"
        h0, hn, y, h1, fin, gu, act, f, h2, pp, gt, mix = saved[i]
        dpp, dgt = ple_bwd(dh, pp, gt, f"{t}_ple")
        grads["ple_w_proj"] = mm(p[i, 0], dpp, "tn", f"{t}_dwp", out_dtype=BF16, out_dev=LANES, tm=256, tn=LANES,
                                 stack=(grads["ple_w_proj"], DEPTH, i), after=token)
        grads["ple_w_gate"] = by_device(mm(h2, dgt, "tn", f"{t}_dwg", out_dtype=BF16,
                                           stack=(stacked("ple_w_gate"), DEPTH, i)))
        dh2 = mm(dgt, W["w_gate"], "nt", f"{t}_dh2", add=dh)
        df, g_norm[i][3] = rms_bwd(f, W["g"][3], dh2, f"{t}_n3", out_dtype=BF16)
        grads["ffn_w_out"] = by_device(mm(Dev(act, 0), df, "tn", f"{t}_dwout", out_dtype=BF16, tm=FF_W,
                                          stack=(stacked("ffn_w_out"), DEPTH, i)))
        dact = mm(df, W["w_out"], "nt", f"{t}_dact", out_dev=FF_W, tn=FF_W)
        dgu = swiglu_bwd(gu, dact, f"{t}_swiglu")
        grads["ffn_w_in"] = mm(fin, Dev(dgu, 0), "tn", f"{t}_dwin", out_dtype=BF16, out_dev=FF_W, tn=FF_W,
                               stack=(grads["ffn_w_in"], DEPTH, i))
        token = start(i, lists[i][-4:], {}, f"scatter_ffn{i}")
        dfin = mm(Dev(dgu, 0), W["w_in"], "nt", f"{t}_dfin", after=token, tk=FF_W)
        dh1, g_norm[i][2] = rms_bwd(h1, W["g"][2], dfin, f"{t}_n2", res=dh2)
        dy, g_norm[i][1] = rms_bwd(y, W["g"][1], dh1, f"{t}_n1", out_dtype=BF16)
        dhn, gr = _mixer_bwd(kind, f"{t}_mix", hn, dy, W, aux, mix)
        if kind == 0:
            d_a, d_uq, d_ukv = _mla_unlayout(gr["w_a"], gr["w_uq"], gr["w_ukv"])
            mine = {"mla_w_a": by_device(d_a[None]), "mla_w_uq": d_uq[:, None], "mla_w_ukv": d_ukv[:, None],
                    "mla_w_o": by_device(gr["w_o"][None])}
            g_qn[j], g_kvn[j] = gr["q_norm"], gr["kv_norm"]
        elif kind == 1:
            mine = {"dil_w_qkv": gr["w_qkv"], "dil_w_o": by_device(gr["w_o"][None])}
            g_rel = jnp.concatenate(
                [mm(jnp.asarray(_bucket_onehot(DIL[g][1])), gr["dil_dbias"][g].reshape(HEADS, -1), "nt",
                    f"{t}_drel{g}", precise=True, tk=4096) for g in range(3)], axis=1)
        else:
            wide = gr["w_qkvf"]
            mine = {"fox_w_qkvf": jnp.stack([wide[:, 386 * dev:386 * (dev + 1)] for dev in range(N_DEV)])[:, None],
                    "fox_w_o": by_device(gr["w_o"][None])}
            g_bf = gr["b_f"][:, :HEADS]
        token = start(i, lists[i][:-4], mine, f"scatter_mix{i}")
        dh, g_norm[i][0] = rms_bwd(h0, W["g"][0], dhn, f"{t}_n0", res=dh1, after=token)
    grad_x = dh[None]

    own = {n: [] for n, _, _ in BIG}
    for idx, (s_sem, r_sem, names, mine, src_l, land_l, rows) in enumerate(in_flight):
        srcs = [mine[n] if n in mine else grads[n] for n in names]
        srcs, got = scatter_wait(s_sem, r_sem, srcs, src_l, [landed[n] for n in names], land_l, rows, dh,
                                 f"scatter_wait{idx}")
        for n, src, ld, sl, ll, rw in zip(names, srcs, got, src_l, land_l, rows):
            landed[n] = ld
            if n not in mine:
                grads[n] = src
            blk = lax.dynamic_index_in_dim(src, me, axis=1 if rw else 0, keepdims=False)[sl]
            own[n].append((ll, blk))
    big_out = []
    for n, _, _ in BIG:
        part = landed[n]
        for ll, blk in own[n]:
            part = lax.dynamic_update_slice(part, blk[None, None], (me, ll, 0, 0))
        big_out.append(adamw(given[n], given["m_" + n], given["v_" + n], part, f"adamw_{n}"))

    small_full = [jnp.stack([jnp.concatenate(r, axis=0) for r in g_norm]).reshape(-1),
                  jnp.concatenate(g_qn, axis=0).reshape(-1), jnp.concatenate(g_kvn, axis=0).reshape(-1),
                  g_rel.reshape(-1), g_bf.reshape(-1), loss_lanes.reshape(-1)]
    small_n = sum(a.shape[0] for a in small_full)
    small_rows = _rows(small_n)
    parts, = all_gather([_pack(small_full, small_rows, F32)], [False], "gather_small_grads", in_vmem=True)
    tot = _unpack(sum_parts(parts, "sum_small_grads"), [(4, 4, D), (2, Q_RANK), (2, KV_RANK), (32, 48), (1, 16), (LANES,)])
    loss = jnp.sum(tot[5])
    small_g = [lax.dynamic_slice_in_dim(tot[0], me * 128, 128, axis=2), lax.dynamic_slice_in_dim(tot[1], me * 48, 48, axis=1),
               lax.dynamic_slice_in_dim(tot[2], me * 32, 32, axis=1), tot[3], tot[4]]
    small_names = [n for n, _, _ in SMALL_SHARDED] + [n for n, _ in SMALL_REPL]
    small_shapes = [s for _, s, _ in SMALL_SHARDED] + [s for _, s in SMALL_REPL]
    s_rows = _rows(sum(int(np.prod(s)) for s in small_shapes))
    small_out = adamw(_pack([given[n] for n in small_names], s_rows, F32)[None],
                      _pack([given["m_" + n] for n in small_names], s_rows, F32)[None],
                      _pack([given["v_" + n] for n in small_names], s_rows, F32)[None],
                      _pack(small_g, s_rows, F32)[None, None], "adamw_small")
    small_out = [_unpack(o_, small_shapes) for o_ in small_out]

    res = [{}, {}, {}, {}]
    for k in range(4):
        for idx, (n, _, _) in enumerate(BIG):
            res[k][n] = big_out[idx][k]
        for idx, n in enumerate(small_names):
            res[k][n] = small_out[k][idx]
    return (loss, grad_x, *[res[0][n] for n in WEIGHTS], *[res[1][n] for n in WEIGHTS],
            *[res[2][n] for n in WEIGHTS], *[res[3][n] for n in WEIGHTS])
```

```python
import math
from typing import NamedTuple

import numpy as np
import jax
import jax.numpy as jnp
from jax import lax
from jax.experimental import pallas as pl
from jax.experimental.pallas import tpu as pltpu

F32 = jnp.float32
BF16 = jnp.bfloat16
MESH_ID = pl.DeviceIdType.MESH

N_DEV = 8
S = 2048
D = 1024
DEPTH = 4
D_FF = 2816
D_PLE = 256
EPS = 1e-6
NEG = -1e30
LANES = 128
HEADS = 16
PAIRS = 8
Q_RANK = 384
KV_RANK = 256
QBLK = 128
DIL = ((128, 1), (512, 4), (2048, 16))
REL_BUCKETS = 32
FOX_W = 3200
VMEM_LIMIT = 56 * 1024 * 1024

ADAM_LR, ADAM_B1, ADAM_B2, ADAM_EPS, ADAM_WD, ADAM_STEP = 1e-3, 0.9, 0.999, 1e-8, 0.01, 10


BIG = (
    ("ffn_w_in", (4, 1024, 704), 2), ("ffn_w_out", (4, 352, 1024), 1),
    ("ple_w_proj", (4, 256, 128), 2), ("ple_w_gate", (4, 128, 1024), 1),
    ("mla_w_a", (2, 128, 672), 1), ("mla_w_uq", (2, 384, 192), 2),
    ("mla_w_ukv", (2, 256, 256), 2), ("mla_w_o", (2, 128, 1024), 1),
    ("dil_w_qkv", (1, 1024, 1152), 2), ("dil_w_o", (1, 128, 1024), 1),
    ("fox_w_qkvf", (1, 1024, 386), 2), ("fox_w_o", (1, 128, 1024), 1),
)
SMALL_SHARDED = (("norm_g", (4, 4, 128), 2), ("mla_q_norm", (2, 48), 1), ("mla_kv_norm", (2, 32), 1))
SMALL_REPL = (("rel_bias", (32, 48)), ("fox_b_f", (1, 16)))
WEIGHTS = ("norm_g", "ffn_w_in", "ffn_w_out", "ple_w_proj", "ple_w_gate", "rel_bias", "mla_w_a", "mla_q_norm",
           "mla_kv_norm", "mla_w_uq", "mla_w_ukv", "mla_w_o", "dil_w_qkv", "dil_w_o", "fox_w_qkvf", "fox_b_f",
           "fox_w_o")


def _rows(n):
    return -(-n // (8 * LANES)) * 8


def _t5_bucket_np(dist):
    max_exact = REL_BUCKETS // 2
    n = np.maximum(dist.astype(np.float32), np.float32(1.0))
    large = max_exact + (np.log(n / np.float32(max_exact)) / np.float32(math.log(2048 / max_exact))
                         * np.float32(REL_BUCKETS - max_exact)).astype(np.int32)
    large = np.minimum(large, REL_BUCKETS - 1)
    return np.where(dist < max_exact, dist, large)


def _bucket_onehot(dilation):
    i = np.arange(QBLK)[:, None]
    j = np.arange(2 * QBLK)[None, :]
    bucket = _t5_bucket_np(np.clip(QBLK + i - j, 0, None) * dilation).reshape(-1)
    return (np.arange(REL_BUCKETS)[:, None] == bucket[None, :]).astype(np.float32)


def _rope_inv_lanes():
    half = 16
    inv = (np.float32(10000.0) ** (-np.arange(half, dtype=np.float32) / np.float32(half))).astype(np.float32)
    t = np.zeros((1, LANES), np.float32)
    t[0, 0:16] = inv
    t[0, 64:80] = inv
    return t


def _params(sem=None):
    return pltpu.CompilerParams(dimension_semantics=sem, vmem_limit_bytes=VMEM_LIMIT)


def _tile(dim, target):
    if dim <= target or dim % target == 0:
        return min(dim, target)
    t = (target // LANES) * LANES
    while dim % t:
        t -= LANES
    return t


_DIMS = {"nn": (((1,), (0,)), ((), ())), "nt": (((1,), (1,)), ((), ())), "tn": (((0,), (0,)), ((), ()))}


class Lay(NamedTuple):
    arr: jax.Array
    l: int


class Dev(NamedTuple):
    arr: jax.Array
    l: int


def _lshape(op):
    if isinstance(op, Dev):
        g, _, r, w = op.arr.shape
        return r, g * w
    return op.arr.shape[1:] if isinstance(op, Lay) else op.shape


def _op_spec(op, rows_t, cols_t, row_ix, col_ix):
    if isinstance(op, Dev):
        w = op.arr.shape[3]
        assert w % cols_t == 0 and (cols_t % LANES == 0 or cols_t == w), (w, cols_t)
        nb, l = w // cols_t, op.l
        return pl.BlockSpec((1, 1, rows_t, cols_t),
                            lambda i, j, k: (col_ix(i, j, k) // nb, l, row_ix(i, j, k), col_ix(i, j, k) % nb))
    if isinstance(op, Lay):
        l = op.l
        return pl.BlockSpec((1, rows_t, cols_t), lambda i, j, k: (l, row_ix(i, j, k), col_ix(i, j, k)))
    return pl.BlockSpec((rows_t, cols_t), lambda i, j, k: (row_ix(i, j, k), col_ix(i, j, k)))


def _mat(ref):
    return ref[(0,) * (len(ref.shape) - 2)]


def mm(a, b, mode, name, out_dtype=F32, precise=False, add=None, out_dev=None, stack=None, after=None,
       tm=1024, tn=512, tk=2048):
    (ar, ac), (br, bc) = _lshape(a), _lshape(b)
    M, K = (ac, ar) if mode == "tn" else (ar, ac)
    N = br if mode == "nt" else bc
    assert K == (bc if mode == "nt" else br)
    tm, tn, tk = _tile(M, tm), _tile(N, tn), _tile(K, tk)
    nk = K // tk
    ix_i, ix_j, ix_k = (lambda i, j, k: i), (lambda i, j, k: j), (lambda i, j, k: k)
    a_spec = _op_spec(a, tk, tm, ix_k, ix_i) if mode == "tn" else _op_spec(a, tm, tk, ix_i, ix_k)
    b_spec = _op_spec(b, tn, tk, ix_j, ix_k) if mode == "nt" else _op_spec(b, tk, tn, ix_k, ix_j)
    buf, n_l, l = stack if stack is not None else (None, 1, 0)
    if out_dev is not None:
        out = Dev(jax.ShapeDtypeStruct((N // out_dev, n_l, M, out_dev), out_dtype), l)
    elif stack is not None:
        out = Lay(jax.ShapeDtypeStruct((n_l, M, N), out_dtype), l)
    else:
        out = jax.ShapeDtypeStruct((M, N), out_dtype)
    o_spec = _op_spec(out, tm, tn, ix_i, ix_j)
    n_in = 3 if add is not None else 2

    def body(*refs):
        a_ref, b_ref = refs[0], refs[1]
        o_ref = refs[n_in + (buf is not None) + (after is not None)]
        if precise:
            part = lax.dot_general(_mat(a_ref), _mat(b_ref), _DIMS[mode], precision=lax.Precision.HIGHEST,
                                   preferred_element_type=F32)
        else:
            part = lax.dot_general(_mat(a_ref).astype(BF16), _mat(b_ref).astype(BF16), _DIMS[mode],
                                   preferred_element_type=F32)

        def finish(r):
            r = r + refs[2][...] if add is not None else r
            o_ref[...] = r.astype(o_ref.dtype).reshape(o_ref.shape)

        if nk == 1:
            finish(part)
            return
        acc, k = refs[-1], pl.program_id(2)

        @pl.when(k == 0)
        def _():
            acc[...] = part

        @pl.when(k > 0)
        def _():
            acc[...] += part

        @pl.when(k == nk - 1)
        def _():
            finish(acc[...])

    ins = [getattr(a, "arr", a), getattr(b, "arr", b)] + ([add] if add is not None else [])
    in_specs = [a_spec, b_spec] + ([o_spec] if add is not None else [])
    aliases = {}
    if buf is not None:
        ins.append(buf)
        in_specs.append(pl.BlockSpec(memory_space=pl.ANY))
        aliases = {n_in: 0}
    if after is not None:
        ins.append(after)
        in_specs.append(pl.BlockSpec(memory_space=pl.ANY))
    return pl.pallas_call(
        body, name=name, grid=(M // tm, N // tn, nk), in_specs=in_specs, out_specs=o_spec,
        out_shape=getattr(out, "arr", out), input_output_aliases=aliases,
        scratch_shapes=[pltpu.VMEM((tm, tn), F32)] if nk > 1 else [],
        compiler_params=_params(("parallel", "parallel", "arbitrary")),
    )(*ins)


def _rows_call(body, name, ins, outs, tr=256, acc_outs=()):
    n = ins[0].shape[0]
    tr = min(tr, n)
    in_specs = [pl.BlockSpec((tr, a.shape[1]), lambda i: (i, 0)) if a.shape[0] == n else
                pl.BlockSpec(a.shape, lambda i: (0, 0)) for a in ins]
    out_specs = [pl.BlockSpec((tr, w), lambda i: (i, 0)) for w, _ in outs] + \
                [pl.BlockSpec((1, w), lambda i: (0, 0)) for w in acc_outs]
    out_shape = [jax.ShapeDtypeStruct((n, w), dt) for w, dt in outs] + \
                [jax.ShapeDtypeStruct((1, w), F32) for w in acc_outs]
    res = pl.pallas_call(body, name=name, grid=(n // tr,), in_specs=in_specs, out_specs=out_specs,
                         out_shape=out_shape, compiler_params=_params(("arbitrary",)))(*ins)
    return res[0] if len(res) == 1 else res


def _acc(ref, val):
    @pl.when(pl.program_id(0) == 0)
    def _():
        ref[...] = jnp.zeros_like(ref)

    ref[...] += val


def rms_fwd(x, g, name, res=None, out_dtype=F32):
    def body(*refs):
        x_ref, g_ref = refs[0], refs[1]
        o_ref = refs[-1]
        xv = x_ref[...]
        y = xv * lax.rsqrt(jnp.mean(xv * xv, axis=-1, keepdims=True) + EPS) * g_ref[...]
        o_ref[...] = (y + refs[2][...] if res is not None else y).astype(o_ref.dtype)

    ins = [x, g] + ([res] if res is not None else [])
    return _rows_call(body, name, ins, [(x.shape[1], out_dtype)])


def rms_bwd(x, g, dy, name, res=None, out_dtype=F32, after=None):
    def body(*refs):
        x_ref, g_ref, dy_ref = refs[:3]
        dx_ref, dg_ref = refs[-2], refs[-1]
        xv, dyv = x_ref[...], dy_ref[...]
        r = lax.rsqrt(jnp.mean(xv * xv, axis=-1, keepdims=True) + EPS)
        xh = xv * r
        dxh = dyv * g_ref[...]
        dx = r * (dxh - xh * jnp.mean(dxh * xh, axis=-1, keepdims=True))
        dx_ref[...] = (dx + refs[3][...] if res is not None else dx).astype(dx_ref.dtype)
        _acc(dg_ref, jnp.sum(dyv * xh, axis=0, keepdims=True))

    ins = [x, g, dy] + ([res] if res is not None else []) + ([after] if after is not None else [])
    return _rows_call(body, name, ins, [(x.shape[1], out_dtype)], acc_outs=(x.shape[1],))


def _sigmoid(x):
    return 0.5 * jnp.tanh(0.5 * x) + 0.5


FF_W = 704
FF_TR = 512


def _ff_spec(shift):
    return pl.BlockSpec((1, 1, FF_TR, FF_W), lambda d, i: (d + shift, 0, i, 0))


def swiglu_fwd(gu, name):
    def body(g_ref, u_ref, o_ref):
        gate = g_ref[...]
        o_ref[...] = (gate * _sigmoid(gate) * u_ref[...]).astype(BF16)

    return pl.pallas_call(body, name=name, grid=(4, S // FF_TR), in_specs=[_ff_spec(0), _ff_spec(4)],
                          out_specs=_ff_spec(0), out_shape=jax.ShapeDtypeStruct((4, 1, S, FF_W), BF16),
                          compiler_params=_params(("parallel", "parallel")))(gu, gu)


def swiglu_bwd(gu, dact, name):
    def body(g_ref, u_ref, d_ref, o_ref):
        gate, d = g_ref[...], d_ref[...]
        sg = _sigmoid(gate)

        @pl.when(pl.program_id(0) < 4)
        def _():
            o_ref[...] = (d * u_ref[...] * sg * (1.0 + gate * (1.0 - sg))).astype(BF16)

        @pl.when(pl.program_id(0) >= 4)
        def _():
            o_ref[...] = (d * gate * sg).astype(BF16)

    def half(shift):
        return pl.BlockSpec((1, 1, FF_TR, FF_W), lambda d, i: (d % 4 + shift, 0, i, 0))

    return pl.pallas_call(body, name=name, grid=(8, S // FF_TR), in_specs=[half(0), half(4), half(0)],
                          out_specs=_ff_spec(0), out_shape=jax.ShapeDtypeStruct((8, 1, S, FF_W), BF16),
                          compiler_params=_params(("parallel", "parallel")))(gu, gu, dact)


def ple_fwd(h, pp, gt, name):
    def body(h_ref, p_ref, g_ref, o_ref):
        o_ref[...] = h_ref[...] + p_ref[...] * _sigmoid(g_ref[...])

    return _rows_call(body, name, [h, pp, gt], [(D, F32)])


def ple_bwd(dh, pp, gt, name):
    def body(d_ref, p_ref, g_ref, dp_ref, dg_ref):
        d, sg = d_ref[...], _sigmoid(g_ref[...])
        dp_ref[...] = (d * sg).astype(BF16)
        dg_ref[...] = (d * p_ref[...] * sg * (1.0 - sg)).astype(BF16)

    return _rows_call(body, name, [dh, pp, gt], [(D, BF16), (D, BF16)])


def loss_head(y, target, name):
    def body(y_ref, t_ref, d_ref, l_ref):
        e = y_ref[...] - t_ref[...]
        d_ref[...] = e * (1.0 / D)
        col = jnp.sum(e * e, axis=0, keepdims=True) * (0.5 / D)
        _acc(l_ref, sum(col[:, LANES * c:LANES * (c + 1)] for c in range(D // LANES)))

    return _rows_call(body, name, [y, target], [(D, F32)], acc_outs=(LANES,))


def rope_tables(pos_col, name):
    inv = jnp.asarray(_rope_inv_lanes())

    def body(p_ref, inv_ref, c_ref, s_ref):
        ang = p_ref[...].astype(F32) * inv_ref[...]
        lane = lax.broadcasted_iota(jnp.int32, ang.shape, 1)
        first, second = lane < 16, (lane >= 64) & (lane < 80)
        c_ref[...] = jnp.where(first | second, jnp.cos(ang), 1.0)
        sn = jnp.sin(ang)
        s_ref[...] = jnp.where(first, -sn, jnp.where(second, sn, 0.0))

    return _rows_call(body, name, [pos_col, inv], [(LANES, F32), (LANES, F32)])


def _rope(x, c, s):
    return x * c + pltpu.roll(x, 64, axis=1) * s


def _rope_t(d, c, s):
    return d * c + pltpu.roll(d * s, 64, axis=1)


def mla_qk_fwd(qp, kvp, kr, cos, sin, name):
    def body(q_ref, k_ref, kr_ref, c_ref, s_ref, qo_ref, ko_ref):
        c, s = c_ref[...], s_ref[...]
        kr_rot = _rope(kr_ref[...], c, s)
        for h in range(HEADS):
            sl = slice(LANES * h, LANES * (h + 1))
            qo_ref[:, sl] = _rope(q_ref[:, sl], c, s).astype(BF16)
            ko_ref[:, sl] = (k_ref[:, sl] + kr_rot).astype(BF16)

    n = qp.shape[0]
    tr = 256
    w = HEADS * LANES
    return pl.pallas_call(
        body, name=name, grid=(n // tr,),
        in_specs=[pl.BlockSpec((tr, w), lambda i: (i, 0)), pl.BlockSpec((tr, w), lambda i: (i, 0)),
                  pl.BlockSpec((tr, LANES), lambda i: (i, 0)), pl.BlockSpec((tr, LANES), lambda i: (i, 0)),
                  pl.BlockSpec((tr, LANES), lambda i: (i, 0))],
        out_specs=[pl.BlockSpec((tr, w), lambda i: (i, 0))] * 2,
        out_shape=[jax.ShapeDtypeStruct((n, w), BF16)] * 2, compiler_params=_params(("arbitrary",)),
    )(qp, kvp, kr, cos, sin)


def mla_qk_bwd(dq, dk, cos, sin, name):
    def body(dq_ref, dk_ref, c_ref, s_ref, dqp_ref, dkr_ref):
        c, s = c_ref[...], s_ref[...]
        tot = jnp.zeros(c.shape, F32)
        for h in range(HEADS):
            sl = slice(LANES * h, LANES * (h + 1))
            dqp_ref[:, sl] = _rope_t(dq_ref[:, sl], c, s).astype(BF16)
            tot = tot + dk_ref[:, sl]
        dkr_ref[...] = _rope_t(tot, c, s).astype(BF16)

    return _rows_call(body, name, [dq, dk, cos, sin], [(HEADS * LANES, BF16), (LANES, BF16)])


TQ = 256


def _pair_masks(shape):
    lane = lax.broadcasted_iota(jnp.int32, shape, 1)
    return (lane < 64, lane >= 64)


def _scaled_q(q_a, scale):
    return (q_a * jnp.asarray(scale, q_a.dtype), None) if scale == 0.125 else (q_a, scale)


def _causal_probs(q_a, k_a, scale, b0, cq, ck):
    s = lax.dot_general(q_a, k_a, _DIMS["nt"], preferred_element_type=F32)
    if scale is not None:
        s = s * scale
    if cq is not None:
        s = s + (cq - ck)
    row = lax.broadcasted_iota(jnp.int32, (TQ, TQ), 0)
    col = lax.broadcasted_iota(jnp.int32, (TQ, TQ), 1)
    diag = jnp.where(col <= row, s[:, b0:], NEG)
    s = diag if b0 == 0 else jnp.concatenate([s[:, :b0], diag], axis=1)
    e = jnp.exp(s - jnp.max(s, axis=-1, keepdims=True))
    return e * (1.0 / jnp.sum(e, axis=-1, keepdims=True))


def attn_fwd(q, k, v, name, *, wide, scale, q_off=0, k_off=0, v_off=0, cum=None, cum_t=None):
    qw = 2 * LANES if wide else LANES
    forget = cum is not None

    def body(*refs):
        q_ref, k_ref, v_ref = refs[:3]
        o_ref = refs[-1]
        m0, m1 = _pair_masks((TQ, LANES))
        for qi in range(S // TQ):
            b0, b1 = qi * TQ, (qi + 1) * TQ
            outs = []
            for a, msk in enumerate((m0, m1)):
                if wide:
                    q_a, k_a = q_ref[b0:b1, LANES * a:LANES * (a + 1)], k_ref[:b1, LANES * a:LANES * (a + 1)]
                else:
                    q_a, k_a = jnp.where(msk, q_ref[b0:b1, :], jnp.zeros((), BF16)), k_ref[:b1, :]
                cq = refs[3][0, b0:b1, a:a + 1] if forget else None
                ck = refs[4][0, a:a + 1, :b1] if forget else None
                q_a, left = _scaled_q(q_a, scale)
                p = _causal_probs(q_a, k_a, left, b0, cq, ck)
                outs.append(jnp.dot(p.astype(BF16), v_ref[:b1, :], preferred_element_type=F32))
            o_ref[b0:b1, :] = jnp.where(m0, outs[0], outs[1])

    in_specs = [pl.BlockSpec((S, qw), lambda h: (0, q_off * LANES // qw + h)),
                pl.BlockSpec((S, qw), lambda h: (0, k_off * LANES // qw + h)),
                pl.BlockSpec((S, LANES), lambda h: (0, v_off + h))]
    ins = [q, k, v]
    if forget:
        in_specs += [pl.BlockSpec((1, S, 2), lambda h: (h, 0, 0)), pl.BlockSpec((1, 2, S), lambda h: (h, 0, 0))]
        ins += [cum, cum_t]
    return pl.pallas_call(
        body, name=name, grid=(PAIRS,), in_specs=in_specs, out_specs=pl.BlockSpec((S, LANES), lambda h: (0, h)),
        out_shape=jax.ShapeDtypeStruct((S, PAIRS * LANES), F32), compiler_params=_params(("arbitrary",)),
    )(*ins)


def attn_bwd(q, k, v, o, do, name, *, wide, scale, out_dtype=F32, q_off=0, k_off=0, v_off=0, cum=None, cum_t=None):
    qw = 2 * LANES if wide else LANES
    forget = cum is not None

    def body(*refs):
        q_ref, k_ref, v_ref, o_ref, do_ref = refs[:5]
        n_out = 5 if forget else 3
        outs = refs[-(n_out + 2):-2]
        dq_ref, dk_ref, dv_ref = outs[:3]
        dk_acc, dv_acc = refs[-2], refs[-1]
        dk_acc[...] = jnp.zeros_like(dk_acc)
        dv_acc[...] = jnp.zeros_like(dv_acc)
        if forget:
            dcq_ref, dck_ref = outs[3], outs[4]
            dck_ref[...] = jnp.zeros_like(dck_ref)
        m0, m1 = _pair_masks((TQ, LANES))
        for qi in range(S // TQ):
            b0, b1 = qi * TQ, (qi + 1) * TQ
            do2 = do_ref[b0:b1, :]
            dd = do2 * o_ref[b0:b1, :]
            do_b = do2.astype(BF16)
            mk0, mk1 = _pair_masks((b1, LANES))
            dqs = []
            for a, (msk, mk) in enumerate(((m0, mk0), (m1, mk1))):
                lanes = slice(LANES * a, LANES * (a + 1)) if wide else slice(0, LANES)
                if wide:
                    q_a, k_a = q_ref[b0:b1, lanes], k_ref[:b1, lanes]
                else:
                    q_a, k_a = jnp.where(msk, q_ref[b0:b1, :], jnp.zeros((), BF16)), k_ref[:b1, :]
                cq = refs[5][0, b0:b1, a:a + 1] if forget else None
                ck = refs[6][0, a:a + 1, :b1] if forget else None
                q_a, left = _scaled_q(q_a, scale)
                p = _causal_probs(q_a, k_a, left, b0, cq, ck)
                dp = lax.dot_general(jnp.where(msk, do_b, jnp.zeros((), BF16)), v_ref[:b1, :], _DIMS["nt"],
                                     preferred_element_type=F32)
                delta = jnp.sum(jnp.where(msk, dd, 0.0), axis=-1, keepdims=True)
                ds = p * (dp - delta)
                if forget:
                    dcq_ref[0, b0:b1, a:a + 1] = jnp.sum(ds, axis=-1, keepdims=True)
                    dck_ref[0, a:a + 1, :b1] -= jnp.sum(ds, axis=0, keepdims=True)
                ds_b = ds.astype(BF16)
                dqs.append(jnp.dot(ds_b, k_a, preferred_element_type=F32) * scale)
                dk_a = lax.dot_general(ds_b, q_a, _DIMS["tn"], preferred_element_type=F32)
                dk_acc[:b1, lanes] += dk_a if left is None else dk_a * scale
                dv_acc[:b1, :] += jnp.where(mk, lax.dot_general(p.astype(BF16), do_b, _DIMS["tn"],
                                                                 preferred_element_type=F32), 0.0)
            if wide:
                dq_ref[b0:b1, :LANES] = dqs[0].astype(out_dtype)
                dq_ref[b0:b1, LANES:] = dqs[1].astype(out_dtype)
            else:
                dq_ref[b0:b1, :] = jnp.where(m0, dqs[0], dqs[1]).astype(out_dtype)
        dk_ref[...] = dk_acc[...].astype(out_dtype)
        dv_ref[...] = dv_acc[...].astype(out_dtype)

    pair = pl.BlockSpec((S, LANES), lambda h: (0, h))
    qk_out = pl.BlockSpec((S, qw), lambda h: (0, h))
    in_specs = [pl.BlockSpec((S, qw), lambda h: (0, q_off * LANES // qw + h)),
                pl.BlockSpec((S, qw), lambda h: (0, k_off * LANES // qw + h)),
                pl.BlockSpec((S, LANES), lambda h: (0, v_off + h)), pair, pair]
    ins = [q, k, v, o, do]
    out_specs = [qk_out, qk_out, pair]
    out_shape = [jax.ShapeDtypeStruct((S, PAIRS * qw), out_dtype)] * 2 + \
                [jax.ShapeDtypeStruct((S, PAIRS * LANES), out_dtype)]
    if forget:
        by_q, by_k = pl.BlockSpec((1, S, 2), lambda h: (h, 0, 0)), pl.BlockSpec((1, 2, S), lambda h: (h, 0, 0))
        in_specs += [by_q, by_k]
        ins += [cum, cum_t]
        out_specs += [by_q, by_k]
        out_shape += [jax.ShapeDtypeStruct((PAIRS, S, 2), F32), jax.ShapeDtypeStruct((PAIRS, 2, S), F32)]
    return pl.pallas_call(
        body, name=name, grid=(PAIRS,), in_specs=in_specs, out_specs=out_specs, out_shape=out_shape,
        scratch_shapes=[pltpu.VMEM((S, qw), F32), pltpu.VMEM((S, LANES), F32)],
        compiler_params=_params(("arbitrary",)),
    )(*ins)


def _tri(lower):
    r = lax.broadcasted_iota(jnp.int32, (QBLK, QBLK), 0)
    c = lax.broadcasted_iota(jnp.int32, (QBLK, QBLK), 1)
    return jnp.where((c <= r) if lower else (c >= r), 1.0, 0.0).astype(F32)


def _hi_dot(a, b):
    return jnp.dot(a, b, precision=lax.Precision.HIGHEST, preferred_element_type=F32)


def fox_gate_fwd(fl, bias, name):
    def body(f_ref, b_ref, o_ref):
        tri = _tri(True)
        carry = jnp.zeros((1, LANES), F32)
        for n in range(S // QBLK):
            x = f_ref[n * QBLK:(n + 1) * QBLK, :].astype(F32) + b_ref[...]
            lf = jnp.minimum(x, 0.0) - jnp.log(1.0 + jnp.exp(-jnp.abs(x)))
            c = _hi_dot(tri, lf) + carry
            o_ref[n * QBLK:(n + 1) * QBLK, :] = c
            carry = c[QBLK - 1:QBLK, :]

    return pl.pallas_call(body, name=name, out_shape=jax.ShapeDtypeStruct((S, LANES), F32),
                          compiler_params=_params())(fl, bias)


def fox_gate_bwd(fl, bias, dcq, dck, name):
    def body(f_ref, b_ref, dq_ref, dk_ref, o_ref, db_ref):
        tri = _tri(False)
        carry = jnp.zeros((1, LANES), F32)
        db = jnp.zeros((1, LANES), F32)
        for n in reversed(range(S // QBLK)):
            rows = slice(n * QBLK, (n + 1) * QBLK)
            dlf = _hi_dot(tri, dq_ref[rows, :] + dk_ref[rows, :]) + carry
            carry = dlf[0:1, :]
            x = f_ref[rows, :].astype(F32) + b_ref[...]
            dx = dlf * (1.0 - _sigmoid(x))
            o_ref[rows, :] = dx
            db = db + jnp.sum(dx, axis=0, keepdims=True)
        db_ref[...] = db

    return pl.pallas_call(body, name=name, out_shape=[jax.ShapeDtypeStruct((S, LANES), F32),
                                                      jax.ShapeDtypeStruct((1, LANES), F32)],
                          compiler_params=_params())(fl, bias, dcq, dck)


def _band_valid(first):
    w = QBLK if first else 2 * QBLK
    i = lax.broadcasted_iota(jnp.int32, (QBLK, w), 0)
    j = lax.broadcasted_iota(jnp.int32, (QBLK, w), 1)
    return (j <= i) if first else ((j >= i) & (j - QBLK <= i))


def _band_q(q_ref, rows, msk):
    return jnp.where(msk, q_ref[rows, :], jnp.zeros((), BF16)) * jnp.asarray(0.125, BF16)


def _band_logits(q_a, kk, bias, first):
    s = lax.dot_general(q_a, kk, _DIMS["nt"], preferred_element_type=F32) + bias
    return jnp.where(_band_valid(first), s, NEG)


def dil_fwd(qkv, bias, g, name):
    d = DIL[g][1]
    ls = S // d
    view = qkv.reshape(ls, d * 9216)

    def body(q_ref, k_ref, v_ref, b_ref, o_ref, l_ref):
        m0, m1 = _pair_masks((QBLK, LANES))
        for n in range(ls // QBLK):
            rows = slice(n * QBLK, (n + 1) * QBLK)
            keys = rows if n == 0 else slice((n - 1) * QBLK, (n + 1) * QBLK)
            os_, ls_ = [], []
            for a, msk in enumerate((m0, m1)):
                q_a = _band_q(q_ref, rows, msk)
                bias_a = b_ref[a, :, QBLK:] if n == 0 else b_ref[a]
                s = _band_logits(q_a, k_ref[keys, :], bias_a, n == 0)
                mx = jnp.max(s, axis=-1, keepdims=True)
                e = jnp.exp(s - mx)
                l = jnp.sum(e, axis=-1, keepdims=True)
                os_.append(jnp.dot((e * (1.0 / l)).astype(BF16), v_ref[keys, :], preferred_element_type=F32))
                ls_.append(mx + jnp.log(l))
            o_ref[rows, :] = jnp.where(m0, os_[0], os_[1])
            l_ref[rows, :] = jnp.where(m0, ls_[0], ls_[1])

    def col(j):
        return lambda h, r: (0, r * 72 + g * 24 + j * 8 + h)

    out = pl.BlockSpec((ls, LANES), lambda h, r: (0, r * 8 + h))
    o, lse = pl.pallas_call(
        body, name=name, grid=(PAIRS, d),
        in_specs=[pl.BlockSpec((ls, LANES), col(0)), pl.BlockSpec((ls, LANES), col(1)), pl.BlockSpec((ls, LANES), col(2)),
                  pl.BlockSpec((2, QBLK, 2 * QBLK), lambda h, r: (h, 0, 0))],
        out_specs=[out, out], out_shape=[jax.ShapeDtypeStruct((ls, d * D), F32)] * 2,
        compiler_params=_params(("arbitrary", "arbitrary")),
    )(view, view, view, bias)
    return o.reshape(S, D), lse.reshape(S, D)


def dil_merge(os_, lses, name):
    def body(o0, o1, o2, l0, l1, l2, o_ref, l_ref):
        ls_ = [l0[...], l1[...], l2[...]]
        mx = jnp.maximum(jnp.maximum(ls_[0], ls_[1]), ls_[2])
        tot = mx + jnp.log(sum(jnp.exp(l - mx) for l in ls_))
        o_ref[...] = sum(jnp.exp(l - tot) * o[...] for l, o in zip(ls_, (o0, o1, o2)))
        l_ref[...] = tot

    return _rows_call(body, name, list(os_) + list(lses), [(D, F32), (D, F32)])


def dil_bwd(qkv, bias, o, lse, do, g, name):
    d = DIL[g][1]
    ls = S // d
    view = qkv.reshape(ls, d * 9216)
    o, lse, do = (t.reshape(ls, d * D) for t in (o, lse, do))

    def body(q_ref, k_ref, v_ref, b_ref, o_ref, l_ref, do_ref, dq_ref, dk_ref, dv_ref, db_ref, dk_acc, dv_acc):
        @pl.when(pl.program_id(1) == 0)
        def _():
            db_ref[...] = jnp.zeros_like(db_ref)

        dk_acc[...] = jnp.zeros_like(dk_acc)
        dv_acc[...] = jnp.zeros_like(dv_acc)
        m0, m1 = _pair_masks((QBLK, LANES))
        for n in range(ls // QBLK):
            rows = slice(n * QBLK, (n + 1) * QBLK)
            keys = rows if n == 0 else slice((n - 1) * QBLK, (n + 1) * QBLK)
            nk = QBLK if n == 0 else 2 * QBLK
            do2, lse2 = do_ref[rows, :], l_ref[rows, :]
            dd = do2 * o_ref[rows, :]
            do_b = do2.astype(BF16)
            mk0, mk1 = _pair_masks((nk, LANES))
            dqs = []
            for a, (msk, mk) in enumerate(((m0, mk0), (m1, mk1))):
                q_a = _band_q(q_ref, rows, msk)
                kk = k_ref[keys, :]
                bias_a = b_ref[a, :, QBLK:] if n == 0 else b_ref[a]
                s = _band_logits(q_a, kk, bias_a, n == 0)
                lse_a = jnp.max(jnp.where(msk, lse2, -jnp.inf), axis=-1, keepdims=True)
                p = jnp.exp(s - lse_a)
                dp = lax.dot_general(jnp.where(msk, do_b, jnp.zeros((), BF16)), v_ref[keys, :], _DIMS["nt"],
                                     preferred_element_type=F32)
                delta = jnp.sum(jnp.where(msk, dd, 0.0), axis=-1, keepdims=True)
                ds = p * (dp - delta)
                if n == 0:
                    db_ref[a, :, QBLK:] += ds
                else:
                    db_ref[a] += ds
                ds_b = ds.astype(BF16)
                dqs.append(jnp.dot(ds_b, kk, preferred_element_type=F32) * 0.125)
                dk_acc[keys, :] += lax.dot_general(ds_b, q_a, _DIMS["tn"], preferred_element_type=F32)
                dv_acc[keys, :] += jnp.where(mk, lax.dot_general(p.astype(BF16), do_b, _DIMS["tn"],
                                                                 preferred_element_type=F32), 0.0)
            dq_ref[rows, :] = jnp.where(m0, dqs[0], dqs[1]).astype(BF16)
        dk_ref[...] = dk_acc[...].astype(BF16)
        dv_ref[...] = dv_acc[...].astype(BF16)

    def col(j):
        return lambda h, r: (0, r * 72 + g * 24 + j * 8 + h)

    nat = pl.BlockSpec((ls, LANES), lambda h, r: (0, r * 8 + h))
    b_spec = pl.BlockSpec((2, QBLK, 2 * QBLK), lambda h, r: (h, 0, 0))
    dq, dk, dv, db = pl.pallas_call(
        body, name=name, grid=(PAIRS, d),
        in_specs=[pl.BlockSpec((ls, LANES), col(0)), pl.BlockSpec((ls, LANES), col(1)), pl.BlockSpec((ls, LANES), col(2)),
                  b_spec, nat, nat, nat],
        out_specs=[nat, nat, nat, b_spec],
        out_shape=[jax.ShapeDtypeStruct((ls, d * D), BF16)] * 3 + [jax.ShapeDtypeStruct((HEADS, QBLK, 2 * QBLK), F32)],
        scratch_shapes=[pltpu.VMEM((ls, LANES), F32), pltpu.VMEM((ls, LANES), F32)],
        compiler_params=_params(("arbitrary", "arbitrary")),
    )(view, view, view, bias, o, lse, do)
    return dq.reshape(S, D), dk.reshape(S, D), dv.reshape(S, D), db


def _place():
    x, y, c = lax.axis_index("x"), lax.axis_index("y"), lax.axis_index("c")
    return x, y, c


def _dev_slot(ref, by_rows, dev):
    return ref.at[:, dev] if by_rows else ref.at[dev]


def all_gather(shards, by_rows, name, in_vmem=False):
    n = len(shards)

    def body(*refs):
        x_refs, out_refs = refs[:n], refs[n:2 * n]
        send_sems, recv_sems, local_sems = refs[2 * n:]
        x, y, c = _place()
        me, sibling = (x, y, c), (x, y, 1 - c)
        chips = [(1 - x, y), (x, 1 - y), (1 - x, 1 - y)]

        def slot(t, px, py, pc):
            return _dev_slot(out_refs[t], by_rows[t], 4 * px + 2 * py + pc)

        def copy(t, k, blk, to, src=None):
            return pltpu.make_async_remote_copy(
                src_ref=slot(t, *blk) if src is None else src, dst_ref=slot(t, *blk), send_sem=send_sems.at[7 * t + k],
                recv_sem=recv_sems.at[7 * t + k], device_id=to, device_id_type=MESH_ID)

        mine = [pltpu.make_async_copy(x_refs[t], slot(t, *me), local_sems.at[t]) for t in range(n)]
        for cp in mine:
            cp.start()
        first = []
        for t in range(n):
            first.append(copy(t, 0, me, sibling, src=x_refs[t]))
            first += [copy(t, 1 + j, me, (*chip, c), src=x_refs[t]) for j, chip in enumerate(chips)]
        for cp in first:
            cp.start()
        passed = []
        for j, chip in enumerate(chips):
            for t in range(n):
                copy(t, 1 + j, (*chip, c), me).wait_recv()
                passed.append(copy(t, 4 + j, (*chip, c), sibling))
                passed[-1].start()
        for t in range(n):
            copy(t, 0, sibling, me).wait_recv()
            for j, chip in enumerate(chips):
                copy(t, 4 + j, (*chip, 1 - c), me).wait_recv()
        for cp in first + passed:
            cp.wait_send()
        for cp in mine:
            cp.wait()

    def gathered(s, rows):
        shp = (s.shape[0], N_DEV) + s.shape[1:] if rows else (N_DEV,) + s.shape
        return jax.ShapeDtypeStruct(shp, s.dtype)

    space = pl.BlockSpec(memory_space=pltpu.VMEM if in_vmem else pl.ANY)
    return pl.pallas_call(
        body, name=name, out_shape=[gathered(s, r) for s, r in zip(shards, by_rows)],
        in_specs=[space] * n, out_specs=[space] * n,
        scratch_shapes=[pltpu.SemaphoreType.DMA((7 * n,)), pltpu.SemaphoreType.DMA((7 * n,)),
                        pltpu.SemaphoreType.DMA((n,))],
        compiler_params=pltpu.CompilerParams(vmem_limit_bytes=VMEM_LIMIT),
    )(*shards)


_HBM = pl.BlockSpec(memory_space=pltpu.HBM)
_SEM = pl.BlockSpec(memory_space=pltpu.SEMAPHORE)
_SPLIT = dict(has_side_effects=pltpu.SideEffectType.DATAFLOW_SIDE_EFFECTING)


def _hbm(a):
    return pltpu.with_memory_space_constraint(a, pltpu.HBM)


def _gathered_shape(s, rows):
    return (s.shape[0], N_DEV) + s.shape[1:] if rows else (N_DEV,) + s.shape


def _peers(x, y, c):
    return [(1 - x if k & 4 else x, 1 - y if k & 2 else y, 1 - c if k & 1 else c) for k in range(1, N_DEV)]


def gather_start(shards, lands, by_rows, name):
    n = len(shards)

    def body(*refs):
        x_refs, land_refs = refs[:n], refs[n:2 * n]
        send_sems, recv_sems = refs[2 * n], refs[2 * n + 1]
        x, y, c = _place()
        me = 4 * x + 2 * y + c
        for t in range(n):
            for k, peer in enumerate(_peers(x, y, c)):
                pltpu.make_async_remote_copy(
                    src_ref=x_refs[t], dst_ref=_dev_slot(land_refs[t], by_rows[t], me), send_sem=send_sems.at[7 * t + k],
                    recv_sem=recv_sems.at[7 * t + k], device_id=peer, device_id_type=MESH_ID).start()

    sems = pltpu.SemaphoreType.DMA((7 * n,))
    res = pl.pallas_call(
        body, name=name,
        out_shape=(sems, sems) + tuple(pltpu.HBM(a.shape, a.dtype) for a in list(shards) + list(lands)),
        in_specs=[_HBM] * (2 * n), out_specs=(_SEM, _SEM) + (_HBM,) * (2 * n),
        input_output_aliases={i: 2 + i for i in range(2 * n)},
        compiler_params=pltpu.CompilerParams(**_SPLIT),
    )(*[_hbm(a) for a in list(shards) + list(lands)])
    return res[0], res[1], list(res[2:2 + n]), list(res[2 + n:])


def gather_wait(send_sems, recv_sems, first, shards, lands, by_rows, after, name):
    n = len(shards)

    def body(*refs):
        x_refs, land_refs = refs[:n], refs[n:2 * n]
        send_sems, recv_sems = refs[2 * n], refs[2 * n + 1]
        x, y, c = _place()
        for t in range(n):
            for k, (px, py, pc) in enumerate(_peers(x, y, c)):
                cp = pltpu.make_async_remote_copy(
                    src_ref=x_refs[t], dst_ref=_dev_slot(land_refs[t], by_rows[t], 4 * px + 2 * py + pc),
                    send_sem=send_sems.at[7 * (first + t) + k], recv_sem=recv_sems.at[7 * (first + t) + k],
                    device_id=(px, py, pc), device_id_type=MESH_ID)
                cp.wait_send()
                cp.wait_recv()

    res = pl.pallas_call(
        body, name=name, out_shape=tuple(pltpu.HBM(a.shape, a.dtype) for a in list(shards) + list(lands)),
        in_specs=[_HBM] * (2 * n) + [_SEM, _SEM, pl.BlockSpec(memory_space=pl.ANY)], out_specs=(_HBM,) * (2 * n),
        input_output_aliases={i: i for i in range(2 * n)},
        compiler_params=pltpu.CompilerParams(**_SPLIT),
    )(*shards, *lands, send_sems, recv_sems, after)
    return list(res[n:])


def scatter_start(srcs, src_l, lands, land_l, by_rows, name):
    n = len(srcs)

    def body(*refs):
        x_refs, land_refs = refs[:n], refs[n:2 * n]
        send_sems, recv_sems, token = refs[2 * n], refs[2 * n + 1], refs[-1]
        x, y, c = _place()
        me = 4 * x + 2 * y + c
        for k, (px, py, pc) in enumerate(_peers(x, y, c)):
            for t in range(n):
                blk = _dev_slot(x_refs[t], by_rows[t], 4 * px + 2 * py + pc)
                pltpu.make_async_remote_copy(
                    src_ref=blk.at[src_l[t]], dst_ref=land_refs[t].at[me, land_l[t]], send_sem=send_sems.at[7 * t + k],
                    recv_sem=recv_sems.at[7 * t + k], device_id=(px, py, pc), device_id_type=MESH_ID).start()
        token[...] = jnp.zeros_like(token)

    lands = [lax.empty((N_DEV, 1) + s.shape[2:], s.dtype) if ld is None else ld for s, ld in zip(srcs, lands)]
    sems = pltpu.SemaphoreType.DMA((7 * n,))
    res = pl.pallas_call(
        body, name=name,
        out_shape=(sems, sems) + tuple(pltpu.HBM(a.shape, a.dtype) for a in list(srcs) + lands)
        + (jax.ShapeDtypeStruct((8, LANES), F32),),
        in_specs=[_HBM] * (2 * n),
        out_specs=(_SEM, _SEM) + (_HBM,) * (2 * n) + (pl.BlockSpec(memory_space=pltpu.VMEM),),
        input_output_aliases={i: 2 + i for i in range(2 * n)},
        compiler_params=pltpu.CompilerParams(**_SPLIT),
    )(*[_hbm(a) for a in list(srcs) + lands])
    return res[0], res[1], list(res[2:2 + n]), list(res[2 + n:2 + 2 * n]), res[-1]


def scatter_wait(send_sems, recv_sems, srcs, src_l, lands, land_l, by_rows, after, name):
    n = len(srcs)

    def body(*refs):
        x_refs, land_refs = refs[:n], refs[n:2 * n]
        send_sems, recv_sems = refs[2 * n], refs[2 * n + 1]
        x, y, c = _place()
        for k, (px, py, pc) in enumerate(_peers(x, y, c)):
            peer = 4 * px + 2 * py + pc
            for t in range(n):
                cp = pltpu.make_async_remote_copy(
                    src_ref=_dev_slot(x_refs[t], by_rows[t], peer).at[src_l[t]], dst_ref=land_refs[t].at[peer, land_l[t]],
                    send_sem=send_sems.at[7 * t + k], recv_sem=recv_sems.at[7 * t + k], device_id=(px, py, pc),
                    device_id_type=MESH_ID)
                cp.wait_send()
                cp.wait_recv()

    res = pl.pallas_call(
        body, name=name, out_shape=tuple(pltpu.HBM(a.shape, a.dtype) for a in list(srcs) + list(lands)),
        in_specs=[_HBM] * (2 * n) + [_SEM, _SEM, pl.BlockSpec(memory_space=pl.ANY)], out_specs=(_HBM,) * (2 * n),
        input_output_aliases={i: i for i in range(2 * n)},
        compiler_params=pltpu.CompilerParams(**_SPLIT),
    )(*srcs, *lands, send_sems, recv_sems, after)
    return list(res[:n]), list(res[n:])


ADAM_BLOCK_BYTES = 3 << 19


def adamw(w, m, v, parts, name):
    n_parts = parts.shape[0]
    n_l, r, c = w.shape
    lane_c = -(-c // LANES) * LANES
    fits = [t for t in range(16, r, 16) if r % t == 0 and t * lane_c * 4 <= ADAM_BLOCK_BYTES]
    tr = max(fits) if fits and r * lane_c * 4 > ADAM_BLOCK_BYTES else r
    c1 = 1.0 / (1.0 - ADAM_B1 ** ADAM_STEP)
    c2 = 1.0 / (1.0 - ADAM_B2 ** ADAM_STEP)

    def body(w_ref, m_ref, v_ref, p_ref, g_ref, d_ref, nm_ref, nv_ref):
        g = p_ref[0].astype(F32)
        for j in range(1, n_parts):
            g = g + p_ref[j].astype(F32)
        nm = ADAM_B1 * m_ref[...] + (1.0 - ADAM_B1) * g
        nv = ADAM_B2 * v_ref[...] + (1.0 - ADAM_B2) * (g * g)
        g_ref[...] = g
        nm_ref[...] = nm
        nv_ref[...] = nv
        d_ref[...] = -ADAM_LR * ((nm * c1) / (jnp.sqrt(nv * c2) + ADAM_EPS) + ADAM_WD * w_ref[...])

    blk = pl.BlockSpec((1, tr, c), lambda l, i: (l, i, 0))
    return pl.pallas_call(
        body, name=name, grid=(n_l, r // tr),
        in_specs=[blk, blk, blk, pl.BlockSpec((n_parts, 1, tr, c), lambda l, i: (0, l, i, 0))],
        out_specs=[blk] * 4, out_shape=[jax.ShapeDtypeStruct((n_l, r, c), F32)] * 4,
        compiler_params=_params(("parallel", "parallel")),
    )(w, m, v, parts)


def sum_parts(parts, name):
    def body(p_ref, o_ref):
        g = p_ref[0]
        for j in range(1, parts.shape[0]):
            g = g + p_ref[j]
        o_ref[...] = g

    return pl.pallas_call(body, name=name, out_shape=jax.ShapeDtypeStruct(parts.shape[1:], F32),
                          compiler_params=_params())(parts)


def _pack(arrs, rows, dtype):
    flat = jnp.concatenate([a.reshape(-1).astype(dtype) for a in arrs])
    return jnp.pad(flat, (0, rows * LANES - flat.shape[0])).reshape(rows, LANES)


def _unpack(packed, shapes):
    flat, out, off = packed.reshape(-1), [], 0
    for shp in shapes:
        n = int(np.prod(shp))
        out.append(flat[off:off + n].reshape(shp))
        off += n
    return out


def _cat(parts):
    return jnp.concatenate(parts, axis=1)


def _layer_list(i):
    kind, j = i % 3, i // 3
    mix = ([("mla_w_a", j), ("mla_w_uq", j), ("mla_w_ukv", j), ("mla_w_o", j)] if kind == 0 else
           [("dil_w_qkv", 0), ("dil_w_o", 0)] if kind == 1 else [("fox_w_qkvf", 0), ("fox_w_o", 0)])
    return mix + [("ffn_w_in", i), ("ffn_w_out", i), ("ple_w_proj", i), ("ple_w_gate", i)]


def _mla_layout(w_a, g_uq, g_ukv, j):
    def z(r, n):
        return jnp.zeros((r, n), BF16)

    wa = w_a[j]
    a = _cat([wa[:, :640], wa[:, 640:656], z(D, 48), wa[:, 656:672], z(D, 48)])
    q, k, v = [], [], []
    for h in range(HEADS):
        b = g_uq[h // 2, j][:, 96 * (h % 2):96 * (h % 2 + 1)]
        q += [b[:, 64:80], b[:, 0:32], z(Q_RANK, 16), b[:, 80:96], b[:, 32:64], z(Q_RANK, 16)]
        b = g_ukv[h // 2, j][:, LANES * (h % 2):LANES * (h % 2 + 1)]
        k += [z(KV_RANK, 16), b[:, 0:32], z(KV_RANK, 32), b[:, 32:64], z(KV_RANK, 16)]
        v.append(b[:, 64:128])
    return a, _cat(q), _cat(k + v)


def _mla_unlayout(d_a, d_uq, d_ukv):
    a = _cat([d_a[:, :640], d_a[:, 640:656], d_a[:, 704:720]])
    uq, ukv = [], []
    for dev in range(N_DEV):
        q, kv = [], []
        for h in (2 * dev, 2 * dev + 1):
            b = d_uq[:, LANES * h:LANES * (h + 1)]
            q += [b[:, 16:48], b[:, 80:112], b[:, 0:16], b[:, 64:80]]
            b = d_ukv[:, LANES * h:LANES * (h + 1)]
            kv += [b[:, 16:48], b[:, 80:112], d_ukv[:, HEADS * LANES + 64 * h:HEADS * LANES + 64 * (h + 1)]]
        uq.append(_cat(q))
        ukv.append(_cat(kv))
    return a, jnp.stack(uq), jnp.stack(ukv)


def _mixer_fwd(kind, tag, hn, W, aux):
    if kind == 0:
        a = mm(hn, W["w_a"], "nn", f"{tag}_a", tn=768)
        cq = rms_fwd(a[:, :Q_RANK], W["q_norm"], f"{tag}_cq", out_dtype=BF16)
        ckv = rms_fwd(a[:, Q_RANK:Q_RANK + KV_RANK], W["kv_norm"], f"{tag}_ckv", out_dtype=BF16)
        qp = mm(cq, W["w_uq"], "nn", f"{tag}_uq", tk=384)
        kvp = mm(ckv, W["w_ukv"], "nn", f"{tag}_ukv", tk=256)
        q, k = mla_qk_fwd(qp, kvp, a[:, 640:], aux["cos"], aux["sin"], f"{tag}_qk")
        v = kvp.astype(BF16)
        o = attn_fwd(q, k, v, f"{tag}_attn", wide=True, scale=96 ** -0.5, v_off=HEADS)
        y = mm(o, W["w_o"], "nn", f"{tag}_o")
        return y, (a, cq, ckv, q, k, v, o)
    if kind == 1:
        qkv = mm(hn, W["w_qkv"], "nn", f"{tag}_qkv", out_dtype=BF16, tn=1152)
        parts = [dil_fwd(qkv, aux["dil_bias"][g], g, f"{tag}_g{g}") for g in range(3)]
        o, lse = dil_merge([p_[0] for p_ in parts], [p_[1] for p_ in parts], f"{tag}_merge")
        y = mm(o, W["w_o"], "nn", f"{tag}_o")
        return y, (qkv, o, lse)
    a = mm(hn, W["w_qkvf"], "nn", f"{tag}_qkvf", tn=640)
    fl = a[:, 3072:]
    cum = fox_gate_fwd(fl, aux["fox_b"], f"{tag}_gate")[:, :HEADS]
    cum_q = cum.reshape(S, PAIRS, 2).transpose(1, 0, 2)
    cum_k = cum.T.reshape(PAIRS, 2, S)
    ab = a.astype(BF16)
    o = attn_fwd(ab, ab, ab, f"{tag}_attn", wide=False, scale=0.125, k_off=PAIRS, v_off=2 * PAIRS, cum=cum_q, cum_t=cum_k)
    y = mm(o, W["w_o"], "nn", f"{tag}_o")
    return y, (fl, ab, cum_q, cum_k, o)


def _mixer_bwd(kind, tag, hn, dy, W, aux, saved):
    gr = {}
    if kind == 0:
        a, cq, ckv, q, k, v, o = saved
        gr["w_o"] = mm(o, dy, "tn", f"{tag}_dwo", out_dtype=BF16)
        do = mm(dy, W["w_o"], "nt", f"{tag}_do")
        dq, dk, dv = attn_bwd(q, k, v, o, do, f"{tag}_attn_b", wide=True, scale=96 ** -0.5, v_off=HEADS)
        dqp, dkr = mla_qk_bwd(dq, dk, aux["cos"], aux["sin"], f"{tag}_qk_b")
        dkvp = jnp.concatenate([dk, dv], axis=1)
        gr["w_ukv"] = mm(ckv, dkvp, "tn", f"{tag}_dwukv", out_dtype=BF16, tm=256)
        dckv = mm(dkvp, W["w_ukv"], "nt", f"{tag}_dckv", tn=256)
        gr["w_uq"] = mm(cq, dqp, "tn", f"{tag}_dwuq", out_dtype=BF16, tm=384)
        dcq = mm(dqp, W["w_uq"], "nt", f"{tag}_dcq", tn=384)
        da_q, gr["q_norm"] = rms_bwd(a[:, :Q_RANK], W["q_norm"], dcq, f"{tag}_cq_b", out_dtype=BF16)
        da_kv, gr["kv_norm"] = rms_bwd(a[:, Q_RANK:Q_RANK + KV_RANK], W["kv_norm"], dckv, f"{tag}_ckv_b",
                                       out_dtype=BF16)
        da = jnp.concatenate([da_q, da_kv, dkr], axis=1)
        gr["w_a"] = mm(hn, da, "tn", f"{tag}_dwa", out_dtype=BF16, tn=768)
        return mm(da, W["w_a"], "nt", f"{tag}_dhn", tk=768), gr
    if kind == 1:
        qkv, o, lse = saved
        gr["w_o"] = mm(o, dy, "tn", f"{tag}_dwo", out_dtype=BF16)
        do = mm(dy, W["w_o"], "nt", f"{tag}_do")
        cols, dbs = [], []
        for g in range(3):
            dq, dk, dv, db = dil_bwd(qkv, aux["dil_bias"][g], o, lse, do, g, f"{tag}_g{g}_b")
            cols += [dq, dk, dv]
            dbs.append(db)
        dqkv = jnp.concatenate(cols, axis=1)
        gr["dil_dbias"] = dbs
        gr["w_qkv"] = mm(hn, dqkv, "tn", f"{tag}_dwqkv", out_dtype=BF16, out_dev=1152, tn=1152)
        return mm(dqkv, W["w_qkv"], "nt", f"{tag}_dhn", tk=1152), gr
    fl, ab, cum_q, cum_k, o = saved
    gr["w_o"] = mm(o, dy, "tn", f"{tag}_dwo", out_dtype=BF16)
    do = mm(dy, W["w_o"], "nt", f"{tag}_do")
    dq, dk, dv, dcq, dck = attn_bwd(ab, ab, ab, o, do, f"{tag}_attn_b", wide=False, scale=0.125, out_dtype=BF16,
                                    k_off=PAIRS, v_off=2 * PAIRS, cum=cum_q, cum_t=cum_k)
    pad = ((0, 0), (0, LANES - HEADS))
    dcq = jnp.pad(dcq.transpose(1, 0, 2).reshape(S, HEADS), pad)
    dck = jnp.pad(dck.reshape(HEADS, S).T, pad)
    dfl, gr["b_f"] = fox_gate_bwd(fl, aux["fox_b"], dcq, dck, f"{tag}_gate_b")
    da = jnp.concatenate([dq, dk, dv, dfl.astype(BF16)], axis=1)
    gr["w_qkvf"] = mm(hn, da, "tn", f"{tag}_dwqkvf", out_dtype=BF16, tn=640)
    return mm(da, W["w_qkvf"], "nt", f"{tag}_dhn", tk=640), gr


def kernel(x, p, positions, norm_g, ffn_w_in, ffn_w_out, ple_w_proj, ple_w_gate, rel_bias, mla_w_a, mla_q_norm, mla_kv_norm, mla_w_uq, mla_w_ukv, mla_w_o, dil_w_qkv, dil_w_o, fox_w_qkvf, fox_b_f, fox_w_o, loss_target, m_norm_g, m_ffn_w_in, m_ffn_w_out, m_ple_w_proj, m_ple_w_gate, m_rel_bias, m_mla_w_a, m_mla_q_norm, m_mla_kv_norm, m_mla_w_uq, m_mla_w_ukv, m_mla_w_o, m_dil_w_qkv, m_dil_w_o, m_fox_w_qkvf, m_fox_b_f, m_fox_w_o, v_norm_g, v_ffn_w_in, v_ffn_w_out, v_ple_w_proj, v_ple_w_gate, v_rel_bias, v_mla_w_a, v_mla_q_norm, v_mla_kv_norm, v_mla_w_uq, v_mla_w_ukv, v_mla_w_o, v_dil_w_qkv, v_dil_w_o, v_fox_w_qkvf, v_fox_b_f, v_fox_w_o):
    given = dict(locals())
    me = 4 * lax.axis_index("x") + 2 * lax.axis_index("y") + lax.axis_index("c")

    rows_of = {n: axis == 1 for n, _, axis in BIG}
    shape_of = {n: shp for n, shp, _ in BIG}
    lists = [_layer_list(i) for i in range(DEPTH)]
    flat = [nl for ls in lists for nl in ls]
    flat_rows = [rows_of[n] for n, _ in flat]
    shards = [given[n][l:l + 1].astype(BF16) for n, l in flat]
    lands = [lax.dynamic_update_slice(lax.empty(_gathered_shape(s, r), BF16), s[:, None] if r else s[None],
                                      (0, me, 0, 0) if r else (me, 0, 0, 0)) for s, r in zip(shards, flat_rows)]
    send_s, recv_s, shards, lands = gather_start(shards, lands, flat_rows, "gather_start")
    full = {}

    gain_rows = _rows(sum(int(np.prod(s)) for _, s, _ in SMALL_SHARDED))
    gains, = all_gather([_pack([given[n] for n, _, _ in SMALL_SHARDED], gain_rows, F32)], [False], "gather_gains",
                        in_vmem=True)
    gains = gains.reshape(N_DEV, gain_rows * LANES)
    off = 0
    for n, shp, axis in SMALL_SHARDED:
        cnt = int(np.prod(shp))
        g = jnp.moveaxis(gains[:, off:off + cnt].reshape((N_DEV,) + shp), 0, axis)
        full[n] = g.reshape(shp[:axis] + (N_DEV * shp[axis],))
        off += cnt

    cos, sin = rope_tables(positions.reshape(S, 1), "rope_tables")
    dil_bias = [mm(rel_bias[:, HEADS * g:HEADS * (g + 1)], jnp.asarray(_bucket_onehot(DIL[g][1])), "tn",
                   f"dil_bias{g}", precise=True, tn=4096).reshape(HEADS, QBLK, 2 * QBLK) for g in range(3)]
    aux = {"cos": cos, "sin": sin, "dil_bias": dil_bias,
           "fox_b": jnp.pad(fox_b_f, ((0, 0), (0, LANES - HEADS)))}

    def arrived(i, part, behind):
        n_mix = len(lists[i]) - 4
        first = sum(len(ls) for ls in lists[:i]) + (n_mix if part else 0)
        sl = slice(first, first + (4 if part else n_mix))
        got = gather_wait(send_s, recv_s, first, shards[sl], lands[sl], flat_rows[sl], behind, f"gather_wait{i}_{part}")
        return {n: g.reshape(1, N_DEV * shape_of[n][1], shape_of[n][2]) if rows_of[n] else g
                for (n, _), g in zip(flat[sl], got)}

    def mixer_weights(i, behind):
        kind, j = i % 3, i // 3
        w = arrived(i, 0, behind)
        W = {"g": [full["norm_g"][i, r][None, :] for r in range(4)]}
        if kind == 0:
            w_a, w_uq, w_ukv = _mla_layout(w["mla_w_a"], w["mla_w_uq"], w["mla_w_ukv"], 0)
            W.update(w_a=w_a, w_uq=w_uq, w_ukv=w_ukv, w_o=Lay(w["mla_w_o"], 0),
                     q_norm=full["mla_q_norm"][j][None, :], kv_norm=full["mla_kv_norm"][j][None, :])
        elif kind == 1:
            W.update(w_qkv=Dev(w["dil_w_qkv"], 0), w_o=Lay(w["dil_w_o"], 0))
        else:
            fox_w = jnp.pad(_cat([w["fox_w_qkvf"][dev, 0] for dev in range(N_DEV)]), ((0, 0), (0, FOX_W - 3088)))
            W.update(w_qkvf=fox_w, w_o=Lay(w["fox_w_o"], 0))
        return W

    def ffn_weights(i, behind):
        w = arrived(i, 1, behind)
        return {"w_in": Dev(w["ffn_w_in"], 0), "w_out": Lay(w["ffn_w_out"], 0), "w_proj": Dev(w["ple_w_proj"], 0),
                "w_gate": Lay(w["ple_w_gate"], 0)}

    h = x[0]
    saved, weights = [], []
    for i in range(DEPTH):
        kind, j, W = i % 3, i // 3, mixer_weights(i, h)
        weights.append(W)
        t = f"l{i}"
        hn = rms_fwd(h, W["g"][0], f"{t}_n0", out_dtype=BF16)
        y, mix = _mixer_fwd(kind, f"{t}_mix", hn, W, aux)
        W.update(ffn_weights(i, y))
        h1 = rms_fwd(y, W["g"][1], f"{t}_n1", res=h)
        fin = rms_fwd(h1, W["g"][2], f"{t}_n2", out_dtype=BF16)
        gu = mm(fin, W["w_in"], "nn", f"{t}_ffn_in", out_dev=FF_W, tn=FF_W)
        act = swiglu_fwd(gu, f"{t}_swiglu")
        f = mm(Dev(act, 0), W["w_out"], "nn", f"{t}_ffn_out", tk=FF_W)
        h2 = rms_fwd(f, W["g"][3], f"{t}_n3", res=h1)
        pp = mm(p[i, 0], W["w_proj"], "nn", f"{t}_ple_p", tn=LANES, tk=256)
        gt = mm(h2, W["w_gate"], "nn", f"{t}_ple_g")
        h3 = ple_fwd(h2, pp, gt, f"{t}_ple")
        saved.append((h, hn, y, h1, fin, gu, act, f, h2, pp, gt, mix))
        h = h3

    dh, loss_lanes = loss_head(h, loss_target[0], "loss_head")

    grads = {n: None for n, _, _ in BIG}
    landed = {n: (lax.empty((N_DEV,) + shp, BF16) if shp[0] > 1 else None) for n, shp, _ in BIG}
    in_flight = []
    g_norm = [[None] * 4 for _ in range(DEPTH)]
    g_qn, g_kvn = [None, None], [None, None]
    g_rel, g_bf = None, None

    def stacked(n):
        g = grads[n]
        return None if g is None else g.reshape(g.shape[0], N_DEV * g.shape[2], g.shape[3])

    def by_device(g):
        return g.reshape(g.shape[0], N_DEV, g.shape[1] // N_DEV, g.shape[2])

    def start(i, entries, mine, tag):
        names = [n for n, _ in entries]
        srcs = [mine[n] if n in mine else grads[n] for n in names]
        src_l = [0 if n in mine else i for n in names]
        land_l = [l if shape_of[n][0] > 1 else 0 for n, l in entries]
        rows = [rows_of[n] for n in names]
        s_sem, r_sem, srcs, got, token = scatter_start(srcs, src_l, [landed[n] for n in names], land_l, rows, tag)
        for n, src, ld in zip(names, srcs, got):
            landed[n] = ld
            if n in mine:
                mine[n] = src
            else:
                grads[n] = src
        in_flight.append((s_sem, r_sem, names, mine, src_l, land_l, rows))
        return token

    token = None
    for i in reversed(range(DEPTH)):
        kind, j, W = i % 3, i // 3, weights[i]
        t = f"l{i}b"
        h0, hn, y, h1, fin, gu, act, f, h2, pp, gt, mix = saved[i]
        dpp, dgt = ple_bwd(dh, pp, gt, f"{t}_ple")
        grads["ple_w_proj"] = mm(p[i, 0], dpp, "tn", f"{t}_dwp", out_dtype=BF16, out_dev=LANES, tm=256, tn=LANES,
                                 stack=(grads["ple_w_proj"], DEPTH, i), after=token)
        grads["ple_w_gate"] = by_device(mm(h2, dgt, "tn", f"{t}_dwg", out_dtype=BF16,
                                           stack=(stacked("ple_w_gate"), DEPTH, i)))
        dh2 = mm(dgt, W["w_gate"], "nt", f"{t}_dh2", add=dh)
        df, g_norm[i][3] = rms_bwd(f, W["g"][3], dh2, f"{t}_n3", out_dtype=BF16)
        grads["ffn_w_out"] = by_device(mm(Dev(act, 0), df, "tn", f"{t}_dwout", out_dtype=BF16, tm=FF_W,
                                          stack=(stacked("ffn_w_out"), DEPTH, i)))
        dact = mm(df, W["w_out"], "nt", f"{t}_dact", out_dev=FF_W, tn=FF_W)
        dgu = swiglu_bwd(gu, dact, f"{t}_swiglu")
        grads["ffn_w_in"] = mm(fin, Dev(dgu, 0), "tn", f"{t}_dwin", out_dtype=BF16, out_dev=FF_W, tn=FF_W,
                               stack=(grads["ffn_w_in"], DEPTH, i))
        token = start(i, lists[i][-4:], {}, f"scatter_ffn{i}")
        dfin = mm(Dev(dgu, 0), W["w_in"], "nt", f"{t}_dfin", after=token, tk=FF_W)
        dh1, g_norm[i][2] = rms_bwd(h1, W["g"][2], dfin, f"{t}_n2", res=dh2)
        dy, g_norm[i][1] = rms_bwd(y, W["g"][1], dh1, f"{t}_n1", out_dtype=BF16)
        dhn, gr = _mixer_bwd(kind, f"{t}_mix", hn, dy, W, aux, mix)
        if kind == 0:
            d_a, d_uq, d_ukv = _mla_unlayout(gr["w_a"], gr["w_uq"], gr["w_ukv"])
            mine = {"mla_w_a": by_device(d_a[None]), "mla_w_uq": d_uq[:, None], "mla_w_ukv": d_ukv[:, None],
                    "mla_w_o": by_device(gr["w_o"][None])}
            g_qn[j], g_kvn[j] = gr["q_norm"], gr["kv_norm"]
        elif kind == 1:
            mine = {"dil_w_qkv": gr["w_qkv"], "dil_w_o": by_device(gr["w_o"][None])}
            g_rel = jnp.concatenate(
                [mm(jnp.asarray(_bucket_onehot(DIL[g][1])), gr["dil_dbias"][g].reshape(HEADS, -1), "nt",
                    f"{t}_drel{g}", precise=True, tk=4096) for g in range(3)], axis=1)
        else:
            wide = gr["w_qkvf"]
            mine = {"fox_w_qkvf": jnp.stack([wide[:, 386 * dev:386 * (dev + 1)] for dev in range(N_DEV)])[:, None],
                    "fox_w_o": by_device(gr["w_o"][None])}
            g_bf = gr["b_f"][:, :HEADS]
        token = start(i, lists[i][:-4], mine, f"scatter_mix{i}")
        dh, g_norm[i][0] = rms_bwd(h0, W["g"][0], dhn, f"{t}_n0", res=dh1, after=token)
    grad_x = dh[None]

    own = {n: [] for n, _, _ in BIG}
    for idx, (s_sem, r_sem, names, mine, src_l, land_l, rows) in enumerate(in_flight):
        srcs = [mine[n] if n in mine else grads[n] for n in names]
        srcs, got = scatter_wait(s_sem, r_sem, srcs, src_l, [landed[n] for n in names], land_l, rows, dh,
                                 f"scatter_wait{idx}")
        for n, src, ld, sl, ll, rw in zip(names, srcs, got, src_l, land_l, rows):
            landed[n] = ld
            if n not in mine:
                grads[n] = src
            blk = lax.dynamic_index_in_dim(src, me, axis=1 if rw else 0, keepdims=False)[sl]
            own[n].append((ll, blk))
    big_out = []
    for n, _, _ in BIG:
        part = landed[n]
        for ll, blk in own[n]:
            part = lax.dynamic_update_slice(part, blk[None, None], (me, ll, 0, 0))
        big_out.append(adamw(given[n], given["m_" + n], given["v_" + n], part, f"adamw_{n}"))

    small_full = [jnp.stack([jnp.concatenate(r, axis=0) for r in g_norm]).reshape(-1),
                  jnp.concatenate(g_qn, axis=0).reshape(-1), jnp.concatenate(g_kvn, axis=0).reshape(-1),
                  g_rel.reshape(-1), g_bf.reshape(-1), loss_lanes.reshape(-1)]
    small_n = sum(a.shape[0] for a in small_full)
    small_rows = _rows(small_n)
    parts, = all_gather([_pack(small_full, small_rows, F32)], [False], "gather_small_grads", in_vmem=True)
    tot = _unpack(sum_parts(parts, "sum_small_grads"), [(4, 4, D), (2, Q_RANK), (2, KV_RANK), (32, 48), (1, 16), (LANES,)])
    loss = jnp.sum(tot[5])
    small_g = [lax.dynamic_slice_in_dim(tot[0], me * 128, 128, axis=2), lax.dynamic_slice_in_dim(tot[1], me * 48, 48, axis=1),
               lax.dynamic_slice_in_dim(tot[2], me * 32, 32, axis=1), tot[3], tot[4]]
    small_names = [n for n, _, _ in SMALL_SHARDED] + [n for n, _ in SMALL_REPL]
    small_shapes = [s for _, s, _ in SMALL_SHARDED] + [s for _, s in SMALL_REPL]
    s_rows = _rows(sum(int(np.prod(s)) for s in small_shapes))
    small_out = adamw(_pack([given[n] for n in small_names], s_rows, F32)[None],
                      _pack([given["m_" + n] for n in small_names], s_rows, F32)[None],
                      _pack([given["v_" + n] for n in small_names], s_rows, F32)[None],
                      _pack(small_g, s_rows, F32)[None, None], "adamw_small")
    small_out = [_unpack(o_, small_shapes) for o_ in small_out]

    res = [{}, {}, {}, {}]
    for k in range(4):
        for idx, (n, _, _) in enumerate(BIG):
            res[k][n] = big_out[idx][k]
        for idx, n in enumerate(small_names):
            res[k][n] = small_out[k][idx]
    return (loss, grad_x, *[res[0][n] for n in WEIGHTS], *[res[1][n] for n in WEIGHTS],
            *[res[2][n] for n in WEIGHTS], *[res[3][n] for n in WEIGHTS])
```

```python
import math
from typing import NamedTuple

import numpy as np
import jax
import jax.numpy as jnp
from jax import lax
from jax.experimental import pallas as pl
from jax.experimental.pallas import tpu as pltpu

F32 = jnp.float32
BF16 = jnp.bfloat16
MESH_ID = pl.DeviceIdType.MESH

N_DEV = 8
S = 2048
D = 1024
DEPTH = 4
D_FF = 2816
D_PLE = 256
EPS = 1e-6
NEG = -1e30
LANES = 128
HEADS = 16
PAIRS = 8
Q_RANK = 384
KV_RANK = 256
QBLK = 128
DIL = ((128, 1), (512, 4), (2048, 16))
REL_BUCKETS = 32
FOX_W = 3200
VMEM_LIMIT = 56 * 1024 * 1024

ADAM_LR, ADAM_B1, ADAM_B2, ADAM_EPS, ADAM_WD, ADAM_STEP = 1e-3, 0.9, 0.999, 1e-8, 0.01, 10


BIG = (
    ("ffn_w_in", (4, 1024, 704), 2), ("ffn_w_out", (4, 352, 1024), 1),
    ("ple_w_proj", (4, 256, 128), 2), ("ple_w_gate", (4, 128, 1024), 1),
    ("mla_w_a", (2, 128, 672), 1), ("mla_w_uq", (2, 384, 192), 2),
    ("mla_w_ukv", (2, 256, 256), 2), ("mla_w_o", (2, 128, 1024), 1),
    ("dil_w_qkv", (1, 1024, 1152), 2), ("dil_w_o", (1, 128, 1024), 1),
    ("fox_w_qkvf", (1, 1024, 386), 2), ("fox_w_o", (1, 128, 1024), 1),
)
SMALL_SHARDED = (("norm_g", (4, 4, 128), 2), ("mla_q_norm", (2, 48), 1), ("mla_kv_norm", (2, 32), 1))
SMALL_REPL = (("rel_bias", (32, 48)), ("fox_b_f", (1, 16)))
WEIGHTS = ("norm_g", "ffn_w_in", "ffn_w_out", "ple_w_proj", "ple_w_gate", "rel_bias", "mla_w_a", "mla_q_norm",
           "mla_kv_norm", "mla_w_uq", "mla_w_ukv", "mla_w_o", "dil_w_qkv", "dil_w_o", "fox_w_qkvf", "fox_b_f",
           "fox_w_o")


def _rows(n):
    return -(-n // (8 * LANES)) * 8


def _t5_bucket_np(dist):
    max_exact = REL_BUCKETS // 2
    n = np.maximum(dist.astype(np.float32), np.float32(1.0))
    large = max_exact + (np.log(n / np.float32(max_exact)) / np.float32(math.log(2048 / max_exact))
                         * np.float32(REL_BUCKETS - max_exact)).astype(np.int32)
    large = np.minimum(large, REL_BUCKETS - 1)
    return np.where(dist < max_exact, dist, large)


def _bucket_onehot(dilation):
    i = np.arange(QBLK)[:, None]
    j = np.arange(2 * QBLK)[None, :]
    bucket = _t5_bucket_np(np.clip(QBLK + i - j, 0, None) * dilation).reshape(-1)
    return (np.arange(REL_BUCKETS)[:, None] == bucket[None, :]).astype(np.float32)


def _rope_inv_lanes():
    half = 16
    inv = (np.float32(10000.0) ** (-np.arange(half, dtype=np.float32) / np.float32(half))).astype(np.float32)
    t = np.zeros((1, LANES), np.float32)
    t[0, 0:16] = inv
    t[0, 64:80] = inv
    return t


def _params(sem=None):
    return pltpu.CompilerParams(dimension_semantics=sem, vmem_limit_bytes=VMEM_LIMIT)


def _tile(dim, target):
    if dim <= target or dim % target == 0:
        return min(dim, target)
    t = (target // LANES) * LANES
    while dim % t:
        t -= LANES
    return t


_DIMS = {"nn": (((1,), (0,)), ((), ())), "nt": (((1,), (1,)), ((), ())), "tn": (((0,), (0,)), ((), ()))}


class Lay(NamedTuple):
    arr: jax.Array
    l: int


class Dev(NamedTuple):
    arr: jax.Array
    l: int


def _lshape(op):
    if isinstance(op, Dev):
        g, _, r, w = op.arr.shape
        return r, g * w
    return op.arr.shape[1:] if isinstance(op, Lay) else op.shape


def _op_spec(op, rows_t, cols_t, row_ix, col_ix):
    if isinstance(op, Dev):
        w = op.arr.shape[3]
        assert w % cols_t == 0 and (cols_t % LANES == 0 or cols_t == w), (w, cols_t)
        nb, l = w // cols_t, op.l
        return pl.BlockSpec((1, 1, rows_t, cols_t),
                            lambda i, j, k: (col_ix(i, j, k) // nb, l, row_ix(i, j, k), col_ix(i, j, k) % nb))
    if isinstance(op, Lay):
        l = op.l
        return pl.BlockSpec((1, rows_t, cols_t), lambda i, j, k: (l, row_ix(i, j, k), col_ix(i, j, k)))
    return pl.BlockSpec((rows_t, cols_t), lambda i, j, k: (row_ix(i, j, k), col_ix(i, j, k)))


def _mat(ref):
    return ref[(0,) * (len(ref.shape) - 2)]


def mm(a, b, mode, name, out_dtype=F32, precise=False, add=None, out_dev=None, stack=None, after=None,
       tm=1024, tn=512, tk=2048):
    (ar, ac), (br, bc) = _lshape(a), _lshape(b)
    M, K = (ac, ar) if mode == "tn" else (ar, ac)
    N = br if mode == "nt" else bc
    assert K == (bc if mode == "nt" else br)
    tm, tn, tk = _tile(M, tm), _tile(N, tn), _tile(K, tk)
    nk = K // tk
    ix_i, ix_j, ix_k = (lambda i, j, k: i), (lambda i, j, k: j), (lambda i, j, k: k)
    a_spec = _op_spec(a, tk, tm, ix_k, ix_i) if mode == "tn" else _op_spec(a, tm, tk, ix_i, ix_k)
    b_spec = _op_spec(b, tn, tk, ix_j, ix_k) if mode == "nt" else _op_spec(b, tk, tn, ix_k, ix_j)
    buf, n_l, l = stack if stack is not None else (None, 1, 0)
    if out_dev is not None:
        out = Dev(jax.ShapeDtypeStruct((N // out_dev, n_l, M, out_dev), out_dtype), l)
    elif stack is not None:
        out = Lay(jax.ShapeDtypeStruct((n_l, M, N), out_dtype), l)
    else:
        out = jax.ShapeDtypeStruct((M, N), out_dtype)
    o_spec = _op_spec(out, tm, tn, ix_i, ix_j)
    n_in = 3 if add is not None else 2

    def body(*refs):
        a_ref, b_ref = refs[0], refs[1]
        o_ref = refs[n_in + (buf is not None) + (after is not None)]
        if precise:
            part = lax.dot_general(_mat(a_ref), _mat(b_ref), _DIMS[mode], precision=lax.Precision.HIGHEST,
                                   preferred_element_type=F32)
        else:
            part = lax.dot_general(_mat(a_ref).astype(BF16), _mat(b_ref).astype(BF16), _DIMS[mode],
                                   preferred_element_type=F32)

        def finish(r):
            r = r + refs[2][...] if add is not None else r
            o_ref[...] = r.astype(o_ref.dtype).reshape(o_ref.shape)

        if nk == 1:
            finish(part)
            return
        acc, k = refs[-1], pl.program_id(2)

        @pl.when(k == 0)
        def _():
            acc[...] = part

        @pl.when(k > 0)
        def _():
            acc[...] += part

        @pl.when(k == nk - 1)
        def _():
            finish(acc[...])

    ins = [getattr(a, "arr", a), getattr(b, "arr", b)] + ([add] if add is not None else [])
    in_specs = [a_spec, b_spec] + ([o_spec] if add is not None else [])
    aliases = {}
    if buf is not None:
        ins.append(buf)
        in_specs.append(pl.BlockSpec(memory_space=pl.ANY))
        aliases = {n_in: 0}
    if after is not None:
        ins.append(after)
        in_specs.append(pl.BlockSpec(memory_space=pl.ANY))
    return pl.pallas_call(
        body, name=name, grid=(M // tm, N // tn, nk), in_specs=in_specs, out_specs=o_spec,
        out_shape=getattr(out, "arr", out), input_output_aliases=aliases,
        scratch_shapes=[pltpu.VMEM((tm, tn), F32)] if nk > 1 else [],
        compiler_params=_params(("parallel", "parallel", "arbitrary")),
    )(*ins)


def _rows_call(body, name, ins, outs, tr=512, acc_outs=()):
    n = ins[0].shape[0]
    tr = min(tr, n)
    in_specs = [pl.BlockSpec((tr, a.shape[1]), lambda i: (i, 0)) if a.shape[0] == n else
                pl.BlockSpec(a.shape, lambda i: (0, 0)) for a in ins]
    out_specs = [pl.BlockSpec((tr, w), lambda i: (i, 0)) for w, _ in outs] + \
                [pl.BlockSpec((1, w), lambda i: (0, 0)) for w in acc_outs]
    out_shape = [jax.ShapeDtypeStruct((n, w), dt) for w, dt in outs] + \
                [jax.ShapeDtypeStruct((1, w), F32) for w in acc_outs]
    res = pl.pallas_call(body, name=name, grid=(n // tr,), in_specs=in_specs, out_specs=out_specs,
                         out_shape=out_shape, compiler_params=_params(("arbitrary",)))(*ins)
    return res[0] if len(res) == 1 else res


def _acc(ref, val):
    @pl.when(pl.program_id(0) == 0)
    def _():
        ref[...] = jnp.zeros_like(ref)

    ref[...] += val


def rms_fwd(x, g, name, res=None, out_dtype=F32):
    def body(*refs):
        x_ref, g_ref = refs[0], refs[1]
        o_ref = refs[-1]
        xv = x_ref[...]
        y = xv * lax.rsqrt(jnp.mean(xv * xv, axis=-1, keepdims=True) + EPS) * g_ref[...]
        o_ref[...] = (y + refs[2][...] if res is not None else y).astype(o_ref.dtype)

    ins = [x, g] + ([res] if res is not None else [])
    return _rows_call(body, name, ins, [(x.shape[1], out_dtype)])


def rms_bwd(x, g, dy, name, res=None, out_dtype=F32, after=None):
    def body(*refs):
        x_ref, g_ref, dy_ref = refs[:3]
        dx_ref, dg_ref = refs[-2], refs[-1]
        xv, dyv = x_ref[...], dy_ref[...]
        r = lax.rsqrt(jnp.mean(xv * xv, axis=-1, keepdims=True) + EPS)
        xh = xv * r
        dxh = dyv * g_ref[...]
        dx = r * (dxh - xh * jnp.mean(dxh * xh, axis=-1, keepdims=True))
        dx_ref[...] = (dx + refs[3][...] if res is not None else dx).astype(dx_ref.dtype)
        _acc(dg_ref, jnp.sum(dyv * xh, axis=0, keepdims=True))

    ins = [x, g, dy] + ([res] if res is not None else []) + ([after] if after is not None else [])
    return _rows_call(body, name, ins, [(x.shape[1], out_dtype)], acc_outs=(x.shape[1],))


def _sigmoid(x):
    return 0.5 * jnp.tanh(0.5 * x) + 0.5


FF_W = 704
FF_TR = 512


def _ff_spec(shift):
    return pl.BlockSpec((1, 1, FF_TR, FF_W), lambda d, i: (d + shift, 0, i, 0))


def swiglu_fwd(gu, name):
    def body(g_ref, u_ref, o_ref):
        gate = g_ref[...]
        o_ref[...] = (gate * _sigmoid(gate) * u_ref[...]).astype(BF16)

    return pl.pallas_call(body, name=name, grid=(4, S // FF_TR), in_specs=[_ff_spec(0), _ff_spec(4)],
                          out_specs=_ff_spec(0), out_shape=jax.ShapeDtypeStruct((4, 1, S, FF_W), BF16),
                          compiler_params=_params(("parallel", "parallel")))(gu, gu)


def swiglu_bwd(gu, dact, name):
    def body(g_ref, u_ref, d_ref, o_ref):
        gate, d = g_ref[...], d_ref[...]
        sg = _sigmoid(gate)

        @pl.when(pl.program_id(0) < 4)
        def _():
            o_ref[...] = (d * u_ref[...] * sg * (1.0 + gate * (1.0 - sg))).astype(BF16)

        @pl.when(pl.program_id(0) >= 4)
        def _():
            o_ref[...] = (d * gate * sg).astype(BF16)

    def half(shift):
        return pl.BlockSpec((1, 1, FF_TR, FF_W), lambda d, i: (d % 4 + shift, 0, i, 0))

    return pl.pallas_call(body, name=name, grid=(8, S // FF_TR), in_specs=[half(0), half(4), half(0)],
                          out_specs=_ff_spec(0), out_shape=jax.ShapeDtypeStruct((8, 1, S, FF_W), BF16),
                          compiler_params=_params(("parallel", "parallel")))(gu, gu, dact)


def ple_fwd(h, pp, gt, name):
    def body(h_ref, p_ref, g_ref, o_ref):
        o_ref[...] = h_ref[...] + p_ref[...] * _sigmoid(g_ref[...])

    return _rows_call(body, name, [h, pp, gt], [(D, F32)])


def ple_bwd(dh, pp, gt, name):
    def body(d_ref, p_ref, g_ref, dp_ref, dg_ref):
        d, sg = d_ref[...], _sigmoid(g_ref[...])
        dp_ref[...] = (d * sg).astype(BF16)
        dg_ref[...] = (d * p_ref[...] * sg * (1.0 - sg)).astype(BF16)

    return _rows_call(body, name, [dh, pp, gt], [(D, BF16), (D, BF16)])


def loss_head(y, target, name):
    def body(y_ref, t_ref, d_ref, l_ref):
        e = y_ref[...] - t_ref[...]
        d_ref[...] = e * (1.0 / D)
        col = jnp.sum(e * e, axis=0, keepdims=True) * (0.5 / D)
        _acc(l_ref, sum(col[:, LANES * c:LANES * (c + 1)] for c in range(D // LANES)))

    return _rows_call(body, name, [y, target], [(D, F32)], acc_outs=(LANES,))


def rope_tables(pos_col, name):
    inv = jnp.asarray(_rope_inv_lanes())

    def body(p_ref, inv_ref, c_ref, s_ref):
        ang = p_ref[...].astype(F32) * inv_ref[...]
        lane = lax.broadcasted_iota(jnp.int32, ang.shape, 1)
        first, second = lane < 16, (lane >= 64) & (lane < 80)
        c_ref[...] = jnp.where(first | second, jnp.cos(ang), 1.0)
        sn = jnp.sin(ang)
        s_ref[...] = jnp.where(first, -sn, jnp.where(second, sn, 0.0))

    return _rows_call(body, name, [pos_col, inv], [(LANES, F32), (LANES, F32)])


def _rope(x, c, s):
    return x * c + pltpu.roll(x, 64, axis=1) * s


def _rope_t(d, c, s):
    return d * c + pltpu.roll(d * s, 64, axis=1)


def mla_qk_fwd(qp, kvp, kr, cos, sin, name):
    def body(q_ref, k_ref, kr_ref, c_ref, s_ref, qo_ref, ko_ref):
        c, s = c_ref[...], s_ref[...]
        kr_rot = _rope(kr_ref[...], c, s)
        for h in range(HEADS):
            sl = slice(LANES * h, LANES * (h + 1))
            qo_ref[:, sl] = _rope(q_ref[:, sl], c, s).astype(BF16)
            ko_ref[:, sl] = (k_ref[:, sl] + kr_rot).astype(BF16)

    n = qp.shape[0]
    tr = 256
    w = HEADS * LANES
    return pl.pallas_call(
        body, name=name, grid=(n // tr,),
        in_specs=[pl.BlockSpec((tr, w), lambda i: (i, 0)), pl.BlockSpec((tr, w), lambda i: (i, 0)),
                  pl.BlockSpec((tr, LANES), lambda i: (i, 0)), pl.BlockSpec((tr, LANES), lambda i: (i, 0)),
                  pl.BlockSpec((tr, LANES), lambda i: (i, 0))],
        out_specs=[pl.BlockSpec((tr, w), lambda i: (i, 0))] * 2,
        out_shape=[jax.ShapeDtypeStruct((n, w), BF16)] * 2, compiler_params=_params(("arbitrary",)),
    )(qp, kvp, kr, cos, sin)


def mla_qk_bwd(dq, dk, cos, sin, name):
    def body(dq_ref, dk_ref, c_ref, s_ref, dqp_ref, dkr_ref):
        c, s = c_ref[...], s_ref[...]
        tot = jnp.zeros(c.shape, F32)
        for h in range(HEADS):
            sl = slice(LANES * h, LANES * (h + 1))
            dqp_ref[:, sl] = _rope_t(dq_ref[:, sl], c, s).astype(BF16)
            tot = tot + dk_ref[:, sl]
        dkr_ref[...] = _rope_t(tot, c, s).astype(BF16)

    return _rows_call(body, name, [dq, dk, cos, sin], [(HEADS * LANES, BF16), (LANES, BF16)])


TQ = 256


def _pair_masks(shape):
    lane = lax.broadcasted_iota(jnp.int32, shape, 1)
    return (lane < 64, lane >= 64)


def _scaled_q(q_a, scale):
    return (q_a * jnp.asarray(scale, q_a.dtype), None) if scale == 0.125 else (q_a, scale)


def _causal_probs(q_a, k_a, scale, b0, cq, ck):
    s = lax.dot_general(q_a, k_a, _DIMS["nt"], preferred_element_type=F32)
    if scale is not None:
        s = s * scale
    if cq is not None:
        s = s + (cq - ck)
    row = lax.broadcasted_iota(jnp.int32, (TQ, TQ), 0)
    col = lax.broadcasted_iota(jnp.int32, (TQ, TQ), 1)
    diag = jnp.where(col <= row, s[:, b0:], NEG)
    s = diag if b0 == 0 else jnp.concatenate([s[:, :b0], diag], axis=1)
    e = jnp.exp(s - jnp.max(s, axis=-1, keepdims=True))
    return e * (1.0 / jnp.sum(e, axis=-1, keepdims=True))


def attn_fwd(q, k, v, name, *, wide, scale, q_off=0, k_off=0, v_off=0, cum=None, cum_t=None):
    qw = 2 * LANES if wide else LANES
    forget = cum is not None

    def body(*refs):
        q_ref, k_ref, v_ref = refs[:3]
        o_ref = refs[-1]
        m0, m1 = _pair_masks((TQ, LANES))
        for qi in range(S // TQ):
            b0, b1 = qi * TQ, (qi + 1) * TQ
            outs = []
            for a, msk in enumerate((m0, m1)):
                if wide:
                    q_a, k_a = q_ref[b0:b1, LANES * a:LANES * (a + 1)], k_ref[:b1, LANES * a:LANES * (a + 1)]
                else:
                    q_a, k_a = jnp.where(msk, q_ref[b0:b1, :], jnp.zeros((), BF16)), k_ref[:b1, :]
                cq = refs[3][0, b0:b1, a:a + 1] if forget else None
                ck = refs[4][0, a:a + 1, :b1] if forget else None
                q_a, left = _scaled_q(q_a, scale)
                p = _causal_probs(q_a, k_a, left, b0, cq, ck)
                outs.append(jnp.dot(p.astype(BF16), v_ref[:b1, :], preferred_element_type=F32))
            o_ref[b0:b1, :] = jnp.where(m0, outs[0], outs[1])

    in_specs = [pl.BlockSpec((S, qw), lambda h: (0, q_off * LANES // qw + h)),
                pl.BlockSpec((S, qw), lambda h: (0, k_off * LANES // qw + h)),
                pl.BlockSpec((S, LANES), lambda h: (0, v_off + h))]
    ins = [q, k, v]
    if forget:
        in_specs += [pl.BlockSpec((1, S, 2), lambda h: (h, 0, 0)), pl.BlockSpec((1, 2, S), lambda h: (h, 0, 0))]
        ins += [cum, cum_t]
    return pl.pallas_call(
        body, name=name, grid=(PAIRS,), in_specs=in_specs, out_specs=pl.BlockSpec((S, LANES), lambda h: (0, h)),
        out_shape=jax.ShapeDtypeStruct((S, PAIRS * LANES), F32), compiler_params=_params(("arbitrary",)),
    )(*ins)


def attn_bwd(q, k, v, o, do, name, *, wide, scale, out_dtype=F32, q_off=0, k_off=0, v_off=0, cum=None, cum_t=None):
    qw = 2 * LANES if wide else LANES
    forget = cum is not None

    def body(*refs):
        q_ref, k_ref, v_ref, o_ref, do_ref = refs[:5]
        n_out = 5 if forget else 3
        outs = refs[-(n_out + 2):-2]
        dq_ref, dk_ref, dv_ref = outs[:3]
        dk_acc, dv_acc = refs[-2], refs[-1]
        dk_acc[...] = jnp.zeros_like(dk_acc)
        dv_acc[...] = jnp.zeros_like(dv_acc)
        if forget:
            dcq_ref, dck_ref = outs[3], outs[4]
            dck_ref[...] = jnp.zeros_like(dck_ref)
        m0, m1 = _pair_masks((TQ, LANES))
        for qi in range(S // TQ):
            b0, b1 = qi * TQ, (qi + 1) * TQ
            do2 = do_ref[b0:b1, :]
            dd = do2 * o_ref[b0:b1, :]
            do_b = do2.astype(BF16)
            mk0, mk1 = _pair_masks((b1, LANES))
            dqs = []
            for a, (msk, mk) in enumerate(((m0, mk0), (m1, mk1))):
                lanes = slice(LANES * a, LANES * (a + 1)) if wide else slice(0, LANES)
                if wide:
                    q_a, k_a = q_ref[b0:b1, lanes], k_ref[:b1, lanes]
                else:
                    q_a, k_a = jnp.where(msk, q_ref[b0:b1, :], jnp.zeros((), BF16)), k_ref[:b1, :]
                cq = refs[5][0, b0:b1, a:a + 1] if forget else None
                ck = refs[6][0, a:a + 1, :b1] if forget else None
                q_a, left = _scaled_q(q_a, scale)
                p = _causal_probs(q_a, k_a, left, b0, cq, ck)
                dp = lax.dot_general(jnp.where(msk, do_b, jnp.zeros((), BF16)), v_ref[:b1, :], _DIMS["nt"],
                                     preferred_element_type=F32)
                delta = jnp.sum(jnp.where(msk, dd, 0.0), axis=-1, keepdims=True)
                ds = p * (dp - delta)
                if forget:
                    dcq_ref[0, b0:b1, a:a + 1] = jnp.sum(ds, axis=-1, keepdims=True)
                    dck_ref[0, a:a + 1, :b1] -= jnp.sum(ds, axis=0, keepdims=True)
                ds_b = ds.astype(BF16)
                dqs.append(jnp.dot(ds_b, k_a, preferred_element_type=F32) * scale)
                dk_a = lax.dot_general(ds_b, q_a, _DIMS["tn"], preferred_element_type=F32)
                dk_acc[:b1, lanes] += dk_a if left is None else dk_a * scale
                dv_acc[:b1, :] += jnp.where(mk, lax.dot_general(p.astype(BF16), do_b, _DIMS["tn"],
                                                                 preferred_element_type=F32), 0.0)
            if wide:
                dq_ref[b0:b1, :LANES] = dqs[0].astype(out_dtype)
                dq_ref[b0:b1, LANES:] = dqs[1].astype(out_dtype)
            else:
                dq_ref[b0:b1, :] = jnp.where(m0, dqs[0], dqs[1]).astype(out_dtype)
        dk_ref[...] = dk_acc[...].astype(out_dtype)
        dv_ref[...] = dv_acc[...].astype(out_dtype)

    pair = pl.BlockSpec((S, LANES), lambda h: (0, h))
    qk_out = pl.BlockSpec((S, qw), lambda h: (0, h))
    in_specs = [pl.BlockSpec((S, qw), lambda h: (0, q_off * LANES // qw + h)),
                pl.BlockSpec((S, qw), lambda h: (0, k_off * LANES // qw + h)),
                pl.BlockSpec((S, LANES), lambda h: (0, v_off + h)), pair, pair]
    ins = [q, k, v, o, do]
    out_specs = [qk_out, qk_out, pair]
    out_shape = [jax.ShapeDtypeStruct((S, PAIRS * qw), out_dtype)] * 2 + \
                [jax.ShapeDtypeStruct((S, PAIRS * LANES), out_dtype)]
    if forget:
        by_q, by_k = pl.BlockSpec((1, S, 2), lambda h: (h, 0, 0)), pl.BlockSpec((1, 2, S), lambda h: (h, 0, 0))
        in_specs += [by_q, by_k]
        ins += [cum, cum_t]
        out_specs += [by_q, by_k]
        out_shape += [jax.ShapeDtypeStruct((PAIRS, S, 2), F32), jax.ShapeDtypeStruct((PAIRS, 2, S), F32)]
    return pl.pallas_call(
        body, name=name, grid=(PAIRS,), in_specs=in_specs, out_specs=out_specs, out_shape=out_shape,
        scratch_shapes=[pltpu.VMEM((S, qw), F32), pltpu.VMEM((S, LANES), F32)],
        compiler_params=_params(("arbitrary",)),
    )(*ins)


def _tri(lower):
    r = lax.broadcasted_iota(jnp.int32, (QBLK, QBLK), 0)
    c = lax.broadcasted_iota(jnp.int32, (QBLK, QBLK), 1)
    return jnp.where((c <= r) if lower else (c >= r), 1.0, 0.0).astype(F32)


def _hi_dot(a, b):
    return jnp.dot(a, b, precision=lax.Precision.HIGHEST, preferred_element_type=F32)


def fox_gate_fwd(fl, bias, name):
    def body(f_ref, b_ref, o_ref):
        tri = _tri(True)
        carry = jnp.zeros((1, LANES), F32)
        for n in range(S // QBLK):
            x = f_ref[n * QBLK:(n + 1) * QBLK, :].astype(F32) + b_ref[...]
            lf = jnp.minimum(x, 0.0) - jnp.log(1.0 + jnp.exp(-jnp.abs(x)))
            c = _hi_dot(tri, lf) + carry
            o_ref[n * QBLK:(n + 1) * QBLK, :] = c
            carry = c[QBLK - 1:QBLK, :]

    return pl.pallas_call(body, name=name, out_shape=jax.ShapeDtypeStruct((S, LANES), F32),
                          compiler_params=_params())(fl, bias)


def fox_gate_bwd(fl, bias, dcq, dck, name):
    def body(f_ref, b_ref, dq_ref, dk_ref, o_ref, db_ref):
        tri = _tri(False)
        carry = jnp.zeros((1, LANES), F32)
        db = jnp.zeros((1, LANES), F32)
        for n in reversed(range(S // QBLK)):
            rows = slice(n * QBLK, (n + 1) * QBLK)
            dlf = _hi_dot(tri, dq_ref[rows, :] + dk_ref[rows, :]) + carry
            carry = dlf[0:1, :]
            x = f_ref[rows, :].astype(F32) + b_ref[...]
            dx = dlf * (1.0 - _sigmoid(x))
            o_ref[rows, :] = dx
            db = db + jnp.sum(dx, axis=0, keepdims=True)
        db_ref[...] = db

    return pl.pallas_call(body, name=name, out_shape=[jax.ShapeDtypeStruct((S, LANES), F32),
                                                      jax.ShapeDtypeStruct((1, LANES), F32)],
                          compiler_params=_params())(fl, bias, dcq, dck)


def _band_valid(first):
    w = QBLK if first else 2 * QBLK
    i = lax.broadcasted_iota(jnp.int32, (QBLK, w), 0)
    j = lax.broadcasted_iota(jnp.int32, (QBLK, w), 1)
    return (j <= i) if first else ((j >= i) & (j - QBLK <= i))


def _band_q(q_ref, rows, msk):
    return jnp.where(msk, q_ref[rows, :], jnp.zeros((), BF16)) * jnp.asarray(0.125, BF16)


def _band_logits(q_a, kk, bias, first):
    s = lax.dot_general(q_a, kk, _DIMS["nt"], preferred_element_type=F32) + bias
    return jnp.where(_band_valid(first), s, NEG)


def dil_fwd(qkv, bias, g, name):
    d = DIL[g][1]
    ls = S // d
    view = qkv.reshape(ls, d * 9216)

    def body(q_ref, k_ref, v_ref, b_ref, o_ref, l_ref):
        m0, m1 = _pair_masks((QBLK, LANES))
        for n in range(ls // QBLK):
            rows = slice(n * QBLK, (n + 1) * QBLK)
            keys = rows if n == 0 else slice((n - 1) * QBLK, (n + 1) * QBLK)
            os_, ls_ = [], []
            for a, msk in enumerate((m0, m1)):
                q_a = _band_q(q_ref, rows, msk)
                bias_a = b_ref[a, :, QBLK:] if n == 0 else b_ref[a]
                s = _band_logits(q_a, k_ref[keys, :], bias_a, n == 0)
                mx = jnp.max(s, axis=-1, keepdims=True)
                e = jnp.exp(s - mx)
                l = jnp.sum(e, axis=-1, keepdims=True)
                os_.append(jnp.dot((e * (1.0 / l)).astype(BF16), v_ref[keys, :], preferred_element_type=F32))
                ls_.append(mx + jnp.log(l))
            o_ref[rows, :] = jnp.where(m0, os_[0], os_[1])
            l_ref[rows, :] = jnp.where(m0, ls_[0], ls_[1])

    def col(j):
        return lambda h, r: (0, r * 72 + g * 24 + j * 8 + h)

    out = pl.BlockSpec((ls, LANES), lambda h, r: (0, r * 8 + h))
    o, lse = pl.pallas_call(
        body, name=name, grid=(PAIRS, d),
        in_specs=[pl.BlockSpec((ls, LANES), col(0)), pl.BlockSpec((ls, LANES), col(1)), pl.BlockSpec((ls, LANES), col(2)),
                  pl.BlockSpec((2, QBLK, 2 * QBLK), lambda h, r: (h, 0, 0))],
        out_specs=[out, out], out_shape=[jax.ShapeDtypeStruct((ls, d * D), F32)] * 2,
        compiler_params=_params(("arbitrary", "arbitrary")),
    )(view, view, view, bias)
    return o.reshape(S, D), lse.reshape(S, D)


def dil_merge(os_, lses, name):
    def body(o0, o1, o2, l0, l1, l2, o_ref, l_ref):
        ls_ = [l0[...], l1[...], l2[...]]
        mx = jnp.maximum(jnp.maximum(ls_[0], ls_[1]), ls_[2])
        tot = mx + jnp.log(sum(jnp.exp(l - mx) for l in ls_))
        o_ref[...] = sum(jnp.exp(l - tot) * o[...] for l, o in zip(ls_, (o0, o1, o2)))
        l_ref[...] = tot

    return _rows_call(body, name, list(os_) + list(lses), [(D, F32), (D, F32)])


def dil_bwd(qkv, bias, o, lse, do, g, name):
    d = DIL[g][1]
    ls = S // d
    view = qkv.reshape(ls, d * 9216)
    o, lse, do = (t.reshape(ls, d * D) for t in (o, lse, do))

    def body(q_ref, k_ref, v_ref, b_ref, o_ref, l_ref, do_ref, dq_ref, dk_ref, dv_ref, db_ref, dk_acc, dv_acc):
        @pl.when(pl.program_id(1) == 0)
        def _():
            db_ref[...] = jnp.zeros_like(db_ref)

        dk_acc[...] = jnp.zeros_like(dk_acc)
        dv_acc[...] = jnp.zeros_like(dv_acc)
        m0, m1 = _pair_masks((QBLK, LANES))
        for n in range(ls // QBLK):
            rows = slice(n * QBLK, (n + 1) * QBLK)
            keys = rows if n == 0 else slice((n - 1) * QBLK, (n + 1) * QBLK)
            nk = QBLK if n == 0 else 2 * QBLK
            do2, lse2 = do_ref[rows, :], l_ref[rows, :]
            dd = do2 * o_ref[rows, :]
            do_b = do2.astype(BF16)
            mk0, mk1 = _pair_masks((nk, LANES))
            dqs = []
            for a, (msk, mk) in enumerate(((m0, mk0), (m1, mk1))):
                q_a = _band_q(q_ref, rows, msk)
                kk = k_ref[keys, :]
                bias_a = b_ref[a, :, QBLK:] if n == 0 else b_ref[a]
                s = _band_logits(q_a, kk, bias_a, n == 0)
                lse_a = jnp.max(jnp.where(msk, lse2, -jnp.inf), axis=-1, keepdims=True)
                p = jnp.exp(s - lse_a)
                dp = lax.dot_general(jnp.where(msk, do_b, jnp.zeros((), BF16)), v_ref[keys, :], _DIMS["nt"],
                                     preferred_element_type=F32)
                delta = jnp.sum(jnp.where(msk, dd, 0.0), axis=-1, keepdims=True)
                ds = p * (dp - delta)
                if n == 0:
                    db_ref[a, :, QBLK:] += ds
                else:
                    db_ref[a] += ds
                ds_b = ds.astype(BF16)
                dqs.append(jnp.dot(ds_b, kk, preferred_element_type=F32) * 0.125)
                dk_acc[keys, :] += lax.dot_general(ds_b, q_a, _DIMS["tn"], preferred_element_type=F32)
                dv_acc[keys, :] += jnp.where(mk, lax.dot_general(p.astype(BF16), do_b, _DIMS["tn"],
                                                                 preferred_element_type=F32), 0.0)
            dq_ref[rows, :] = jnp.where(m0, dqs[0], dqs[1]).astype(BF16)
        dk_ref[...] = dk_acc[...].astype(BF16)
        dv_ref[...] = dv_acc[...].astype(BF16)

    def col(j):
        return lambda h, r: (0, r * 72 + g * 24 + j * 8 + h)

    nat = pl.BlockSpec((ls, LANES), lambda h, r: (0, r * 8 + h))
    b_spec = pl.BlockSpec((2, QBLK, 2 * QBLK), lambda h, r: (h, 0, 0))
    dq, dk, dv, db = pl.pallas_call(
        body, name=name, grid=(PAIRS, d),
        in_specs=[pl.BlockSpec((ls, LANES), col(0)), pl.BlockSpec((ls, LANES), col(1)), pl.BlockSpec((ls, LANES), col(2)),
                  b_spec, nat, nat, nat],
        out_specs=[nat, nat, nat, b_spec],
        out_shape=[jax.ShapeDtypeStruct((ls, d * D), BF16)] * 3 + [jax.ShapeDtypeStruct((HEADS, QBLK, 2 * QBLK), F32)],
        scratch_shapes=[pltpu.VMEM((ls, LANES), F32), pltpu.VMEM((ls, LANES), F32)],
        compiler_params=_params(("arbitrary", "arbitrary")),
    )(view, view, view, bias, o, lse, do)
    return dq.reshape(S, D), dk.reshape(S, D), dv.reshape(S, D), db


def _place():
    x, y, c = lax.axis_index("x"), lax.axis_index("y"), lax.axis_index("c")
    return x, y, c


def _dev_slot(ref, by_rows, dev):
    return ref.at[:, dev] if by_rows else ref.at[dev]


def all_gather(shards, by_rows, name, in_vmem=False):
    n = len(shards)

    def body(*refs):
        x_refs, out_refs = refs[:n], refs[n:2 * n]
        send_sems, recv_sems, local_sems = refs[2 * n:]
        x, y, c = _place()
        me, sibling = (x, y, c), (x, y, 1 - c)
        chips = [(1 - x, y), (x, 1 - y), (1 - x, 1 - y)]

        def slot(t, px, py, pc):
            return _dev_slot(out_refs[t], by_rows[t], 4 * px + 2 * py + pc)

        def copy(t, k, blk, to, src=None):
            return pltpu.make_async_remote_copy(
                src_ref=slot(t, *blk) if src is None else src, dst_ref=slot(t, *blk), send_sem=send_sems.at[7 * t + k],
                recv_sem=recv_sems.at[7 * t + k], device_id=to, device_id_type=MESH_ID)

        mine = [pltpu.make_async_copy(x_refs[t], slot(t, *me), local_sems.at[t]) for t in range(n)]
        for cp in mine:
            cp.start()
        first = []
        for t in range(n):
            first.append(copy(t, 0, me, sibling, src=x_refs[t]))
            first += [copy(t, 1 + j, me, (*chip, c), src=x_refs[t]) for j, chip in enumerate(chips)]
        for cp in first:
            cp.start()
        passed = []
        for j, chip in enumerate(chips):
            for t in range(n):
                copy(t, 1 + j, (*chip, c), me).wait_recv()
                passed.append(copy(t, 4 + j, (*chip, c), sibling))
                passed[-1].start()
        for t in range(n):
            copy(t, 0, sibling, me).wait_recv()
            for j, chip in enumerate(chips):
                copy(t, 4 + j, (*chip, 1 - c), me).wait_recv()
        for cp in first + passed:
            cp.wait_send()
        for cp in mine:
            cp.wait()

    def gathered(s, rows):
        shp = (s.shape[0], N_DEV) + s.shape[1:] if rows else (N_DEV,) + s.shape
        return jax.ShapeDtypeStruct(shp, s.dtype)

    space = pl.BlockSpec(memory_space=pltpu.VMEM if in_vmem else pl.ANY)
    return pl.pallas_call(
        body, name=name, out_shape=[gathered(s, r) for s, r in zip(shards, by_rows)],
        in_specs=[space] * n, out_specs=[space] * n,
        scratch_shapes=[pltpu.SemaphoreType.DMA((7 * n,)), pltpu.SemaphoreType.DMA((7 * n,)),
                        pltpu.SemaphoreType.DMA((n,))],
        compiler_params=pltpu.CompilerParams(vmem_limit_bytes=VMEM_LIMIT),
    )(*shards)


_HBM = pl.BlockSpec(memory_space=pltpu.HBM)
_SEM = pl.BlockSpec(memory_space=pltpu.SEMAPHORE)
_SPLIT = dict(has_side_effects=pltpu.SideEffectType.DATAFLOW_SIDE_EFFECTING)


def _hbm(a):
    return pltpu.with_memory_space_constraint(a, pltpu.HBM)


def _gathered_shape(s, rows):
    return (s.shape[0], N_DEV) + s.shape[1:] if rows else (N_DEV,) + s.shape


def _peers(x, y, c):
    return [(1 - x if k & 4 else x, 1 - y if k & 2 else y, 1 - c if k & 1 else c) for k in range(1, N_DEV)]


def gather_start(shards, lands, by_rows, name):
    n = len(shards)

    def body(*refs):
        x_refs, land_refs = refs[:n], refs[n:2 * n]
        send_sems, recv_sems = refs[2 * n], refs[2 * n + 1]
        x, y, c = _place()
        me = 4 * x + 2 * y + c
        for t in range(n):
            for k, peer in enumerate(_peers(x, y, c)):
                pltpu.make_async_remote_copy(
                    src_ref=x_refs[t], dst_ref=_dev_slot(land_refs[t], by_rows[t], me), send_sem=send_sems.at[7 * t + k],
                    recv_sem=recv_sems.at[7 * t + k], device_id=peer, device_id_type=MESH_ID).start()

    sems = pltpu.SemaphoreType.DMA((7 * n,))
    res = pl.pallas_call(
        body, name=name,
        out_shape=(sems, sems) + tuple(pltpu.HBM(a.shape, a.dtype) for a in list(shards) + list(lands)),
        in_specs=[_HBM] * (2 * n), out_specs=(_SEM, _SEM) + (_HBM,) * (2 * n),
        input_output_aliases={i: 2 + i for i in range(2 * n)},
        compiler_params=pltpu.CompilerParams(**_SPLIT),
    )(*[_hbm(a) for a in list(shards) + list(lands)])
    return res[0], res[1], list(res[2:2 + n]), list(res[2 + n:])


def gather_wait(send_sems, recv_sems, first, shards, lands, by_rows, after, name):
    n = len(shards)

    def body(*refs):
        x_refs, land_refs = refs[:n], refs[n:2 * n]
        send_sems, recv_sems = refs[2 * n], refs[2 * n + 1]
        x, y, c = _place()
        for t in range(n):
            for k, (px, py, pc) in enumerate(_peers(x, y, c)):
                cp = pltpu.make_async_remote_copy(
                    src_ref=x_refs[t], dst_ref=_dev_slot(land_refs[t], by_rows[t], 4 * px + 2 * py + pc),
                    send_sem=send_sems.at[7 * (first + t) + k], recv_sem=recv_sems.at[7 * (first + t) + k],
                    device_id=(px, py, pc), device_id_type=MESH_ID)
                cp.wait_send()
                cp.wait_recv()

    res = pl.pallas_call(
        body, name=name, out_shape=tuple(pltpu.HBM(a.shape, a.dtype) for a in list(shards) + list(lands)),
        in_specs=[_HBM] * (2 * n) + [_SEM, _SEM, pl.BlockSpec(memory_space=pl.ANY)], out_specs=(_HBM,) * (2 * n),
        input_output_aliases={i: i for i in range(2 * n)},
        compiler_params=pltpu.CompilerParams(**_SPLIT),
    )(*shards, *lands, send_sems, recv_sems, after)
    return list(res[n:])


def scatter_start(srcs, src_l, lands, land_l, by_rows, name):
    n = len(srcs)

    def body(*refs):
        x_refs, land_refs = refs[:n], refs[n:2 * n]
        send_sems, recv_sems, token = refs[2 * n], refs[2 * n + 1], refs[-1]
        x, y, c = _place()
        me = 4 * x + 2 * y + c
        for k, (px, py, pc) in enumerate(_peers(x, y, c)):
            for t in range(n):
                blk = _dev_slot(x_refs[t], by_rows[t], 4 * px + 2 * py + pc)
                pltpu.make_async_remote_copy(
                    src_ref=blk.at[src_l[t]], dst_ref=land_refs[t].at[me, land_l[t]], send_sem=send_sems.at[7 * t + k],
                    recv_sem=recv_sems.at[7 * t + k], device_id=(px, py, pc), device_id_type=MESH_ID).start()
        token[...] = jnp.zeros_like(token)

    lands = [lax.empty((N_DEV, 1) + s.shape[2:], s.dtype) if ld is None else ld for s, ld in zip(srcs, lands)]
    sems = pltpu.SemaphoreType.DMA((7 * n,))
    res = pl.pallas_call(
        body, name=name,
        out_shape=(sems, sems) + tuple(pltpu.HBM(a.shape, a.dtype) for a in list(srcs) + lands)
        + (jax.ShapeDtypeStruct((8, LANES), F32),),
        in_specs=[_HBM] * (2 * n),
        out_specs=(_SEM, _SEM) + (_HBM,) * (2 * n) + (pl.BlockSpec(memory_space=pltpu.VMEM),),
        input_output_aliases={i: 2 + i for i in range(2 * n)},
        compiler_params=pltpu.CompilerParams(**_SPLIT),
    )(*[_hbm(a) for a in list(srcs) + lands])
    return res[0], res[1], list(res[2:2 + n]), list(res[2 + n:2 + 2 * n]), res[-1]


def scatter_wait(send_sems, recv_sems, srcs, src_l, lands, land_l, by_rows, after, name):
    n = len(srcs)

    def body(*refs):
        x_refs, land_refs = refs[:n], refs[n:2 * n]
        send_sems, recv_sems = refs[2 * n], refs[2 * n + 1]
        x, y, c = _place()
        for k, (px, py, pc) in enumerate(_peers(x, y, c)):
            peer = 4 * px + 2 * py + pc
            for t in range(n):
                cp = pltpu.make_async_remote_copy(
                    src_ref=_dev_slot(x_refs[t], by_rows[t], peer).at[src_l[t]], dst_ref=land_refs[t].at[peer, land_l[t]],
                    send_sem=send_sems.at[7 * t + k], recv_sem=recv_sems.at[7 * t + k], device_id=(px, py, pc),
                    device_id_type=MESH_ID)
                cp.wait_send()
                cp.wait_recv()

    res = pl.pallas_call(
        body, name=name, out_shape=tuple(pltpu.HBM(a.shape, a.dtype) for a in list(srcs) + list(lands)),
        in_specs=[_HBM] * (2 * n) + [_SEM, _SEM, pl.BlockSpec(memory_space=pl.ANY)], out_specs=(_HBM,) * (2 * n),
        input_output_aliases={i: i for i in range(2 * n)},
        compiler_params=pltpu.CompilerParams(**_SPLIT),
    )(*srcs, *lands, send_sems, recv_sems, after)
    return list(res[:n]), list(res[n:])


ADAM_BLOCK_BYTES = 3 << 19


def adamw(w, m, v, parts, name):
    n_parts = parts.shape[0]
    n_l, r, c = w.shape
    lane_c = -(-c // LANES) * LANES
    fits = [t for t in range(16, r, 16) if r % t == 0 and t * lane_c * 4 <= ADAM_BLOCK_BYTES]
    tr = max(fits) if fits and r * lane_c * 4 > ADAM_BLOCK_BYTES else r
    c1 = 1.0 / (1.0 - ADAM_B1 ** ADAM_STEP)
    c2 = 1.0 / (1.0 - ADAM_B2 ** ADAM_STEP)

    def body(w_ref, m_ref, v_ref, p_ref, g_ref, d_ref, nm_ref, nv_ref):
        g = p_ref[0].astype(F32)
        for j in range(1, n_parts):
            g = g + p_ref[j].astype(F32)
        nm = ADAM_B1 * m_ref[...] + (1.0 - ADAM_B1) * g
        nv = ADAM_B2 * v_ref[...] + (1.0 - ADAM_B2) * (g * g)
        g_ref[...] = g
        nm_ref[...] = nm
        nv_ref[...] = nv
        d_ref[...] = -ADAM_LR * ((nm * c1) / (jnp.sqrt(nv * c2) + ADAM_EPS) + ADAM_WD * w_ref[...])

    blk = pl.BlockSpec((1, tr, c), lambda l, i: (l, i, 0))
    return pl.pallas_call(
        body, name=name, grid=(n_l, r // tr),
        in_specs=[blk, blk, blk, pl.BlockSpec((n_parts, 1, tr, c), lambda l, i: (0, l, i, 0))],
        out_specs=[blk] * 4, out_shape=[jax.ShapeDtypeStruct((n_l, r, c), F32)] * 4,
        compiler_params=_params(("parallel", "parallel")),
    )(w, m, v, parts)


def sum_parts(parts, name):
    def body(p_ref, o_ref):
        g = p_ref[0]
        for j in range(1, parts.shape[0]):
            g = g + p_ref[j]
        o_ref[...] = g

    return pl.pallas_call(body, name=name, out_shape=jax.ShapeDtypeStruct(parts.shape[1:], F32),
                          compiler_params=_params())(parts)


def _pack(arrs, rows, dtype):
    flat = jnp.concatenate([a.reshape(-1).astype(dtype) for a in arrs])
    return jnp.pad(flat, (0, rows * LANES - flat.shape[0])).reshape(rows, LANES)


def _unpack(packed, shapes):
    flat, out, off = packed.reshape(-1), [], 0
    for shp in shapes:
        n = int(np.prod(shp))
        out.append(flat[off:off + n].reshape(shp))
        off += n
    return out


def _cat(parts):
    return jnp.concatenate(parts, axis=1)


def _layer_list(i):
    kind, j = i % 3, i // 3
    mix = ([("mla_w_a", j), ("mla_w_uq", j), ("mla_w_ukv", j), ("mla_w_o", j)] if kind == 0 else
           [("dil_w_qkv", 0), ("dil_w_o", 0)] if kind == 1 else [("fox_w_qkvf", 0), ("fox_w_o", 0)])
    return mix + [("ffn_w_in", i), ("ffn_w_out", i), ("ple_w_proj", i), ("ple_w_gate", i)]


def _mla_layout(w_a, g_uq, g_ukv, j):
    def z(r, n):
        return jnp.zeros((r, n), BF16)

    wa = w_a[j]
    a = _cat([wa[:, :640], wa[:, 640:656], z(D, 48), wa[:, 656:672], z(D, 48)])
    q, k, v = [], [], []
    for h in range(HEADS):
        b = g_uq[h // 2, j][:, 96 * (h % 2):96 * (h % 2 + 1)]
        q += [b[:, 64:80], b[:, 0:32], z(Q_RANK, 16), b[:, 80:96], b[:, 32:64], z(Q_RANK, 16)]
        b = g_ukv[h // 2, j][:, LANES * (h % 2):LANES * (h % 2 + 1)]
        k += [z(KV_RANK, 16), b[:, 0:32], z(KV_RANK, 32), b[:, 32:64], z(KV_RANK, 16)]
        v.append(b[:, 64:128])
    return a, _cat(q), _cat(k + v)


def _mla_unlayout(d_a, d_uq, d_ukv):
    a = _cat([d_a[:, :640], d_a[:, 640:656], d_a[:, 704:720]])
    uq, ukv = [], []
    for dev in range(N_DEV):
        q, kv = [], []
        for h in (2 * dev, 2 * dev + 1):
            b = d_uq[:, LANES * h:LANES * (h + 1)]
            q += [b[:, 16:48], b[:, 80:112], b[:, 0:16], b[:, 64:80]]
            b = d_ukv[:, LANES * h:LANES * (h + 1)]
            kv += [b[:, 16:48], b[:, 80:112], d_ukv[:, HEADS * LANES + 64 * h:HEADS * LANES + 64 * (h + 1)]]
        uq.append(_cat(q))
        ukv.append(_cat(kv))
    return a, jnp.stack(uq), jnp.stack(ukv)


def _mixer_fwd(kind, tag, hn, W, aux):
    if kind == 0:
        a = mm(hn, W["w_a"], "nn", f"{tag}_a", tn=768)
        cq = rms_fwd(a[:, :Q_RANK], W["q_norm"], f"{tag}_cq", out_dtype=BF16)
        ckv = rms_fwd(a[:, Q_RANK:Q_RANK + KV_RANK], W["kv_norm"], f"{tag}_ckv", out_dtype=BF16)
        qp = mm(cq, W["w_uq"], "nn", f"{tag}_uq", tk=384)
        kvp = mm(ckv, W["w_ukv"], "nn", f"{tag}_ukv", tk=256)
        q, k = mla_qk_fwd(qp, kvp, a[:, 640:], aux["cos"], aux["sin"], f"{tag}_qk")
        v = kvp.astype(BF16)
        o = attn_fwd(q, k, v, f"{tag}_attn", wide=True, scale=96 ** -0.5, v_off=HEADS)
        y = mm(o, W["w_o"], "nn", f"{tag}_o")
        return y, (a, cq, ckv, q, k, v, o)
    if kind == 1:
        qkv = mm(hn, W["w_qkv"], "nn", f"{tag}_qkv", out_dtype=BF16, tn=1152)
        parts = [dil_fwd(qkv, aux["dil_bias"][g], g, f"{tag}_g{g}") for g in range(3)]
        o, lse = dil_merge([p_[0] for p_ in parts], [p_[1] for p_ in parts], f"{tag}_merge")
        y = mm(o, W["w_o"], "nn", f"{tag}_o")
        return y, (qkv, o, lse)
    a = mm(hn, W["w_qkvf"], "nn", f"{tag}_qkvf", tn=640)
    fl = a[:, 3072:]
    cum = fox_gate_fwd(fl, aux["fox_b"], f"{tag}_gate")[:, :HEADS]
    cum_q = cum.reshape(S, PAIRS, 2).transpose(1, 0, 2)
    cum_k = cum.T.reshape(PAIRS, 2, S)
    ab = a.astype(BF16)
    o = attn_fwd(ab, ab, ab, f"{tag}_attn", wide=False, scale=0.125, k_off=PAIRS, v_off=2 * PAIRS, cum=cum_q, cum_t=cum_k)
    y = mm(o, W["w_o"], "nn", f"{tag}_o")
    return y, (fl, ab, cum_q, cum_k, o)


def _mixer_bwd(kind, tag, hn, dy, W, aux, saved):
    gr = {}
    if kind == 0:
        a, cq, ckv, q, k, v, o = saved
        gr["w_o"] = mm(o, dy, "tn", f"{tag}_dwo", out_dtype=BF16)
        do = mm(dy, W["w_o"], "nt", f"{tag}_do")
        dq, dk, dv = attn_bwd(q, k, v, o, do, f"{tag}_attn_b", wide=True, scale=96 ** -0.5, v_off=HEADS)
        dqp, dkr = mla_qk_bwd(dq, dk, aux["cos"], aux["sin"], f"{tag}_qk_b")
        dkvp = jnp.concatenate([dk, dv], axis=1)
        gr["w_ukv"] = mm(ckv, dkvp, "tn", f"{tag}_dwukv", out_dtype=BF16, tm=256)
        dckv = mm(dkvp, W["w_ukv"], "nt", f"{tag}_dckv", tn=256)
        gr["w_uq"] = mm(cq, dqp, "tn", f"{tag}_dwuq", out_dtype=BF16, tm=384)
        dcq = mm(dqp, W["w_uq"], "nt", f"{tag}_dcq", tn=384)
        da_q, gr["q_norm"] = rms_bwd(a[:, :Q_RANK], W["q_norm"], dcq, f"{tag}_cq_b", out_dtype=BF16)
        da_kv, gr["kv_norm"] = rms_bwd(a[:, Q_RANK:Q_RANK + KV_RANK], W["kv_norm"], dckv, f"{tag}_ckv_b",
                                       out_dtype=BF16)
        da = jnp.concatenate([da_q, da_kv, dkr], axis=1)
        gr["w_a"] = mm(hn, da, "tn", f"{tag}_dwa", out_dtype=BF16, tn=768)
        return mm(da, W["w_a"], "nt", f"{tag}_dhn", tk=768), gr
    if kind == 1:
        qkv, o, lse = saved
        gr["w_o"] = mm(o, dy, "tn", f"{tag}_dwo", out_dtype=BF16)
        do = mm(dy, W["w_o"], "nt", f"{tag}_do")
        cols, dbs = [], []
        for g in range(3):
            dq, dk, dv, db = dil_bwd(qkv, aux["dil_bias"][g], o, lse, do, g, f"{tag}_g{g}_b")
            cols += [dq, dk, dv]
            dbs.append(db)
        dqkv = jnp.concatenate(cols, axis=1)
        gr["dil_dbias"] = dbs
        gr["w_qkv"] = mm(hn, dqkv, "tn", f"{tag}_dwqkv", out_dtype=BF16, out_dev=1152, tn=1152)
        return mm(dqkv, W["w_qkv"], "nt", f"{tag}_dhn", tk=1152), gr
    fl, ab, cum_q, cum_k, o = saved
    gr["w_o"] = mm(o, dy, "tn", f"{tag}_dwo", out_dtype=BF16)
    do = mm(dy, W["w_o"], "nt", f"{tag}_do")
    dq, dk, dv, dcq, dck = attn_bwd(ab, ab, ab, o, do, f"{tag}_attn_b", wide=False, scale=0.125, out_dtype=BF16,
                                    k_off=PAIRS, v_off=2 * PAIRS, cum=cum_q, cum_t=cum_k)
    pad = ((0, 0), (0, LANES - HEADS))
    dcq = jnp.pad(dcq.transpose(1, 0, 2).reshape(S, HEADS), pad)
    dck = jnp.pad(dck.reshape(HEADS, S).T, pad)
    dfl, gr["b_f"] = fox_gate_bwd(fl, aux["fox_b"], dcq, dck, f"{tag}_gate_b")
    da = jnp.concatenate([dq, dk, dv, dfl.astype(BF16)], axis=1)
    gr["w_qkvf"] = mm(hn, da, "tn", f"{tag}_dwqkvf", out_dtype=BF16, tn=640)
    return mm(da, W["w_qkvf"], "nt", f"{tag}_dhn", tk=640), gr


def kernel(x, p, positions, norm_g, ffn_w_in, ffn_w_out, ple_w_proj, ple_w_gate, rel_bias, mla_w_a, mla_q_norm, mla_kv_norm, mla_w_uq, mla_w_ukv, mla_w_o, dil_w_qkv, dil_w_o, fox_w_qkvf, fox_b_f, fox_w_o, loss_target, m_norm_g, m_ffn_w_in, m_ffn_w_out, m_ple_w_proj, m_ple_w_gate, m_rel_bias, m_mla_w_a, m_mla_q_norm, m_mla_kv_norm, m_mla_w_uq, m_mla_w_ukv, m_mla_w_o, m_dil_w_qkv, m_dil_w_o, m_fox_w_qkvf, m_fox_b_f, m_fox_w_o, v_norm_g, v_ffn_w_in, v_ffn_w_out, v_ple_w_proj, v_ple_w_gate, v_rel_bias, v_mla_w_a, v_mla_q_norm, v_mla_kv_norm, v_mla_w_uq, v_mla_w_ukv, v_mla_w_o, v_dil_w_qkv, v_dil_w_o, v_fox_w_qkvf, v_fox_b_f, v_fox_w_o):
    given = dict(locals())
    me = 4 * lax.axis_index("x") + 2 * lax.axis_index("y") + lax.axis_index("c")

    rows_of = {n: axis == 1 for n, _, axis in BIG}
    rows_of["gains"] = False
    shape_of = {n: shp for n, shp, _ in BIG}
    lists = [_layer_list(i) for i in range(DEPTH)]
    lists[0] = [("gains", 0)] + lists[0]
    flat = [nl for ls in lists for nl in ls]
    flat_rows = [rows_of[n] for n, _ in flat]
    gain_rows = _rows(sum(int(np.prod(s)) for _, s, _ in SMALL_SHARDED))
    shards = [_pack([given[k] for k, _, _ in SMALL_SHARDED], gain_rows, F32)[None] if n == "gains" else
              given[n][l:l + 1].astype(BF16) for n, l in flat]
    lands = [lax.dynamic_update_slice(lax.empty(_gathered_shape(s, r), s.dtype), s[:, None] if r else s[None],
                                      (0, me, 0, 0) if r else (me, 0, 0, 0)) for s, r in zip(shards, flat_rows)]
    send_s, recv_s, shards, lands = gather_start(shards, lands, flat_rows, "gather_start")

    cos, sin = rope_tables(positions.reshape(S, 1), "rope_tables")
    dil_bias = [mm(rel_bias[:, HEADS * g:HEADS * (g + 1)], jnp.asarray(_bucket_onehot(DIL[g][1])), "tn",
                   f"dil_bias{g}", precise=True, tn=4096).reshape(HEADS, QBLK, 2 * QBLK) for g in range(3)]
    aux = {"cos": cos, "sin": sin, "dil_bias": dil_bias,
           "fox_b": jnp.pad(fox_b_f, ((0, 0), (0, LANES - HEADS)))}

    def arrived(i, part, behind):
        n_mix = len(lists[i]) - 4
        first = sum(len(ls) for ls in lists[:i]) + (n_mix if part else 0)
        sl = slice(first, first + (4 if part else n_mix))
        got = gather_wait(send_s, recv_s, first, shards[sl], lands[sl], flat_rows[sl], behind, f"gather_wait{i}_{part}")
        return {n: g.reshape(1, N_DEV * shape_of[n][1], shape_of[n][2]) if rows_of[n] else g
                for (n, _), g in zip(flat[sl], got)}

    full = {}

    def mixer_weights(i, behind):
        kind, j = i % 3, i // 3
        w = arrived(i, 0, behind)
        if i == 0:
            gains, off = w["gains"].reshape(N_DEV, gain_rows * LANES), 0
            for n, shp, axis in SMALL_SHARDED:
                cnt = int(np.prod(shp))
                g = jnp.moveaxis(gains[:, off:off + cnt].reshape((N_DEV,) + shp), 0, axis)
                full[n] = g.reshape(shp[:axis] + (N_DEV * shp[axis],))
                off += cnt
        W = {"g": [full["norm_g"][i, r][None, :] for r in range(4)]}
        if kind == 0:
            w_a, w_uq, w_ukv = _mla_layout(w["mla_w_a"], w["mla_w_uq"], w["mla_w_ukv"], 0)
            W.update(w_a=w_a, w_uq=w_uq, w_ukv=w_ukv, w_o=Lay(w["mla_w_o"], 0),
                     q_norm=full["mla_q_norm"][j][None, :], kv_norm=full["mla_kv_norm"][j][None, :])
        elif kind == 1:
            W.update(w_qkv=Dev(w["dil_w_qkv"], 0), w_o=Lay(w["dil_w_o"], 0))
        else:
            fox_w = jnp.pad(_cat([w["fox_w_qkvf"][dev, 0] for dev in range(N_DEV)]), ((0, 0), (0, FOX_W - 3088)))
            W.update(w_qkvf=fox_w, w_o=Lay(w["fox_w_o"], 0))
        return W

    def ffn_weights(i, behind):
        w = arrived(i, 1, behind)
        return {"w_in": Dev(w["ffn_w_in"], 0), "w_out": Lay(w["ffn_w_out"], 0), "w_proj": Dev(w["ple_w_proj"], 0),
                "w_gate": Lay(w["ple_w_gate"], 0)}

    h = x[0]
    saved, weights = [], []
    for i in range(DEPTH):
        kind, j, W = i % 3, i // 3, mixer_weights(i, h)
        weights.append(W)
        t = f"l{i}"
        hn = rms_fwd(h, W["g"][0], f"{t}_n0", out_dtype=BF16)
        y, mix = _mixer_fwd(kind, f"{t}_mix", hn, W, aux)
        W.update(ffn_weights(i, y))
        h1 = rms_fwd(y, W["g"][1], f"{t}_n1", res=h)
        fin = rms_fwd(h1, W["g"][2], f"{t}_n2", out_dtype=BF16)
        gu = mm(fin, W["w_in"], "nn", f"{t}_ffn_in", out_dev=FF_W, tn=FF_W)
        act = swiglu_fwd(gu, f"{t}_swiglu")
        f = mm(Dev(act, 0), W["w_out"], "nn", f"{t}_ffn_out", tk=FF_W)
        h2 = rms_fwd(f, W["g"][3], f"{t}_n3", res=h1)
        pp = mm(p[i, 0], W["w_proj"], "nn", f"{t}_ple_p", tn=LANES, tk=256)
        gt = mm(h2, W["w_gate"], "nn", f"{t}_ple_g")
        h3 = ple_fwd(h2, pp, gt, f"{t}_ple")
        saved.append((h, hn, y, h1, fin, gu, act, f, h2, pp, gt, mix))
        h = h3

    dh, loss_lanes = loss_head(h, loss_target[0], "loss_head")

    grads = {n: None for n, _, _ in BIG}
    landed = {n: (lax.empty((N_DEV,) + shp, BF16) if shp[0] > 1 else None) for n, shp, _ in BIG}
    in_flight = []
    g_norm = [[None] * 4 for _ in range(DEPTH)]
    g_qn, g_kvn = [None, None], [None, None]
    g_rel, g_bf = None, None

    def stacked(n):
        g = grads[n]
        return None if g is None else g.reshape(g.shape[0], N_DEV * g.shape[2], g.shape[3])

    def by_device(g):
        return g.reshape(g.shape[0], N_DEV, g.shape[1] // N_DEV, g.shape[2])

    def start(i, entries, mine, tag):
        names = [n for n, _ in entries]
        srcs = [mine[n] if n in mine else grads[n] for n in names]
        src_l = [0 if n in mine else i for n in names]
        land_l = [l if shape_of[n][0] > 1 else 0 for n, l in entries]
        rows = [rows_of[n] for n in names]
        s_sem, r_sem, srcs, got, token = scatter_start(srcs, src_l, [landed[n] for n in names], land_l, rows, tag)
        for n, src, ld in zip(names, srcs, got):
            landed[n] = ld
            if n in mine:
                mine[n] = src
            else:
                grads[n] = src
        in_flight.append((s_sem, r_sem, names, mine, src_l, land_l, rows))
        return token

    token = None
    for i in reversed(range(DEPTH)):
        kind, j, W = i % 3, i // 3, weights[i]
        t = f"l{i}b"
        h0, hn, y, h1, fin, gu, act, f, h2, pp, gt, mix = saved[i]
        dpp, dgt = ple_bwd(dh, pp, gt, f"{t}_ple")
        grads["ple_w_proj"] = mm(p[i, 0], dpp, "tn", f"{t}_dwp", out_dtype=BF16, out_dev=LANES, tm=256, tn=LANES,
                                 stack=(grads["ple_w_proj"], DEPTH, i), after=token)
        grads["ple_w_gate"] = by_device(mm(h2, dgt, "tn", f"{t}_dwg", out_dtype=BF16,
                                           stack=(stacked("ple_w_gate"), DEPTH, i)))
        dh2 = mm(dgt, W["w_gate"], "nt", f"{t}_dh2", add=dh)
        df, g_norm[i][3] = rms_bwd(f, W["g"][3], dh2, f"{t}_n3", out_dtype=BF16)
        grads["ffn_w_out"] = by_device(mm(Dev(act, 0), df, "tn", f"{t}_dwout", out_dtype=BF16, tm=FF_W,
                                          stack=(stacked("ffn_w_out"), DEPTH, i)))
        dact = mm(df, W["w_out"], "nt", f"{t}_dact", out_dev=FF_W, tn=FF_W)
        dgu = swiglu_bwd(gu, dact, f"{t}_swiglu")
        grads["ffn_w_in"] = mm(fin, Dev(dgu, 0), "tn", f"{t}_dwin", out_dtype=BF16, out_dev=FF_W, tn=FF_W,
                               stack=(grads["ffn_w_in"], DEPTH, i))
        token = start(i, lists[i][-4:], {}, f"scatter_ffn{i}")
        dfin = mm(Dev(dgu, 0), W["w_in"], "nt", f"{t}_dfin", after=token, tk=FF_W)
        dh1, g_norm[i][2] = rms_bwd(h1, W["g"][2], dfin, f"{t}_n2", res=dh2)
        dy, g_norm[i][1] = rms_bwd(y, W["g"][1], dh1, f"{t}_n1", out_dtype=BF16)
        dhn, gr = _mixer_bwd(kind, f"{t}_mix", hn, dy, W, aux, mix)
        if kind == 0:
            d_a, d_uq, d_ukv = _mla_unlayout(gr["w_a"], gr["w_uq"], gr["w_ukv"])
            mine = {"mla_w_a": by_device(d_a[None]), "mla_w_uq": d_uq[:, None], "mla_w_ukv": d_ukv[:, None],
                    "mla_w_o": by_device(gr["w_o"][None])}
            g_qn[j], g_kvn[j] = gr["q_norm"], gr["kv_norm"]
        elif kind == 1:
            mine = {"dil_w_qkv": gr["w_qkv"], "dil_w_o": by_device(gr["w_o"][None])}
            g_rel = jnp.concatenate(
                [mm(jnp.asarray(_bucket_onehot(DIL[g][1])), gr["dil_dbias"][g].reshape(HEADS, -1), "nt",
                    f"{t}_drel{g}", precise=True, tk=4096) for g in range(3)], axis=1)
        else:
            wide = gr["w_qkvf"]
            mine = {"fox_w_qkvf": jnp.stack([wide[:, 386 * dev:386 * (dev + 1)] for dev in range(N_DEV)])[:, None],
                    "fox_w_o": by_device(gr["w_o"][None])}
            g_bf = gr["b_f"][:, :HEADS]
        token = start(i, [e for e in lists[i][:-4] if e[0] in mine], mine, f"scatter_mix{i}")
        dh, g_norm[i][0] = rms_bwd(h0, W["g"][0], dhn, f"{t}_n0", res=dh1, after=token)
    grad_x = dh[None]

    own = {n: [] for n, _, _ in BIG}
    for idx, (s_sem, r_sem, names, mine, src_l, land_l, rows) in enumerate(in_flight):
        srcs = [mine[n] if n in mine else grads[n] for n in names]
        srcs, got = scatter_wait(s_sem, r_sem, srcs, src_l, [landed[n] for n in names], land_l, rows, dh,
                                 f"scatter_wait{idx}")
        for n, src, ld, sl, ll, rw in zip(names, srcs, got, src_l, land_l, rows):
            landed[n] = ld
            if n not in mine:
                grads[n] = src
            blk = lax.dynamic_index_in_dim(src, me, axis=1 if rw else 0, keepdims=False)[sl]
            own[n].append((ll, blk))
    big_out = []
    for n, _, _ in BIG:
        part = landed[n]
        for ll, blk in own[n]:
            part = lax.dynamic_update_slice(part, blk[None, None], (me, ll, 0, 0))
        big_out.append(adamw(given[n], given["m_" + n], given["v_" + n], part, f"adamw_{n}"))

    small_full = [jnp.stack([jnp.concatenate(r, axis=0) for r in g_norm]).reshape(-1),
                  jnp.concatenate(g_qn, axis=0).reshape(-1), jnp.concatenate(g_kvn, axis=0).reshape(-1),
                  g_rel.reshape(-1), g_bf.reshape(-1), loss_lanes.reshape(-1)]
    small_n = sum(a.shape[0] for a in small_full)
    small_rows = _rows(small_n)
    parts, = all_gather([_pack(small_full, small_rows, F32)], [False], "gather_small_grads", in_vmem=True)
    tot = _unpack(sum_parts(parts, "sum_small_grads"), [(4, 4, D), (2, Q_RANK), (2, KV_RANK), (32, 48), (1, 16), (LANES,)])
    loss = jnp.sum(tot[5])
    small_g = [lax.dynamic_slice_in_dim(tot[0], me * 128, 128, axis=2), lax.dynamic_slice_in_dim(tot[1], me * 48, 48, axis=1),
               lax.dynamic_slice_in_dim(tot[2], me * 32, 32, axis=1), tot[3], tot[4]]
    small_names = [n for n, _, _ in SMALL_SHARDED] + [n for n, _ in SMALL_REPL]
    small_shapes = [s for _, s, _ in SMALL_SHARDED] + [s for _, s in SMALL_REPL]
    s_rows = _rows(sum(int(np.prod(s)) for s in small_shapes))
    small_out = adamw(_pack([given[n] for n in small_names], s_rows, F32)[None],
                      _pack([given["m_" + n] for n in small_names], s_rows, F32)[None],
                      _pack([given["v_" + n] for n in small_names], s_rows, F32)[None],
                      _pack(small_g, s_rows, F32)[None, None], "adamw_small")
    small_out = [_unpack(o_, small_shapes) for o_ in small_out]

    res = [{}, {}, {}, {}]
    for k in range(4):
        for idx, (n, _, _) in enumerate(BIG):
            res[k][n] = big_out[idx][k]
        for idx, n in enumerate(small_names):
            res[k][n] = small_out[k][idx]
    return (loss, grad_x, *[res[0][n] for n in WEIGHTS], *[res[1][n] for n in WEIGHTS],
            *[res[2][n] for n in WEIGHTS], *[res[3][n] for n in WEIGHTS])
```

```python
import math
from typing import NamedTuple

import numpy as np
import jax
import jax.numpy as jnp
from jax import lax
from jax.experimental import pallas as pl
from jax.experimental.pallas import tpu as pltpu

F32 = jnp.float32
BF16 = jnp.bfloat16
MESH_ID = pl.DeviceIdType.MESH

N_DEV = 8
S = 2048
D = 1024
DEPTH = 4
D_FF = 2816
D_PLE = 256
EPS = 1e-6
NEG = -1e30
LANES = 128
HEADS = 16
PAIRS = 8
Q_RANK = 384
KV_RANK = 256
QBLK = 128
DIL = ((128, 1), (512, 4), (2048, 16))
REL_BUCKETS = 32
FOX_W = 3200
VMEM_LIMIT = 56 * 1024 * 1024

ADAM_LR, ADAM_B1, ADAM_B2, ADAM_EPS, ADAM_WD, ADAM_STEP = 1e-3, 0.9, 0.999, 1e-8, 0.01, 10


TRANSPOSED = ("ffn_w_in",)
BIG = (
    ("ffn_w_in", (4, 704, 1024), 1), ("ffn_w_out", (4, 352, 1024), 1),
    ("ple_w_proj", (4, 256, 128), 2), ("ple_w_gate", (4, 128, 1024), 1),
    ("mla_w_a", (2, 128, 672), 1), ("mla_w_uq", (2, 384, 192), 2),
    ("mla_w_ukv", (2, 256, 256), 2), ("mla_w_o", (2, 128, 1024), 1),
    ("dil_w_qkv", (1, 1024, 1152), 2), ("dil_w_o", (1, 128, 1024), 1),
    ("fox_w_qkvf", (1, 1024, 386), 2), ("fox_w_o", (1, 128, 1024), 1),
)
SMALL_SHARDED = (("norm_g", (4, 4, 128), 2), ("mla_q_norm", (2, 48), 1), ("mla_kv_norm", (2, 32), 1))
SMALL_REPL = (("rel_bias", (32, 48)), ("fox_b_f", (1, 16)))
WEIGHTS = ("norm_g", "ffn_w_in", "ffn_w_out", "ple_w_proj", "ple_w_gate", "rel_bias", "mla_w_a", "mla_q_norm",
           "mla_kv_norm", "mla_w_uq", "mla_w_ukv", "mla_w_o", "dil_w_qkv", "dil_w_o", "fox_w_qkvf", "fox_b_f",
           "fox_w_o")


def _rows(n):
    return -(-n // (8 * LANES)) * 8


def _t5_bucket_np(dist):
    max_exact = REL_BUCKETS // 2
    n = np.maximum(dist.astype(np.float32), np.float32(1.0))
    large = max_exact + (np.log(n / np.float32(max_exact)) / np.float32(math.log(2048 / max_exact))
                         * np.float32(REL_BUCKETS - max_exact)).astype(np.int32)
    large = np.minimum(large, REL_BUCKETS - 1)
    return np.where(dist < max_exact, dist, large)


def _bucket_onehot(dilation):
    i = np.arange(QBLK)[:, None]
    j = np.arange(2 * QBLK)[None, :]
    bucket = _t5_bucket_np(np.clip(QBLK + i - j, 0, None) * dilation).reshape(-1)
    return (np.arange(REL_BUCKETS)[:, None] == bucket[None, :]).astype(np.float32)


def _rope_inv_lanes():
    half = 16
    inv = (np.float32(10000.0) ** (-np.arange(half, dtype=np.float32) / np.float32(half))).astype(np.float32)
    t = np.zeros((1, LANES), np.float32)
    t[0, 0:16] = inv
    t[0, 64:80] = inv
    return t


def _params(sem=None):
    return pltpu.CompilerParams(dimension_semantics=sem, vmem_limit_bytes=VMEM_LIMIT)


def _tile(dim, target):
    if dim <= target or dim % target == 0:
        return min(dim, target)
    t = (target // LANES) * LANES
    while dim % t:
        t -= LANES
    return t


_DIMS = {"nn": (((1,), (0,)), ((), ())), "nt": (((1,), (1,)), ((), ())), "tn": (((0,), (0,)), ((), ()))}


class Lay(NamedTuple):
    arr: jax.Array
    l: int


class Dev(NamedTuple):
    arr: jax.Array
    l: int


def _lshape(op):
    if isinstance(op, Dev):
        g, _, r, w = op.arr.shape
        return r, g * w
    return op.arr.shape[1:] if isinstance(op, Lay) else op.shape


def _op_spec(op, rows_t, cols_t, row_ix, col_ix):
    if isinstance(op, Dev):
        w = op.arr.shape[3]
        assert w % cols_t == 0 and (cols_t % LANES == 0 or cols_t == w), (w, cols_t)
        nb, l = w // cols_t, op.l
        return pl.BlockSpec((1, 1, rows_t, cols_t),
                            lambda i, j, k: (col_ix(i, j, k) // nb, l, row_ix(i, j, k), col_ix(i, j, k) % nb))
    if isinstance(op, Lay):
        l = op.l
        return pl.BlockSpec((1, rows_t, cols_t), lambda i, j, k: (l, row_ix(i, j, k), col_ix(i, j, k)))
    return pl.BlockSpec((rows_t, cols_t), lambda i, j, k: (row_ix(i, j, k), col_ix(i, j, k)))


def _mat(ref):
    return ref[(0,) * (len(ref.shape) - 2)]


def mm(a, b, mode, name, out_dtype=F32, precise=False, add=None, out_dev=None, stack=None, after=None,
       tm=1024, tn=512, tk=2048):
    (ar, ac), (br, bc) = _lshape(a), _lshape(b)
    M, K = (ac, ar) if mode == "tn" else (ar, ac)
    N = br if mode == "nt" else bc
    assert K == (bc if mode == "nt" else br)
    tm, tn, tk = _tile(M, tm), _tile(N, tn), _tile(K, tk)
    nk = K // tk
    ix_i, ix_j, ix_k = (lambda i, j, k: i), (lambda i, j, k: j), (lambda i, j, k: k)
    a_spec = _op_spec(a, tk, tm, ix_k, ix_i) if mode == "tn" else _op_spec(a, tm, tk, ix_i, ix_k)
    b_spec = _op_spec(b, tn, tk, ix_j, ix_k) if mode == "nt" else _op_spec(b, tk, tn, ix_k, ix_j)
    buf, n_l, l = stack if stack is not None else (None, 1, 0)
    if out_dev is not None:
        out = Dev(jax.ShapeDtypeStruct((N // out_dev, n_l, M, out_dev), out_dtype), l)
    elif stack is not None:
        out = Lay(jax.ShapeDtypeStruct((n_l, M, N), out_dtype), l)
    else:
        out = jax.ShapeDtypeStruct((M, N), out_dtype)
    o_spec = _op_spec(out, tm, tn, ix_i, ix_j)
    n_in = 3 if add is not None else 2

    def body(*refs):
        a_ref, b_ref = refs[0], refs[1]
        o_ref = refs[n_in + (buf is not None) + (after is not None)]
        if precise:
            part = lax.dot_general(_mat(a_ref), _mat(b_ref), _DIMS[mode], precision=lax.Precision.HIGHEST,
                                   preferred_element_type=F32)
        else:
            part = lax.dot_general(_mat(a_ref).astype(BF16), _mat(b_ref).astype(BF16), _DIMS[mode],
                                   preferred_element_type=F32)

        def finish(r):
            r = r + refs[2][...] if add is not None else r
            o_ref[...] = r.astype(o_ref.dtype).reshape(o_ref.shape)

        if nk == 1:
            finish(part)
            return
        acc, k = refs[-1], pl.program_id(2)

        @pl.when(k == 0)
        def _():
            acc[...] = part

        @pl.when(k > 0)
        def _():
            acc[...] += part

        @pl.when(k == nk - 1)
        def _():
            finish(acc[...])

    ins = [getattr(a, "arr", a), getattr(b, "arr", b)] + ([add] if add is not None else [])
    in_specs = [a_spec, b_spec] + ([o_spec] if add is not None else [])
    aliases = {}
    if buf is not None:
        ins.append(buf)
        in_specs.append(pl.BlockSpec(memory_space=pl.ANY))
        aliases = {n_in: 0}
    if after is not None:
        ins.append(after)
        in_specs.append(pl.BlockSpec(memory_space=pl.ANY))
    return pl.pallas_call(
        body, name=name, grid=(M // tm, N // tn, nk), in_specs=in_specs, out_specs=o_spec,
        out_shape=getattr(out, "arr", out), input_output_aliases=aliases,
        scratch_shapes=[pltpu.VMEM((tm, tn), F32)] if nk > 1 else [],
        compiler_params=_params(("parallel", "parallel", "arbitrary")),
    )(*ins)


def _rows_call(body, name, ins, outs, tr=512, acc_outs=()):
    n = ins[0].shape[0]
    tr = min(tr, n)
    in_specs = [pl.BlockSpec((tr, a.shape[1]), lambda i: (i, 0)) if a.shape[0] == n else
                pl.BlockSpec(a.shape, lambda i: (0, 0)) for a in ins]
    out_specs = [pl.BlockSpec((tr, w), lambda i: (i, 0)) for w, _ in outs] + \
                [pl.BlockSpec((1, w), lambda i: (0, 0)) for w in acc_outs]
    out_shape = [jax.ShapeDtypeStruct((n, w), dt) for w, dt in outs] + \
                [jax.ShapeDtypeStruct((1, w), F32) for w in acc_outs]
    res = pl.pallas_call(body, name=name, grid=(n // tr,), in_specs=in_specs, out_specs=out_specs,
                         out_shape=out_shape, compiler_params=_params(("arbitrary",)))(*ins)
    return res[0] if len(res) == 1 else res


def _acc(ref, val):
    @pl.when(pl.program_id(0) == 0)
    def _():
        ref[...] = jnp.zeros_like(ref)

    ref[...] += val


def rms_fwd(x, g, name, res=None, out_dtype=F32):
    def body(*refs):
        x_ref, g_ref = refs[0], refs[1]
        o_ref = refs[-1]
        xv = x_ref[...]
        y = xv * lax.rsqrt(jnp.mean(xv * xv, axis=-1, keepdims=True) + EPS) * g_ref[...]
        o_ref[...] = (y + refs[2][...] if res is not None else y).astype(o_ref.dtype)

    ins = [x, g] + ([res] if res is not None else [])
    return _rows_call(body, name, ins, [(x.shape[1], out_dtype)])


def rms_bwd(x, g, dy, name, res=None, out_dtype=F32, after=None):
    def body(*refs):
        x_ref, g_ref, dy_ref = refs[:3]
        dx_ref, dg_ref = refs[-2], refs[-1]
        xv, dyv = x_ref[...], dy_ref[...]
        r = lax.rsqrt(jnp.mean(xv * xv, axis=-1, keepdims=True) + EPS)
        xh = xv * r
        dxh = dyv * g_ref[...]
        dx = r * (dxh - xh * jnp.mean(dxh * xh, axis=-1, keepdims=True))
        dx_ref[...] = (dx + refs[3][...] if res is not None else dx).astype(dx_ref.dtype)
        _acc(dg_ref, jnp.sum(dyv * xh, axis=0, keepdims=True))

    ins = [x, g, dy] + ([res] if res is not None else []) + ([after] if after is not None else [])
    return _rows_call(body, name, ins, [(x.shape[1], out_dtype)], acc_outs=(x.shape[1],))


def _sigmoid(x):
    return 0.5 * jnp.tanh(0.5 * x) + 0.5


def swiglu_fwd(gu, name):
    def body(gu_ref, o_ref):
        gate = gu_ref[:, :D_FF]
        o_ref[...] = (gate * _sigmoid(gate) * gu_ref[:, D_FF:]).astype(BF16)

    return _rows_call(body, name, [gu], [(D_FF, BF16)], tr=256)


def swiglu_bwd(gu, dact, name):
    def body(gu_ref, d_ref, o_ref):
        gate, d = gu_ref[:, :D_FF], d_ref[...]
        sg = _sigmoid(gate)
        o_ref[:, :D_FF] = (d * gu_ref[:, D_FF:] * sg * (1.0 + gate * (1.0 - sg))).astype(BF16)
        o_ref[:, D_FF:] = (d * gate * sg).astype(BF16)

    return _rows_call(body, name, [gu, dact], [(2 * D_FF, BF16)], tr=256)


def ple_fwd(h, pp, gt, name):
    def body(h_ref, p_ref, g_ref, o_ref):
        o_ref[...] = h_ref[...] + p_ref[...] * _sigmoid(g_ref[...])

    return _rows_call(body, name, [h, pp, gt], [(D, F32)])


def ple_bwd(dh, pp, gt, name):
    def body(d_ref, p_ref, g_ref, dp_ref, dg_ref):
        d, sg = d_ref[...], _sigmoid(g_ref[...])
        dp_ref[...] = (d * sg).astype(BF16)
        dg_ref[...] = (d * p_ref[...] * sg * (1.0 - sg)).astype(BF16)

    return _rows_call(body, name, [dh, pp, gt], [(D, BF16), (D, BF16)])


def loss_head(y, target, name):
    def body(y_ref, t_ref, d_ref, l_ref):
        e = y_ref[...] - t_ref[...]
        d_ref[...] = e * (1.0 / D)
        col = jnp.sum(e * e, axis=0, keepdims=True) * (0.5 / D)
        _acc(l_ref, sum(col[:, LANES * c:LANES * (c + 1)] for c in range(D // LANES)))

    return _rows_call(body, name, [y, target], [(D, F32)], acc_outs=(LANES,))


def rope_tables(pos_col, name):
    inv = jnp.asarray(_rope_inv_lanes())

    def body(p_ref, inv_ref, c_ref, s_ref):
        ang = p_ref[...].astype(F32) * inv_ref[...]
        lane = lax.broadcasted_iota(jnp.int32, ang.shape, 1)
        first, second = lane < 16, (lane >= 64) & (lane < 80)
        c_ref[...] = jnp.where(first | second, jnp.cos(ang), 1.0)
        sn = jnp.sin(ang)
        s_ref[...] = jnp.where(first, -sn, jnp.where(second, sn, 0.0))

    return _rows_call(body, name, [pos_col, inv], [(LANES, F32), (LANES, F32)])


def _rope(x, c, s):
    return x * c + pltpu.roll(x, 64, axis=1) * s


def _rope_t(d, c, s):
    return d * c + pltpu.roll(d * s, 64, axis=1)


def mla_qk_fwd(qp, kvp, kr, cos, sin, name):
    def body(q_ref, k_ref, kr_ref, c_ref, s_ref, qo_ref, ko_ref):
        c, s = c_ref[...], s_ref[...]
        kr_rot = _rope(kr_ref[...], c, s)
        for h in range(HEADS):
            sl = slice(LANES * h, LANES * (h + 1))
            qo_ref[:, sl] = _rope(q_ref[:, sl], c, s).astype(BF16)
            ko_ref[:, sl] = (k_ref[:, sl] + kr_rot).astype(BF16)

    n = qp.shape[0]
    tr = 256
    w = HEADS * LANES
    return pl.pallas_call(
        body, name=name, grid=(n // tr,),
        in_specs=[pl.BlockSpec((tr, w), lambda i: (i, 0)), pl.BlockSpec((tr, w), lambda i: (i, 0)),
                  pl.BlockSpec((tr, LANES), lambda i: (i, 0)), pl.BlockSpec((tr, LANES), lambda i: (i, 0)),
                  pl.BlockSpec((tr, LANES), lambda i: (i, 0))],
        out_specs=[pl.BlockSpec((tr, w), lambda i: (i, 0))] * 2,
        out_shape=[jax.ShapeDtypeStruct((n, w), BF16)] * 2, compiler_params=_params(("arbitrary",)),
    )(qp, kvp, kr, cos, sin)


def mla_qk_bwd(dq, dk, cos, sin, name):
    def body(dq_ref, dk_ref, c_ref, s_ref, dqp_ref, dkr_ref):
        c, s = c_ref[...], s_ref[...]
        tot = jnp.zeros(c.shape, F32)
        for h in range(HEADS):
            sl = slice(LANES * h, LANES * (h + 1))
            dqp_ref[:, sl] = _rope_t(dq_ref[:, sl], c, s).astype(BF16)
            tot = tot + dk_ref[:, sl]
        dkr_ref[...] = _rope_t(tot, c, s).astype(BF16)

    return _rows_call(body, name, [dq, dk, cos, sin], [(HEADS * LANES, BF16), (LANES, BF16)])


TQ = 256


def _pair_masks(shape):
    lane = lax.broadcasted_iota(jnp.int32, shape, 1)
    return (lane < 64, lane >= 64)


def _scaled_q(q_a, scale):
    return (q_a * jnp.asarray(scale, q_a.dtype), None) if scale == 0.125 else (q_a, scale)


def _causal_probs(q_a, k_a, scale, b0, cq, ck):
    s = lax.dot_general(q_a, k_a, _DIMS["nt"], preferred_element_type=F32)
    if scale is not None:
        s = s * scale
    if cq is not None:
        s = s + (cq - ck)
    row = lax.broadcasted_iota(jnp.int32, (TQ, TQ), 0)
    col = lax.broadcasted_iota(jnp.int32, (TQ, TQ), 1)
    diag = jnp.where(col <= row, s[:, b0:], NEG)
    s = diag if b0 == 0 else jnp.concatenate([s[:, :b0], diag], axis=1)
    e = jnp.exp(s - jnp.max(s, axis=-1, keepdims=True))
    return e * (1.0 / jnp.sum(e, axis=-1, keepdims=True))


def attn_fwd(q, k, v, name, *, wide, scale, q_off=0, k_off=0, v_off=0, cum=None, cum_t=None):
    qw = 2 * LANES if wide else LANES
    forget = cum is not None

    def body(*refs):
        q_ref, k_ref, v_ref = refs[:3]
        o_ref = refs[-1]
        m0, m1 = _pair_masks((TQ, LANES))
        for qi in range(S // TQ):
            b0, b1 = qi * TQ, (qi + 1) * TQ
            outs = []
            for a, msk in enumerate((m0, m1)):
                if wide:
                    q_a, k_a = q_ref[b0:b1, LANES * a:LANES * (a + 1)], k_ref[:b1, LANES * a:LANES * (a + 1)]
                else:
                    q_a, k_a = jnp.where(msk, q_ref[b0:b1, :], jnp.zeros((), BF16)), k_ref[:b1, :]
                cq = refs[3][0, b0:b1, a:a + 1] if forget else None
                ck = refs[4][0, a:a + 1, :b1] if forget else None
                q_a, left = _scaled_q(q_a, scale)
                p = _causal_probs(q_a, k_a, left, b0, cq, ck)
                outs.append(jnp.dot(p.astype(BF16), v_ref[:b1, :], preferred_element_type=F32))
            o_ref[b0:b1, :] = jnp.where(m0, outs[0], outs[1])

    in_specs = [pl.BlockSpec((S, qw), lambda h: (0, q_off * LANES // qw + h)),
                pl.BlockSpec((S, qw), lambda h: (0, k_off * LANES // qw + h)),
                pl.BlockSpec((S, LANES), lambda h: (0, v_off + h))]
    ins = [q, k, v]
    if forget:
        in_specs += [pl.BlockSpec((1, S, 2), lambda h: (h, 0, 0)), pl.BlockSpec((1, 2, S), lambda h: (h, 0, 0))]
        ins += [cum, cum_t]
    return pl.pallas_call(
        body, name=name, grid=(PAIRS,), in_specs=in_specs, out_specs=pl.BlockSpec((S, LANES), lambda h: (0, h)),
        out_shape=jax.ShapeDtypeStruct((S, PAIRS * LANES), F32), compiler_params=_params(("arbitrary",)),
    )(*ins)


def attn_bwd(q, k, v, o, do, name, *, wide, scale, out_dtype=F32, q_off=0, k_off=0, v_off=0, cum=None, cum_t=None):
    qw = 2 * LANES if wide else LANES
    forget = cum is not None

    def body(*refs):
        q_ref, k_ref, v_ref, o_ref, do_ref = refs[:5]
        n_out = 5 if forget else 3
        outs = refs[-(n_out + 2):-2]
        dq_ref, dk_ref, dv_ref = outs[:3]
        dk_acc, dv_acc = refs[-2], refs[-1]
        dk_acc[...] = jnp.zeros_like(dk_acc)
        dv_acc[...] = jnp.zeros_like(dv_acc)
        if forget:
            dcq_ref, dck_ref = outs[3], outs[4]
            dck_ref[...] = jnp.zeros_like(dck_ref)
        m0, m1 = _pair_masks((TQ, LANES))
        for qi in range(S // TQ):
            b0, b1 = qi * TQ, (qi + 1) * TQ
            do2 = do_ref[b0:b1, :]
            dd = do2 * o_ref[b0:b1, :]
            do_b = do2.astype(BF16)
            mk0, mk1 = _pair_masks((b1, LANES))
            dqs = []
            for a, (msk, mk) in enumerate(((m0, mk0), (m1, mk1))):
                lanes = slice(LANES * a, LANES * (a + 1)) if wide else slice(0, LANES)
                if wide:
                    q_a, k_a = q_ref[b0:b1, lanes], k_ref[:b1, lanes]
                else:
                    q_a, k_a = jnp.where(msk, q_ref[b0:b1, :], jnp.zeros((), BF16)), k_ref[:b1, :]
                cq = refs[5][0, b0:b1, a:a + 1] if forget else None
                ck = refs[6][0, a:a + 1, :b1] if forget else None
                q_a, left = _scaled_q(q_a, scale)
                p = _causal_probs(q_a, k_a, left, b0, cq, ck)
                dp = lax.dot_general(jnp.where(msk, do_b, jnp.zeros((), BF16)), v_ref[:b1, :], _DIMS["nt"],
                                     preferred_element_type=F32)
                delta = jnp.sum(jnp.where(msk, dd, 0.0), axis=-1, keepdims=True)
                ds = p * (dp - delta)
                if forget:
                    dcq_ref[0, b0:b1, a:a + 1] = jnp.sum(ds, axis=-1, keepdims=True)
                    dck_ref[0, a:a + 1, :b1] -= jnp.sum(ds, axis=0, keepdims=True)
                ds_b = ds.astype(BF16)
                dqs.append(jnp.dot(ds_b, k_a, preferred_element_type=F32) * scale)
                dk_a = lax.dot_general(ds_b, q_a, _DIMS["tn"], preferred_element_type=F32)
                dk_acc[:b1, lanes] += dk_a if left is None else dk_a * scale
                dv_acc[:b1, :] += jnp.where(mk, lax.dot_general(p.astype(BF16), do_b, _DIMS["tn"],
                                                                 preferred_element_type=F32), 0.0)
            if wide:
                dq_ref[b0:b1, :LANES] = dqs[0].astype(out_dtype)
                dq_ref[b0:b1, LANES:] = dqs[1].astype(out_dtype)
            else:
                dq_ref[b0:b1, :] = jnp.where(m0, dqs[0], dqs[1]).astype(out_dtype)
        dk_ref[...] = dk_acc[...].astype(out_dtype)
        dv_ref[...] = dv_acc[...].astype(out_dtype)

    pair = pl.BlockSpec((S, LANES), lambda h: (0, h))
    qk_out = pl.BlockSpec((S, qw), lambda h: (0, h))
    in_specs = [pl.BlockSpec((S, qw), lambda h: (0, q_off * LANES // qw + h)),
                pl.BlockSpec((S, qw), lambda h: (0, k_off * LANES // qw + h)),
                pl.BlockSpec((S, LANES), lambda h: (0, v_off + h)), pair, pair]
    ins = [q, k, v, o, do]
    out_specs = [qk_out, qk_out, pair]
    out_shape = [jax.ShapeDtypeStruct((S, PAIRS * qw), out_dtype)] * 2 + \
                [jax.ShapeDtypeStruct((S, PAIRS * LANES), out_dtype)]
    if forget:
        by_q, by_k = pl.BlockSpec((1, S, 2), lambda h: (h, 0, 0)), pl.BlockSpec((1, 2, S), lambda h: (h, 0, 0))
        in_specs += [by_q, by_k]
        ins += [cum, cum_t]
        out_specs += [by_q, by_k]
        out_shape += [jax.ShapeDtypeStruct((PAIRS, S, 2), F32), jax.ShapeDtypeStruct((PAIRS, 2, S), F32)]
    return pl.pallas_call(
        body, name=name, grid=(PAIRS,), in_specs=in_specs, out_specs=out_specs, out_shape=out_shape,
        scratch_shapes=[pltpu.VMEM((S, qw), F32), pltpu.VMEM((S, LANES), F32)],
        compiler_params=_params(("arbitrary",)),
    )(*ins)


def _tri(lower):
    r = lax.broadcasted_iota(jnp.int32, (QBLK, QBLK), 0)
    c = lax.broadcasted_iota(jnp.int32, (QBLK, QBLK), 1)
    return jnp.where((c <= r) if lower else (c >= r), 1.0, 0.0).astype(F32)


def _hi_dot(a, b):
    return jnp.dot(a, b, precision=lax.Precision.HIGHEST, preferred_element_type=F32)


def fox_gate_fwd(fl, bias, name):
    def body(f_ref, b_ref, o_ref):
        tri = _tri(True)
        carry = jnp.zeros((1, LANES), F32)
        for n in range(S // QBLK):
            x = f_ref[n * QBLK:(n + 1) * QBLK, :].astype(F32) + b_ref[...]
            lf = jnp.minimum(x, 0.0) - jnp.log(1.0 + jnp.exp(-jnp.abs(x)))
            c = _hi_dot(tri, lf) + carry
            o_ref[n * QBLK:(n + 1) * QBLK, :] = c
            carry = c[QBLK - 1:QBLK, :]

    return pl.pallas_call(body, name=name, out_shape=jax.ShapeDtypeStruct((S, LANES), F32),
                          compiler_params=_params())(fl, bias)


def fox_gate_bwd(fl, bias, dcq, dck, name):
    def body(f_ref, b_ref, dq_ref, dk_ref, o_ref, db_ref):
        tri = _tri(False)
        carry = jnp.zeros((1, LANES), F32)
        db = jnp.zeros((1, LANES), F32)
        for n in reversed(range(S // QBLK)):
            rows = slice(n * QBLK, (n + 1) * QBLK)
            dlf = _hi_dot(tri, dq_ref[rows, :] + dk_ref[rows, :]) + carry
            carry = dlf[0:1, :]
            x = f_ref[rows, :].astype(F32) + b_ref[...]
            dx = dlf * (1.0 - _sigmoid(x))
            o_ref[rows, :] = dx
            db = db + jnp.sum(dx, axis=0, keepdims=True)
        db_ref[...] = db

    return pl.pallas_call(body, name=name, out_shape=[jax.ShapeDtypeStruct((S, LANES), F32),
                                                      jax.ShapeDtypeStruct((1, LANES), F32)],
                          compiler_params=_params())(fl, bias, dcq, dck)


def _band_valid(first):
    w = QBLK if first else 2 * QBLK
    i = lax.broadcasted_iota(jnp.int32, (QBLK, w), 0)
    j = lax.broadcasted_iota(jnp.int32, (QBLK, w), 1)
    return (j <= i) if first else ((j >= i) & (j - QBLK <= i))


def _band_q(q_ref, rows, msk):
    return jnp.where(msk, q_ref[rows, :], jnp.zeros((), BF16)) * jnp.asarray(0.125, BF16)


def _band_logits(q_a, kk, bias, first):
    s = lax.dot_general(q_a, kk, _DIMS["nt"], preferred_element_type=F32) + bias
    return jnp.where(_band_valid(first), s, NEG)


def dil_fwd(qkv, bias, g, name):
    d = DIL[g][1]
    ls = S // d
    view = qkv.reshape(ls, d * 9216)

    def body(q_ref, k_ref, v_ref, b_ref, o_ref, l_ref):
        m0, m1 = _pair_masks((QBLK, LANES))
        for n in range(ls // QBLK):
            rows = slice(n * QBLK, (n + 1) * QBLK)
            keys = rows if n == 0 else slice((n - 1) * QBLK, (n + 1) * QBLK)
            os_, ls_ = [], []
            for a, msk in enumerate((m0, m1)):
                q_a = _band_q(q_ref, rows, msk)
                bias_a = b_ref[a, :, QBLK:] if n == 0 else b_ref[a]
                s = _band_logits(q_a, k_ref[keys, :], bias_a, n == 0)
                mx = jnp.max(s, axis=-1, keepdims=True)
                e = jnp.exp(s - mx)
                l = jnp.sum(e, axis=-1, keepdims=True)
                os_.append(jnp.dot((e * (1.0 / l)).astype(BF16), v_ref[keys, :], preferred_element_type=F32))
                ls_.append(mx + jnp.log(l))
            o_ref[rows, :] = jnp.where(m0, os_[0], os_[1])
            l_ref[rows, :] = jnp.where(m0, ls_[0], ls_[1])

    def col(j):
        return lambda h, r: (0, r * 72 + g * 24 + j * 8 + h)

    out = pl.BlockSpec((ls, LANES), lambda h, r: (0, r * 8 + h))
    o, lse = pl.pallas_call(
        body, name=name, grid=(PAIRS, d),
        in_specs=[pl.BlockSpec((ls, LANES), col(0)), pl.BlockSpec((ls, LANES), col(1)), pl.BlockSpec((ls, LANES), col(2)),
                  pl.BlockSpec((2, QBLK, 2 * QBLK), lambda h, r: (h, 0, 0))],
        out_specs=[out, out], out_shape=[jax.ShapeDtypeStruct((ls, d * D), F32)] * 2,
        compiler_params=_params(("arbitrary", "arbitrary")),
    )(view, view, view, bias)
    return o.reshape(S, D), lse.reshape(S, D)


def dil_merge(os_, lses, name):
    def body(o0, o1, o2, l0, l1, l2, o_ref, l_ref):
        ls_ = [l0[...], l1[...], l2[...]]
        mx = jnp.maximum(jnp.maximum(ls_[0], ls_[1]), ls_[2])
        tot = mx + jnp.log(sum(jnp.exp(l - mx) for l in ls_))
        o_ref[...] = sum(jnp.exp(l - tot) * o[...] for l, o in zip(ls_, (o0, o1, o2)))
        l_ref[...] = tot

    return _rows_call(body, name, list(os_) + list(lses), [(D, F32), (D, F32)])


def dil_bwd(qkv, bias, o, lse, do, g, name):
    d = DIL[g][1]
    ls = S // d
    view = qkv.reshape(ls, d * 9216)
    o, lse, do = (t.reshape(ls, d * D) for t in (o, lse, do))

    def body(q_ref, k_ref, v_ref, b_ref, o_ref, l_ref, do_ref, dq_ref, dk_ref, dv_ref, db_ref, dk_acc, dv_acc):
        @pl.when(pl.program_id(1) == 0)
        def _():
            db_ref[...] = jnp.zeros_like(db_ref)

        dk_acc[...] = jnp.zeros_like(dk_acc)
        dv_acc[...] = jnp.zeros_like(dv_acc)
        m0, m1 = _pair_masks((QBLK, LANES))
        for n in range(ls // QBLK):
            rows = slice(n * QBLK, (n + 1) * QBLK)
            keys = rows if n == 0 else slice((n - 1) * QBLK, (n + 1) * QBLK)
            nk = QBLK if n == 0 else 2 * QBLK
            do2, lse2 = do_ref[rows, :], l_ref[rows, :]
            dd = do2 * o_ref[rows, :]
            do_b = do2.astype(BF16)
            mk0, mk1 = _pair_masks((nk, LANES))
            dqs = []
            for a, (msk, mk) in enumerate(((m0, mk0), (m1, mk1))):
                q_a = _band_q(q_ref, rows, msk)
                kk = k_ref[keys, :]
                bias_a = b_ref[a, :, QBLK:] if n == 0 else b_ref[a]
                s = _band_logits(q_a, kk, bias_a, n == 0)
                lse_a = jnp.max(jnp.where(msk, lse2, -jnp.inf), axis=-1, keepdims=True)
                p = jnp.exp(s - lse_a)
                dp = lax.dot_general(jnp.where(msk, do_b, jnp.zeros((), BF16)), v_ref[keys, :], _DIMS["nt"],
                                     preferred_element_type=F32)
                delta = jnp.sum(jnp.where(msk, dd, 0.0), axis=-1, keepdims=True)
                ds = p * (dp - delta)
                if n == 0:
                    db_ref[a, :, QBLK:] += ds
                else:
                    db_ref[a] += ds
                ds_b = ds.astype(BF16)
                dqs.append(jnp.dot(ds_b, kk, preferred_element_type=F32) * 0.125)
                dk_acc[keys, :] += lax.dot_general(ds_b, q_a, _DIMS["tn"], preferred_element_type=F32)
                dv_acc[keys, :] += jnp.where(mk, lax.dot_general(p.astype(BF16), do_b, _DIMS["tn"],
                                                                 preferred_element_type=F32), 0.0)
            dq_ref[rows, :] = jnp.where(m0, dqs[0], dqs[1]).astype(BF16)
        dk_ref[...] = dk_acc[...].astype(BF16)
        dv_ref[...] = dv_acc[...].astype(BF16)

    def col(j):
        return lambda h, r: (0, r * 72 + g * 24 + j * 8 + h)

    nat = pl.BlockSpec((ls, LANES), lambda h, r: (0, r * 8 + h))
    b_spec = pl.BlockSpec((2, QBLK, 2 * QBLK), lambda h, r: (h, 0, 0))
    dq, dk, dv, db = pl.pallas_call(
        body, name=name, grid=(PAIRS, d),
        in_specs=[pl.BlockSpec((ls, LANES), col(0)), pl.BlockSpec((ls, LANES), col(1)), pl.BlockSpec((ls, LANES), col(2)),
                  b_spec, nat, nat, nat],
        out_specs=[nat, nat, nat, b_spec],
        out_shape=[jax.ShapeDtypeStruct((ls, d * D), BF16)] * 3 + [jax.ShapeDtypeStruct((HEADS, QBLK, 2 * QBLK), F32)],
        scratch_shapes=[pltpu.VMEM((ls, LANES), F32), pltpu.VMEM((ls, LANES), F32)],
        compiler_params=_params(("arbitrary", "arbitrary")),
    )(view, view, view, bias, o, lse, do)
    return dq.reshape(S, D), dk.reshape(S, D), dv.reshape(S, D), db


def _place():
    x, y, c = lax.axis_index("x"), lax.axis_index("y"), lax.axis_index("c")
    return x, y, c


def _dev_slot(ref, by_rows, dev):
    return ref.at[:, dev] if by_rows else ref.at[dev]


def all_gather(shards, by_rows, name, in_vmem=False):
    n = len(shards)

    def body(*refs):
        x_refs, out_refs = refs[:n], refs[n:2 * n]
        send_sems, recv_sems, local_sems = refs[2 * n:]
        x, y, c = _place()
        me, sibling = (x, y, c), (x, y, 1 - c)
        chips = [(1 - x, y), (x, 1 - y), (1 - x, 1 - y)]

        def slot(t, px, py, pc):
            return _dev_slot(out_refs[t], by_rows[t], 4 * px + 2 * py + pc)

        def copy(t, k, blk, to, src=None):
            return pltpu.make_async_remote_copy(
                src_ref=slot(t, *blk) if src is None else src, dst_ref=slot(t, *blk), send_sem=send_sems.at[7 * t + k],
                recv_sem=recv_sems.at[7 * t + k], device_id=to, device_id_type=MESH_ID)

        mine = [pltpu.make_async_copy(x_refs[t], slot(t, *me), local_sems.at[t]) for t in range(n)]
        for cp in mine:
            cp.start()
        first = []
        for t in range(n):
            first.append(copy(t, 0, me, sibling, src=x_refs[t]))
            first += [copy(t, 1 + j, me, (*chip, c), src=x_refs[t]) for j, chip in enumerate(chips)]
        for cp in first:
            cp.start()
        passed = []
        for j, chip in enumerate(chips):
            for t in range(n):
                copy(t, 1 + j, (*chip, c), me).wait_recv()
                passed.append(copy(t, 4 + j, (*chip, c), sibling))
                passed[-1].start()
        for t in range(n):
            copy(t, 0, sibling, me).wait_recv()
            for j, chip in enumerate(chips):
                copy(t, 4 + j, (*chip, 1 - c), me).wait_recv()
        for cp in first + passed:
            cp.wait_send()
        for cp in mine:
            cp.wait()

    def gathered(s, rows):
        shp = (s.shape[0], N_DEV) + s.shape[1:] if rows else (N_DEV,) + s.shape
        return jax.ShapeDtypeStruct(shp, s.dtype)

    space = pl.BlockSpec(memory_space=pltpu.VMEM if in_vmem else pl.ANY)
    return pl.pallas_call(
        body, name=name, out_shape=[gathered(s, r) for s, r in zip(shards, by_rows)],
        in_specs=[space] * n, out_specs=[space] * n,
        scratch_shapes=[pltpu.SemaphoreType.DMA((7 * n,)), pltpu.SemaphoreType.DMA((7 * n,)),
                        pltpu.SemaphoreType.DMA((n,))],
        compiler_params=pltpu.CompilerParams(vmem_limit_bytes=VMEM_LIMIT),
    )(*shards)


_HBM = pl.BlockSpec(memory_space=pltpu.HBM)
_SEM = pl.BlockSpec(memory_space=pltpu.SEMAPHORE)
_SPLIT = dict(has_side_effects=pltpu.SideEffectType.DATAFLOW_SIDE_EFFECTING)


def _hbm(a):
    return pltpu.with_memory_space_constraint(a, pltpu.HBM)


def _gathered_shape(s, rows):
    return (s.shape[0], N_DEV) + s.shape[1:] if rows else (N_DEV,) + s.shape


def _peers(x, y, c):
    return [(1 - x if k & 4 else x, 1 - y if k & 2 else y, 1 - c if k & 1 else c) for k in range(1, N_DEV)]


def gather_start(shards, lands, by_rows, name):
    n = len(shards)

    def body(*refs):
        x_refs, land_refs = refs[:n], refs[n:2 * n]
        send_sems, recv_sems = refs[2 * n], refs[2 * n + 1]
        x, y, c = _place()
        me = 4 * x + 2 * y + c
        for t in range(n):
            for k, peer in enumerate(_peers(x, y, c)):
                pltpu.make_async_remote_copy(
                    src_ref=x_refs[t], dst_ref=_dev_slot(land_refs[t], by_rows[t], me), send_sem=send_sems.at[7 * t + k],
                    recv_sem=recv_sems.at[7 * t + k], device_id=peer, device_id_type=MESH_ID).start()

    sems = pltpu.SemaphoreType.DMA((7 * n,))
    res = pl.pallas_call(
        body, name=name,
        out_shape=(sems, sems) + tuple(pltpu.HBM(a.shape, a.dtype) for a in list(shards) + list(lands)),
        in_specs=[_HBM] * (2 * n), out_specs=(_SEM, _SEM) + (_HBM,) * (2 * n),
        input_output_aliases={i: 2 + i for i in range(2 * n)},
        compiler_params=pltpu.CompilerParams(**_SPLIT),
    )(*[_hbm(a) for a in list(shards) + list(lands)])
    return res[0], res[1], list(res[2:2 + n]), list(res[2 + n:])


def gather_wait(send_sems, recv_sems, first, shards, lands, by_rows, after, name):
    n = len(shards)

    def body(*refs):
        x_refs, land_refs = refs[:n], refs[n:2 * n]
        send_sems, recv_sems = refs[2 * n], refs[2 * n + 1]
        x, y, c = _place()
        for t in range(n):
            for k, (px, py, pc) in enumerate(_peers(x, y, c)):
                cp = pltpu.make_async_remote_copy(
                    src_ref=x_refs[t], dst_ref=_dev_slot(land_refs[t], by_rows[t], 4 * px + 2 * py + pc),
                    send_sem=send_sems.at[7 * (first + t) + k], recv_sem=recv_sems.at[7 * (first + t) + k],
                    device_id=(px, py, pc), device_id_type=MESH_ID)
                cp.wait_send()
                cp.wait_recv()

    res = pl.pallas_call(
        body, name=name, out_shape=tuple(pltpu.HBM(a.shape, a.dtype) for a in list(shards) + list(lands)),
        in_specs=[_HBM] * (2 * n) + [_SEM, _SEM, pl.BlockSpec(memory_space=pl.ANY)], out_specs=(_HBM,) * (2 * n),
        input_output_aliases={i: i for i in range(2 * n)},
        compiler_params=pltpu.CompilerParams(**_SPLIT),
    )(*shards, *lands, send_sems, recv_sems, after)
    return list(res[n:])


def scatter_start(srcs, src_l, lands, land_l, by_rows, name):
    n = len(srcs)

    def body(*refs):
        x_refs, land_refs = refs[:n], refs[n:2 * n]
        send_sems, recv_sems, token = refs[2 * n], refs[2 * n + 1], refs[-1]
        x, y, c = _place()
        me = 4 * x + 2 * y + c
        for k, (px, py, pc) in enumerate(_peers(x, y, c)):
            for t in range(n):
                blk = _dev_slot(x_refs[t], by_rows[t], 4 * px + 2 * py + pc)
                pltpu.make_async_remote_copy(
                    src_ref=blk.at[src_l[t]], dst_ref=land_refs[t].at[me, land_l[t]], send_sem=send_sems.at[7 * t + k],
                    recv_sem=recv_sems.at[7 * t + k], device_id=(px, py, pc), device_id_type=MESH_ID).start()
        token[...] = jnp.zeros_like(token)

    lands = [lax.empty((N_DEV, 1) + s.shape[2:], s.dtype) if ld is None else ld for s, ld in zip(srcs, lands)]
    sems = pltpu.SemaphoreType.DMA((7 * n,))
    res = pl.pallas_call(
        body, name=name,
        out_shape=(sems, sems) + tuple(pltpu.HBM(a.shape, a.dtype) for a in list(srcs) + lands)
        + (jax.ShapeDtypeStruct((8, LANES), F32),),
        in_specs=[_HBM] * (2 * n),
        out_specs=(_SEM, _SEM) + (_HBM,) * (2 * n) + (pl.BlockSpec(memory_space=pltpu.VMEM),),
        input_output_aliases={i: 2 + i for i in range(2 * n)},
        compiler_params=pltpu.CompilerParams(**_SPLIT),
    )(*[_hbm(a) for a in list(srcs) + lands])
    return res[0], res[1], list(res[2:2 + n]), list(res[2 + n:2 + 2 * n]), res[-1]


def scatter_wait(send_sems, recv_sems, srcs, src_l, lands, land_l, by_rows, after, name):
    n = len(srcs)

    def body(*refs):
        x_refs, land_refs = refs[:n], refs[n:2 * n]
        send_sems, recv_sems = refs[2 * n], refs[2 * n + 1]
        x, y, c = _place()
        for k, (px, py, pc) in enumerate(_peers(x, y, c)):
            peer = 4 * px + 2 * py + pc
            for t in range(n):
                cp = pltpu.make_async_remote_copy(
                    src_ref=_dev_slot(x_refs[t], by_rows[t], peer).at[src_l[t]], dst_ref=land_refs[t].at[peer, land_l[t]],
                    send_sem=send_sems.at[7 * t + k], recv_sem=recv_sems.at[7 * t + k], device_id=(px, py, pc),
                    device_id_type=MESH_ID)
                cp.wait_send()
                cp.wait_recv()

    res = pl.pallas_call(
        body, name=name, out_shape=tuple(pltpu.HBM(a.shape, a.dtype) for a in list(srcs) + list(lands)),
        in_specs=[_HBM] * (2 * n) + [_SEM, _SEM, pl.BlockSpec(memory_space=pl.ANY)], out_specs=(_HBM,) * (2 * n),
        input_output_aliases={i: i for i in range(2 * n)},
        compiler_params=pltpu.CompilerParams(**_SPLIT),
    )(*srcs, *lands, send_sems, recv_sems, after)
    return list(res[:n]), list(res[n:])


ADAM_BLOCK_BYTES = 3 << 19


def adamw(w, m, v, parts, name):
    n_parts = parts.shape[0]
    n_l, r, c = w.shape
    lane_c = -(-c // LANES) * LANES
    fits = [t for t in range(16, r, 16) if r % t == 0 and t * lane_c * 4 <= ADAM_BLOCK_BYTES]
    tr = max(fits) if fits and r * lane_c * 4 > ADAM_BLOCK_BYTES else r
    c1 = 1.0 / (1.0 - ADAM_B1 ** ADAM_STEP)
    c2 = 1.0 / (1.0 - ADAM_B2 ** ADAM_STEP)

    def body(w_ref, m_ref, v_ref, p_ref, g_ref, d_ref, nm_ref, nv_ref):
        g = p_ref[0].astype(F32)
        for j in range(1, n_parts):
            g = g + p_ref[j].astype(F32)
        nm = ADAM_B1 * m_ref[...] + (1.0 - ADAM_B1) * g
        nv = ADAM_B2 * v_ref[...] + (1.0 - ADAM_B2) * (g * g)
        g_ref[...] = g
        nm_ref[...] = nm
        nv_ref[...] = nv
        d_ref[...] = -ADAM_LR * ((nm * c1) / (jnp.sqrt(nv * c2) + ADAM_EPS) + ADAM_WD * w_ref[...])

    blk = pl.BlockSpec((1, tr, c), lambda l, i: (l, i, 0))
    return pl.pallas_call(
        body, name=name, grid=(n_l, r // tr),
        in_specs=[blk, blk, blk, pl.BlockSpec((n_parts, 1, tr, c), lambda l, i: (0, l, i, 0))],
        out_specs=[blk] * 4, out_shape=[jax.ShapeDtypeStruct((n_l, r, c), F32)] * 4,
        compiler_params=_params(("parallel", "parallel")),
    )(w, m, v, parts)


def sum_parts(parts, name):
    def body(p_ref, o_ref):
        g = p_ref[0]
        for j in range(1, parts.shape[0]):
            g = g + p_ref[j]
        o_ref[...] = g

    return pl.pallas_call(body, name=name, out_shape=jax.ShapeDtypeStruct(parts.shape[1:], F32),
                          compiler_params=_params())(parts)


def _pack(arrs, rows, dtype):
    flat = jnp.concatenate([a.reshape(-1).astype(dtype) for a in arrs])
    return jnp.pad(flat, (0, rows * LANES - flat.shape[0])).reshape(rows, LANES)


def _unpack(packed, shapes):
    flat, out, off = packed.reshape(-1), [], 0
    for shp in shapes:
        n = int(np.prod(shp))
        out.append(flat[off:off + n].reshape(shp))
        off += n
    return out


def _cat(parts):
    return jnp.concatenate(parts, axis=1)


def _layer_list(i):
    kind, j = i % 3, i // 3
    mix = ([("mla_w_a", j), ("mla_w_uq", j), ("mla_w_ukv", j), ("mla_w_o", j)] if kind == 0 else
           [("dil_w_qkv", 0), ("dil_w_o", 0)] if kind == 1 else [("fox_w_qkvf", 0), ("fox_w_o", 0)])
    return mix + [("ffn_w_in", i), ("ffn_w_out", i), ("ple_w_proj", i), ("ple_w_gate", i)]


def _mla_layout(w_a, g_uq, g_ukv, j):
    def z(r, n):
        return jnp.zeros((r, n), BF16)

    wa = w_a[j]
    a = _cat([wa[:, :640], wa[:, 640:656], z(D, 48), wa[:, 656:672], z(D, 48)])
    q, k, v = [], [], []
    for h in range(HEADS):
        b = g_uq[h // 2, j][:, 96 * (h % 2):96 * (h % 2 + 1)]
        q += [b[:, 64:80], b[:, 0:32], z(Q_RANK, 16), b[:, 80:96], b[:, 32:64], z(Q_RANK, 16)]
        b = g_ukv[h // 2, j][:, LANES * (h % 2):LANES * (h % 2 + 1)]
        k += [z(KV_RANK, 16), b[:, 0:32], z(KV_RANK, 32), b[:, 32:64], z(KV_RANK, 16)]
        v.append(b[:, 64:128])
    return a, _cat(q), _cat(k + v)


def _mla_unlayout(d_a, d_uq, d_ukv):
    a = _cat([d_a[:, :640], d_a[:, 640:656], d_a[:, 704:720]])
    uq, ukv = [], []
    for dev in range(N_DEV):
        q, kv = [], []
        for h in (2 * dev, 2 * dev + 1):
            b = d_uq[:, LANES * h:LANES * (h + 1)]
            q += [b[:, 16:48], b[:, 80:112], b[:, 0:16], b[:, 64:80]]
            b = d_ukv[:, LANES * h:LANES * (h + 1)]
            kv += [b[:, 16:48], b[:, 80:112], d_ukv[:, HEADS * LANES + 64 * h:HEADS * LANES + 64 * (h + 1)]]
        uq.append(_cat(q))
        ukv.append(_cat(kv))
    return a, jnp.stack(uq), jnp.stack(ukv)


def _mixer_fwd(kind, tag, hn, W, aux):
    if kind == 0:
        a = mm(hn, W["w_a"], "nn", f"{tag}_a", tn=768)
        cq = rms_fwd(a[:, :Q_RANK], W["q_norm"], f"{tag}_cq", out_dtype=BF16)
        ckv = rms_fwd(a[:, Q_RANK:Q_RANK + KV_RANK], W["kv_norm"], f"{tag}_ckv", out_dtype=BF16)
        qp = mm(cq, W["w_uq"], "nn", f"{tag}_uq", tk=384)
        kvp = mm(ckv, W["w_ukv"], "nn", f"{tag}_ukv", tk=256)
        q, k = mla_qk_fwd(qp, kvp, a[:, 640:], aux["cos"], aux["sin"], f"{tag}_qk")
        v = kvp.astype(BF16)
        o = attn_fwd(q, k, v, f"{tag}_attn", wide=True, scale=96 ** -0.5, v_off=HEADS)
        y = mm(o, W["w_o"], "nn", f"{tag}_o")
        return y, (a, cq, ckv, q, k, v, o)
    if kind == 1:
        qkv = mm(hn, W["w_qkv"], "nn", f"{tag}_qkv", out_dtype=BF16, tn=1152)
        parts = [dil_fwd(qkv, aux["dil_bias"][g], g, f"{tag}_g{g}") for g in range(3)]
        o, lse = dil_merge([p_[0] for p_ in parts], [p_[1] for p_ in parts], f"{tag}_merge")
        y = mm(o, W["w_o"], "nn", f"{tag}_o")
        return y, (qkv, o, lse)
    a = mm(hn, W["w_qkvf"], "nn", f"{tag}_qkvf", tn=640)
    fl = a[:, 3072:]
    cum = fox_gate_fwd(fl, aux["fox_b"], f"{tag}_gate")[:, :HEADS]
    cum_q = cum.reshape(S, PAIRS, 2).transpose(1, 0, 2)
    cum_k = cum.T.reshape(PAIRS, 2, S)
    ab = a.astype(BF16)
    o = attn_fwd(ab, ab, ab, f"{tag}_attn", wide=False, scale=0.125, k_off=PAIRS, v_off=2 * PAIRS, cum=cum_q, cum_t=cum_k)
    y = mm(o, W["w_o"], "nn", f"{tag}_o")
    return y, (fl, ab, cum_q, cum_k, o)


def _mixer_bwd(kind, tag, hn, dy, W, aux, saved):
    gr = {}
    if kind == 0:
        a, cq, ckv, q, k, v, o = saved
        gr["w_o"] = mm(o, dy, "tn", f"{tag}_dwo", out_dtype=BF16)
        do = mm(dy, W["w_o"], "nt", f"{tag}_do")
        dq, dk, dv = attn_bwd(q, k, v, o, do, f"{tag}_attn_b", wide=True, scale=96 ** -0.5, v_off=HEADS)
        dqp, dkr = mla_qk_bwd(dq, dk, aux["cos"], aux["sin"], f"{tag}_qk_b")
        dkvp = jnp.concatenate([dk, dv], axis=1)
        gr["w_ukv"] = mm(ckv, dkvp, "tn", f"{tag}_dwukv", out_dtype=BF16, tm=256)
        dckv = mm(dkvp, W["w_ukv"], "nt", f"{tag}_dckv", tn=256)
        gr["w_uq"] = mm(cq, dqp, "tn", f"{tag}_dwuq", out_dtype=BF16, tm=384)
        dcq = mm(dqp, W["w_uq"], "nt", f"{tag}_dcq", tn=384)
        da_q, gr["q_norm"] = rms_bwd(a[:, :Q_RANK], W["q_norm"], dcq, f"{tag}_cq_b", out_dtype=BF16)
        da_kv, gr["kv_norm"] = rms_bwd(a[:, Q_RANK:Q_RANK + KV_RANK], W["kv_norm"], dckv, f"{tag}_ckv_b",
                                       out_dtype=BF16)
        da = jnp.concatenate([da_q, da_kv, dkr], axis=1)
        gr["w_a"] = mm(hn, da, "tn", f"{tag}_dwa", out_dtype=BF16, tn=768)
        return mm(da, W["w_a"], "nt", f"{tag}_dhn", tk=768), gr
    if kind == 1:
        qkv, o, lse = saved
        gr["w_o"] = mm(o, dy, "tn", f"{tag}_dwo", out_dtype=BF16)
        do = mm(dy, W["w_o"], "nt", f"{tag}_do")
        cols, dbs = [], []
        for g in range(3):
            dq, dk, dv, db = dil_bwd(qkv, aux["dil_bias"][g], o, lse, do, g, f"{tag}_g{g}_b")
            cols += [dq, dk, dv]
            dbs.append(db)
        dqkv = jnp.concatenate(cols, axis=1)
        gr["dil_dbias"] = dbs
        gr["w_qkv"] = mm(hn, dqkv, "tn", f"{tag}_dwqkv", out_dtype=BF16, out_dev=1152, tn=1152)
        return mm(dqkv, W["w_qkv"], "nt", f"{tag}_dhn", tk=1152), gr
    fl, ab, cum_q, cum_k, o = saved
    gr["w_o"] = mm(o, dy, "tn", f"{tag}_dwo", out_dtype=BF16)
    do = mm(dy, W["w_o"], "nt", f"{tag}_do")
    dq, dk, dv, dcq, dck = attn_bwd(ab, ab, ab, o, do, f"{tag}_attn_b", wide=False, scale=0.125, out_dtype=BF16,
                                    k_off=PAIRS, v_off=2 * PAIRS, cum=cum_q, cum_t=cum_k)
    pad = ((0, 0), (0, LANES - HEADS))
    dcq = jnp.pad(dcq.transpose(1, 0, 2).reshape(S, HEADS), pad)
    dck = jnp.pad(dck.reshape(HEADS, S).T, pad)
    dfl, gr["b_f"] = fox_gate_bwd(fl, aux["fox_b"], dcq, dck, f"{tag}_gate_b")
    da = jnp.concatenate([dq, dk, dv, dfl.astype(BF16)], axis=1)
    gr["w_qkvf"] = mm(hn, da, "tn", f"{tag}_dwqkvf", out_dtype=BF16, tn=640)
    return mm(da, W["w_qkvf"], "nt", f"{tag}_dhn", tk=640), gr


def kernel(x, p, positions, norm_g, ffn_w_in, ffn_w_out, ple_w_proj, ple_w_gate, rel_bias, mla_w_a, mla_q_norm, mla_kv_norm, mla_w_uq, mla_w_ukv, mla_w_o, dil_w_qkv, dil_w_o, fox_w_qkvf, fox_b_f, fox_w_o, loss_target, m_norm_g, m_ffn_w_in, m_ffn_w_out, m_ple_w_proj, m_ple_w_gate, m_rel_bias, m_mla_w_a, m_mla_q_norm, m_mla_kv_norm, m_mla_w_uq, m_mla_w_ukv, m_mla_w_o, m_dil_w_qkv, m_dil_w_o, m_fox_w_qkvf, m_fox_b_f, m_fox_w_o, v_norm_g, v_ffn_w_in, v_ffn_w_out, v_ple_w_proj, v_ple_w_gate, v_rel_bias, v_mla_w_a, v_mla_q_norm, v_mla_kv_norm, v_mla_w_uq, v_mla_w_ukv, v_mla_w_o, v_dil_w_qkv, v_dil_w_o, v_fox_w_qkvf, v_fox_b_f, v_fox_w_o):
    given = dict(locals())
    me = 4 * lax.axis_index("x") + 2 * lax.axis_index("y") + lax.axis_index("c")
    for n in TRANSPOSED:
        for pre in ("", "m_", "v_"):
            given[pre + n] = jnp.swapaxes(given[pre + n], 1, 2)

    rows_of = {n: axis == 1 for n, _, axis in BIG}
    rows_of["gains"] = False
    shape_of = {n: shp for n, shp, _ in BIG}
    lists = [_layer_list(i) for i in range(DEPTH)]
    lists[0] = [("gains", 0)] + lists[0]
    flat = [nl for ls in lists for nl in ls]
    flat_rows = [rows_of[n] for n, _ in flat]
    gain_rows = _rows(sum(int(np.prod(s)) for _, s, _ in SMALL_SHARDED))
    shards = [_pack([given[k] for k, _, _ in SMALL_SHARDED], gain_rows, F32)[None] if n == "gains" else
              given[n][l:l + 1].astype(BF16) for n, l in flat]
    lands = [lax.dynamic_update_slice(lax.empty(_gathered_shape(s, r), s.dtype), s[:, None] if r else s[None],
                                      (0, me, 0, 0) if r else (me, 0, 0, 0)) for s, r in zip(shards, flat_rows)]
    send_s, recv_s, shards, lands = gather_start(shards, lands, flat_rows, "gather_start")

    cos, sin = rope_tables(positions.reshape(S, 1), "rope_tables")
    dil_bias = [mm(rel_bias[:, HEADS * g:HEADS * (g + 1)], jnp.asarray(_bucket_onehot(DIL[g][1])), "tn",
                   f"dil_bias{g}", precise=True, tn=4096).reshape(HEADS, QBLK, 2 * QBLK) for g in range(3)]
    aux = {"cos": cos, "sin": sin, "dil_bias": dil_bias,
           "fox_b": jnp.pad(fox_b_f, ((0, 0), (0, LANES - HEADS)))}

    def arrived(i, part, behind):
        n_mix = len(lists[i]) - 4
        first = sum(len(ls) for ls in lists[:i]) + (n_mix if part else 0)
        sl = slice(first, first + (4 if part else n_mix))
        got = gather_wait(send_s, recv_s, first, shards[sl], lands[sl], flat_rows[sl], behind, f"gather_wait{i}_{part}")
        return {n: g.reshape(1, N_DEV * shape_of[n][1], shape_of[n][2]) if rows_of[n] else g
                for (n, _), g in zip(flat[sl], got)}

    full = {}

    def mixer_weights(i, behind):
        kind, j = i % 3, i // 3
        w = arrived(i, 0, behind)
        if i == 0:
            gains, off = w["gains"].reshape(N_DEV, gain_rows * LANES), 0
            for n, shp, axis in SMALL_SHARDED:
                cnt = int(np.prod(shp))
                g = jnp.moveaxis(gains[:, off:off + cnt].reshape((N_DEV,) + shp), 0, axis)
                full[n] = g.reshape(shp[:axis] + (N_DEV * shp[axis],))
                off += cnt
        W = {"g": [full["norm_g"][i, r][None, :] for r in range(4)]}
        if kind == 0:
            w_a, w_uq, w_ukv = _mla_layout(w["mla_w_a"], w["mla_w_uq"], w["mla_w_ukv"], 0)
            W.update(w_a=w_a, w_uq=w_uq, w_ukv=w_ukv, w_o=Lay(w["mla_w_o"], 0),
                     q_norm=full["mla_q_norm"][j][None, :], kv_norm=full["mla_kv_norm"][j][None, :])
        elif kind == 1:
            W.update(w_qkv=Dev(w["dil_w_qkv"], 0), w_o=Lay(w["dil_w_o"], 0))
        else:
            fox_w = jnp.pad(_cat([w["fox_w_qkvf"][dev, 0] for dev in range(N_DEV)]), ((0, 0), (0, FOX_W - 3088)))
            W.update(w_qkvf=fox_w, w_o=Lay(w["fox_w_o"], 0))
        return W

    def ffn_weights(i, behind):
        w = arrived(i, 1, behind)
        return {"w_in_t": Lay(w["ffn_w_in"], 0), "w_out": Lay(w["ffn_w_out"], 0), "w_proj": Dev(w["ple_w_proj"], 0),
                "w_gate": Lay(w["ple_w_gate"], 0)}

    h = x[0]
    saved, weights = [], []
    for i in range(DEPTH):
        kind, j, W = i % 3, i // 3, mixer_weights(i, h)
        weights.append(W)
        t = f"l{i}"
        hn = rms_fwd(h, W["g"][0], f"{t}_n0", out_dtype=BF16)
        y, mix = _mixer_fwd(kind, f"{t}_mix", hn, W, aux)
        W.update(ffn_weights(i, y))
        h1 = rms_fwd(y, W["g"][1], f"{t}_n1", res=h)
        fin = rms_fwd(h1, W["g"][2], f"{t}_n2", out_dtype=BF16)
        gu = mm(fin, W["w_in_t"], "nt", f"{t}_ffn_in")
        act = swiglu_fwd(gu, f"{t}_swiglu")
        f = mm(act, W["w_out"], "nn", f"{t}_ffn_out")
        h2 = rms_fwd(f, W["g"][3], f"{t}_n3", res=h1)
        pp = mm(p[i, 0], W["w_proj"], "nn", f"{t}_ple_p", tn=LANES, tk=256)
        gt = mm(h2, W["w_gate"], "nn", f"{t}_ple_g")
        h3 = ple_fwd(h2, pp, gt, f"{t}_ple")
        saved.append((h, hn, y, h1, fin, gu, act, f, h2, pp, gt, mix))
        h = h3

    dh, loss_lanes = loss_head(h, loss_target[0], "loss_head")

    grads = {n: None for n, _, _ in BIG}
    landed = {n: (lax.empty((N_DEV,) + shp, BF16) if shp[0] > 1 else None) for n, shp, _ in BIG}
    in_flight = []
    g_norm = [[None] * 4 for _ in range(DEPTH)]
    g_qn, g_kvn = [None, None], [None, None]
    g_rel, g_bf = None, None

    def stacked(n):
        g = grads[n]
        return None if g is None else g.reshape(g.shape[0], N_DEV * g.shape[2], g.shape[3])

    def by_device(g):
        return g.reshape(g.shape[0], N_DEV, g.shape[1] // N_DEV, g.shape[2])

    def start(i, entries, mine, tag):
        names = [n for n, _ in entries]
        srcs = [mine[n] if n in mine else grads[n] for n in names]
        src_l = [0 if n in mine else i for n in names]
        land_l = [l if shape_of[n][0] > 1 else 0 for n, l in entries]
        rows = [rows_of[n] for n in names]
        s_sem, r_sem, srcs, got, token = scatter_start(srcs, src_l, [landed[n] for n in names], land_l, rows, tag)
        for n, src, ld in zip(names, srcs, got):
            landed[n] = ld
            if n in mine:
                mine[n] = src
            else:
                grads[n] = src
        in_flight.append((s_sem, r_sem, names, mine, src_l, land_l, rows))
        return token

    token = None
    for i in reversed(range(DEPTH)):
        kind, j, W = i % 3, i // 3, weights[i]
        t = f"l{i}b"
        h0, hn, y, h1, fin, gu, act, f, h2, pp, gt, mix = saved[i]
        dpp, dgt = ple_bwd(dh, pp, gt, f"{t}_ple")
        grads["ple_w_proj"] = mm(p[i, 0], dpp, "tn", f"{t}_dwp", out_dtype=BF16, out_dev=LANES, tm=256, tn=LANES,
                                 stack=(grads["ple_w_proj"], DEPTH, i), after=token)
        grads["ple_w_gate"] = by_device(mm(h2, dgt, "tn", f"{t}_dwg", out_dtype=BF16,
                                           stack=(stacked("ple_w_gate"), DEPTH, i)))
        dh2 = mm(dgt, W["w_gate"], "nt", f"{t}_dh2", add=dh)
        df, g_norm[i][3] = rms_bwd(f, W["g"][3], dh2, f"{t}_n3", out_dtype=BF16)
        grads["ffn_w_out"] = by_device(mm(act, df, "tn", f"{t}_dwout", out_dtype=BF16, tm=1408,
                                          stack=(stacked("ffn_w_out"), DEPTH, i)))
        dact = mm(df, W["w_out"], "nt", f"{t}_dact", tn=1408)
        dgu = swiglu_bwd(gu, dact, f"{t}_swiglu")
        grads["ffn_w_in"] = by_device(mm(dgu, fin, "tn", f"{t}_dwin", out_dtype=BF16, tm=512, tn=1024,
                                         stack=(stacked("ffn_w_in"), DEPTH, i)))
        token = start(i, lists[i][-4:], {}, f"scatter_ffn{i}")
        dfin = mm(dgu, W["w_in_t"], "nn", f"{t}_dfin", after=token, tk=1408)
        dh1, g_norm[i][2] = rms_bwd(h1, W["g"][2], dfin, f"{t}_n2", res=dh2)
        dy, g_norm[i][1] = rms_bwd(y, W["g"][1], dh1, f"{t}_n1", out_dtype=BF16)
        dhn, gr = _mixer_bwd(kind, f"{t}_mix", hn, dy, W, aux, mix)
        if kind == 0:
            d_a, d_uq, d_ukv = _mla_unlayout(gr["w_a"], gr["w_uq"], gr["w_ukv"])
            mine = {"mla_w_a": by_device(d_a[None]), "mla_w_uq": d_uq[:, None], "mla_w_ukv": d_ukv[:, None],
                    "mla_w_o": by_device(gr["w_o"][None])}
            g_qn[j], g_kvn[j] = gr["q_norm"], gr["kv_norm"]
        elif kind == 1:
            mine = {"dil_w_qkv": gr["w_qkv"], "dil_w_o": by_device(gr["w_o"][None])}
            g_rel = jnp.concatenate(
                [mm(jnp.asarray(_bucket_onehot(DIL[g][1])), gr["dil_dbias"][g].reshape(HEADS, -1), "nt",
                    f"{t}_drel{g}", precise=True, tk=4096) for g in range(3)], axis=1)
        else:
            wide = gr["w_qkvf"]
            mine = {"fox_w_qkvf": jnp.stack([wide[:, 386 * dev:386 * (dev + 1)] for dev in range(N_DEV)])[:, None],
                    "fox_w_o": by_device(gr["w_o"][None])}
            g_bf = gr["b_f"][:, :HEADS]
        token = start(i, [e for e in lists[i][:-4] if e[0] in mine], mine, f"scatter_mix{i}")
        dh, g_norm[i][0] = rms_bwd(h0, W["g"][0], dhn, f"{t}_n0", res=dh1, after=token)
    grad_x = dh[None]

    own = {n: [] for n, _, _ in BIG}
    for idx, (s_sem, r_sem, names, mine, src_l, land_l, rows) in enumerate(in_flight):
        srcs = [mine[n] if n in mine else grads[n] for n in names]
        srcs, got = scatter_wait(s_sem, r_sem, srcs, src_l, [landed[n] for n in names], land_l, rows, dh,
                                 f"scatter_wait{idx}")
        for n, src, ld, sl, ll, rw in zip(names, srcs, got, src_l, land_l, rows):
            landed[n] = ld
            if n not in mine:
                grads[n] = src
            blk = lax.dynamic_index_in_dim(src, me, axis=1 if rw else 0, keepdims=False)[sl]
            own[n].append((ll, blk))
    big_out = []
    for n, _, _ in BIG:
        part = landed[n]
        for ll, blk in own[n]:
            part = lax.dynamic_update_slice(part, blk[None, None], (me, ll, 0, 0))
        big_out.append(adamw(given[n], given["m_" + n], given["v_" + n], part, f"adamw_{n}"))

    small_full = [jnp.stack([jnp.concatenate(r, axis=0) for r in g_norm]).reshape(-1),
                  jnp.concatenate(g_qn, axis=0).reshape(-1), jnp.concatenate(g_kvn, axis=0).reshape(-1),
                  g_rel.reshape(-1), g_bf.reshape(-1), loss_lanes.reshape(-1)]
    small_n = sum(a.shape[0] for a in small_full)
    small_rows = _rows(small_n)
    parts, = all_gather([_pack(small_full, small_rows, F32)], [False], "gather_small_grads", in_vmem=True)
    tot = _unpack(sum_parts(parts, "sum_small_grads"), [(4, 4, D), (2, Q_RANK), (2, KV_RANK), (32, 48), (1, 16), (LANES,)])
    loss = jnp.sum(tot[5])
    small_g = [lax.dynamic_slice_in_dim(tot[0], me * 128, 128, axis=2), lax.dynamic_slice_in_dim(tot[1], me * 48, 48, axis=1),
               lax.dynamic_slice_in_dim(tot[2], me * 32, 32, axis=1), tot[3], tot[4]]
    small_names = [n for n, _, _ in SMALL_SHARDED] + [n for n, _ in SMALL_REPL]
    small_shapes = [s for _, s, _ in SMALL_SHARDED] + [s for _, s in SMALL_REPL]
    s_rows = _rows(sum(int(np.prod(s)) for s in small_shapes))
    small_out = adamw(_pack([given[n] for n in small_names], s_rows, F32)[None],
                      _pack([given["m_" + n] for n in small_names], s_rows, F32)[None],
                      _pack([given["v_" + n] for n in small_names], s_rows, F32)[None],
                      _pack(small_g, s_rows, F32)[None, None], "adamw_small")
    small_out = [_unpack(o_, small_shapes) for o_ in small_out]

    res = [{}, {}, {}, {}]
    for k in range(4):
        for idx, (n, _, _) in enumerate(BIG):
            res[k][n] = jnp.swapaxes(big_out[idx][k], 1, 2) if n in TRANSPOSED else big_out[idx][k]
        for idx, n in enumerate(small_names):
            res[k][n] = small_out[k][idx]
    return (loss, grad_x, *[res[0][n] for n in WEIGHTS], *[res[1][n] for n in WEIGHTS],
            *[res[2][n] for n in WEIGHTS], *[res[3][n] for n in WEIGHTS])
```

```python
import math
from typing import NamedTuple

import numpy as np
import jax
import jax.numpy as jnp
from jax import lax
from jax.experimental import pallas as pl
from jax.experimental.pallas import tpu as pltpu

F32 = jnp.float32
BF16 = jnp.bfloat16
MESH_ID = pl.DeviceIdType.MESH

N_DEV = 8
S = 2048
D = 1024
DEPTH = 4
D_FF = 2816
D_PLE = 256
EPS = 1e-6
NEG = -1e30
LANES = 128
HEADS = 16
PAIRS = 8
Q_RANK = 384
KV_RANK = 256
QBLK = 128
DIL = ((128, 1), (512, 4), (2048, 16))
REL_BUCKETS = 32
FOX_W = 3200
VMEM_LIMIT = 56 * 1024 * 1024

ADAM_LR, ADAM_B1, ADAM_B2, ADAM_EPS, ADAM_WD, ADAM_STEP = 1e-3, 0.9, 0.999, 1e-8, 0.01, 10


TRANSPOSED = ("ffn_w_in",)
BIG = (
    ("ffn_w_in", (4, 704, 1024), 1), ("ffn_w_out", (4, 352, 1024), 1),
    ("ple_w_proj", (4, 256, 128), 2), ("ple_w_gate", (4, 128, 1024), 1),
    ("mla_w_a", (2, 128, 672), 1), ("mla_w_uq", (2, 384, 192), 2),
    ("mla_w_ukv", (2, 256, 256), 2), ("mla_w_o", (2, 128, 1024), 1),
    ("dil_w_qkv", (1, 1024, 1152), 2), ("dil_w_o", (1, 128, 1024), 1),
    ("fox_w_qkvf", (1, 1024, 386), 2), ("fox_w_o", (1, 128, 1024), 1),
)
SMALL_SHARDED = (("norm_g", (4, 4, 128), 2), ("mla_q_norm", (2, 48), 1), ("mla_kv_norm", (2, 32), 1))
SMALL_REPL = (("rel_bias", (32, 48)), ("fox_b_f", (1, 16)))
WEIGHTS = ("norm_g", "ffn_w_in", "ffn_w_out", "ple_w_proj", "ple_w_gate", "rel_bias", "mla_w_a", "mla_q_norm",
           "mla_kv_norm", "mla_w_uq", "mla_w_ukv", "mla_w_o", "dil_w_qkv", "dil_w_o", "fox_w_qkvf", "fox_b_f",
           "fox_w_o")


def _rows(n):
    return -(-n // (8 * LANES)) * 8


def _t5_bucket_np(dist):
    max_exact = REL_BUCKETS // 2
    n = np.maximum(dist.astype(np.float32), np.float32(1.0))
    large = max_exact + (np.log(n / np.float32(max_exact)) / np.float32(math.log(2048 / max_exact))
                         * np.float32(REL_BUCKETS - max_exact)).astype(np.int32)
    large = np.minimum(large, REL_BUCKETS - 1)
    return np.where(dist < max_exact, dist, large)


def _bucket_onehot(dilation):
    i = np.arange(QBLK)[:, None]
    j = np.arange(2 * QBLK)[None, :]
    bucket = _t5_bucket_np(np.clip(QBLK + i - j, 0, None) * dilation).reshape(-1)
    return (np.arange(REL_BUCKETS)[:, None] == bucket[None, :]).astype(np.float32)


def _rope_inv_lanes():
    half = 16
    inv = (np.float32(10000.0) ** (-np.arange(half, dtype=np.float32) / np.float32(half))).astype(np.float32)
    t = np.zeros((1, LANES), np.float32)
    t[0, 0:16] = inv
    t[0, 64:80] = inv
    return t


def _params(sem=None):
    return pltpu.CompilerParams(dimension_semantics=sem, vmem_limit_bytes=VMEM_LIMIT)


def _tile(dim, target):
    if dim <= target or dim % target == 0:
        return min(dim, target)
    t = (target // LANES) * LANES
    while dim % t:
        t -= LANES
    return t


_DIMS = {"nn": (((1,), (0,)), ((), ())), "nt": (((1,), (1,)), ((), ())), "tn": (((0,), (0,)), ((), ()))}


class Lay(NamedTuple):
    arr: jax.Array
    l: int


class Dev(NamedTuple):
    arr: jax.Array
    l: int


def _lshape(op):
    if isinstance(op, Dev):
        g, _, r, w = op.arr.shape
        return r, g * w
    return op.arr.shape[1:] if isinstance(op, Lay) else op.shape


def _op_spec(op, rows_t, cols_t, row_ix, col_ix):
    if isinstance(op, Dev):
        w = op.arr.shape[3]
        assert w % cols_t == 0 and (cols_t % LANES == 0 or cols_t == w), (w, cols_t)
        nb, l = w // cols_t, op.l
        return pl.BlockSpec((1, 1, rows_t, cols_t),
                            lambda i, j, k: (col_ix(i, j, k) // nb, l, row_ix(i, j, k), col_ix(i, j, k) % nb))
    if isinstance(op, Lay):
        l = op.l
        return pl.BlockSpec((1, rows_t, cols_t), lambda i, j, k: (l, row_ix(i, j, k), col_ix(i, j, k)))
    return pl.BlockSpec((rows_t, cols_t), lambda i, j, k: (row_ix(i, j, k), col_ix(i, j, k)))


def _mat(ref):
    return ref[(0,) * (len(ref.shape) - 2)]


def mm(a, b, mode, name, out_dtype=F32, precise=False, add=None, out_dev=None, stack=None, after=None,
       tm=1024, tn=512, tk=2048):
    (ar, ac), (br, bc) = _lshape(a), _lshape(b)
    M, K = (ac, ar) if mode == "tn" else (ar, ac)
    N = br if mode == "nt" else bc
    assert K == (bc if mode == "nt" else br)
    tm, tn, tk = _tile(M, tm), _tile(N, tn), _tile(K, tk)
    nk = K // tk
    ix_i, ix_j, ix_k = (lambda i, j, k: i), (lambda i, j, k: j), (lambda i, j, k: k)
    a_spec = _op_spec(a, tk, tm, ix_k, ix_i) if mode == "tn" else _op_spec(a, tm, tk, ix_i, ix_k)
    b_spec = _op_spec(b, tn, tk, ix_j, ix_k) if mode == "nt" else _op_spec(b, tk, tn, ix_k, ix_j)
    buf, n_l, l = stack if stack is not None else (None, 1, 0)
    if out_dev is not None:
        out = Dev(jax.ShapeDtypeStruct((N // out_dev, n_l, M, out_dev), out_dtype), l)
    elif stack is not None:
        out = Lay(jax.ShapeDtypeStruct((n_l, M, N), out_dtype), l)
    else:
        out = jax.ShapeDtypeStruct((M, N), out_dtype)
    o_spec = _op_spec(out, tm, tn, ix_i, ix_j)
    n_in = 3 if add is not None else 2

    def body(*refs):
        a_ref, b_ref = refs[0], refs[1]
        o_ref = refs[n_in + (buf is not None) + (after is not None)]
        if precise:
            part = lax.dot_general(_mat(a_ref), _mat(b_ref), _DIMS[mode], precision=lax.Precision.HIGHEST,
                                   preferred_element_type=F32)
        else:
            part = lax.dot_general(_mat(a_ref).astype(BF16), _mat(b_ref).astype(BF16), _DIMS[mode],
                                   preferred_element_type=F32)

        def finish(r):
            r = r + refs[2][...] if add is not None else r
            o_ref[...] = r.astype(o_ref.dtype).reshape(o_ref.shape)

        if nk == 1:
            finish(part)
            return
        acc, k = refs[-1], pl.program_id(2)

        @pl.when(k == 0)
        def _():
            acc[...] = part

        @pl.when(k > 0)
        def _():
            acc[...] += part

        @pl.when(k == nk - 1)
        def _():
            finish(acc[...])

    ins = [getattr(a, "arr", a), getattr(b, "arr", b)] + ([add] if add is not None else [])
    in_specs = [a_spec, b_spec] + ([o_spec] if add is not None else [])
    aliases = {}
    if buf is not None:
        ins.append(buf)
        in_specs.append(pl.BlockSpec(memory_space=pl.ANY))
        aliases = {n_in: 0}
    if after is not None:
        ins.append(after)
        in_specs.append(pl.BlockSpec(memory_space=pl.ANY))
    return pl.pallas_call(
        body, name=name, grid=(M // tm, N // tn, nk), in_specs=in_specs, out_specs=o_spec,
        out_shape=getattr(out, "arr", out), input_output_aliases=aliases,
        scratch_shapes=[pltpu.VMEM((tm, tn), F32)] if nk > 1 else [],
        compiler_params=_params(("parallel", "parallel", "arbitrary")),
    )(*ins)


def _rows_call(body, name, ins, outs, tr=512, acc_outs=()):
    n = ins[0].shape[0]
    tr = min(tr, n)
    in_specs = [pl.BlockSpec((tr, a.shape[1]), lambda i: (i, 0)) if a.shape[0] == n else
                pl.BlockSpec(a.shape, lambda i: (0, 0)) for a in ins]
    out_specs = [pl.BlockSpec((tr, w), lambda i: (i, 0)) for w, _ in outs] + \
                [pl.BlockSpec((1, w), lambda i: (0, 0)) for w in acc_outs]
    out_shape = [jax.ShapeDtypeStruct((n, w), dt) for w, dt in outs] + \
                [jax.ShapeDtypeStruct((1, w), F32) for w in acc_outs]
    res = pl.pallas_call(body, name=name, grid=(n // tr,), in_specs=in_specs, out_specs=out_specs,
                         out_shape=out_shape, compiler_params=_params(("arbitrary",)))(*ins)
    return res[0] if len(res) == 1 else res


def _acc(ref, val):
    @pl.when(pl.program_id(0) == 0)
    def _():
        ref[...] = jnp.zeros_like(ref)

    ref[...] += val


def rms_fwd(x, g, name, res=None, out_dtype=F32):
    def body(*refs):
        x_ref, g_ref = refs[0], refs[1]
        o_ref = refs[-1]
        xv = x_ref[...]
        y = xv * lax.rsqrt(jnp.mean(xv * xv, axis=-1, keepdims=True) + EPS) * g_ref[...]
        o_ref[...] = (y + refs[2][...] if res is not None else y).astype(o_ref.dtype)

    ins = [x, g] + ([res] if res is not None else [])
    return _rows_call(body, name, ins, [(x.shape[1], out_dtype)])


def rms_bwd(x, g, dy, name, res=None, out_dtype=F32, after=None):
    def body(*refs):
        x_ref, g_ref, dy_ref = refs[:3]
        dx_ref, dg_ref = refs[-2], refs[-1]
        xv, dyv = x_ref[...], dy_ref[...]
        r = lax.rsqrt(jnp.mean(xv * xv, axis=-1, keepdims=True) + EPS)
        xh = xv * r
        dxh = dyv * g_ref[...]
        dx = r * (dxh - xh * jnp.mean(dxh * xh, axis=-1, keepdims=True))
        dx_ref[...] = (dx + refs[3][...] if res is not None else dx).astype(dx_ref.dtype)
        _acc(dg_ref, jnp.sum(dyv * xh, axis=0, keepdims=True))

    ins = [x, g, dy] + ([res] if res is not None else []) + ([after] if after is not None else [])
    return _rows_call(body, name, ins, [(x.shape[1], out_dtype)], acc_outs=(x.shape[1],))


def _sigmoid(x):
    return 0.5 * jnp.tanh(0.5 * x) + 0.5


def swiglu_fwd(gu, name):
    def body(gu_ref, o_ref):
        gate = gu_ref[:, :D_FF]
        o_ref[...] = (gate * _sigmoid(gate) * gu_ref[:, D_FF:]).astype(BF16)

    return _rows_call(body, name, [gu], [(D_FF, BF16)], tr=256)


def swiglu_bwd(gu, dact, name):
    def body(gu_ref, d_ref, o_ref):
        gate, d = gu_ref[:, :D_FF], d_ref[...]
        sg = _sigmoid(gate)
        o_ref[:, :D_FF] = (d * gu_ref[:, D_FF:] * sg * (1.0 + gate * (1.0 - sg))).astype(BF16)
        o_ref[:, D_FF:] = (d * gate * sg).astype(BF16)

    return _rows_call(body, name, [gu, dact], [(2 * D_FF, BF16)], tr=256)


def ple_fwd(h, pp, gt, name):
    def body(h_ref, p_ref, g_ref, o_ref):
        o_ref[...] = h_ref[...] + p_ref[...] * _sigmoid(g_ref[...])

    return _rows_call(body, name, [h, pp, gt], [(D, F32)])


def ple_bwd(dh, pp, gt, name):
    def body(d_ref, p_ref, g_ref, dp_ref, dg_ref):
        d, sg = d_ref[...], _sigmoid(g_ref[...])
        dp_ref[...] = (d * sg).astype(BF16)
        dg_ref[...] = (d * p_ref[...] * sg * (1.0 - sg)).astype(BF16)

    return _rows_call(body, name, [dh, pp, gt], [(D, BF16), (D, BF16)])


def loss_head(y, target, name):
    def body(y_ref, t_ref, d_ref, l_ref):
        e = y_ref[...] - t_ref[...]
        d_ref[...] = e * (1.0 / D)
        col = jnp.sum(e * e, axis=0, keepdims=True) * (0.5 / D)
        _acc(l_ref, sum(col[:, LANES * c:LANES * (c + 1)] for c in range(D // LANES)))

    return _rows_call(body, name, [y, target], [(D, F32)], acc_outs=(LANES,))


def rope_tables(pos_col, name):
    inv = jnp.asarray(_rope_inv_lanes())

    def body(p_ref, inv_ref, c_ref, s_ref):
        ang = p_ref[...].astype(F32) * inv_ref[...]
        lane = lax.broadcasted_iota(jnp.int32, ang.shape, 1)
        first, second = lane < 16, (lane >= 64) & (lane < 80)
        c_ref[...] = jnp.where(first | second, jnp.cos(ang), 1.0)
        sn = jnp.sin(ang)
        s_ref[...] = jnp.where(first, -sn, jnp.where(second, sn, 0.0))

    return _rows_call(body, name, [pos_col, inv], [(LANES, F32), (LANES, F32)])


def _rope(x, c, s):
    return x * c + pltpu.roll(x, 64, axis=1) * s


def _rope_t(d, c, s):
    return d * c + pltpu.roll(d * s, 64, axis=1)


def mla_qk_fwd(qp, kvp, kr, cos, sin, name):
    def body(q_ref, k_ref, kr_ref, c_ref, s_ref, qo_ref, ko_ref):
        c, s = c_ref[...], s_ref[...]
        kr_rot = _rope(kr_ref[...], c, s)
        for h in range(HEADS):
            sl = slice(LANES * h, LANES * (h + 1))
            qo_ref[:, sl] = _rope(q_ref[:, sl], c, s).astype(BF16)
            ko_ref[:, sl] = (k_ref[:, sl] + kr_rot).astype(BF16)

    n = qp.shape[0]
    tr = 256
    w = HEADS * LANES
    return pl.pallas_call(
        body, name=name, grid=(n // tr,),
        in_specs=[pl.BlockSpec((tr, w), lambda i: (i, 0)), pl.BlockSpec((tr, w), lambda i: (i, 0)),
                  pl.BlockSpec((tr, LANES), lambda i: (i, 0)), pl.BlockSpec((tr, LANES), lambda i: (i, 0)),
                  pl.BlockSpec((tr, LANES), lambda i: (i, 0))],
        out_specs=[pl.BlockSpec((tr, w), lambda i: (i, 0))] * 2,
        out_shape=[jax.ShapeDtypeStruct((n, w), BF16)] * 2, compiler_params=_params(("arbitrary",)),
    )(qp, kvp, kr, cos, sin)


def mla_qk_bwd(dq, dk, cos, sin, name):
    def body(dq_ref, dk_ref, c_ref, s_ref, dqp_ref, dkr_ref):
        c, s = c_ref[...], s_ref[...]
        tot = jnp.zeros(c.shape, F32)
        for h in range(HEADS):
            sl = slice(LANES * h, LANES * (h + 1))
            dqp_ref[:, sl] = _rope_t(dq_ref[:, sl], c, s).astype(BF16)
            tot = tot + dk_ref[:, sl]
        dkr_ref[...] = _rope_t(tot, c, s).astype(BF16)

    return _rows_call(body, name, [dq, dk, cos, sin], [(HEADS * LANES, BF16), (LANES, BF16)])


TQ = 256


def _pair_masks(shape):
    lane = lax.broadcasted_iota(jnp.int32, shape, 1)
    return (lane < 64, lane >= 64)


def _scaled_q(q_a, scale):
    return (q_a * jnp.asarray(scale, q_a.dtype), None) if scale == 0.125 else (q_a, scale)


def _causal_probs(q_a, k_a, scale, b0, cq, ck):
    s = lax.dot_general(q_a, k_a, _DIMS["nt"], preferred_element_type=F32)
    if scale is not None:
        s = s * scale
    if cq is not None:
        s = s + (cq - ck)
    row = lax.broadcasted_iota(jnp.int32, (TQ, TQ), 0)
    col = lax.broadcasted_iota(jnp.int32, (TQ, TQ), 1)
    diag = jnp.where(col <= row, s[:, b0:], NEG)
    s = diag if b0 == 0 else jnp.concatenate([s[:, :b0], diag], axis=1)
    e = jnp.exp(s - jnp.max(s, axis=-1, keepdims=True))
    return e * (1.0 / jnp.sum(e, axis=-1, keepdims=True))


def attn_fwd(q, k, v, name, *, wide, scale, q_off=0, k_off=0, v_off=0, cum=None, cum_t=None):
    qw = 2 * LANES if wide else LANES
    forget = cum is not None

    def body(*refs):
        q_ref, k_ref, v_ref = refs[:3]
        o_ref = refs[-1]
        m0, m1 = _pair_masks((TQ, LANES))
        for qi in range(S // TQ):
            b0, b1 = qi * TQ, (qi + 1) * TQ
            outs = []
            for a, msk in enumerate((m0, m1)):
                if wide:
                    q_a, k_a = q_ref[b0:b1, LANES * a:LANES * (a + 1)], k_ref[:b1, LANES * a:LANES * (a + 1)]
                else:
                    q_a, k_a = jnp.where(msk, q_ref[b0:b1, :], jnp.zeros((), BF16)), k_ref[:b1, :]
                cq = refs[3][0, b0:b1, a:a + 1] if forget else None
                ck = refs[4][0, a:a + 1, :b1] if forget else None
                q_a, left = _scaled_q(q_a, scale)
                p = _causal_probs(q_a, k_a, left, b0, cq, ck)
                outs.append(jnp.dot(p.astype(BF16), v_ref[:b1, :], preferred_element_type=F32))
            o_ref[b0:b1, :] = jnp.where(m0, outs[0], outs[1])

    in_specs = [pl.BlockSpec((S, qw), lambda h: (0, q_off * LANES // qw + h)),
                pl.BlockSpec((S, qw), lambda h: (0, k_off * LANES // qw + h)),
                pl.BlockSpec((S, LANES), lambda h: (0, v_off + h))]
    ins = [q, k, v]
    if forget:
        in_specs += [pl.BlockSpec((1, S, 2), lambda h: (h, 0, 0)), pl.BlockSpec((1, 2, S), lambda h: (h, 0, 0))]
        ins += [cum, cum_t]
    return pl.pallas_call(
        body, name=name, grid=(PAIRS,), in_specs=in_specs, out_specs=pl.BlockSpec((S, LANES), lambda h: (0, h)),
        out_shape=jax.ShapeDtypeStruct((S, PAIRS * LANES), F32), compiler_params=_params(("arbitrary",)),
    )(*ins)


def attn_bwd(q, k, v, o, do, name, *, wide, scale, out_dtype=F32, q_off=0, k_off=0, v_off=0, cum=None, cum_t=None):
    qw = 2 * LANES if wide else LANES
    forget = cum is not None

    def body(*refs):
        q_ref, k_ref, v_ref, o_ref, do_ref = refs[:5]
        n_out = 5 if forget else 3
        outs = refs[-(n_out + 2):-2]
        dq_ref, dk_ref, dv_ref = outs[:3]
        dk_acc, dv_acc = refs[-2], refs[-1]
        dk_acc[...] = jnp.zeros_like(dk_acc)
        dv_acc[...] = jnp.zeros_like(dv_acc)
        if forget:
            dcq_ref, dck_ref = outs[3], outs[4]
            dck_ref[...] = jnp.zeros_like(dck_ref)
        m0, m1 = _pair_masks((TQ, LANES))
        for qi in range(S // TQ):
            b0, b1 = qi * TQ, (qi + 1) * TQ
            do2 = do_ref[b0:b1, :]
            dd = do2 * o_ref[b0:b1, :]
            do_b = do2.astype(BF16)
            mk0, mk1 = _pair_masks((b1, LANES))
            dqs = []
            for a, (msk, mk) in enumerate(((m0, mk0), (m1, mk1))):
                lanes = slice(LANES * a, LANES * (a + 1)) if wide else slice(0, LANES)
                if wide:
                    q_a, k_a = q_ref[b0:b1, lanes], k_ref[:b1, lanes]
                else:
                    q_a, k_a = jnp.where(msk, q_ref[b0:b1, :], jnp.zeros((), BF16)), k_ref[:b1, :]
                cq = refs[5][0, b0:b1, a:a + 1] if forget else None
                ck = refs[6][0, a:a + 1, :b1] if forget else None
                q_a, left = _scaled_q(q_a, scale)
                p = _causal_probs(q_a, k_a, left, b0, cq, ck)
                dp = lax.dot_general(jnp.where(msk, do_b, jnp.zeros((), BF16)), v_ref[:b1, :], _DIMS["nt"],
                                     preferred_element_type=F32)
                delta = jnp.sum(jnp.where(msk, dd, 0.0), axis=-1, keepdims=True)
                ds = p * (dp - delta)
                if forget:
                    dcq_ref[0, b0:b1, a:a + 1] = jnp.sum(ds, axis=-1, keepdims=True)
                    dck_ref[0, a:a + 1, :b1] -= jnp.sum(ds, axis=0, keepdims=True)
                ds_b = ds.astype(BF16)
                dqs.append(jnp.dot(ds_b, k_a, preferred_element_type=F32) * scale)
                dk_a = lax.dot_general(ds_b, q_a, _DIMS["tn"], preferred_element_type=F32)
                dk_acc[:b1, lanes] += dk_a if left is None else dk_a * scale
                dv_acc[:b1, :] += jnp.where(mk, lax.dot_general(p.astype(BF16), do_b, _DIMS["tn"],
                                                                 preferred_element_type=F32), 0.0)
            if wide:
                dq_ref[b0:b1, :LANES] = dqs[0].astype(out_dtype)
                dq_ref[b0:b1, LANES:] = dqs[1].astype(out_dtype)
            else:
                dq_ref[b0:b1, :] = jnp.where(m0, dqs[0], dqs[1]).astype(out_dtype)
        dk_ref[...] = dk_acc[...].astype(out_dtype)
        dv_ref[...] = dv_acc[...].astype(out_dtype)

    pair = pl.BlockSpec((S, LANES), lambda h: (0, h))
    qk_out = pl.BlockSpec((S, qw), lambda h: (0, h))
    in_specs = [pl.BlockSpec((S, qw), lambda h: (0, q_off * LANES // qw + h)),
                pl.BlockSpec((S, qw), lambda h: (0, k_off * LANES // qw + h)),
                pl.BlockSpec((S, LANES), lambda h: (0, v_off + h)), pair, pair]
    ins = [q, k, v, o, do]
    out_specs = [qk_out, qk_out, pair]
    out_shape = [jax.ShapeDtypeStruct((S, PAIRS * qw), out_dtype)] * 2 + \
                [jax.ShapeDtypeStruct((S, PAIRS * LANES), out_dtype)]
    if forget:
        by_q, by_k = pl.BlockSpec((1, S, 2), lambda h: (h, 0, 0)), pl.BlockSpec((1, 2, S), lambda h: (h, 0, 0))
        in_specs += [by_q, by_k]
        ins += [cum, cum_t]
        out_specs += [by_q, by_k]
        out_shape += [jax.ShapeDtypeStruct((PAIRS, S, 2), F32), jax.ShapeDtypeStruct((PAIRS, 2, S), F32)]
    return pl.pallas_call(
        body, name=name, grid=(PAIRS,), in_specs=in_specs, out_specs=out_specs, out_shape=out_shape,
        scratch_shapes=[pltpu.VMEM((S, qw), F32), pltpu.VMEM((S, LANES), F32)],
        compiler_params=_params(("arbitrary",)),
    )(*ins)


def _tri(lower):
    r = lax.broadcasted_iota(jnp.int32, (QBLK, QBLK), 0)
    c = lax.broadcasted_iota(jnp.int32, (QBLK, QBLK), 1)
    return jnp.where((c <= r) if lower else (c >= r), 1.0, 0.0).astype(F32)


def _hi_dot(a, b):
    return jnp.dot(a, b, precision=lax.Precision.HIGHEST, preferred_element_type=F32)


def fox_gate_fwd(fl, bias, name):
    def body(f_ref, b_ref, o_ref):
        tri = _tri(True)
        carry = jnp.zeros((1, LANES), F32)
        for n in range(S // QBLK):
            x = f_ref[n * QBLK:(n + 1) * QBLK, :].astype(F32) + b_ref[...]
            lf = jnp.minimum(x, 0.0) - jnp.log(1.0 + jnp.exp(-jnp.abs(x)))
            c = _hi_dot(tri, lf) + carry
            o_ref[n * QBLK:(n + 1) * QBLK, :] = c
            carry = c[QBLK - 1:QBLK, :]

    return pl.pallas_call(body, name=name, out_shape=jax.ShapeDtypeStruct((S, LANES), F32),
                          compiler_params=_params())(fl, bias)


def fox_gate_bwd(fl, bias, dcq, dck, name):
    def body(f_ref, b_ref, dq_ref, dk_ref, o_ref, db_ref):
        tri = _tri(False)
        carry = jnp.zeros((1, LANES), F32)
        db = jnp.zeros((1, LANES), F32)
        for n in reversed(range(S // QBLK)):
            rows = slice(n * QBLK, (n + 1) * QBLK)
            dlf = _hi_dot(tri, dq_ref[rows, :] + dk_ref[rows, :]) + carry
            carry = dlf[0:1, :]
            x = f_ref[rows, :].astype(F32) + b_ref[...]
            dx = dlf * (1.0 - _sigmoid(x))
            o_ref[rows, :] = dx
            db = db + jnp.sum(dx, axis=0, keepdims=True)
        db_ref[...] = db

    return pl.pallas_call(body, name=name, out_shape=[jax.ShapeDtypeStruct((S, LANES), F32),
                                                      jax.ShapeDtypeStruct((1, LANES), F32)],
                          compiler_params=_params())(fl, bias, dcq, dck)


def _band_valid(first):
    w = QBLK if first else 2 * QBLK
    i = lax.broadcasted_iota(jnp.int32, (QBLK, w), 0)
    j = lax.broadcasted_iota(jnp.int32, (QBLK, w), 1)
    return (j <= i) if first else ((j >= i) & (j - QBLK <= i))


def _band_q(q_ref, rows, msk):
    return jnp.where(msk, q_ref[rows, :], jnp.zeros((), BF16)) * jnp.asarray(0.125, BF16)


def _band_logits(q_a, kk, bias, first):
    s = lax.dot_general(q_a, kk, _DIMS["nt"], preferred_element_type=F32) + bias
    return jnp.where(_band_valid(first), s, NEG)


def dil_fwd(qkv, bias, g, name):
    d = DIL[g][1]
    ls = S // d
    view = qkv.reshape(ls, d * 9216)

    def body(q_ref, k_ref, v_ref, b_ref, o_ref, l_ref):
        m0, m1 = _pair_masks((QBLK, LANES))
        for n in range(ls // QBLK):
            rows = slice(n * QBLK, (n + 1) * QBLK)
            keys = rows if n == 0 else slice((n - 1) * QBLK, (n + 1) * QBLK)
            os_, ls_ = [], []
            for a, msk in enumerate((m0, m1)):
                q_a = _band_q(q_ref, rows, msk)
                bias_a = b_ref[a, :, QBLK:] if n == 0 else b_ref[a]
                s = _band_logits(q_a, k_ref[keys, :], bias_a, n == 0)
                mx = jnp.max(s, axis=-1, keepdims=True)
                e = jnp.exp(s - mx)
                l = jnp.sum(e, axis=-1, keepdims=True)
                os_.append(jnp.dot((e * (1.0 / l)).astype(BF16), v_ref[keys, :], preferred_element_type=F32))
                ls_.append(mx + jnp.log(l))
            o_ref[rows, :] = jnp.where(m0, os_[0], os_[1])
            l_ref[rows, :] = jnp.where(m0, ls_[0], ls_[1])

    def col(j):
        return lambda h, r: (0, r * 72 + g * 24 + j * 8 + h)

    out = pl.BlockSpec((ls, LANES), lambda h, r: (0, r * 8 + h))
    o, lse = pl.pallas_call(
        body, name=name, grid=(PAIRS, d),
        in_specs=[pl.BlockSpec((ls, LANES), col(0)), pl.BlockSpec((ls, LANES), col(1)), pl.BlockSpec((ls, LANES), col(2)),
                  pl.BlockSpec((2, QBLK, 2 * QBLK), lambda h, r: (h, 0, 0))],
        out_specs=[out, out], out_shape=[jax.ShapeDtypeStruct((ls, d * D), F32)] * 2,
        compiler_params=_params(("arbitrary", "arbitrary")),
    )(view, view, view, bias)
    return o.reshape(S, D), lse.reshape(S, D)


def dil_merge(os_, lses, name):
    def body(o0, o1, o2, l0, l1, l2, o_ref, l_ref):
        ls_ = [l0[...], l1[...], l2[...]]
        mx = jnp.maximum(jnp.maximum(ls_[0], ls_[1]), ls_[2])
        tot = mx + jnp.log(sum(jnp.exp(l - mx) for l in ls_))
        o_ref[...] = sum(jnp.exp(l - tot) * o[...] for l, o in zip(ls_, (o0, o1, o2)))
        l_ref[...] = tot

    return _rows_call(body, name, list(os_) + list(lses), [(D, F32), (D, F32)])


def dil_bwd(qkv, bias, o, lse, do, g, name):
    d = DIL[g][1]
    ls = S // d
    view = qkv.reshape(ls, d * 9216)
    o, lse, do = (t.reshape(ls, d * D) for t in (o, lse, do))

    def body(q_ref, k_ref, v_ref, b_ref, o_ref, l_ref, do_ref, dq_ref, dk_ref, dv_ref, db_ref, dk_acc, dv_acc):
        @pl.when(pl.program_id(1) == 0)
        def _():
            db_ref[...] = jnp.zeros_like(db_ref)

        dk_acc[...] = jnp.zeros_like(dk_acc)
        dv_acc[...] = jnp.zeros_like(dv_acc)
        m0, m1 = _pair_masks((QBLK, LANES))
        for n in range(ls // QBLK):
            rows = slice(n * QBLK, (n + 1) * QBLK)
            keys = rows if n == 0 else slice((n - 1) * QBLK, (n + 1) * QBLK)
            nk = QBLK if n == 0 else 2 * QBLK
            do2, lse2 = do_ref[rows, :], l_ref[rows, :]
            dd = do2 * o_ref[rows, :]
            do_b = do2.astype(BF16)
            mk0, mk1 = _pair_masks((nk, LANES))
            dqs = []
            for a, (msk, mk) in enumerate(((m0, mk0), (m1, mk1))):
                q_a = _band_q(q_ref, rows, msk)
                kk = k_ref[keys, :]
                bias_a = b_ref[a, :, QBLK:] if n == 0 else b_ref[a]
                s = _band_logits(q_a, kk, bias_a, n == 0)
                lse_a = jnp.max(jnp.where(msk, lse2, -jnp.inf), axis=-1, keepdims=True)
                p = jnp.exp(s - lse_a)
                dp = lax.dot_general(jnp.where(msk, do_b, jnp.zeros((), BF16)), v_ref[keys, :], _DIMS["nt"],
                                     preferred_element_type=F32)
                delta = jnp.sum(jnp.where(msk, dd, 0.0), axis=-1, keepdims=True)
                ds = p * (dp - delta)
                if n == 0:
                    db_ref[a, :, QBLK:] += ds
                else:
                    db_ref[a] += ds
                ds_b = ds.astype(BF16)
                dqs.append(jnp.dot(ds_b, kk, preferred_element_type=F32) * 0.125)
                dk_acc[keys, :] += lax.dot_general(ds_b, q_a, _DIMS["tn"], preferred_element_type=F32)
                dv_acc[keys, :] += jnp.where(mk, lax.dot_general(p.astype(BF16), do_b, _DIMS["tn"],
                                                                 preferred_element_type=F32), 0.0)
            dq_ref[rows, :] = jnp.where(m0, dqs[0], dqs[1]).astype(BF16)
        dk_ref[...] = dk_acc[...].astype(BF16)
        dv_ref[...] = dv_acc[...].astype(BF16)

    def col(j):
        return lambda h, r: (0, r * 72 + g * 24 + j * 8 + h)

    nat = pl.BlockSpec((ls, LANES), lambda h, r: (0, r * 8 + h))
    b_spec = pl.BlockSpec((2, QBLK, 2 * QBLK), lambda h, r: (h, 0, 0))
    dq, dk, dv, db = pl.pallas_call(
        body, name=name, grid=(PAIRS, d),
        in_specs=[pl.BlockSpec((ls, LANES), col(0)), pl.BlockSpec((ls, LANES), col(1)), pl.BlockSpec((ls, LANES), col(2)),
                  b_spec, nat, nat, nat],
        out_specs=[nat, nat, nat, b_spec],
        out_shape=[jax.ShapeDtypeStruct((ls, d * D), BF16)] * 3 + [jax.ShapeDtypeStruct((HEADS, QBLK, 2 * QBLK), F32)],
        scratch_shapes=[pltpu.VMEM((ls, LANES), F32), pltpu.VMEM((ls, LANES), F32)],
        compiler_params=_params(("arbitrary", "arbitrary")),
    )(view, view, view, bias, o, lse, do)
    return dq.reshape(S, D), dk.reshape(S, D), dv.reshape(S, D), db


def _place():
    x, y, c = lax.axis_index("x"), lax.axis_index("y"), lax.axis_index("c")
    return x, y, c


def _dev_slot(ref, by_rows, dev):
    return ref.at[:, dev] if by_rows else ref.at[dev]


def all_gather(shards, by_rows, name, in_vmem=False):
    n = len(shards)

    def body(*refs):
        x_refs, out_refs = refs[:n], refs[n:2 * n]
        send_sems, recv_sems, local_sems = refs[2 * n:]
        x, y, c = _place()
        me, sibling = (x, y, c), (x, y, 1 - c)
        chips = [(1 - x, y), (x, 1 - y), (1 - x, 1 - y)]

        def slot(t, px, py, pc):
            return _dev_slot(out_refs[t], by_rows[t], 4 * px + 2 * py + pc)

        def copy(t, k, blk, to, src=None):
            return pltpu.make_async_remote_copy(
                src_ref=slot(t, *blk) if src is None else src, dst_ref=slot(t, *blk), send_sem=send_sems.at[7 * t + k],
                recv_sem=recv_sems.at[7 * t + k], device_id=to, device_id_type=MESH_ID)

        mine = [pltpu.make_async_copy(x_refs[t], slot(t, *me), local_sems.at[t]) for t in range(n)]
        for cp in mine:
            cp.start()
        first = []
        for t in range(n):
            first.append(copy(t, 0, me, sibling, src=x_refs[t]))
            first += [copy(t, 1 + j, me, (*chip, c), src=x_refs[t]) for j, chip in enumerate(chips)]
        for cp in first:
            cp.start()
        passed = []
        for j, chip in enumerate(chips):
            for t in range(n):
                copy(t, 1 + j, (*chip, c), me).wait_recv()
                passed.append(copy(t, 4 + j, (*chip, c), sibling))
                passed[-1].start()
        for t in range(n):
            copy(t, 0, sibling, me).wait_recv()
            for j, chip in enumerate(chips):
                copy(t, 4 + j, (*chip, 1 - c), me).wait_recv()
        for cp in first + passed:
            cp.wait_send()
        for cp in mine:
            cp.wait()

    def gathered(s, rows):
        shp = (s.shape[0], N_DEV) + s.shape[1:] if rows else (N_DEV,) + s.shape
        return jax.ShapeDtypeStruct(shp, s.dtype)

    space = pl.BlockSpec(memory_space=pltpu.VMEM if in_vmem else pl.ANY)
    return pl.pallas_call(
        body, name=name, out_shape=[gathered(s, r) for s, r in zip(shards, by_rows)],
        in_specs=[space] * n, out_specs=[space] * n,
        scratch_shapes=[pltpu.SemaphoreType.DMA((7 * n,)), pltpu.SemaphoreType.DMA((7 * n,)),
                        pltpu.SemaphoreType.DMA((n,))],
        compiler_params=pltpu.CompilerParams(vmem_limit_bytes=VMEM_LIMIT),
    )(*shards)


_HBM = pl.BlockSpec(memory_space=pltpu.HBM)
_SEM = pl.BlockSpec(memory_space=pltpu.SEMAPHORE)
_SPLIT = dict(has_side_effects=pltpu.SideEffectType.DATAFLOW_SIDE_EFFECTING)


def _hbm(a):
    return pltpu.with_memory_space_constraint(a, pltpu.HBM)


def _gathered_shape(s, rows):
    return (s.shape[0], N_DEV) + s.shape[1:] if rows else (N_DEV,) + s.shape


def _peers(x, y, c):
    return [(1 - x if k & 4 else x, 1 - y if k & 2 else y, 1 - c if k & 1 else c) for k in range(1, N_DEV)]


def gather_start(shards, lands, by_rows, name):
    n = len(shards)

    def body(*refs):
        x_refs, land_refs = refs[:n], refs[n:2 * n]
        send_sems, recv_sems = refs[2 * n], refs[2 * n + 1]
        x, y, c = _place()
        me = 4 * x + 2 * y + c
        for t in range(n):
            for k, peer in enumerate(_peers(x, y, c)):
                pltpu.make_async_remote_copy(
                    src_ref=x_refs[t], dst_ref=_dev_slot(land_refs[t], by_rows[t], me), send_sem=send_sems.at[7 * t + k],
                    recv_sem=recv_sems.at[7 * t + k], device_id=peer, device_id_type=MESH_ID).start()

    sems = pltpu.SemaphoreType.DMA((7 * n,))
    res = pl.pallas_call(
        body, name=name,
        out_shape=(sems, sems) + tuple(pltpu.HBM(a.shape, a.dtype) for a in list(shards) + list(lands)),
        in_specs=[_HBM] * (2 * n), out_specs=(_SEM, _SEM) + (_HBM,) * (2 * n),
        input_output_aliases={i: 2 + i for i in range(2 * n)},
        compiler_params=pltpu.CompilerParams(**_SPLIT),
    )(*[_hbm(a) for a in list(shards) + list(lands)])
    return res[0], res[1], list(res[2:2 + n]), list(res[2 + n:])


def gather_wait(send_sems, recv_sems, first, shards, lands, by_rows, after, name):
    n = len(shards)

    def body(*refs):
        x_refs, land_refs = refs[:n], refs[n:2 * n]
        send_sems, recv_sems = refs[2 * n], refs[2 * n + 1]
        x, y, c = _place()
        for t in range(n):
            for k, (px, py, pc) in enumerate(_peers(x, y, c)):
                cp = pltpu.make_async_remote_copy(
                    src_ref=x_refs[t], dst_ref=_dev_slot(land_refs[t], by_rows[t], 4 * px + 2 * py + pc),
                    send_sem=send_sems.at[7 * (first + t) + k], recv_sem=recv_sems.at[7 * (first + t) + k],
                    device_id=(px, py, pc), device_id_type=MESH_ID)
                cp.wait_send()
                cp.wait_recv()

    res = pl.pallas_call(
        body, name=name, out_shape=tuple(pltpu.HBM(a.shape, a.dtype) for a in list(shards) + list(lands)),
        in_specs=[_HBM] * (2 * n) + [_SEM, _SEM, pl.BlockSpec(memory_space=pl.ANY)], out_specs=(_HBM,) * (2 * n),
        input_output_aliases={i: i for i in range(2 * n)},
        compiler_params=pltpu.CompilerParams(**_SPLIT),
    )(*shards, *lands, send_sems, recv_sems, after)
    return list(res[n:])


def scatter_start(srcs, src_l, lands, land_l, by_rows, name):
    n = len(srcs)

    def body(*refs):
        x_refs, land_refs = refs[:n], refs[n:2 * n]
        send_sems, recv_sems, token = refs[2 * n], refs[2 * n + 1], refs[-1]
        x, y, c = _place()
        me = 4 * x + 2 * y + c
        for k, (px, py, pc) in enumerate(_peers(x, y, c)):
            for t in range(n):
                blk = _dev_slot(x_refs[t], by_rows[t], 4 * px + 2 * py + pc)
                pltpu.make_async_remote_copy(
                    src_ref=blk.at[src_l[t]], dst_ref=land_refs[t].at[me, land_l[t]], send_sem=send_sems.at[7 * t + k],
                    recv_sem=recv_sems.at[7 * t + k], device_id=(px, py, pc), device_id_type=MESH_ID).start()
        token[...] = jnp.zeros_like(token)

    lands = [lax.empty((N_DEV, 1) + s.shape[2:], s.dtype) if ld is None else ld for s, ld in zip(srcs, lands)]
    sems = pltpu.SemaphoreType.DMA((7 * n,))
    res = pl.pallas_call(
        body, name=name,
        out_shape=(sems, sems) + tuple(pltpu.HBM(a.shape, a.dtype) for a in list(srcs) + lands)
        + (jax.ShapeDtypeStruct((8, LANES), F32),),
        in_specs=[_HBM] * (2 * n),
        out_specs=(_SEM, _SEM) + (_HBM,) * (2 * n) + (pl.BlockSpec(memory_space=pltpu.VMEM),),
        input_output_aliases={i: 2 + i for i in range(2 * n)},
        compiler_params=pltpu.CompilerParams(**_SPLIT),
    )(*[_hbm(a) for a in list(srcs) + lands])
    return res[0], res[1], list(res[2:2 + n]), list(res[2 + n:2 + 2 * n]), res[-1]


def scatter_wait(send_sems, recv_sems, srcs, src_l, lands, land_l, by_rows, after, name):
    n = len(srcs)

    def body(*refs):
        x_refs, land_refs = refs[:n], refs[n:2 * n]
        send_sems, recv_sems = refs[2 * n], refs[2 * n + 1]
        x, y, c = _place()
        for k, (px, py, pc) in enumerate(_peers(x, y, c)):
            peer = 4 * px + 2 * py + pc
            for t in range(n):
                cp = pltpu.make_async_remote_copy(
                    src_ref=_dev_slot(x_refs[t], by_rows[t], peer).at[src_l[t]], dst_ref=land_refs[t].at[peer, land_l[t]],
                    send_sem=send_sems.at[7 * t + k], recv_sem=recv_sems.at[7 * t + k], device_id=(px, py, pc),
                    device_id_type=MESH_ID)
                cp.wait_send()
                cp.wait_recv()

    res = pl.pallas_call(
        body, name=name, out_shape=tuple(pltpu.HBM(a.shape, a.dtype) for a in list(srcs) + list(lands)),
        in_specs=[_HBM] * (2 * n) + [_SEM, _SEM, pl.BlockSpec(memory_space=pl.ANY)], out_specs=(_HBM,) * (2 * n),
        input_output_aliases={i: i for i in range(2 * n)},
        compiler_params=pltpu.CompilerParams(**_SPLIT),
    )(*srcs, *lands, send_sems, recv_sems, after)
    return list(res[:n]), list(res[n:])


ADAM_BLOCK_BYTES = 3 << 19


def adamw(w, m, v, parts, name):
    n_parts = parts.shape[0]
    n_l, r, c = w.shape
    lane_c = -(-c // LANES) * LANES
    fits = [t for t in range(16, r, 16) if r % t == 0 and t * lane_c * 4 <= ADAM_BLOCK_BYTES]
    tr = max(fits) if fits and r * lane_c * 4 > ADAM_BLOCK_BYTES else r
    c1 = 1.0 / (1.0 - ADAM_B1 ** ADAM_STEP)
    c2 = 1.0 / (1.0 - ADAM_B2 ** ADAM_STEP)

    def body(w_ref, m_ref, v_ref, p_ref, g_ref, d_ref, nm_ref, nv_ref):
        g = p_ref[0].astype(F32)
        for j in range(1, n_parts):
            g = g + p_ref[j].astype(F32)
        nm = ADAM_B1 * m_ref[...] + (1.0 - ADAM_B1) * g
        nv = ADAM_B2 * v_ref[...] + (1.0 - ADAM_B2) * (g * g)
        g_ref[...] = g
        nm_ref[...] = nm
        nv_ref[...] = nv
        d_ref[...] = -ADAM_LR * ((nm * c1) / (jnp.sqrt(nv * c2) + ADAM_EPS) + ADAM_WD * w_ref[...])

    blk = pl.BlockSpec((1, tr, c), lambda l, i: (l, i, 0))
    return pl.pallas_call(
        body, name=name, grid=(n_l, r // tr),
        in_specs=[blk, blk, blk, pl.BlockSpec((n_parts, 1, tr, c), lambda l, i: (0, l, i, 0))],
        out_specs=[blk] * 4, out_shape=[jax.ShapeDtypeStruct((n_l, r, c), F32)] * 4,
        compiler_params=_params(("parallel", "parallel")),
    )(w, m, v, parts)


def sum_parts(parts, name):
    def body(p_ref, o_ref):
        g = p_ref[0]
        for j in range(1, parts.shape[0]):
            g = g + p_ref[j]
        o_ref[...] = g

    return pl.pallas_call(body, name=name, out_shape=jax.ShapeDtypeStruct(parts.shape[1:], F32),
                          compiler_params=_params())(parts)


def _pack(arrs, rows, dtype):
    flat = jnp.concatenate([a.reshape(-1).astype(dtype) for a in arrs])
    return jnp.pad(flat, (0, rows * LANES - flat.shape[0])).reshape(rows, LANES)


def _unpack(packed, shapes):
    flat, out, off = packed.reshape(-1), [], 0
    for shp in shapes:
        n = int(np.prod(shp))
        out.append(flat[off:off + n].reshape(shp))
        off += n
    return out


def _cat(parts):
    return jnp.concatenate(parts, axis=1)


def _layer_list(i):
    kind, j = i % 3, i // 3
    mix = ([("mla_w_a", j), ("mla_w_uq", j), ("mla_w_ukv", j), ("mla_w_o", j)] if kind == 0 else
           [("dil_w_qkv", 0), ("dil_w_o", 0)] if kind == 1 else [("fox_w_qkvf", 0), ("fox_w_o", 0)])
    return mix + [("ffn_w_in", i), ("ffn_w_out", i), ("ple_w_proj", i), ("ple_w_gate", i)]


def _mla_layout(w_a, g_uq, g_ukv, j):
    def z(r, n):
        return jnp.zeros((r, n), BF16)

    wa = w_a[j]
    a = _cat([wa[:, :640], wa[:, 640:656], z(D, 48), wa[:, 656:672], z(D, 48)])
    q, k, v = [], [], []
    for h in range(HEADS):
        b = g_uq[h // 2, j][:, 96 * (h % 2):96 * (h % 2 + 1)]
        q += [b[:, 64:80], b[:, 0:32], z(Q_RANK, 16), b[:, 80:96], b[:, 32:64], z(Q_RANK, 16)]
        b = g_ukv[h // 2, j][:, LANES * (h % 2):LANES * (h % 2 + 1)]
        k += [z(KV_RANK, 16), b[:, 0:32], z(KV_RANK, 32), b[:, 32:64], z(KV_RANK, 16)]
        v.append(b[:, 64:128])
    return a, _cat(q), _cat(k + v)


def _mla_unlayout(d_a, d_uq, d_ukv):
    a = _cat([d_a[:, :640], d_a[:, 640:656], d_a[:, 704:720]])
    uq, ukv = [], []
    for dev in range(N_DEV):
        q, kv = [], []
        for h in (2 * dev, 2 * dev + 1):
            b = d_uq[:, LANES * h:LANES * (h + 1)]
            q += [b[:, 16:48], b[:, 80:112], b[:, 0:16], b[:, 64:80]]
            b = d_ukv[:, LANES * h:LANES * (h + 1)]
            kv += [b[:, 16:48], b[:, 80:112], d_ukv[:, HEADS * LANES + 64 * h:HEADS * LANES + 64 * (h + 1)]]
        uq.append(_cat(q))
        ukv.append(_cat(kv))
    return a, jnp.stack(uq), jnp.stack(ukv)


def _mixer_fwd(kind, tag, hn, W, aux):
    if kind == 0:
        a = mm(hn, W["w_a"], "nn", f"{tag}_a", tn=768)
        cq = rms_fwd(a[:, :Q_RANK], W["q_norm"], f"{tag}_cq", out_dtype=BF16)
        ckv = rms_fwd(a[:, Q_RANK:Q_RANK + KV_RANK], W["kv_norm"], f"{tag}_ckv", out_dtype=BF16)
        qp = mm(cq, W["w_uq"], "nn", f"{tag}_uq", tk=384)
        kvp = mm(ckv, W["w_ukv"], "nn", f"{tag}_ukv", tk=256)
        q, k = mla_qk_fwd(qp, kvp, a[:, 640:], aux["cos"], aux["sin"], f"{tag}_qk")
        v = kvp.astype(BF16)
        o = attn_fwd(q, k, v, f"{tag}_attn", wide=True, scale=96 ** -0.5, v_off=HEADS)
        y = mm(o, W["w_o"], "nn", f"{tag}_o")
        return y, (a, cq, ckv, q, k, v, o)
    if kind == 1:
        qkv = mm(hn, W["w_qkv"], "nn", f"{tag}_qkv", out_dtype=BF16, tn=1152)
        parts = [dil_fwd(qkv, aux["dil_bias"][g], g, f"{tag}_g{g}") for g in range(3)]
        o, lse = dil_merge([p_[0] for p_ in parts], [p_[1] for p_ in parts], f"{tag}_merge")
        y = mm(o, W["w_o"], "nn", f"{tag}_o")
        return y, (qkv, o, lse)
    a = mm(hn, W["w_qkvf"], "nn", f"{tag}_qkvf", tn=640)
    fl = a[:, 3072:]
    cum = fox_gate_fwd(fl, aux["fox_b"], f"{tag}_gate")[:, :HEADS]
    cum_q = cum.reshape(S, PAIRS, 2).transpose(1, 0, 2)
    cum_k = cum.T.reshape(PAIRS, 2, S)
    ab = a.astype(BF16)
    o = attn_fwd(ab, ab, ab, f"{tag}_attn", wide=False, scale=0.125, k_off=PAIRS, v_off=2 * PAIRS, cum=cum_q, cum_t=cum_k)
    y = mm(o, W["w_o"], "nn", f"{tag}_o")
    return y, (fl, ab, cum_q, cum_k, o)


def _mixer_bwd(kind, tag, hn, dy, W, aux, saved):
    gr = {}
    if kind == 0:
        a, cq, ckv, q, k, v, o = saved
        gr["w_o"] = mm(o, dy, "tn", f"{tag}_dwo", out_dtype=BF16)
        do = mm(dy, W["w_o"], "nt", f"{tag}_do")
        dq, dk, dv = attn_bwd(q, k, v, o, do, f"{tag}_attn_b", wide=True, scale=96 ** -0.5, v_off=HEADS)
        dqp, dkr = mla_qk_bwd(dq, dk, aux["cos"], aux["sin"], f"{tag}_qk_b")
        dkvp = jnp.concatenate([dk, dv], axis=1)
        gr["w_ukv"] = mm(ckv, dkvp, "tn", f"{tag}_dwukv", out_dtype=BF16, tm=256)
        dckv = mm(dkvp, W["w_ukv"], "nt", f"{tag}_dckv", tn=256)
        gr["w_uq"] = mm(cq, dqp, "tn", f"{tag}_dwuq", out_dtype=BF16, tm=384)
        dcq = mm(dqp, W["w_uq"], "nt", f"{tag}_dcq", tn=384)
        da_q, gr["q_norm"] = rms_bwd(a[:, :Q_RANK], W["q_norm"], dcq, f"{tag}_cq_b", out_dtype=BF16)
        da_kv, gr["kv_norm"] = rms_bwd(a[:, Q_RANK:Q_RANK + KV_RANK], W["kv_norm"], dckv, f"{tag}_ckv_b",
                                       out_dtype=BF16)
        da = jnp.concatenate([da_q, da_kv, dkr], axis=1)
        gr["w_a"] = mm(hn, da, "tn", f"{tag}_dwa", out_dtype=BF16, tn=768)
        return mm(da, W["w_a"], "nt", f"{tag}_dhn", tk=768), gr
    if kind == 1:
        qkv, o, lse = saved
        gr["w_o"] = mm(o, dy, "tn", f"{tag}_dwo", out_dtype=BF16)
        do = mm(dy, W["w_o"], "nt", f"{tag}_do")
        cols, dbs = [], []
        for g in range(3):
            dq, dk, dv, db = dil_bwd(qkv, aux["dil_bias"][g], o, lse, do, g, f"{tag}_g{g}_b")
            cols += [dq, dk, dv]
            dbs.append(db)
        dqkv = jnp.concatenate(cols, axis=1)
        gr["dil_dbias"] = dbs
        gr["w_qkv"] = mm(hn, dqkv, "tn", f"{tag}_dwqkv", out_dtype=BF16, out_dev=1152, tn=1152)
        return mm(dqkv, W["w_qkv"], "nt", f"{tag}_dhn", tk=1152), gr
    fl, ab, cum_q, cum_k, o = saved
    gr["w_o"] = mm(o, dy, "tn", f"{tag}_dwo", out_dtype=BF16)
    do = mm(dy, W["w_o"], "nt", f"{tag}_do")
    dq, dk, dv, dcq, dck = attn_bwd(ab, ab, ab, o, do, f"{tag}_attn_b", wide=False, scale=0.125, out_dtype=BF16,
                                    k_off=PAIRS, v_off=2 * PAIRS, cum=cum_q, cum_t=cum_k)
    pad = ((0, 0), (0, LANES - HEADS))
    dcq = jnp.pad(dcq.transpose(1, 0, 2).reshape(S, HEADS), pad)
    dck = jnp.pad(dck.reshape(HEADS, S).T, pad)
    dfl, gr["b_f"] = fox_gate_bwd(fl, aux["fox_b"], dcq, dck, f"{tag}_gate_b")
    da = jnp.concatenate([dq, dk, dv, dfl.astype(BF16)], axis=1)
    gr["w_qkvf"] = mm(hn, da, "tn", f"{tag}_dwqkvf", out_dtype=BF16, tn=640)
    return mm(da, W["w_qkvf"], "nt", f"{tag}_dhn", tk=640), gr


def kernel(x, p, positions, norm_g, ffn_w_in, ffn_w_out, ple_w_proj, ple_w_gate, rel_bias, mla_w_a, mla_q_norm, mla_kv_norm, mla_w_uq, mla_w_ukv, mla_w_o, dil_w_qkv, dil_w_o, fox_w_qkvf, fox_b_f, fox_w_o, loss_target, m_norm_g, m_ffn_w_in, m_ffn_w_out, m_ple_w_proj, m_ple_w_gate, m_rel_bias, m_mla_w_a, m_mla_q_norm, m_mla_kv_norm, m_mla_w_uq, m_mla_w_ukv, m_mla_w_o, m_dil_w_qkv, m_dil_w_o, m_fox_w_qkvf, m_fox_b_f, m_fox_w_o, v_norm_g, v_ffn_w_in, v_ffn_w_out, v_ple_w_proj, v_ple_w_gate, v_rel_bias, v_mla_w_a, v_mla_q_norm, v_mla_kv_norm, v_mla_w_uq, v_mla_w_ukv, v_mla_w_o, v_dil_w_qkv, v_dil_w_o, v_fox_w_qkvf, v_fox_b_f, v_fox_w_o):
    given = dict(locals())
    me = 4 * lax.axis_index("x") + 2 * lax.axis_index("y") + lax.axis_index("c")
    for n in TRANSPOSED:
        for pre in ("", "m_", "v_"):
            given[pre + n] = jnp.swapaxes(given[pre + n], 1, 2)

    rows_of = {n: axis == 1 for n, _, axis in BIG}
    rows_of["gains"] = False
    shape_of = {n: shp for n, shp, _ in BIG}
    lists = [_layer_list(i) for i in range(DEPTH)]
    lists[0] = [("gains", 0)] + lists[0]
    flat = [nl for ls in lists for nl in ls]
    flat_rows = [rows_of[n] for n, _ in flat]
    gain_rows = _rows(sum(int(np.prod(s)) for _, s, _ in SMALL_SHARDED))
    shards = [_pack([given[k] for k, _, _ in SMALL_SHARDED], gain_rows, F32)[None] if n == "gains" else
              given[n][l:l + 1].astype(BF16) for n, l in flat]
    lands = [lax.dynamic_update_slice(lax.empty(_gathered_shape(s, r), s.dtype), s[:, None] if r else s[None],
                                      (0, me, 0, 0) if r else (me, 0, 0, 0)) for s, r in zip(shards, flat_rows)]
    send_s, recv_s, shards, lands = gather_start(shards, lands, flat_rows, "gather_start")

    cos, sin = rope_tables(positions.reshape(S, 1), "rope_tables")
    dil_bias = [mm(rel_bias[:, HEADS * g:HEADS * (g + 1)], jnp.asarray(_bucket_onehot(DIL[g][1])), "tn",
                   f"dil_bias{g}", precise=True, tn=4096).reshape(HEADS, QBLK, 2 * QBLK) for g in range(3)]
    aux = {"cos": cos, "sin": sin, "dil_bias": dil_bias,
           "fox_b": jnp.pad(fox_b_f, ((0, 0), (0, LANES - HEADS)))}

    def arrived(i, part, behind):
        n_mix = len(lists[i]) - 4
        first = sum(len(ls) for ls in lists[:i]) + (n_mix if part else 0)
        sl = slice(first, first + (4 if part else n_mix))
        got = gather_wait(send_s, recv_s, first, shards[sl], lands[sl], flat_rows[sl], behind, f"gather_wait{i}_{part}")
        return {n: g.reshape(1, N_DEV * shape_of[n][1], shape_of[n][2]) if rows_of[n] else g
                for (n, _), g in zip(flat[sl], got)}

    full = {}

    def mixer_weights(i, behind):
        kind, j = i % 3, i // 3
        w = arrived(i, 0, behind)
        if i == 0:
            gains, off = w["gains"].reshape(N_DEV, gain_rows * LANES), 0
            for n, shp, axis in SMALL_SHARDED:
                cnt = int(np.prod(shp))
                g = jnp.moveaxis(gains[:, off:off + cnt].reshape((N_DEV,) + shp), 0, axis)
                full[n] = g.reshape(shp[:axis] + (N_DEV * shp[axis],))
                off += cnt
        W = {"g": [full["norm_g"][i, r][None, :] for r in range(4)]}
        if kind == 0:
            w_a, w_uq, w_ukv = _mla_layout(w["mla_w_a"], w["mla_w_uq"], w["mla_w_ukv"], 0)
            W.update(w_a=w_a, w_uq=w_uq, w_ukv=w_ukv, w_o=Lay(w["mla_w_o"], 0),
                     q_norm=full["mla_q_norm"][j][None, :], kv_norm=full["mla_kv_norm"][j][None, :])
        elif kind == 1:
            W.update(w_qkv=Dev(w["dil_w_qkv"], 0), w_o=Lay(w["dil_w_o"], 0))
        else:
            fox_w = jnp.pad(_cat([w["fox_w_qkvf"][dev, 0] for dev in range(N_DEV)]), ((0, 0), (0, FOX_W - 3088)))
            W.update(w_qkvf=fox_w, w_o=Lay(w["fox_w_o"], 0))
        return W

    def ffn_weights(i, behind):
        w = arrived(i, 1, behind)
        return {"w_in_t": Lay(w["ffn_w_in"], 0), "w_out": Lay(w["ffn_w_out"], 0), "w_proj": Dev(w["ple_w_proj"], 0),
                "w_gate": Lay(w["ple_w_gate"], 0)}

    h = x[0]
    saved, weights = [], []
    for i in range(DEPTH):
        kind, j, W = i % 3, i // 3, mixer_weights(i, h)
        weights.append(W)
        t = f"l{i}"
        hn = rms_fwd(h, W["g"][0], f"{t}_n0", out_dtype=BF16)
        y, mix = _mixer_fwd(kind, f"{t}_mix", hn, W, aux)
        W.update(ffn_weights(i, y))
        h1 = rms_fwd(y, W["g"][1], f"{t}_n1", res=h)
        fin = rms_fwd(h1, W["g"][2], f"{t}_n2", out_dtype=BF16)
        gu = mm(fin, W["w_in_t"], "nt", f"{t}_ffn_in")
        act = swiglu_fwd(gu, f"{t}_swiglu")
        f = mm(act, W["w_out"], "nn", f"{t}_ffn_out")
        h2 = rms_fwd(f, W["g"][3], f"{t}_n3", res=h1)
        pp = mm(p[i, 0], W["w_proj"], "nn", f"{t}_ple_p", tn=LANES, tk=256)
        gt = mm(h2, W["w_gate"], "nn", f"{t}_ple_g")
        h3 = ple_fwd(h2, pp, gt, f"{t}_ple")
        saved.append((h, hn, y, h1, fin, gu, act, f, h2, pp, gt, mix))
        h = h3

    dh, loss_lanes = loss_head(h, loss_target[0], "loss_head")

    grads = {n: None for n, _, _ in BIG}
    landed = {n: (lax.empty((N_DEV,) + shp, BF16) if shp[0] > 1 else None) for n, shp, _ in BIG}
    in_flight = []
    own = {n: [] for n, _, _ in BIG}
    g_norm = [[None] * 4 for _ in range(DEPTH)]
    g_qn, g_kvn = [None, None], [None, None]
    g_rel, g_bf = None, None

    def stacked(n):
        g = grads[n]
        return None if g is None else g.reshape(g.shape[0], N_DEV * g.shape[2], g.shape[3])

    def by_device(g):
        return g.reshape(g.shape[0], N_DEV, g.shape[1] // N_DEV, g.shape[2])

    def start(i, entries, mine, tag):
        names = [n for n, _ in entries]
        srcs = [mine[n] if n in mine else grads[n] for n in names]
        src_l = [0 if n in mine else i for n in names]
        land_l = [l if shape_of[n][0] > 1 else 0 for n, l in entries]
        rows = [rows_of[n] for n in names]
        for n, src, sl, ll, rw in zip(names, srcs, src_l, land_l, rows):
            own[n].append((ll, lax.dynamic_index_in_dim(src, me, axis=1 if rw else 0, keepdims=False)[sl]))
        s_sem, r_sem, srcs, got, token = scatter_start(srcs, src_l, [landed[n] for n in names], land_l, rows, tag)
        for n, src, ld in zip(names, srcs, got):
            landed[n] = ld
            if n in mine:
                mine[n] = src
            else:
                grads[n] = src
        in_flight.append((s_sem, r_sem, names, mine, src_l, land_l, rows))
        return token

    token = None
    for i in reversed(range(DEPTH)):
        kind, j, W = i % 3, i // 3, weights[i]
        t = f"l{i}b"
        h0, hn, y, h1, fin, gu, act, f, h2, pp, gt, mix = saved[i]
        dpp, dgt = ple_bwd(dh, pp, gt, f"{t}_ple")
        grads["ple_w_proj"] = mm(p[i, 0], dpp, "tn", f"{t}_dwp", out_dtype=BF16, out_dev=LANES, tm=256, tn=LANES,
                                 stack=(grads["ple_w_proj"], DEPTH, i), after=token)
        grads["ple_w_gate"] = by_device(mm(h2, dgt, "tn", f"{t}_dwg", out_dtype=BF16,
                                           stack=(stacked("ple_w_gate"), DEPTH, i)))
        dh2 = mm(dgt, W["w_gate"], "nt", f"{t}_dh2", add=dh)
        df, g_norm[i][3] = rms_bwd(f, W["g"][3], dh2, f"{t}_n3", out_dtype=BF16)
        grads["ffn_w_out"] = by_device(mm(act, df, "tn", f"{t}_dwout", out_dtype=BF16, tm=1408,
                                          stack=(stacked("ffn_w_out"), DEPTH, i)))
        dact = mm(df, W["w_out"], "nt", f"{t}_dact", tn=1408)
        dgu = swiglu_bwd(gu, dact, f"{t}_swiglu")
        grads["ffn_w_in"] = by_device(mm(dgu, fin, "tn", f"{t}_dwin", out_dtype=BF16, tm=512, tn=1024,
                                         stack=(stacked("ffn_w_in"), DEPTH, i)))
        token = start(i, lists[i][-4:], {}, f"scatter_ffn{i}")
        dfin = mm(dgu, W["w_in_t"], "nn", f"{t}_dfin", after=token, tk=1408)
        dh1, g_norm[i][2] = rms_bwd(h1, W["g"][2], dfin, f"{t}_n2", res=dh2)
        dy, g_norm[i][1] = rms_bwd(y, W["g"][1], dh1, f"{t}_n1", out_dtype=BF16)
        dhn, gr = _mixer_bwd(kind, f"{t}_mix", hn, dy, W, aux, mix)
        if kind == 0:
            d_a, d_uq, d_ukv = _mla_unlayout(gr["w_a"], gr["w_uq"], gr["w_ukv"])
            mine = {"mla_w_a": by_device(d_a[None]), "mla_w_uq": d_uq[:, None], "mla_w_ukv": d_ukv[:, None],
                    "mla_w_o": by_device(gr["w_o"][None])}
            g_qn[j], g_kvn[j] = gr["q_norm"], gr["kv_norm"]
        elif kind == 1:
            mine = {"dil_w_qkv": gr["w_qkv"], "dil_w_o": by_device(gr["w_o"][None])}
            g_rel = jnp.concatenate(
                [mm(jnp.asarray(_bucket_onehot(DIL[g][1])), gr["dil_dbias"][g].reshape(HEADS, -1), "nt",
                    f"{t}_drel{g}", precise=True, tk=4096) for g in range(3)], axis=1)
        else:
            wide = gr["w_qkvf"]
            mine = {"fox_w_qkvf": jnp.stack([wide[:, 386 * dev:386 * (dev + 1)] for dev in range(N_DEV)])[:, None],
                    "fox_w_o": by_device(gr["w_o"][None])}
            g_bf = gr["b_f"][:, :HEADS]
        token = start(i, [e for e in lists[i][:-4] if e[0] in mine], mine, f"scatter_mix{i}")
        dh, g_norm[i][0] = rms_bwd(h0, W["g"][0], dhn, f"{t}_n0", res=dh1, after=token)
    grad_x = dh[None]

    for idx, (s_sem, r_sem, names, mine, src_l, land_l, rows) in enumerate(in_flight):
        srcs = [mine[n] if n in mine else grads[n] for n in names]
        srcs, got = scatter_wait(s_sem, r_sem, srcs, src_l, [landed[n] for n in names], land_l, rows, dh,
                                 f"scatter_wait{idx}")
        for n, src, ld in zip(names, srcs, got):
            landed[n] = ld
            if n not in mine:
                grads[n] = src
    big_out = []
    for n, _, _ in BIG:
        part = landed[n]
        for ll, blk in own[n]:
            part = lax.dynamic_update_slice(part, blk[None, None], (me, ll, 0, 0))
        big_out.append(adamw(given[n], given["m_" + n], given["v_" + n], part, f"adamw_{n}"))

    small_full = [jnp.stack([jnp.concatenate(r, axis=0) for r in g_norm]).reshape(-1),
                  jnp.concatenate(g_qn, axis=0).reshape(-1), jnp.concatenate(g_kvn, axis=0).reshape(-1),
                  g_rel.reshape(-1), g_bf.reshape(-1), loss_lanes.reshape(-1)]
    small_n = sum(a.shape[0] for a in small_full)
    small_rows = _rows(small_n)
    parts, = all_gather([_pack(small_full, small_rows, F32)], [False], "gather_small_grads", in_vmem=True)
    tot = _unpack(sum_parts(parts, "sum_small_grads"), [(4, 4, D), (2, Q_RANK), (2, KV_RANK), (32, 48), (1, 16), (LANES,)])
    loss = jnp.sum(tot[5])
    small_g = [lax.dynamic_slice_in_dim(tot[0], me * 128, 128, axis=2), lax.dynamic_slice_in_dim(tot[1], me * 48, 48, axis=1),
               lax.dynamic_slice_in_dim(tot[2], me * 32, 32, axis=1), tot[3], tot[4]]
    small_names = [n for n, _, _ in SMALL_SHARDED] + [n for n, _ in SMALL_REPL]
    small_shapes = [s for _, s, _ in SMALL_SHARDED] + [s for _, s in SMALL_REPL]
    s_rows = _rows(sum(int(np.prod(s)) for s in small_shapes))
    small_out = adamw(_pack([given[n] for n in small_names], s_rows, F32)[None],
                      _pack([given["m_" + n] for n in small_names], s_rows, F32)[None],
                      _pack([given["v_" + n] for n in small_names], s_rows, F32)[None],
                      _pack(small_g, s_rows, F32)[None, None], "adamw_small")
    small_out = [_unpack(o_, small_shapes) for o_ in small_out]

    res = [{}, {}, {}, {}]
    for k in range(4):
        for idx, (n, _, _) in enumerate(BIG):
            res[k][n] = jnp.swapaxes(big_out[idx][k], 1, 2) if n in TRANSPOSED else big_out[idx][k]
        for idx, n in enumerate(small_names):
            res[k][n] = small_out[k][idx]
    return (loss, grad_x, *[res[0][n] for n in WEIGHTS], *[res[1][n] for n in WEIGHTS],
            *[res[2][n] for n in WEIGHTS], *[res[3][n] for n in WEIGHTS])
```

```python
import math
from typing import NamedTuple

import numpy as np
import jax
import jax.numpy as jnp
from jax import lax
from jax.experimental import pallas as pl
from jax.experimental.pallas import tpu as pltpu

F32 = jnp.float32
BF16 = jnp.bfloat16
MESH_ID = pl.DeviceIdType.MESH

N_DEV = 8
S = 2048
D = 1024
DEPTH = 4
D_FF = 2816
D_PLE = 256
EPS = 1e-6
NEG = -1e30
LANES = 128
HEADS = 16
PAIRS = 8
Q_RANK = 384
KV_RANK = 256
QBLK = 128
DIL = ((128, 1), (512, 4), (2048, 16))
REL_BUCKETS = 32
FOX_W = 3200
VMEM_LIMIT = 56 * 1024 * 1024

ADAM_LR, ADAM_B1, ADAM_B2, ADAM_EPS, ADAM_WD, ADAM_STEP = 1e-3, 0.9, 0.999, 1e-8, 0.01, 10


TRANSPOSED = ("ffn_w_in",)
BIG = (
    ("ffn_w_in", (4, 704, 1024), 1), ("ffn_w_out", (4, 352, 1024), 1),
    ("ple_w_proj", (4, 256, 128), 2), ("ple_w_gate", (4, 128, 1024), 1),
    ("mla_w_a", (2, 128, 672), 1), ("mla_w_uq", (2, 384, 192), 2),
    ("mla_w_ukv", (2, 256, 256), 2), ("mla_w_o", (2, 128, 1024), 1),
    ("dil_w_qkv", (1, 1024, 1152), 2), ("dil_w_o", (1, 128, 1024), 1),
    ("fox_w_qkvf", (1, 1024, 386), 2), ("fox_w_o", (1, 128, 1024), 1),
)
SMALL_SHARDED = (("norm_g", (4, 4, 128), 2), ("mla_q_norm", (2, 48), 1), ("mla_kv_norm", (2, 32), 1))
SMALL_REPL = (("rel_bias", (32, 48)), ("fox_b_f", (1, 16)))
WEIGHTS = ("norm_g", "ffn_w_in", "ffn_w_out", "ple_w_proj", "ple_w_gate", "rel_bias", "mla_w_a", "mla_q_norm",
           "mla_kv_norm", "mla_w_uq", "mla_w_ukv", "mla_w_o", "dil_w_qkv", "dil_w_o", "fox_w_qkvf", "fox_b_f",
           "fox_w_o")


def _rows(n):
    return -(-n // (8 * LANES)) * 8


def _t5_bucket_np(dist):
    max_exact = REL_BUCKETS // 2
    n = np.maximum(dist.astype(np.float32), np.float32(1.0))
    large = max_exact + (np.log(n / np.float32(max_exact)) / np.float32(math.log(2048 / max_exact))
                         * np.float32(REL_BUCKETS - max_exact)).astype(np.int32)
    large = np.minimum(large, REL_BUCKETS - 1)
    return np.where(dist < max_exact, dist, large)


def _bucket_onehot(dilation):
    i = np.arange(QBLK)[:, None]
    j = np.arange(2 * QBLK)[None, :]
    bucket = _t5_bucket_np(np.clip(QBLK + i - j, 0, None) * dilation).reshape(-1)
    return (np.arange(REL_BUCKETS)[:, None] == bucket[None, :]).astype(np.float32)


def _rope_inv_lanes():
    half = 16
    inv = (np.float32(10000.0) ** (-np.arange(half, dtype=np.float32) / np.float32(half))).astype(np.float32)
    t = np.zeros((1, LANES), np.float32)
    t[0, 0:16] = inv
    t[0, 64:80] = inv
    return t


def _params(sem=None):
    return pltpu.CompilerParams(dimension_semantics=sem, vmem_limit_bytes=VMEM_LIMIT)


def _tile(dim, target):
    if dim <= target or dim % target == 0:
        return min(dim, target)
    t = (target // LANES) * LANES
    while dim % t:
        t -= LANES
    return t


_DIMS = {"nn": (((1,), (0,)), ((), ())), "nt": (((1,), (1,)), ((), ())), "tn": (((0,), (0,)), ((), ()))}


class Lay(NamedTuple):
    arr: jax.Array
    l: int


class Dev(NamedTuple):
    arr: jax.Array
    l: int


def _lshape(op):
    if isinstance(op, Dev):
        g, _, r, w = op.arr.shape
        return r, g * w
    return op.arr.shape[1:] if isinstance(op, Lay) else op.shape


def _op_spec(op, rows_t, cols_t, row_ix, col_ix):
    if isinstance(op, Dev):
        w = op.arr.shape[3]
        assert w % cols_t == 0 and (cols_t % LANES == 0 or cols_t == w), (w, cols_t)
        nb, l = w // cols_t, op.l
        return pl.BlockSpec((1, 1, rows_t, cols_t),
                            lambda i, j, k: (col_ix(i, j, k) // nb, l, row_ix(i, j, k), col_ix(i, j, k) % nb))
    if isinstance(op, Lay):
        l = op.l
        return pl.BlockSpec((1, rows_t, cols_t), lambda i, j, k: (l, row_ix(i, j, k), col_ix(i, j, k)))
    return pl.BlockSpec((rows_t, cols_t), lambda i, j, k: (row_ix(i, j, k), col_ix(i, j, k)))


def _mat(ref):
    return ref[(0,) * (len(ref.shape) - 2)]


def mm(a, b, mode, name, out_dtype=F32, precise=False, add=None, out_dev=None, stack=None, after=None, b_cols=None,
       tm=1024, tn=512, tk=2048):
    (ar, ac), (br, bc) = _lshape(a), _lshape(b)
    M, K = (ac, ar) if mode == "tn" else (ar, ac)
    N = br if mode == "nt" else bc
    assert K == (bc if mode == "nt" else br)
    tm, tn, tk = _tile(M, tm), _tile(N, tn), _tile(K, tk)
    nk = K // tk
    j0, n_blocks = b_cols if b_cols is not None else (0, N // tn)
    N = n_blocks * tn
    ix_i, ix_j, ix_k = (lambda i, j, k: i), (lambda i, j, k: j), (lambda i, j, k: k)
    ix_jb = lambda i, j, k: j + j0
    a_spec = _op_spec(a, tk, tm, ix_k, ix_i) if mode == "tn" else _op_spec(a, tm, tk, ix_i, ix_k)
    b_spec = _op_spec(b, tn, tk, ix_jb, ix_k) if mode == "nt" else _op_spec(b, tk, tn, ix_k, ix_jb)
    buf, n_l, l = stack if stack is not None else (None, 1, 0)
    if out_dev is not None:
        out = Dev(jax.ShapeDtypeStruct((N // out_dev, n_l, M, out_dev), out_dtype), l)
    elif stack is not None:
        out = Lay(jax.ShapeDtypeStruct((n_l, M, N), out_dtype), l)
    else:
        out = jax.ShapeDtypeStruct((M, N), out_dtype)
    o_spec = _op_spec(out, tm, tn, ix_i, ix_j)
    n_in = 3 if add is not None else 2

    def body(*refs):
        a_ref, b_ref = refs[0], refs[1]
        o_ref = refs[n_in + (buf is not None) + (after is not None)]
        if precise:
            part = lax.dot_general(_mat(a_ref), _mat(b_ref), _DIMS[mode], precision=lax.Precision.HIGHEST,
                                   preferred_element_type=F32)
        else:
            part = lax.dot_general(_mat(a_ref).astype(BF16), _mat(b_ref).astype(BF16), _DIMS[mode],
                                   preferred_element_type=F32)

        def finish(r):
            r = r + refs[2][...] if add is not None else r
            o_ref[...] = r.astype(o_ref.dtype).reshape(o_ref.shape)

        if nk == 1:
            finish(part)
            return
        acc, k = refs[-1], pl.program_id(2)

        @pl.when(k == 0)
        def _():
            acc[...] = part

        @pl.when(k > 0)
        def _():
            acc[...] += part

        @pl.when(k == nk - 1)
        def _():
            finish(acc[...])

    ins = [getattr(a, "arr", a), getattr(b, "arr", b)] + ([add] if add is not None else [])
    in_specs = [a_spec, b_spec] + ([o_spec] if add is not None else [])
    aliases = {}
    if buf is not None:
        ins.append(buf)
        in_specs.append(pl.BlockSpec(memory_space=pl.ANY))
        aliases = {n_in: 0}
    if after is not None:
        ins.append(after)
        in_specs.append(pl.BlockSpec(memory_space=pl.ANY))
    return pl.pallas_call(
        body, name=name, grid=(M // tm, N // tn, nk), in_specs=in_specs, out_specs=o_spec,
        out_shape=getattr(out, "arr", out), input_output_aliases=aliases,
        scratch_shapes=[pltpu.VMEM((tm, tn), F32)] if nk > 1 else [],
        compiler_params=_params(("parallel", "parallel", "arbitrary")),
    )(*ins)


def _rows_call(body, name, ins, outs, tr=512, acc_outs=()):
    n = ins[0].shape[0]
    tr = min(tr, n)
    in_specs = [pl.BlockSpec((tr, a.shape[1]), lambda i: (i, 0)) if a.shape[0] == n else
                pl.BlockSpec(a.shape, lambda i: (0, 0)) for a in ins]
    out_specs = [pl.BlockSpec((tr, w), lambda i: (i, 0)) for w, _ in outs] + \
                [pl.BlockSpec((1, w), lambda i: (0, 0)) for w in acc_outs]
    out_shape = [jax.ShapeDtypeStruct((n, w), dt) for w, dt in outs] + \
                [jax.ShapeDtypeStruct((1, w), F32) for w in acc_outs]
    res = pl.pallas_call(body, name=name, grid=(n // tr,), in_specs=in_specs, out_specs=out_specs,
                         out_shape=out_shape, compiler_params=_params(("arbitrary",)))(*ins)
    return res[0] if len(res) == 1 else res


def _acc(ref, val):
    @pl.when(pl.program_id(0) == 0)
    def _():
        ref[...] = jnp.zeros_like(ref)

    ref[...] += val


def rms_fwd(x, g, name, res=None, out_dtype=F32):
    def body(*refs):
        x_ref, g_ref = refs[0], refs[1]
        o_ref = refs[-1]
        xv = x_ref[...]
        y = xv * lax.rsqrt(jnp.mean(xv * xv, axis=-1, keepdims=True) + EPS) * g_ref[...]
        o_ref[...] = (y + refs[2][...] if res is not None else y).astype(o_ref.dtype)

    ins = [x, g] + ([res] if res is not None else [])
    return _rows_call(body, name, ins, [(x.shape[1], out_dtype)])


def rms_bwd(x, g, dy, name, res=None, out_dtype=F32, after=None):
    def body(*refs):
        x_ref, g_ref, dy_ref = refs[:3]
        dx_ref, dg_ref = refs[-2], refs[-1]
        xv, dyv = x_ref[...], dy_ref[...]
        r = lax.rsqrt(jnp.mean(xv * xv, axis=-1, keepdims=True) + EPS)
        xh = xv * r
        dxh = dyv * g_ref[...]
        dx = r * (dxh - xh * jnp.mean(dxh * xh, axis=-1, keepdims=True))
        dx_ref[...] = (dx + refs[3][...] if res is not None else dx).astype(dx_ref.dtype)
        _acc(dg_ref, jnp.sum(dyv * xh, axis=0, keepdims=True))

    ins = [x, g, dy] + ([res] if res is not None else []) + ([after] if after is not None else [])
    return _rows_call(body, name, ins, [(x.shape[1], out_dtype)], acc_outs=(x.shape[1],))


def _sigmoid(x):
    return 0.5 * jnp.tanh(0.5 * x) + 0.5


def swiglu_fwd(gu, name):
    def body(gu_ref, o_ref):
        gate = gu_ref[:, :D_FF]
        o_ref[...] = (gate * _sigmoid(gate) * gu_ref[:, D_FF:]).astype(BF16)

    return _rows_call(body, name, [gu], [(D_FF, BF16)], tr=256)


def swiglu_bwd(gu, dact, name):
    def body(gu_ref, d_ref, o_ref):
        gate, d = gu_ref[:, :D_FF], d_ref[...]
        sg = _sigmoid(gate)
        o_ref[:, :D_FF] = (d * gu_ref[:, D_FF:] * sg * (1.0 + gate * (1.0 - sg))).astype(BF16)
        o_ref[:, D_FF:] = (d * gate * sg).astype(BF16)

    return _rows_call(body, name, [gu, dact], [(2 * D_FF, BF16)], tr=256)


def ple_fwd(h, pp, gt, name):
    def body(h_ref, p_ref, g_ref, o_ref):
        o_ref[...] = h_ref[...] + p_ref[...] * _sigmoid(g_ref[...])

    return _rows_call(body, name, [h, pp, gt], [(D, F32)])


def ple_bwd(dh, pp, gt, name):
    def body(d_ref, p_ref, g_ref, dp_ref, dg_ref):
        d, sg = d_ref[...], _sigmoid(g_ref[...])
        dp_ref[...] = (d * sg).astype(BF16)
        dg_ref[...] = (d * p_ref[...] * sg * (1.0 - sg)).astype(BF16)

    return _rows_call(body, name, [dh, pp, gt], [(D, BF16), (D, BF16)])


def loss_head(y, target, name):
    def body(y_ref, t_ref, d_ref, l_ref):
        e = y_ref[...] - t_ref[...]
        d_ref[...] = e * (1.0 / D)
        col = jnp.sum(e * e, axis=0, keepdims=True) * (0.5 / D)
        _acc(l_ref, sum(col[:, LANES * c:LANES * (c + 1)] for c in range(D // LANES)))

    return _rows_call(body, name, [y, target], [(D, F32)], acc_outs=(LANES,))


def rope_tables(pos_col, name):
    inv = jnp.asarray(_rope_inv_lanes())

    def body(p_ref, inv_ref, c_ref, s_ref):
        ang = p_ref[...].astype(F32) * inv_ref[...]
        lane = lax.broadcasted_iota(jnp.int32, ang.shape, 1)
        first, second = lane < 16, (lane >= 64) & (lane < 80)
        c_ref[...] = jnp.where(first | second, jnp.cos(ang), 1.0)
        sn = jnp.sin(ang)
        s_ref[...] = jnp.where(first, -sn, jnp.where(second, sn, 0.0))

    return _rows_call(body, name, [pos_col, inv], [(LANES, F32), (LANES, F32)])


def _rope(x, c, s):
    return x * c + pltpu.roll(x, 64, axis=1) * s


def _rope_t(d, c, s):
    return d * c + pltpu.roll(d * s, 64, axis=1)


def mla_qk_fwd(qp, kvp, kr, cos, sin, name):
    def body(q_ref, k_ref, kr_ref, c_ref, s_ref, qo_ref, ko_ref):
        c, s = c_ref[...], s_ref[...]
        kr_rot = _rope(kr_ref[...], c, s)
        for h in range(HEADS):
            sl = slice(LANES * h, LANES * (h + 1))
            qo_ref[:, sl] = _rope(q_ref[:, sl], c, s).astype(BF16)
            ko_ref[:, sl] = (k_ref[:, sl] + kr_rot).astype(BF16)

    n = qp.shape[0]
    tr = 256
    w = HEADS * LANES
    return pl.pallas_call(
        body, name=name, grid=(n // tr,),
        in_specs=[pl.BlockSpec((tr, w), lambda i: (i, 0)), pl.BlockSpec((tr, w), lambda i: (i, 0)),
                  pl.BlockSpec((tr, LANES), lambda i: (i, 0)), pl.BlockSpec((tr, LANES), lambda i: (i, 0)),
                  pl.BlockSpec((tr, LANES), lambda i: (i, 0))],
        out_specs=[pl.BlockSpec((tr, w), lambda i: (i, 0))] * 2,
        out_shape=[jax.ShapeDtypeStruct((n, w), BF16)] * 2, compiler_params=_params(("arbitrary",)),
    )(qp, kvp, kr, cos, sin)


def mla_qk_bwd(dq, dk, cos, sin, name):
    def body(dq_ref, dk_ref, c_ref, s_ref, dqp_ref, dkr_ref):
        c, s = c_ref[...], s_ref[...]
        tot = jnp.zeros(c.shape, F32)
        for h in range(HEADS):
            sl = slice(LANES * h, LANES * (h + 1))
            dqp_ref[:, sl] = _rope_t(dq_ref[:, sl], c, s).astype(BF16)
            tot = tot + dk_ref[:, sl]
        dkr_ref[...] = _rope_t(tot, c, s).astype(BF16)

    return _rows_call(body, name, [dq, dk, cos, sin], [(HEADS * LANES, BF16), (LANES, BF16)])


TQ = 256


def _pair_masks(shape):
    lane = lax.broadcasted_iota(jnp.int32, shape, 1)
    return (lane < 64, lane >= 64)


def _scaled_q(q_a, scale):
    return (q_a * jnp.asarray(scale, q_a.dtype), None) if scale == 0.125 else (q_a, scale)


def _causal_probs(q_a, k_a, scale, b0, cq, ck):
    s = lax.dot_general(q_a, k_a, _DIMS["nt"], preferred_element_type=F32)
    if scale is not None:
        s = s * scale
    if cq is not None:
        s = s + (cq - ck)
    row = lax.broadcasted_iota(jnp.int32, (TQ, TQ), 0)
    col = lax.broadcasted_iota(jnp.int32, (TQ, TQ), 1)
    diag = jnp.where(col <= row, s[:, b0:], NEG)
    s = diag if b0 == 0 else jnp.concatenate([s[:, :b0], diag], axis=1)
    e = jnp.exp(s - jnp.max(s, axis=-1, keepdims=True))
    return e * (1.0 / jnp.sum(e, axis=-1, keepdims=True))


def attn_fwd(q, k, v, name, *, wide, scale, q_off=0, k_off=0, v_off=0, cum=None, cum_t=None):
    qw = 2 * LANES if wide else LANES
    forget = cum is not None

    def body(*refs):
        q_ref, k_ref, v_ref = refs[:3]
        o_ref = refs[-1]
        m0, m1 = _pair_masks((TQ, LANES))
        for qi in range(S // TQ):
            b0, b1 = qi * TQ, (qi + 1) * TQ
            outs = []
            for a, msk in enumerate((m0, m1)):
                if wide:
                    q_a, k_a = q_ref[b0:b1, LANES * a:LANES * (a + 1)], k_ref[:b1, LANES * a:LANES * (a + 1)]
                else:
                    q_a, k_a = jnp.where(msk, q_ref[b0:b1, :], jnp.zeros((), BF16)), k_ref[:b1, :]
                cq = refs[3][0, b0:b1, a:a + 1] if forget else None
                ck = refs[4][0, a:a + 1, :b1] if forget else None
                q_a, left = _scaled_q(q_a, scale)
                p = _causal_probs(q_a, k_a, left, b0, cq, ck)
                outs.append(jnp.dot(p.astype(BF16), v_ref[:b1, :], preferred_element_type=F32))
            o_ref[b0:b1, :] = jnp.where(m0, outs[0], outs[1])

    in_specs = [pl.BlockSpec((S, qw), lambda h: (0, q_off * LANES // qw + h)),
                pl.BlockSpec((S, qw), lambda h: (0, k_off * LANES // qw + h)),
                pl.BlockSpec((S, LANES), lambda h: (0, v_off + h))]
    ins = [q, k, v]
    if forget:
        in_specs += [pl.BlockSpec((1, S, 2), lambda h: (h, 0, 0)), pl.BlockSpec((1, 2, S), lambda h: (h, 0, 0))]
        ins += [cum, cum_t]
    return pl.pallas_call(
        body, name=name, grid=(PAIRS,), in_specs=in_specs, out_specs=pl.BlockSpec((S, LANES), lambda h: (0, h)),
        out_shape=jax.ShapeDtypeStruct((S, PAIRS * LANES), F32), compiler_params=_params(("arbitrary",)),
    )(*ins)


def attn_bwd(q, k, v, o, do, name, *, wide, scale, out_dtype=F32, q_off=0, k_off=0, v_off=0, cum=None, cum_t=None):
    qw = 2 * LANES if wide else LANES
    forget = cum is not None

    def body(*refs):
        q_ref, k_ref, v_ref, o_ref, do_ref = refs[:5]
        n_out = 5 if forget else 3
        outs = refs[-(n_out + 2):-2]
        dq_ref, dk_ref, dv_ref = outs[:3]
        dk_acc, dv_acc = refs[-2], refs[-1]
        dk_acc[...] = jnp.zeros_like(dk_acc)
        dv_acc[...] = jnp.zeros_like(dv_acc)
        if forget:
            dcq_ref, dck_ref = outs[3], outs[4]
            dck_ref[...] = jnp.zeros_like(dck_ref)
        m0, m1 = _pair_masks((TQ, LANES))
        for qi in range(S // TQ):
            b0, b1 = qi * TQ, (qi + 1) * TQ
            do2 = do_ref[b0:b1, :]
            dd = do2 * o_ref[b0:b1, :]
            do_b = do2.astype(BF16)
            mk0, mk1 = _pair_masks((b1, LANES))
            dqs = []
            for a, (msk, mk) in enumerate(((m0, mk0), (m1, mk1))):
                lanes = slice(LANES * a, LANES * (a + 1)) if wide else slice(0, LANES)
                if wide:
                    q_a, k_a = q_ref[b0:b1, lanes], k_ref[:b1, lanes]
                else:
                    q_a, k_a = jnp.where(msk, q_ref[b0:b1, :], jnp.zeros((), BF16)), k_ref[:b1, :]
                cq = refs[5][0, b0:b1, a:a + 1] if forget else None
                ck = refs[6][0, a:a + 1, :b1] if forget else None
                q_a, left = _scaled_q(q_a, scale)
                p = _causal_probs(q_a, k_a, left, b0, cq, ck)
                dp = lax.dot_general(jnp.where(msk, do_b, jnp.zeros((), BF16)), v_ref[:b1, :], _DIMS["nt"],
                                     preferred_element_type=F32)
                delta = jnp.sum(jnp.where(msk, dd, 0.0), axis=-1, keepdims=True)
                ds = p * (dp - delta)
                if forget:
                    dcq_ref[0, b0:b1, a:a + 1] = jnp.sum(ds, axis=-1, keepdims=True)
                    dck_ref[0, a:a + 1, :b1] -= jnp.sum(ds, axis=0, keepdims=True)
                ds_b = ds.astype(BF16)
                dqs.append(jnp.dot(ds_b, k_a, preferred_element_type=F32) * scale)
                dk_a = lax.dot_general(ds_b, q_a, _DIMS["tn"], preferred_element_type=F32)
                dk_acc[:b1, lanes] += dk_a if left is None else dk_a * scale
                dv_acc[:b1, :] += jnp.where(mk, lax.dot_general(p.astype(BF16), do_b, _DIMS["tn"],
                                                                 preferred_element_type=F32), 0.0)
            if wide:
                dq_ref[b0:b1, :LANES] = dqs[0].astype(out_dtype)
                dq_ref[b0:b1, LANES:] = dqs[1].astype(out_dtype)
            else:
                dq_ref[b0:b1, :] = jnp.where(m0, dqs[0], dqs[1]).astype(out_dtype)
        dk_ref[...] = dk_acc[...].astype(out_dtype)
        dv_ref[...] = dv_acc[...].astype(out_dtype)

    pair = pl.BlockSpec((S, LANES), lambda h: (0, h))
    qk_out = pl.BlockSpec((S, qw), lambda h: (0, h))
    in_specs = [pl.BlockSpec((S, qw), lambda h: (0, q_off * LANES // qw + h)),
                pl.BlockSpec((S, qw), lambda h: (0, k_off * LANES // qw + h)),
                pl.BlockSpec((S, LANES), lambda h: (0, v_off + h)), pair, pair]
    ins = [q, k, v, o, do]
    out_specs = [qk_out, qk_out, pair]
    out_shape = [jax.ShapeDtypeStruct((S, PAIRS * qw), out_dtype)] * 2 + \
                [jax.ShapeDtypeStruct((S, PAIRS * LANES), out_dtype)]
    if forget:
        by_q, by_k = pl.BlockSpec((1, S, 2), lambda h: (h, 0, 0)), pl.BlockSpec((1, 2, S), lambda h: (h, 0, 0))
        in_specs += [by_q, by_k]
        ins += [cum, cum_t]
        out_specs += [by_q, by_k]
        out_shape += [jax.ShapeDtypeStruct((PAIRS, S, 2), F32), jax.ShapeDtypeStruct((PAIRS, 2, S), F32)]
    return pl.pallas_call(
        body, name=name, grid=(PAIRS,), in_specs=in_specs, out_specs=out_specs, out_shape=out_shape,
        scratch_shapes=[pltpu.VMEM((S, qw), F32), pltpu.VMEM((S, LANES), F32)],
        compiler_params=_params(("arbitrary",)),
    )(*ins)


def _tri(lower):
    r = lax.broadcasted_iota(jnp.int32, (QBLK, QBLK), 0)
    c = lax.broadcasted_iota(jnp.int32, (QBLK, QBLK), 1)
    return jnp.where((c <= r) if lower else (c >= r), 1.0, 0.0).astype(F32)


def _hi_dot(a, b):
    return jnp.dot(a, b, precision=lax.Precision.HIGHEST, preferred_element_type=F32)


def fox_gate_fwd(fl, bias, name):
    def body(f_ref, b_ref, o_ref):
        tri = _tri(True)
        carry = jnp.zeros((1, LANES), F32)
        for n in range(S // QBLK):
            x = f_ref[n * QBLK:(n + 1) * QBLK, :].astype(F32) + b_ref[...]
            lf = jnp.minimum(x, 0.0) - jnp.log(1.0 + jnp.exp(-jnp.abs(x)))
            c = _hi_dot(tri, lf) + carry
            o_ref[n * QBLK:(n + 1) * QBLK, :] = c
            carry = c[QBLK - 1:QBLK, :]

    return pl.pallas_call(body, name=name, out_shape=jax.ShapeDtypeStruct((S, LANES), F32),
                          compiler_params=_params())(fl, bias)


def fox_gate_bwd(fl, bias, dcq, dck, name):
    def body(f_ref, b_ref, dq_ref, dk_ref, o_ref, db_ref):
        tri = _tri(False)
        carry = jnp.zeros((1, LANES), F32)
        db = jnp.zeros((1, LANES), F32)
        for n in reversed(range(S // QBLK)):
            rows = slice(n * QBLK, (n + 1) * QBLK)
            dlf = _hi_dot(tri, dq_ref[rows, :] + dk_ref[rows, :]) + carry
            carry = dlf[0:1, :]
            x = f_ref[rows, :].astype(F32) + b_ref[...]
            dx = dlf * (1.0 - _sigmoid(x))
            o_ref[rows, :] = dx
            db = db + jnp.sum(dx, axis=0, keepdims=True)
        db_ref[...] = db

    return pl.pallas_call(body, name=name, out_shape=[jax.ShapeDtypeStruct((S, LANES), F32),
                                                      jax.ShapeDtypeStruct((1, LANES), F32)],
                          compiler_params=_params())(fl, bias, dcq, dck)


def _band_valid(first):
    w = QBLK if first else 2 * QBLK
    i = lax.broadcasted_iota(jnp.int32, (QBLK, w), 0)
    j = lax.broadcasted_iota(jnp.int32, (QBLK, w), 1)
    return (j <= i) if first else ((j >= i) & (j - QBLK <= i))


def _band_q(q_ref, rows, msk):
    return jnp.where(msk, q_ref[rows, :], jnp.zeros((), BF16)) * jnp.asarray(0.125, BF16)


def _band_logits(q_a, kk, bias, first):
    s = lax.dot_general(q_a, kk, _DIMS["nt"], preferred_element_type=F32) + bias
    return jnp.where(_band_valid(first), s, NEG)


def residue_major(t, d):
    return t if d == 1 else t.reshape(S // d, d, t.shape[1]).transpose(1, 0, 2).reshape(S, t.shape[1])


def token_major(t, d):
    return t if d == 1 else t.reshape(d, S // d, t.shape[1]).transpose(1, 0, 2).reshape(S, t.shape[1])


def dil_fwd(qkv, bias, g, name):
    d = DIL[g][1]
    ls = S // d

    def body(q_ref, k_ref, v_ref, b_ref, o_ref, l_ref):
        m0, m1 = _pair_masks((QBLK, LANES))
        for n in range(ls // QBLK):
            rows = slice(n * QBLK, (n + 1) * QBLK)
            keys = rows if n == 0 else slice((n - 1) * QBLK, (n + 1) * QBLK)
            os_, ls_ = [], []
            for a, msk in enumerate((m0, m1)):
                q_a = _band_q(q_ref, rows, msk)
                bias_a = b_ref[a, :, QBLK:] if n == 0 else b_ref[a]
                s = _band_logits(q_a, k_ref[keys, :], bias_a, n == 0)
                mx = jnp.max(s, axis=-1, keepdims=True)
                e = jnp.exp(s - mx)
                l = jnp.sum(e, axis=-1, keepdims=True)
                os_.append(jnp.dot((e * (1.0 / l)).astype(BF16), v_ref[keys, :], preferred_element_type=F32))
                ls_.append(mx + jnp.log(l))
            o_ref[rows, :] = jnp.where(m0, os_[0], os_[1])
            l_ref[rows, :] = jnp.where(m0, ls_[0], ls_[1])

    def col(j):
        return lambda h, r: (r, j * PAIRS + h)

    out = pl.BlockSpec((ls, LANES), lambda h, r: (r, h))
    return pl.pallas_call(
        body, name=name, grid=(PAIRS, d),
        in_specs=[pl.BlockSpec((ls, LANES), col(0)), pl.BlockSpec((ls, LANES), col(1)), pl.BlockSpec((ls, LANES), col(2)),
                  pl.BlockSpec((2, QBLK, 2 * QBLK), lambda h, r: (h, 0, 0))],
        out_specs=[out, out], out_shape=[jax.ShapeDtypeStruct((S, D), F32)] * 2,
        compiler_params=_params(("arbitrary", "arbitrary")),
    )(qkv, qkv, qkv, bias)


def dil_merge(os_, lses, name):
    def body(o0, o1, o2, l0, l1, l2, o_ref, l_ref):
        ls_ = [l0[...], l1[...], l2[...]]
        mx = jnp.maximum(jnp.maximum(ls_[0], ls_[1]), ls_[2])
        tot = mx + jnp.log(sum(jnp.exp(l - mx) for l in ls_))
        o_ref[...] = sum(jnp.exp(l - tot) * o[...] for l, o in zip(ls_, (o0, o1, o2)))
        l_ref[...] = tot

    return _rows_call(body, name, list(os_) + list(lses), [(D, F32), (D, F32)])


def dil_bwd(qkv, bias, o, lse, do, g, name):
    d = DIL[g][1]
    ls = S // d

    def body(q_ref, k_ref, v_ref, b_ref, o_ref, l_ref, do_ref, dq_ref, dk_ref, dv_ref, db_ref, dk_acc, dv_acc):
        @pl.when(pl.program_id(1) == 0)
        def _():
            db_ref[...] = jnp.zeros_like(db_ref)

        dk_acc[...] = jnp.zeros_like(dk_acc)
        dv_acc[...] = jnp.zeros_like(dv_acc)
        m0, m1 = _pair_masks((QBLK, LANES))
        for n in range(ls // QBLK):
            rows = slice(n * QBLK, (n + 1) * QBLK)
            keys = rows if n == 0 else slice((n - 1) * QBLK, (n + 1) * QBLK)
            nk = QBLK if n == 0 else 2 * QBLK
            do2, lse2 = do_ref[rows, :], l_ref[rows, :]
            dd = do2 * o_ref[rows, :]
            do_b = do2.astype(BF16)
            mk0, mk1 = _pair_masks((nk, LANES))
            dqs = []
            for a, (msk, mk) in enumerate(((m0, mk0), (m1, mk1))):
                q_a = _band_q(q_ref, rows, msk)
                kk = k_ref[keys, :]
                bias_a = b_ref[a, :, QBLK:] if n == 0 else b_ref[a]
                s = _band_logits(q_a, kk, bias_a, n == 0)
                lse_a = jnp.max(jnp.where(msk, lse2, -jnp.inf), axis=-1, keepdims=True)
                p = jnp.exp(s - lse_a)
                dp = lax.dot_general(jnp.where(msk, do_b, jnp.zeros((), BF16)), v_ref[keys, :], _DIMS["nt"],
                                     preferred_element_type=F32)
                delta = jnp.sum(jnp.where(msk, dd, 0.0), axis=-1, keepdims=True)
                ds = p * (dp - delta)
                if n == 0:
                    db_ref[a, :, QBLK:] += ds
                else:
                    db_ref[a] += ds
                ds_b = ds.astype(BF16)
                dqs.append(jnp.dot(ds_b, kk, preferred_element_type=F32) * 0.125)
                dk_acc[keys, :] += lax.dot_general(ds_b, q_a, _DIMS["tn"], preferred_element_type=F32)
                dv_acc[keys, :] += jnp.where(mk, lax.dot_general(p.astype(BF16), do_b, _DIMS["tn"],
                                                                 preferred_element_type=F32), 0.0)
            dq_ref[rows, :] = jnp.where(m0, dqs[0], dqs[1]).astype(BF16)
        dk_ref[...] = dk_acc[...].astype(BF16)
        dv_ref[...] = dv_acc[...].astype(BF16)

    def col(j):
        return lambda h, r: (r, j * PAIRS + h)

    nat = pl.BlockSpec((ls, LANES), lambda h, r: (r, h))
    b_spec = pl.BlockSpec((2, QBLK, 2 * QBLK), lambda h, r: (h, 0, 0))
    return pl.pallas_call(
        body, name=name, grid=(PAIRS, d),
        in_specs=[pl.BlockSpec((ls, LANES), col(0)), pl.BlockSpec((ls, LANES), col(1)), pl.BlockSpec((ls, LANES), col(2)),
                  b_spec, nat, nat, nat],
        out_specs=[nat, nat, nat, b_spec],
        out_shape=[jax.ShapeDtypeStruct((S, D), BF16)] * 3 + [jax.ShapeDtypeStruct((HEADS, QBLK, 2 * QBLK), F32)],
        scratch_shapes=[pltpu.VMEM((ls, LANES), F32), pltpu.VMEM((ls, LANES), F32)],
        compiler_params=_params(("arbitrary", "arbitrary")),
    )(qkv, qkv, qkv, bias, o, lse, do)


def _place():
    x, y, c = lax.axis_index("x"), lax.axis_index("y"), lax.axis_index("c")
    return x, y, c


def _dev_slot(ref, by_rows, dev):
    return ref.at[:, dev] if by_rows else ref.at[dev]


def all_gather(shards, by_rows, name, in_vmem=False):
    n = len(shards)

    def body(*refs):
        x_refs, out_refs = refs[:n], refs[n:2 * n]
        send_sems, recv_sems, local_sems = refs[2 * n:]
        x, y, c = _place()
        me, sibling = (x, y, c), (x, y, 1 - c)
        chips = [(1 - x, y), (x, 1 - y), (1 - x, 1 - y)]

        def slot(t, px, py, pc):
            return _dev_slot(out_refs[t], by_rows[t], 4 * px + 2 * py + pc)

        def copy(t, k, blk, to, src=None):
            return pltpu.make_async_remote_copy(
                src_ref=slot(t, *blk) if src is None else src, dst_ref=slot(t, *blk), send_sem=send_sems.at[7 * t + k],
                recv_sem=recv_sems.at[7 * t + k], device_id=to, device_id_type=MESH_ID)

        mine = [pltpu.make_async_copy(x_refs[t], slot(t, *me), local_sems.at[t]) for t in range(n)]
        for cp in mine:
            cp.start()
        first = []
        for t in range(n):
            first.append(copy(t, 0, me, sibling, src=x_refs[t]))
            first += [copy(t, 1 + j, me, (*chip, c), src=x_refs[t]) for j, chip in enumerate(chips)]
        for cp in first:
            cp.start()
        passed = []
        for j, chip in enumerate(chips):
            for t in range(n):
                copy(t, 1 + j, (*chip, c), me).wait_recv()
                passed.append(copy(t, 4 + j, (*chip, c), sibling))
                passed[-1].start()
        for t in range(n):
            copy(t, 0, sibling, me).wait_recv()
            for j, chip in enumerate(chips):
                copy(t, 4 + j, (*chip, 1 - c), me).wait_recv()
        for cp in first + passed:
            cp.wait_send()
        for cp in mine:
            cp.wait()

    def gathered(s, rows):
        shp = (s.shape[0], N_DEV) + s.shape[1:] if rows else (N_DEV,) + s.shape
        return jax.ShapeDtypeStruct(shp, s.dtype)

    space = pl.BlockSpec(memory_space=pltpu.VMEM if in_vmem else pl.ANY)
    return pl.pallas_call(
        body, name=name, out_shape=[gathered(s, r) for s, r in zip(shards, by_rows)],
        in_specs=[space] * n, out_specs=[space] * n,
        scratch_shapes=[pltpu.SemaphoreType.DMA((7 * n,)), pltpu.SemaphoreType.DMA((7 * n,)),
                        pltpu.SemaphoreType.DMA((n,))],
        compiler_params=pltpu.CompilerParams(vmem_limit_bytes=VMEM_LIMIT),
    )(*shards)


_HBM = pl.BlockSpec(memory_space=pltpu.HBM)
_SEM = pl.BlockSpec(memory_space=pltpu.SEMAPHORE)
_SPLIT = dict(has_side_effects=pltpu.SideEffectType.DATAFLOW_SIDE_EFFECTING)


def _hbm(a):
    return pltpu.with_memory_space_constraint(a, pltpu.HBM)


def _gathered_shape(s, rows):
    return (s.shape[0], N_DEV) + s.shape[1:] if rows else (N_DEV,) + s.shape


def _peers(x, y, c):
    return [(1 - x if k & 4 else x, 1 - y if k & 2 else y, 1 - c if k & 1 else c) for k in range(1, N_DEV)]


def gather_start(shards, lands, by_rows, name):
    n = len(shards)

    def body(*refs):
        x_refs, land_refs = refs[:n], refs[n:2 * n]
        send_sems, recv_sems = refs[2 * n], refs[2 * n + 1]
        x, y, c = _place()
        me = 4 * x + 2 * y + c
        for t in range(n):
            for k, peer in enumerate(_peers(x, y, c)):
                pltpu.make_async_remote_copy(
                    src_ref=x_refs[t], dst_ref=_dev_slot(land_refs[t], by_rows[t], me), send_sem=send_sems.at[7 * t + k],
                    recv_sem=recv_sems.at[7 * t + k], device_id=peer, device_id_type=MESH_ID).start()

    sems = pltpu.SemaphoreType.DMA((7 * n,))
    res = pl.pallas_call(
        body, name=name,
        out_shape=(sems, sems) + tuple(pltpu.HBM(a.shape, a.dtype) for a in list(shards) + list(lands)),
        in_specs=[_HBM] * (2 * n), out_specs=(_SEM, _SEM) + (_HBM,) * (2 * n),
        input_output_aliases={i: 2 + i for i in range(2 * n)},
        compiler_params=pltpu.CompilerParams(**_SPLIT),
    )(*[_hbm(a) for a in list(shards) + list(lands)])
    return res[0], res[1], list(res[2:2 + n]), list(res[2 + n:])


def gather_wait(send_sems, recv_sems, first, shards, lands, by_rows, after, name):
    n = len(shards)

    def body(*refs):
        x_refs, land_refs = refs[:n], refs[n:2 * n]
        send_sems, recv_sems = refs[2 * n], refs[2 * n + 1]
        x, y, c = _place()
        for t in range(n):
            for k, (px, py, pc) in enumerate(_peers(x, y, c)):
                cp = pltpu.make_async_remote_copy(
                    src_ref=x_refs[t], dst_ref=_dev_slot(land_refs[t], by_rows[t], 4 * px + 2 * py + pc),
                    send_sem=send_sems.at[7 * (first + t) + k], recv_sem=recv_sems.at[7 * (first + t) + k],
                    device_id=(px, py, pc), device_id_type=MESH_ID)
                cp.wait_send()
                cp.wait_recv()

    res = pl.pallas_call(
        body, name=name, out_shape=tuple(pltpu.HBM(a.shape, a.dtype) for a in list(shards) + list(lands)),
        in_specs=[_HBM] * (2 * n) + [_SEM, _SEM, pl.BlockSpec(memory_space=pl.ANY)], out_specs=(_HBM,) * (2 * n),
        input_output_aliases={i: i for i in range(2 * n)},
        compiler_params=pltpu.CompilerParams(**_SPLIT),
    )(*shards, *lands, send_sems, recv_sems, after)
    return list(res[n:])


def scatter_start(srcs, src_l, lands, land_l, by_rows, name):
    n = len(srcs)

    def body(*refs):
        x_refs, land_refs = refs[:n], refs[n:2 * n]
        send_sems, recv_sems, token = refs[2 * n], refs[2 * n + 1], refs[-1]
        x, y, c = _place()
        me = 4 * x + 2 * y + c
        for k, (px, py, pc) in enumerate(_peers(x, y, c)):
            for t in range(n):
                blk = _dev_slot(x_refs[t], by_rows[t], 4 * px + 2 * py + pc)
                pltpu.make_async_remote_copy(
                    src_ref=blk.at[src_l[t]], dst_ref=land_refs[t].at[me, land_l[t]], send_sem=send_sems.at[7 * t + k],
                    recv_sem=recv_sems.at[7 * t + k], device_id=(px, py, pc), device_id_type=MESH_ID).start()
        token[...] = jnp.zeros_like(token)

    lands = [lax.empty((N_DEV, 1) + s.shape[2:], s.dtype) if ld is None else ld for s, ld in zip(srcs, lands)]
    sems = pltpu.SemaphoreType.DMA((7 * n,))
    res = pl.pallas_call(
        body, name=name,
        out_shape=(sems, sems) + tuple(pltpu.HBM(a.shape, a.dtype) for a in list(srcs) + lands)
        + (jax.ShapeDtypeStruct((8, LANES), F32),),
        in_specs=[_HBM] * (2 * n),
        out_specs=(_SEM, _SEM) + (_HBM,) * (2 * n) + (pl.BlockSpec(memory_space=pltpu.VMEM),),
        input_output_aliases={i: 2 + i for i in range(2 * n)},
        compiler_params=pltpu.CompilerParams(**_SPLIT),
    )(*[_hbm(a) for a in list(srcs) + lands])
    return res[0], res[1], list(res[2:2 + n]), list(res[2 + n:2 + 2 * n]), res[-1]


def scatter_wait(send_sems, recv_sems, srcs, src_l, lands, land_l, by_rows, after, name):
    n = len(srcs)

    def body(*refs):
        x_refs, land_refs = refs[:n], refs[n:2 * n]
        send_sems, recv_sems = refs[2 * n], refs[2 * n + 1]
        x, y, c = _place()
        for k, (px, py, pc) in enumerate(_peers(x, y, c)):
            peer = 4 * px + 2 * py + pc
            for t in range(n):
                cp = pltpu.make_async_remote_copy(
                    src_ref=_dev_slot(x_refs[t], by_rows[t], peer).at[src_l[t]], dst_ref=land_refs[t].at[peer, land_l[t]],
                    send_sem=send_sems.at[7 * t + k], recv_sem=recv_sems.at[7 * t + k], device_id=(px, py, pc),
                    device_id_type=MESH_ID)
                cp.wait_send()
                cp.wait_recv()

    res = pl.pallas_call(
        body, name=name, out_shape=tuple(pltpu.HBM(a.shape, a.dtype) for a in list(srcs) + list(lands)),
        in_specs=[_HBM] * (2 * n) + [_SEM, _SEM, pl.BlockSpec(memory_space=pl.ANY)], out_specs=(_HBM,) * (2 * n),
        input_output_aliases={i: i for i in range(2 * n)},
        compiler_params=pltpu.CompilerParams(**_SPLIT),
    )(*srcs, *lands, send_sems, recv_sems, after)
    return list(res[:n]), list(res[n:])


ADAM_BLOCK_BYTES = 3 << 19


def adamw(w, m, v, parts, name):
    n_parts = parts.shape[0]
    n_l, r, c = w.shape
    lane_c = -(-c // LANES) * LANES
    fits = [t for t in range(16, r, 16) if r % t == 0 and t * lane_c * 4 <= ADAM_BLOCK_BYTES]
    tr = max(fits) if fits and r * lane_c * 4 > ADAM_BLOCK_BYTES else r
    c1 = 1.0 / (1.0 - ADAM_B1 ** ADAM_STEP)
    c2 = 1.0 / (1.0 - ADAM_B2 ** ADAM_STEP)

    def body(w_ref, m_ref, v_ref, p_ref, g_ref, d_ref, nm_ref, nv_ref):
        g = p_ref[0].astype(F32)
        for j in range(1, n_parts):
            g = g + p_ref[j].astype(F32)
        nm = ADAM_B1 * m_ref[...] + (1.0 - ADAM_B1) * g
        nv = ADAM_B2 * v_ref[...] + (1.0 - ADAM_B2) * (g * g)
        g_ref[...] = g
        nm_ref[...] = nm
        nv_ref[...] = nv
        d_ref[...] = -ADAM_LR * ((nm * c1) / (jnp.sqrt(nv * c2) + ADAM_EPS) + ADAM_WD * w_ref[...])

    blk = pl.BlockSpec((1, tr, c), lambda l, i: (l, i, 0))
    return pl.pallas_call(
        body, name=name, grid=(n_l, r // tr),
        in_specs=[blk, blk, blk, pl.BlockSpec((n_parts, 1, tr, c), lambda l, i: (0, l, i, 0))],
        out_specs=[blk] * 4, out_shape=[jax.ShapeDtypeStruct((n_l, r, c), F32)] * 4,
        compiler_params=_params(("parallel", "parallel")),
    )(w, m, v, parts)


def sum_parts(parts, name):
    def body(p_ref, o_ref):
        g = p_ref[0]
        for j in range(1, parts.shape[0]):
            g = g + p_ref[j]
        o_ref[...] = g

    return pl.pallas_call(body, name=name, out_shape=jax.ShapeDtypeStruct(parts.shape[1:], F32),
                          compiler_params=_params())(parts)


def _pack(arrs, rows, dtype):
    flat = jnp.concatenate([a.reshape(-1).astype(dtype) for a in arrs])
    return jnp.pad(flat, (0, rows * LANES - flat.shape[0])).reshape(rows, LANES)


def _unpack(packed, shapes):
    flat, out, off = packed.reshape(-1), [], 0
    for shp in shapes:
        n = int(np.prod(shp))
        out.append(flat[off:off + n].reshape(shp))
        off += n
    return out


def _cat(parts):
    return jnp.concatenate(parts, axis=1)


def _layer_list(i):
    kind, j = i % 3, i // 3
    mix = ([("mla_w_a", j), ("mla_w_uq", j), ("mla_w_ukv", j), ("mla_w_o", j)] if kind == 0 else
           [("dil_w_qkv", 0), ("dil_w_o", 0)] if kind == 1 else [("fox_w_qkvf", 0), ("fox_w_o", 0)])
    return mix + [("ffn_w_in", i), ("ffn_w_out", i), ("ple_w_proj", i), ("ple_w_gate", i)]


def _mla_layout(w_a, g_uq, g_ukv, j):
    def z(r, n):
        return jnp.zeros((r, n), BF16)

    wa = w_a[j]
    a = _cat([wa[:, :640], wa[:, 640:656], z(D, 48), wa[:, 656:672], z(D, 48)])
    q, k, v = [], [], []
    for h in range(HEADS):
        b = g_uq[h // 2, j][:, 96 * (h % 2):96 * (h % 2 + 1)]
        q += [b[:, 64:80], b[:, 0:32], z(Q_RANK, 16), b[:, 80:96], b[:, 32:64], z(Q_RANK, 16)]
        b = g_ukv[h // 2, j][:, LANES * (h % 2):LANES * (h % 2 + 1)]
        k += [z(KV_RANK, 16), b[:, 0:32], z(KV_RANK, 32), b[:, 32:64], z(KV_RANK, 16)]
        v.append(b[:, 64:128])
    return a, _cat(q), _cat(k + v)


def _mla_unlayout(d_a, d_uq, d_ukv):
    a = _cat([d_a[:, :640], d_a[:, 640:656], d_a[:, 704:720]])
    uq, ukv = [], []
    for dev in range(N_DEV):
        q, kv = [], []
        for h in (2 * dev, 2 * dev + 1):
            b = d_uq[:, LANES * h:LANES * (h + 1)]
            q += [b[:, 16:48], b[:, 80:112], b[:, 0:16], b[:, 64:80]]
            b = d_ukv[:, LANES * h:LANES * (h + 1)]
            kv += [b[:, 16:48], b[:, 80:112], d_ukv[:, HEADS * LANES + 64 * h:HEADS * LANES + 64 * (h + 1)]]
        uq.append(_cat(q))
        ukv.append(_cat(kv))
    return a, jnp.stack(uq), jnp.stack(ukv)


def _mixer_fwd(kind, tag, hn, W, aux):
    if kind == 0:
        a = mm(hn, W["w_a"], "nn", f"{tag}_a", tn=768)
        cq = rms_fwd(a[:, :Q_RANK], W["q_norm"], f"{tag}_cq", out_dtype=BF16)
        ckv = rms_fwd(a[:, Q_RANK:Q_RANK + KV_RANK], W["kv_norm"], f"{tag}_ckv", out_dtype=BF16)
        qp = mm(cq, W["w_uq"], "nn", f"{tag}_uq", tk=384)
        kvp = mm(ckv, W["w_ukv"], "nn", f"{tag}_ukv", tk=256)
        q, k = mla_qk_fwd(qp, kvp, a[:, 640:], aux["cos"], aux["sin"], f"{tag}_qk")
        v = kvp.astype(BF16)
        o = attn_fwd(q, k, v, f"{tag}_attn", wide=True, scale=96 ** -0.5, v_off=HEADS)
        y = mm(o, W["w_o"], "nn", f"{tag}_o")
        return y, (a, cq, ckv, q, k, v, o)
    if kind == 1:
        qkv = [mm(residue_major(hn, DIL[g][1]), W["w_qkv"], "nn", f"{tag}_qkv{g}", out_dtype=BF16, tn=384,
                  b_cols=(8 * g, 8)) for g in range(3)]
        parts = [dil_fwd(qkv[g], aux["dil_bias"][g], g, f"{tag}_g{g}") for g in range(3)]
        o, lse = dil_merge([token_major(p_[0], DIL[g][1]) for g, p_ in enumerate(parts)],
                           [token_major(p_[1], DIL[g][1]) for g, p_ in enumerate(parts)], f"{tag}_merge")
        y = mm(o, W["w_o"], "nn", f"{tag}_o")
        return y, (qkv, o, lse)
    a = mm(hn, W["w_qkvf"], "nn", f"{tag}_qkvf", tn=640)
    fl = a[:, 3072:]
    cum = fox_gate_fwd(fl, aux["fox_b"], f"{tag}_gate")[:, :HEADS]
    cum_q = cum.reshape(S, PAIRS, 2).transpose(1, 0, 2)
    cum_k = cum.T.reshape(PAIRS, 2, S)
    ab = a.astype(BF16)
    o = attn_fwd(ab, ab, ab, f"{tag}_attn", wide=False, scale=0.125, k_off=PAIRS, v_off=2 * PAIRS, cum=cum_q, cum_t=cum_k)
    y = mm(o, W["w_o"], "nn", f"{tag}_o")
    return y, (fl, ab, cum_q, cum_k, o)


def _mixer_bwd(kind, tag, hn, dy, W, aux, saved):
    gr = {}
    if kind == 0:
        a, cq, ckv, q, k, v, o = saved
        gr["w_o"] = mm(o, dy, "tn", f"{tag}_dwo", out_dtype=BF16)
        do = mm(dy, W["w_o"], "nt", f"{tag}_do")
        dq, dk, dv = attn_bwd(q, k, v, o, do, f"{tag}_attn_b", wide=True, scale=96 ** -0.5, v_off=HEADS)
        dqp, dkr = mla_qk_bwd(dq, dk, aux["cos"], aux["sin"], f"{tag}_qk_b")
        dkvp = jnp.concatenate([dk, dv], axis=1)
        gr["w_ukv"] = mm(ckv, dkvp, "tn", f"{tag}_dwukv", out_dtype=BF16, tm=256)
        dckv = mm(dkvp, W["w_ukv"], "nt", f"{tag}_dckv", tn=256)
        gr["w_uq"] = mm(cq, dqp, "tn", f"{tag}_dwuq", out_dtype=BF16, tm=384)
        dcq = mm(dqp, W["w_uq"], "nt", f"{tag}_dcq", tn=384)
        da_q, gr["q_norm"] = rms_bwd(a[:, :Q_RANK], W["q_norm"], dcq, f"{tag}_cq_b", out_dtype=BF16)
        da_kv, gr["kv_norm"] = rms_bwd(a[:, Q_RANK:Q_RANK + KV_RANK], W["kv_norm"], dckv, f"{tag}_ckv_b",
                                       out_dtype=BF16)
        da = jnp.concatenate([da_q, da_kv, dkr], axis=1)
        gr["w_a"] = mm(hn, da, "tn", f"{tag}_dwa", out_dtype=BF16, tn=768)
        return mm(da, W["w_a"], "nt", f"{tag}_dhn", tk=768), gr
    if kind == 1:
        qkv, o, lse = saved
        gr["w_o"] = mm(o, dy, "tn", f"{tag}_dwo", out_dtype=BF16)
        do = mm(dy, W["w_o"], "nt", f"{tag}_do")
        cols, dbs = [], []
        for g, (_, d) in enumerate(DIL):
            dq, dk, dv, db = dil_bwd(qkv[g], aux["dil_bias"][g], residue_major(o, d), residue_major(lse, d),
                                     residue_major(do, d), g, f"{tag}_g{g}_b")
            cols += [token_major(t, d) for t in (dq, dk, dv)]
            dbs.append(db)
        dqkv = jnp.concatenate(cols, axis=1)
        gr["dil_dbias"] = dbs
        gr["w_qkv"] = mm(hn, dqkv, "tn", f"{tag}_dwqkv", out_dtype=BF16, out_dev=1152, tn=1152)
        return mm(dqkv, W["w_qkv"], "nt", f"{tag}_dhn", tk=1152), gr
    fl, ab, cum_q, cum_k, o = saved
    gr["w_o"] = mm(o, dy, "tn", f"{tag}_dwo", out_dtype=BF16)
    do = mm(dy, W["w_o"], "nt", f"{tag}_do")
    dq, dk, dv, dcq, dck = attn_bwd(ab, ab, ab, o, do, f"{tag}_attn_b", wide=False, scale=0.125, out_dtype=BF16,
                                    k_off=PAIRS, v_off=2 * PAIRS, cum=cum_q, cum_t=cum_k)
    pad = ((0, 0), (0, LANES - HEADS))
    dcq = jnp.pad(dcq.transpose(1, 0, 2).reshape(S, HEADS), pad)
    dck = jnp.pad(dck.reshape(HEADS, S).T, pad)
    dfl, gr["b_f"] = fox_gate_bwd(fl, aux["fox_b"], dcq, dck, f"{tag}_gate_b")
    da = jnp.concatenate([dq, dk, dv, dfl.astype(BF16)], axis=1)
    gr["w_qkvf"] = mm(hn, da, "tn", f"{tag}_dwqkvf", out_dtype=BF16, tn=640)
    return mm(da, W["w_qkvf"], "nt", f"{tag}_dhn", tk=640), gr


def kernel(x, p, positions, norm_g, ffn_w_in, ffn_w_out, ple_w_proj, ple_w_gate, rel_bias, mla_w_a, mla_q_norm, mla_kv_norm, mla_w_uq, mla_w_ukv, mla_w_o, dil_w_qkv, dil_w_o, fox_w_qkvf, fox_b_f, fox_w_o, loss_target, m_norm_g, m_ffn_w_in, m_ffn_w_out, m_ple_w_proj, m_ple_w_gate, m_rel_bias, m_mla_w_a, m_mla_q_norm, m_mla_kv_norm, m_mla_w_uq, m_mla_w_ukv, m_mla_w_o, m_dil_w_qkv, m_dil_w_o, m_fox_w_qkvf, m_fox_b_f, m_fox_w_o, v_norm_g, v_ffn_w_in, v_ffn_w_out, v_ple_w_proj, v_ple_w_gate, v_rel_bias, v_mla_w_a, v_mla_q_norm, v_mla_kv_norm, v_mla_w_uq, v_mla_w_ukv, v_mla_w_o, v_dil_w_qkv, v_dil_w_o, v_fox_w_qkvf, v_fox_b_f, v_fox_w_o):
    given = dict(locals())
    me = 4 * lax.axis_index("x") + 2 * lax.axis_index("y") + lax.axis_index("c")
    for n in TRANSPOSED:
        for pre in ("", "m_", "v_"):
            given[pre + n] = jnp.swapaxes(given[pre + n], 1, 2)

    rows_of = {n: axis == 1 for n, _, axis in BIG}
    rows_of["gains"] = False
    shape_of = {n: shp for n, shp, _ in BIG}
    lists = [_layer_list(i) for i in range(DEPTH)]
    lists[0] = [("gains", 0)] + lists[0]
    flat = [nl for ls in lists for nl in ls]
    flat_rows = [rows_of[n] for n, _ in flat]
    gain_rows = _rows(sum(int(np.prod(s)) for _, s, _ in SMALL_SHARDED))
    shards = [_pack([given[k] for k, _, _ in SMALL_SHARDED], gain_rows, F32)[None] if n == "gains" else
              given[n][l:l + 1].astype(BF16) for n, l in flat]
    lands = [lax.dynamic_update_slice(lax.empty(_gathered_shape(s, r), s.dtype), s[:, None] if r else s[None],
                                      (0, me, 0, 0) if r else (me, 0, 0, 0)) for s, r in zip(shards, flat_rows)]
    send_s, recv_s, shards, lands = gather_start(shards, lands, flat_rows, "gather_start")

    cos, sin = rope_tables(positions.reshape(S, 1), "rope_tables")
    dil_bias = [mm(rel_bias[:, HEADS * g:HEADS * (g + 1)], jnp.asarray(_bucket_onehot(DIL[g][1])), "tn",
                   f"dil_bias{g}", precise=True, tn=4096).reshape(HEADS, QBLK, 2 * QBLK) for g in range(3)]
    aux = {"cos": cos, "sin": sin, "dil_bias": dil_bias,
           "fox_b": jnp.pad(fox_b_f, ((0, 0), (0, LANES - HEADS)))}

    def arrived(i, part, behind):
        n_mix = len(lists[i]) - 4
        first = sum(len(ls) for ls in lists[:i]) + (n_mix if part else 0)
        sl = slice(first, first + (4 if part else n_mix))
        got = gather_wait(send_s, recv_s, first, shards[sl], lands[sl], flat_rows[sl], behind, f"gather_wait{i}_{part}")
        return {n: g.reshape(1, N_DEV * shape_of[n][1], shape_of[n][2]) if rows_of[n] else g
                for (n, _), g in zip(flat[sl], got)}

    full = {}

    def mixer_weights(i, behind):
        kind, j = i % 3, i // 3
        w = arrived(i, 0, behind)
        if i == 0:
            gains, off = w["gains"].reshape(N_DEV, gain_rows * LANES), 0
            for n, shp, axis in SMALL_SHARDED:
                cnt = int(np.prod(shp))
                g = jnp.moveaxis(gains[:, off:off + cnt].reshape((N_DEV,) + shp), 0, axis)
                full[n] = g.reshape(shp[:axis] + (N_DEV * shp[axis],))
                off += cnt
        W = {"g": [full["norm_g"][i, r][None, :] for r in range(4)]}
        if kind == 0:
            w_a, w_uq, w_ukv = _mla_layout(w["mla_w_a"], w["mla_w_uq"], w["mla_w_ukv"], 0)
            W.update(w_a=w_a, w_uq=w_uq, w_ukv=w_ukv, w_o=Lay(w["mla_w_o"], 0),
                     q_norm=full["mla_q_norm"][j][None, :], kv_norm=full["mla_kv_norm"][j][None, :])
        elif kind == 1:
            W.update(w_qkv=Dev(w["dil_w_qkv"], 0), w_o=Lay(w["dil_w_o"], 0))
        else:
            fox_w = jnp.pad(_cat([w["fox_w_qkvf"][dev, 0] for dev in range(N_DEV)]), ((0, 0), (0, FOX_W - 3088)))
            W.update(w_qkvf=fox_w, w_o=Lay(w["fox_w_o"], 0))
        return W

    def ffn_weights(i, behind):
        w = arrived(i, 1, behind)
        return {"w_in_t": Lay(w["ffn_w_in"], 0), "w_out": Lay(w["ffn_w_out"], 0), "w_proj": Dev(w["ple_w_proj"], 0),
                "w_gate": Lay(w["ple_w_gate"], 0)}

    h = x[0]
    saved, weights = [], []
    for i in range(DEPTH):
        kind, j, W = i % 3, i // 3, mixer_weights(i, h)
        weights.append(W)
        t = f"l{i}"
        hn = rms_fwd(h, W["g"][0], f"{t}_n0", out_dtype=BF16)
        y, mix = _mixer_fwd(kind, f"{t}_mix", hn, W, aux)
        W.update(ffn_weights(i, y))
        h1 = rms_fwd(y, W["g"][1], f"{t}_n1", res=h)
        fin = rms_fwd(h1, W["g"][2], f"{t}_n2", out_dtype=BF16)
        gu = mm(fin, W["w_in_t"], "nt", f"{t}_ffn_in")
        act = swiglu_fwd(gu, f"{t}_swiglu")
        f = mm(act, W["w_out"], "nn", f"{t}_ffn_out")
        h2 = rms_fwd(f, W["g"][3], f"{t}_n3", res=h1)
        pp = mm(p[i, 0], W["w_proj"], "nn", f"{t}_ple_p", tn=LANES, tk=256)
        gt = mm(h2, W["w_gate"], "nn", f"{t}_ple_g")
        h3 = ple_fwd(h2, pp, gt, f"{t}_ple")
        saved.append((h, hn, y, h1, fin, gu, act, f, h2, pp, gt, mix))
        h = h3

    dh, loss_lanes = loss_head(h, loss_target[0], "loss_head")

    grads = {n: None for n, _, _ in BIG}
    landed = {n: (lax.empty((N_DEV,) + shp, BF16) if shp[0] > 1 else None) for n, shp, _ in BIG}
    in_flight = []
    own = {n: [] for n, _, _ in BIG}
    g_norm = [[None] * 4 for _ in range(DEPTH)]
    g_qn, g_kvn = [None, None], [None, None]
    g_rel, g_bf = None, None

    def stacked(n):
        g = grads[n]
        return None if g is None else g.reshape(g.shape[0], N_DEV * g.shape[2], g.shape[3])

    def by_device(g):
        return g.reshape(g.shape[0], N_DEV, g.shape[1] // N_DEV, g.shape[2])

    def start(i, entries, mine, tag):
        names = [n for n, _ in entries]
        srcs = [mine[n] if n in mine else grads[n] for n in names]
        src_l = [0 if n in mine else i for n in names]
        land_l = [l if shape_of[n][0] > 1 else 0 for n, l in entries]
        rows = [rows_of[n] for n in names]
        s_sem, r_sem, srcs, got, token = scatter_start(srcs, src_l, [landed[n] for n in names], land_l, rows, tag)
        for n, src, ld in zip(names, srcs, got):
            landed[n] = ld
            if n in mine:
                mine[n] = src
            else:
                grads[n] = src
        in_flight.append((s_sem, r_sem, names, mine, src_l, land_l, rows))
        return token

    token = None
    for i in reversed(range(DEPTH)):
        kind, j, W = i % 3, i // 3, weights[i]
        t = f"l{i}b"
        h0, hn, y, h1, fin, gu, act, f, h2, pp, gt, mix = saved[i]
        dpp, dgt = ple_bwd(dh, pp, gt, f"{t}_ple")
        grads["ple_w_proj"] = mm(p[i, 0], dpp, "tn", f"{t}_dwp", out_dtype=BF16, out_dev=LANES, tm=256, tn=LANES,
                                 stack=(grads["ple_w_proj"], DEPTH, i), after=token)
        grads["ple_w_gate"] = by_device(mm(h2, dgt, "tn", f"{t}_dwg", out_dtype=BF16,
                                           stack=(stacked("ple_w_gate"), DEPTH, i)))
        dh2 = mm(dgt, W["w_gate"], "nt", f"{t}_dh2", add=dh)
        df, g_norm[i][3] = rms_bwd(f, W["g"][3], dh2, f"{t}_n3", out_dtype=BF16)
        grads["ffn_w_out"] = by_device(mm(act, df, "tn", f"{t}_dwout", out_dtype=BF16, tm=1408,
                                          stack=(stacked("ffn_w_out"), DEPTH, i)))
        dact = mm(df, W["w_out"], "nt", f"{t}_dact", tn=1408)
        dgu = swiglu_bwd(gu, dact, f"{t}_swiglu")
        grads["ffn_w_in"] = by_device(mm(dgu, fin, "tn", f"{t}_dwin", out_dtype=BF16, tm=512, tn=1024,
                                         stack=(stacked("ffn_w_in"), DEPTH, i)))
        token = start(i, lists[i][-4:], {}, f"scatter_ffn{i}")
        dfin = mm(dgu, W["w_in_t"], "nn", f"{t}_dfin", after=token, tk=1408)
        dh1, g_norm[i][2] = rms_bwd(h1, W["g"][2], dfin, f"{t}_n2", res=dh2)
        dy, g_norm[i][1] = rms_bwd(y, W["g"][1], dh1, f"{t}_n1", out_dtype=BF16)
        dhn, gr = _mixer_bwd(kind, f"{t}_mix", hn, dy, W, aux, mix)
        if kind == 0:
            d_a, d_uq, d_ukv = _mla_unlayout(gr["w_a"], gr["w_uq"], gr["w_ukv"])
            mine = {"mla_w_a": by_device(d_a[None]), "mla_w_uq": d_uq[:, None], "mla_w_ukv": d_ukv[:, None],
                    "mla_w_o": by_device(gr["w_o"][None])}
            g_qn[j], g_kvn[j] = gr["q_norm"], gr["kv_norm"]
        elif kind == 1:
            mine = {"dil_w_qkv": gr["w_qkv"], "dil_w_o": by_device(gr["w_o"][None])}
            g_rel = jnp.concatenate(
                [mm(jnp.asarray(_bucket_onehot(DIL[g][1])), gr["dil_dbias"][g].reshape(HEADS, -1), "nt",
                    f"{t}_drel{g}", precise=True, tk=4096) for g in range(3)], axis=1)
        else:
            wide = gr["w_qkvf"]
            mine = {"fox_w_qkvf": jnp.stack([wide[:, 386 * dev:386 * (dev + 1)] for dev in range(N_DEV)])[:, None],
                    "fox_w_o": by_device(gr["w_o"][None])}
            g_bf = gr["b_f"][:, :HEADS]
        token = start(i, [e for e in lists[i][:-4] if e[0] in mine], mine, f"scatter_mix{i}")
        dh, g_norm[i][0] = rms_bwd(h0, W["g"][0], dhn, f"{t}_n0", res=dh1, after=token)
    grad_x = dh[None]

    for idx, (s_sem, r_sem, names, mine, src_l, land_l, rows) in enumerate(in_flight):
        srcs = [mine[n] if n in mine else grads[n] for n in names]
        srcs, got = scatter_wait(s_sem, r_sem, srcs, src_l, [landed[n] for n in names], land_l, rows, dh,
                                 f"scatter_wait{idx}")
        for n, src, ld in zip(names, srcs, got):
            landed[n] = ld
            if n in mine:
                mine[n] = src
            else:
                grads[n] = src
    for _, _, names, mine, src_l, land_l, rows in in_flight:
        for n, sl, ll, rw in zip(names, src_l, land_l, rows):
            src = mine[n] if n in mine else grads[n]
            own[n].append((ll, lax.dynamic_index_in_dim(src, me, axis=1 if rw else 0, keepdims=False)[sl]))
    big_out = []
    for n, _, _ in BIG:
        part = landed[n]
        for ll, blk in own[n]:
            part = lax.dynamic_update_slice(part, blk[None, None], (me, ll, 0, 0))
        big_out.append(adamw(given[n], given["m_" + n], given["v_" + n], part, f"adamw_{n}"))

    small_full = [jnp.stack([jnp.concatenate(r, axis=0) for r in g_norm]).reshape(-1),
                  jnp.concatenate(g_qn, axis=0).reshape(-1), jnp.concatenate(g_kvn, axis=0).reshape(-1),
                  g_rel.reshape(-1), g_bf.reshape(-1), loss_lanes.reshape(-1)]
    small_n = sum(a.shape[0] for a in small_full)
    small_rows = _rows(small_n)
    parts, = all_gather([_pack(small_full, small_rows, F32)], [False], "gather_small_grads", in_vmem=True)
    tot = _unpack(sum_parts(parts, "sum_small_grads"), [(4, 4, D), (2, Q_RANK), (2, KV_RANK), (32, 48), (1, 16), (LANES,)])
    loss = jnp.sum(tot[5])
    small_g = [lax.dynamic_slice_in_dim(tot[0], me * 128, 128, axis=2), lax.dynamic_slice_in_dim(tot[1], me * 48, 48, axis=1),
               lax.dynamic_slice_in_dim(tot[2], me * 32, 32, axis=1), tot[3], tot[4]]
    small_names = [n for n, _, _ in SMALL_SHARDED] + [n for n, _ in SMALL_REPL]
    small_shapes = [s for _, s, _ in SMALL_SHARDED] + [s for _, s in SMALL_REPL]
    s_rows = _rows(sum(int(np.prod(s)) for s in small_shapes))
    small_out = adamw(_pack([given[n] for n in small_names], s_rows, F32)[None],
                      _pack([given["m_" + n] for n in small_names], s_rows, F32)[None],
                      _pack([given["v_" + n] for n in small_names], s_rows, F32)[None],
                      _pack(small_g, s_rows, F32)[None, None], "adamw_small")
    small_out = [_unpack(o_, small_shapes) for o_ in small_out]

    res = [{}, {}, {}, {}]
    for k in range(4):
        for idx, (n, _, _) in enumerate(BIG):
            res[k][n] = jnp.swapaxes(big_out[idx][k], 1, 2) if n in TRANSPOSED else big_out[idx][k]
        for idx, n in enumerate(small_names):
            res[k][n] = small_out[k][idx]
    return (loss, grad_x, *[res[0][n] for n in WEIGHTS], *[res[1][n] for n in WEIGHTS],
            *[res[2][n] for n in WEIGHTS], *[res[3][n] for n in WEIGHTS])
```

```python
import math
from typing import NamedTuple

import numpy as np
import jax
import jax.numpy as jnp
from jax import lax
from jax.experimental import pallas as pl
from jax.experimental.pallas import tpu as pltpu

F32 = jnp.float32
BF16 = jnp.bfloat16
MESH_ID = pl.DeviceIdType.MESH

N_DEV = 8
S = 2048
D = 1024
DEPTH = 4
D_FF = 2816
D_PLE = 256
EPS = 1e-6
NEG = -1e30
LANES = 128
HEADS = 16
PAIRS = 8
Q_RANK = 384
KV_RANK = 256
QBLK = 128
DIL = ((128, 1), (512, 4), (2048, 16))
REL_BUCKETS = 32
FOX_W = 3200
VMEM_LIMIT = 56 * 1024 * 1024

ADAM_LR, ADAM_B1, ADAM_B2, ADAM_EPS, ADAM_WD, ADAM_STEP = 1e-3, 0.9, 0.999, 1e-8, 0.01, 10


TRANSPOSED = ("ffn_w_in",)
BIG = (
    ("ffn_w_in", (4, 704, 1024), 1), ("ffn_w_out", (4, 352, 1024), 1),
    ("ple_w_proj", (4, 256, 128), 2), ("ple_w_gate", (4, 128, 1024), 1),
    ("mla_w_a", (2, 128, 672), 1), ("mla_w_uq", (2, 384, 192), 2),
    ("mla_w_ukv", (2, 256, 256), 2), ("mla_w_o", (2, 128, 1024), 1),
    ("dil_w_qkv", (1, 1024, 1152), 2), ("dil_w_o", (1, 128, 1024), 1),
    ("fox_w_qkvf", (1, 1024, 386), 2), ("fox_w_o", (1, 128, 1024), 1),
)
SMALL_SHARDED = (("norm_g", (4, 4, 128), 2), ("mla_q_norm", (2, 48), 1), ("mla_kv_norm", (2, 32), 1))
SMALL_REPL = (("rel_bias", (32, 48)), ("fox_b_f", (1, 16)))
WEIGHTS = ("norm_g", "ffn_w_in", "ffn_w_out", "ple_w_proj", "ple_w_gate", "rel_bias", "mla_w_a", "mla_q_norm",
           "mla_kv_norm", "mla_w_uq", "mla_w_ukv", "mla_w_o", "dil_w_qkv", "dil_w_o", "fox_w_qkvf", "fox_b_f",
           "fox_w_o")


def _rows(n):
    return -(-n // (8 * LANES)) * 8


def _t5_bucket_np(dist):
    max_exact = REL_BUCKETS // 2
    n = np.maximum(dist.astype(np.float32), np.float32(1.0))
    large = max_exact + (np.log(n / np.float32(max_exact)) / np.float32(math.log(2048 / max_exact))
                         * np.float32(REL_BUCKETS - max_exact)).astype(np.int32)
    large = np.minimum(large, REL_BUCKETS - 1)
    return np.where(dist < max_exact, dist, large)


def _bucket_onehot(dilation):
    i = np.arange(QBLK)[:, None]
    j = np.arange(2 * QBLK)[None, :]
    bucket = _t5_bucket_np(np.clip(QBLK + i - j, 0, None) * dilation).reshape(-1)
    return (np.arange(REL_BUCKETS)[:, None] == bucket[None, :]).astype(np.float32)


def _rope_inv_lanes():
    half = 16
    inv = (np.float32(10000.0) ** (-np.arange(half, dtype=np.float32) / np.float32(half))).astype(np.float32)
    t = np.zeros((1, LANES), np.float32)
    t[0, 0:16] = inv
    t[0, 64:80] = inv
    return t


def _params(sem=None):
    return pltpu.CompilerParams(dimension_semantics=sem, vmem_limit_bytes=VMEM_LIMIT)


def _tile(dim, target):
    if dim <= target or dim % target == 0:
        return min(dim, target)
    t = (target // LANES) * LANES
    while dim % t:
        t -= LANES
    return t


_DIMS = {"nn": (((1,), (0,)), ((), ())), "nt": (((1,), (1,)), ((), ())), "tn": (((0,), (0,)), ((), ()))}


class Lay(NamedTuple):
    arr: jax.Array
    l: int


class Dev(NamedTuple):
    arr: jax.Array
    l: int


def _lshape(op):
    if isinstance(op, Dev):
        g, _, r, w = op.arr.shape
        return r, g * w
    return op.arr.shape[1:] if isinstance(op, Lay) else op.shape


def _op_spec(op, rows_t, cols_t, row_ix, col_ix):
    if isinstance(op, Dev):
        w = op.arr.shape[3]
        assert w % cols_t == 0 and (cols_t % LANES == 0 or cols_t == w), (w, cols_t)
        nb, l = w // cols_t, op.l
        return pl.BlockSpec((1, 1, rows_t, cols_t),
                            lambda i, j, k: (col_ix(i, j, k) // nb, l, row_ix(i, j, k), col_ix(i, j, k) % nb))
    if isinstance(op, Lay):
        l = op.l
        return pl.BlockSpec((1, rows_t, cols_t), lambda i, j, k: (l, row_ix(i, j, k), col_ix(i, j, k)))
    return pl.BlockSpec((rows_t, cols_t), lambda i, j, k: (row_ix(i, j, k), col_ix(i, j, k)))


def _mat(ref):
    return ref[(0,) * (len(ref.shape) - 2)]


def mm(a, b, mode, name, out_dtype=F32, precise=False, add=None, out_dev=None, stack=None, after=None, b_cols=None,
       tm=1024, tn=512, tk=2048):
    (ar, ac), (br, bc) = _lshape(a), _lshape(b)
    M, K = (ac, ar) if mode == "tn" else (ar, ac)
    N = br if mode == "nt" else bc
    assert K == (bc if mode == "nt" else br)
    tm, tn, tk = _tile(M, tm), _tile(N, tn), _tile(K, tk)
    nk = K // tk
    j0, n_blocks = b_cols if b_cols is not None else (0, N // tn)
    N = n_blocks * tn
    ix_i, ix_j, ix_k = (lambda i, j, k: i), (lambda i, j, k: j), (lambda i, j, k: k)
    ix_jb = lambda i, j, k: j + j0
    a_spec = _op_spec(a, tk, tm, ix_k, ix_i) if mode == "tn" else _op_spec(a, tm, tk, ix_i, ix_k)
    b_spec = _op_spec(b, tn, tk, ix_jb, ix_k) if mode == "nt" else _op_spec(b, tk, tn, ix_k, ix_jb)
    buf, n_l, l = stack if stack is not None else (None, 1, 0)
    if out_dev is not None:
        out = Dev(jax.ShapeDtypeStruct((N // out_dev, n_l, M, out_dev), out_dtype), l)
    elif stack is not None:
        out = Lay(jax.ShapeDtypeStruct((n_l, M, N), out_dtype), l)
    else:
        out = jax.ShapeDtypeStruct((M, N), out_dtype)
    o_spec = _op_spec(out, tm, tn, ix_i, ix_j)
    n_in = 3 if add is not None else 2

    def body(*refs):
        a_ref, b_ref = refs[0], refs[1]
        o_ref = refs[n_in + (buf is not None) + (after is not None)]
        if precise:
            part = lax.dot_general(_mat(a_ref), _mat(b_ref), _DIMS[mode], precision=lax.Precision.HIGHEST,
                                   preferred_element_type=F32)
        else:
            part = lax.dot_general(_mat(a_ref).astype(BF16), _mat(b_ref).astype(BF16), _DIMS[mode],
                                   preferred_element_type=F32)

        def finish(r):
            r = r + refs[2][...] if add is not None else r
            o_ref[...] = r.astype(o_ref.dtype).reshape(o_ref.shape)

        if nk == 1:
            finish(part)
            return
        acc, k = refs[-1], pl.program_id(2)

        @pl.when(k == 0)
        def _():
            acc[...] = part

        @pl.when(k > 0)
        def _():
            acc[...] += part

        @pl.when(k == nk - 1)
        def _():
            finish(acc[...])

    ins = [getattr(a, "arr", a), getattr(b, "arr", b)] + ([add] if add is not None else [])
    in_specs = [a_spec, b_spec] + ([o_spec] if add is not None else [])
    aliases = {}
    if buf is not None:
        ins.append(buf)
        in_specs.append(pl.BlockSpec(memory_space=pl.ANY))
        aliases = {n_in: 0}
    if after is not None:
        ins.append(after)
        in_specs.append(pl.BlockSpec(memory_space=pl.ANY))
    return pl.pallas_call(
        body, name=name, grid=(M // tm, N // tn, nk), in_specs=in_specs, out_specs=o_spec,
        out_shape=getattr(out, "arr", out), input_output_aliases=aliases,
        scratch_shapes=[pltpu.VMEM((tm, tn), F32)] if nk > 1 else [],
        compiler_params=_params(("parallel", "parallel", "arbitrary")),
    )(*ins)


def _rows_call(body, name, ins, outs, tr=512, acc_outs=()):
    n = ins[0].shape[0]
    tr = min(tr, n)
    in_specs = [pl.BlockSpec((tr, a.shape[1]), lambda i: (i, 0)) if a.shape[0] == n else
                pl.BlockSpec(a.shape, lambda i: (0, 0)) for a in ins]
    out_specs = [pl.BlockSpec((tr, w), lambda i: (i, 0)) for w, _ in outs] + \
                [pl.BlockSpec((1, w), lambda i: (0, 0)) for w in acc_outs]
    out_shape = [jax.ShapeDtypeStruct((n, w), dt) for w, dt in outs] + \
                [jax.ShapeDtypeStruct((1, w), F32) for w in acc_outs]
    res = pl.pallas_call(body, name=name, grid=(n // tr,), in_specs=in_specs, out_specs=out_specs,
                         out_shape=out_shape, compiler_params=_params(("arbitrary",)))(*ins)
    return res[0] if len(res) == 1 else res


def _acc(ref, val):
    @pl.when(pl.program_id(0) == 0)
    def _():
        ref[...] = jnp.zeros_like(ref)

    ref[...] += val


def rms_fwd(x, g, name, res=None, out_dtype=F32):
    def body(*refs):
        x_ref, g_ref = refs[0], refs[1]
        o_ref = refs[-1]
        xv = x_ref[...]
        y = xv * lax.rsqrt(jnp.mean(xv * xv, axis=-1, keepdims=True) + EPS) * g_ref[...]
        o_ref[...] = (y + refs[2][...] if res is not None else y).astype(o_ref.dtype)

    ins = [x, g] + ([res] if res is not None else [])
    return _rows_call(body, name, ins, [(x.shape[1], out_dtype)])


def rms_bwd(x, g, dy, name, res=None, out_dtype=F32, after=None):
    def body(*refs):
        x_ref, g_ref, dy_ref = refs[:3]
        dx_ref, dg_ref = refs[-2], refs[-1]
        xv, dyv = x_ref[...], dy_ref[...]
        r = lax.rsqrt(jnp.mean(xv * xv, axis=-1, keepdims=True) + EPS)
        xh = xv * r
        dxh = dyv * g_ref[...]
        dx = r * (dxh - xh * jnp.mean(dxh * xh, axis=-1, keepdims=True))
        dx_ref[...] = (dx + refs[3][...] if res is not None else dx).astype(dx_ref.dtype)
        _acc(dg_ref, jnp.sum(dyv * xh, axis=0, keepdims=True))

    ins = [x, g, dy] + ([res] if res is not None else []) + ([after] if after is not None else [])
    return _rows_call(body, name, ins, [(x.shape[1], out_dtype)], acc_outs=(x.shape[1],))


def _sigmoid(x):
    return 0.5 * jnp.tanh(0.5 * x) + 0.5


def swiglu_fwd(gu, name):
    def body(gu_ref, o_ref):
        gate = gu_ref[:, :D_FF]
        o_ref[...] = (gate * _sigmoid(gate) * gu_ref[:, D_FF:]).astype(BF16)

    return _rows_call(body, name, [gu], [(D_FF, BF16)], tr=256)


def swiglu_bwd(gu, dact, name):
    def body(gu_ref, d_ref, o_ref):
        gate, d = gu_ref[:, :D_FF], d_ref[...]
        sg = _sigmoid(gate)
        o_ref[:, :D_FF] = (d * gu_ref[:, D_FF:] * sg * (1.0 + gate * (1.0 - sg))).astype(BF16)
        o_ref[:, D_FF:] = (d * gate * sg).astype(BF16)

    return _rows_call(body, name, [gu, dact], [(2 * D_FF, BF16)], tr=256)


def ple_fwd(h, pp, gt, name):
    def body(h_ref, p_ref, g_ref, o_ref):
        o_ref[...] = h_ref[...] + p_ref[...] * _sigmoid(g_ref[...])

    return _rows_call(body, name, [h, pp, gt], [(D, F32)])


def ple_bwd(dh, pp, gt, name):
    def body(d_ref, p_ref, g_ref, dp_ref, dg_ref):
        d, sg = d_ref[...], _sigmoid(g_ref[...])
        dp_ref[...] = (d * sg).astype(BF16)
        dg_ref[...] = (d * p_ref[...] * sg * (1.0 - sg)).astype(BF16)

    return _rows_call(body, name, [dh, pp, gt], [(D, BF16), (D, BF16)])


def loss_head(y, target, name):
    def body(y_ref, t_ref, d_ref, l_ref):
        e = y_ref[...] - t_ref[...]
        d_ref[...] = e * (1.0 / D)
        col = jnp.sum(e * e, axis=0, keepdims=True) * (0.5 / D)
        _acc(l_ref, sum(col[:, LANES * c:LANES * (c + 1)] for c in range(D // LANES)))

    return _rows_call(body, name, [y, target], [(D, F32)], acc_outs=(LANES,))


def rope_tables(pos_col, name):
    inv = jnp.asarray(_rope_inv_lanes())

    def body(p_ref, inv_ref, c_ref, s_ref):
        ang = p_ref[...].astype(F32) * inv_ref[...]
        lane = lax.broadcasted_iota(jnp.int32, ang.shape, 1)
        first, second = lane < 16, (lane >= 64) & (lane < 80)
        c_ref[...] = jnp.where(first | second, jnp.cos(ang), 1.0)
        sn = jnp.sin(ang)
        s_ref[...] = jnp.where(first, -sn, jnp.where(second, sn, 0.0))

    return _rows_call(body, name, [pos_col, inv], [(LANES, F32), (LANES, F32)])


def _rope(x, c, s):
    return x * c + pltpu.roll(x, 64, axis=1) * s


def _rope_t(d, c, s):
    return d * c + pltpu.roll(d * s, 64, axis=1)


def mla_qk_fwd(qp, kvp, kr, cos, sin, name):
    def body(q_ref, k_ref, kr_ref, c_ref, s_ref, qo_ref, ko_ref):
        c, s = c_ref[...], s_ref[...]
        kr_rot = _rope(kr_ref[...], c, s)
        for h in range(HEADS):
            sl = slice(LANES * h, LANES * (h + 1))
            qo_ref[:, sl] = _rope(q_ref[:, sl], c, s).astype(BF16)
            ko_ref[:, sl] = (k_ref[:, sl] + kr_rot).astype(BF16)

    n = qp.shape[0]
    tr = 256
    w = HEADS * LANES
    return pl.pallas_call(
        body, name=name, grid=(n // tr,),
        in_specs=[pl.BlockSpec((tr, w), lambda i: (i, 0)), pl.BlockSpec((tr, w), lambda i: (i, 0)),
                  pl.BlockSpec((tr, LANES), lambda i: (i, 0)), pl.BlockSpec((tr, LANES), lambda i: (i, 0)),
                  pl.BlockSpec((tr, LANES), lambda i: (i, 0))],
        out_specs=[pl.BlockSpec((tr, w), lambda i: (i, 0))] * 2,
        out_shape=[jax.ShapeDtypeStruct((n, w), BF16)] * 2, compiler_params=_params(("arbitrary",)),
    )(qp, kvp, kr, cos, sin)


def mla_qk_bwd(dq, dk, cos, sin, name):
    def body(dq_ref, dk_ref, c_ref, s_ref, dqp_ref, dkr_ref):
        c, s = c_ref[...], s_ref[...]
        tot = jnp.zeros(c.shape, F32)
        for h in range(HEADS):
            sl = slice(LANES * h, LANES * (h + 1))
            dqp_ref[:, sl] = _rope_t(dq_ref[:, sl], c, s).astype(BF16)
            tot = tot + dk_ref[:, sl]
        dkr_ref[...] = _rope_t(tot, c, s).astype(BF16)

    return _rows_call(body, name, [dq, dk, cos, sin], [(HEADS * LANES, BF16), (LANES, BF16)])


TQ = 512


def _pair_masks(shape):
    lane = lax.broadcasted_iota(jnp.int32, shape, 1)
    return (lane < 64, lane >= 64)


def _scaled_q(q_a, scale):
    return (q_a * jnp.asarray(scale, q_a.dtype), None) if scale == 0.125 else (q_a, scale)


def _causal_probs(q_a, k_a, scale, b0, cq, ck):
    s = lax.dot_general(q_a, k_a, _DIMS["nt"], preferred_element_type=F32)
    if scale is not None:
        s = s * scale
    if cq is not None:
        s = s + (cq - ck)
    row = lax.broadcasted_iota(jnp.int32, (TQ, TQ), 0)
    col = lax.broadcasted_iota(jnp.int32, (TQ, TQ), 1)
    diag = jnp.where(col <= row, s[:, b0:], NEG)
    s = diag if b0 == 0 else jnp.concatenate([s[:, :b0], diag], axis=1)
    e = jnp.exp(s - jnp.max(s, axis=-1, keepdims=True))
    return e * (1.0 / jnp.sum(e, axis=-1, keepdims=True))


def attn_fwd(q, k, v, name, *, wide, scale, q_off=0, k_off=0, v_off=0, cum=None, cum_t=None):
    qw = 2 * LANES if wide else LANES
    forget = cum is not None

    def body(*refs):
        q_ref, k_ref, v_ref = refs[:3]
        o_ref = refs[-1]
        m0, m1 = _pair_masks((TQ, LANES))
        for qi in range(S // TQ):
            b0, b1 = qi * TQ, (qi + 1) * TQ
            outs = []
            for a, msk in enumerate((m0, m1)):
                if wide:
                    q_a, k_a = q_ref[b0:b1, LANES * a:LANES * (a + 1)], k_ref[:b1, LANES * a:LANES * (a + 1)]
                else:
                    q_a, k_a = jnp.where(msk, q_ref[b0:b1, :], jnp.zeros((), BF16)), k_ref[:b1, :]
                cq = refs[3][0, b0:b1, a:a + 1] if forget else None
                ck = refs[4][0, a:a + 1, :b1] if forget else None
                q_a, left = _scaled_q(q_a, scale)
                p = _causal_probs(q_a, k_a, left, b0, cq, ck)
                outs.append(jnp.dot(p.astype(BF16), v_ref[:b1, :], preferred_element_type=F32))
            o_ref[b0:b1, :] = jnp.where(m0, outs[0], outs[1])

    in_specs = [pl.BlockSpec((S, qw), lambda h: (0, q_off * LANES // qw + h)),
                pl.BlockSpec((S, qw), lambda h: (0, k_off * LANES // qw + h)),
                pl.BlockSpec((S, LANES), lambda h: (0, v_off + h))]
    ins = [q, k, v]
    if forget:
        in_specs += [pl.BlockSpec((1, S, 2), lambda h: (h, 0, 0)), pl.BlockSpec((1, 2, S), lambda h: (h, 0, 0))]
        ins += [cum, cum_t]
    return pl.pallas_call(
        body, name=name, grid=(PAIRS,), in_specs=in_specs, out_specs=pl.BlockSpec((S, LANES), lambda h: (0, h)),
        out_shape=jax.ShapeDtypeStruct((S, PAIRS * LANES), F32), compiler_params=_params(("arbitrary",)),
    )(*ins)


def attn_bwd(q, k, v, o, do, name, *, wide, scale, out_dtype=F32, q_off=0, k_off=0, v_off=0, cum=None, cum_t=None):
    qw = 2 * LANES if wide else LANES
    forget = cum is not None

    def body(*refs):
        q_ref, k_ref, v_ref, o_ref, do_ref = refs[:5]
        n_out = 5 if forget else 3
        outs = refs[-(n_out + 2):-2]
        dq_ref, dk_ref, dv_ref = outs[:3]
        dk_acc, dv_acc = refs[-2], refs[-1]
        dk_acc[...] = jnp.zeros_like(dk_acc)
        dv_acc[...] = jnp.zeros_like(dv_acc)
        if forget:
            dcq_ref, dck_ref = outs[3], outs[4]
            dck_ref[...] = jnp.zeros_like(dck_ref)
        m0, m1 = _pair_masks((TQ, LANES))
        for qi in range(S // TQ):
            b0, b1 = qi * TQ, (qi + 1) * TQ
            do2 = do_ref[b0:b1, :]
            dd = do2 * o_ref[b0:b1, :]
            do_b = do2.astype(BF16)
            mk0, mk1 = _pair_masks((b1, LANES))
            dqs = []
            for a, (msk, mk) in enumerate(((m0, mk0), (m1, mk1))):
                lanes = slice(LANES * a, LANES * (a + 1)) if wide else slice(0, LANES)
                if wide:
                    q_a, k_a = q_ref[b0:b1, lanes], k_ref[:b1, lanes]
                else:
                    q_a, k_a = jnp.where(msk, q_ref[b0:b1, :], jnp.zeros((), BF16)), k_ref[:b1, :]
                cq = refs[5][0, b0:b1, a:a + 1] if forget else None
                ck = refs[6][0, a:a + 1, :b1] if forget else None
                q_a, left = _scaled_q(q_a, scale)
                p = _causal_probs(q_a, k_a, left, b0, cq, ck)
                dp = lax.dot_general(jnp.where(msk, do_b, jnp.zeros((), BF16)), v_ref[:b1, :], _DIMS["nt"],
                                     preferred_element_type=F32)
                delta = jnp.sum(jnp.where(msk, dd, 0.0), axis=-1, keepdims=True)
                ds = p * (dp - delta)
                if forget:
                    dcq_ref[0, b0:b1, a:a + 1] = jnp.sum(ds, axis=-1, keepdims=True)
                    dck_ref[0, a:a + 1, :b1] -= jnp.sum(ds, axis=0, keepdims=True)
                ds_b = ds.astype(BF16)
                dqs.append(jnp.dot(ds_b, k_a, preferred_element_type=F32) * scale)
                dk_a = lax.dot_general(ds_b, q_a, _DIMS["tn"], preferred_element_type=F32)
                dk_acc[:b1, lanes] += dk_a if left is None else dk_a * scale
                dv_acc[:b1, :] += jnp.where(mk, lax.dot_general(p.astype(BF16), do_b, _DIMS["tn"],
                                                                 preferred_element_type=F32), 0.0)
            if wide:
                dq_ref[b0:b1, :LANES] = dqs[0].astype(out_dtype)
                dq_ref[b0:b1, LANES:] = dqs[1].astype(out_dtype)
            else:
                dq_ref[b0:b1, :] = jnp.where(m0, dqs[0], dqs[1]).astype(out_dtype)
        dk_ref[...] = dk_acc[...].astype(out_dtype)
        dv_ref[...] = dv_acc[...].astype(out_dtype)

    pair = pl.BlockSpec((S, LANES), lambda h: (0, h))
    qk_out = pl.BlockSpec((S, qw), lambda h: (0, h))
    in_specs = [pl.BlockSpec((S, qw), lambda h: (0, q_off * LANES // qw + h)),
                pl.BlockSpec((S, qw), lambda h: (0, k_off * LANES // qw + h)),
                pl.BlockSpec((S, LANES), lambda h: (0, v_off + h)), pair, pair]
    ins = [q, k, v, o, do]
    out_specs = [qk_out, qk_out, pair]
    out_shape = [jax.ShapeDtypeStruct((S, PAIRS * qw), out_dtype)] * 2 + \
                [jax.ShapeDtypeStruct((S, PAIRS * LANES), out_dtype)]
    if forget:
        by_q, by_k = pl.BlockSpec((1, S, 2), lambda h: (h, 0, 0)), pl.BlockSpec((1, 2, S), lambda h: (h, 0, 0))
        in_specs += [by_q, by_k]
        ins += [cum, cum_t]
        out_specs += [by_q, by_k]
        out_shape += [jax.ShapeDtypeStruct((PAIRS, S, 2), F32), jax.ShapeDtypeStruct((PAIRS, 2, S), F32)]
    return pl.pallas_call(
        body, name=name, grid=(PAIRS,), in_specs=in_specs, out_specs=out_specs, out_shape=out_shape,
        scratch_shapes=[pltpu.VMEM((S, qw), F32), pltpu.VMEM((S, LANES), F32)],
        compiler_params=_params(("arbitrary",)),
    )(*ins)


def _tri(lower):
    r = lax.broadcasted_iota(jnp.int32, (QBLK, QBLK), 0)
    c = lax.broadcasted_iota(jnp.int32, (QBLK, QBLK), 1)
    return jnp.where((c <= r) if lower else (c >= r), 1.0, 0.0).astype(F32)


def _hi_dot(a, b):
    return jnp.dot(a, b, precision=lax.Precision.HIGHEST, preferred_element_type=F32)


def fox_gate_fwd(fl, bias, name):
    def body(f_ref, b_ref, o_ref):
        tri = _tri(True)
        carry = jnp.zeros((1, LANES), F32)
        for n in range(S // QBLK):
            x = f_ref[n * QBLK:(n + 1) * QBLK, :].astype(F32) + b_ref[...]
            lf = jnp.minimum(x, 0.0) - jnp.log(1.0 + jnp.exp(-jnp.abs(x)))
            c = _hi_dot(tri, lf) + carry
            o_ref[n * QBLK:(n + 1) * QBLK, :] = c
            carry = c[QBLK - 1:QBLK, :]

    return pl.pallas_call(body, name=name, out_shape=jax.ShapeDtypeStruct((S, LANES), F32),
                          compiler_params=_params())(fl, bias)


def fox_gate_bwd(fl, bias, dcq, dck, name):
    def body(f_ref, b_ref, dq_ref, dk_ref, o_ref, db_ref):
        tri = _tri(False)
        carry = jnp.zeros((1, LANES), F32)
        db = jnp.zeros((1, LANES), F32)
        for n in reversed(range(S // QBLK)):
            rows = slice(n * QBLK, (n + 1) * QBLK)
            dlf = _hi_dot(tri, dq_ref[rows, :] + dk_ref[rows, :]) + carry
            carry = dlf[0:1, :]
            x = f_ref[rows, :].astype(F32) + b_ref[...]
            dx = dlf * (1.0 - _sigmoid(x))
            o_ref[rows, :] = dx
            db = db + jnp.sum(dx, axis=0, keepdims=True)
        db_ref[...] = db

    return pl.pallas_call(body, name=name, out_shape=[jax.ShapeDtypeStruct((S, LANES), F32),
                                                      jax.ShapeDtypeStruct((1, LANES), F32)],
                          compiler_params=_params())(fl, bias, dcq, dck)


def _band_valid(first):
    w = QBLK if first else 2 * QBLK
    i = lax.broadcasted_iota(jnp.int32, (QBLK, w), 0)
    j = lax.broadcasted_iota(jnp.int32, (QBLK, w), 1)
    return (j <= i) if first else ((j >= i) & (j - QBLK <= i))


def _band_q(q_ref, rows, msk):
    return jnp.where(msk, q_ref[rows, :], jnp.zeros((), BF16)) * jnp.asarray(0.125, BF16)


def _band_logits(q_a, kk, bias, first):
    s = lax.dot_general(q_a, kk, _DIMS["nt"], preferred_element_type=F32) + bias
    return jnp.where(_band_valid(first), s, NEG)


def residue_major(t, d):
    return t if d == 1 else t.reshape(S // d, d, t.shape[1]).transpose(1, 0, 2).reshape(S, t.shape[1])


def token_major(t, d):
    return t if d == 1 else t.reshape(d, S // d, t.shape[1]).transpose(1, 0, 2).reshape(S, t.shape[1])


def dil_fwd(qkv, bias, g, name):
    d = DIL[g][1]
    ls = S // d

    def body(q_ref, k_ref, v_ref, b_ref, o_ref, l_ref):
        m0, m1 = _pair_masks((QBLK, LANES))
        for n in range(ls // QBLK):
            rows = slice(n * QBLK, (n + 1) * QBLK)
            keys = rows if n == 0 else slice((n - 1) * QBLK, (n + 1) * QBLK)
            os_, ls_ = [], []
            for a, msk in enumerate((m0, m1)):
                q_a = _band_q(q_ref, rows, msk)
                bias_a = b_ref[a, :, QBLK:] if n == 0 else b_ref[a]
                s = _band_logits(q_a, k_ref[keys, :], bias_a, n == 0)
                mx = jnp.max(s, axis=-1, keepdims=True)
                e = jnp.exp(s - mx)
                l = jnp.sum(e, axis=-1, keepdims=True)
                os_.append(jnp.dot((e * (1.0 / l)).astype(BF16), v_ref[keys, :], preferred_element_type=F32))
                ls_.append(mx + jnp.log(l))
            o_ref[rows, :] = jnp.where(m0, os_[0], os_[1])
            l_ref[rows, :] = jnp.where(m0, ls_[0], ls_[1])

    def col(j):
        return lambda h, r: (r, j * PAIRS + h)

    out = pl.BlockSpec((ls, LANES), lambda h, r: (r, h))
    return pl.pallas_call(
        body, name=name, grid=(PAIRS, d),
        in_specs=[pl.BlockSpec((ls, LANES), col(0)), pl.BlockSpec((ls, LANES), col(1)), pl.BlockSpec((ls, LANES), col(2)),
                  pl.BlockSpec((2, QBLK, 2 * QBLK), lambda h, r: (h, 0, 0))],
        out_specs=[out, out], out_shape=[jax.ShapeDtypeStruct((S, D), F32)] * 2,
        compiler_params=_params(("arbitrary", "arbitrary")),
    )(qkv, qkv, qkv, bias)


def dil_merge(os_, lses, name):
    def body(o0, o1, o2, l0, l1, l2, o_ref, l_ref):
        ls_ = [l0[...], l1[...], l2[...]]
        mx = jnp.maximum(jnp.maximum(ls_[0], ls_[1]), ls_[2])
        tot = mx + jnp.log(sum(jnp.exp(l - mx) for l in ls_))
        o_ref[...] = sum(jnp.exp(l - tot) * o[...] for l, o in zip(ls_, (o0, o1, o2)))
        l_ref[...] = tot

    return _rows_call(body, name, list(os_) + list(lses), [(D, F32), (D, F32)])


def dil_bwd(qkv, bias, o, lse, do, g, name):
    d = DIL[g][1]
    ls = S // d

    def body(q_ref, k_ref, v_ref, b_ref, o_ref, l_ref, do_ref, dq_ref, dk_ref, dv_ref, db_ref, dk_acc, dv_acc):
        @pl.when(pl.program_id(1) == 0)
        def _():
            db_ref[...] = jnp.zeros_like(db_ref)

        dk_acc[...] = jnp.zeros_like(dk_acc)
        dv_acc[...] = jnp.zeros_like(dv_acc)
        m0, m1 = _pair_masks((QBLK, LANES))
        for n in range(ls // QBLK):
            rows = slice(n * QBLK, (n + 1) * QBLK)
            keys = rows if n == 0 else slice((n - 1) * QBLK, (n + 1) * QBLK)
            nk = QBLK if n == 0 else 2 * QBLK
            do2, lse2 = do_ref[rows, :], l_ref[rows, :]
            dd = do2 * o_ref[rows, :]
            do_b = do2.astype(BF16)
            mk0, mk1 = _pair_masks((nk, LANES))
            dqs = []
            for a, (msk, mk) in enumerate(((m0, mk0), (m1, mk1))):
                q_a = _band_q(q_ref, rows, msk)
                kk = k_ref[keys, :]
                bias_a = b_ref[a, :, QBLK:] if n == 0 else b_ref[a]
                s = _band_logits(q_a, kk, bias_a, n == 0)
                lse_a = jnp.max(jnp.where(msk, lse2, -jnp.inf), axis=-1, keepdims=True)
                p = jnp.exp(s - lse_a)
                dp = lax.dot_general(jnp.where(msk, do_b, jnp.zeros((), BF16)), v_ref[keys, :], _DIMS["nt"],
                                     preferred_element_type=F32)
                delta = jnp.sum(jnp.where(msk, dd, 0.0), axis=-1, keepdims=True)
                ds = p * (dp - delta)
                if n == 0:
                    db_ref[a, :, QBLK:] += ds
                else:
                    db_ref[a] += ds
                ds_b = ds.astype(BF16)
                dqs.append(jnp.dot(ds_b, kk, preferred_element_type=F32) * 0.125)
                dk_acc[keys, :] += lax.dot_general(ds_b, q_a, _DIMS["tn"], preferred_element_type=F32)
                dv_acc[keys, :] += jnp.where(mk, lax.dot_general(p.astype(BF16), do_b, _DIMS["tn"],
                                                                 preferred_element_type=F32), 0.0)
            dq_ref[rows, :] = jnp.where(m0, dqs[0], dqs[1]).astype(BF16)
        dk_ref[...] = dk_acc[...].astype(BF16)
        dv_ref[...] = dv_acc[...].astype(BF16)

    def col(j):
        return lambda h, r: (r, j * PAIRS + h)

    nat = pl.BlockSpec((ls, LANES), lambda h, r: (r, h))
    b_spec = pl.BlockSpec((2, QBLK, 2 * QBLK), lambda h, r: (h, 0, 0))
    return pl.pallas_call(
        body, name=name, grid=(PAIRS, d),
        in_specs=[pl.BlockSpec((ls, LANES), col(0)), pl.BlockSpec((ls, LANES), col(1)), pl.BlockSpec((ls, LANES), col(2)),
                  b_spec, nat, nat, nat],
        out_specs=[nat, nat, nat, b_spec],
        out_shape=[jax.ShapeDtypeStruct((S, D), BF16)] * 3 + [jax.ShapeDtypeStruct((HEADS, QBLK, 2 * QBLK), F32)],
        scratch_shapes=[pltpu.VMEM((ls, LANES), F32), pltpu.VMEM((ls, LANES), F32)],
        compiler_params=_params(("arbitrary", "arbitrary")),
    )(qkv, qkv, qkv, bias, o, lse, do)


def _place():
    x, y, c = lax.axis_index("x"), lax.axis_index("y"), lax.axis_index("c")
    return x, y, c


def _dev_slot(ref, by_rows, dev):
    return ref.at[:, dev] if by_rows else ref.at[dev]


def all_gather(shards, by_rows, name, in_vmem=False):
    n = len(shards)

    def body(*refs):
        x_refs, out_refs = refs[:n], refs[n:2 * n]
        send_sems, recv_sems, local_sems = refs[2 * n:]
        x, y, c = _place()
        me, sibling = (x, y, c), (x, y, 1 - c)
        chips = [(1 - x, y), (x, 1 - y), (1 - x, 1 - y)]

        def slot(t, px, py, pc):
            return _dev_slot(out_refs[t], by_rows[t], 4 * px + 2 * py + pc)

        def copy(t, k, blk, to, src=None):
            return pltpu.make_async_remote_copy(
                src_ref=slot(t, *blk) if src is None else src, dst_ref=slot(t, *blk), send_sem=send_sems.at[7 * t + k],
                recv_sem=recv_sems.at[7 * t + k], device_id=to, device_id_type=MESH_ID)

        mine = [pltpu.make_async_copy(x_refs[t], slot(t, *me), local_sems.at[t]) for t in range(n)]
        for cp in mine:
            cp.start()
        first = []
        for t in range(n):
            first.append(copy(t, 0, me, sibling, src=x_refs[t]))
            first += [copy(t, 1 + j, me, (*chip, c), src=x_refs[t]) for j, chip in enumerate(chips)]
        for cp in first:
            cp.start()
        passed = []
        for j, chip in enumerate(chips):
            for t in range(n):
                copy(t, 1 + j, (*chip, c), me).wait_recv()
                passed.append(copy(t, 4 + j, (*chip, c), sibling))
                passed[-1].start()
        for t in range(n):
            copy(t, 0, sibling, me).wait_recv()
            for j, chip in enumerate(chips):
                copy(t, 4 + j, (*chip, 1 - c), me).wait_recv()
        for cp in first + passed:
            cp.wait_send()
        for cp in mine:
            cp.wait()

    def gathered(s, rows):
        shp = (s.shape[0], N_DEV) + s.shape[1:] if rows else (N_DEV,) + s.shape
        return jax.ShapeDtypeStruct(shp, s.dtype)

    space = pl.BlockSpec(memory_space=pltpu.VMEM if in_vmem else pl.ANY)
    return pl.pallas_call(
        body, name=name, out_shape=[gathered(s, r) for s, r in zip(shards, by_rows)],
        in_specs=[space] * n, out_specs=[space] * n,
        scratch_shapes=[pltpu.SemaphoreType.DMA((7 * n,)), pltpu.SemaphoreType.DMA((7 * n,)),
                        pltpu.SemaphoreType.DMA((n,))],
        compiler_params=pltpu.CompilerParams(vmem_limit_bytes=VMEM_LIMIT),
    )(*shards)


_HBM = pl.BlockSpec(memory_space=pltpu.HBM)
_SEM = pl.BlockSpec(memory_space=pltpu.SEMAPHORE)
_SPLIT = dict(has_side_effects=pltpu.SideEffectType.DATAFLOW_SIDE_EFFECTING)


def _hbm(a):
    return pltpu.with_memory_space_constraint(a, pltpu.HBM)


def _gathered_shape(s, rows):
    return (s.shape[0], N_DEV) + s.shape[1:] if rows else (N_DEV,) + s.shape


def _peers(x, y, c):
    return [(1 - x if k & 4 else x, 1 - y if k & 2 else y, 1 - c if k & 1 else c) for k in range(1, N_DEV)]


def gather_start(shards, lands, by_rows, name):
    n = len(shards)

    def body(*refs):
        x_refs, land_refs = refs[:n], refs[n:2 * n]
        send_sems, recv_sems = refs[2 * n], refs[2 * n + 1]
        x, y, c = _place()
        me = 4 * x + 2 * y + c
        for t in range(n):
            for k, peer in enumerate(_peers(x, y, c)):
                pltpu.make_async_remote_copy(
                    src_ref=x_refs[t], dst_ref=_dev_slot(land_refs[t], by_rows[t], me), send_sem=send_sems.at[7 * t + k],
                    recv_sem=recv_sems.at[7 * t + k], device_id=peer, device_id_type=MESH_ID).start()

    sems = pltpu.SemaphoreType.DMA((7 * n,))
    res = pl.pallas_call(
        body, name=name,
        out_shape=(sems, sems) + tuple(pltpu.HBM(a.shape, a.dtype) for a in list(shards) + list(lands)),
        in_specs=[_HBM] * (2 * n), out_specs=(_SEM, _SEM) + (_HBM,) * (2 * n),
        input_output_aliases={i: 2 + i for i in range(2 * n)},
        compiler_params=pltpu.CompilerParams(**_SPLIT),
    )(*[_hbm(a) for a in list(shards) + list(lands)])
    return res[0], res[1], list(res[2:2 + n]), list(res[2 + n:])


def gather_wait(send_sems, recv_sems, first, shards, lands, by_rows, after, name):
    n = len(shards)

    def body(*refs):
        x_refs, land_refs = refs[:n], refs[n:2 * n]
        send_sems, recv_sems = refs[2 * n], refs[2 * n + 1]
        x, y, c = _place()
        for t in range(n):
            for k, (px, py, pc) in enumerate(_peers(x, y, c)):
                cp = pltpu.make_async_remote_copy(
                    src_ref=x_refs[t], dst_ref=_dev_slot(land_refs[t], by_rows[t], 4 * px + 2 * py + pc),
                    send_sem=send_sems.at[7 * (first + t) + k], recv_sem=recv_sems.at[7 * (first + t) + k],
                    device_id=(px, py, pc), device_id_type=MESH_ID)
                cp.wait_send()
                cp.wait_recv()

    res = pl.pallas_call(
        body, name=name, out_shape=tuple(pltpu.HBM(a.shape, a.dtype) for a in list(shards) + list(lands)),
        in_specs=[_HBM] * (2 * n) + [_SEM, _SEM, pl.BlockSpec(memory_space=pl.ANY)], out_specs=(_HBM,) * (2 * n),
        input_output_aliases={i: i for i in range(2 * n)},
        compiler_params=pltpu.CompilerParams(**_SPLIT),
    )(*shards, *lands, send_sems, recv_sems, after)
    return list(res[n:])


def scatter_start(srcs, src_l, lands, land_l, by_rows, name):
    n = len(srcs)

    def body(*refs):
        x_refs, land_refs = refs[:n], refs[n:2 * n]
        send_sems, recv_sems, token = refs[2 * n], refs[2 * n + 1], refs[-1]
        x, y, c = _place()
        me = 4 * x + 2 * y + c
        for k, (px, py, pc) in enumerate(_peers(x, y, c)):
            for t in range(n):
                blk = _dev_slot(x_refs[t], by_rows[t], 4 * px + 2 * py + pc)
                pltpu.make_async_remote_copy(
                    src_ref=blk.at[src_l[t]], dst_ref=land_refs[t].at[me, land_l[t]], send_sem=send_sems.at[7 * t + k],
                    recv_sem=recv_sems.at[7 * t + k], device_id=(px, py, pc), device_id_type=MESH_ID).start()
        token[...] = jnp.zeros_like(token)

    lands = [lax.empty((N_DEV, 1) + s.shape[2:], s.dtype) if ld is None else ld for s, ld in zip(srcs, lands)]
    sems = pltpu.SemaphoreType.DMA((7 * n,))
    res = pl.pallas_call(
        body, name=name,
        out_shape=(sems, sems) + tuple(pltpu.HBM(a.shape, a.dtype) for a in list(srcs) + lands)
        + (jax.ShapeDtypeStruct((8, LANES), F32),),
        in_specs=[_HBM] * (2 * n),
        out_specs=(_SEM, _SEM) + (_HBM,) * (2 * n) + (pl.BlockSpec(memory_space=pltpu.VMEM),),
        input_output_aliases={i: 2 + i for i in range(2 * n)},
        compiler_params=pltpu.CompilerParams(**_SPLIT),
    )(*[_hbm(a) for a in list(srcs) + lands])
    return res[0], res[1], list(res[2:2 + n]), list(res[2 + n:2 + 2 * n]), res[-1]


def scatter_wait(send_sems, recv_sems, srcs, src_l, lands, land_l, by_rows, after, name):
    n = len(srcs)

    def body(*refs):
        x_refs, land_refs = refs[:n], refs[n:2 * n]
        send_sems, recv_sems = refs[2 * n], refs[2 * n + 1]
        x, y, c = _place()
        for k, (px, py, pc) in enumerate(_peers(x, y, c)):
            peer = 4 * px + 2 * py + pc
            for t in range(n):
                cp = pltpu.make_async_remote_copy(
                    src_ref=_dev_slot(x_refs[t], by_rows[t], peer).at[src_l[t]], dst_ref=land_refs[t].at[peer, land_l[t]],
                    send_sem=send_sems.at[7 * t + k], recv_sem=recv_sems.at[7 * t + k], device_id=(px, py, pc),
                    device_id_type=MESH_ID)
                cp.wait_send()
                cp.wait_recv()

    res = pl.pallas_call(
        body, name=name, out_shape=tuple(pltpu.HBM(a.shape, a.dtype) for a in list(srcs) + list(lands)),
        in_specs=[_HBM] * (2 * n) + [_SEM, _SEM, pl.BlockSpec(memory_space=pl.ANY)], out_specs=(_HBM,) * (2 * n),
        input_output_aliases={i: i for i in range(2 * n)},
        compiler_params=pltpu.CompilerParams(**_SPLIT),
    )(*srcs, *lands, send_sems, recv_sems, after)
    return list(res[:n]), list(res[n:])


ADAM_BLOCK_BYTES = 3 << 19


def adamw(w, m, v, parts, name):
    n_parts = parts.shape[0]
    n_l, r, c = w.shape
    lane_c = -(-c // LANES) * LANES
    fits = [t for t in range(16, r, 16) if r % t == 0 and t * lane_c * 4 <= ADAM_BLOCK_BYTES]
    tr = max(fits) if fits and r * lane_c * 4 > ADAM_BLOCK_BYTES else r
    c1 = 1.0 / (1.0 - ADAM_B1 ** ADAM_STEP)
    c2 = 1.0 / (1.0 - ADAM_B2 ** ADAM_STEP)

    def body(w_ref, m_ref, v_ref, p_ref, g_ref, d_ref, nm_ref, nv_ref):
        g = p_ref[0].astype(F32)
        for j in range(1, n_parts):
            g = g + p_ref[j].astype(F32)
        nm = ADAM_B1 * m_ref[...] + (1.0 - ADAM_B1) * g
        nv = ADAM_B2 * v_ref[...] + (1.0 - ADAM_B2) * (g * g)
        g_ref[...] = g
        nm_ref[...] = nm
        nv_ref[...] = nv
        d_ref[...] = -ADAM_LR * ((nm * c1) / (jnp.sqrt(nv * c2) + ADAM_EPS) + ADAM_WD * w_ref[...])

    blk = pl.BlockSpec((1, tr, c), lambda l, i: (l, i, 0))
    return pl.pallas_call(
        body, name=name, grid=(n_l, r // tr),
        in_specs=[blk, blk, blk, pl.BlockSpec((n_parts, 1, tr, c), lambda l, i: (0, l, i, 0))],
        out_specs=[blk] * 4, out_shape=[jax.ShapeDtypeStruct((n_l, r, c), F32)] * 4,
        compiler_params=_params(("parallel", "parallel")),
    )(w, m, v, parts)


def sum_parts(parts, name):
    def body(p_ref, o_ref):
        g = p_ref[0]
        for j in range(1, parts.shape[0]):
            g = g + p_ref[j]
        o_ref[...] = g

    return pl.pallas_call(body, name=name, out_shape=jax.ShapeDtypeStruct(parts.shape[1:], F32),
                          compiler_params=_params())(parts)


def _pack(arrs, rows, dtype):
    flat = jnp.concatenate([a.reshape(-1).astype(dtype) for a in arrs])
    return jnp.pad(flat, (0, rows * LANES - flat.shape[0])).reshape(rows, LANES)


def _unpack(packed, shapes):
    flat, out, off = packed.reshape(-1), [], 0
    for shp in shapes:
        n = int(np.prod(shp))
        out.append(flat[off:off + n].reshape(shp))
        off += n
    return out


def _cat(parts):
    return jnp.concatenate(parts, axis=1)


def _layer_list(i):
    kind, j = i % 3, i // 3
    mix = ([("mla_w_a", j), ("mla_w_uq", j), ("mla_w_ukv", j), ("mla_w_o", j)] if kind == 0 else
           [("dil_w_qkv", 0), ("dil_w_o", 0)] if kind == 1 else [("fox_w_qkvf", 0), ("fox_w_o", 0)])
    return mix + [("ffn_w_in", i), ("ffn_w_out", i), ("ple_w_proj", i), ("ple_w_gate", i)]


def _mla_layout(w_a, g_uq, g_ukv, j):
    def z(r, n):
        return jnp.zeros((r, n), BF16)

    wa = w_a[j]
    a = _cat([wa[:, :640], wa[:, 640:656], z(D, 48), wa[:, 656:672], z(D, 48)])
    q, k, v = [], [], []
    for h in range(HEADS):
        b = g_uq[h // 2, j][:, 96 * (h % 2):96 * (h % 2 + 1)]
        q += [b[:, 64:80], b[:, 0:32], z(Q_RANK, 16), b[:, 80:96], b[:, 32:64], z(Q_RANK, 16)]
        b = g_ukv[h // 2, j][:, LANES * (h % 2):LANES * (h % 2 + 1)]
        k += [z(KV_RANK, 16), b[:, 0:32], z(KV_RANK, 32), b[:, 32:64], z(KV_RANK, 16)]
        v.append(b[:, 64:128])
    return a, _cat(q), _cat(k + v)


def _mla_unlayout(d_a, d_uq, d_ukv):
    a = _cat([d_a[:, :640], d_a[:, 640:656], d_a[:, 704:720]])
    uq, ukv = [], []
    for dev in range(N_DEV):
        q, kv = [], []
        for h in (2 * dev, 2 * dev + 1):
            b = d_uq[:, LANES * h:LANES * (h + 1)]
            q += [b[:, 16:48], b[:, 80:112], b[:, 0:16], b[:, 64:80]]
            b = d_ukv[:, LANES * h:LANES * (h + 1)]
            kv += [b[:, 16:48], b[:, 80:112], d_ukv[:, HEADS * LANES + 64 * h:HEADS * LANES + 64 * (h + 1)]]
        uq.append(_cat(q))
        ukv.append(_cat(kv))
    return a, jnp.stack(uq), jnp.stack(ukv)


def _mixer_fwd(kind, tag, hn, W, aux):
    if kind == 0:
        a = mm(hn, W["w_a"], "nn", f"{tag}_a", tn=768)
        cq = rms_fwd(a[:, :Q_RANK], W["q_norm"], f"{tag}_cq", out_dtype=BF16)
        ckv = rms_fwd(a[:, Q_RANK:Q_RANK + KV_RANK], W["kv_norm"], f"{tag}_ckv", out_dtype=BF16)
        qp = mm(cq, W["w_uq"], "nn", f"{tag}_uq", tk=384)
        kvp = mm(ckv, W["w_ukv"], "nn", f"{tag}_ukv", tk=256)
        q, k = mla_qk_fwd(qp, kvp, a[:, 640:], aux["cos"], aux["sin"], f"{tag}_qk")
        v = kvp.astype(BF16)
        o = attn_fwd(q, k, v, f"{tag}_attn", wide=True, scale=96 ** -0.5, v_off=HEADS)
        y = mm(o, W["w_o"], "nn", f"{tag}_o")
        return y, (a, cq, ckv, q, k, v, o)
    if kind == 1:
        qkv = [mm(residue_major(hn, DIL[g][1]), W["w_qkv"], "nn", f"{tag}_qkv{g}", out_dtype=BF16, tn=384,
                  b_cols=(8 * g, 8)) for g in range(3)]
        parts = [dil_fwd(qkv[g], aux["dil_bias"][g], g, f"{tag}_g{g}") for g in range(3)]
        o, lse = dil_merge([token_major(p_[0], DIL[g][1]) for g, p_ in enumerate(parts)],
                           [token_major(p_[1], DIL[g][1]) for g, p_ in enumerate(parts)], f"{tag}_merge")
        y = mm(o, W["w_o"], "nn", f"{tag}_o")
        return y, (qkv, o, lse)
    a = mm(hn, W["w_qkvf"], "nn", f"{tag}_qkvf", tn=640)
    fl = a[:, 3072:]
    cum = fox_gate_fwd(fl, aux["fox_b"], f"{tag}_gate")[:, :HEADS]
    cum_q = cum.reshape(S, PAIRS, 2).transpose(1, 0, 2)
    cum_k = cum.T.reshape(PAIRS, 2, S)
    ab = a.astype(BF16)
    o = attn_fwd(ab, ab, ab, f"{tag}_attn", wide=False, scale=0.125, k_off=PAIRS, v_off=2 * PAIRS, cum=cum_q, cum_t=cum_k)
    y = mm(o, W["w_o"], "nn", f"{tag}_o")
    return y, (fl, ab, cum_q, cum_k, o)


def _mixer_bwd(kind, tag, hn, dy, W, aux, saved):
    gr = {}
    if kind == 0:
        a, cq, ckv, q, k, v, o = saved
        gr["w_o"] = mm(o, dy, "tn", f"{tag}_dwo", out_dtype=BF16)
        do = mm(dy, W["w_o"], "nt", f"{tag}_do")
        dq, dk, dv = attn_bwd(q, k, v, o, do, f"{tag}_attn_b", wide=True, scale=96 ** -0.5, v_off=HEADS)
        dqp, dkr = mla_qk_bwd(dq, dk, aux["cos"], aux["sin"], f"{tag}_qk_b")
        dkvp = jnp.concatenate([dk, dv], axis=1)
        gr["w_ukv"] = mm(ckv, dkvp, "tn", f"{tag}_dwukv", out_dtype=BF16, tm=256)
        dckv = mm(dkvp, W["w_ukv"], "nt", f"{tag}_dckv", tn=256)
        gr["w_uq"] = mm(cq, dqp, "tn", f"{tag}_dwuq", out_dtype=BF16, tm=384)
        dcq = mm(dqp, W["w_uq"], "nt", f"{tag}_dcq", tn=384)
        da_q, gr["q_norm"] = rms_bwd(a[:, :Q_RANK], W["q_norm"], dcq, f"{tag}_cq_b", out_dtype=BF16)
        da_kv, gr["kv_norm"] = rms_bwd(a[:, Q_RANK:Q_RANK + KV_RANK], W["kv_norm"], dckv, f"{tag}_ckv_b",
                                       out_dtype=BF16)
        da = jnp.concatenate([da_q, da_kv, dkr], axis=1)
        gr["w_a"] = mm(hn, da, "tn", f"{tag}_dwa", out_dtype=BF16, tn=768)
        return mm(da, W["w_a"], "nt", f"{tag}_dhn", tk=768), gr
    if kind == 1:
        qkv, o, lse = saved
        gr["w_o"] = mm(o, dy, "tn", f"{tag}_dwo", out_dtype=BF16)
        do = mm(dy, W["w_o"], "nt", f"{tag}_do")
        cols, dbs = [], []
        for g, (_, d) in enumerate(DIL):
            dq, dk, dv, db = dil_bwd(qkv[g], aux["dil_bias"][g], residue_major(o, d), residue_major(lse, d),
                                     residue_major(do, d), g, f"{tag}_g{g}_b")
            cols += [token_major(t, d) for t in (dq, dk, dv)]
            dbs.append(db)
        dqkv = jnp.concatenate(cols, axis=1)
        gr["dil_dbias"] = dbs
        gr["w_qkv"] = mm(hn, dqkv, "tn", f"{tag}_dwqkv", out_dtype=BF16, out_dev=1152, tn=1152)
        return mm(dqkv, W["w_qkv"], "nt", f"{tag}_dhn", tk=1152), gr
    fl, ab, cum_q, cum_k, o = saved
    gr["w_o"] = mm(o, dy, "tn", f"{tag}_dwo", out_dtype=BF16)
    do = mm(dy, W["w_o"], "nt", f"{tag}_do")
    dq, dk, dv, dcq, dck = attn_bwd(ab, ab, ab, o, do, f"{tag}_attn_b", wide=False, scale=0.125, out_dtype=BF16,
                                    k_off=PAIRS, v_off=2 * PAIRS, cum=cum_q, cum_t=cum_k)
    pad = ((0, 0), (0, LANES - HEADS))
    dcq = jnp.pad(dcq.transpose(1, 0, 2).reshape(S, HEADS), pad)
    dck = jnp.pad(dck.reshape(HEADS, S).T, pad)
    dfl, gr["b_f"] = fox_gate_bwd(fl, aux["fox_b"], dcq, dck, f"{tag}_gate_b")
    da = jnp.concatenate([dq, dk, dv, dfl.astype(BF16)], axis=1)
    gr["w_qkvf"] = mm(hn, da, "tn", f"{tag}_dwqkvf", out_dtype=BF16, tn=640)
    return mm(da, W["w_qkvf"], "nt", f"{tag}_dhn", tk=640), gr


def kernel(x, p, positions, norm_g, ffn_w_in, ffn_w_out, ple_w_proj, ple_w_gate, rel_bias, mla_w_a, mla_q_norm, mla_kv_norm, mla_w_uq, mla_w_ukv, mla_w_o, dil_w_qkv, dil_w_o, fox_w_qkvf, fox_b_f, fox_w_o, loss_target, m_norm_g, m_ffn_w_in, m_ffn_w_out, m_ple_w_proj, m_ple_w_gate, m_rel_bias, m_mla_w_a, m_mla_q_norm, m_mla_kv_norm, m_mla_w_uq, m_mla_w_ukv, m_mla_w_o, m_dil_w_qkv, m_dil_w_o, m_fox_w_qkvf, m_fox_b_f, m_fox_w_o, v_norm_g, v_ffn_w_in, v_ffn_w_out, v_ple_w_proj, v_ple_w_gate, v_rel_bias, v_mla_w_a, v_mla_q_norm, v_mla_kv_norm, v_mla_w_uq, v_mla_w_ukv, v_mla_w_o, v_dil_w_qkv, v_dil_w_o, v_fox_w_qkvf, v_fox_b_f, v_fox_w_o):
    given = dict(locals())
    me = 4 * lax.axis_index("x") + 2 * lax.axis_index("y") + lax.axis_index("c")
    for n in TRANSPOSED:
        for pre in ("", "m_", "v_"):
            given[pre + n] = jnp.swapaxes(given[pre + n], 1, 2)

    rows_of = {n: axis == 1 for n, _, axis in BIG}
    rows_of["gains"] = False
    shape_of = {n: shp for n, shp, _ in BIG}
    lists = [_layer_list(i) for i in range(DEPTH)]
    lists[0] = [("gains", 0)] + lists[0]
    flat = [nl for ls in lists for nl in ls]
    flat_rows = [rows_of[n] for n, _ in flat]
    gain_rows = _rows(sum(int(np.prod(s)) for _, s, _ in SMALL_SHARDED))
    shards = [_pack([given[k] for k, _, _ in SMALL_SHARDED], gain_rows, F32)[None] if n == "gains" else
              given[n][l:l + 1].astype(BF16) for n, l in flat]
    lands = [lax.dynamic_update_slice(lax.empty(_gathered_shape(s, r), s.dtype), s[:, None] if r else s[None],
                                      (0, me, 0, 0) if r else (me, 0, 0, 0)) for s, r in zip(shards, flat_rows)]
    send_s, recv_s, shards, lands = gather_start(shards, lands, flat_rows, "gather_start")

    cos, sin = rope_tables(positions.reshape(S, 1), "rope_tables")
    dil_bias = [mm(rel_bias[:, HEADS * g:HEADS * (g + 1)], jnp.asarray(_bucket_onehot(DIL[g][1])), "tn",
                   f"dil_bias{g}", precise=True, tn=4096).reshape(HEADS, QBLK, 2 * QBLK) for g in range(3)]
    aux = {"cos": cos, "sin": sin, "dil_bias": dil_bias,
           "fox_b": jnp.pad(fox_b_f, ((0, 0), (0, LANES - HEADS)))}

    def arrived(i, part, behind):
        n_mix = len(lists[i]) - 4
        first = sum(len(ls) for ls in lists[:i]) + (n_mix if part else 0)
        sl = slice(first, first + (4 if part else n_mix))
        got = gather_wait(send_s, recv_s, first, shards[sl], lands[sl], flat_rows[sl], behind, f"gather_wait{i}_{part}")
        return {n: g.reshape(1, N_DEV * shape_of[n][1], shape_of[n][2]) if rows_of[n] else g
                for (n, _), g in zip(flat[sl], got)}

    full = {}

    def mixer_weights(i, behind):
        kind, j = i % 3, i // 3
        w = arrived(i, 0, behind)
        if i == 0:
            gains, off = w["gains"].reshape(N_DEV, gain_rows * LANES), 0
            for n, shp, axis in SMALL_SHARDED:
                cnt = int(np.prod(shp))
                g = jnp.moveaxis(gains[:, off:off + cnt].reshape((N_DEV,) + shp), 0, axis)
                full[n] = g.reshape(shp[:axis] + (N_DEV * shp[axis],))
                off += cnt
        W = {"g": [full["norm_g"][i, r][None, :] for r in range(4)]}
        if kind == 0:
            w_a, w_uq, w_ukv = _mla_layout(w["mla_w_a"], w["mla_w_uq"], w["mla_w_ukv"], 0)
            W.update(w_a=w_a, w_uq=w_uq, w_ukv=w_ukv, w_o=Lay(w["mla_w_o"], 0),
                     q_norm=full["mla_q_norm"][j][None, :], kv_norm=full["mla_kv_norm"][j][None, :])
        elif kind == 1:
            W.update(w_qkv=Dev(w["dil_w_qkv"], 0), w_o=Lay(w["dil_w_o"], 0))
        else:
            fox_w = jnp.pad(_cat([w["fox_w_qkvf"][dev, 0] for dev in range(N_DEV)]), ((0, 0), (0, FOX_W - 3088)))
            W.update(w_qkvf=fox_w, w_o=Lay(w["fox_w_o"], 0))
        return W

    def ffn_weights(i, behind):
        w = arrived(i, 1, behind)
        return {"w_in_t": Lay(w["ffn_w_in"], 0), "w_out": Lay(w["ffn_w_out"], 0), "w_proj": Dev(w["ple_w_proj"], 0),
                "w_gate": Lay(w["ple_w_gate"], 0)}

    h = x[0]
    saved, weights = [], []
    for i in range(DEPTH):
        kind, j, W = i % 3, i // 3, mixer_weights(i, h)
        weights.append(W)
        t = f"l{i}"
        hn = rms_fwd(h, W["g"][0], f"{t}_n0", out_dtype=BF16)
        y, mix = _mixer_fwd(kind, f"{t}_mix", hn, W, aux)
        W.update(ffn_weights(i, y))
        h1 = rms_fwd(y, W["g"][1], f"{t}_n1", res=h)
        fin = rms_fwd(h1, W["g"][2], f"{t}_n2", out_dtype=BF16)
        gu = mm(fin, W["w_in_t"], "nt", f"{t}_ffn_in")
        act = swiglu_fwd(gu, f"{t}_swiglu")
        f = mm(act, W["w_out"], "nn", f"{t}_ffn_out")
        h2 = rms_fwd(f, W["g"][3], f"{t}_n3", res=h1)
        pp = mm(p[i, 0], W["w_proj"], "nn", f"{t}_ple_p", tn=LANES, tk=256)
        gt = mm(h2, W["w_gate"], "nn", f"{t}_ple_g")
        h3 = ple_fwd(h2, pp, gt, f"{t}_ple")
        saved.append((h, hn, y, h1, fin, gu, act, f, h2, pp, gt, mix))
        h = h3

    dh, loss_lanes = loss_head(h, loss_target[0], "loss_head")

    grads = {n: None for n, _, _ in BIG}
    landed = {n: (lax.empty((N_DEV,) + shp, BF16) if shp[0] > 1 else None) for n, shp, _ in BIG}
    in_flight = []
    own = {n: [] for n, _, _ in BIG}
    g_norm = [[None] * 4 for _ in range(DEPTH)]
    g_qn, g_kvn = [None, None], [None, None]
    g_rel, g_bf = None, None

    def stacked(n):
        g = grads[n]
        return None if g is None else g.reshape(g.shape[0], N_DEV * g.shape[2], g.shape[3])

    def by_device(g):
        return g.reshape(g.shape[0], N_DEV, g.shape[1] // N_DEV, g.shape[2])

    def start(i, entries, mine, tag):
        names = [n for n, _ in entries]
        srcs = [mine[n] if n in mine else grads[n] for n in names]
        src_l = [0 if n in mine else i for n in names]
        land_l = [l if shape_of[n][0] > 1 else 0 for n, l in entries]
        rows = [rows_of[n] for n in names]
        s_sem, r_sem, srcs, got, token = scatter_start(srcs, src_l, [landed[n] for n in names], land_l, rows, tag)
        for n, src, ld in zip(names, srcs, got):
            landed[n] = ld
            if n in mine:
                mine[n] = src
            else:
                grads[n] = src
        in_flight.append((s_sem, r_sem, names, mine, src_l, land_l, rows))
        return token

    token = None
    for i in reversed(range(DEPTH)):
        kind, j, W = i % 3, i // 3, weights[i]
        t = f"l{i}b"
        h0, hn, y, h1, fin, gu, act, f, h2, pp, gt, mix = saved[i]
        dpp, dgt = ple_bwd(dh, pp, gt, f"{t}_ple")
        grads["ple_w_proj"] = mm(p[i, 0], dpp, "tn", f"{t}_dwp", out_dtype=BF16, out_dev=LANES, tm=256, tn=LANES,
                                 stack=(grads["ple_w_proj"], DEPTH, i), after=token)
        grads["ple_w_gate"] = by_device(mm(h2, dgt, "tn", f"{t}_dwg", out_dtype=BF16,
                                           stack=(stacked("ple_w_gate"), DEPTH, i)))
        dh2 = mm(dgt, W["w_gate"], "nt", f"{t}_dh2", add=dh)
        df, g_norm[i][3] = rms_bwd(f, W["g"][3], dh2, f"{t}_n3", out_dtype=BF16)
        grads["ffn_w_out"] = by_device(mm(act, df, "tn", f"{t}_dwout", out_dtype=BF16, tm=1408,
                                          stack=(stacked("ffn_w_out"), DEPTH, i)))
        dact = mm(df, W["w_out"], "nt", f"{t}_dact", tn=1408)
        dgu = swiglu_bwd(gu, dact, f"{t}_swiglu")
        grads["ffn_w_in"] = by_device(mm(dgu, fin, "tn", f"{t}_dwin", out_dtype=BF16, tm=512, tn=1024,
                                         stack=(stacked("ffn_w_in"), DEPTH, i)))
        token = start(i, lists[i][-4:], {}, f"scatter_ffn{i}")
        dfin = mm(dgu, W["w_in_t"], "nn", f"{t}_dfin", after=token, tk=1408)
        dh1, g_norm[i][2] = rms_bwd(h1, W["g"][2], dfin, f"{t}_n2", res=dh2)
        dy, g_norm[i][1] = rms_bwd(y, W["g"][1], dh1, f"{t}_n1", out_dtype=BF16)
        dhn, gr = _mixer_bwd(kind, f"{t}_mix", hn, dy, W, aux, mix)
        if kind == 0:
            d_a, d_uq, d_ukv = _mla_unlayout(gr["w_a"], gr["w_uq"], gr["w_ukv"])
            mine = {"mla_w_a": by_device(d_a[None]), "mla_w_uq": d_uq[:, None], "mla_w_ukv": d_ukv[:, None],
                    "mla_w_o": by_device(gr["w_o"][None])}
            g_qn[j], g_kvn[j] = gr["q_norm"], gr["kv_norm"]
        elif kind == 1:
            mine = {"dil_w_qkv": gr["w_qkv"], "dil_w_o": by_device(gr["w_o"][None])}
            g_rel = jnp.concatenate(
                [mm(jnp.asarray(_bucket_onehot(DIL[g][1])), gr["dil_dbias"][g].reshape(HEADS, -1), "nt",
                    f"{t}_drel{g}", precise=True, tk=4096) for g in range(3)], axis=1)
        else:
            wide = gr["w_qkvf"]
            mine = {"fox_w_qkvf": jnp.stack([wide[:, 386 * dev:386 * (dev + 1)] for dev in range(N_DEV)])[:, None],
                    "fox_w_o": by_device(gr["w_o"][None])}
            g_bf = gr["b_f"][:, :HEADS]
        token = start(i, [e for e in lists[i][:-4] if e[0] in mine], mine, f"scatter_mix{i}")
        dh, g_norm[i][0] = rms_bwd(h0, W["g"][0], dhn, f"{t}_n0", res=dh1, after=token)
    grad_x = dh[None]

    for idx, (s_sem, r_sem, names, mine, src_l, land_l, rows) in enumerate(in_flight):
        srcs = [mine[n] if n in mine else grads[n] for n in names]
        srcs, got = scatter_wait(s_sem, r_sem, srcs, src_l, [landed[n] for n in names], land_l, rows, dh,
                                 f"scatter_wait{idx}")
        for n, src, ld in zip(names, srcs, got):
            landed[n] = ld
            if n in mine:
                mine[n] = src
            else:
                grads[n] = src
    for _, _, names, mine, src_l, land_l, rows in in_flight:
        for n, sl, ll, rw in zip(names, src_l, land_l, rows):
            src = mine[n] if n in mine else grads[n]
            own[n].append((ll, lax.dynamic_index_in_dim(src, me, axis=1 if rw else 0, keepdims=False)[sl]))
    big_out = []
    for n, _, _ in BIG:
        part = landed[n]
        for ll, blk in own[n]:
            part = lax.dynamic_update_slice(part, blk[None, None], (me, ll, 0, 0))
        big_out.append(adamw(given[n], given["m_" + n], given["v_" + n], part, f"adamw_{n}"))

    small_full = [jnp.stack([jnp.concatenate(r, axis=0) for r in g_norm]).reshape(-1),
                  jnp.concatenate(g_qn, axis=0).reshape(-1), jnp.concatenate(g_kvn, axis=0).reshape(-1),
                  g_rel.reshape(-1), g_bf.reshape(-1), loss_lanes.reshape(-1)]
    small_n = sum(a.shape[0] for a in small_full)
    small_rows = _rows(small_n)
    parts, = all_gather([_pack(small_full, small_rows, F32)], [False], "gather_small_grads", in_vmem=True)
    tot = _unpack(sum_parts(parts, "sum_small_grads"), [(4, 4, D), (2, Q_RANK), (2, KV_RANK), (32, 48), (1, 16), (LANES,)])
    loss = jnp.sum(tot[5])
    small_g = [lax.dynamic_slice_in_dim(tot[0], me * 128, 128, axis=2), lax.dynamic_slice_in_dim(tot[1], me * 48, 48, axis=1),
               lax.dynamic_slice_in_dim(tot[2], me * 32, 32, axis=1), tot[3], tot[4]]
    small_names = [n for n, _, _ in SMALL_SHARDED] + [n for n, _ in SMALL_REPL]
    small_shapes = [s for _, s, _ in SMALL_SHARDED] + [s for _, s in SMALL_REPL]
    s_rows = _rows(sum(int(np.prod(s)) for s in small_shapes))
    small_out = adamw(_pack([given[n] for n in small_names], s_rows, F32)[None],
                      _pack([given["m_" + n] for n in small_names], s_rows, F32)[None],
                      _pack([given["v_" + n] for n in small_names], s_rows, F32)[None],
                      _pack(small_g, s_rows, F32)[None, None], "adamw_small")
    small_out = [_unpack(o_, small_shapes) for o_ in small_out]

    res = [{}, {}, {}, {}]
    for k in range(4):
        for idx, (n, _, _) in enumerate(BIG):
            res[k][n] = jnp.swapaxes(big_out[idx][k], 1, 2) if n in TRANSPOSED else big_out[idx][k]
        for idx, n in enumerate(small_names):
            res[k][n] = small_out[k][idx]
    return (loss, grad_x, *[res[0][n] for n in WEIGHTS], *[res[1][n] for n in WEIGHTS],
            *[res[2][n] for n in WEIGHTS], *[res[3][n] for n in WEIGHTS])
```

```python
import math
from typing import NamedTuple

import numpy as np
import jax
import jax.numpy as jnp
from jax import lax
from jax.experimental import pallas as pl
from jax.experimental.pallas import tpu as pltpu

F32 = jnp.float32
BF16 = jnp.bfloat16
MESH_ID = pl.DeviceIdType.MESH

N_DEV = 8
S = 2048
D = 1024
DEPTH = 4
D_FF = 2816
D_PLE = 256
EPS = 1e-6
NEG = -1e30
LANES = 128
HEADS = 16
PAIRS = 8
Q_RANK = 384
KV_RANK = 256
QBLK = 128
DIL = ((128, 1), (512, 4), (2048, 16))
REL_BUCKETS = 32
FOX_W = 3200
VMEM_LIMIT = 56 * 1024 * 1024

ADAM_LR, ADAM_B1, ADAM_B2, ADAM_EPS, ADAM_WD, ADAM_STEP = 1e-3, 0.9, 0.999, 1e-8, 0.01, 10


TRANSPOSED = ("ffn_w_in",)
BIG = (
    ("ffn_w_in", (4, 704, 1024), 1), ("ffn_w_out", (4, 352, 1024), 1),
    ("ple_w_proj", (4, 256, 128), 2), ("ple_w_gate", (4, 128, 1024), 1),
    ("mla_w_a", (2, 128, 672), 1), ("mla_w_uq", (2, 384, 192), 2),
    ("mla_w_ukv", (2, 256, 256), 2), ("mla_w_o", (2, 128, 1024), 1),
    ("dil_w_qkv", (1, 1024, 1152), 2), ("dil_w_o", (1, 128, 1024), 1),
    ("fox_w_qkvf", (1, 1024, 386), 2), ("fox_w_o", (1, 128, 1024), 1),
)
SMALL_SHARDED = (("norm_g", (4, 4, 128), 2), ("mla_q_norm", (2, 48), 1), ("mla_kv_norm", (2, 32), 1))
SMALL_REPL = (("rel_bias", (32, 48)), ("fox_b_f", (1, 16)))
WEIGHTS = ("norm_g", "ffn_w_in", "ffn_w_out", "ple_w_proj", "ple_w_gate", "rel_bias", "mla_w_a", "mla_q_norm",
           "mla_kv_norm", "mla_w_uq", "mla_w_ukv", "mla_w_o", "dil_w_qkv", "dil_w_o", "fox_w_qkvf", "fox_b_f",
           "fox_w_o")


def _rows(n):
    return -(-n // (8 * LANES)) * 8


def _t5_bucket_np(dist):
    max_exact = REL_BUCKETS // 2
    n = np.maximum(dist.astype(np.float32), np.float32(1.0))
    large = max_exact + (np.log(n / np.float32(max_exact)) / np.float32(math.log(2048 / max_exact))
                         * np.float32(REL_BUCKETS - max_exact)).astype(np.int32)
    large = np.minimum(large, REL_BUCKETS - 1)
    return np.where(dist < max_exact, dist, large)


def _bucket_onehot(dilation):
    i = np.arange(QBLK)[:, None]
    j = np.arange(2 * QBLK)[None, :]
    bucket = _t5_bucket_np(np.clip(QBLK + i - j, 0, None) * dilation).reshape(-1)
    return (np.arange(REL_BUCKETS)[:, None] == bucket[None, :]).astype(np.float32)


def _rope_inv_lanes():
    half = 16
    inv = (np.float32(10000.0) ** (-np.arange(half, dtype=np.float32) / np.float32(half))).astype(np.float32)
    t = np.zeros((1, LANES), np.float32)
    t[0, 0:16] = inv
    t[0, 64:80] = inv
    return t


def _params(sem=None):
    return pltpu.CompilerParams(dimension_semantics=sem, vmem_limit_bytes=VMEM_LIMIT)


def _tile(dim, target):
    if dim <= target or dim % target == 0:
        return min(dim, target)
    t = (target // LANES) * LANES
    while dim % t:
        t -= LANES
    return t


_DIMS = {"nn": (((1,), (0,)), ((), ())), "nt": (((1,), (1,)), ((), ())), "tn": (((0,), (0,)), ((), ()))}


class Lay(NamedTuple):
    arr: jax.Array
    l: int


class Dev(NamedTuple):
    arr: jax.Array
    l: int


def _lshape(op):
    if isinstance(op, Dev):
        g, _, r, w = op.arr.shape
        return r, g * w
    return op.arr.shape[1:] if isinstance(op, Lay) else op.shape


def _op_spec(op, rows_t, cols_t, row_ix, col_ix):
    if isinstance(op, Dev):
        w = op.arr.shape[3]
        assert w % cols_t == 0 and (cols_t % LANES == 0 or cols_t == w), (w, cols_t)
        nb, l = w // cols_t, op.l
        return pl.BlockSpec((1, 1, rows_t, cols_t),
                            lambda i, j, k: (col_ix(i, j, k) // nb, l, row_ix(i, j, k), col_ix(i, j, k) % nb))
    if isinstance(op, Lay):
        l = op.l
        return pl.BlockSpec((1, rows_t, cols_t), lambda i, j, k: (l, row_ix(i, j, k), col_ix(i, j, k)))
    return pl.BlockSpec((rows_t, cols_t), lambda i, j, k: (row_ix(i, j, k), col_ix(i, j, k)))


def _mat(ref):
    return ref[(0,) * (len(ref.shape) - 2)]


def mm(a, b, mode, name, out_dtype=F32, precise=False, add=None, out_dev=None, stack=None, after=None, b_cols=None,
       tm=1024, tn=512, tk=2048):
    (ar, ac), (br, bc) = _lshape(a), _lshape(b)
    M, K = (ac, ar) if mode == "tn" else (ar, ac)
    N = br if mode == "nt" else bc
    assert K == (bc if mode == "nt" else br)
    tm, tn, tk = _tile(M, tm), _tile(N, tn), _tile(K, tk)
    nk = K // tk
    j0, n_blocks = b_cols if b_cols is not None else (0, N // tn)
    N = n_blocks * tn
    ix_i, ix_j, ix_k = (lambda i, j, k: i), (lambda i, j, k: j), (lambda i, j, k: k)
    ix_jb = lambda i, j, k: j + j0
    a_spec = _op_spec(a, tk, tm, ix_k, ix_i) if mode == "tn" else _op_spec(a, tm, tk, ix_i, ix_k)
    b_spec = _op_spec(b, tn, tk, ix_jb, ix_k) if mode == "nt" else _op_spec(b, tk, tn, ix_k, ix_jb)
    buf, n_l, l = stack if stack is not None else (None, 1, 0)
    if out_dev is not None:
        out = Dev(jax.ShapeDtypeStruct((N // out_dev, n_l, M, out_dev), out_dtype), l)
    elif stack is not None:
        out = Lay(jax.ShapeDtypeStruct((n_l, M, N), out_dtype), l)
    else:
        out = jax.ShapeDtypeStruct((M, N), out_dtype)
    o_spec = _op_spec(out, tm, tn, ix_i, ix_j)
    n_in = 3 if add is not None else 2

    def body(*refs):
        a_ref, b_ref = refs[0], refs[1]
        o_ref = refs[n_in + (buf is not None) + (after is not None)]
        if precise:
            part = lax.dot_general(_mat(a_ref), _mat(b_ref), _DIMS[mode], precision=lax.Precision.HIGHEST,
                                   preferred_element_type=F32)
        else:
            part = lax.dot_general(_mat(a_ref).astype(BF16), _mat(b_ref).astype(BF16), _DIMS[mode],
                                   preferred_element_type=F32)

        def finish(r):
            r = r + refs[2][...] if add is not None else r
            o_ref[...] = r.astype(o_ref.dtype).reshape(o_ref.shape)

        if nk == 1:
            finish(part)
            return
        acc, k = refs[-1], pl.program_id(2)

        @pl.when(k == 0)
        def _():
            acc[...] = part

        @pl.when(k > 0)
        def _():
            acc[...] += part

        @pl.when(k == nk - 1)
        def _():
            finish(acc[...])

    ins = [getattr(a, "arr", a), getattr(b, "arr", b)] + ([add] if add is not None else [])
    in_specs = [a_spec, b_spec] + ([o_spec] if add is not None else [])
    aliases = {}
    if buf is not None:
        ins.append(buf)
        in_specs.append(pl.BlockSpec(memory_space=pl.ANY))
        aliases = {n_in: 0}
    if after is not None:
        ins.append(after)
        in_specs.append(pl.BlockSpec(memory_space=pl.ANY))
    return pl.pallas_call(
        body, name=name, grid=(M // tm, N // tn, nk), in_specs=in_specs, out_specs=o_spec,
        out_shape=getattr(out, "arr", out), input_output_aliases=aliases,
        scratch_shapes=[pltpu.VMEM((tm, tn), F32)] if nk > 1 else [],
        compiler_params=_params(("parallel", "parallel", "arbitrary")),
    )(*ins)


def _rows_call(body, name, ins, outs, tr=512, acc_outs=()):
    n = ins[0].shape[0]
    tr = min(tr, n)
    in_specs = [pl.BlockSpec((tr, a.shape[1]), lambda i: (i, 0)) if a.shape[0] == n else
                pl.BlockSpec(a.shape, lambda i: (0, 0)) for a in ins]
    out_specs = [pl.BlockSpec((tr, w), lambda i: (i, 0)) for w, _ in outs] + \
                [pl.BlockSpec((1, w), lambda i: (0, 0)) for w in acc_outs]
    out_shape = [jax.ShapeDtypeStruct((n, w), dt) for w, dt in outs] + \
                [jax.ShapeDtypeStruct((1, w), F32) for w in acc_outs]
    res = pl.pallas_call(body, name=name, grid=(n // tr,), in_specs=in_specs, out_specs=out_specs,
                         out_shape=out_shape, compiler_params=_params(("arbitrary",)))(*ins)
    return res[0] if len(res) == 1 else res


def _acc(ref, val):
    @pl.when(pl.program_id(0) == 0)
    def _():
        ref[...] = jnp.zeros_like(ref)

    ref[...] += val


def rms_fwd(x, g, name, res=None, out_dtype=F32):
    def body(*refs):
        x_ref, g_ref = refs[0], refs[1]
        o_ref = refs[-1]
        xv = x_ref[...]
        y = xv * lax.rsqrt(jnp.mean(xv * xv, axis=-1, keepdims=True) + EPS) * g_ref[...]
        o_ref[...] = (y + refs[2][...] if res is not None else y).astype(o_ref.dtype)

    ins = [x, g] + ([res] if res is not None else [])
    return _rows_call(body, name, ins, [(x.shape[1], out_dtype)])


def rms_bwd(x, g, dy, name, res=None, out_dtype=F32, after=None):
    def body(*refs):
        x_ref, g_ref, dy_ref = refs[:3]
        dx_ref, dg_ref = refs[-2], refs[-1]
        xv, dyv = x_ref[...], dy_ref[...]
        r = lax.rsqrt(jnp.mean(xv * xv, axis=-1, keepdims=True) + EPS)
        xh = xv * r
        dxh = dyv * g_ref[...]
        dx = r * (dxh - xh * jnp.mean(dxh * xh, axis=-1, keepdims=True))
        dx_ref[...] = (dx + refs[3][...] if res is not None else dx).astype(dx_ref.dtype)
        _acc(dg_ref, jnp.sum(dyv * xh, axis=0, keepdims=True))

    ins = [x, g, dy] + ([res] if res is not None else []) + ([after] if after is not None else [])
    return _rows_call(body, name, ins, [(x.shape[1], out_dtype)], acc_outs=(x.shape[1],))


def _sigmoid(x):
    return 0.5 * jnp.tanh(0.5 * x) + 0.5


def swiglu_fwd(gu, name):
    def body(gu_ref, o_ref):
        gate = gu_ref[:, :D_FF]
        o_ref[...] = (gate * _sigmoid(gate) * gu_ref[:, D_FF:]).astype(BF16)

    return _rows_call(body, name, [gu], [(D_FF, BF16)], tr=256)


def swiglu_bwd(gu, dact, name):
    def body(gu_ref, d_ref, o_ref):
        gate, d = gu_ref[:, :D_FF], d_ref[...]
        sg = _sigmoid(gate)
        o_ref[:, :D_FF] = (d * gu_ref[:, D_FF:] * sg * (1.0 + gate * (1.0 - sg))).astype(BF16)
        o_ref[:, D_FF:] = (d * gate * sg).astype(BF16)

    return _rows_call(body, name, [gu, dact], [(2 * D_FF, BF16)], tr=256)


def ple_fwd(h, pp, gt, name):
    def body(h_ref, p_ref, g_ref, o_ref):
        o_ref[...] = h_ref[...] + p_ref[...] * _sigmoid(g_ref[...])

    return _rows_call(body, name, [h, pp, gt], [(D, F32)])


def ple_bwd(dh, pp, gt, name):
    def body(d_ref, p_ref, g_ref, dp_ref, dg_ref):
        d, sg = d_ref[...], _sigmoid(g_ref[...])
        dp_ref[...] = (d * sg).astype(BF16)
        dg_ref[...] = (d * p_ref[...] * sg * (1.0 - sg)).astype(BF16)

    return _rows_call(body, name, [dh, pp, gt], [(D, BF16), (D, BF16)])


def loss_head(y, target, name):
    def body(y_ref, t_ref, d_ref, l_ref):
        e = y_ref[...] - t_ref[...]
        d_ref[...] = e * (1.0 / D)
        col = jnp.sum(e * e, axis=0, keepdims=True) * (0.5 / D)
        _acc(l_ref, sum(col[:, LANES * c:LANES * (c + 1)] for c in range(D // LANES)))

    return _rows_call(body, name, [y, target], [(D, F32)], acc_outs=(LANES,))


def rope_tables(pos_col, name):
    inv = jnp.asarray(_rope_inv_lanes())

    def body(p_ref, inv_ref, c_ref, s_ref):
        ang = p_ref[...].astype(F32) * inv_ref[...]
        lane = lax.broadcasted_iota(jnp.int32, ang.shape, 1)
        first, second = lane < 16, (lane >= 64) & (lane < 80)
        c_ref[...] = jnp.where(first | second, jnp.cos(ang), 1.0)
        sn = jnp.sin(ang)
        s_ref[...] = jnp.where(first, -sn, jnp.where(second, sn, 0.0))

    return _rows_call(body, name, [pos_col, inv], [(LANES, F32), (LANES, F32)])


def _rope(x, c, s):
    return x * c + pltpu.roll(x, 64, axis=1) * s


def _rope_t(d, c, s):
    return d * c + pltpu.roll(d * s, 64, axis=1)


def mla_qk_fwd(qp, kvp, kr, cos, sin, name):
    def body(q_ref, k_ref, kr_ref, c_ref, s_ref, qo_ref, ko_ref):
        c, s = c_ref[...], s_ref[...]
        kr_rot = _rope(kr_ref[...], c, s)
        for h in range(HEADS):
            sl = slice(LANES * h, LANES * (h + 1))
            qo_ref[:, sl] = _rope(q_ref[:, sl], c, s).astype(BF16)
            ko_ref[:, sl] = (k_ref[:, sl] + kr_rot).astype(BF16)

    n = qp.shape[0]
    tr = 256
    w = HEADS * LANES
    return pl.pallas_call(
        body, name=name, grid=(n // tr,),
        in_specs=[pl.BlockSpec((tr, w), lambda i: (i, 0)), pl.BlockSpec((tr, w), lambda i: (i, 0)),
                  pl.BlockSpec((tr, LANES), lambda i: (i, 0)), pl.BlockSpec((tr, LANES), lambda i: (i, 0)),
                  pl.BlockSpec((tr, LANES), lambda i: (i, 0))],
        out_specs=[pl.BlockSpec((tr, w), lambda i: (i, 0))] * 2,
        out_shape=[jax.ShapeDtypeStruct((n, w), BF16)] * 2, compiler_params=_params(("arbitrary",)),
    )(qp, kvp, kr, cos, sin)


def mla_qk_bwd(dq, dk, cos, sin, name):
    def body(dq_ref, dk_ref, c_ref, s_ref, dqp_ref, dkr_ref):
        c, s = c_ref[...], s_ref[...]
        tot = jnp.zeros(c.shape, F32)
        for h in range(HEADS):
            sl = slice(LANES * h, LANES * (h + 1))
            dqp_ref[:, sl] = _rope_t(dq_ref[:, sl], c, s).astype(BF16)
            tot = tot + dk_ref[:, sl]
        dkr_ref[...] = _rope_t(tot, c, s).astype(BF16)

    return _rows_call(body, name, [dq, dk, cos, sin], [(HEADS * LANES, BF16), (LANES, BF16)])


TQ_FWD = 512
TQ_BWD = 256


def _pair_masks(shape):
    lane = lax.broadcasted_iota(jnp.int32, shape, 1)
    return (lane < 64, lane >= 64)


def _scaled_q(q_a, scale):
    return (q_a * jnp.asarray(scale, q_a.dtype), None) if scale == 0.125 else (q_a, scale)


def _causal_probs(q_a, k_a, scale, b0, cq, ck):
    s = lax.dot_general(q_a, k_a, _DIMS["nt"], preferred_element_type=F32)
    if scale is not None:
        s = s * scale
    if cq is not None:
        s = s + (cq - ck)
    tq = s.shape[0]
    row = lax.broadcasted_iota(jnp.int32, (tq, tq), 0)
    col = lax.broadcasted_iota(jnp.int32, (tq, tq), 1)
    diag = jnp.where(col <= row, s[:, b0:], NEG)
    s = diag if b0 == 0 else jnp.concatenate([s[:, :b0], diag], axis=1)
    e = jnp.exp(s - jnp.max(s, axis=-1, keepdims=True))
    return e * (1.0 / jnp.sum(e, axis=-1, keepdims=True))


def attn_fwd(q, k, v, name, *, wide, scale, q_off=0, k_off=0, v_off=0, cum=None, cum_t=None):
    qw = 2 * LANES if wide else LANES
    forget = cum is not None

    def body(*refs):
        q_ref, k_ref, v_ref = refs[:3]
        o_ref = refs[-1]
        m0, m1 = _pair_masks((TQ_FWD, LANES))
        for qi in range(S // TQ_FWD):
            b0, b1 = qi * TQ_FWD, (qi + 1) * TQ_FWD
            outs = []
            for a, msk in enumerate((m0, m1)):
                if wide:
                    q_a, k_a = q_ref[b0:b1, LANES * a:LANES * (a + 1)], k_ref[:b1, LANES * a:LANES * (a + 1)]
                else:
                    q_a, k_a = jnp.where(msk, q_ref[b0:b1, :], jnp.zeros((), BF16)), k_ref[:b1, :]
                cq = refs[3][0, b0:b1, a:a + 1] if forget else None
                ck = refs[4][0, a:a + 1, :b1] if forget else None
                q_a, left = _scaled_q(q_a, scale)
                p = _causal_probs(q_a, k_a, left, b0, cq, ck)
                outs.append(jnp.dot(p.astype(BF16), v_ref[:b1, :], preferred_element_type=F32))
            o_ref[b0:b1, :] = jnp.where(m0, outs[0], outs[1])

    in_specs = [pl.BlockSpec((S, qw), lambda h: (0, q_off * LANES // qw + h)),
                pl.BlockSpec((S, qw), lambda h: (0, k_off * LANES // qw + h)),
                pl.BlockSpec((S, LANES), lambda h: (0, v_off + h))]
    ins = [q, k, v]
    if forget:
        in_specs += [pl.BlockSpec((1, S, 2), lambda h: (h, 0, 0)), pl.BlockSpec((1, 2, S), lambda h: (h, 0, 0))]
        ins += [cum, cum_t]
    return pl.pallas_call(
        body, name=name, grid=(PAIRS,), in_specs=in_specs, out_specs=pl.BlockSpec((S, LANES), lambda h: (0, h)),
        out_shape=jax.ShapeDtypeStruct((S, PAIRS * LANES), F32), compiler_params=_params(("arbitrary",)),
    )(*ins)


def attn_bwd(q, k, v, o, do, name, *, wide, scale, out_dtype=F32, q_off=0, k_off=0, v_off=0, cum=None, cum_t=None):
    qw = 2 * LANES if wide else LANES
    forget = cum is not None

    def body(*refs):
        q_ref, k_ref, v_ref, o_ref, do_ref = refs[:5]
        n_out = 5 if forget else 3
        outs = refs[-(n_out + 2):-2]
        dq_ref, dk_ref, dv_ref = outs[:3]
        dk_acc, dv_acc = refs[-2], refs[-1]
        dk_acc[...] = jnp.zeros_like(dk_acc)
        dv_acc[...] = jnp.zeros_like(dv_acc)
        if forget:
            dcq_ref, dck_ref = outs[3], outs[4]
            dck_ref[...] = jnp.zeros_like(dck_ref)
        m0, m1 = _pair_masks((TQ_BWD, LANES))
        for qi in range(S // TQ_BWD):
            b0, b1 = qi * TQ_BWD, (qi + 1) * TQ_BWD
            do2 = do_ref[b0:b1, :]
            dd = do2 * o_ref[b0:b1, :]
            do_b = do2.astype(BF16)
            mk0, mk1 = _pair_masks((b1, LANES))
            dqs = []
            for a, (msk, mk) in enumerate(((m0, mk0), (m1, mk1))):
                lanes = slice(LANES * a, LANES * (a + 1)) if wide else slice(0, LANES)
                if wide:
                    q_a, k_a = q_ref[b0:b1, lanes], k_ref[:b1, lanes]
                else:
                    q_a, k_a = jnp.where(msk, q_ref[b0:b1, :], jnp.zeros((), BF16)), k_ref[:b1, :]
                cq = refs[5][0, b0:b1, a:a + 1] if forget else None
                ck = refs[6][0, a:a + 1, :b1] if forget else None
                q_a, left = _scaled_q(q_a, scale)
                p = _causal_probs(q_a, k_a, left, b0, cq, ck)
                dp = lax.dot_general(jnp.where(msk, do_b, jnp.zeros((), BF16)), v_ref[:b1, :], _DIMS["nt"],
                                     preferred_element_type=F32)
                delta = jnp.sum(jnp.where(msk, dd, 0.0), axis=-1, keepdims=True)
                ds = p * (dp - delta)
                if forget:
                    dcq_ref[0, b0:b1, a:a + 1] = jnp.sum(ds, axis=-1, keepdims=True)
                    dck_ref[0, a:a + 1, :b1] -= jnp.sum(ds, axis=0, keepdims=True)
                ds_b = ds.astype(BF16)
                dqs.append(jnp.dot(ds_b, k_a, preferred_element_type=F32) * scale)
                dk_a = lax.dot_general(ds_b, q_a, _DIMS["tn"], preferred_element_type=F32)
                dk_acc[:b1, lanes] += dk_a if left is None else dk_a * scale
                dv_acc[:b1, :] += jnp.where(mk, lax.dot_general(p.astype(BF16), do_b, _DIMS["tn"],
                                                                 preferred_element_type=F32), 0.0)
            if wide:
                dq_ref[b0:b1, :LANES] = dqs[0].astype(out_dtype)
                dq_ref[b0:b1, LANES:] = dqs[1].astype(out_dtype)
            else:
                dq_ref[b0:b1, :] = jnp.where(m0, dqs[0], dqs[1]).astype(out_dtype)
        dk_ref[...] = dk_acc[...].astype(out_dtype)
        dv_ref[...] = dv_acc[...].astype(out_dtype)

    pair = pl.BlockSpec((S, LANES), lambda h: (0, h))
    qk_out = pl.BlockSpec((S, qw), lambda h: (0, h))
    in_specs = [pl.BlockSpec((S, qw), lambda h: (0, q_off * LANES // qw + h)),
                pl.BlockSpec((S, qw), lambda h: (0, k_off * LANES // qw + h)),
                pl.BlockSpec((S, LANES), lambda h: (0, v_off + h)), pair, pair]
    ins = [q, k, v, o, do]
    out_specs = [qk_out, qk_out, pair]
    out_shape = [jax.ShapeDtypeStruct((S, PAIRS * qw), out_dtype)] * 2 + \
                [jax.ShapeDtypeStruct((S, PAIRS * LANES), out_dtype)]
    if forget:
        by_q, by_k = pl.BlockSpec((1, S, 2), lambda h: (h, 0, 0)), pl.BlockSpec((1, 2, S), lambda h: (h, 0, 0))
        in_specs += [by_q, by_k]
        ins += [cum, cum_t]
        out_specs += [by_q, by_k]
        out_shape += [jax.ShapeDtypeStruct((PAIRS, S, 2), F32), jax.ShapeDtypeStruct((PAIRS, 2, S), F32)]
    return pl.pallas_call(
        body, name=name, grid=(PAIRS,), in_specs=in_specs, out_specs=out_specs, out_shape=out_shape,
        scratch_shapes=[pltpu.VMEM((S, qw), F32), pltpu.VMEM((S, LANES), F32)],
        compiler_params=_params(("arbitrary",)),
    )(*ins)


def _tri(lower):
    r = lax.broadcasted_iota(jnp.int32, (QBLK, QBLK), 0)
    c = lax.broadcasted_iota(jnp.int32, (QBLK, QBLK), 1)
    return jnp.where((c <= r) if lower else (c >= r), 1.0, 0.0).astype(F32)


def _hi_dot(a, b):
    return jnp.dot(a, b, precision=lax.Precision.HIGHEST, preferred_element_type=F32)


def fox_gate_fwd(fl, bias, name):
    def body(f_ref, b_ref, o_ref):
        tri = _tri(True)
        carry = jnp.zeros((1, LANES), F32)
        for n in range(S // QBLK):
            x = f_ref[n * QBLK:(n + 1) * QBLK, :].astype(F32) + b_ref[...]
            lf = jnp.minimum(x, 0.0) - jnp.log(1.0 + jnp.exp(-jnp.abs(x)))
            c = _hi_dot(tri, lf) + carry
            o_ref[n * QBLK:(n + 1) * QBLK, :] = c
            carry = c[QBLK - 1:QBLK, :]

    return pl.pallas_call(body, name=name, out_shape=jax.ShapeDtypeStruct((S, LANES), F32),
                          compiler_params=_params())(fl, bias)


def fox_gate_bwd(fl, bias, dcq, dck, name):
    def body(f_ref, b_ref, dq_ref, dk_ref, o_ref, db_ref):
        tri = _tri(False)
        carry = jnp.zeros((1, LANES), F32)
        db = jnp.zeros((1, LANES), F32)
        for n in reversed(range(S // QBLK)):
            rows = slice(n * QBLK, (n + 1) * QBLK)
            dlf = _hi_dot(tri, dq_ref[rows, :] + dk_ref[rows, :]) + carry
            carry = dlf[0:1, :]
            x = f_ref[rows, :].astype(F32) + b_ref[...]
            dx = dlf * (1.0 - _sigmoid(x))
            o_ref[rows, :] = dx
            db = db + jnp.sum(dx, axis=0, keepdims=True)
        db_ref[...] = db

    return pl.pallas_call(body, name=name, out_shape=[jax.ShapeDtypeStruct((S, LANES), F32),
                                                      jax.ShapeDtypeStruct((1, LANES), F32)],
                          compiler_params=_params())(fl, bias, dcq, dck)


def _band_valid(first):
    w = QBLK if first else 2 * QBLK
    i = lax.broadcasted_iota(jnp.int32, (QBLK, w), 0)
    j = lax.broadcasted_iota(jnp.int32, (QBLK, w), 1)
    return (j <= i) if first else ((j >= i) & (j - QBLK <= i))


def _band_q(q_ref, rows, msk):
    return jnp.where(msk, q_ref[rows, :], jnp.zeros((), BF16)) * jnp.asarray(0.125, BF16)


def _band_logits(q_a, kk, bias, first):
    s = lax.dot_general(q_a, kk, _DIMS["nt"], preferred_element_type=F32) + bias
    return jnp.where(_band_valid(first), s, NEG)


def residue_major(t, d):
    return t if d == 1 else t.reshape(S // d, d, t.shape[1]).transpose(1, 0, 2).reshape(S, t.shape[1])


def token_major(t, d):
    return t if d == 1 else t.reshape(d, S // d, t.shape[1]).transpose(1, 0, 2).reshape(S, t.shape[1])


def dil_fwd(qkv, bias, g, name):
    d = DIL[g][1]
    ls = S // d

    def body(q_ref, k_ref, v_ref, b_ref, o_ref, l_ref):
        m0, m1 = _pair_masks((QBLK, LANES))
        for n in range(ls // QBLK):
            rows = slice(n * QBLK, (n + 1) * QBLK)
            keys = rows if n == 0 else slice((n - 1) * QBLK, (n + 1) * QBLK)
            os_, ls_ = [], []
            for a, msk in enumerate((m0, m1)):
                q_a = _band_q(q_ref, rows, msk)
                bias_a = b_ref[a, :, QBLK:] if n == 0 else b_ref[a]
                s = _band_logits(q_a, k_ref[keys, :], bias_a, n == 0)
                mx = jnp.max(s, axis=-1, keepdims=True)
                e = jnp.exp(s - mx)
                l = jnp.sum(e, axis=-1, keepdims=True)
                os_.append(jnp.dot((e * (1.0 / l)).astype(BF16), v_ref[keys, :], preferred_element_type=F32))
                ls_.append(mx + jnp.log(l))
            o_ref[rows, :] = jnp.where(m0, os_[0], os_[1])
            l_ref[rows, :] = jnp.where(m0, ls_[0], ls_[1])

    def col(j):
        return lambda h, r: (r, j * PAIRS + h)

    out = pl.BlockSpec((ls, LANES), lambda h, r: (r, h))
    return pl.pallas_call(
        body, name=name, grid=(PAIRS, d),
        in_specs=[pl.BlockSpec((ls, LANES), col(0)), pl.BlockSpec((ls, LANES), col(1)), pl.BlockSpec((ls, LANES), col(2)),
                  pl.BlockSpec((2, QBLK, 2 * QBLK), lambda h, r: (h, 0, 0))],
        out_specs=[out, out], out_shape=[jax.ShapeDtypeStruct((S, D), F32)] * 2,
        compiler_params=_params(("arbitrary", "arbitrary")),
    )(qkv, qkv, qkv, bias)


def dil_merge(os_, lses, name):
    def body(o0, o1, o2, l0, l1, l2, o_ref, l_ref):
        ls_ = [l0[...], l1[...], l2[...]]
        mx = jnp.maximum(jnp.maximum(ls_[0], ls_[1]), ls_[2])
        tot = mx + jnp.log(sum(jnp.exp(l - mx) for l in ls_))
        o_ref[...] = sum(jnp.exp(l - tot) * o[...] for l, o in zip(ls_, (o0, o1, o2)))
        l_ref[...] = tot

    return _rows_call(body, name, list(os_) + list(lses), [(D, F32), (D, F32)])


def dil_bwd(qkv, bias, o, lse, do, g, name):
    d = DIL[g][1]
    ls = S // d

    def body(q_ref, k_ref, v_ref, b_ref, o_ref, l_ref, do_ref, dq_ref, dk_ref, dv_ref, db_ref, dk_acc, dv_acc):
        @pl.when(pl.program_id(1) == 0)
        def _():
            db_ref[...] = jnp.zeros_like(db_ref)

        dk_acc[...] = jnp.zeros_like(dk_acc)
        dv_acc[...] = jnp.zeros_like(dv_acc)
        m0, m1 = _pair_masks((QBLK, LANES))
        for n in range(ls // QBLK):
            rows = slice(n * QBLK, (n + 1) * QBLK)
            keys = rows if n == 0 else slice((n - 1) * QBLK, (n + 1) * QBLK)
            nk = QBLK if n == 0 else 2 * QBLK
            do2, lse2 = do_ref[rows, :], l_ref[rows, :]
            dd = do2 * o_ref[rows, :]
            do_b = do2.astype(BF16)
            mk0, mk1 = _pair_masks((nk, LANES))
            dqs = []
            for a, (msk, mk) in enumerate(((m0, mk0), (m1, mk1))):
                q_a = _band_q(q_ref, rows, msk)
                kk = k_ref[keys, :]
                bias_a = b_ref[a, :, QBLK:] if n == 0 else b_ref[a]
                s = _band_logits(q_a, kk, bias_a, n == 0)
                lse_a = jnp.max(jnp.where(msk, lse2, -jnp.inf), axis=-1, keepdims=True)
                p = jnp.exp(s - lse_a)
                dp = lax.dot_general(jnp.where(msk, do_b, jnp.zeros((), BF16)), v_ref[keys, :], _DIMS["nt"],
                                     preferred_element_type=F32)
                delta = jnp.sum(jnp.where(msk, dd, 0.0), axis=-1, keepdims=True)
                ds = p * (dp - delta)
                if n == 0:
                    db_ref[a, :, QBLK:] += ds
                else:
                    db_ref[a] += ds
                ds_b = ds.astype(BF16)
                dqs.append(jnp.dot(ds_b, kk, preferred_element_type=F32) * 0.125)
                dk_acc[keys, :] += lax.dot_general(ds_b, q_a, _DIMS["tn"], preferred_element_type=F32)
                dv_acc[keys, :] += jnp.where(mk, lax.dot_general(p.astype(BF16), do_b, _DIMS["tn"],
                                                                 preferred_element_type=F32), 0.0)
            dq_ref[rows, :] = jnp.where(m0, dqs[0], dqs[1]).astype(BF16)
        dk_ref[...] = dk_acc[...].astype(BF16)
        dv_ref[...] = dv_acc[...].astype(BF16)

    def col(j):
        return lambda h, r: (r, j * PAIRS + h)

    nat = pl.BlockSpec((ls, LANES), lambda h, r: (r, h))
    b_spec = pl.BlockSpec((2, QBLK, 2 * QBLK), lambda h, r: (h, 0, 0))
    return pl.pallas_call(
        body, name=name, grid=(PAIRS, d),
        in_specs=[pl.BlockSpec((ls, LANES), col(0)), pl.BlockSpec((ls, LANES), col(1)), pl.BlockSpec((ls, LANES), col(2)),
                  b_spec, nat, nat, nat],
        out_specs=[nat, nat, nat, b_spec],
        out_shape=[jax.ShapeDtypeStruct((S, D), BF16)] * 3 + [jax.ShapeDtypeStruct((HEADS, QBLK, 2 * QBLK), F32)],
        scratch_shapes=[pltpu.VMEM((ls, LANES), F32), pltpu.VMEM((ls, LANES), F32)],
        compiler_params=_params(("arbitrary", "arbitrary")),
    )(qkv, qkv, qkv, bias, o, lse, do)


def _place():
    x, y, c = lax.axis_index("x"), lax.axis_index("y"), lax.axis_index("c")
    return x, y, c


def _dev_slot(ref, by_rows, dev):
    return ref.at[:, dev] if by_rows else ref.at[dev]


def all_gather(shards, by_rows, name, in_vmem=False):
    n = len(shards)

    def body(*refs):
        x_refs, out_refs = refs[:n], refs[n:2 * n]
        send_sems, recv_sems, local_sems = refs[2 * n:]
        x, y, c = _place()
        me, sibling = (x, y, c), (x, y, 1 - c)
        chips = [(1 - x, y), (x, 1 - y), (1 - x, 1 - y)]

        def slot(t, px, py, pc):
            return _dev_slot(out_refs[t], by_rows[t], 4 * px + 2 * py + pc)

        def copy(t, k, blk, to, src=None):
            return pltpu.make_async_remote_copy(
                src_ref=slot(t, *blk) if src is None else src, dst_ref=slot(t, *blk), send_sem=send_sems.at[7 * t + k],
                recv_sem=recv_sems.at[7 * t + k], device_id=to, device_id_type=MESH_ID)

        mine = [pltpu.make_async_copy(x_refs[t], slot(t, *me), local_sems.at[t]) for t in range(n)]
        for cp in mine:
            cp.start()
        first = []
        for t in range(n):
            first.append(copy(t, 0, me, sibling, src=x_refs[t]))
            first += [copy(t, 1 + j, me, (*chip, c), src=x_refs[t]) for j, chip in enumerate(chips)]
        for cp in first:
            cp.start()
        passed = []
        for j, chip in enumerate(chips):
            for t in range(n):
                copy(t, 1 + j, (*chip, c), me).wait_recv()
                passed.append(copy(t, 4 + j, (*chip, c), sibling))
                passed[-1].start()
        for t in range(n):
            copy(t, 0, sibling, me).wait_recv()
            for j, chip in enumerate(chips):
                copy(t, 4 + j, (*chip, 1 - c), me).wait_recv()
        for cp in first + passed:
            cp.wait_send()
        for cp in mine:
            cp.wait()

    def gathered(s, rows):
        shp = (s.shape[0], N_DEV) + s.shape[1:] if rows else (N_DEV,) + s.shape
        return jax.ShapeDtypeStruct(shp, s.dtype)

    space = pl.BlockSpec(memory_space=pltpu.VMEM if in_vmem else pl.ANY)
    return pl.pallas_call(
        body, name=name, out_shape=[gathered(s, r) for s, r in zip(shards, by_rows)],
        in_specs=[space] * n, out_specs=[space] * n,
        scratch_shapes=[pltpu.SemaphoreType.DMA((7 * n,)), pltpu.SemaphoreType.DMA((7 * n,)),
                        pltpu.SemaphoreType.DMA((n,))],
        compiler_params=pltpu.CompilerParams(vmem_limit_bytes=VMEM_LIMIT),
    )(*shards)


_HBM = pl.BlockSpec(memory_space=pltpu.HBM)
_SEM = pl.BlockSpec(memory_space=pltpu.SEMAPHORE)
_SPLIT = dict(has_side_effects=pltpu.SideEffectType.DATAFLOW_SIDE_EFFECTING)


def _hbm(a):
    return pltpu.with_memory_space_constraint(a, pltpu.HBM)


def _gathered_shape(s, rows):
    return (s.shape[0], N_DEV) + s.shape[1:] if rows else (N_DEV,) + s.shape


def _peers(x, y, c):
    return [(1 - x if k & 4 else x, 1 - y if k & 2 else y, 1 - c if k & 1 else c) for k in range(1, N_DEV)]


def gather_start(shards, lands, by_rows, name):
    n = len(shards)

    def body(*refs):
        x_refs, land_refs = refs[:n], refs[n:2 * n]
        send_sems, recv_sems = refs[2 * n], refs[2 * n + 1]
        x, y, c = _place()
        me = 4 * x + 2 * y + c
        for t in range(n):
            for k, peer in enumerate(_peers(x, y, c)):
                pltpu.make_async_remote_copy(
                    src_ref=x_refs[t], dst_ref=_dev_slot(land_refs[t], by_rows[t], me), send_sem=send_sems.at[7 * t + k],
                    recv_sem=recv_sems.at[7 * t + k], device_id=peer, device_id_type=MESH_ID).start()

    sems = pltpu.SemaphoreType.DMA((7 * n,))
    res = pl.pallas_call(
        body, name=name,
        out_shape=(sems, sems) + tuple(pltpu.HBM(a.shape, a.dtype) for a in list(shards) + list(lands)),
        in_specs=[_HBM] * (2 * n), out_specs=(_SEM, _SEM) + (_HBM,) * (2 * n),
        input_output_aliases={i: 2 + i for i in range(2 * n)},
        compiler_params=pltpu.CompilerParams(**_SPLIT),
    )(*[_hbm(a) for a in list(shards) + list(lands)])
    return res[0], res[1], list(res[2:2 + n]), list(res[2 + n:])


def gather_wait(send_sems, recv_sems, first, shards, lands, by_rows, after, name):
    n = len(shards)

    def body(*refs):
        x_refs, land_refs = refs[:n], refs[n:2 * n]
        send_sems, recv_sems = refs[2 * n], refs[2 * n + 1]
        x, y, c = _place()
        for t in range(n):
            for k, (px, py, pc) in enumerate(_peers(x, y, c)):
                cp = pltpu.make_async_remote_copy(
                    src_ref=x_refs[t], dst_ref=_dev_slot(land_refs[t], by_rows[t], 4 * px + 2 * py + pc),
                    send_sem=send_sems.at[7 * (first + t) + k], recv_sem=recv_sems.at[7 * (first + t) + k],
                    device_id=(px, py, pc), device_id_type=MESH_ID)
                cp.wait_send()
                cp.wait_recv()

    res = pl.pallas_call(
        body, name=name, out_shape=tuple(pltpu.HBM(a.shape, a.dtype) for a in list(shards) + list(lands)),
        in_specs=[_HBM] * (2 * n) + [_SEM, _SEM, pl.BlockSpec(memory_space=pl.ANY)], out_specs=(_HBM,) * (2 * n),
        input_output_aliases={i: i for i in range(2 * n)},
        compiler_params=pltpu.CompilerParams(**_SPLIT),
    )(*shards, *lands, send_sems, recv_sems, after)
    return list(res[n:])


def scatter_start(srcs, src_l, lands, land_l, by_rows, name):
    n = len(srcs)

    def body(*refs):
        x_refs, land_refs = refs[:n], refs[n:2 * n]
        send_sems, recv_sems, token = refs[2 * n], refs[2 * n + 1], refs[-1]
        x, y, c = _place()
        me = 4 * x + 2 * y + c
        for k, (px, py, pc) in enumerate(_peers(x, y, c)):
            for t in range(n):
                blk = _dev_slot(x_refs[t], by_rows[t], 4 * px + 2 * py + pc)
                pltpu.make_async_remote_copy(
                    src_ref=blk.at[src_l[t]], dst_ref=land_refs[t].at[me, land_l[t]], send_sem=send_sems.at[7 * t + k],
                    recv_sem=recv_sems.at[7 * t + k], device_id=(px, py, pc), device_id_type=MESH_ID).start()
        token[...] = jnp.zeros_like(token)

    lands = [lax.empty((N_DEV, 1) + s.shape[2:], s.dtype) if ld is None else ld for s, ld in zip(srcs, lands)]
    sems = pltpu.SemaphoreType.DMA((7 * n,))
    res = pl.pallas_call(
        body, name=name,
        out_shape=(sems, sems) + tuple(pltpu.HBM(a.shape, a.dtype) for a in list(srcs) + lands)
        + (jax.ShapeDtypeStruct((8, LANES), F32),),
        in_specs=[_HBM] * (2 * n),
        out_specs=(_SEM, _SEM) + (_HBM,) * (2 * n) + (pl.BlockSpec(memory_space=pltpu.VMEM),),
        input_output_aliases={i: 2 + i for i in range(2 * n)},
        compiler_params=pltpu.CompilerParams(**_SPLIT),
    )(*[_hbm(a) for a in list(srcs) + lands])
    return res[0], res[1], list(res[2:2 + n]), list(res[2 + n:2 + 2 * n]), res[-1]


def scatter_wait(send_sems, recv_sems, srcs, src_l, lands, land_l, by_rows, after, name):
    n = len(srcs)

    def body(*refs):
        x_refs, land_refs = refs[:n], refs[n:2 * n]
        send_sems, recv_sems = refs[2 * n], refs[2 * n + 1]
        x, y, c = _place()
        for k, (px, py, pc) in enumerate(_peers(x, y, c)):
            peer = 4 * px + 2 * py + pc
            for t in range(n):
                cp = pltpu.make_async_remote_copy(
                    src_ref=_dev_slot(x_refs[t], by_rows[t], peer).at[src_l[t]], dst_ref=land_refs[t].at[peer, land_l[t]],
                    send_sem=send_sems.at[7 * t + k], recv_sem=recv_sems.at[7 * t + k], device_id=(px, py, pc),
                    device_id_type=MESH_ID)
                cp.wait_send()
                cp.wait_recv()

    res = pl.pallas_call(
        body, name=name, out_shape=tuple(pltpu.HBM(a.shape, a.dtype) for a in list(srcs) + list(lands)),
        in_specs=[_HBM] * (2 * n) + [_SEM, _SEM, pl.BlockSpec(memory_space=pl.ANY)], out_specs=(_HBM,) * (2 * n),
        input_output_aliases={i: i for i in range(2 * n)},
        compiler_params=pltpu.CompilerParams(**_SPLIT),
    )(*srcs, *lands, send_sems, recv_sems, after)
    return list(res[:n]), list(res[n:])


ADAM_BLOCK_BYTES = 3 << 19


def adamw(w, m, v, parts, name):
    n_parts = parts.shape[0]
    n_l, r, c = w.shape
    lane_c = -(-c // LANES) * LANES
    fits = [t for t in range(16, r, 16) if r % t == 0 and t * lane_c * 4 <= ADAM_BLOCK_BYTES]
    tr = max(fits) if fits and r * lane_c * 4 > ADAM_BLOCK_BYTES else r
    c1 = 1.0 / (1.0 - ADAM_B1 ** ADAM_STEP)
    c2 = 1.0 / (1.0 - ADAM_B2 ** ADAM_STEP)

    def body(w_ref, m_ref, v_ref, p_ref, g_ref, d_ref, nm_ref, nv_ref):
        g = p_ref[0].astype(F32)
        for j in range(1, n_parts):
            g = g + p_ref[j].astype(F32)
        nm = ADAM_B1 * m_ref[...] + (1.0 - ADAM_B1) * g
        nv = ADAM_B2 * v_ref[...] + (1.0 - ADAM_B2) * (g * g)
        g_ref[...] = g
        nm_ref[...] = nm
        nv_ref[...] = nv
        d_ref[...] = -ADAM_LR * ((nm * c1) / (jnp.sqrt(nv * c2) + ADAM_EPS) + ADAM_WD * w_ref[...])

    blk = pl.BlockSpec((1, tr, c), lambda l, i: (l, i, 0))
    return pl.pallas_call(
        body, name=name, grid=(n_l, r // tr),
        in_specs=[blk, blk, blk, pl.BlockSpec((n_parts, 1, tr, c), lambda l, i: (0, l, i, 0))],
        out_specs=[blk] * 4, out_shape=[jax.ShapeDtypeStruct((n_l, r, c), F32)] * 4,
        compiler_params=_params(("parallel", "parallel")),
    )(w, m, v, parts)


def sum_parts(parts, name):
    def body(p_ref, o_ref):
        g = p_ref[0]
        for j in range(1, parts.shape[0]):
            g = g + p_ref[j]
        o_ref[...] = g

    return pl.pallas_call(body, name=name, out_shape=jax.ShapeDtypeStruct(parts.shape[1:], F32),
                          compiler_params=_params())(parts)


def _pack(arrs, rows, dtype):
    flat = jnp.concatenate([a.reshape(-1).astype(dtype) for a in arrs])
    return jnp.pad(flat, (0, rows * LANES - flat.shape[0])).reshape(rows, LANES)


def _unpack(packed, shapes):
    flat, out, off = packed.reshape(-1), [], 0
    for shp in shapes:
        n = int(np.prod(shp))
        out.append(flat[off:off + n].reshape(shp))
        off += n
    return out


def _cat(parts):
    return jnp.concatenate(parts, axis=1)


def _layer_list(i):
    kind, j = i % 3, i // 3
    mix = ([("mla_w_a", j), ("mla_w_uq", j), ("mla_w_ukv", j), ("mla_w_o", j)] if kind == 0 else
           [("dil_w_qkv", 0), ("dil_w_o", 0)] if kind == 1 else [("fox_w_qkvf", 0), ("fox_w_o", 0)])
    return mix + [("ffn_w_in", i), ("ffn_w_out", i), ("ple_w_proj", i), ("ple_w_gate", i)]


def _mla_layout(w_a, g_uq, g_ukv, j):
    def z(r, n):
        return jnp.zeros((r, n), BF16)

    wa = w_a[j]
    a = _cat([wa[:, :640], wa[:, 640:656], z(D, 48), wa[:, 656:672], z(D, 48)])
    q, k, v = [], [], []
    for h in range(HEADS):
        b = g_uq[h // 2, j][:, 96 * (h % 2):96 * (h % 2 + 1)]
        q += [b[:, 64:80], b[:, 0:32], z(Q_RANK, 16), b[:, 80:96], b[:, 32:64], z(Q_RANK, 16)]
        b = g_ukv[h // 2, j][:, LANES * (h % 2):LANES * (h % 2 + 1)]
        k += [z(KV_RANK, 16), b[:, 0:32], z(KV_RANK, 32), b[:, 32:64], z(KV_RANK, 16)]
        v.append(b[:, 64:128])
    return a, _cat(q), _cat(k + v)


def _mla_unlayout(d_a, d_uq, d_ukv):
    a = _cat([d_a[:, :640], d_a[:, 640:656], d_a[:, 704:720]])
    uq, ukv = [], []
    for dev in range(N_DEV):
        q, kv = [], []
        for h in (2 * dev, 2 * dev + 1):
            b = d_uq[:, LANES * h:LANES * (h + 1)]
            q += [b[:, 16:48], b[:, 80:112], b[:, 0:16], b[:, 64:80]]
            b = d_ukv[:, LANES * h:LANES * (h + 1)]
            kv += [b[:, 16:48], b[:, 80:112], d_ukv[:, HEADS * LANES + 64 * h:HEADS * LANES + 64 * (h + 1)]]
        uq.append(_cat(q))
        ukv.append(_cat(kv))
    return a, jnp.stack(uq), jnp.stack(ukv)


def _mixer_fwd(kind, tag, hn, W, aux):
    if kind == 0:
        a = mm(hn, W["w_a"], "nn", f"{tag}_a", tn=768)
        cq = rms_fwd(a[:, :Q_RANK], W["q_norm"], f"{tag}_cq", out_dtype=BF16)
        ckv = rms_fwd(a[:, Q_RANK:Q_RANK + KV_RANK], W["kv_norm"], f"{tag}_ckv", out_dtype=BF16)
        qp = mm(cq, W["w_uq"], "nn", f"{tag}_uq", tk=384)
        kvp = mm(ckv, W["w_ukv"], "nn", f"{tag}_ukv", tk=256)
        q, k = mla_qk_fwd(qp, kvp, a[:, 640:], aux["cos"], aux["sin"], f"{tag}_qk")
        v = kvp.astype(BF16)
        o = attn_fwd(q, k, v, f"{tag}_attn", wide=True, scale=96 ** -0.5, v_off=HEADS)
        y = mm(o, W["w_o"], "nn", f"{tag}_o")
        return y, (a, cq, ckv, q, k, v, o)
    if kind == 1:
        qkv = [mm(residue_major(hn, DIL[g][1]), W["w_qkv"], "nn", f"{tag}_qkv{g}", out_dtype=BF16, tn=384,
                  b_cols=(8 * g, 8)) for g in range(3)]
        parts = [dil_fwd(qkv[g], aux["dil_bias"][g], g, f"{tag}_g{g}") for g in range(3)]
        o, lse = dil_merge([token_major(p_[0], DIL[g][1]) for g, p_ in enumerate(parts)],
                           [token_major(p_[1], DIL[g][1]) for g, p_ in enumerate(parts)], f"{tag}_merge")
        y = mm(o, W["w_o"], "nn", f"{tag}_o")
        return y, (qkv, o, lse)
    a = mm(hn, W["w_qkvf"], "nn", f"{tag}_qkvf", tn=640)
    fl = a[:, 3072:]
    cum = fox_gate_fwd(fl, aux["fox_b"], f"{tag}_gate")[:, :HEADS]
    cum_q = cum.reshape(S, PAIRS, 2).transpose(1, 0, 2)
    cum_k = cum.T.reshape(PAIRS, 2, S)
    ab = a.astype(BF16)
    o = attn_fwd(ab, ab, ab, f"{tag}_attn", wide=False, scale=0.125, k_off=PAIRS, v_off=2 * PAIRS, cum=cum_q, cum_t=cum_k)
    y = mm(o, W["w_o"], "nn", f"{tag}_o")
    return y, (fl, ab, cum_q, cum_k, o)


def _mixer_bwd(kind, tag, hn, dy, W, aux, saved):
    gr = {}
    if kind == 0:
        a, cq, ckv, q, k, v, o = saved
        gr["w_o"] = mm(o, dy, "tn", f"{tag}_dwo", out_dtype=BF16)
        do = mm(dy, W["w_o"], "nt", f"{tag}_do")
        dq, dk, dv = attn_bwd(q, k, v, o, do, f"{tag}_attn_b", wide=True, scale=96 ** -0.5, v_off=HEADS)
        dqp, dkr = mla_qk_bwd(dq, dk, aux["cos"], aux["sin"], f"{tag}_qk_b")
        dkvp = jnp.concatenate([dk, dv], axis=1)
        gr["w_ukv"] = mm(ckv, dkvp, "tn", f"{tag}_dwukv", out_dtype=BF16, tm=256)
        dckv = mm(dkvp, W["w_ukv"], "nt", f"{tag}_dckv", tn=256)
        gr["w_uq"] = mm(cq, dqp, "tn", f"{tag}_dwuq", out_dtype=BF16, tm=384)
        dcq = mm(dqp, W["w_uq"], "nt", f"{tag}_dcq", tn=384)
        da_q, gr["q_norm"] = rms_bwd(a[:, :Q_RANK], W["q_norm"], dcq, f"{tag}_cq_b", out_dtype=BF16)
        da_kv, gr["kv_norm"] = rms_bwd(a[:, Q_RANK:Q_RANK + KV_RANK], W["kv_norm"], dckv, f"{tag}_ckv_b",
                                       out_dtype=BF16)
        da = jnp.concatenate([da_q, da_kv, dkr], axis=1)
        gr["w_a"] = mm(hn, da, "tn", f"{tag}_dwa", out_dtype=BF16, tn=768)
        return mm(da, W["w_a"], "nt", f"{tag}_dhn", tk=768), gr
    if kind == 1:
        qkv, o, lse = saved
        gr["w_o"] = mm(o, dy, "tn", f"{tag}_dwo", out_dtype=BF16)
        do = mm(dy, W["w_o"], "nt", f"{tag}_do")
        cols, dbs = [], []
        for g, (_, d) in enumerate(DIL):
            dq, dk, dv, db = dil_bwd(qkv[g], aux["dil_bias"][g], residue_major(o, d), residue_major(lse, d),
                                     residue_major(do, d), g, f"{tag}_g{g}_b")
            cols += [token_major(t, d) for t in (dq, dk, dv)]
            dbs.append(db)
        dqkv = jnp.concatenate(cols, axis=1)
        gr["dil_dbias"] = dbs
        gr["w_qkv"] = mm(hn, dqkv, "tn", f"{tag}_dwqkv", out_dtype=BF16, out_dev=1152, tn=1152)
        return mm(dqkv, W["w_qkv"], "nt", f"{tag}_dhn", tk=1152), gr
    fl, ab, cum_q, cum_k, o = saved
    gr["w_o"] = mm(o, dy, "tn", f"{tag}_dwo", out_dtype=BF16)
    do = mm(dy, W["w_o"], "nt", f"{tag}_do")
    dq, dk, dv, dcq, dck = attn_bwd(ab, ab, ab, o, do, f"{tag}_attn_b", wide=False, scale=0.125, out_dtype=BF16,
                                    k_off=PAIRS, v_off=2 * PAIRS, cum=cum_q, cum_t=cum_k)
    pad = ((0, 0), (0, LANES - HEADS))
    dcq = jnp.pad(dcq.transpose(1, 0, 2).reshape(S, HEADS), pad)
    dck = jnp.pad(dck.reshape(HEADS, S).T, pad)
    dfl, gr["b_f"] = fox_gate_bwd(fl, aux["fox_b"], dcq, dck, f"{tag}_gate_b")
    da = jnp.concatenate([dq, dk, dv, dfl.astype(BF16)], axis=1)
    gr["w_qkvf"] = mm(hn, da, "tn", f"{tag}_dwqkvf", out_dtype=BF16, tn=640)
    return mm(da, W["w_qkvf"], "nt", f"{tag}_dhn", tk=640), gr


def kernel(x, p, positions, norm_g, ffn_w_in, ffn_w_out, ple_w_proj, ple_w_gate, rel_bias, mla_w_a, mla_q_norm, mla_kv_norm, mla_w_uq, mla_w_ukv, mla_w_o, dil_w_qkv, dil_w_o, fox_w_qkvf, fox_b_f, fox_w_o, loss_target, m_norm_g, m_ffn_w_in, m_ffn_w_out, m_ple_w_proj, m_ple_w_gate, m_rel_bias, m_mla_w_a, m_mla_q_norm, m_mla_kv_norm, m_mla_w_uq, m_mla_w_ukv, m_mla_w_o, m_dil_w_qkv, m_dil_w_o, m_fox_w_qkvf, m_fox_b_f, m_fox_w_o, v_norm_g, v_ffn_w_in, v_ffn_w_out, v_ple_w_proj, v_ple_w_gate, v_rel_bias, v_mla_w_a, v_mla_q_norm, v_mla_kv_norm, v_mla_w_uq, v_mla_w_ukv, v_mla_w_o, v_dil_w_qkv, v_dil_w_o, v_fox_w_qkvf, v_fox_b_f, v_fox_w_o):
    given = dict(locals())
    me = 4 * lax.axis_index("x") + 2 * lax.axis_index("y") + lax.axis_index("c")
    for n in TRANSPOSED:
        for pre in ("", "m_", "v_"):
            given[pre + n] = jnp.swapaxes(given[pre + n], 1, 2)

    rows_of = {n: axis == 1 for n, _, axis in BIG}
    rows_of["gains"] = False
    shape_of = {n: shp for n, shp, _ in BIG}
    lists = [_layer_list(i) for i in range(DEPTH)]
    lists[0] = [("gains", 0)] + lists[0]
    flat = [nl for ls in lists for nl in ls]
    flat_rows = [rows_of[n] for n, _ in flat]
    gain_rows = _rows(sum(int(np.prod(s)) for _, s, _ in SMALL_SHARDED))
    shards = [_pack([given[k] for k, _, _ in SMALL_SHARDED], gain_rows, F32)[None] if n == "gains" else
              given[n][l:l + 1].astype(BF16) for n, l in flat]
    lands = [lax.dynamic_update_slice(lax.empty(_gathered_shape(s, r), s.dtype), s[:, None] if r else s[None],
                                      (0, me, 0, 0) if r else (me, 0, 0, 0)) for s, r in zip(shards, flat_rows)]
    send_s, recv_s, shards, lands = gather_start(shards, lands, flat_rows, "gather_start")

    cos, sin = rope_tables(positions.reshape(S, 1), "rope_tables")
    dil_bias = [mm(rel_bias[:, HEADS * g:HEADS * (g + 1)], jnp.asarray(_bucket_onehot(DIL[g][1])), "tn",
                   f"dil_bias{g}", precise=True, tn=4096).reshape(HEADS, QBLK, 2 * QBLK) for g in range(3)]
    aux = {"cos": cos, "sin": sin, "dil_bias": dil_bias,
           "fox_b": jnp.pad(fox_b_f, ((0, 0), (0, LANES - HEADS)))}

    def arrived(i, part, behind):
        n_mix = len(lists[i]) - 4
        first = sum(len(ls) for ls in lists[:i]) + (n_mix if part else 0)
        sl = slice(first, first + (4 if part else n_mix))
        got = gather_wait(send_s, recv_s, first, shards[sl], lands[sl], flat_rows[sl], behind, f"gather_wait{i}_{part}")
        return {n: g.reshape(1, N_DEV * shape_of[n][1], shape_of[n][2]) if rows_of[n] else g
                for (n, _), g in zip(flat[sl], got)}

    full = {}

    def mixer_weights(i, behind):
        kind, j = i % 3, i // 3
        w = arrived(i, 0, behind)
        if i == 0:
            gains, off = w["gains"].reshape(N_DEV, gain_rows * LANES), 0
            for n, shp, axis in SMALL_SHARDED:
                cnt = int(np.prod(shp))
                g = jnp.moveaxis(gains[:, off:off + cnt].reshape((N_DEV,) + shp), 0, axis)
                full[n] = g.reshape(shp[:axis] + (N_DEV * shp[axis],))
                off += cnt
        W = {"g": [full["norm_g"][i, r][None, :] for r in range(4)]}
        if kind == 0:
            w_a, w_uq, w_ukv = _mla_layout(w["mla_w_a"], w["mla_w_uq"], w["mla_w_ukv"], 0)
            W.update(w_a=w_a, w_uq=w_uq, w_ukv=w_ukv, w_o=Lay(w["mla_w_o"], 0),
                     q_norm=full["mla_q_norm"][j][None, :], kv_norm=full["mla_kv_norm"][j][None, :])
        elif kind == 1:
            W.update(w_qkv=Dev(w["dil_w_qkv"], 0), w_o=Lay(w["dil_w_o"], 0))
        else:
            fox_w = jnp.pad(_cat([w["fox_w_qkvf"][dev, 0] for dev in range(N_DEV)]), ((0, 0), (0, FOX_W - 3088)))
            W.update(w_qkvf=fox_w, w_o=Lay(w["fox_w_o"], 0))
        return W

    def ffn_weights(i, behind):
        w = arrived(i, 1, behind)
        return {"w_in_t": Lay(w["ffn_w_in"], 0), "w_out": Lay(w["ffn_w_out"], 0), "w_proj": Dev(w["ple_w_proj"], 0),
                "w_gate": Lay(w["ple_w_gate"], 0)}

    h = x[0]
    saved, weights = [], []
    for i in range(DEPTH):
        kind, j, W = i % 3, i // 3, mixer_weights(i, h)
        weights.append(W)
        t = f"l{i}"
        hn = rms_fwd(h, W["g"][0], f"{t}_n0", out_dtype=BF16)
        y, mix = _mixer_fwd(kind, f"{t}_mix", hn, W, aux)
        W.update(ffn_weights(i, y))
        h1 = rms_fwd(y, W["g"][1], f"{t}_n1", res=h)
        fin = rms_fwd(h1, W["g"][2], f"{t}_n2", out_dtype=BF16)
        gu = mm(fin, W["w_in_t"], "nt", f"{t}_ffn_in")
        act = swiglu_fwd(gu, f"{t}_swiglu")
        f = mm(act, W["w_out"], "nn", f"{t}_ffn_out")
        h2 = rms_fwd(f, W["g"][3], f"{t}_n3", res=h1)
        pp = mm(p[i, 0], W["w_proj"], "nn", f"{t}_ple_p", tn=LANES, tk=256)
        gt = mm(h2, W["w_gate"], "nn", f"{t}_ple_g")
        h3 = ple_fwd(h2, pp, gt, f"{t}_ple")
        saved.append((h, hn, y, h1, fin, gu, act, f, h2, pp, gt, mix))
        h = h3

    dh, loss_lanes = loss_head(h, loss_target[0], "loss_head")

    grads = {n: None for n, _, _ in BIG}
    landed = {n: (lax.empty((N_DEV,) + shp, BF16) if shp[0] > 1 else None) for n, shp, _ in BIG}
    in_flight = []
    own = {n: [] for n, _, _ in BIG}
    g_norm = [[None] * 4 for _ in range(DEPTH)]
    g_qn, g_kvn = [None, None], [None, None]
    g_rel, g_bf = None, None

    def stacked(n):
        g = grads[n]
        return None if g is None else g.reshape(g.shape[0], N_DEV * g.shape[2], g.shape[3])

    def by_device(g):
        return g.reshape(g.shape[0], N_DEV, g.shape[1] // N_DEV, g.shape[2])

    def start(i, entries, mine, tag):
        names = [n for n, _ in entries]
        srcs = [mine[n] if n in mine else grads[n] for n in names]
        src_l = [0 if n in mine else i for n in names]
        land_l = [l if shape_of[n][0] > 1 else 0 for n, l in entries]
        rows = [rows_of[n] for n in names]
        s_sem, r_sem, srcs, got, token = scatter_start(srcs, src_l, [landed[n] for n in names], land_l, rows, tag)
        for n, src, ld in zip(names, srcs, got):
            landed[n] = ld
            if n in mine:
                mine[n] = src
            else:
                grads[n] = src
        in_flight.append((s_sem, r_sem, names, mine, src_l, land_l, rows))
        return token

    token = None
    for i in reversed(range(DEPTH)):
        kind, j, W = i % 3, i // 3, weights[i]
        t = f"l{i}b"
        h0, hn, y, h1, fin, gu, act, f, h2, pp, gt, mix = saved[i]
        dpp, dgt = ple_bwd(dh, pp, gt, f"{t}_ple")
        grads["ple_w_proj"] = mm(p[i, 0], dpp, "tn", f"{t}_dwp", out_dtype=BF16, out_dev=LANES, tm=256, tn=LANES,
                                 stack=(grads["ple_w_proj"], DEPTH, i), after=token)
        grads["ple_w_gate"] = by_device(mm(h2, dgt, "tn", f"{t}_dwg", out_dtype=BF16,
                                           stack=(stacked("ple_w_gate"), DEPTH, i)))
        dh2 = mm(dgt, W["w_gate"], "nt", f"{t}_dh2", add=dh)
        df, g_norm[i][3] = rms_bwd(f, W["g"][3], dh2, f"{t}_n3", out_dtype=BF16)
        grads["ffn_w_out"] = by_device(mm(act, df, "tn", f"{t}_dwout", out_dtype=BF16, tm=1408,
                                          stack=(stacked("ffn_w_out"), DEPTH, i)))
        dact = mm(df, W["w_out"], "nt", f"{t}_dact", tn=1408)
        dgu = swiglu_bwd(gu, dact, f"{t}_swiglu")
        grads["ffn_w_in"] = by_device(mm(dgu, fin, "tn", f"{t}_dwin", out_dtype=BF16, tm=512, tn=1024,
                                         stack=(stacked("ffn_w_in"), DEPTH, i)))
        token = start(i, lists[i][-4:], {}, f"scatter_ffn{i}")
        dfin = mm(dgu, W["w_in_t"], "nn", f"{t}_dfin", after=token, tk=1408)
        dh1, g_norm[i][2] = rms_bwd(h1, W["g"][2], dfin, f"{t}_n2", res=dh2)
        dy, g_norm[i][1] = rms_bwd(y, W["g"][1], dh1, f"{t}_n1", out_dtype=BF16)
        dhn, gr = _mixer_bwd(kind, f"{t}_mix", hn, dy, W, aux, mix)
        if kind == 0:
            d_a, d_uq, d_ukv = _mla_unlayout(gr["w_a"], gr["w_uq"], gr["w_ukv"])
            mine = {"mla_w_a": by_device(d_a[None]), "mla_w_uq": d_uq[:, None], "mla_w_ukv": d_ukv[:, None],
                    "mla_w_o": by_device(gr["w_o"][None])}
            g_qn[j], g_kvn[j] = gr["q_norm"], gr["kv_norm"]
        elif kind == 1:
            mine = {"dil_w_qkv": gr["w_qkv"], "dil_w_o": by_device(gr["w_o"][None])}
            g_rel = jnp.concatenate(
                [mm(jnp.asarray(_bucket_onehot(DIL[g][1])), gr["dil_dbias"][g].reshape(HEADS, -1), "nt",
                    f"{t}_drel{g}", precise=True, tk=4096) for g in range(3)], axis=1)
        else:
            wide = gr["w_qkvf"]
            mine = {"fox_w_qkvf": jnp.stack([wide[:, 386 * dev:386 * (dev + 1)] for dev in range(N_DEV)])[:, None],
                    "fox_w_o": by_device(gr["w_o"][None])}
            g_bf = gr["b_f"][:, :HEADS]
        token = start(i, [e for e in lists[i][:-4] if e[0] in mine], mine, f"scatter_mix{i}")
        dh, g_norm[i][0] = rms_bwd(h0, W["g"][0], dhn, f"{t}_n0", res=dh1, after=token)
    grad_x = dh[None]

    for idx, (s_sem, r_sem, names, mine, src_l, land_l, rows) in enumerate(in_flight):
        srcs = [mine[n] if n in mine else grads[n] for n in names]
        srcs, got = scatter_wait(s_sem, r_sem, srcs, src_l, [landed[n] for n in names], land_l, rows, dh,
                                 f"scatter_wait{idx}")
        for n, src, ld in zip(names, srcs, got):
            landed[n] = ld
            if n in mine:
                mine[n] = src
            else:
                grads[n] = src
    for _, _, names, mine, src_l, land_l, rows in in_flight:
        for n, sl, ll, rw in zip(names, src_l, land_l, rows):
            src = mine[n] if n in mine else grads[n]
            own[n].append((ll, lax.dynamic_index_in_dim(src, me, axis=1 if rw else 0, keepdims=False)[sl]))
    big_out = []
    for n, _, _ in BIG:
        part = landed[n]
        for ll, blk in own[n]:
            part = lax.dynamic_update_slice(part, blk[None, None], (me, ll, 0, 0))
        big_out.append(adamw(given[n], given["m_" + n], given["v_" + n], part, f"adamw_{n}"))

    small_full = [jnp.stack([jnp.concatenate(r, axis=0) for r in g_norm]).reshape(-1),
                  jnp.concatenate(g_qn, axis=0).reshape(-1), jnp.concatenate(g_kvn, axis=0).reshape(-1),
                  g_rel.reshape(-1), g_bf.reshape(-1), loss_lanes.reshape(-1)]
    small_n = sum(a.shape[0] for a in small_full)
    small_rows = _rows(small_n)
    parts, = all_gather([_pack(small_full, small_rows, F32)], [False], "gather_small_grads", in_vmem=True)
    tot = _unpack(sum_parts(parts, "sum_small_grads"), [(4, 4, D), (2, Q_RANK), (2, KV_RANK), (32, 48), (1, 16), (LANES,)])
    loss = jnp.sum(tot[5])
    small_g = [lax.dynamic_slice_in_dim(tot[0], me * 128, 128, axis=2), lax.dynamic_slice_in_dim(tot[1], me * 48, 48, axis=1),
               lax.dynamic_slice_in_dim(tot[2], me * 32, 32, axis=1), tot[3], tot[4]]
    small_names = [n for n, _, _ in SMALL_SHARDED] + [n for n, _ in SMALL_REPL]
    small_shapes = [s for _, s, _ in SMALL_SHARDED] + [s for _, s in SMALL_REPL]
    s_rows = _rows(sum(int(np.prod(s)) for s in small_shapes))
    small_out = adamw(_pack([given[n] for n in small_names], s_rows, F32)[None],
                      _pack([given["m_" + n] for n in small_names], s_rows, F32)[None],
                      _pack([given["v_" + n] for n in small_names], s_rows, F32)[None],
                      _pack(small_g, s_rows, F32)[None, None], "adamw_small")
    small_out = [_unpack(o_, small_shapes) for o_ in small_out]

    res = [{}, {}, {}, {}]
    for k in range(4):
        for idx, (n, _, _) in enumerate(BIG):
            res[k][n] = jnp.swapaxes(big_out[idx][k], 1, 2) if n in TRANSPOSED else big_out[idx][k]
        for idx, n in enumerate(small_names):
            res[k][n] = small_out[k][idx]
    return (loss, grad_x, *[res[0][n] for n in WEIGHTS], *[res[1][n] for n in WEIGHTS],
            *[res[2][n] for n in WEIGHTS], *[res[3][n] for n in WEIGHTS])
```

```python
import math
from typing import NamedTuple

import numpy as np
import jax
import jax.numpy as jnp
from jax import lax
from jax.experimental import pallas as pl
from jax.experimental.pallas import tpu as pltpu

F32 = jnp.float32
BF16 = jnp.bfloat16
MESH_ID = pl.DeviceIdType.MESH

N_DEV = 8
S = 2048
D = 1024
DEPTH = 4
D_FF = 2816
D_PLE = 256
EPS = 1e-6
NEG = -1e30
LANES = 128
HEADS = 16
PAIRS = 8
Q_RANK = 384
KV_RANK = 256
QBLK = 128
DIL = ((128, 1), (512, 4), (2048, 16))
REL_BUCKETS = 32
FOX_W = 3200
VMEM_LIMIT = 56 * 1024 * 1024

ADAM_LR, ADAM_B1, ADAM_B2, ADAM_EPS, ADAM_WD, ADAM_STEP = 1e-3, 0.9, 0.999, 1e-8, 0.01, 10


TRANSPOSED = ("ffn_w_in",)
BIG = (
    ("ffn_w_in", (4, 704, 1024), 1), ("ffn_w_out", (4, 352, 1024), 1),
    ("ple_w_proj", (4, 256, 128), 2), ("ple_w_gate", (4, 128, 1024), 1),
    ("mla_w_a", (2, 128, 672), 1), ("mla_w_uq", (2, 384, 192), 2),
    ("mla_w_ukv", (2, 256, 256), 2), ("mla_w_o", (2, 128, 1024), 1),
    ("dil_w_qkv", (1, 1024, 1152), 2), ("dil_w_o", (1, 128, 1024), 1),
    ("fox_w_qkvf", (1, 1024, 386), 2), ("fox_w_o", (1, 128, 1024), 1),
)
SMALL_SHARDED = (("norm_g", (4, 4, 128), 2), ("mla_q_norm", (2, 48), 1), ("mla_kv_norm", (2, 32), 1))
SMALL_REPL = (("rel_bias", (32, 48)), ("fox_b_f", (1, 16)))
WEIGHTS = ("norm_g", "ffn_w_in", "ffn_w_out", "ple_w_proj", "ple_w_gate", "rel_bias", "mla_w_a", "mla_q_norm",
           "mla_kv_norm", "mla_w_uq", "mla_w_ukv", "mla_w_o", "dil_w_qkv", "dil_w_o", "fox_w_qkvf", "fox_b_f",
           "fox_w_o")


def _rows(n):
    return -(-n // (8 * LANES)) * 8


def _t5_bucket_np(dist):
    max_exact = REL_BUCKETS // 2
    n = np.maximum(dist.astype(np.float32), np.float32(1.0))
    large = max_exact + (np.log(n / np.float32(max_exact)) / np.float32(math.log(2048 / max_exact))
                         * np.float32(REL_BUCKETS - max_exact)).astype(np.int32)
    large = np.minimum(large, REL_BUCKETS - 1)
    return np.where(dist < max_exact, dist, large)


def _bucket_onehot(dilation):
    i = np.arange(QBLK)[:, None]
    j = np.arange(2 * QBLK)[None, :]
    bucket = _t5_bucket_np(np.clip(QBLK + i - j, 0, None) * dilation).reshape(-1)
    return (np.arange(REL_BUCKETS)[:, None] == bucket[None, :]).astype(np.float32)


def _rope_inv_lanes():
    half = 16
    inv = (np.float32(10000.0) ** (-np.arange(half, dtype=np.float32) / np.float32(half))).astype(np.float32)
    t = np.zeros((1, LANES), np.float32)
    t[0, 0:16] = inv
    t[0, 64:80] = inv
    return t


def _params(sem=None):
    return pltpu.CompilerParams(dimension_semantics=sem, vmem_limit_bytes=VMEM_LIMIT)


def _tile(dim, target):
    if dim <= target or dim % target == 0:
        return min(dim, target)
    t = (target // LANES) * LANES
    while dim % t:
        t -= LANES
    return t


_DIMS = {"nn": (((1,), (0,)), ((), ())), "nt": (((1,), (1,)), ((), ())), "tn": (((0,), (0,)), ((), ()))}


class Lay(NamedTuple):
    arr: jax.Array
    l: int


class Dev(NamedTuple):
    arr: jax.Array
    l: int


def _lshape(op):
    if isinstance(op, Dev):
        g, _, r, w = op.arr.shape
        return r, g * w
    return op.arr.shape[1:] if isinstance(op, Lay) else op.shape


def _op_spec(op, rows_t, cols_t, row_ix, col_ix):
    if isinstance(op, Dev):
        w = op.arr.shape[3]
        assert w % cols_t == 0 and (cols_t % LANES == 0 or cols_t == w), (w, cols_t)
        nb, l = w // cols_t, op.l
        return pl.BlockSpec((1, 1, rows_t, cols_t),
                            lambda i, j, k: (col_ix(i, j, k) // nb, l, row_ix(i, j, k), col_ix(i, j, k) % nb))
    if isinstance(op, Lay):
        l = op.l
        return pl.BlockSpec((1, rows_t, cols_t), lambda i, j, k: (l, row_ix(i, j, k), col_ix(i, j, k)))
    return pl.BlockSpec((rows_t, cols_t), lambda i, j, k: (row_ix(i, j, k), col_ix(i, j, k)))


def _mat(ref):
    return ref[(0,) * (len(ref.shape) - 2)]


def mm(a, b, mode, name, out_dtype=F32, precise=False, add=None, out_dev=None, stack=None, after=None, b_cols=None,
       tm=1024, tn=512, tk=2048):
    (ar, ac), (br, bc) = _lshape(a), _lshape(b)
    M, K = (ac, ar) if mode == "tn" else (ar, ac)
    N = br if mode == "nt" else bc
    assert K == (bc if mode == "nt" else br)
    tm, tn, tk = _tile(M, tm), _tile(N, tn), _tile(K, tk)
    nk = K // tk
    j0, n_blocks = b_cols if b_cols is not None else (0, N // tn)
    N = n_blocks * tn
    ix_i, ix_j, ix_k = (lambda i, j, k: i), (lambda i, j, k: j), (lambda i, j, k: k)
    ix_jb = lambda i, j, k: j + j0
    a_spec = _op_spec(a, tk, tm, ix_k, ix_i) if mode == "tn" else _op_spec(a, tm, tk, ix_i, ix_k)
    b_spec = _op_spec(b, tn, tk, ix_jb, ix_k) if mode == "nt" else _op_spec(b, tk, tn, ix_k, ix_jb)
    buf, n_l, l = stack if stack is not None else (None, 1, 0)
    if out_dev is not None:
        out = Dev(jax.ShapeDtypeStruct((N // out_dev, n_l, M, out_dev), out_dtype), l)
    elif stack is not None:
        out = Lay(jax.ShapeDtypeStruct((n_l, M, N), out_dtype), l)
    else:
        out = jax.ShapeDtypeStruct((M, N), out_dtype)
    o_spec = _op_spec(out, tm, tn, ix_i, ix_j)
    n_in = 3 if add is not None else 2

    def body(*refs):
        a_ref, b_ref = refs[0], refs[1]
        o_ref = refs[n_in + (buf is not None) + (after is not None)]
        if precise:
            part = lax.dot_general(_mat(a_ref), _mat(b_ref), _DIMS[mode], precision=lax.Precision.HIGHEST,
                                   preferred_element_type=F32)
        else:
            part = lax.dot_general(_mat(a_ref).astype(BF16), _mat(b_ref).astype(BF16), _DIMS[mode],
                                   preferred_element_type=F32)

        def finish(r):
            r = r + refs[2][...] if add is not None else r
            o_ref[...] = r.astype(o_ref.dtype).reshape(o_ref.shape)

        if nk == 1:
            finish(part)
            return
        acc, k = refs[-1], pl.program_id(2)

        @pl.when(k == 0)
        def _():
            acc[...] = part

        @pl.when(k > 0)
        def _():
            acc[...] += part

        @pl.when(k == nk - 1)
        def _():
            finish(acc[...])

    ins = [getattr(a, "arr", a), getattr(b, "arr", b)] + ([add] if add is not None else [])
    in_specs = [a_spec, b_spec] + ([o_spec] if add is not None else [])
    aliases = {}
    if buf is not None:
        ins.append(buf)
        in_specs.append(pl.BlockSpec(memory_space=pl.ANY))
        aliases = {n_in: 0}
    if after is not None:
        ins.append(after)
        in_specs.append(pl.BlockSpec(memory_space=pl.ANY))
    return pl.pallas_call(
        body, name=name, grid=(M // tm, N // tn, nk), in_specs=in_specs, out_specs=o_spec,
        out_shape=getattr(out, "arr", out), input_output_aliases=aliases,
        scratch_shapes=[pltpu.VMEM((tm, tn), F32)] if nk > 1 else [],
        compiler_params=_params(("parallel", "parallel", "arbitrary")),
    )(*ins)


def _rows_call(body, name, ins, outs, tr=512, acc_outs=()):
    n = ins[0].shape[0]
    tr = min(tr, n)
    in_specs = [pl.BlockSpec((tr, a.shape[1]), lambda i: (i, 0)) if a.shape[0] == n else
                pl.BlockSpec(a.shape, lambda i: (0, 0)) for a in ins]
    out_specs = [pl.BlockSpec((tr, w), lambda i: (i, 0)) for w, _ in outs] + \
                [pl.BlockSpec((1, w), lambda i: (0, 0)) for w in acc_outs]
    out_shape = [jax.ShapeDtypeStruct((n, w), dt) for w, dt in outs] + \
                [jax.ShapeDtypeStruct((1, w), F32) for w in acc_outs]
    res = pl.pallas_call(body, name=name, grid=(n // tr,), in_specs=in_specs, out_specs=out_specs,
                         out_shape=out_shape, compiler_params=_params(("arbitrary",)))(*ins)
    return res[0] if len(res) == 1 else res


def _acc(ref, val):
    @pl.when(pl.program_id(0) == 0)
    def _():
        ref[...] = jnp.zeros_like(ref)

    ref[...] += val


def rms_fwd(x, g, name, res=None, out_dtype=F32):
    def body(*refs):
        x_ref, g_ref = refs[0], refs[1]
        o_ref = refs[-1]
        xv = x_ref[...]
        y = xv * lax.rsqrt(jnp.mean(xv * xv, axis=-1, keepdims=True) + EPS) * g_ref[...]
        o_ref[...] = (y + refs[2][...] if res is not None else y).astype(o_ref.dtype)

    ins = [x, g] + ([res] if res is not None else [])
    return _rows_call(body, name, ins, [(x.shape[1], out_dtype)])


def rms_bwd(x, g, dy, name, res=None, out_dtype=F32, after=None):
    def body(*refs):
        x_ref, g_ref, dy_ref = refs[:3]
        dx_ref, dg_ref = refs[-2], refs[-1]
        xv, dyv = x_ref[...], dy_ref[...]
        r = lax.rsqrt(jnp.mean(xv * xv, axis=-1, keepdims=True) + EPS)
        xh = xv * r
        dxh = dyv * g_ref[...]
        dx = r * (dxh - xh * jnp.mean(dxh * xh, axis=-1, keepdims=True))
        dx_ref[...] = (dx + refs[3][...] if res is not None else dx).astype(dx_ref.dtype)
        _acc(dg_ref, jnp.sum(dyv * xh, axis=0, keepdims=True))

    ins = [x, g, dy] + ([res] if res is not None else []) + ([after] if after is not None else [])
    return _rows_call(body, name, ins, [(x.shape[1], out_dtype)], acc_outs=(x.shape[1],))


def _sigmoid(x):
    return 0.5 * jnp.tanh(0.5 * x) + 0.5


def swiglu_fwd(gu, name):
    def body(gu_ref, o_ref):
        gate = gu_ref[:, :D_FF]
        o_ref[...] = (gate * _sigmoid(gate) * gu_ref[:, D_FF:]).astype(BF16)

    return _rows_call(body, name, [gu], [(D_FF, BF16)], tr=256)


def swiglu_bwd(gu, dact, name):
    def body(gu_ref, d_ref, o_ref):
        gate, d = gu_ref[:, :D_FF], d_ref[...]
        sg = _sigmoid(gate)
        o_ref[:, :D_FF] = (d * gu_ref[:, D_FF:] * sg * (1.0 + gate * (1.0 - sg))).astype(BF16)
        o_ref[:, D_FF:] = (d * gate * sg).astype(BF16)

    return _rows_call(body, name, [gu, dact], [(2 * D_FF, BF16)], tr=256)


def ple_fwd(h, pp, gt, name):
    def body(h_ref, p_ref, g_ref, o_ref):
        o_ref[...] = h_ref[...] + p_ref[...] * _sigmoid(g_ref[...])

    return _rows_call(body, name, [h, pp, gt], [(D, F32)])


def ple_bwd(dh, pp, gt, name):
    def body(d_ref, p_ref, g_ref, dp_ref, dg_ref):
        d, sg = d_ref[...], _sigmoid(g_ref[...])
        dp_ref[...] = (d * sg).astype(BF16)
        dg_ref[...] = (d * p_ref[...] * sg * (1.0 - sg)).astype(BF16)

    return _rows_call(body, name, [dh, pp, gt], [(D, BF16), (D, BF16)])


def loss_head(y, target, name):
    def body(y_ref, t_ref, d_ref, l_ref):
        e = y_ref[...] - t_ref[...]
        d_ref[...] = e * (1.0 / D)
        col = jnp.sum(e * e, axis=0, keepdims=True) * (0.5 / D)
        _acc(l_ref, sum(col[:, LANES * c:LANES * (c + 1)] for c in range(D // LANES)))

    return _rows_call(body, name, [y, target], [(D, F32)], acc_outs=(LANES,))


def rope_tables(pos_col, name):
    inv = jnp.asarray(_rope_inv_lanes())

    def body(p_ref, inv_ref, c_ref, s_ref):
        ang = p_ref[...].astype(F32) * inv_ref[...]
        lane = lax.broadcasted_iota(jnp.int32, ang.shape, 1)
        first, second = lane < 16, (lane >= 64) & (lane < 80)
        c_ref[...] = jnp.where(first | second, jnp.cos(ang), 1.0)
        sn = jnp.sin(ang)
        s_ref[...] = jnp.where(first, -sn, jnp.where(second, sn, 0.0))

    return _rows_call(body, name, [pos_col, inv], [(LANES, F32), (LANES, F32)])


def _rope(x, c, s):
    return x * c + pltpu.roll(x, 64, axis=1) * s


def _rope_t(d, c, s):
    return d * c + pltpu.roll(d * s, 64, axis=1)


def mla_qk_fwd(qp, kvp, kr, cos, sin, name):
    def body(q_ref, k_ref, kr_ref, c_ref, s_ref, qo_ref, ko_ref):
        c, s = c_ref[...], s_ref[...]
        kr_rot = _rope(kr_ref[...], c, s)
        for h in range(HEADS):
            sl = slice(LANES * h, LANES * (h + 1))
            qo_ref[:, sl] = _rope(q_ref[:, sl], c, s).astype(BF16)
            ko_ref[:, sl] = (k_ref[:, sl] + kr_rot).astype(BF16)

    n = qp.shape[0]
    tr = 256
    w = HEADS * LANES
    return pl.pallas_call(
        body, name=name, grid=(n // tr,),
        in_specs=[pl.BlockSpec((tr, w), lambda i: (i, 0)), pl.BlockSpec((tr, w), lambda i: (i, 0)),
                  pl.BlockSpec((tr, LANES), lambda i: (i, 0)), pl.BlockSpec((tr, LANES), lambda i: (i, 0)),
                  pl.BlockSpec((tr, LANES), lambda i: (i, 0))],
        out_specs=[pl.BlockSpec((tr, w), lambda i: (i, 0))] * 2,
        out_shape=[jax.ShapeDtypeStruct((n, w), BF16)] * 2, compiler_params=_params(("arbitrary",)),
    )(qp, kvp, kr, cos, sin)


def mla_qk_bwd(dq, dk, cos, sin, name):
    def body(dq_ref, dk_ref, c_ref, s_ref, dqp_ref, dkr_ref):
        c, s = c_ref[...], s_ref[...]
        tot = jnp.zeros(c.shape, F32)
        for h in range(HEADS):
            sl = slice(LANES * h, LANES * (h + 1))
            dqp_ref[:, sl] = _rope_t(dq_ref[:, sl], c, s).astype(BF16)
            tot = tot + dk_ref[:, sl]
        dkr_ref[...] = _rope_t(tot, c, s).astype(BF16)

    return _rows_call(body, name, [dq, dk, cos, sin], [(HEADS * LANES, BF16), (LANES, BF16)])


TQ_FWD = 512
TQ_BWD = 256


def _pair_masks(shape):
    lane = lax.broadcasted_iota(jnp.int32, shape, 1)
    return (lane < 64, lane >= 64)


def _scaled_q(q_a, scale):
    return (q_a * jnp.asarray(scale, q_a.dtype), None) if scale == 0.125 else (q_a, scale)


def _causal_probs(q_a, k_a, scale, b0, cq, ck):
    s = lax.dot_general(q_a, k_a, _DIMS["nt"], preferred_element_type=F32)
    if scale is not None:
        s = s * scale
    if cq is not None:
        s = s + (cq - ck)
    tq = s.shape[0]
    row = lax.broadcasted_iota(jnp.int32, (tq, tq), 0)
    col = lax.broadcasted_iota(jnp.int32, (tq, tq), 1)
    diag = jnp.where(col <= row, s[:, b0:], NEG)
    s = diag if b0 == 0 else jnp.concatenate([s[:, :b0], diag], axis=1)
    e = jnp.exp(s - jnp.max(s, axis=-1, keepdims=True))
    return e * (1.0 / jnp.sum(e, axis=-1, keepdims=True))


def attn_fwd(q, k, v, name, *, wide, scale, q_off=0, k_off=0, v_off=0, cum=None, cum_t=None):
    qw = 2 * LANES if wide else LANES
    forget = cum is not None

    def body(*refs):
        q_ref, k_ref, v_ref = refs[:3]
        o_ref = refs[-1]
        m0, m1 = _pair_masks((TQ_FWD, LANES))
        for qi in range(S // TQ_FWD):
            b0, b1 = qi * TQ_FWD, (qi + 1) * TQ_FWD
            outs = []
            for a, msk in enumerate((m0, m1)):
                if wide:
                    q_a, k_a = q_ref[b0:b1, LANES * a:LANES * (a + 1)], k_ref[:b1, LANES * a:LANES * (a + 1)]
                else:
                    q_a, k_a = jnp.where(msk, q_ref[b0:b1, :], jnp.zeros((), BF16)), k_ref[:b1, :]
                cq = refs[3][0, b0:b1, a:a + 1] if forget else None
                ck = refs[4][0, a:a + 1, :b1] if forget else None
                q_a, left = _scaled_q(q_a, scale)
                p = _causal_probs(q_a, k_a, left, b0, cq, ck)
                outs.append(jnp.dot(p.astype(BF16), v_ref[:b1, :], preferred_element_type=F32))
            o_ref[b0:b1, :] = jnp.where(m0, outs[0], outs[1])

    in_specs = [pl.BlockSpec((S, qw), lambda h: (0, q_off * LANES // qw + h)),
                pl.BlockSpec((S, qw), lambda h: (0, k_off * LANES // qw + h)),
                pl.BlockSpec((S, LANES), lambda h: (0, v_off + h))]
    ins = [q, k, v]
    if forget:
        in_specs += [pl.BlockSpec((1, S, 2), lambda h: (h, 0, 0)), pl.BlockSpec((1, 2, S), lambda h: (h, 0, 0))]
        ins += [cum, cum_t]
    return pl.pallas_call(
        body, name=name, grid=(PAIRS,), in_specs=in_specs, out_specs=pl.BlockSpec((S, LANES), lambda h: (0, h)),
        out_shape=jax.ShapeDtypeStruct((S, PAIRS * LANES), F32), compiler_params=_params(("arbitrary",)),
    )(*ins)


def attn_bwd(q, k, v, o, do, name, *, wide, scale, out_dtype=F32, q_off=0, k_off=0, v_off=0, cum=None, cum_t=None):
    qw = 2 * LANES if wide else LANES
    forget = cum is not None

    def body(*refs):
        q_ref, k_ref, v_ref, o_ref, do_ref = refs[:5]
        n_out = 5 if forget else 3
        outs = refs[-(n_out + 2):-2]
        dq_ref, dk_ref, dv_ref = outs[:3]
        dk_acc, dv_acc = refs[-2], refs[-1]
        dk_acc[...] = jnp.zeros_like(dk_acc)
        dv_acc[...] = jnp.zeros_like(dv_acc)
        if forget:
            dcq_ref, dck_ref = outs[3], outs[4]
            dck_ref[...] = jnp.zeros_like(dck_ref)
        m0, m1 = _pair_masks((TQ_BWD, LANES))
        for qi in range(S // TQ_BWD):
            b0, b1 = qi * TQ_BWD, (qi + 1) * TQ_BWD
            do2 = do_ref[b0:b1, :]
            dd = do2 * o_ref[b0:b1, :]
            do_b = do2.astype(BF16)
            mk0, mk1 = _pair_masks((b1, LANES))
            dqs = []
            for a, (msk, mk) in enumerate(((m0, mk0), (m1, mk1))):
                lanes = slice(LANES * a, LANES * (a + 1)) if wide else slice(0, LANES)
                if wide:
                    q_a, k_a = q_ref[b0:b1, lanes], k_ref[:b1, lanes]
                else:
                    q_a, k_a = jnp.where(msk, q_ref[b0:b1, :], jnp.zeros((), BF16)), k_ref[:b1, :]
                cq = refs[5][0, b0:b1, a:a + 1] if forget else None
                ck = refs[6][0, a:a + 1, :b1] if forget else None
                q_a, left = _scaled_q(q_a, scale)
                p = _causal_probs(q_a, k_a, left, b0, cq, ck)
                dp = lax.dot_general(jnp.where(msk, do_b, jnp.zeros((), BF16)), v_ref[:b1, :], _DIMS["nt"],
                                     preferred_element_type=F32)
                delta = jnp.sum(jnp.where(msk, dd, 0.0), axis=-1, keepdims=True)
                ds = p * (dp - delta)
                if forget:
                    dcq_ref[0, b0:b1, a:a + 1] = jnp.sum(ds, axis=-1, keepdims=True)
                    dck_ref[0, a:a + 1, :b1] -= jnp.sum(ds, axis=0, keepdims=True)
                ds_b = ds.astype(BF16)
                dqs.append(jnp.dot(ds_b, k_a, preferred_element_type=F32) * scale)
                dk_a = lax.dot_general(ds_b, q_a, _DIMS["tn"], preferred_element_type=F32)
                dk_acc[:b1, lanes] += dk_a if left is None else dk_a * scale
                dv_acc[:b1, :] += jnp.where(mk, lax.dot_general(p.astype(BF16), do_b, _DIMS["tn"],
                                                                 preferred_element_type=F32), 0.0)
            if wide:
                dq_ref[b0:b1, :LANES] = dqs[0].astype(out_dtype)
                dq_ref[b0:b1, LANES:] = dqs[1].astype(out_dtype)
            else:
                dq_ref[b0:b1, :] = jnp.where(m0, dqs[0], dqs[1]).astype(out_dtype)
        dk_ref[...] = dk_acc[...].astype(out_dtype)
        dv_ref[...] = dv_acc[...].astype(out_dtype)

    pair = pl.BlockSpec((S, LANES), lambda h: (0, h))
    qk_out = pl.BlockSpec((S, qw), lambda h: (0, h))
    in_specs = [pl.BlockSpec((S, qw), lambda h: (0, q_off * LANES // qw + h)),
                pl.BlockSpec((S, qw), lambda h: (0, k_off * LANES // qw + h)),
                pl.BlockSpec((S, LANES), lambda h: (0, v_off + h)), pair, pair]
    ins = [q, k, v, o, do]
    out_specs = [qk_out, qk_out, pair]
    out_shape = [jax.ShapeDtypeStruct((S, PAIRS * qw), out_dtype)] * 2 + \
                [jax.ShapeDtypeStruct((S, PAIRS * LANES), out_dtype)]
    if forget:
        by_q, by_k = pl.BlockSpec((1, S, 2), lambda h: (h, 0, 0)), pl.BlockSpec((1, 2, S), lambda h: (h, 0, 0))
        in_specs += [by_q, by_k]
        ins += [cum, cum_t]
        out_specs += [by_q, by_k]
        out_shape += [jax.ShapeDtypeStruct((PAIRS, S, 2), F32), jax.ShapeDtypeStruct((PAIRS, 2, S), F32)]
    return pl.pallas_call(
        body, name=name, grid=(PAIRS,), in_specs=in_specs, out_specs=out_specs, out_shape=out_shape,
        scratch_shapes=[pltpu.VMEM((S, qw), F32), pltpu.VMEM((S, LANES), F32)],
        compiler_params=_params(("arbitrary",)),
    )(*ins)


def _tri(lower):
    r = lax.broadcasted_iota(jnp.int32, (QBLK, QBLK), 0)
    c = lax.broadcasted_iota(jnp.int32, (QBLK, QBLK), 1)
    return jnp.where((c <= r) if lower else (c >= r), 1.0, 0.0).astype(F32)


def _hi_dot(a, b):
    return jnp.dot(a, b, precision=lax.Precision.HIGHEST, preferred_element_type=F32)


def fox_gate_fwd(fl, bias, name):
    def body(f_ref, b_ref, o_ref):
        tri = _tri(True)
        carry = jnp.zeros((1, LANES), F32)
        for n in range(S // QBLK):
            x = f_ref[n * QBLK:(n + 1) * QBLK, :].astype(F32) + b_ref[...]
            lf = jnp.minimum(x, 0.0) - jnp.log(1.0 + jnp.exp(-jnp.abs(x)))
            c = _hi_dot(tri, lf) + carry
            o_ref[n * QBLK:(n + 1) * QBLK, :] = c
            carry = c[QBLK - 1:QBLK, :]

    return pl.pallas_call(body, name=name, out_shape=jax.ShapeDtypeStruct((S, LANES), F32),
                          compiler_params=_params())(fl, bias)


def fox_gate_bwd(fl, bias, dcq, dck, name):
    def body(f_ref, b_ref, dq_ref, dk_ref, o_ref, db_ref):
        tri = _tri(False)
        carry = jnp.zeros((1, LANES), F32)
        db = jnp.zeros((1, LANES), F32)
        for n in reversed(range(S // QBLK)):
            rows = slice(n * QBLK, (n + 1) * QBLK)
            dlf = _hi_dot(tri, dq_ref[rows, :] + dk_ref[rows, :]) + carry
            carry = dlf[0:1, :]
            x = f_ref[rows, :].astype(F32) + b_ref[...]
            dx = dlf * (1.0 - _sigmoid(x))
            o_ref[rows, :] = dx
            db = db + jnp.sum(dx, axis=0, keepdims=True)
        db_ref[...] = db

    return pl.pallas_call(body, name=name, out_shape=[jax.ShapeDtypeStruct((S, LANES), F32),
                                                      jax.ShapeDtypeStruct((1, LANES), F32)],
                          compiler_params=_params())(fl, bias, dcq, dck)


def _band_valid(first):
    w = QBLK if first else 2 * QBLK
    i = lax.broadcasted_iota(jnp.int32, (QBLK, w), 0)
    j = lax.broadcasted_iota(jnp.int32, (QBLK, w), 1)
    return (j <= i) if first else ((j >= i) & (j - QBLK <= i))


def _band_q(q_ref, rows, msk):
    return jnp.where(msk, q_ref[rows, :], jnp.zeros((), BF16)) * jnp.asarray(0.125, BF16)


def _band_logits(q_a, kk, bias, first):
    s = lax.dot_general(q_a, kk, _DIMS["nt"], preferred_element_type=F32) + bias
    return jnp.where(_band_valid(first), s, NEG)


def residue_major(t, d):
    return t if d == 1 else t.reshape(S // d, d, t.shape[1]).transpose(1, 0, 2).reshape(S, t.shape[1])


def token_major(t, d):
    return t if d == 1 else t.reshape(d, S // d, t.shape[1]).transpose(1, 0, 2).reshape(S, t.shape[1])


def dil_fwd(qkv, bias, g, name):
    d = DIL[g][1]
    ls = S // d

    def body(q_ref, k_ref, v_ref, b_ref, o_ref, l_ref):
        m0, m1 = _pair_masks((QBLK, LANES))
        for n in range(ls // QBLK):
            rows = slice(n * QBLK, (n + 1) * QBLK)
            keys = rows if n == 0 else slice((n - 1) * QBLK, (n + 1) * QBLK)
            os_, ls_ = [], []
            for a, msk in enumerate((m0, m1)):
                q_a = _band_q(q_ref, rows, msk)
                bias_a = b_ref[a, :, QBLK:] if n == 0 else b_ref[a]
                s = _band_logits(q_a, k_ref[keys, :], bias_a, n == 0)
                mx = jnp.max(s, axis=-1, keepdims=True)
                e = jnp.exp(s - mx)
                l = jnp.sum(e, axis=-1, keepdims=True)
                os_.append(jnp.dot((e * (1.0 / l)).astype(BF16), v_ref[keys, :], preferred_element_type=F32))
                ls_.append(mx + jnp.log(l))
            o_ref[rows, :] = jnp.where(m0, os_[0], os_[1])
            l_ref[rows, :] = jnp.where(m0, ls_[0], ls_[1])

    def col(j):
        return lambda h, r: (r, j * PAIRS + h)

    out = pl.BlockSpec((ls, LANES), lambda h, r: (r, h))
    return pl.pallas_call(
        body, name=name, grid=(PAIRS, d),
        in_specs=[pl.BlockSpec((ls, LANES), col(0)), pl.BlockSpec((ls, LANES), col(1)), pl.BlockSpec((ls, LANES), col(2)),
                  pl.BlockSpec((2, QBLK, 2 * QBLK), lambda h, r: (h, 0, 0))],
        out_specs=[out, out], out_shape=[jax.ShapeDtypeStruct((S, D), F32)] * 2,
        compiler_params=_params(("arbitrary", "arbitrary")),
    )(qkv, qkv, qkv, bias)


def dil_merge(os_, lses, name):
    def body(o0, o1, o2, l0, l1, l2, o_ref, l_ref):
        ls_ = [l0[...], l1[...], l2[...]]
        mx = jnp.maximum(jnp.maximum(ls_[0], ls_[1]), ls_[2])
        tot = mx + jnp.log(sum(jnp.exp(l - mx) for l in ls_))
        o_ref[...] = sum(jnp.exp(l - tot) * o[...] for l, o in zip(ls_, (o0, o1, o2)))
        l_ref[...] = tot

    return _rows_call(body, name, list(os_) + list(lses), [(D, F32), (D, F32)])


def dil_bwd(qkv, bias, o, lse, do, g, name):
    d = DIL[g][1]
    ls = S // d

    def body(q_ref, k_ref, v_ref, b_ref, o_ref, l_ref, do_ref, dq_ref, dk_ref, dv_ref, db_ref, dk_acc, dv_acc):
        @pl.when(pl.program_id(1) == 0)
        def _():
            db_ref[...] = jnp.zeros_like(db_ref)

        dk_acc[...] = jnp.zeros_like(dk_acc)
        dv_acc[...] = jnp.zeros_like(dv_acc)
        m0, m1 = _pair_masks((QBLK, LANES))
        for n in range(ls // QBLK):
            rows = slice(n * QBLK, (n + 1) * QBLK)
            keys = rows if n == 0 else slice((n - 1) * QBLK, (n + 1) * QBLK)
            nk = QBLK if n == 0 else 2 * QBLK
            do2, lse2 = do_ref[rows, :], l_ref[rows, :]
            dd = do2 * o_ref[rows, :]
            do_b = do2.astype(BF16)
            mk0, mk1 = _pair_masks((nk, LANES))
            dqs = []
            for a, (msk, mk) in enumerate(((m0, mk0), (m1, mk1))):
                q_a = _band_q(q_ref, rows, msk)
                kk = k_ref[keys, :]
                bias_a = b_ref[a, :, QBLK:] if n == 0 else b_ref[a]
                s = _band_logits(q_a, kk, bias_a, n == 0)
                lse_a = jnp.max(jnp.where(msk, lse2, -jnp.inf), axis=-1, keepdims=True)
                p = jnp.exp(s - lse_a)
                dp = lax.dot_general(jnp.where(msk, do_b, jnp.zeros((), BF16)), v_ref[keys, :], _DIMS["nt"],
                                     preferred_element_type=F32)
                delta = jnp.sum(jnp.where(msk, dd, 0.0), axis=-1, keepdims=True)
                ds = p * (dp - delta)
                if n == 0:
                    db_ref[a, :, QBLK:] += ds
                else:
                    db_ref[a] += ds
                ds_b = ds.astype(BF16)
                dqs.append(jnp.dot(ds_b, kk, preferred_element_type=F32) * 0.125)
                dk_acc[keys, :] += lax.dot_general(ds_b, q_a, _DIMS["tn"], preferred_element_type=F32)
                dv_acc[keys, :] += jnp.where(mk, lax.dot_general(p.astype(BF16), do_b, _DIMS["tn"],
                                                                 preferred_element_type=F32), 0.0)
            dq_ref[rows, :] = jnp.where(m0, dqs[0], dqs[1]).astype(BF16)
        dk_ref[...] = dk_acc[...].astype(BF16)
        dv_ref[...] = dv_acc[...].astype(BF16)

    def col(j):
        return lambda h, r: (r, j * PAIRS + h)

    nat = pl.BlockSpec((ls, LANES), lambda h, r: (r, h))
    b_spec = pl.BlockSpec((2, QBLK, 2 * QBLK), lambda h, r: (h, 0, 0))
    return pl.pallas_call(
        body, name=name, grid=(PAIRS, d),
        in_specs=[pl.BlockSpec((ls, LANES), col(0)), pl.BlockSpec((ls, LANES), col(1)), pl.BlockSpec((ls, LANES), col(2)),
                  b_spec, nat, nat, nat],
        out_specs=[nat, nat, nat, b_spec],
        out_shape=[jax.ShapeDtypeStruct((S, D), BF16)] * 3 + [jax.ShapeDtypeStruct((HEADS, QBLK, 2 * QBLK), F32)],
        scratch_shapes=[pltpu.VMEM((ls, LANES), F32), pltpu.VMEM((ls, LANES), F32)],
        compiler_params=_params(("arbitrary", "arbitrary")),
    )(qkv, qkv, qkv, bias, o, lse, do)


def _place():
    x, y, c = lax.axis_index("x"), lax.axis_index("y"), lax.axis_index("c")
    return x, y, c


def _dev_slot(ref, by_rows, dev):
    return ref.at[:, dev] if by_rows else ref.at[dev]


def all_gather(shards, by_rows, name, in_vmem=False, after=()):
    n, n_after = len(shards), len(after)

    def body(*refs):
        x_refs, out_refs = refs[:n], refs[n + n_after:2 * n + n_after]
        send_sems, recv_sems, local_sems = refs[2 * n + n_after:]
        x, y, c = _place()
        me, sibling = (x, y, c), (x, y, 1 - c)
        chips = [(1 - x, y), (x, 1 - y), (1 - x, 1 - y)]

        def slot(t, px, py, pc):
            return _dev_slot(out_refs[t], by_rows[t], 4 * px + 2 * py + pc)

        def copy(t, k, blk, to, src=None):
            return pltpu.make_async_remote_copy(
                src_ref=slot(t, *blk) if src is None else src, dst_ref=slot(t, *blk), send_sem=send_sems.at[7 * t + k],
                recv_sem=recv_sems.at[7 * t + k], device_id=to, device_id_type=MESH_ID)

        mine = [pltpu.make_async_copy(x_refs[t], slot(t, *me), local_sems.at[t]) for t in range(n)]
        for cp in mine:
            cp.start()
        first = []
        for t in range(n):
            first.append(copy(t, 0, me, sibling, src=x_refs[t]))
            first += [copy(t, 1 + j, me, (*chip, c), src=x_refs[t]) for j, chip in enumerate(chips)]
        for cp in first:
            cp.start()
        passed = []
        for j, chip in enumerate(chips):
            for t in range(n):
                copy(t, 1 + j, (*chip, c), me).wait_recv()
                passed.append(copy(t, 4 + j, (*chip, c), sibling))
                passed[-1].start()
        for t in range(n):
            copy(t, 0, sibling, me).wait_recv()
            for j, chip in enumerate(chips):
                copy(t, 4 + j, (*chip, 1 - c), me).wait_recv()
        for cp in first + passed:
            cp.wait_send()
        for cp in mine:
            cp.wait()

    def gathered(s, rows):
        shp = (s.shape[0], N_DEV) + s.shape[1:] if rows else (N_DEV,) + s.shape
        return jax.ShapeDtypeStruct(shp, s.dtype)

    space = pl.BlockSpec(memory_space=pltpu.VMEM if in_vmem else pl.ANY)
    return pl.pallas_call(
        body, name=name, out_shape=[gathered(s, r) for s, r in zip(shards, by_rows)],
        in_specs=[space] * n + [pl.BlockSpec(memory_space=pl.ANY)] * n_after, out_specs=[space] * n,
        scratch_shapes=[pltpu.SemaphoreType.DMA((7 * n,)), pltpu.SemaphoreType.DMA((7 * n,)),
                        pltpu.SemaphoreType.DMA((n,))],
        compiler_params=pltpu.CompilerParams(vmem_limit_bytes=VMEM_LIMIT),
    )(*shards, *after)


_HBM = pl.BlockSpec(memory_space=pltpu.HBM)
_SEM = pl.BlockSpec(memory_space=pltpu.SEMAPHORE)
_SPLIT = dict(has_side_effects=pltpu.SideEffectType.DATAFLOW_SIDE_EFFECTING)


def _hbm(a):
    return pltpu.with_memory_space_constraint(a, pltpu.HBM)


def _gathered_shape(s, rows):
    return (s.shape[0], N_DEV) + s.shape[1:] if rows else (N_DEV,) + s.shape


def _peers(x, y, c):
    return [(1 - x if k & 4 else x, 1 - y if k & 2 else y, 1 - c if k & 1 else c) for k in range(1, N_DEV)]


def gather_start(shards, lands, by_rows, name):
    n = len(shards)

    def body(*refs):
        x_refs, land_refs = refs[:n], refs[n:2 * n]
        send_sems, recv_sems = refs[2 * n], refs[2 * n + 1]
        x, y, c = _place()
        me = 4 * x + 2 * y + c
        for t in range(n):
            for k, peer in enumerate(_peers(x, y, c)):
                pltpu.make_async_remote_copy(
                    src_ref=x_refs[t], dst_ref=_dev_slot(land_refs[t], by_rows[t], me), send_sem=send_sems.at[7 * t + k],
                    recv_sem=recv_sems.at[7 * t + k], device_id=peer, device_id_type=MESH_ID).start()

    sems = pltpu.SemaphoreType.DMA((7 * n,))
    res = pl.pallas_call(
        body, name=name,
        out_shape=(sems, sems) + tuple(pltpu.HBM(a.shape, a.dtype) for a in list(shards) + list(lands)),
        in_specs=[_HBM] * (2 * n), out_specs=(_SEM, _SEM) + (_HBM,) * (2 * n),
        input_output_aliases={i: 2 + i for i in range(2 * n)},
        compiler_params=pltpu.CompilerParams(**_SPLIT),
    )(*[_hbm(a) for a in list(shards) + list(lands)])
    return res[0], res[1], list(res[2:2 + n]), list(res[2 + n:])


def gather_wait(send_sems, recv_sems, first, shards, lands, by_rows, after, name):
    n = len(shards)

    def body(*refs):
        x_refs, land_refs = refs[:n], refs[n:2 * n]
        send_sems, recv_sems = refs[2 * n], refs[2 * n + 1]
        x, y, c = _place()
        for t in range(n):
            for k, (px, py, pc) in enumerate(_peers(x, y, c)):
                cp = pltpu.make_async_remote_copy(
                    src_ref=x_refs[t], dst_ref=_dev_slot(land_refs[t], by_rows[t], 4 * px + 2 * py + pc),
                    send_sem=send_sems.at[7 * (first + t) + k], recv_sem=recv_sems.at[7 * (first + t) + k],
                    device_id=(px, py, pc), device_id_type=MESH_ID)
                cp.wait_send()
                cp.wait_recv()

    res = pl.pallas_call(
        body, name=name, out_shape=tuple(pltpu.HBM(a.shape, a.dtype) for a in list(shards) + list(lands)),
        in_specs=[_HBM] * (2 * n) + [_SEM, _SEM, pl.BlockSpec(memory_space=pl.ANY)], out_specs=(_HBM,) * (2 * n),
        input_output_aliases={i: i for i in range(2 * n)},
        compiler_params=pltpu.CompilerParams(**_SPLIT),
    )(*shards, *lands, send_sems, recv_sems, after)
    return list(res[n:])


def scatter_start(srcs, src_l, lands, land_l, by_rows, name):
    n = len(srcs)

    def body(*refs):
        x_refs, land_refs = refs[:n], refs[n:2 * n]
        send_sems, recv_sems, token = refs[2 * n], refs[2 * n + 1], refs[-1]
        x, y, c = _place()
        me = 4 * x + 2 * y + c
        for k, (px, py, pc) in enumerate(_peers(x, y, c)):
            for t in range(n):
                blk = _dev_slot(x_refs[t], by_rows[t], 4 * px + 2 * py + pc)
                pltpu.make_async_remote_copy(
                    src_ref=blk.at[src_l[t]], dst_ref=land_refs[t].at[me, land_l[t]], send_sem=send_sems.at[7 * t + k],
                    recv_sem=recv_sems.at[7 * t + k], device_id=(px, py, pc), device_id_type=MESH_ID).start()
        token[...] = jnp.zeros_like(token)

    lands = [lax.empty((N_DEV, 1) + s.shape[2:], s.dtype) if ld is None else ld for s, ld in zip(srcs, lands)]
    sems = pltpu.SemaphoreType.DMA((7 * n,))
    res = pl.pallas_call(
        body, name=name,
        out_shape=(sems, sems) + tuple(pltpu.HBM(a.shape, a.dtype) for a in list(srcs) + lands)
        + (jax.ShapeDtypeStruct((8, LANES), F32),),
        in_specs=[_HBM] * (2 * n),
        out_specs=(_SEM, _SEM) + (_HBM,) * (2 * n) + (pl.BlockSpec(memory_space=pltpu.VMEM),),
        input_output_aliases={i: 2 + i for i in range(2 * n)},
        compiler_params=pltpu.CompilerParams(**_SPLIT),
    )(*[_hbm(a) for a in list(srcs) + lands])
    return res[0], res[1], list(res[2:2 + n]), list(res[2 + n:2 + 2 * n]), res[-1]


def scatter_wait(send_sems, recv_sems, srcs, src_l, lands, land_l, by_rows, after, name):
    n = len(srcs)

    def body(*refs):
        x_refs, land_refs = refs[:n], refs[n:2 * n]
        send_sems, recv_sems = refs[2 * n], refs[2 * n + 1]
        x, y, c = _place()
        for k, (px, py, pc) in enumerate(_peers(x, y, c)):
            peer = 4 * px + 2 * py + pc
            for t in range(n):
                cp = pltpu.make_async_remote_copy(
                    src_ref=_dev_slot(x_refs[t], by_rows[t], peer).at[src_l[t]], dst_ref=land_refs[t].at[peer, land_l[t]],
                    send_sem=send_sems.at[7 * t + k], recv_sem=recv_sems.at[7 * t + k], device_id=(px, py, pc),
                    device_id_type=MESH_ID)
                cp.wait_send()
                cp.wait_recv()

    res = pl.pallas_call(
        body, name=name, out_shape=tuple(pltpu.HBM(a.shape, a.dtype) for a in list(srcs) + list(lands)),
        in_specs=[_HBM] * (2 * n) + [_SEM, _SEM, pl.BlockSpec(memory_space=pl.ANY)], out_specs=(_HBM,) * (2 * n),
        input_output_aliases={i: i for i in range(2 * n)},
        compiler_params=pltpu.CompilerParams(**_SPLIT),
    )(*srcs, *lands, send_sems, recv_sems, after)
    return list(res[:n]), list(res[n:])


ADAM_BLOCK_BYTES = 3 << 19


def adamw(w, m, v, parts, name):
    n_parts = parts.shape[0]
    n_l, r, c = w.shape
    lane_c = -(-c // LANES) * LANES
    fits = [t for t in range(16, r, 16) if r % t == 0 and t * lane_c * 4 <= ADAM_BLOCK_BYTES]
    tr = max(fits) if fits and r * lane_c * 4 > ADAM_BLOCK_BYTES else r
    c1 = 1.0 / (1.0 - ADAM_B1 ** ADAM_STEP)
    c2 = 1.0 / (1.0 - ADAM_B2 ** ADAM_STEP)

    def body(w_ref, m_ref, v_ref, p_ref, g_ref, d_ref, nm_ref, nv_ref):
        g = p_ref[0].astype(F32)
        for j in range(1, n_parts):
            g = g + p_ref[j].astype(F32)
        nm = ADAM_B1 * m_ref[...] + (1.0 - ADAM_B1) * g
        nv = ADAM_B2 * v_ref[...] + (1.0 - ADAM_B2) * (g * g)
        g_ref[...] = g
        nm_ref[...] = nm
        nv_ref[...] = nv
        d_ref[...] = -ADAM_LR * ((nm * c1) / (jnp.sqrt(nv * c2) + ADAM_EPS) + ADAM_WD * w_ref[...])

    blk = pl.BlockSpec((1, tr, c), lambda l, i: (l, i, 0))
    return pl.pallas_call(
        body, name=name, grid=(n_l, r // tr),
        in_specs=[blk, blk, blk, pl.BlockSpec((n_parts, 1, tr, c), lambda l, i: (0, l, i, 0))],
        out_specs=[blk] * 4, out_shape=[jax.ShapeDtypeStruct((n_l, r, c), F32)] * 4,
        compiler_params=_params(("parallel", "parallel")),
    )(w, m, v, parts)


def sum_parts(parts, name):
    def body(p_ref, o_ref):
        g = p_ref[0]
        for j in range(1, parts.shape[0]):
            g = g + p_ref[j]
        o_ref[...] = g

    return pl.pallas_call(body, name=name, out_shape=jax.ShapeDtypeStruct(parts.shape[1:], F32),
                          compiler_params=_params())(parts)


def _pack(arrs, rows, dtype):
    flat = jnp.concatenate([a.reshape(-1).astype(dtype) for a in arrs])
    return jnp.pad(flat, (0, rows * LANES - flat.shape[0])).reshape(rows, LANES)


def _unpack(packed, shapes):
    flat, out, off = packed.reshape(-1), [], 0
    for shp in shapes:
        n = int(np.prod(shp))
        out.append(flat[off:off + n].reshape(shp))
        off += n
    return out


def _cat(parts):
    return jnp.concatenate(parts, axis=1)


def _layer_list(i):
    kind, j = i % 3, i // 3
    mix = ([("mla_w_a", j), ("mla_w_uq", j), ("mla_w_ukv", j), ("mla_w_o", j)] if kind == 0 else
           [("dil_w_qkv", 0), ("dil_w_o", 0)] if kind == 1 else [("fox_w_qkvf", 0), ("fox_w_o", 0)])
    return mix + [("ffn_w_in", i), ("ffn_w_out", i), ("ple_w_proj", i), ("ple_w_gate", i)]


def _mla_layout(w_a, g_uq, g_ukv, j):
    def z(r, n):
        return jnp.zeros((r, n), BF16)

    wa = w_a[j]
    a = _cat([wa[:, :640], wa[:, 640:656], z(D, 48), wa[:, 656:672], z(D, 48)])
    q, k, v = [], [], []
    for h in range(HEADS):
        b = g_uq[h // 2, j][:, 96 * (h % 2):96 * (h % 2 + 1)]
        q += [b[:, 64:80], b[:, 0:32], z(Q_RANK, 16), b[:, 80:96], b[:, 32:64], z(Q_RANK, 16)]
        b = g_ukv[h // 2, j][:, LANES * (h % 2):LANES * (h % 2 + 1)]
        k += [z(KV_RANK, 16), b[:, 0:32], z(KV_RANK, 32), b[:, 32:64], z(KV_RANK, 16)]
        v.append(b[:, 64:128])
    return a, _cat(q), _cat(k + v)


def _mla_unlayout(d_a, d_uq, d_ukv):
    a = _cat([d_a[:, :640], d_a[:, 640:656], d_a[:, 704:720]])
    uq, ukv = [], []
    for dev in range(N_DEV):
        q, kv = [], []
        for h in (2 * dev, 2 * dev + 1):
            b = d_uq[:, LANES * h:LANES * (h + 1)]
            q += [b[:, 16:48], b[:, 80:112], b[:, 0:16], b[:, 64:80]]
            b = d_ukv[:, LANES * h:LANES * (h + 1)]
            kv += [b[:, 16:48], b[:, 80:112], d_ukv[:, HEADS * LANES + 64 * h:HEADS * LANES + 64 * (h + 1)]]
        uq.append(_cat(q))
        ukv.append(_cat(kv))
    return a, jnp.stack(uq), jnp.stack(ukv)


def _mixer_fwd(kind, tag, hn, W, aux):
    if kind == 0:
        a = mm(hn, W["w_a"], "nn", f"{tag}_a", tn=768)
        cq = rms_fwd(a[:, :Q_RANK], W["q_norm"], f"{tag}_cq", out_dtype=BF16)
        ckv = rms_fwd(a[:, Q_RANK:Q_RANK + KV_RANK], W["kv_norm"], f"{tag}_ckv", out_dtype=BF16)
        qp = mm(cq, W["w_uq"], "nn", f"{tag}_uq", tk=384)
        kvp = mm(ckv, W["w_ukv"], "nn", f"{tag}_ukv", tk=256)
        q, k = mla_qk_fwd(qp, kvp, a[:, 640:], aux["cos"], aux["sin"], f"{tag}_qk")
        v = kvp.astype(BF16)
        o = attn_fwd(q, k, v, f"{tag}_attn", wide=True, scale=96 ** -0.5, v_off=HEADS)
        y = mm(o, W["w_o"], "nn", f"{tag}_o")
        return y, (a, cq, ckv, q, k, v, o)
    if kind == 1:
        qkv = [mm(residue_major(hn, DIL[g][1]), W["w_qkv"], "nn", f"{tag}_qkv{g}", out_dtype=BF16, tn=384,
                  b_cols=(8 * g, 8)) for g in range(3)]
        parts = [dil_fwd(qkv[g], aux["dil_bias"][g], g, f"{tag}_g{g}") for g in range(3)]
        o, lse = dil_merge([token_major(p_[0], DIL[g][1]) for g, p_ in enumerate(parts)],
                           [token_major(p_[1], DIL[g][1]) for g, p_ in enumerate(parts)], f"{tag}_merge")
        y = mm(o, W["w_o"], "nn", f"{tag}_o")
        return y, (qkv, o, lse)
    a = mm(hn, W["w_qkvf"], "nn", f"{tag}_qkvf", tn=640)
    fl = a[:, 3072:]
    cum = fox_gate_fwd(fl, aux["fox_b"], f"{tag}_gate")[:, :HEADS]
    cum_q = cum.reshape(S, PAIRS, 2).transpose(1, 0, 2)
    cum_k = cum.T.reshape(PAIRS, 2, S)
    ab = a.astype(BF16)
    o = attn_fwd(ab, ab, ab, f"{tag}_attn", wide=False, scale=0.125, k_off=PAIRS, v_off=2 * PAIRS, cum=cum_q, cum_t=cum_k)
    y = mm(o, W["w_o"], "nn", f"{tag}_o")
    return y, (fl, ab, cum_q, cum_k, o)


def _mixer_bwd(kind, tag, hn, dy, W, aux, saved):
    gr = {}
    if kind == 0:
        a, cq, ckv, q, k, v, o = saved
        gr["w_o"] = mm(o, dy, "tn", f"{tag}_dwo", out_dtype=BF16)
        do = mm(dy, W["w_o"], "nt", f"{tag}_do")
        dq, dk, dv = attn_bwd(q, k, v, o, do, f"{tag}_attn_b", wide=True, scale=96 ** -0.5, v_off=HEADS)
        dqp, dkr = mla_qk_bwd(dq, dk, aux["cos"], aux["sin"], f"{tag}_qk_b")
        dkvp = jnp.concatenate([dk, dv], axis=1)
        gr["w_ukv"] = mm(ckv, dkvp, "tn", f"{tag}_dwukv", out_dtype=BF16, tm=256)
        dckv = mm(dkvp, W["w_ukv"], "nt", f"{tag}_dckv", tn=256)
        gr["w_uq"] = mm(cq, dqp, "tn", f"{tag}_dwuq", out_dtype=BF16, tm=384)
        dcq = mm(dqp, W["w_uq"], "nt", f"{tag}_dcq", tn=384)
        da_q, gr["q_norm"] = rms_bwd(a[:, :Q_RANK], W["q_norm"], dcq, f"{tag}_cq_b", out_dtype=BF16)
        da_kv, gr["kv_norm"] = rms_bwd(a[:, Q_RANK:Q_RANK + KV_RANK], W["kv_norm"], dckv, f"{tag}_ckv_b",
                                       out_dtype=BF16)
        da = jnp.concatenate([da_q, da_kv, dkr], axis=1)
        gr["w_a"] = mm(hn, da, "tn", f"{tag}_dwa", out_dtype=BF16, tn=768)
        return mm(da, W["w_a"], "nt", f"{tag}_dhn", tk=768), gr
    if kind == 1:
        qkv, o, lse = saved
        gr["w_o"] = mm(o, dy, "tn", f"{tag}_dwo", out_dtype=BF16)
        do = mm(dy, W["w_o"], "nt", f"{tag}_do")
        cols, dbs = [], []
        for g, (_, d) in enumerate(DIL):
            dq, dk, dv, db = dil_bwd(qkv[g], aux["dil_bias"][g], residue_major(o, d), residue_major(lse, d),
                                     residue_major(do, d), g, f"{tag}_g{g}_b")
            cols += [token_major(t, d) for t in (dq, dk, dv)]
            dbs.append(db)
        dqkv = jnp.concatenate(cols, axis=1)
        gr["dil_dbias"] = dbs
        gr["w_qkv"] = mm(hn, dqkv, "tn", f"{tag}_dwqkv", out_dtype=BF16, out_dev=1152, tn=1152)
        return mm(dqkv, W["w_qkv"], "nt", f"{tag}_dhn", tk=1152), gr
    fl, ab, cum_q, cum_k, o = saved
    gr["w_o"] = mm(o, dy, "tn", f"{tag}_dwo", out_dtype=BF16)
    do = mm(dy, W["w_o"], "nt", f"{tag}_do")
    dq, dk, dv, dcq, dck = attn_bwd(ab, ab, ab, o, do, f"{tag}_attn_b", wide=False, scale=0.125, out_dtype=BF16,
                                    k_off=PAIRS, v_off=2 * PAIRS, cum=cum_q, cum_t=cum_k)
    pad = ((0, 0), (0, LANES - HEADS))
    dcq = jnp.pad(dcq.transpose(1, 0, 2).reshape(S, HEADS), pad)
    dck = jnp.pad(dck.reshape(HEADS, S).T, pad)
    dfl, gr["b_f"] = fox_gate_bwd(fl, aux["fox_b"], dcq, dck, f"{tag}_gate_b")
    da = jnp.concatenate([dq, dk, dv, dfl.astype(BF16)], axis=1)
    gr["w_qkvf"] = mm(hn, da, "tn", f"{tag}_dwqkvf", out_dtype=BF16, tn=640)
    return mm(da, W["w_qkvf"], "nt", f"{tag}_dhn", tk=640), gr


def kernel(x, p, positions, norm_g, ffn_w_in, ffn_w_out, ple_w_proj, ple_w_gate, rel_bias, mla_w_a, mla_q_norm, mla_kv_norm, mla_w_uq, mla_w_ukv, mla_w_o, dil_w_qkv, dil_w_o, fox_w_qkvf, fox_b_f, fox_w_o, loss_target, m_norm_g, m_ffn_w_in, m_ffn_w_out, m_ple_w_proj, m_ple_w_gate, m_rel_bias, m_mla_w_a, m_mla_q_norm, m_mla_kv_norm, m_mla_w_uq, m_mla_w_ukv, m_mla_w_o, m_dil_w_qkv, m_dil_w_o, m_fox_w_qkvf, m_fox_b_f, m_fox_w_o, v_norm_g, v_ffn_w_in, v_ffn_w_out, v_ple_w_proj, v_ple_w_gate, v_rel_bias, v_mla_w_a, v_mla_q_norm, v_mla_kv_norm, v_mla_w_uq, v_mla_w_ukv, v_mla_w_o, v_dil_w_qkv, v_dil_w_o, v_fox_w_qkvf, v_fox_b_f, v_fox_w_o):
    given = dict(locals())
    me = 4 * lax.axis_index("x") + 2 * lax.axis_index("y") + lax.axis_index("c")
    for n in TRANSPOSED:
        for pre in ("", "m_", "v_"):
            given[pre + n] = jnp.swapaxes(given[pre + n], 1, 2)

    rows_of = {n: axis == 1 for n, _, axis in BIG}
    rows_of["gains"] = False
    shape_of = {n: shp for n, shp, _ in BIG}
    lists = [_layer_list(i) for i in range(DEPTH)]
    lists[0] = [("gains", 0)] + lists[0]
    flat = [nl for ls in lists for nl in ls]
    flat_rows = [rows_of[n] for n, _ in flat]
    gain_rows = _rows(sum(int(np.prod(s)) for _, s, _ in SMALL_SHARDED))
    shards = [_pack([given[k] for k, _, _ in SMALL_SHARDED], gain_rows, F32)[None] if n == "gains" else
              given[n][l:l + 1].astype(BF16) for n, l in flat]
    lands = [lax.dynamic_update_slice(lax.empty(_gathered_shape(s, r), s.dtype), s[:, None] if r else s[None],
                                      (0, me, 0, 0) if r else (me, 0, 0, 0)) for s, r in zip(shards, flat_rows)]
    send_s, recv_s, shards, lands = gather_start(shards, lands, flat_rows, "gather_start")

    cos, sin = rope_tables(positions.reshape(S, 1), "rope_tables")
    dil_bias = [mm(rel_bias[:, HEADS * g:HEADS * (g + 1)], jnp.asarray(_bucket_onehot(DIL[g][1])), "tn",
                   f"dil_bias{g}", precise=True, tn=4096).reshape(HEADS, QBLK, 2 * QBLK) for g in range(3)]
    aux = {"cos": cos, "sin": sin, "dil_bias": dil_bias,
           "fox_b": jnp.pad(fox_b_f, ((0, 0), (0, LANES - HEADS)))}

    def arrived(i, part, behind):
        n_mix = len(lists[i]) - 4
        first = sum(len(ls) for ls in lists[:i]) + (n_mix if part else 0)
        sl = slice(first, first + (4 if part else n_mix))
        got = gather_wait(send_s, recv_s, first, shards[sl], lands[sl], flat_rows[sl], behind, f"gather_wait{i}_{part}")
        return {n: g.reshape(1, N_DEV * shape_of[n][1], shape_of[n][2]) if rows_of[n] else g
                for (n, _), g in zip(flat[sl], got)}

    full = {}

    def mixer_weights(i, behind):
        kind, j = i % 3, i // 3
        w = arrived(i, 0, behind)
        if i == 0:
            gains, off = w["gains"].reshape(N_DEV, gain_rows * LANES), 0
            for n, shp, axis in SMALL_SHARDED:
                cnt = int(np.prod(shp))
                g = jnp.moveaxis(gains[:, off:off + cnt].reshape((N_DEV,) + shp), 0, axis)
                full[n] = g.reshape(shp[:axis] + (N_DEV * shp[axis],))
                off += cnt
        W = {"g": [full["norm_g"][i, r][None, :] for r in range(4)]}
        if kind == 0:
            w_a, w_uq, w_ukv = _mla_layout(w["mla_w_a"], w["mla_w_uq"], w["mla_w_ukv"], 0)
            W.update(w_a=w_a, w_uq=w_uq, w_ukv=w_ukv, w_o=Lay(w["mla_w_o"], 0),
                     q_norm=full["mla_q_norm"][j][None, :], kv_norm=full["mla_kv_norm"][j][None, :])
        elif kind == 1:
            W.update(w_qkv=Dev(w["dil_w_qkv"], 0), w_o=Lay(w["dil_w_o"], 0))
        else:
            fox_w = jnp.pad(_cat([w["fox_w_qkvf"][dev, 0] for dev in range(N_DEV)]), ((0, 0), (0, FOX_W - 3088)))
            W.update(w_qkvf=fox_w, w_o=Lay(w["fox_w_o"], 0))
        return W

    def ffn_weights(i, behind):
        w = arrived(i, 1, behind)
        return {"w_in_t": Lay(w["ffn_w_in"], 0), "w_out": Lay(w["ffn_w_out"], 0), "w_gate": Lay(w["ple_w_gate"], 0),
                "w_proj": _cat([w["ple_w_proj"][dev, 0] for dev in range(N_DEV)])}

    h = x[0]
    saved, weights = [], []
    for i in range(DEPTH):
        kind, j, W = i % 3, i // 3, mixer_weights(i, h)
        weights.append(W)
        t = f"l{i}"
        hn = rms_fwd(h, W["g"][0], f"{t}_n0", out_dtype=BF16)
        y, mix = _mixer_fwd(kind, f"{t}_mix", hn, W, aux)
        W.update(ffn_weights(i, y))
        h1 = rms_fwd(y, W["g"][1], f"{t}_n1", res=h)
        fin = rms_fwd(h1, W["g"][2], f"{t}_n2", out_dtype=BF16)
        gu = mm(fin, W["w_in_t"], "nt", f"{t}_ffn_in")
        act = swiglu_fwd(gu, f"{t}_swiglu")
        f = mm(act, W["w_out"], "nn", f"{t}_ffn_out")
        h2 = rms_fwd(f, W["g"][3], f"{t}_n3", res=h1)
        pp = mm(p[i, 0], W["w_proj"], "nn", f"{t}_ple_p")
        gt = mm(h2, W["w_gate"], "nn", f"{t}_ple_g")
        h3 = ple_fwd(h2, pp, gt, f"{t}_ple")
        saved.append((h, hn, y, h1, fin, gu, act, f, h2, pp, gt, mix))
        h = h3

    dh, loss_lanes = loss_head(h, loss_target[0], "loss_head")

    grads = {n: None for n, _, _ in BIG}
    landed = {n: (lax.empty((N_DEV,) + shp, BF16) if shp[0] > 1 else None) for n, shp, _ in BIG}
    in_flight = []
    own = {n: [] for n, _, _ in BIG}
    g_norm = [[None] * 4 for _ in range(DEPTH)]
    g_qn, g_kvn = [None, None], [None, None]
    g_rel, g_bf = None, None

    def stacked(n):
        g = grads[n]
        return None if g is None else g.reshape(g.shape[0], N_DEV * g.shape[2], g.shape[3])

    def by_device(g):
        return g.reshape(g.shape[0], N_DEV, g.shape[1] // N_DEV, g.shape[2])

    def start(i, entries, mine, tag):
        names = [n for n, _ in entries]
        srcs = [mine[n] if n in mine else grads[n] for n in names]
        src_l = [0 if n in mine else i for n in names]
        land_l = [l if shape_of[n][0] > 1 else 0 for n, l in entries]
        rows = [rows_of[n] for n in names]
        s_sem, r_sem, srcs, got, token = scatter_start(srcs, src_l, [landed[n] for n in names], land_l, rows, tag)
        for n, src, ld in zip(names, srcs, got):
            landed[n] = ld
            if n in mine:
                mine[n] = src
            else:
                grads[n] = src
        in_flight.append((s_sem, r_sem, names, mine, src_l, land_l, rows))
        return token

    token = None
    for i in reversed(range(DEPTH)):
        kind, j, W = i % 3, i // 3, weights[i]
        t = f"l{i}b"
        h0, hn, y, h1, fin, gu, act, f, h2, pp, gt, mix = saved[i]
        dpp, dgt = ple_bwd(dh, pp, gt, f"{t}_ple")
        grads["ple_w_proj"] = mm(p[i, 0], dpp, "tn", f"{t}_dwp", out_dtype=BF16, out_dev=LANES, tm=256, tn=LANES,
                                 stack=(grads["ple_w_proj"], DEPTH, i), after=token)
        grads["ple_w_gate"] = by_device(mm(h2, dgt, "tn", f"{t}_dwg", out_dtype=BF16,
                                           stack=(stacked("ple_w_gate"), DEPTH, i)))
        dh2 = mm(dgt, W["w_gate"], "nt", f"{t}_dh2", add=dh)
        df, g_norm[i][3] = rms_bwd(f, W["g"][3], dh2, f"{t}_n3", out_dtype=BF16)
        grads["ffn_w_out"] = by_device(mm(act, df, "tn", f"{t}_dwout", out_dtype=BF16, tm=1408,
                                          stack=(stacked("ffn_w_out"), DEPTH, i)))
        dact = mm(df, W["w_out"], "nt", f"{t}_dact", tn=1408)
        dgu = swiglu_bwd(gu, dact, f"{t}_swiglu")
        grads["ffn_w_in"] = by_device(mm(dgu, fin, "tn", f"{t}_dwin", out_dtype=BF16, tm=512, tn=1024,
                                         stack=(stacked("ffn_w_in"), DEPTH, i)))
        token = start(i, lists[i][-4:], {}, f"scatter_ffn{i}")
        dfin = mm(dgu, W["w_in_t"], "nn", f"{t}_dfin", after=token, tk=1408)
        dh1, g_norm[i][2] = rms_bwd(h1, W["g"][2], dfin, f"{t}_n2", res=dh2)
        dy, g_norm[i][1] = rms_bwd(y, W["g"][1], dh1, f"{t}_n1", out_dtype=BF16)
        dhn, gr = _mixer_bwd(kind, f"{t}_mix", hn, dy, W, aux, mix)
        if kind == 0:
            d_a, d_uq, d_ukv = _mla_unlayout(gr["w_a"], gr["w_uq"], gr["w_ukv"])
            mine = {"mla_w_a": by_device(d_a[None]), "mla_w_uq": d_uq[:, None], "mla_w_ukv": d_ukv[:, None],
                    "mla_w_o": by_device(gr["w_o"][None])}
            g_qn[j], g_kvn[j] = gr["q_norm"], gr["kv_norm"]
        elif kind == 1:
            mine = {"dil_w_qkv": gr["w_qkv"], "dil_w_o": by_device(gr["w_o"][None])}
            g_rel = jnp.concatenate(
                [mm(jnp.asarray(_bucket_onehot(DIL[g][1])), gr["dil_dbias"][g].reshape(HEADS, -1), "nt",
                    f"{t}_drel{g}", precise=True, tk=4096) for g in range(3)], axis=1)
        else:
            wide = gr["w_qkvf"]
            mine = {"fox_w_qkvf": jnp.stack([wide[:, 386 * dev:386 * (dev + 1)] for dev in range(N_DEV)])[:, None],
                    "fox_w_o": by_device(gr["w_o"][None])}
            g_bf = gr["b_f"][:, :HEADS]
        token = start(i, [e for e in lists[i][:-4] if e[0] in mine], mine, f"scatter_mix{i}")
        dh, g_norm[i][0] = rms_bwd(h0, W["g"][0], dhn, f"{t}_n0", res=dh1, after=token)
    grad_x = dh[None]

    updated = {}

    def finish(groups, behind):
        for idx, (s_sem, r_sem, names, mine, src_l, land_l, rows) in groups:
            srcs = [mine[n] if n in mine else grads[n] for n in names]
            srcs, got = scatter_wait(s_sem, r_sem, srcs, src_l, [landed[n] for n in names], land_l, rows, behind,
                                     f"scatter_wait{idx}")
            for n, src, ld in zip(names, srcs, got):
                landed[n] = ld
                if n in mine:
                    mine[n] = src
                else:
                    grads[n] = src
        for _, (_, _, names, mine, src_l, land_l, rows) in groups:
            for n, sl, ll, rw in zip(names, src_l, land_l, rows):
                src = mine[n] if n in mine else grads[n]
                own[n].append((ll, lax.dynamic_index_in_dim(src, me, axis=1 if rw else 0, keepdims=False)[sl]))
        for n in dict.fromkeys(n for _, g in groups for n in g[2]):
            part = landed[n]
            for ll, blk in own[n]:
                part = lax.dynamic_update_slice(part, blk[None, None], (me, ll, 0, 0))
            updated[n] = adamw(given[n], given["m_" + n], given["v_" + n], part, f"adamw_{n}")

    finish([(k, g) for k, g in enumerate(in_flight) if g[2][0] == "ffn_w_in"], dh)
    finish([(k, g) for k, g in enumerate(in_flight) if g[2][0] != "ffn_w_in"], updated["ffn_w_in"][0])
    big_out = [updated[n] for n, _, _ in BIG]

    small_full = [jnp.stack([jnp.concatenate(r, axis=0) for r in g_norm]).reshape(-1),
                  jnp.concatenate(g_qn, axis=0).reshape(-1), jnp.concatenate(g_kvn, axis=0).reshape(-1),
                  g_rel.reshape(-1), g_bf.reshape(-1), loss_lanes.reshape(-1)]
    small_n = sum(a.shape[0] for a in small_full)
    small_rows = _rows(small_n)
    parts, = all_gather([_pack(small_full, small_rows, F32)], [False], "gather_small_grads", in_vmem=True,
                        after=[big_out[idx][0] for idx, (n, _, _) in enumerate(BIG) if n.startswith(("ffn", "ple"))])
    tot = _unpack(sum_parts(parts, "sum_small_grads"), [(4, 4, D), (2, Q_RANK), (2, KV_RANK), (32, 48), (1, 16), (LANES,)])
    loss = jnp.sum(tot[5])
    small_g = [lax.dynamic_slice_in_dim(tot[0], me * 128, 128, axis=2), lax.dynamic_slice_in_dim(tot[1], me * 48, 48, axis=1),
               lax.dynamic_slice_in_dim(tot[2], me * 32, 32, axis=1), tot[3], tot[4]]
    small_names = [n for n, _, _ in SMALL_SHARDED] + [n for n, _ in SMALL_REPL]
    small_shapes = [s for _, s, _ in SMALL_SHARDED] + [s for _, s in SMALL_REPL]
    s_rows = _rows(sum(int(np.prod(s)) for s in small_shapes))
    small_out = adamw(_pack([given[n] for n in small_names], s_rows, F32)[None],
                      _pack([given["m_" + n] for n in small_names], s_rows, F32)[None],
                      _pack([given["v_" + n] for n in small_names], s_rows, F32)[None],
                      _pack(small_g, s_rows, F32)[None, None], "adamw_small")
    small_out = [_unpack(o_, small_shapes) for o_ in small_out]

    res = [{}, {}, {}, {}]
    for k in range(4):
        for idx, (n, _, _) in enumerate(BIG):
            res[k][n] = jnp.swapaxes(big_out[idx][k], 1, 2) if n in TRANSPOSED else big_out[idx][k]
        for idx, n in enumerate(small_names):
            res[k][n] = small_out[k][idx]
    return (loss, grad_x, *[res[0][n] for n in WEIGHTS], *[res[1][n] for n in WEIGHTS],
            *[res[2][n] for n in WEIGHTS], *[res[3][n] for n in WEIGHTS])
```

```python
import math
from typing import NamedTuple

import numpy as np
import jax
import jax.numpy as jnp
from jax import lax
from jax.experimental import pallas as pl
from jax.experimental.pallas import tpu as pltpu

F32 = jnp.float32
BF16 = jnp.bfloat16
MESH_ID = pl.DeviceIdType.MESH

N_DEV = 8
S = 2048
D = 1024
DEPTH = 4
D_FF = 2816
D_PLE = 256
EPS = 1e-6
NEG = -1e30
LANES = 128
HEADS = 16
PAIRS = 8
Q_RANK = 384
KV_RANK = 256
QBLK = 128
DIL = ((128, 1), (512, 4), (2048, 16))
REL_BUCKETS = 32
FOX_W = 3200
VMEM_LIMIT = 56 * 1024 * 1024

ADAM_LR, ADAM_B1, ADAM_B2, ADAM_EPS, ADAM_WD, ADAM_STEP = 1e-3, 0.9, 0.999, 1e-8, 0.01, 10


TRANSPOSED = ("ffn_w_in",)
BIG = (
    ("ffn_w_in", (4, 704, 1024), 1), ("ffn_w_out", (4, 352, 1024), 1),
    ("ple_w_proj", (4, 256, 128), 2), ("ple_w_gate", (4, 128, 1024), 1),
    ("mla_w_a", (2, 128, 672), 1), ("mla_w_uq", (2, 384, 192), 2),
    ("mla_w_ukv", (2, 256, 256), 2), ("mla_w_o", (2, 128, 1024), 1),
    ("dil_w_qkv", (1, 1024, 1152), 2), ("dil_w_o", (1, 128, 1024), 1),
    ("fox_w_qkvf", (1, 1024, 386), 2), ("fox_w_o", (1, 128, 1024), 1),
)
SMALL_SHARDED = (("norm_g", (4, 4, 128), 2), ("mla_q_norm", (2, 48), 1), ("mla_kv_norm", (2, 32), 1))
SMALL_REPL = (("rel_bias", (32, 48)), ("fox_b_f", (1, 16)))
WEIGHTS = ("norm_g", "ffn_w_in", "ffn_w_out", "ple_w_proj", "ple_w_gate", "rel_bias", "mla_w_a", "mla_q_norm",
           "mla_kv_norm", "mla_w_uq", "mla_w_ukv", "mla_w_o", "dil_w_qkv", "dil_w_o", "fox_w_qkvf", "fox_b_f",
           "fox_w_o")


def _rows(n):
    return -(-n // (8 * LANES)) * 8


def _t5_bucket_np(dist):
    max_exact = REL_BUCKETS // 2
    n = np.maximum(dist.astype(np.float32), np.float32(1.0))
    large = max_exact + (np.log(n / np.float32(max_exact)) / np.float32(math.log(2048 / max_exact))
                         * np.float32(REL_BUCKETS - max_exact)).astype(np.int32)
    large = np.minimum(large, REL_BUCKETS - 1)
    return np.where(dist < max_exact, dist, large)


def _bucket_onehot(dilation):
    i = np.arange(QBLK)[:, None]
    j = np.arange(2 * QBLK)[None, :]
    bucket = _t5_bucket_np(np.clip(QBLK + i - j, 0, None) * dilation).reshape(-1)
    return (np.arange(REL_BUCKETS)[:, None] == bucket[None, :]).astype(np.float32)


def _rope_inv_lanes():
    half = 16
    inv = (np.float32(10000.0) ** (-np.arange(half, dtype=np.float32) / np.float32(half))).astype(np.float32)
    t = np.zeros((1, LANES), np.float32)
    t[0, 0:16] = inv
    t[0, 64:80] = inv
    return t


def _params(sem=None):
    return pltpu.CompilerParams(dimension_semantics=sem, vmem_limit_bytes=VMEM_LIMIT)


def _tile(dim, target):
    if dim <= target or dim % target == 0:
        return min(dim, target)
    t = (target // LANES) * LANES
    while dim % t:
        t -= LANES
    return t


_DIMS = {"nn": (((1,), (0,)), ((), ())), "nt": (((1,), (1,)), ((), ())), "tn": (((0,), (0,)), ((), ()))}


class Lay(NamedTuple):
    arr: jax.Array
    l: int


class Dev(NamedTuple):
    arr: jax.Array
    l: int


def _lshape(op):
    if isinstance(op, Dev):
        g, _, r, w = op.arr.shape
        return r, g * w
    return op.arr.shape[1:] if isinstance(op, Lay) else op.shape


def _op_spec(op, rows_t, cols_t, row_ix, col_ix):
    if isinstance(op, Dev):
        w = op.arr.shape[3]
        assert w % cols_t == 0 and (cols_t % LANES == 0 or cols_t == w), (w, cols_t)
        nb, l = w // cols_t, op.l
        return pl.BlockSpec((1, 1, rows_t, cols_t),
                            lambda i, j, k: (col_ix(i, j, k) // nb, l, row_ix(i, j, k), col_ix(i, j, k) % nb))
    if isinstance(op, Lay):
        l = op.l
        return pl.BlockSpec((1, rows_t, cols_t), lambda i, j, k: (l, row_ix(i, j, k), col_ix(i, j, k)))
    return pl.BlockSpec((rows_t, cols_t), lambda i, j, k: (row_ix(i, j, k), col_ix(i, j, k)))


def _mat(ref):
    return ref[(0,) * (len(ref.shape) - 2)]


def mm(a, b, mode, name, out_dtype=F32, precise=False, add=None, out_dev=None, stack=None, after=None, b_cols=None,
       tm=1024, tn=512, tk=2048):
    (ar, ac), (br, bc) = _lshape(a), _lshape(b)
    M, K = (ac, ar) if mode == "tn" else (ar, ac)
    N = br if mode == "nt" else bc
    assert K == (bc if mode == "nt" else br)
    tm, tn, tk = _tile(M, tm), _tile(N, tn), _tile(K, tk)
    nk = K // tk
    j0, n_blocks = b_cols if b_cols is not None else (0, N // tn)
    N = n_blocks * tn
    ix_i, ix_j, ix_k = (lambda i, j, k: i), (lambda i, j, k: j), (lambda i, j, k: k)
    ix_jb = lambda i, j, k: j + j0
    a_spec = _op_spec(a, tk, tm, ix_k, ix_i) if mode == "tn" else _op_spec(a, tm, tk, ix_i, ix_k)
    b_spec = _op_spec(b, tn, tk, ix_jb, ix_k) if mode == "nt" else _op_spec(b, tk, tn, ix_k, ix_jb)
    buf, n_l, l = stack if stack is not None else (None, 1, 0)
    if out_dev is not None:
        out = Dev(jax.ShapeDtypeStruct((N // out_dev, n_l, M, out_dev), out_dtype), l)
    elif stack is not None:
        out = Lay(jax.ShapeDtypeStruct((n_l, M, N), out_dtype), l)
    else:
        out = jax.ShapeDtypeStruct((M, N), out_dtype)
    o_spec = _op_spec(out, tm, tn, ix_i, ix_j)
    n_in = 3 if add is not None else 2

    def body(*refs):
        a_ref, b_ref = refs[0], refs[1]
        o_ref = refs[n_in + (buf is not None) + (after is not None)]
        if precise:
            part = lax.dot_general(_mat(a_ref), _mat(b_ref), _DIMS[mode], precision=lax.Precision.HIGHEST,
                                   preferred_element_type=F32)
        else:
            part = lax.dot_general(_mat(a_ref).astype(BF16), _mat(b_ref).astype(BF16), _DIMS[mode],
                                   preferred_element_type=F32)

        def finish(r):
            r = r + refs[2][...] if add is not None else r
            o_ref[...] = r.astype(o_ref.dtype).reshape(o_ref.shape)

        if nk == 1:
            finish(part)
            return
        acc, k = refs[-1], pl.program_id(2)

        @pl.when(k == 0)
        def _():
            acc[...] = part

        @pl.when(k > 0)
        def _():
            acc[...] += part

        @pl.when(k == nk - 1)
        def _():
            finish(acc[...])

    ins = [getattr(a, "arr", a), getattr(b, "arr", b)] + ([add] if add is not None else [])
    in_specs = [a_spec, b_spec] + ([o_spec] if add is not None else [])
    aliases = {}
    if buf is not None:
        ins.append(buf)
        in_specs.append(pl.BlockSpec(memory_space=pl.ANY))
        aliases = {n_in: 0}
    if after is not None:
        ins.append(after)
        in_specs.append(pl.BlockSpec(memory_space=pl.ANY))
    return pl.pallas_call(
        body, name=name, grid=(M // tm, N // tn, nk), in_specs=in_specs, out_specs=o_spec,
        out_shape=getattr(out, "arr", out), input_output_aliases=aliases,
        scratch_shapes=[pltpu.VMEM((tm, tn), F32)] if nk > 1 else [],
        compiler_params=_params(("parallel", "parallel", "arbitrary")),
    )(*ins)


def _rows_call(body, name, ins, outs, tr=512, acc_outs=()):
    n = ins[0].shape[0]
    tr = min(tr, n)
    in_specs = [pl.BlockSpec((tr, a.shape[1]), lambda i: (i, 0)) if a.shape[0] == n else
                pl.BlockSpec(a.shape, lambda i: (0, 0)) for a in ins]
    out_specs = [pl.BlockSpec((tr, w), lambda i: (i, 0)) for w, _ in outs] + \
                [pl.BlockSpec((1, w), lambda i: (0, 0)) for w in acc_outs]
    out_shape = [jax.ShapeDtypeStruct((n, w), dt) for w, dt in outs] + \
                [jax.ShapeDtypeStruct((1, w), F32) for w in acc_outs]
    res = pl.pallas_call(body, name=name, grid=(n // tr,), in_specs=in_specs, out_specs=out_specs,
                         out_shape=out_shape, compiler_params=_params(("arbitrary",)))(*ins)
    return res[0] if len(res) == 1 else res


def _acc(ref, val):
    @pl.when(pl.program_id(0) == 0)
    def _():
        ref[...] = jnp.zeros_like(ref)

    ref[...] += val


def rms_fwd(x, g, name, res=None, out_dtype=F32):
    def body(*refs):
        x_ref, g_ref = refs[0], refs[1]
        o_ref = refs[-1]
        xv = x_ref[...]
        y = xv * lax.rsqrt(jnp.mean(xv * xv, axis=-1, keepdims=True) + EPS) * g_ref[...]
        o_ref[...] = (y + refs[2][...] if res is not None else y).astype(o_ref.dtype)

    ins = [x, g] + ([res] if res is not None else [])
    return _rows_call(body, name, ins, [(x.shape[1], out_dtype)], tr=1024)


def rms_bwd(x, g, dy, name, res=None, out_dtype=F32, after=None):
    def body(*refs):
        x_ref, g_ref, dy_ref = refs[:3]
        dx_ref, dg_ref = refs[-2], refs[-1]
        xv, dyv = x_ref[...], dy_ref[...]
        r = lax.rsqrt(jnp.mean(xv * xv, axis=-1, keepdims=True) + EPS)
        xh = xv * r
        dxh = dyv * g_ref[...]
        dx = r * (dxh - xh * jnp.mean(dxh * xh, axis=-1, keepdims=True))
        dx_ref[...] = (dx + refs[3][...] if res is not None else dx).astype(dx_ref.dtype)
        _acc(dg_ref, jnp.sum(dyv * xh, axis=0, keepdims=True))

    ins = [x, g, dy] + ([res] if res is not None else []) + ([after] if after is not None else [])
    return _rows_call(body, name, ins, [(x.shape[1], out_dtype)], tr=1024, acc_outs=(x.shape[1],))


def _sigmoid(x):
    return 0.5 * jnp.tanh(0.5 * x) + 0.5


def swiglu_fwd(gu, name):
    def body(gu_ref, o_ref):
        gate = gu_ref[:, :D_FF]
        o_ref[...] = (gate * _sigmoid(gate) * gu_ref[:, D_FF:]).astype(BF16)

    return _rows_call(body, name, [gu], [(D_FF, BF16)], tr=512)


def swiglu_bwd(gu, dact, name):
    def body(gu_ref, d_ref, o_ref):
        gate, d = gu_ref[:, :D_FF], d_ref[...]
        sg = _sigmoid(gate)
        o_ref[:, :D_FF] = (d * gu_ref[:, D_FF:] * sg * (1.0 + gate * (1.0 - sg))).astype(BF16)
        o_ref[:, D_FF:] = (d * gate * sg).astype(BF16)

    return _rows_call(body, name, [gu, dact], [(2 * D_FF, BF16)], tr=256)


def ple_fwd(h, pp, gt, name):
    def body(h_ref, p_ref, g_ref, o_ref):
        o_ref[...] = h_ref[...] + p_ref[...] * _sigmoid(g_ref[...])

    return _rows_call(body, name, [h, pp, gt], [(D, F32)], tr=1024)


def ple_bwd(dh, pp, gt, name):
    def body(d_ref, p_ref, g_ref, dp_ref, dg_ref):
        d, sg = d_ref[...], _sigmoid(g_ref[...])
        dp_ref[...] = (d * sg).astype(BF16)
        dg_ref[...] = (d * p_ref[...] * sg * (1.0 - sg)).astype(BF16)

    return _rows_call(body, name, [dh, pp, gt], [(D, BF16), (D, BF16)], tr=1024)


def loss_head(y, target, name):
    def body(y_ref, t_ref, d_ref, l_ref):
        e = y_ref[...] - t_ref[...]
        d_ref[...] = e * (1.0 / D)
        col = jnp.sum(e * e, axis=0, keepdims=True) * (0.5 / D)
        _acc(l_ref, sum(col[:, LANES * c:LANES * (c + 1)] for c in range(D // LANES)))

    return _rows_call(body, name, [y, target], [(D, F32)], acc_outs=(LANES,))


def rope_tables(pos_col, name):
    inv = jnp.asarray(_rope_inv_lanes())

    def body(p_ref, inv_ref, c_ref, s_ref):
        ang = p_ref[...].astype(F32) * inv_ref[...]
        lane = lax.broadcasted_iota(jnp.int32, ang.shape, 1)
        first, second = lane < 16, (lane >= 64) & (lane < 80)
        c_ref[...] = jnp.where(first | second, jnp.cos(ang), 1.0)
        sn = jnp.sin(ang)
        s_ref[...] = jnp.where(first, -sn, jnp.where(second, sn, 0.0))

    return _rows_call(body, name, [pos_col, inv], [(LANES, F32), (LANES, F32)])


def _rope(x, c, s):
    return x * c + pltpu.roll(x, 64, axis=1) * s


def _rope_t(d, c, s):
    return d * c + pltpu.roll(d * s, 64, axis=1)


def mla_qk_fwd(qp, kvp, kr, cos, sin, name):
    def body(q_ref, k_ref, kr_ref, c_ref, s_ref, qo_ref, ko_ref):
        c, s = c_ref[...], s_ref[...]
        kr_rot = _rope(kr_ref[...], c, s)
        for h in range(HEADS):
            sl = slice(LANES * h, LANES * (h + 1))
            qo_ref[:, sl] = _rope(q_ref[:, sl], c, s).astype(BF16)
            ko_ref[:, sl] = (k_ref[:, sl] + kr_rot).astype(BF16)

    n = qp.shape[0]
    tr = 256
    w = HEADS * LANES
    return pl.pallas_call(
        body, name=name, grid=(n // tr,),
        in_specs=[pl.BlockSpec((tr, w), lambda i: (i, 0)), pl.BlockSpec((tr, w), lambda i: (i, 0)),
                  pl.BlockSpec((tr, LANES), lambda i: (i, 0)), pl.BlockSpec((tr, LANES), lambda i: (i, 0)),
                  pl.BlockSpec((tr, LANES), lambda i: (i, 0))],
        out_specs=[pl.BlockSpec((tr, w), lambda i: (i, 0))] * 2,
        out_shape=[jax.ShapeDtypeStruct((n, w), BF16)] * 2, compiler_params=_params(("arbitrary",)),
    )(qp, kvp, kr, cos, sin)


def mla_qk_bwd(dq, dk, cos, sin, name):
    def body(dq_ref, dk_ref, c_ref, s_ref, dqp_ref, dkr_ref):
        c, s = c_ref[...], s_ref[...]
        tot = jnp.zeros(c.shape, F32)
        for h in range(HEADS):
            sl = slice(LANES * h, LANES * (h + 1))
            dqp_ref[:, sl] = _rope_t(dq_ref[:, sl], c, s).astype(BF16)
            tot = tot + dk_ref[:, sl]
        dkr_ref[...] = _rope_t(tot, c, s).astype(BF16)

    return _rows_call(body, name, [dq, dk, cos, sin], [(HEADS * LANES, BF16), (LANES, BF16)])


TQ_FWD = 512
TQ_BWD = 256


def _pair_masks(shape):
    lane = lax.broadcasted_iota(jnp.int32, shape, 1)
    return (lane < 64, lane >= 64)


def _scaled_q(q_a, scale):
    return (q_a * jnp.asarray(scale, q_a.dtype), None) if scale == 0.125 else (q_a, scale)


def _causal_probs(q_a, k_a, scale, b0, cq, ck):
    s = lax.dot_general(q_a, k_a, _DIMS["nt"], preferred_element_type=F32)
    if scale is not None:
        s = s * scale
    if cq is not None:
        s = s + (cq - ck)
    tq = s.shape[0]
    row = lax.broadcasted_iota(jnp.int32, (tq, tq), 0)
    col = lax.broadcasted_iota(jnp.int32, (tq, tq), 1)
    diag = jnp.where(col <= row, s[:, b0:], NEG)
    s = diag if b0 == 0 else jnp.concatenate([s[:, :b0], diag], axis=1)
    e = jnp.exp(s - jnp.max(s, axis=-1, keepdims=True))
    return e * (1.0 / jnp.sum(e, axis=-1, keepdims=True))


def attn_fwd(q, k, v, name, *, wide, scale, q_off=0, k_off=0, v_off=0, cum=None, cum_t=None):
    qw = 2 * LANES if wide else LANES
    forget = cum is not None

    def body(*refs):
        q_ref, k_ref, v_ref = refs[:3]
        o_ref = refs[-1]
        m0, m1 = _pair_masks((TQ_FWD, LANES))
        for qi in range(S // TQ_FWD):
            b0, b1 = qi * TQ_FWD, (qi + 1) * TQ_FWD
            outs = []
            for a, msk in enumerate((m0, m1)):
                if wide:
                    q_a, k_a = q_ref[b0:b1, LANES * a:LANES * (a + 1)], k_ref[:b1, LANES * a:LANES * (a + 1)]
                else:
                    q_a, k_a = jnp.where(msk, q_ref[b0:b1, :], jnp.zeros((), BF16)), k_ref[:b1, :]
                cq = refs[3][0, b0:b1, a:a + 1] if forget else None
                ck = refs[4][0, a:a + 1, :b1] if forget else None
                q_a, left = _scaled_q(q_a, scale)
                p = _causal_probs(q_a, k_a, left, b0, cq, ck)
                outs.append(jnp.dot(p.astype(BF16), v_ref[:b1, :], preferred_element_type=F32))
            o_ref[b0:b1, :] = jnp.where(m0, outs[0], outs[1])

    in_specs = [pl.BlockSpec((S, qw), lambda h: (0, q_off * LANES // qw + h)),
                pl.BlockSpec((S, qw), lambda h: (0, k_off * LANES // qw + h)),
                pl.BlockSpec((S, LANES), lambda h: (0, v_off + h))]
    ins = [q, k, v]
    if forget:
        in_specs += [pl.BlockSpec((1, S, 2), lambda h: (h, 0, 0)), pl.BlockSpec((1, 2, S), lambda h: (h, 0, 0))]
        ins += [cum, cum_t]
    return pl.pallas_call(
        body, name=name, grid=(PAIRS,), in_specs=in_specs, out_specs=pl.BlockSpec((S, LANES), lambda h: (0, h)),
        out_shape=jax.ShapeDtypeStruct((S, PAIRS * LANES), F32), compiler_params=_params(("arbitrary",)),
    )(*ins)


def attn_bwd(q, k, v, o, do, name, *, wide, scale, out_dtype=F32, q_off=0, k_off=0, v_off=0, cum=None, cum_t=None):
    qw = 2 * LANES if wide else LANES
    forget = cum is not None

    def body(*refs):
        q_ref, k_ref, v_ref, o_ref, do_ref = refs[:5]
        n_out = 5 if forget else 3
        outs = refs[-(n_out + 2):-2]
        dq_ref, dk_ref, dv_ref = outs[:3]
        dk_acc, dv_acc = refs[-2], refs[-1]
        dk_acc[...] = jnp.zeros_like(dk_acc)
        dv_acc[...] = jnp.zeros_like(dv_acc)
        if forget:
            dcq_ref, dck_ref = outs[3], outs[4]
            dck_ref[...] = jnp.zeros_like(dck_ref)
        m0, m1 = _pair_masks((TQ_BWD, LANES))
        for qi in range(S // TQ_BWD):
            b0, b1 = qi * TQ_BWD, (qi + 1) * TQ_BWD
            do2 = do_ref[b0:b1, :]
            dd = do2 * o_ref[b0:b1, :]
            do_b = do2.astype(BF16)
            mk0, mk1 = _pair_masks((b1, LANES))
            dqs = []
            for a, (msk, mk) in enumerate(((m0, mk0), (m1, mk1))):
                lanes = slice(LANES * a, LANES * (a + 1)) if wide else slice(0, LANES)
                if wide:
                    q_a, k_a = q_ref[b0:b1, lanes], k_ref[:b1, lanes]
                else:
                    q_a, k_a = jnp.where(msk, q_ref[b0:b1, :], jnp.zeros((), BF16)), k_ref[:b1, :]
                cq = refs[5][0, b0:b1, a:a + 1] if forget else None
                ck = refs[6][0, a:a + 1, :b1] if forget else None
                q_a, left = _scaled_q(q_a, scale)
                p = _causal_probs(q_a, k_a, left, b0, cq, ck)
                dp = lax.dot_general(jnp.where(msk, do_b, jnp.zeros((), BF16)), v_ref[:b1, :], _DIMS["nt"],
                                     preferred_element_type=F32)
                delta = jnp.sum(jnp.where(msk, dd, 0.0), axis=-1, keepdims=True)
                ds = p * (dp - delta)
                if forget:
                    dcq_ref[0, b0:b1, a:a + 1] = jnp.sum(ds, axis=-1, keepdims=True)
                    dck_ref[0, a:a + 1, :b1] -= jnp.sum(ds, axis=0, keepdims=True)
                ds_b = ds.astype(BF16)
                dqs.append(jnp.dot(ds_b, k_a, preferred_element_type=F32) * scale)
                dk_a = lax.dot_general(ds_b, q_a, _DIMS["tn"], preferred_element_type=F32)
                dk_acc[:b1, lanes] += dk_a if left is None else dk_a * scale
                dv_acc[:b1, :] += jnp.where(mk, lax.dot_general(p.astype(BF16), do_b, _DIMS["tn"],
                                                                 preferred_element_type=F32), 0.0)
            if wide:
                dq_ref[b0:b1, :LANES] = dqs[0].astype(out_dtype)
                dq_ref[b0:b1, LANES:] = dqs[1].astype(out_dtype)
            else:
                dq_ref[b0:b1, :] = jnp.where(m0, dqs[0], dqs[1]).astype(out_dtype)
        dk_ref[...] = dk_acc[...].astype(out_dtype)
        dv_ref[...] = dv_acc[...].astype(out_dtype)

    pair = pl.BlockSpec((S, LANES), lambda h: (0, h))
    qk_out = pl.BlockSpec((S, qw), lambda h: (0, h))
    in_specs = [pl.BlockSpec((S, qw), lambda h: (0, q_off * LANES // qw + h)),
                pl.BlockSpec((S, qw), lambda h: (0, k_off * LANES // qw + h)),
                pl.BlockSpec((S, LANES), lambda h: (0, v_off + h)), pair, pair]
    ins = [q, k, v, o, do]
    out_specs = [qk_out, qk_out, pair]
    out_shape = [jax.ShapeDtypeStruct((S, PAIRS * qw), out_dtype)] * 2 + \
                [jax.ShapeDtypeStruct((S, PAIRS * LANES), out_dtype)]
    if forget:
        by_q, by_k = pl.BlockSpec((1, S, 2), lambda h: (h, 0, 0)), pl.BlockSpec((1, 2, S), lambda h: (h, 0, 0))
        in_specs += [by_q, by_k]
        ins += [cum, cum_t]
        out_specs += [by_q, by_k]
        out_shape += [jax.ShapeDtypeStruct((PAIRS, S, 2), F32), jax.ShapeDtypeStruct((PAIRS, 2, S), F32)]
    return pl.pallas_call(
        body, name=name, grid=(PAIRS,), in_specs=in_specs, out_specs=out_specs, out_shape=out_shape,
        scratch_shapes=[pltpu.VMEM((S, qw), F32), pltpu.VMEM((S, LANES), F32)],
        compiler_params=_params(("arbitrary",)),
    )(*ins)


def _tri(lower):
    r = lax.broadcasted_iota(jnp.int32, (QBLK, QBLK), 0)
    c = lax.broadcasted_iota(jnp.int32, (QBLK, QBLK), 1)
    return jnp.where((c <= r) if lower else (c >= r), 1.0, 0.0).astype(F32)


def _hi_dot(a, b):
    return jnp.dot(a, b, precision=lax.Precision.HIGHEST, preferred_element_type=F32)


def fox_gate_fwd(fl, bias, name):
    def body(f_ref, b_ref, o_ref):
        tri = _tri(True)
        carry = jnp.zeros((1, LANES), F32)
        for n in range(S // QBLK):
            x = f_ref[n * QBLK:(n + 1) * QBLK, :].astype(F32) + b_ref[...]
            lf = jnp.minimum(x, 0.0) - jnp.log(1.0 + jnp.exp(-jnp.abs(x)))
            c = _hi_dot(tri, lf) + carry
            o_ref[n * QBLK:(n + 1) * QBLK, :] = c
            carry = c[QBLK - 1:QBLK, :]

    return pl.pallas_call(body, name=name, out_shape=jax.ShapeDtypeStruct((S, LANES), F32),
                          compiler_params=_params())(fl, bias)


def fox_gate_bwd(fl, bias, dcq, dck, name):
    def body(f_ref, b_ref, dq_ref, dk_ref, o_ref, db_ref):
        tri = _tri(False)
        carry = jnp.zeros((1, LANES), F32)
        db = jnp.zeros((1, LANES), F32)
        for n in reversed(range(S // QBLK)):
            rows = slice(n * QBLK, (n + 1) * QBLK)
            dlf = _hi_dot(tri, dq_ref[rows, :] + dk_ref[rows, :]) + carry
            carry = dlf[0:1, :]
            x = f_ref[rows, :].astype(F32) + b_ref[...]
            dx = dlf * (1.0 - _sigmoid(x))
            o_ref[rows, :] = dx
            db = db + jnp.sum(dx, axis=0, keepdims=True)
        db_ref[...] = db

    return pl.pallas_call(body, name=name, out_shape=[jax.ShapeDtypeStruct((S, LANES), F32),
                                                      jax.ShapeDtypeStruct((1, LANES), F32)],
                          compiler_params=_params())(fl, bias, dcq, dck)


def _band_valid(first):
    w = QBLK if first else 2 * QBLK
    i = lax.broadcasted_iota(jnp.int32, (QBLK, w), 0)
    j = lax.broadcasted_iota(jnp.int32, (QBLK, w), 1)
    return (j <= i) if first else ((j >= i) & (j - QBLK <= i))


def _band_q(q_ref, rows, msk):
    return jnp.where(msk, q_ref[rows, :], jnp.zeros((), BF16)) * jnp.asarray(0.125, BF16)


def _band_logits(q_a, kk, bias, first):
    s = lax.dot_general(q_a, kk, _DIMS["nt"], preferred_element_type=F32) + bias
    return jnp.where(_band_valid(first), s, NEG)


def residue_major(t, d):
    return t if d == 1 else t.reshape(S // d, d, t.shape[1]).transpose(1, 0, 2).reshape(S, t.shape[1])


def token_major(t, d):
    return t if d == 1 else t.reshape(d, S // d, t.shape[1]).transpose(1, 0, 2).reshape(S, t.shape[1])


def dil_fwd(qkv, bias, g, name):
    d = DIL[g][1]
    ls = S // d

    def body(q_ref, k_ref, v_ref, b_ref, o_ref, l_ref):
        m0, m1 = _pair_masks((QBLK, LANES))
        for n in range(ls // QBLK):
            rows = slice(n * QBLK, (n + 1) * QBLK)
            keys = rows if n == 0 else slice((n - 1) * QBLK, (n + 1) * QBLK)
            os_, ls_ = [], []
            for a, msk in enumerate((m0, m1)):
                q_a = _band_q(q_ref, rows, msk)
                bias_a = b_ref[a, :, QBLK:] if n == 0 else b_ref[a]
                s = _band_logits(q_a, k_ref[keys, :], bias_a, n == 0)
                mx = jnp.max(s, axis=-1, keepdims=True)
                e = jnp.exp(s - mx)
                l = jnp.sum(e, axis=-1, keepdims=True)
                os_.append(jnp.dot((e * (1.0 / l)).astype(BF16), v_ref[keys, :], preferred_element_type=F32))
                ls_.append(mx + jnp.log(l))
            o_ref[rows, :] = jnp.where(m0, os_[0], os_[1])
            l_ref[rows, :] = jnp.where(m0, ls_[0], ls_[1])

    def col(j):
        return lambda h, r: (r, j * PAIRS + h)

    out = pl.BlockSpec((ls, LANES), lambda h, r: (r, h))
    return pl.pallas_call(
        body, name=name, grid=(PAIRS, d),
        in_specs=[pl.BlockSpec((ls, LANES), col(0)), pl.BlockSpec((ls, LANES), col(1)), pl.BlockSpec((ls, LANES), col(2)),
                  pl.BlockSpec((2, QBLK, 2 * QBLK), lambda h, r: (h, 0, 0))],
        out_specs=[out, out], out_shape=[jax.ShapeDtypeStruct((S, D), F32)] * 2,
        compiler_params=_params(("arbitrary", "arbitrary")),
    )(qkv, qkv, qkv, bias)


def dil_merge(os_, lses, name):
    def body(o0, o1, o2, l0, l1, l2, o_ref, l_ref):
        ls_ = [l0[...], l1[...], l2[...]]
        mx = jnp.maximum(jnp.maximum(ls_[0], ls_[1]), ls_[2])
        tot = mx + jnp.log(sum(jnp.exp(l - mx) for l in ls_))
        o_ref[...] = sum(jnp.exp(l - tot) * o[...] for l, o in zip(ls_, (o0, o1, o2)))
        l_ref[...] = tot

    return _rows_call(body, name, list(os_) + list(lses), [(D, F32), (D, F32)])


def dil_bwd(qkv, bias, o, lse, do, g, name):
    d = DIL[g][1]
    ls = S // d

    def body(q_ref, k_ref, v_ref, b_ref, o_ref, l_ref, do_ref, dq_ref, dk_ref, dv_ref, db_ref, dk_acc, dv_acc):
        @pl.when(pl.program_id(1) == 0)
        def _():
            db_ref[...] = jnp.zeros_like(db_ref)

        dk_acc[...] = jnp.zeros_like(dk_acc)
        dv_acc[...] = jnp.zeros_like(dv_acc)
        m0, m1 = _pair_masks((QBLK, LANES))
        for n in range(ls // QBLK):
            rows = slice(n * QBLK, (n + 1) * QBLK)
            keys = rows if n == 0 else slice((n - 1) * QBLK, (n + 1) * QBLK)
            nk = QBLK if n == 0 else 2 * QBLK
            do2, lse2 = do_ref[rows, :], l_ref[rows, :]
            dd = do2 * o_ref[rows, :]
            do_b = do2.astype(BF16)
            mk0, mk1 = _pair_masks((nk, LANES))
            dqs = []
            for a, (msk, mk) in enumerate(((m0, mk0), (m1, mk1))):
                q_a = _band_q(q_ref, rows, msk)
                kk = k_ref[keys, :]
                bias_a = b_ref[a, :, QBLK:] if n == 0 else b_ref[a]
                s = _band_logits(q_a, kk, bias_a, n == 0)
                lse_a = jnp.max(jnp.where(msk, lse2, -jnp.inf), axis=-1, keepdims=True)
                p = jnp.exp(s - lse_a)
                dp = lax.dot_general(jnp.where(msk, do_b, jnp.zeros((), BF16)), v_ref[keys, :], _DIMS["nt"],
                                     preferred_element_type=F32)
                delta = jnp.sum(jnp.where(msk, dd, 0.0), axis=-1, keepdims=True)
                ds = p * (dp - delta)
                if n == 0:
                    db_ref[a, :, QBLK:] += ds
                else:
                    db_ref[a] += ds
                ds_b = ds.astype(BF16)
                dqs.append(jnp.dot(ds_b, kk, preferred_element_type=F32) * 0.125)
                dk_acc[keys, :] += lax.dot_general(ds_b, q_a, _DIMS["tn"], preferred_element_type=F32)
                dv_acc[keys, :] += jnp.where(mk, lax.dot_general(p.astype(BF16), do_b, _DIMS["tn"],
                                                                 preferred_element_type=F32), 0.0)
            dq_ref[rows, :] = jnp.where(m0, dqs[0], dqs[1]).astype(BF16)
        dk_ref[...] = dk_acc[...].astype(BF16)
        dv_ref[...] = dv_acc[...].astype(BF16)

    def col(j):
        return lambda h, r: (r, j * PAIRS + h)

    nat = pl.BlockSpec((ls, LANES), lambda h, r: (r, h))
    b_spec = pl.BlockSpec((2, QBLK, 2 * QBLK), lambda h, r: (h, 0, 0))
    return pl.pallas_call(
        body, name=name, grid=(PAIRS, d),
        in_specs=[pl.BlockSpec((ls, LANES), col(0)), pl.BlockSpec((ls, LANES), col(1)), pl.BlockSpec((ls, LANES), col(2)),
                  b_spec, nat, nat, nat],
        out_specs=[nat, nat, nat, b_spec],
        out_shape=[jax.ShapeDtypeStruct((S, D), BF16)] * 3 + [jax.ShapeDtypeStruct((HEADS, QBLK, 2 * QBLK), F32)],
        scratch_shapes=[pltpu.VMEM((ls, LANES), F32), pltpu.VMEM((ls, LANES), F32)],
        compiler_params=_params(("arbitrary", "arbitrary")),
    )(qkv, qkv, qkv, bias, o, lse, do)


def _place():
    x, y, c = lax.axis_index("x"), lax.axis_index("y"), lax.axis_index("c")
    return x, y, c


def _dev_slot(ref, by_rows, dev):
    return ref.at[:, dev] if by_rows else ref.at[dev]


def all_gather(shards, by_rows, name, in_vmem=False, after=()):
    n, n_after = len(shards), len(after)

    def body(*refs):
        x_refs, out_refs = refs[:n], refs[n + n_after:2 * n + n_after]
        send_sems, recv_sems, local_sems = refs[2 * n + n_after:]
        x, y, c = _place()
        me, sibling = (x, y, c), (x, y, 1 - c)
        chips = [(1 - x, y), (x, 1 - y), (1 - x, 1 - y)]

        def slot(t, px, py, pc):
            return _dev_slot(out_refs[t], by_rows[t], 4 * px + 2 * py + pc)

        def copy(t, k, blk, to, src=None):
            return pltpu.make_async_remote_copy(
                src_ref=slot(t, *blk) if src is None else src, dst_ref=slot(t, *blk), send_sem=send_sems.at[7 * t + k],
                recv_sem=recv_sems.at[7 * t + k], device_id=to, device_id_type=MESH_ID)

        mine = [pltpu.make_async_copy(x_refs[t], slot(t, *me), local_sems.at[t]) for t in range(n)]
        for cp in mine:
            cp.start()
        first = []
        for t in range(n):
            first.append(copy(t, 0, me, sibling, src=x_refs[t]))
            first += [copy(t, 1 + j, me, (*chip, c), src=x_refs[t]) for j, chip in enumerate(chips)]
        for cp in first:
            cp.start()
        passed = []
        for j, chip in enumerate(chips):
            for t in range(n):
                copy(t, 1 + j, (*chip, c), me).wait_recv()
                passed.append(copy(t, 4 + j, (*chip, c), sibling))
                passed[-1].start()
        for t in range(n):
            copy(t, 0, sibling, me).wait_recv()
            for j, chip in enumerate(chips):
                copy(t, 4 + j, (*chip, 1 - c), me).wait_recv()
        for cp in first + passed:
            cp.wait_send()
        for cp in mine:
            cp.wait()

    def gathered(s, rows):
        shp = (s.shape[0], N_DEV) + s.shape[1:] if rows else (N_DEV,) + s.shape
        return jax.ShapeDtypeStruct(shp, s.dtype)

    space = pl.BlockSpec(memory_space=pltpu.VMEM if in_vmem else pl.ANY)
    return pl.pallas_call(
        body, name=name, out_shape=[gathered(s, r) for s, r in zip(shards, by_rows)],
        in_specs=[space] * n + [pl.BlockSpec(memory_space=pl.ANY)] * n_after, out_specs=[space] * n,
        scratch_shapes=[pltpu.SemaphoreType.DMA((7 * n,)), pltpu.SemaphoreType.DMA((7 * n,)),
                        pltpu.SemaphoreType.DMA((n,))],
        compiler_params=pltpu.CompilerParams(vmem_limit_bytes=VMEM_LIMIT),
    )(*shards, *after)


_HBM = pl.BlockSpec(memory_space=pltpu.HBM)
_SEM = pl.BlockSpec(memory_space=pltpu.SEMAPHORE)
_SPLIT = dict(has_side_effects=pltpu.SideEffectType.DATAFLOW_SIDE_EFFECTING)


def _hbm(a):
    return pltpu.with_memory_space_constraint(a, pltpu.HBM)


def _gathered_shape(s, rows):
    return (s.shape[0], N_DEV) + s.shape[1:] if rows else (N_DEV,) + s.shape


def _peers(x, y, c):
    return [(1 - x if k & 4 else x, 1 - y if k & 2 else y, 1 - c if k & 1 else c) for k in range(1, N_DEV)]


def gather_start(shards, lands, by_rows, name):
    n = len(shards)

    def body(*refs):
        x_refs, land_refs = refs[:n], refs[n:2 * n]
        send_sems, recv_sems = refs[2 * n], refs[2 * n + 1]
        x, y, c = _place()
        me = 4 * x + 2 * y + c
        for t in range(n):
            for k, peer in enumerate(_peers(x, y, c)):
                pltpu.make_async_remote_copy(
                    src_ref=x_refs[t], dst_ref=_dev_slot(land_refs[t], by_rows[t], me), send_sem=send_sems.at[7 * t + k],
                    recv_sem=recv_sems.at[7 * t + k], device_id=peer, device_id_type=MESH_ID).start()

    sems = pltpu.SemaphoreType.DMA((7 * n,))
    res = pl.pallas_call(
        body, name=name,
        out_shape=(sems, sems) + tuple(pltpu.HBM(a.shape, a.dtype) for a in list(shards) + list(lands)),
        in_specs=[_HBM] * (2 * n), out_specs=(_SEM, _SEM) + (_HBM,) * (2 * n),
        input_output_aliases={i: 2 + i for i in range(2 * n)},
        compiler_params=pltpu.CompilerParams(**_SPLIT),
    )(*[_hbm(a) for a in list(shards) + list(lands)])
    return res[0], res[1], list(res[2:2 + n]), list(res[2 + n:])


def gather_wait(send_sems, recv_sems, first, shards, lands, by_rows, after, name):
    n = len(shards)

    def body(*refs):
        x_refs, land_refs = refs[:n], refs[n:2 * n]
        send_sems, recv_sems = refs[2 * n], refs[2 * n + 1]
        x, y, c = _place()
        for t in range(n):
            for k, (px, py, pc) in enumerate(_peers(x, y, c)):
                cp = pltpu.make_async_remote_copy(
                    src_ref=x_refs[t], dst_ref=_dev_slot(land_refs[t], by_rows[t], 4 * px + 2 * py + pc),
                    send_sem=send_sems.at[7 * (first + t) + k], recv_sem=recv_sems.at[7 * (first + t) + k],
                    device_id=(px, py, pc), device_id_type=MESH_ID)
                cp.wait_send()
                cp.wait_recv()

    res = pl.pallas_call(
        body, name=name, out_shape=tuple(pltpu.HBM(a.shape, a.dtype) for a in list(shards) + list(lands)),
        in_specs=[_HBM] * (2 * n) + [_SEM, _SEM, pl.BlockSpec(memory_space=pl.ANY)], out_specs=(_HBM,) * (2 * n),
        input_output_aliases={i: i for i in range(2 * n)},
        compiler_params=pltpu.CompilerParams(**_SPLIT),
    )(*shards, *lands, send_sems, recv_sems, after)
    return list(res[n:])


def scatter_start(srcs, src_l, lands, land_l, by_rows, name):
    n = len(srcs)

    def body(*refs):
        x_refs, land_refs = refs[:n], refs[n:2 * n]
        send_sems, recv_sems, token = refs[2 * n], refs[2 * n + 1], refs[-1]
        x, y, c = _place()
        me = 4 * x + 2 * y + c
        for k, (px, py, pc) in enumerate(_peers(x, y, c)):
            for t in range(n):
                blk = _dev_slot(x_refs[t], by_rows[t], 4 * px + 2 * py + pc)
                pltpu.make_async_remote_copy(
                    src_ref=blk.at[src_l[t]], dst_ref=land_refs[t].at[me, land_l[t]], send_sem=send_sems.at[7 * t + k],
                    recv_sem=recv_sems.at[7 * t + k], device_id=(px, py, pc), device_id_type=MESH_ID).start()
        token[...] = jnp.zeros_like(token)

    lands = [lax.empty((N_DEV, 1) + s.shape[2:], s.dtype) if ld is None else ld for s, ld in zip(srcs, lands)]
    sems = pltpu.SemaphoreType.DMA((7 * n,))
    res = pl.pallas_call(
        body, name=name,
        out_shape=(sems, sems) + tuple(pltpu.HBM(a.shape, a.dtype) for a in list(srcs) + lands)
        + (jax.ShapeDtypeStruct((8, LANES), F32),),
        in_specs=[_HBM] * (2 * n),
        out_specs=(_SEM, _SEM) + (_HBM,) * (2 * n) + (pl.BlockSpec(memory_space=pltpu.VMEM),),
        input_output_aliases={i: 2 + i for i in range(2 * n)},
        compiler_params=pltpu.CompilerParams(**_SPLIT),
    )(*[_hbm(a) for a in list(srcs) + lands])
    return res[0], res[1], list(res[2:2 + n]), list(res[2 + n:2 + 2 * n]), res[-1]


def scatter_wait(send_sems, recv_sems, srcs, src_l, lands, land_l, by_rows, after, name):
    n = len(srcs)

    def body(*refs):
        x_refs, land_refs = refs[:n], refs[n:2 * n]
        send_sems, recv_sems = refs[2 * n], refs[2 * n + 1]
        x, y, c = _place()
        for k, (px, py, pc) in enumerate(_peers(x, y, c)):
            peer = 4 * px + 2 * py + pc
            for t in range(n):
                cp = pltpu.make_async_remote_copy(
                    src_ref=_dev_slot(x_refs[t], by_rows[t], peer).at[src_l[t]], dst_ref=land_refs[t].at[peer, land_l[t]],
                    send_sem=send_sems.at[7 * t + k], recv_sem=recv_sems.at[7 * t + k], device_id=(px, py, pc),
                    device_id_type=MESH_ID)
                cp.wait_send()
                cp.wait_recv()

    res = pl.pallas_call(
        body, name=name, out_shape=tuple(pltpu.HBM(a.shape, a.dtype) for a in list(srcs) + list(lands)),
        in_specs=[_HBM] * (2 * n) + [_SEM, _SEM, pl.BlockSpec(memory_space=pl.ANY)], out_specs=(_HBM,) * (2 * n),
        input_output_aliases={i: i for i in range(2 * n)},
        compiler_params=pltpu.CompilerParams(**_SPLIT),
    )(*srcs, *lands, send_sems, recv_sems, after)
    return list(res[:n]), list(res[n:])


ADAM_BLOCK_BYTES = 3 << 19


def adamw(w, m, v, parts, name):
    n_parts = parts.shape[0]
    n_l, r, c = w.shape
    lane_c = -(-c // LANES) * LANES
    fits = [t for t in range(16, r, 16) if r % t == 0 and t * lane_c * 4 <= ADAM_BLOCK_BYTES]
    tr = max(fits) if fits and r * lane_c * 4 > ADAM_BLOCK_BYTES else r
    c1 = 1.0 / (1.0 - ADAM_B1 ** ADAM_STEP)
    c2 = 1.0 / (1.0 - ADAM_B2 ** ADAM_STEP)

    def body(w_ref, m_ref, v_ref, p_ref, g_ref, d_ref, nm_ref, nv_ref):
        g = p_ref[0].astype(F32)
        for j in range(1, n_parts):
            g = g + p_ref[j].astype(F32)
        nm = ADAM_B1 * m_ref[...] + (1.0 - ADAM_B1) * g
        nv = ADAM_B2 * v_ref[...] + (1.0 - ADAM_B2) * (g * g)
        g_ref[...] = g
        nm_ref[...] = nm
        nv_ref[...] = nv
        d_ref[...] = -ADAM_LR * ((nm * c1) / (jnp.sqrt(nv * c2) + ADAM_EPS) + ADAM_WD * w_ref[...])

    blk = pl.BlockSpec((1, tr, c), lambda l, i: (l, i, 0))
    return pl.pallas_call(
        body, name=name, grid=(n_l, r // tr),
        in_specs=[blk, blk, blk, pl.BlockSpec((n_parts, 1, tr, c), lambda l, i: (0, l, i, 0))],
        out_specs=[blk] * 4, out_shape=[jax.ShapeDtypeStruct((n_l, r, c), F32)] * 4,
        compiler_params=_params(("parallel", "parallel")),
    )(w, m, v, parts)


def sum_parts(parts, name):
    def body(p_ref, o_ref):
        g = p_ref[0]
        for j in range(1, parts.shape[0]):
            g = g + p_ref[j]
        o_ref[...] = g

    return pl.pallas_call(body, name=name, out_shape=jax.ShapeDtypeStruct(parts.shape[1:], F32),
                          compiler_params=_params())(parts)


def _pack(arrs, rows, dtype):
    flat = jnp.concatenate([a.reshape(-1).astype(dtype) for a in arrs])
    return jnp.pad(flat, (0, rows * LANES - flat.shape[0])).reshape(rows, LANES)


def _unpack(packed, shapes):
    flat, out, off = packed.reshape(-1), [], 0
    for shp in shapes:
        n = int(np.prod(shp))
        out.append(flat[off:off + n].reshape(shp))
        off += n
    return out


def _cat(parts):
    return jnp.concatenate(parts, axis=1)


def _layer_list(i):
    kind, j = i % 3, i // 3
    mix = ([("mla_w_a", j), ("mla_w_uq", j), ("mla_w_ukv", j), ("mla_w_o", j)] if kind == 0 else
           [("dil_w_qkv", 0), ("dil_w_o", 0)] if kind == 1 else [("fox_w_qkvf", 0), ("fox_w_o", 0)])
    return mix + [("ffn_w_in", i), ("ffn_w_out", i), ("ple_w_proj", i), ("ple_w_gate", i)]


def _mla_layout(w_a, g_uq, g_ukv, j):
    def z(r, n):
        return jnp.zeros((r, n), BF16)

    wa = w_a[j]
    a = _cat([wa[:, :640], wa[:, 640:656], z(D, 48), wa[:, 656:672], z(D, 48)])
    q, k, v = [], [], []
    for h in range(HEADS):
        b = g_uq[h // 2, j][:, 96 * (h % 2):96 * (h % 2 + 1)]
        q += [b[:, 64:80], b[:, 0:32], z(Q_RANK, 16), b[:, 80:96], b[:, 32:64], z(Q_RANK, 16)]
        b = g_ukv[h // 2, j][:, LANES * (h % 2):LANES * (h % 2 + 1)]
        k += [z(KV_RANK, 16), b[:, 0:32], z(KV_RANK, 32), b[:, 32:64], z(KV_RANK, 16)]
        v.append(b[:, 64:128])
    return a, _cat(q), _cat(k + v)


def _mla_unlayout(d_a, d_uq, d_ukv):
    a = _cat([d_a[:, :640], d_a[:, 640:656], d_a[:, 704:720]])
    uq, ukv = [], []
    for dev in range(N_DEV):
        q, kv = [], []
        for h in (2 * dev, 2 * dev + 1):
            b = d_uq[:, LANES * h:LANES * (h + 1)]
            q += [b[:, 16:48], b[:, 80:112], b[:, 0:16], b[:, 64:80]]
            b = d_ukv[:, LANES * h:LANES * (h + 1)]
            kv += [b[:, 16:48], b[:, 80:112], d_ukv[:, HEADS * LANES + 64 * h:HEADS * LANES + 64 * (h + 1)]]
        uq.append(_cat(q))
        ukv.append(_cat(kv))
    return a, jnp.stack(uq), jnp.stack(ukv)


def _mixer_fwd(kind, tag, hn, W, aux):
    if kind == 0:
        a = mm(hn, W["w_a"], "nn", f"{tag}_a", tn=768)
        cq = rms_fwd(a[:, :Q_RANK], W["q_norm"], f"{tag}_cq", out_dtype=BF16)
        ckv = rms_fwd(a[:, Q_RANK:Q_RANK + KV_RANK], W["kv_norm"], f"{tag}_ckv", out_dtype=BF16)
        qp = mm(cq, W["w_uq"], "nn", f"{tag}_uq", tk=384)
        kvp = mm(ckv, W["w_ukv"], "nn", f"{tag}_ukv", tk=256)
        q, k = mla_qk_fwd(qp, kvp, a[:, 640:], aux["cos"], aux["sin"], f"{tag}_qk")
        v = kvp.astype(BF16)
        o = attn_fwd(q, k, v, f"{tag}_attn", wide=True, scale=96 ** -0.5, v_off=HEADS)
        y = mm(o, W["w_o"], "nn", f"{tag}_o")
        return y, (a, cq, ckv, q, k, v, o)
    if kind == 1:
        qkv = [mm(residue_major(hn, DIL[g][1]), W["w_qkv"], "nn", f"{tag}_qkv{g}", out_dtype=BF16, tn=384,
                  b_cols=(8 * g, 8)) for g in range(3)]
        parts = [dil_fwd(qkv[g], aux["dil_bias"][g], g, f"{tag}_g{g}") for g in range(3)]
        o, lse = dil_merge([token_major(p_[0], DIL[g][1]) for g, p_ in enumerate(parts)],
                           [token_major(p_[1], DIL[g][1]) for g, p_ in enumerate(parts)], f"{tag}_merge")
        y = mm(o, W["w_o"], "nn", f"{tag}_o")
        return y, (qkv, o, lse)
    a = mm(hn, W["w_qkvf"], "nn", f"{tag}_qkvf", tn=640)
    fl = a[:, 3072:]
    cum = fox_gate_fwd(fl, aux["fox_b"], f"{tag}_gate")[:, :HEADS]
    cum_q = cum.reshape(S, PAIRS, 2).transpose(1, 0, 2)
    cum_k = cum.T.reshape(PAIRS, 2, S)
    ab = a.astype(BF16)
    o = attn_fwd(ab, ab, ab, f"{tag}_attn", wide=False, scale=0.125, k_off=PAIRS, v_off=2 * PAIRS, cum=cum_q, cum_t=cum_k)
    y = mm(o, W["w_o"], "nn", f"{tag}_o")
    return y, (fl, ab, cum_q, cum_k, o)


def _mixer_bwd(kind, tag, hn, dy, W, aux, saved):
    gr = {}
    if kind == 0:
        a, cq, ckv, q, k, v, o = saved
        gr["w_o"] = mm(o, dy, "tn", f"{tag}_dwo", out_dtype=BF16)
        do = mm(dy, W["w_o"], "nt", f"{tag}_do")
        dq, dk, dv = attn_bwd(q, k, v, o, do, f"{tag}_attn_b", wide=True, scale=96 ** -0.5, v_off=HEADS)
        dqp, dkr = mla_qk_bwd(dq, dk, aux["cos"], aux["sin"], f"{tag}_qk_b")
        dkvp = jnp.concatenate([dk, dv], axis=1)
        gr["w_ukv"] = mm(ckv, dkvp, "tn", f"{tag}_dwukv", out_dtype=BF16, tm=256)
        dckv = mm(dkvp, W["w_ukv"], "nt", f"{tag}_dckv", tn=256)
        gr["w_uq"] = mm(cq, dqp, "tn", f"{tag}_dwuq", out_dtype=BF16, tm=384)
        dcq = mm(dqp, W["w_uq"], "nt", f"{tag}_dcq", tn=384)
        da_q, gr["q_norm"] = rms_bwd(a[:, :Q_RANK], W["q_norm"], dcq, f"{tag}_cq_b", out_dtype=BF16)
        da_kv, gr["kv_norm"] = rms_bwd(a[:, Q_RANK:Q_RANK + KV_RANK], W["kv_norm"], dckv, f"{tag}_ckv_b",
                                       out_dtype=BF16)
        da = jnp.concatenate([da_q, da_kv, dkr], axis=1)
        gr["w_a"] = mm(hn, da, "tn", f"{tag}_dwa", out_dtype=BF16, tn=768)
        return mm(da, W["w_a"], "nt", f"{tag}_dhn", tk=768), gr
    if kind == 1:
        qkv, o, lse = saved
        gr["w_o"] = mm(o, dy, "tn", f"{tag}_dwo", out_dtype=BF16)
        do = mm(dy, W["w_o"], "nt", f"{tag}_do")
        cols, dbs = [], []
        for g, (_, d) in enumerate(DIL):
            dq, dk, dv, db = dil_bwd(qkv[g], aux["dil_bias"][g], residue_major(o, d), residue_major(lse, d),
                                     residue_major(do, d), g, f"{tag}_g{g}_b")
            cols += [token_major(t, d) for t in (dq, dk, dv)]
            dbs.append(db)
        dqkv = jnp.concatenate(cols, axis=1)
        gr["dil_dbias"] = dbs
        gr["w_qkv"] = mm(hn, dqkv, "tn", f"{tag}_dwqkv", out_dtype=BF16, out_dev=1152, tn=1152)
        return mm(dqkv, W["w_qkv"], "nt", f"{tag}_dhn", tk=1152), gr
    fl, ab, cum_q, cum_k, o = saved
    gr["w_o"] = mm(o, dy, "tn", f"{tag}_dwo", out_dtype=BF16)
    do = mm(dy, W["w_o"], "nt", f"{tag}_do")
    dq, dk, dv, dcq, dck = attn_bwd(ab, ab, ab, o, do, f"{tag}_attn_b", wide=False, scale=0.125, out_dtype=BF16,
                                    k_off=PAIRS, v_off=2 * PAIRS, cum=cum_q, cum_t=cum_k)
    pad = ((0, 0), (0, LANES - HEADS))
    dcq = jnp.pad(dcq.transpose(1, 0, 2).reshape(S, HEADS), pad)
    dck = jnp.pad(dck.reshape(HEADS, S).T, pad)
    dfl, gr["b_f"] = fox_gate_bwd(fl, aux["fox_b"], dcq, dck, f"{tag}_gate_b")
    da = jnp.concatenate([dq, dk, dv, dfl.astype(BF16)], axis=1)
    gr["w_qkvf"] = mm(hn, da, "tn", f"{tag}_dwqkvf", out_dtype=BF16, tn=640)
    return mm(da, W["w_qkvf"], "nt", f"{tag}_dhn", tk=640), gr


def kernel(x, p, positions, norm_g, ffn_w_in, ffn_w_out, ple_w_proj, ple_w_gate, rel_bias, mla_w_a, mla_q_norm, mla_kv_norm, mla_w_uq, mla_w_ukv, mla_w_o, dil_w_qkv, dil_w_o, fox_w_qkvf, fox_b_f, fox_w_o, loss_target, m_norm_g, m_ffn_w_in, m_ffn_w_out, m_ple_w_proj, m_ple_w_gate, m_rel_bias, m_mla_w_a, m_mla_q_norm, m_mla_kv_norm, m_mla_w_uq, m_mla_w_ukv, m_mla_w_o, m_dil_w_qkv, m_dil_w_o, m_fox_w_qkvf, m_fox_b_f, m_fox_w_o, v_norm_g, v_ffn_w_in, v_ffn_w_out, v_ple_w_proj, v_ple_w_gate, v_rel_bias, v_mla_w_a, v_mla_q_norm, v_mla_kv_norm, v_mla_w_uq, v_mla_w_ukv, v_mla_w_o, v_dil_w_qkv, v_dil_w_o, v_fox_w_qkvf, v_fox_b_f, v_fox_w_o):
    given = dict(locals())
    me = 4 * lax.axis_index("x") + 2 * lax.axis_index("y") + lax.axis_index("c")
    for n in TRANSPOSED:
        for pre in ("", "m_", "v_"):
            given[pre + n] = jnp.swapaxes(given[pre + n], 1, 2)

    rows_of = {n: axis == 1 for n, _, axis in BIG}
    rows_of["gains"] = False
    shape_of = {n: shp for n, shp, _ in BIG}
    lists = [_layer_list(i) for i in range(DEPTH)]
    lists[0] = [("gains", 0)] + lists[0]
    flat = [nl for ls in lists for nl in ls]
    flat_rows = [rows_of[n] for n, _ in flat]
    gain_rows = _rows(sum(int(np.prod(s)) for _, s, _ in SMALL_SHARDED))
    shards = [_pack([given[k] for k, _, _ in SMALL_SHARDED], gain_rows, F32)[None] if n == "gains" else
              given[n][l:l + 1].astype(BF16) for n, l in flat]
    lands = [lax.dynamic_update_slice(lax.empty(_gathered_shape(s, r), s.dtype), s[:, None] if r else s[None],
                                      (0, me, 0, 0) if r else (me, 0, 0, 0)) for s, r in zip(shards, flat_rows)]
    send_s, recv_s, shards, lands = gather_start(shards, lands, flat_rows, "gather_start")

    cos, sin = rope_tables(positions.reshape(S, 1), "rope_tables")
    dil_bias = [mm(rel_bias[:, HEADS * g:HEADS * (g + 1)], jnp.asarray(_bucket_onehot(DIL[g][1])), "tn",
                   f"dil_bias{g}", precise=True, tn=4096).reshape(HEADS, QBLK, 2 * QBLK) for g in range(3)]
    aux = {"cos": cos, "sin": sin, "dil_bias": dil_bias,
           "fox_b": jnp.pad(fox_b_f, ((0, 0), (0, LANES - HEADS)))}

    def arrived(i, part, behind):
        n_mix = len(lists[i]) - 4
        first = sum(len(ls) for ls in lists[:i]) + (n_mix if part else 0)
        sl = slice(first, first + (4 if part else n_mix))
        got = gather_wait(send_s, recv_s, first, shards[sl], lands[sl], flat_rows[sl], behind, f"gather_wait{i}_{part}")
        return {n: g.reshape(1, N_DEV * shape_of[n][1], shape_of[n][2]) if rows_of[n] else g
                for (n, _), g in zip(flat[sl], got)}

    full = {}

    def mixer_weights(i, behind):
        kind, j = i % 3, i // 3
        w = arrived(i, 0, behind)
        if i == 0:
            gains, off = w["gains"].reshape(N_DEV, gain_rows * LANES), 0
            for n, shp, axis in SMALL_SHARDED:
                cnt = int(np.prod(shp))
                g = jnp.moveaxis(gains[:, off:off + cnt].reshape((N_DEV,) + shp), 0, axis)
                full[n] = g.reshape(shp[:axis] + (N_DEV * shp[axis],))
                off += cnt
        W = {"g": [full["norm_g"][i, r][None, :] for r in range(4)]}
        if kind == 0:
            w_a, w_uq, w_ukv = _mla_layout(w["mla_w_a"], w["mla_w_uq"], w["mla_w_ukv"], 0)
            W.update(w_a=w_a, w_uq=w_uq, w_ukv=w_ukv, w_o=Lay(w["mla_w_o"], 0),
                     q_norm=full["mla_q_norm"][j][None, :], kv_norm=full["mla_kv_norm"][j][None, :])
        elif kind == 1:
            W.update(w_qkv=Dev(w["dil_w_qkv"], 0), w_o=Lay(w["dil_w_o"], 0))
        else:
            fox_w = jnp.pad(_cat([w["fox_w_qkvf"][dev, 0] for dev in range(N_DEV)]), ((0, 0), (0, FOX_W - 3088)))
            W.update(w_qkvf=fox_w, w_o=Lay(w["fox_w_o"], 0))
        return W

    def ffn_weights(i, behind):
        w = arrived(i, 1, behind)
        return {"w_in_t": Lay(w["ffn_w_in"], 0), "w_out": Lay(w["ffn_w_out"], 0), "w_gate": Lay(w["ple_w_gate"], 0),
                "w_proj": _cat([w["ple_w_proj"][dev, 0] for dev in range(N_DEV)])}

    h = x[0]
    saved, weights = [], []
    for i in range(DEPTH):
        kind, j, W = i % 3, i // 3, mixer_weights(i, h)
        weights.append(W)
        t = f"l{i}"
        hn = rms_fwd(h, W["g"][0], f"{t}_n0", out_dtype=BF16)
        y, mix = _mixer_fwd(kind, f"{t}_mix", hn, W, aux)
        W.update(ffn_weights(i, y))
        h1 = rms_fwd(y, W["g"][1], f"{t}_n1", res=h)
        fin = rms_fwd(h1, W["g"][2], f"{t}_n2", out_dtype=BF16)
        gu = mm(fin, W["w_in_t"], "nt", f"{t}_ffn_in")
        act = swiglu_fwd(gu, f"{t}_swiglu")
        f = mm(act, W["w_out"], "nn", f"{t}_ffn_out")
        h2 = rms_fwd(f, W["g"][3], f"{t}_n3", res=h1)
        pp = mm(p[i, 0], W["w_proj"], "nn", f"{t}_ple_p")
        gt = mm(h2, W["w_gate"], "nn", f"{t}_ple_g")
        h3 = ple_fwd(h2, pp, gt, f"{t}_ple")
        saved.append((h, hn, y, h1, fin, gu, act, f, h2, pp, gt, mix))
        h = h3

    dh, loss_lanes = loss_head(h, loss_target[0], "loss_head")

    grads = {n: None for n, _, _ in BIG}
    landed = {n: (lax.empty((N_DEV,) + shp, BF16) if shp[0] > 1 else None) for n, shp, _ in BIG}
    in_flight = []
    own = {n: [] for n, _, _ in BIG}
    g_norm = [[None] * 4 for _ in range(DEPTH)]
    g_qn, g_kvn = [None, None], [None, None]
    g_rel, g_bf = None, None

    def stacked(n):
        g = grads[n]
        return None if g is None else g.reshape(g.shape[0], N_DEV * g.shape[2], g.shape[3])

    def by_device(g):
        return g.reshape(g.shape[0], N_DEV, g.shape[1] // N_DEV, g.shape[2])

    def start(i, entries, mine, tag):
        names = [n for n, _ in entries]
        srcs = [mine[n] if n in mine else grads[n] for n in names]
        src_l = [0 if n in mine else i for n in names]
        land_l = [l if shape_of[n][0] > 1 else 0 for n, l in entries]
        rows = [rows_of[n] for n in names]
        s_sem, r_sem, srcs, got, token = scatter_start(srcs, src_l, [landed[n] for n in names], land_l, rows, tag)
        for n, src, ld in zip(names, srcs, got):
            landed[n] = ld
            if n in mine:
                mine[n] = src
            else:
                grads[n] = src
        in_flight.append((s_sem, r_sem, names, mine, src_l, land_l, rows))
        return token

    token = None
    for i in reversed(range(DEPTH)):
        kind, j, W = i % 3, i // 3, weights[i]
        t = f"l{i}b"
        h0, hn, y, h1, fin, gu, act, f, h2, pp, gt, mix = saved[i]
        dpp, dgt = ple_bwd(dh, pp, gt, f"{t}_ple")
        grads["ple_w_proj"] = mm(p[i, 0], dpp, "tn", f"{t}_dwp", out_dtype=BF16, out_dev=LANES, tm=256, tn=LANES,
                                 stack=(grads["ple_w_proj"], DEPTH, i), after=token)
        grads["ple_w_gate"] = by_device(mm(h2, dgt, "tn", f"{t}_dwg", out_dtype=BF16,
                                           stack=(stacked("ple_w_gate"), DEPTH, i)))
        dh2 = mm(dgt, W["w_gate"], "nt", f"{t}_dh2", add=dh)
        df, g_norm[i][3] = rms_bwd(f, W["g"][3], dh2, f"{t}_n3", out_dtype=BF16)
        grads["ffn_w_out"] = by_device(mm(act, df, "tn", f"{t}_dwout", out_dtype=BF16, tm=1408,
                                          stack=(stacked("ffn_w_out"), DEPTH, i)))
        dact = mm(df, W["w_out"], "nt", f"{t}_dact", tn=1408)
        dgu = swiglu_bwd(gu, dact, f"{t}_swiglu")
        grads["ffn_w_in"] = by_device(mm(dgu, fin, "tn", f"{t}_dwin", out_dtype=BF16, tm=512, tn=1024,
                                         stack=(stacked("ffn_w_in"), DEPTH, i)))
        token = start(i, lists[i][-4:], {}, f"scatter_ffn{i}")
        dfin = mm(dgu, W["w_in_t"], "nn", f"{t}_dfin", after=token, tk=1408)
        dh1, g_norm[i][2] = rms_bwd(h1, W["g"][2], dfin, f"{t}_n2", res=dh2)
        dy, g_norm[i][1] = rms_bwd(y, W["g"][1], dh1, f"{t}_n1", out_dtype=BF16)
        dhn, gr = _mixer_bwd(kind, f"{t}_mix", hn, dy, W, aux, mix)
        if kind == 0:
            d_a, d_uq, d_ukv = _mla_unlayout(gr["w_a"], gr["w_uq"], gr["w_ukv"])
            mine = {"mla_w_a": by_device(d_a[None]), "mla_w_uq": d_uq[:, None], "mla_w_ukv": d_ukv[:, None],
                    "mla_w_o": by_device(gr["w_o"][None])}
            g_qn[j], g_kvn[j] = gr["q_norm"], gr["kv_norm"]
        elif kind == 1:
            mine = {"dil_w_qkv": gr["w_qkv"], "dil_w_o": by_device(gr["w_o"][None])}
            g_rel = jnp.concatenate(
                [mm(jnp.asarray(_bucket_onehot(DIL[g][1])), gr["dil_dbias"][g].reshape(HEADS, -1), "nt",
                    f"{t}_drel{g}", precise=True, tk=4096) for g in range(3)], axis=1)
        else:
            wide = gr["w_qkvf"]
            mine = {"fox_w_qkvf": jnp.stack([wide[:, 386 * dev:386 * (dev + 1)] for dev in range(N_DEV)])[:, None],
                    "fox_w_o": by_device(gr["w_o"][None])}
            g_bf = gr["b_f"][:, :HEADS]
        token = start(i, [e for e in lists[i][:-4] if e[0] in mine], mine, f"scatter_mix{i}")
        dh, g_norm[i][0] = rms_bwd(h0, W["g"][0], dhn, f"{t}_n0", res=dh1, after=token)
    grad_x = dh[None]

    updated = {}

    def finish(groups, behind):
        for idx, (s_sem, r_sem, names, mine, src_l, land_l, rows) in groups:
            srcs = [mine[n] if n in mine else grads[n] for n in names]
            srcs, got = scatter_wait(s_sem, r_sem, srcs, src_l, [landed[n] for n in names], land_l, rows, behind,
                                     f"scatter_wait{idx}")
            for n, src, ld in zip(names, srcs, got):
                landed[n] = ld
                if n in mine:
                    mine[n] = src
                else:
                    grads[n] = src
        for _, (_, _, names, mine, src_l, land_l, rows) in groups:
            for n, sl, ll, rw in zip(names, src_l, land_l, rows):
                src = mine[n] if n in mine else grads[n]
                own[n].append((ll, lax.dynamic_index_in_dim(src, me, axis=1 if rw else 0, keepdims=False)[sl]))
        for n in dict.fromkeys(n for _, g in groups for n in g[2]):
            part = landed[n]
            for ll, blk in own[n]:
                part = lax.dynamic_update_slice(part, blk[None, None], (me, ll, 0, 0))
            updated[n] = adamw(given[n], given["m_" + n], given["v_" + n], part, f"adamw_{n}")

    finish([(k, g) for k, g in enumerate(in_flight) if g[2][0] == "ffn_w_in"], dh)
    finish([(k, g) for k, g in enumerate(in_flight) if g[2][0] != "ffn_w_in"], updated["ffn_w_in"][0])
    big_out = [updated[n] for n, _, _ in BIG]

    small_full = [jnp.stack([jnp.concatenate(r, axis=0) for r in g_norm]).reshape(-1),
                  jnp.concatenate(g_qn, axis=0).reshape(-1), jnp.concatenate(g_kvn, axis=0).reshape(-1),
                  g_rel.reshape(-1), g_bf.reshape(-1), loss_lanes.reshape(-1)]
    small_n = sum(a.shape[0] for a in small_full)
    small_rows = _rows(small_n)
    parts, = all_gather([_pack(small_full, small_rows, F32)], [False], "gather_small_grads", in_vmem=True,
                        after=[big_out[idx][0] for idx, (n, _, _) in enumerate(BIG) if n.startswith(("ffn", "ple"))])
    tot = _unpack(sum_parts(parts, "sum_small_grads"), [(4, 4, D), (2, Q_RANK), (2, KV_RANK), (32, 48), (1, 16), (LANES,)])
    loss = jnp.sum(tot[5])
    small_g = [lax.dynamic_slice_in_dim(tot[0], me * 128, 128, axis=2), lax.dynamic_slice_in_dim(tot[1], me * 48, 48, axis=1),
               lax.dynamic_slice_in_dim(tot[2], me * 32, 32, axis=1), tot[3], tot[4]]
    small_names = [n for n, _, _ in SMALL_SHARDED] + [n for n, _ in SMALL_REPL]
    small_shapes = [s for _, s, _ in SMALL_SHARDED] + [s for _, s in SMALL_REPL]
    s_rows = _rows(sum(int(np.prod(s)) for s in small_shapes))
    small_out = adamw(_pack([given[n] for n in small_names], s_rows, F32)[None],
                      _pack([given["m_" + n] for n in small_names], s_rows, F32)[None],
                      _pack([given["v_" + n] for n in small_names], s_rows, F32)[None],
                      _pack(small_g, s_rows, F32)[None, None], "adamw_small")
    small_out = [_unpack(o_, small_shapes) for o_ in small_out]

    res = [{}, {}, {}, {}]
    for k in range(4):
        for idx, (n, _, _) in enumerate(BIG):
            res[k][n] = jnp.swapaxes(big_out[idx][k], 1, 2) if n in TRANSPOSED else big_out[idx][k]
        for idx, n in enumerate(small_names):
            res[k][n] = small_out[k][idx]
    return (loss, grad_x, *[res[0][n] for n in WEIGHTS], *[res[1][n] for n in WEIGHTS],
            *[res[2][n] for n in WEIGHTS], *[res[3][n] for n in WEIGHTS])
```

```python
import math
from typing import NamedTuple

import numpy as np
import jax
import jax.numpy as jnp
from jax import lax
from jax.experimental import pallas as pl
from jax.experimental.pallas import tpu as pltpu

F32 = jnp.float32
BF16 = jnp.bfloat16
MESH_ID = pl.DeviceIdType.MESH

N_DEV = 8
S = 2048
D = 1024
DEPTH = 4
D_FF = 2816
D_PLE = 256
EPS = 1e-6
NEG = -1e30
LANES = 128
HEADS = 16
PAIRS = 8
Q_RANK = 384
KV_RANK = 256
QBLK = 128
DIL = ((128, 1), (512, 4), (2048, 16))
REL_BUCKETS = 32
FOX_W = 3200
VMEM_LIMIT = 56 * 1024 * 1024

ADAM_LR, ADAM_B1, ADAM_B2, ADAM_EPS, ADAM_WD, ADAM_STEP = 1e-3, 0.9, 0.999, 1e-8, 0.01, 10


TRANSPOSED = ("ffn_w_in",)
BIG = (
    ("ffn_w_in", (4, 704, 1024), 1), ("ffn_w_out", (4, 352, 1024), 1),
    ("ple_w_proj", (4, 256, 128), 2), ("ple_w_gate", (4, 128, 1024), 1),
    ("mla_w_a", (2, 128, 672), 1), ("mla_w_uq", (2, 384, 192), 2),
    ("mla_w_ukv", (2, 256, 256), 2), ("mla_w_o", (2, 128, 1024), 1),
    ("dil_w_qkv", (1, 1024, 1152), 2), ("dil_w_o", (1, 128, 1024), 1),
    ("fox_w_qkvf", (1, 1024, 386), 2), ("fox_w_o", (1, 128, 1024), 1),
)
SMALL_SHARDED = (("norm_g", (4, 4, 128), 2), ("mla_q_norm", (2, 48), 1), ("mla_kv_norm", (2, 32), 1))
SMALL_REPL = (("rel_bias", (32, 48)), ("fox_b_f", (1, 16)))
WEIGHTS = ("norm_g", "ffn_w_in", "ffn_w_out", "ple_w_proj", "ple_w_gate", "rel_bias", "mla_w_a", "mla_q_norm",
           "mla_kv_norm", "mla_w_uq", "mla_w_ukv", "mla_w_o", "dil_w_qkv", "dil_w_o", "fox_w_qkvf", "fox_b_f",
           "fox_w_o")


def _rows(n):
    return -(-n // (8 * LANES)) * 8


def _t5_bucket_np(dist):
    max_exact = REL_BUCKETS // 2
    n = np.maximum(dist.astype(np.float32), np.float32(1.0))
    large = max_exact + (np.log(n / np.float32(max_exact)) / np.float32(math.log(2048 / max_exact))
                         * np.float32(REL_BUCKETS - max_exact)).astype(np.int32)
    large = np.minimum(large, REL_BUCKETS - 1)
    return np.where(dist < max_exact, dist, large)


def _bucket_onehot(dilation):
    i = np.arange(QBLK)[:, None]
    j = np.arange(2 * QBLK)[None, :]
    bucket = _t5_bucket_np(np.clip(QBLK + i - j, 0, None) * dilation).reshape(-1)
    return (np.arange(REL_BUCKETS)[:, None] == bucket[None, :]).astype(np.float32)


def _rope_inv_lanes():
    half = 16
    inv = (np.float32(10000.0) ** (-np.arange(half, dtype=np.float32) / np.float32(half))).astype(np.float32)
    t = np.zeros((1, LANES), np.float32)
    t[0, 0:16] = inv
    t[0, 64:80] = inv
    return t


def _params(sem=None):
    return pltpu.CompilerParams(dimension_semantics=sem, vmem_limit_bytes=VMEM_LIMIT)


def _tile(dim, target):
    if dim <= target or dim % target == 0:
        return min(dim, target)
    t = (target // LANES) * LANES
    while dim % t:
        t -= LANES
    return t


_DIMS = {"nn": (((1,), (0,)), ((), ())), "nt": (((1,), (1,)), ((), ())), "tn": (((0,), (0,)), ((), ()))}


class Lay(NamedTuple):
    arr: jax.Array
    l: int


class Dev(NamedTuple):
    arr: jax.Array
    l: int


def _lshape(op):
    if isinstance(op, Dev):
        g, _, r, w = op.arr.shape
        return r, g * w
    return op.arr.shape[1:] if isinstance(op, Lay) else op.shape


def _op_spec(op, rows_t, cols_t, row_ix, col_ix):
    if isinstance(op, Dev):
        w = op.arr.shape[3]
        assert w % cols_t == 0 and (cols_t % LANES == 0 or cols_t == w), (w, cols_t)
        nb, l = w // cols_t, op.l
        return pl.BlockSpec((1, 1, rows_t, cols_t),
                            lambda i, j, k: (col_ix(i, j, k) // nb, l, row_ix(i, j, k), col_ix(i, j, k) % nb))
    if isinstance(op, Lay):
        l = op.l
        return pl.BlockSpec((1, rows_t, cols_t), lambda i, j, k: (l, row_ix(i, j, k), col_ix(i, j, k)))
    return pl.BlockSpec((rows_t, cols_t), lambda i, j, k: (row_ix(i, j, k), col_ix(i, j, k)))


def _mat(ref):
    return ref[(0,) * (len(ref.shape) - 2)]


def mm(a, b, mode, name, out_dtype=F32, precise=False, add=None, out_dev=None, stack=None, after=None, b_cols=None,
       tm=1024, tn=512, tk=2048):
    (ar, ac), (br, bc) = _lshape(a), _lshape(b)
    M, K = (ac, ar) if mode == "tn" else (ar, ac)
    N = br if mode == "nt" else bc
    assert K == (bc if mode == "nt" else br)
    tm, tn, tk = _tile(M, tm), _tile(N, tn), _tile(K, tk)
    nk = K // tk
    j0, n_blocks = b_cols if b_cols is not None else (0, N // tn)
    N = n_blocks * tn
    ix_i, ix_j, ix_k = (lambda i, j, k: i), (lambda i, j, k: j), (lambda i, j, k: k)
    ix_jb = lambda i, j, k: j + j0
    a_spec = _op_spec(a, tk, tm, ix_k, ix_i) if mode == "tn" else _op_spec(a, tm, tk, ix_i, ix_k)
    b_spec = _op_spec(b, tn, tk, ix_jb, ix_k) if mode == "nt" else _op_spec(b, tk, tn, ix_k, ix_jb)
    buf, n_l, l = stack if stack is not None else (None, 1, 0)
    if out_dev is not None:
        out = Dev(jax.ShapeDtypeStruct((N // out_dev, n_l, M, out_dev), out_dtype), l)
    elif stack is not None:
        out = Lay(jax.ShapeDtypeStruct((n_l, M, N), out_dtype), l)
    else:
        out = jax.ShapeDtypeStruct((M, N), out_dtype)
    o_spec = _op_spec(out, tm, tn, ix_i, ix_j)
    n_in = 3 if add is not None else 2

    def body(*refs):
        a_ref, b_ref = refs[0], refs[1]
        o_ref = refs[n_in + (buf is not None) + (after is not None)]
        if precise:
            part = lax.dot_general(_mat(a_ref), _mat(b_ref), _DIMS[mode], precision=lax.Precision.HIGHEST,
                                   preferred_element_type=F32)
        else:
            part = lax.dot_general(_mat(a_ref).astype(BF16), _mat(b_ref).astype(BF16), _DIMS[mode],
                                   preferred_element_type=F32)

        def finish(r):
            r = r + refs[2][...] if add is not None else r
            o_ref[...] = r.astype(o_ref.dtype).reshape(o_ref.shape)

        if nk == 1:
            finish(part)
            return
        acc, k = refs[-1], pl.program_id(2)

        @pl.when(k == 0)
        def _():
            acc[...] = part

        @pl.when(k > 0)
        def _():
            acc[...] += part

        @pl.when(k == nk - 1)
        def _():
            finish(acc[...])

    ins = [getattr(a, "arr", a), getattr(b, "arr", b)] + ([add] if add is not None else [])
    in_specs = [a_spec, b_spec] + ([o_spec] if add is not None else [])
    aliases = {}
    if buf is not None:
        ins.append(buf)
        in_specs.append(pl.BlockSpec(memory_space=pl.ANY))
        aliases = {n_in: 0}
    if after is not None:
        ins.append(after)
        in_specs.append(pl.BlockSpec(memory_space=pl.ANY))
    return pl.pallas_call(
        body, name=name, grid=(M // tm, N // tn, nk), in_specs=in_specs, out_specs=o_spec,
        out_shape=getattr(out, "arr", out), input_output_aliases=aliases,
        scratch_shapes=[pltpu.VMEM((tm, tn), F32)] if nk > 1 else [],
        compiler_params=_params(("parallel", "parallel", "arbitrary")),
    )(*ins)


def _rows_call(body, name, ins, outs, tr=512, acc_outs=()):
    n = ins[0].shape[0]
    tr = min(tr, n)
    in_specs = [pl.BlockSpec((tr, a.shape[1]), lambda i: (i, 0)) if a.shape[0] == n else
                pl.BlockSpec(a.shape, lambda i: (0, 0)) for a in ins]
    out_specs = [pl.BlockSpec((tr, w), lambda i: (i, 0)) for w, _ in outs] + \
                [pl.BlockSpec((1, w), lambda i: (0, 0)) for w in acc_outs]
    out_shape = [jax.ShapeDtypeStruct((n, w), dt) for w, dt in outs] + \
                [jax.ShapeDtypeStruct((1, w), F32) for w in acc_outs]
    res = pl.pallas_call(body, name=name, grid=(n // tr,), in_specs=in_specs, out_specs=out_specs,
                         out_shape=out_shape, compiler_params=_params(("arbitrary",)))(*ins)
    return res[0] if len(res) == 1 else res


def _acc(ref, val):
    @pl.when(pl.program_id(0) == 0)
    def _():
        ref[...] = jnp.zeros_like(ref)

    ref[...] += val


def rms_fwd(x, g, name, res=None, out_dtype=F32):
    def body(*refs):
        x_ref, g_ref = refs[0], refs[1]
        o_ref = refs[-1]
        xv = x_ref[...]
        y = xv * lax.rsqrt(jnp.mean(xv * xv, axis=-1, keepdims=True) + EPS) * g_ref[...]
        o_ref[...] = (y + refs[2][...] if res is not None else y).astype(o_ref.dtype)

    ins = [x, g] + ([res] if res is not None else [])
    return _rows_call(body, name, ins, [(x.shape[1], out_dtype)], tr=1024)


def rms_bwd(x, g, dy, name, res=None, out_dtype=F32, after=None):
    def body(*refs):
        x_ref, g_ref, dy_ref = refs[:3]
        dx_ref, dg_ref = refs[-2], refs[-1]
        xv, dyv = x_ref[...], dy_ref[...]
        r = lax.rsqrt(jnp.mean(xv * xv, axis=-1, keepdims=True) + EPS)
        xh = xv * r
        dxh = dyv * g_ref[...]
        dx = r * (dxh - xh * jnp.mean(dxh * xh, axis=-1, keepdims=True))
        dx_ref[...] = (dx + refs[3][...] if res is not None else dx).astype(dx_ref.dtype)
        _acc(dg_ref, jnp.sum(dyv * xh, axis=0, keepdims=True))

    ins = [x, g, dy] + ([res] if res is not None else []) + ([after] if after is not None else [])
    return _rows_call(body, name, ins, [(x.shape[1], out_dtype)], tr=1024, acc_outs=(x.shape[1],))


def _sigmoid(x):
    return 0.5 * jnp.tanh(0.5 * x) + 0.5


def swiglu_fwd(gu, name):
    def body(gu_ref, o_ref):
        gate = gu_ref[:, :D_FF]
        o_ref[...] = (gate * _sigmoid(gate) * gu_ref[:, D_FF:]).astype(BF16)

    return _rows_call(body, name, [gu], [(D_FF, BF16)], tr=512)


def swiglu_bwd(gu, dact, name):
    def body(gu_ref, d_ref, o_ref):
        gate, d = gu_ref[:, :D_FF], d_ref[...]
        sg = _sigmoid(gate)
        o_ref[:, :D_FF] = (d * gu_ref[:, D_FF:] * sg * (1.0 + gate * (1.0 - sg))).astype(BF16)
        o_ref[:, D_FF:] = (d * gate * sg).astype(BF16)

    return _rows_call(body, name, [gu, dact], [(2 * D_FF, BF16)], tr=256)


def ple_fwd(h, pp, gt, name):
    def body(h_ref, p_ref, g_ref, o_ref):
        o_ref[...] = h_ref[...] + p_ref[...] * _sigmoid(g_ref[...])

    return _rows_call(body, name, [h, pp, gt], [(D, F32)], tr=1024)


def ple_bwd(dh, pp, gt, name):
    def body(d_ref, p_ref, g_ref, dp_ref, dg_ref):
        d, sg = d_ref[...], _sigmoid(g_ref[...])
        dp_ref[...] = (d * sg).astype(BF16)
        dg_ref[...] = (d * p_ref[...] * sg * (1.0 - sg)).astype(BF16)

    return _rows_call(body, name, [dh, pp, gt], [(D, BF16), (D, BF16)], tr=1024)


def loss_head(y, target, name):
    def body(y_ref, t_ref, d_ref, l_ref):
        e = y_ref[...] - t_ref[...]
        d_ref[...] = e * (1.0 / D)
        col = jnp.sum(e * e, axis=0, keepdims=True) * (0.5 / D)
        _acc(l_ref, sum(col[:, LANES * c:LANES * (c + 1)] for c in range(D // LANES)))

    return _rows_call(body, name, [y, target], [(D, F32)], acc_outs=(LANES,))


def rope_tables(pos_col, name):
    inv = jnp.asarray(_rope_inv_lanes())

    def body(p_ref, inv_ref, c_ref, s_ref):
        ang = p_ref[...].astype(F32) * inv_ref[...]
        lane = lax.broadcasted_iota(jnp.int32, ang.shape, 1)
        first, second = lane < 16, (lane >= 64) & (lane < 80)
        c_ref[...] = jnp.where(first | second, jnp.cos(ang), 1.0)
        sn = jnp.sin(ang)
        s_ref[...] = jnp.where(first, -sn, jnp.where(second, sn, 0.0))

    return _rows_call(body, name, [pos_col, inv], [(LANES, F32), (LANES, F32)])


def _rope(x, c, s):
    return x * c + pltpu.roll(x, 64, axis=1) * s


def _rope_t(d, c, s):
    return d * c + pltpu.roll(d * s, 64, axis=1)


def mla_qk_fwd(qp, kvp, kr, cos, sin, name):
    def body(q_ref, k_ref, kr_ref, c_ref, s_ref, qo_ref, ko_ref):
        c, s = c_ref[...], s_ref[...]
        kr_rot = _rope(kr_ref[...], c, s)
        for h in range(HEADS):
            sl = slice(LANES * h, LANES * (h + 1))
            qo_ref[:, sl] = _rope(q_ref[:, sl], c, s).astype(BF16)
            ko_ref[:, sl] = (k_ref[:, sl] + kr_rot).astype(BF16)

    n = qp.shape[0]
    tr = 256
    w = HEADS * LANES
    return pl.pallas_call(
        body, name=name, grid=(n // tr,),
        in_specs=[pl.BlockSpec((tr, w), lambda i: (i, 0)), pl.BlockSpec((tr, w), lambda i: (i, 0)),
                  pl.BlockSpec((tr, LANES), lambda i: (i, 0)), pl.BlockSpec((tr, LANES), lambda i: (i, 0)),
                  pl.BlockSpec((tr, LANES), lambda i: (i, 0))],
        out_specs=[pl.BlockSpec((tr, w), lambda i: (i, 0))] * 2,
        out_shape=[jax.ShapeDtypeStruct((n, w), BF16)] * 2, compiler_params=_params(("arbitrary",)),
    )(qp, kvp, kr, cos, sin)


def mla_qk_bwd(dq, dk, cos, sin, name):
    def body(dq_ref, dk_ref, c_ref, s_ref, dqp_ref, dkr_ref):
        c, s = c_ref[...], s_ref[...]
        tot = jnp.zeros(c.shape, F32)
        for h in range(HEADS):
            sl = slice(LANES * h, LANES * (h + 1))
            dqp_ref[:, sl] = _rope_t(dq_ref[:, sl], c, s).astype(BF16)
            tot = tot + dk_ref[:, sl]
        dkr_ref[...] = _rope_t(tot, c, s).astype(BF16)

    return _rows_call(body, name, [dq, dk, cos, sin], [(HEADS * LANES, BF16), (LANES, BF16)])


TQ_FWD = 1024
TQ_BWD = 256


def _pair_masks(shape):
    lane = lax.broadcasted_iota(jnp.int32, shape, 1)
    return (lane < 64, lane >= 64)


def _scaled_q(q_a, scale):
    return (q_a * jnp.asarray(scale, q_a.dtype), None) if scale == 0.125 else (q_a, scale)


def _causal_probs(q_a, k_a, scale, b0, cq, ck):
    s = lax.dot_general(q_a, k_a, _DIMS["nt"], preferred_element_type=F32)
    if scale is not None:
        s = s * scale
    if cq is not None:
        s = s + (cq - ck)
    tq = s.shape[0]
    row = lax.broadcasted_iota(jnp.int32, (tq, tq), 0)
    col = lax.broadcasted_iota(jnp.int32, (tq, tq), 1)
    diag = jnp.where(col <= row, s[:, b0:], NEG)
    s = diag if b0 == 0 else jnp.concatenate([s[:, :b0], diag], axis=1)
    e = jnp.exp(s - jnp.max(s, axis=-1, keepdims=True))
    return e * (1.0 / jnp.sum(e, axis=-1, keepdims=True))


def attn_fwd(q, k, v, name, *, wide, scale, q_off=0, k_off=0, v_off=0, cum=None, cum_t=None):
    qw = 2 * LANES if wide else LANES
    forget = cum is not None

    def body(*refs):
        q_ref, k_ref, v_ref = refs[:3]
        o_ref = refs[-1]
        m0, m1 = _pair_masks((TQ_FWD, LANES))
        for qi in range(S // TQ_FWD):
            b0, b1 = qi * TQ_FWD, (qi + 1) * TQ_FWD
            outs = []
            for a, msk in enumerate((m0, m1)):
                if wide:
                    q_a, k_a = q_ref[b0:b1, LANES * a:LANES * (a + 1)], k_ref[:b1, LANES * a:LANES * (a + 1)]
                else:
                    q_a, k_a = jnp.where(msk, q_ref[b0:b1, :], jnp.zeros((), BF16)), k_ref[:b1, :]
                cq = refs[3][0, b0:b1, a:a + 1] if forget else None
                ck = refs[4][0, a:a + 1, :b1] if forget else None
                q_a, left = _scaled_q(q_a, scale)
                p = _causal_probs(q_a, k_a, left, b0, cq, ck)
                outs.append(jnp.dot(p.astype(BF16), v_ref[:b1, :], preferred_element_type=F32))
            o_ref[b0:b1, :] = jnp.where(m0, outs[0], outs[1])

    in_specs = [pl.BlockSpec((S, qw), lambda h: (0, q_off * LANES // qw + h)),
                pl.BlockSpec((S, qw), lambda h: (0, k_off * LANES // qw + h)),
                pl.BlockSpec((S, LANES), lambda h: (0, v_off + h))]
    ins = [q, k, v]
    if forget:
        in_specs += [pl.BlockSpec((1, S, 2), lambda h: (h, 0, 0)), pl.BlockSpec((1, 2, S), lambda h: (h, 0, 0))]
        ins += [cum, cum_t]
    return pl.pallas_call(
        body, name=name, grid=(PAIRS,), in_specs=in_specs, out_specs=pl.BlockSpec((S, LANES), lambda h: (0, h)),
        out_shape=jax.ShapeDtypeStruct((S, PAIRS * LANES), F32), compiler_params=_params(("arbitrary",)),
    )(*ins)


def attn_bwd(q, k, v, o, do, name, *, wide, scale, out_dtype=F32, q_off=0, k_off=0, v_off=0, cum=None, cum_t=None):
    qw = 2 * LANES if wide else LANES
    forget = cum is not None

    def body(*refs):
        q_ref, k_ref, v_ref, o_ref, do_ref = refs[:5]
        n_out = 5 if forget else 3
        outs = refs[-(n_out + 2):-2]
        dq_ref, dk_ref, dv_ref = outs[:3]
        dk_acc, dv_acc = refs[-2], refs[-1]
        dk_acc[...] = jnp.zeros_like(dk_acc)
        dv_acc[...] = jnp.zeros_like(dv_acc)
        if forget:
            dcq_ref, dck_ref = outs[3], outs[4]
            dck_ref[...] = jnp.zeros_like(dck_ref)
        m0, m1 = _pair_masks((TQ_BWD, LANES))
        for qi in range(S // TQ_BWD):
            b0, b1 = qi * TQ_BWD, (qi + 1) * TQ_BWD
            do2 = do_ref[b0:b1, :]
            dd = do2 * o_ref[b0:b1, :]
            do_b = do2.astype(BF16)
            mk0, mk1 = _pair_masks((b1, LANES))
            dqs = []
            for a, (msk, mk) in enumerate(((m0, mk0), (m1, mk1))):
                lanes = slice(LANES * a, LANES * (a + 1)) if wide else slice(0, LANES)
                if wide:
                    q_a, k_a = q_ref[b0:b1, lanes], k_ref[:b1, lanes]
                else:
                    q_a, k_a = jnp.where(msk, q_ref[b0:b1, :], jnp.zeros((), BF16)), k_ref[:b1, :]
                cq = refs[5][0, b0:b1, a:a + 1] if forget else None
                ck = refs[6][0, a:a + 1, :b1] if forget else None
                q_a, left = _scaled_q(q_a, scale)
                p = _causal_probs(q_a, k_a, left, b0, cq, ck)
                dp = lax.dot_general(jnp.where(msk, do_b, jnp.zeros((), BF16)), v_ref[:b1, :], _DIMS["nt"],
                                     preferred_element_type=F32)
                delta = jnp.sum(jnp.where(msk, dd, 0.0), axis=-1, keepdims=True)
                ds = p * (dp - delta)
                if forget:
                    dcq_ref[0, b0:b1, a:a + 1] = jnp.sum(ds, axis=-1, keepdims=True)
                    dck_ref[0, a:a + 1, :b1] -= jnp.sum(ds, axis=0, keepdims=True)
                ds_b = ds.astype(BF16)
                dqs.append(jnp.dot(ds_b, k_a, preferred_element_type=F32) * scale)
                dk_a = lax.dot_general(ds_b, q_a, _DIMS["tn"], preferred_element_type=F32)
                dk_acc[:b1, lanes] += dk_a if left is None else dk_a * scale
                dv_acc[:b1, :] += jnp.where(mk, lax.dot_general(p.astype(BF16), do_b, _DIMS["tn"],
                                                                 preferred_element_type=F32), 0.0)
            if wide:
                dq_ref[b0:b1, :LANES] = dqs[0].astype(out_dtype)
                dq_ref[b0:b1, LANES:] = dqs[1].astype(out_dtype)
            else:
                dq_ref[b0:b1, :] = jnp.where(m0, dqs[0], dqs[1]).astype(out_dtype)
        dk_ref[...] = dk_acc[...].astype(out_dtype)
        dv_ref[...] = dv_acc[...].astype(out_dtype)

    pair = pl.BlockSpec((S, LANES), lambda h: (0, h))
    qk_out = pl.BlockSpec((S, qw), lambda h: (0, h))
    in_specs = [pl.BlockSpec((S, qw), lambda h: (0, q_off * LANES // qw + h)),
                pl.BlockSpec((S, qw), lambda h: (0, k_off * LANES // qw + h)),
                pl.BlockSpec((S, LANES), lambda h: (0, v_off + h)), pair, pair]
    ins = [q, k, v, o, do]
    out_specs = [qk_out, qk_out, pair]
    out_shape = [jax.ShapeDtypeStruct((S, PAIRS * qw), out_dtype)] * 2 + \
                [jax.ShapeDtypeStruct((S, PAIRS * LANES), out_dtype)]
    if forget:
        by_q, by_k = pl.BlockSpec((1, S, 2), lambda h: (h, 0, 0)), pl.BlockSpec((1, 2, S), lambda h: (h, 0, 0))
        in_specs += [by_q, by_k]
        ins += [cum, cum_t]
        out_specs += [by_q, by_k]
        out_shape += [jax.ShapeDtypeStruct((PAIRS, S, 2), F32), jax.ShapeDtypeStruct((PAIRS, 2, S), F32)]
    return pl.pallas_call(
        body, name=name, grid=(PAIRS,), in_specs=in_specs, out_specs=out_specs, out_shape=out_shape,
        scratch_shapes=[pltpu.VMEM((S, qw), F32), pltpu.VMEM((S, LANES), F32)],
        compiler_params=_params(("arbitrary",)),
    )(*ins)


def _tri(lower):
    r = lax.broadcasted_iota(jnp.int32, (QBLK, QBLK), 0)
    c = lax.broadcasted_iota(jnp.int32, (QBLK, QBLK), 1)
    return jnp.where((c <= r) if lower else (c >= r), 1.0, 0.0).astype(F32)


def _hi_dot(a, b):
    return jnp.dot(a, b, precision=lax.Precision.HIGHEST, preferred_element_type=F32)


def fox_gate_fwd(fl, bias, name):
    def body(f_ref, b_ref, o_ref):
        tri = _tri(True)
        carry = jnp.zeros((1, LANES), F32)
        for n in range(S // QBLK):
            x = f_ref[n * QBLK:(n + 1) * QBLK, :].astype(F32) + b_ref[...]
            lf = jnp.minimum(x, 0.0) - jnp.log(1.0 + jnp.exp(-jnp.abs(x)))
            c = _hi_dot(tri, lf) + carry
            o_ref[n * QBLK:(n + 1) * QBLK, :] = c
            carry = c[QBLK - 1:QBLK, :]

    return pl.pallas_call(body, name=name, out_shape=jax.ShapeDtypeStruct((S, LANES), F32),
                          compiler_params=_params())(fl, bias)


def fox_gate_bwd(fl, bias, dcq, dck, name):
    def body(f_ref, b_ref, dq_ref, dk_ref, o_ref, db_ref):
        tri = _tri(False)
        carry = jnp.zeros((1, LANES), F32)
        db = jnp.zeros((1, LANES), F32)
        for n in reversed(range(S // QBLK)):
            rows = slice(n * QBLK, (n + 1) * QBLK)
            dlf = _hi_dot(tri, dq_ref[rows, :] + dk_ref[rows, :]) + carry
            carry = dlf[0:1, :]
            x = f_ref[rows, :].astype(F32) + b_ref[...]
            dx = dlf * (1.0 - _sigmoid(x))
            o_ref[rows, :] = dx
            db = db + jnp.sum(dx, axis=0, keepdims=True)
        db_ref[...] = db

    return pl.pallas_call(body, name=name, out_shape=[jax.ShapeDtypeStruct((S, LANES), F32),
                                                      jax.ShapeDtypeStruct((1, LANES), F32)],
                          compiler_params=_params())(fl, bias, dcq, dck)


def _band_valid(first):
    w = QBLK if first else 2 * QBLK
    i = lax.broadcasted_iota(jnp.int32, (QBLK, w), 0)
    j = lax.broadcasted_iota(jnp.int32, (QBLK, w), 1)
    return (j <= i) if first else ((j >= i) & (j - QBLK <= i))


def _band_q(q_ref, rows, msk):
    return jnp.where(msk, q_ref[rows, :], jnp.zeros((), BF16)) * jnp.asarray(0.125, BF16)


def _band_logits(q_a, kk, bias, first):
    s = lax.dot_general(q_a, kk, _DIMS["nt"], preferred_element_type=F32) + bias
    return jnp.where(_band_valid(first), s, NEG)


def residue_major(t, d):
    return t if d == 1 else t.reshape(S // d, d, t.shape[1]).transpose(1, 0, 2).reshape(S, t.shape[1])


def token_major(t, d):
    return t if d == 1 else t.reshape(d, S // d, t.shape[1]).transpose(1, 0, 2).reshape(S, t.shape[1])


def dil_fwd(qkv, bias, g, name):
    d = DIL[g][1]
    ls = S // d

    def body(q_ref, k_ref, v_ref, b_ref, o_ref, l_ref):
        m0, m1 = _pair_masks((QBLK, LANES))
        for n in range(ls // QBLK):
            rows = slice(n * QBLK, (n + 1) * QBLK)
            keys = rows if n == 0 else slice((n - 1) * QBLK, (n + 1) * QBLK)
            os_, ls_ = [], []
            for a, msk in enumerate((m0, m1)):
                q_a = _band_q(q_ref, rows, msk)
                bias_a = b_ref[a, :, QBLK:] if n == 0 else b_ref[a]
                s = _band_logits(q_a, k_ref[keys, :], bias_a, n == 0)
                mx = jnp.max(s, axis=-1, keepdims=True)
                e = jnp.exp(s - mx)
                l = jnp.sum(e, axis=-1, keepdims=True)
                os_.append(jnp.dot((e * (1.0 / l)).astype(BF16), v_ref[keys, :], preferred_element_type=F32))
                ls_.append(mx + jnp.log(l))
            o_ref[rows, :] = jnp.where(m0, os_[0], os_[1])
            l_ref[rows, :] = jnp.where(m0, ls_[0], ls_[1])

    def col(j):
        return lambda h, r: (r, j * PAIRS + h)

    out = pl.BlockSpec((ls, LANES), lambda h, r: (r, h))
    return pl.pallas_call(
        body, name=name, grid=(PAIRS, d),
        in_specs=[pl.BlockSpec((ls, LANES), col(0)), pl.BlockSpec((ls, LANES), col(1)), pl.BlockSpec((ls, LANES), col(2)),
                  pl.BlockSpec((2, QBLK, 2 * QBLK), lambda h, r: (h, 0, 0))],
        out_specs=[out, out], out_shape=[jax.ShapeDtypeStruct((S, D), F32)] * 2,
        compiler_params=_params(("arbitrary", "arbitrary")),
    )(qkv, qkv, qkv, bias)


def dil_merge(os_, lses, name):
    def body(o0, o1, o2, l0, l1, l2, o_ref, l_ref):
        ls_ = [l0[...], l1[...], l2[...]]
        mx = jnp.maximum(jnp.maximum(ls_[0], ls_[1]), ls_[2])
        tot = mx + jnp.log(sum(jnp.exp(l - mx) for l in ls_))
        o_ref[...] = sum(jnp.exp(l - tot) * o[...] for l, o in zip(ls_, (o0, o1, o2)))
        l_ref[...] = tot

    return _rows_call(body, name, list(os_) + list(lses), [(D, F32), (D, F32)])


def dil_bwd(qkv, bias, o, lse, do, g, name):
    d = DIL[g][1]
    ls = S // d

    def body(q_ref, k_ref, v_ref, b_ref, o_ref, l_ref, do_ref, dq_ref, dk_ref, dv_ref, db_ref, dk_acc, dv_acc):
        @pl.when(pl.program_id(1) == 0)
        def _():
            db_ref[...] = jnp.zeros_like(db_ref)

        dk_acc[...] = jnp.zeros_like(dk_acc)
        dv_acc[...] = jnp.zeros_like(dv_acc)
        m0, m1 = _pair_masks((QBLK, LANES))
        for n in range(ls // QBLK):
            rows = slice(n * QBLK, (n + 1) * QBLK)
            keys = rows if n == 0 else slice((n - 1) * QBLK, (n + 1) * QBLK)
            nk = QBLK if n == 0 else 2 * QBLK
            do2, lse2 = do_ref[rows, :], l_ref[rows, :]
            dd = do2 * o_ref[rows, :]
            do_b = do2.astype(BF16)
            mk0, mk1 = _pair_masks((nk, LANES))
            dqs = []
            for a, (msk, mk) in enumerate(((m0, mk0), (m1, mk1))):
                q_a = _band_q(q_ref, rows, msk)
                kk = k_ref[keys, :]
                bias_a = b_ref[a, :, QBLK:] if n == 0 else b_ref[a]
                s = _band_logits(q_a, kk, bias_a, n == 0)
                lse_a = jnp.max(jnp.where(msk, lse2, -jnp.inf), axis=-1, keepdims=True)
                p = jnp.exp(s - lse_a)
                dp = lax.dot_general(jnp.where(msk, do_b, jnp.zeros((), BF16)), v_ref[keys, :], _DIMS["nt"],
                                     preferred_element_type=F32)
                delta = jnp.sum(jnp.where(msk, dd, 0.0), axis=-1, keepdims=True)
                ds = p * (dp - delta)
                if n == 0:
                    db_ref[a, :, QBLK:] += ds
                else:
                    db_ref[a] += ds
                ds_b = ds.astype(BF16)
                dqs.append(jnp.dot(ds_b, kk, preferred_element_type=F32) * 0.125)
                dk_acc[keys, :] += lax.dot_general(ds_b, q_a, _DIMS["tn"], preferred_element_type=F32)
                dv_acc[keys, :] += jnp.where(mk, lax.dot_general(p.astype(BF16), do_b, _DIMS["tn"],
                                                                 preferred_element_type=F32), 0.0)
            dq_ref[rows, :] = jnp.where(m0, dqs[0], dqs[1]).astype(BF16)
        dk_ref[...] = dk_acc[...].astype(BF16)
        dv_ref[...] = dv_acc[...].astype(BF16)

    def col(j):
        return lambda h, r: (r, j * PAIRS + h)

    nat = pl.BlockSpec((ls, LANES), lambda h, r: (r, h))
    b_spec = pl.BlockSpec((2, QBLK, 2 * QBLK), lambda h, r: (h, 0, 0))
    return pl.pallas_call(
        body, name=name, grid=(PAIRS, d),
        in_specs=[pl.BlockSpec((ls, LANES), col(0)), pl.BlockSpec((ls, LANES), col(1)), pl.BlockSpec((ls, LANES), col(2)),
                  b_spec, nat, nat, nat],
        out_specs=[nat, nat, nat, b_spec],
        out_shape=[jax.ShapeDtypeStruct((S, D), BF16)] * 3 + [jax.ShapeDtypeStruct((HEADS, QBLK, 2 * QBLK), F32)],
        scratch_shapes=[pltpu.VMEM((ls, LANES), F32), pltpu.VMEM((ls, LANES), F32)],
        compiler_params=_params(("arbitrary", "arbitrary")),
    )(qkv, qkv, qkv, bias, o, lse, do)


def _place():
    x, y, c = lax.axis_index("x"), lax.axis_index("y"), lax.axis_index("c")
    return x, y, c


def _dev_slot(ref, by_rows, dev):
    return ref.at[:, dev] if by_rows else ref.at[dev]


def all_gather(shards, by_rows, name, in_vmem=False, after=()):
    n, n_after = len(shards), len(after)

    def body(*refs):
        x_refs, out_refs = refs[:n], refs[n + n_after:2 * n + n_after]
        send_sems, recv_sems, local_sems = refs[2 * n + n_after:]
        x, y, c = _place()
        me, sibling = (x, y, c), (x, y, 1 - c)
        chips = [(1 - x, y), (x, 1 - y), (1 - x, 1 - y)]

        def slot(t, px, py, pc):
            return _dev_slot(out_refs[t], by_rows[t], 4 * px + 2 * py + pc)

        def copy(t, k, blk, to, src=None):
            return pltpu.make_async_remote_copy(
                src_ref=slot(t, *blk) if src is None else src, dst_ref=slot(t, *blk), send_sem=send_sems.at[7 * t + k],
                recv_sem=recv_sems.at[7 * t + k], device_id=to, device_id_type=MESH_ID)

        mine = [pltpu.make_async_copy(x_refs[t], slot(t, *me), local_sems.at[t]) for t in range(n)]
        for cp in mine:
            cp.start()
        first = []
        for t in range(n):
            first.append(copy(t, 0, me, sibling, src=x_refs[t]))
            first += [copy(t, 1 + j, me, (*chip, c), src=x_refs[t]) for j, chip in enumerate(chips)]
        for cp in first:
            cp.start()
        passed = []
        for j, chip in enumerate(chips):
            for t in range(n):
                copy(t, 1 + j, (*chip, c), me).wait_recv()
                passed.append(copy(t, 4 + j, (*chip, c), sibling))
                passed[-1].start()
        for t in range(n):
            copy(t, 0, sibling, me).wait_recv()
            for j, chip in enumerate(chips):
                copy(t, 4 + j, (*chip, 1 - c), me).wait_recv()
        for cp in first + passed:
            cp.wait_send()
        for cp in mine:
            cp.wait()

    def gathered(s, rows):
        shp = (s.shape[0], N_DEV) + s.shape[1:] if rows else (N_DEV,) + s.shape
        return jax.ShapeDtypeStruct(shp, s.dtype)

    space = pl.BlockSpec(memory_space=pltpu.VMEM if in_vmem else pl.ANY)
    return pl.pallas_call(
        body, name=name, out_shape=[gathered(s, r) for s, r in zip(shards, by_rows)],
        in_specs=[space] * n + [pl.BlockSpec(memory_space=pl.ANY)] * n_after, out_specs=[space] * n,
        scratch_shapes=[pltpu.SemaphoreType.DMA((7 * n,)), pltpu.SemaphoreType.DMA((7 * n,)),
                        pltpu.SemaphoreType.DMA((n,))],
        compiler_params=pltpu.CompilerParams(vmem_limit_bytes=VMEM_LIMIT),
    )(*shards, *after)


_HBM = pl.BlockSpec(memory_space=pltpu.HBM)
_SEM = pl.BlockSpec(memory_space=pltpu.SEMAPHORE)
_SPLIT = dict(has_side_effects=pltpu.SideEffectType.DATAFLOW_SIDE_EFFECTING)


def _hbm(a):
    return pltpu.with_memory_space_constraint(a, pltpu.HBM)


def _gathered_shape(s, rows):
    return (s.shape[0], N_DEV) + s.shape[1:] if rows else (N_DEV,) + s.shape


def _peers(x, y, c):
    return [(1 - x if k & 4 else x, 1 - y if k & 2 else y, 1 - c if k & 1 else c) for k in range(1, N_DEV)]


def gather_start(shards, lands, by_rows, name):
    n = len(shards)

    def body(*refs):
        x_refs, land_refs = refs[:n], refs[n:2 * n]
        send_sems, recv_sems = refs[2 * n], refs[2 * n + 1]
        x, y, c = _place()
        me = 4 * x + 2 * y + c
        for t in range(n):
            for k, peer in enumerate(_peers(x, y, c)):
                pltpu.make_async_remote_copy(
                    src_ref=x_refs[t], dst_ref=_dev_slot(land_refs[t], by_rows[t], me), send_sem=send_sems.at[7 * t + k],
                    recv_sem=recv_sems.at[7 * t + k], device_id=peer, device_id_type=MESH_ID).start()

    sems = pltpu.SemaphoreType.DMA((7 * n,))
    res = pl.pallas_call(
        body, name=name,
        out_shape=(sems, sems) + tuple(pltpu.HBM(a.shape, a.dtype) for a in list(shards) + list(lands)),
        in_specs=[_HBM] * (2 * n), out_specs=(_SEM, _SEM) + (_HBM,) * (2 * n),
        input_output_aliases={i: 2 + i for i in range(2 * n)},
        compiler_params=pltpu.CompilerParams(**_SPLIT),
    )(*[_hbm(a) for a in list(shards) + list(lands)])
    return res[0], res[1], list(res[2:2 + n]), list(res[2 + n:])


def gather_wait(send_sems, recv_sems, first, shards, lands, by_rows, after, name):
    n = len(shards)

    def body(*refs):
        x_refs, land_refs = refs[:n], refs[n:2 * n]
        send_sems, recv_sems = refs[2 * n], refs[2 * n + 1]
        x, y, c = _place()
        for t in range(n):
            for k, (px, py, pc) in enumerate(_peers(x, y, c)):
                cp = pltpu.make_async_remote_copy(
                    src_ref=x_refs[t], dst_ref=_dev_slot(land_refs[t], by_rows[t], 4 * px + 2 * py + pc),
                    send_sem=send_sems.at[7 * (first + t) + k], recv_sem=recv_sems.at[7 * (first + t) + k],
                    device_id=(px, py, pc), device_id_type=MESH_ID)
                cp.wait_send()
                cp.wait_recv()

    res = pl.pallas_call(
        body, name=name, out_shape=tuple(pltpu.HBM(a.shape, a.dtype) for a in list(shards) + list(lands)),
        in_specs=[_HBM] * (2 * n) + [_SEM, _SEM, pl.BlockSpec(memory_space=pl.ANY)], out_specs=(_HBM,) * (2 * n),
        input_output_aliases={i: i for i in range(2 * n)},
        compiler_params=pltpu.CompilerParams(**_SPLIT),
    )(*shards, *lands, send_sems, recv_sems, after)
    return list(res[n:])


def scatter_start(srcs, src_l, lands, land_l, by_rows, name):
    n = len(srcs)

    def body(*refs):
        x_refs, land_refs = refs[:n], refs[n:2 * n]
        send_sems, recv_sems, token = refs[2 * n], refs[2 * n + 1], refs[-1]
        x, y, c = _place()
        me = 4 * x + 2 * y + c
        for k, (px, py, pc) in enumerate(_peers(x, y, c)):
            for t in range(n):
                blk = _dev_slot(x_refs[t], by_rows[t], 4 * px + 2 * py + pc)
                pltpu.make_async_remote_copy(
                    src_ref=blk.at[src_l[t]], dst_ref=land_refs[t].at[me, land_l[t]], send_sem=send_sems.at[7 * t + k],
                    recv_sem=recv_sems.at[7 * t + k], device_id=(px, py, pc), device_id_type=MESH_ID).start()
        token[...] = jnp.zeros_like(token)

    lands = [lax.empty((N_DEV, 1) + s.shape[2:], s.dtype) if ld is None else ld for s, ld in zip(srcs, lands)]
    sems = pltpu.SemaphoreType.DMA((7 * n,))
    res = pl.pallas_call(
        body, name=name,
        out_shape=(sems, sems) + tuple(pltpu.HBM(a.shape, a.dtype) for a in list(srcs) + lands)
        + (jax.ShapeDtypeStruct((8, LANES), F32),),
        in_specs=[_HBM] * (2 * n),
        out_specs=(_SEM, _SEM) + (_HBM,) * (2 * n) + (pl.BlockSpec(memory_space=pltpu.VMEM),),
        input_output_aliases={i: 2 + i for i in range(2 * n)},
        compiler_params=pltpu.CompilerParams(**_SPLIT),
    )(*[_hbm(a) for a in list(srcs) + lands])
    return res[0], res[1], list(res[2:2 + n]), list(res[2 + n:2 + 2 * n]), res[-1]


def scatter_wait(send_sems, recv_sems, srcs, src_l, lands, land_l, by_rows, after, name):
    n = len(srcs)

    def body(*refs):
        x_refs, land_refs = refs[:n], refs[n:2 * n]
        send_sems, recv_sems = refs[2 * n], refs[2 * n + 1]
        x, y, c = _place()
        for k, (px, py, pc) in enumerate(_peers(x, y, c)):
            peer = 4 * px + 2 * py + pc
            for t in range(n):
                cp = pltpu.make_async_remote_copy(
                    src_ref=_dev_slot(x_refs[t], by_rows[t], peer).at[src_l[t]], dst_ref=land_refs[t].at[peer, land_l[t]],
                    send_sem=send_sems.at[7 * t + k], recv_sem=recv_sems.at[7 * t + k], device_id=(px, py, pc),
                    device_id_type=MESH_ID)
                cp.wait_send()
                cp.wait_recv()

    res = pl.pallas_call(
        body, name=name, out_shape=tuple(pltpu.HBM(a.shape, a.dtype) for a in list(srcs) + list(lands)),
        in_specs=[_HBM] * (2 * n) + [_SEM, _SEM, pl.BlockSpec(memory_space=pl.ANY)], out_specs=(_HBM,) * (2 * n),
        input_output_aliases={i: i for i in range(2 * n)},
        compiler_params=pltpu.CompilerParams(**_SPLIT),
    )(*srcs, *lands, send_sems, recv_sems, after)
    return list(res[:n]), list(res[n:])


ADAM_BLOCK_BYTES = 3 << 19


def adamw(w, m, v, parts, name):
    n_parts = parts.shape[0]
    n_l, r, c = w.shape
    lane_c = -(-c // LANES) * LANES
    fits = [t for t in range(16, r, 16) if r % t == 0 and t * lane_c * 4 <= ADAM_BLOCK_BYTES]
    tr = max(fits) if fits and r * lane_c * 4 > ADAM_BLOCK_BYTES else r
    c1 = 1.0 / (1.0 - ADAM_B1 ** ADAM_STEP)
    c2 = 1.0 / (1.0 - ADAM_B2 ** ADAM_STEP)

    def body(w_ref, m_ref, v_ref, p_ref, g_ref, d_ref, nm_ref, nv_ref):
        g = p_ref[0].astype(F32)
        for j in range(1, n_parts):
            g = g + p_ref[j].astype(F32)
        nm = ADAM_B1 * m_ref[...] + (1.0 - ADAM_B1) * g
        nv = ADAM_B2 * v_ref[...] + (1.0 - ADAM_B2) * (g * g)
        g_ref[...] = g
        nm_ref[...] = nm
        nv_ref[...] = nv
        d_ref[...] = -ADAM_LR * ((nm * c1) / (jnp.sqrt(nv * c2) + ADAM_EPS) + ADAM_WD * w_ref[...])

    blk = pl.BlockSpec((1, tr, c), lambda l, i: (l, i, 0))
    return pl.pallas_call(
        body, name=name, grid=(n_l, r // tr),
        in_specs=[blk, blk, blk, pl.BlockSpec((n_parts, 1, tr, c), lambda l, i: (0, l, i, 0))],
        out_specs=[blk] * 4, out_shape=[jax.ShapeDtypeStruct((n_l, r, c), F32)] * 4,
        compiler_params=_params(("parallel", "parallel")),
    )(w, m, v, parts)


def sum_parts(parts, name):
    def body(p_ref, o_ref):
        g = p_ref[0]
        for j in range(1, parts.shape[0]):
            g = g + p_ref[j]
        o_ref[...] = g

    return pl.pallas_call(body, name=name, out_shape=jax.ShapeDtypeStruct(parts.shape[1:], F32),
                          compiler_params=_params())(parts)


def _pack(arrs, rows, dtype):
    flat = jnp.concatenate([a.reshape(-1).astype(dtype) for a in arrs])
    return jnp.pad(flat, (0, rows * LANES - flat.shape[0])).reshape(rows, LANES)


def _unpack(packed, shapes):
    flat, out, off = packed.reshape(-1), [], 0
    for shp in shapes:
        n = int(np.prod(shp))
        out.append(flat[off:off + n].reshape(shp))
        off += n
    return out


def _cat(parts):
    return jnp.concatenate(parts, axis=1)


def _layer_list(i):
    kind, j = i % 3, i // 3
    mix = ([("mla_w_a", j), ("mla_w_uq", j), ("mla_w_ukv", j), ("mla_w_o", j)] if kind == 0 else
           [("dil_w_qkv", 0), ("dil_w_o", 0)] if kind == 1 else [("fox_w_qkvf", 0), ("fox_w_o", 0)])
    return mix + [("ffn_w_in", i), ("ffn_w_out", i), ("ple_w_proj", i), ("ple_w_gate", i)]


def _mla_layout(w_a, g_uq, g_ukv, j):
    def z(r, n):
        return jnp.zeros((r, n), BF16)

    wa = w_a[j]
    a = _cat([wa[:, :640], wa[:, 640:656], z(D, 48), wa[:, 656:672], z(D, 48)])
    q, k, v = [], [], []
    for h in range(HEADS):
        b = g_uq[h // 2, j][:, 96 * (h % 2):96 * (h % 2 + 1)]
        q += [b[:, 64:80], b[:, 0:32], z(Q_RANK, 16), b[:, 80:96], b[:, 32:64], z(Q_RANK, 16)]
        b = g_ukv[h // 2, j][:, LANES * (h % 2):LANES * (h % 2 + 1)]
        k += [z(KV_RANK, 16), b[:, 0:32], z(KV_RANK, 32), b[:, 32:64], z(KV_RANK, 16)]
        v.append(b[:, 64:128])
    return a, _cat(q), _cat(k + v)


def _mla_unlayout(d_a, d_uq, d_ukv):
    a = _cat([d_a[:, :640], d_a[:, 640:656], d_a[:, 704:720]])
    uq, ukv = [], []
    for dev in range(N_DEV):
        q, kv = [], []
        for h in (2 * dev, 2 * dev + 1):
            b = d_uq[:, LANES * h:LANES * (h + 1)]
            q += [b[:, 16:48], b[:, 80:112], b[:, 0:16], b[:, 64:80]]
            b = d_ukv[:, LANES * h:LANES * (h + 1)]
            kv += [b[:, 16:48], b[:, 80:112], d_ukv[:, HEADS * LANES + 64 * h:HEADS * LANES + 64 * (h + 1)]]
        uq.append(_cat(q))
        ukv.append(_cat(kv))
    return a, jnp.stack(uq), jnp.stack(ukv)


def _mixer_fwd(kind, tag, hn, W, aux):
    if kind == 0:
        a = mm(hn, W["w_a"], "nn", f"{tag}_a", tn=768)
        cq = rms_fwd(a[:, :Q_RANK], W["q_norm"], f"{tag}_cq", out_dtype=BF16)
        ckv = rms_fwd(a[:, Q_RANK:Q_RANK + KV_RANK], W["kv_norm"], f"{tag}_ckv", out_dtype=BF16)
        qp = mm(cq, W["w_uq"], "nn", f"{tag}_uq", tk=384)
        kvp = mm(ckv, W["w_ukv"], "nn", f"{tag}_ukv", tk=256)
        q, k = mla_qk_fwd(qp, kvp, a[:, 640:], aux["cos"], aux["sin"], f"{tag}_qk")
        v = kvp.astype(BF16)
        o = attn_fwd(q, k, v, f"{tag}_attn", wide=True, scale=96 ** -0.5, v_off=HEADS)
        y = mm(o, W["w_o"], "nn", f"{tag}_o")
        return y, (a, cq, ckv, q, k, v, o)
    if kind == 1:
        qkv = [mm(residue_major(hn, DIL[g][1]), W["w_qkv"], "nn", f"{tag}_qkv{g}", out_dtype=BF16, tn=384,
                  b_cols=(8 * g, 8)) for g in range(3)]
        parts = [dil_fwd(qkv[g], aux["dil_bias"][g], g, f"{tag}_g{g}") for g in range(3)]
        o, lse = dil_merge([token_major(p_[0], DIL[g][1]) for g, p_ in enumerate(parts)],
                           [token_major(p_[1], DIL[g][1]) for g, p_ in enumerate(parts)], f"{tag}_merge")
        y = mm(o, W["w_o"], "nn", f"{tag}_o")
        return y, (qkv, o, lse)
    a = mm(hn, W["w_qkvf"], "nn", f"{tag}_qkvf", tn=640)
    fl = a[:, 3072:]
    cum = fox_gate_fwd(fl, aux["fox_b"], f"{tag}_gate")[:, :HEADS]
    cum_q = cum.reshape(S, PAIRS, 2).transpose(1, 0, 2)
    cum_k = cum.T.reshape(PAIRS, 2, S)
    ab = a.astype(BF16)
    o = attn_fwd(ab, ab, ab, f"{tag}_attn", wide=False, scale=0.125, k_off=PAIRS, v_off=2 * PAIRS, cum=cum_q, cum_t=cum_k)
    y = mm(o, W["w_o"], "nn", f"{tag}_o")
    return y, (fl, ab, cum_q, cum_k, o)


def _mixer_bwd(kind, tag, hn, dy, W, aux, saved):
    gr = {}
    if kind == 0:
        a, cq, ckv, q, k, v, o = saved
        gr["w_o"] = mm(o, dy, "tn", f"{tag}_dwo", out_dtype=BF16)
        do = mm(dy, W["w_o"], "nt", f"{tag}_do")
        dq, dk, dv = attn_bwd(q, k, v, o, do, f"{tag}_attn_b", wide=True, scale=96 ** -0.5, v_off=HEADS)
        dqp, dkr = mla_qk_bwd(dq, dk, aux["cos"], aux["sin"], f"{tag}_qk_b")
        dkvp = jnp.concatenate([dk, dv], axis=1)
        gr["w_ukv"] = mm(ckv, dkvp, "tn", f"{tag}_dwukv", out_dtype=BF16, tm=256)
        dckv = mm(dkvp, W["w_ukv"], "nt", f"{tag}_dckv", tn=256)
        gr["w_uq"] = mm(cq, dqp, "tn", f"{tag}_dwuq", out_dtype=BF16, tm=384)
        dcq = mm(dqp, W["w_uq"], "nt", f"{tag}_dcq", tn=384)
        da_q, gr["q_norm"] = rms_bwd(a[:, :Q_RANK], W["q_norm"], dcq, f"{tag}_cq_b", out_dtype=BF16)
        da_kv, gr["kv_norm"] = rms_bwd(a[:, Q_RANK:Q_RANK + KV_RANK], W["kv_norm"], dckv, f"{tag}_ckv_b",
                                       out_dtype=BF16)
        da = jnp.concatenate([da_q, da_kv, dkr], axis=1)
        gr["w_a"] = mm(hn, da, "tn", f"{tag}_dwa", out_dtype=BF16, tn=768)
        return mm(da, W["w_a"], "nt", f"{tag}_dhn", tk=768), gr
    if kind == 1:
        qkv, o, lse = saved
        gr["w_o"] = mm(o, dy, "tn", f"{tag}_dwo", out_dtype=BF16)
        do = mm(dy, W["w_o"], "nt", f"{tag}_do")
        cols, dbs = [], []
        for g, (_, d) in enumerate(DIL):
            dq, dk, dv, db = dil_bwd(qkv[g], aux["dil_bias"][g], residue_major(o, d), residue_major(lse, d),
                                     residue_major(do, d), g, f"{tag}_g{g}_b")
            cols += [token_major(t, d) for t in (dq, dk, dv)]
            dbs.append(db)
        dqkv = jnp.concatenate(cols, axis=1)
        gr["dil_dbias"] = dbs
        gr["w_qkv"] = mm(hn, dqkv, "tn", f"{tag}_dwqkv", out_dtype=BF16, out_dev=1152, tn=1152)
        return mm(dqkv, W["w_qkv"], "nt", f"{tag}_dhn", tk=1152), gr
    fl, ab, cum_q, cum_k, o = saved
    gr["w_o"] = mm(o, dy, "tn", f"{tag}_dwo", out_dtype=BF16)
    do = mm(dy, W["w_o"], "nt", f"{tag}_do")
    dq, dk, dv, dcq, dck = attn_bwd(ab, ab, ab, o, do, f"{tag}_attn_b", wide=False, scale=0.125, out_dtype=BF16,
                                    k_off=PAIRS, v_off=2 * PAIRS, cum=cum_q, cum_t=cum_k)
    pad = ((0, 0), (0, LANES - HEADS))
    dcq = jnp.pad(dcq.transpose(1, 0, 2).reshape(S, HEADS), pad)
    dck = jnp.pad(dck.reshape(HEADS, S).T, pad)
    dfl, gr["b_f"] = fox_gate_bwd(fl, aux["fox_b"], dcq, dck, f"{tag}_gate_b")
    da = jnp.concatenate([dq, dk, dv, dfl.astype(BF16)], axis=1)
    gr["w_qkvf"] = mm(hn, da, "tn", f"{tag}_dwqkvf", out_dtype=BF16, tn=640)
    return mm(da, W["w_qkvf"], "nt", f"{tag}_dhn", tk=640), gr


def kernel(x, p, positions, norm_g, ffn_w_in, ffn_w_out, ple_w_proj, ple_w_gate, rel_bias, mla_w_a, mla_q_norm, mla_kv_norm, mla_w_uq, mla_w_ukv, mla_w_o, dil_w_qkv, dil_w_o, fox_w_qkvf, fox_b_f, fox_w_o, loss_target, m_norm_g, m_ffn_w_in, m_ffn_w_out, m_ple_w_proj, m_ple_w_gate, m_rel_bias, m_mla_w_a, m_mla_q_norm, m_mla_kv_norm, m_mla_w_uq, m_mla_w_ukv, m_mla_w_o, m_dil_w_qkv, m_dil_w_o, m_fox_w_qkvf, m_fox_b_f, m_fox_w_o, v_norm_g, v_ffn_w_in, v_ffn_w_out, v_ple_w_proj, v_ple_w_gate, v_rel_bias, v_mla_w_a, v_mla_q_norm, v_mla_kv_norm, v_mla_w_uq, v_mla_w_ukv, v_mla_w_o, v_dil_w_qkv, v_dil_w_o, v_fox_w_qkvf, v_fox_b_f, v_fox_w_o):
    given = dict(locals())
    me = 4 * lax.axis_index("x") + 2 * lax.axis_index("y") + lax.axis_index("c")
    for n in TRANSPOSED:
        for pre in ("", "m_", "v_"):
            given[pre + n] = jnp.swapaxes(given[pre + n], 1, 2)

    rows_of = {n: axis == 1 for n, _, axis in BIG}
    rows_of["gains"] = False
    shape_of = {n: shp for n, shp, _ in BIG}
    lists = [_layer_list(i) for i in range(DEPTH)]
    lists[0] = [("gains", 0)] + lists[0]
    flat = [nl for ls in lists for nl in ls]
    flat_rows = [rows_of[n] for n, _ in flat]
    gain_rows = _rows(sum(int(np.prod(s)) for _, s, _ in SMALL_SHARDED))
    shards = [_pack([given[k] for k, _, _ in SMALL_SHARDED], gain_rows, F32)[None] if n == "gains" else
              given[n][l:l + 1].astype(BF16) for n, l in flat]
    lands = [lax.dynamic_update_slice(lax.empty(_gathered_shape(s, r), s.dtype), s[:, None] if r else s[None],
                                      (0, me, 0, 0) if r else (me, 0, 0, 0)) for s, r in zip(shards, flat_rows)]
    send_s, recv_s, shards, lands = gather_start(shards, lands, flat_rows, "gather_start")

    cos, sin = rope_tables(positions.reshape(S, 1), "rope_tables")
    dil_bias = [mm(rel_bias[:, HEADS * g:HEADS * (g + 1)], jnp.asarray(_bucket_onehot(DIL[g][1])), "tn",
                   f"dil_bias{g}", precise=True, tn=4096).reshape(HEADS, QBLK, 2 * QBLK) for g in range(3)]
    aux = {"cos": cos, "sin": sin, "dil_bias": dil_bias,
           "fox_b": jnp.pad(fox_b_f, ((0, 0), (0, LANES - HEADS)))}

    def arrived(i, part, behind):
        n_mix = len(lists[i]) - 4
        first = sum(len(ls) for ls in lists[:i]) + (n_mix if part else 0)
        sl = slice(first, first + (4 if part else n_mix))
        got = gather_wait(send_s, recv_s, first, shards[sl], lands[sl], flat_rows[sl], behind, f"gather_wait{i}_{part}")
        return {n: g.reshape(1, N_DEV * shape_of[n][1], shape_of[n][2]) if rows_of[n] else g
                for (n, _), g in zip(flat[sl], got)}

    full = {}

    def mixer_weights(i, behind):
        kind, j = i % 3, i // 3
        w = arrived(i, 0, behind)
        if i == 0:
            gains, off = w["gains"].reshape(N_DEV, gain_rows * LANES), 0
            for n, shp, axis in SMALL_SHARDED:
                cnt = int(np.prod(shp))
                g = jnp.moveaxis(gains[:, off:off + cnt].reshape((N_DEV,) + shp), 0, axis)
                full[n] = g.reshape(shp[:axis] + (N_DEV * shp[axis],))
                off += cnt
        W = {"g": [full["norm_g"][i, r][None, :] for r in range(4)]}
        if kind == 0:
            w_a, w_uq, w_ukv = _mla_layout(w["mla_w_a"], w["mla_w_uq"], w["mla_w_ukv"], 0)
            W.update(w_a=w_a, w_uq=w_uq, w_ukv=w_ukv, w_o=Lay(w["mla_w_o"], 0),
                     q_norm=full["mla_q_norm"][j][None, :], kv_norm=full["mla_kv_norm"][j][None, :])
        elif kind == 1:
            W.update(w_qkv=Dev(w["dil_w_qkv"], 0), w_o=Lay(w["dil_w_o"], 0))
        else:
            fox_w = jnp.pad(_cat([w["fox_w_qkvf"][dev, 0] for dev in range(N_DEV)]), ((0, 0), (0, FOX_W - 3088)))
            W.update(w_qkvf=fox_w, w_o=Lay(w["fox_w_o"], 0))
        return W

    def ffn_weights(i, behind):
        w = arrived(i, 1, behind)
        return {"w_in_t": Lay(w["ffn_w_in"], 0), "w_out": Lay(w["ffn_w_out"], 0), "w_gate": Lay(w["ple_w_gate"], 0),
                "w_proj": _cat([w["ple_w_proj"][dev, 0] for dev in range(N_DEV)])}

    h = x[0]
    saved, weights = [], []
    for i in range(DEPTH):
        kind, j, W = i % 3, i // 3, mixer_weights(i, h)
        weights.append(W)
        t = f"l{i}"
        hn = rms_fwd(h, W["g"][0], f"{t}_n0", out_dtype=BF16)
        y, mix = _mixer_fwd(kind, f"{t}_mix", hn, W, aux)
        W.update(ffn_weights(i, y))
        h1 = rms_fwd(y, W["g"][1], f"{t}_n1", res=h)
        fin = rms_fwd(h1, W["g"][2], f"{t}_n2", out_dtype=BF16)
        gu = mm(fin, W["w_in_t"], "nt", f"{t}_ffn_in")
        act = swiglu_fwd(gu, f"{t}_swiglu")
        f = mm(act, W["w_out"], "nn", f"{t}_ffn_out")
        h2 = rms_fwd(f, W["g"][3], f"{t}_n3", res=h1)
        pp = mm(p[i, 0], W["w_proj"], "nn", f"{t}_ple_p")
        gt = mm(h2, W["w_gate"], "nn", f"{t}_ple_g")
        h3 = ple_fwd(h2, pp, gt, f"{t}_ple")
        saved.append((h, hn, y, h1, fin, gu, act, f, h2, pp, gt, mix))
        h = h3

    dh, loss_lanes = loss_head(h, loss_target[0], "loss_head")

    grads = {n: None for n, _, _ in BIG}
    landed = {n: (lax.empty((N_DEV,) + shp, BF16) if shp[0] > 1 else None) for n, shp, _ in BIG}
    in_flight = []
    own = {n: [] for n, _, _ in BIG}
    g_norm = [[None] * 4 for _ in range(DEPTH)]
    g_qn, g_kvn = [None, None], [None, None]
    g_rel, g_bf = None, None

    def stacked(n):
        g = grads[n]
        return None if g is None else g.reshape(g.shape[0], N_DEV * g.shape[2], g.shape[3])

    def by_device(g):
        return g.reshape(g.shape[0], N_DEV, g.shape[1] // N_DEV, g.shape[2])

    def start(i, entries, mine, tag):
        names = [n for n, _ in entries]
        srcs = [mine[n] if n in mine else grads[n] for n in names]
        src_l = [0 if n in mine else i for n in names]
        land_l = [l if shape_of[n][0] > 1 else 0 for n, l in entries]
        rows = [rows_of[n] for n in names]
        s_sem, r_sem, srcs, got, token = scatter_start(srcs, src_l, [landed[n] for n in names], land_l, rows, tag)
        for n, src, ld in zip(names, srcs, got):
            landed[n] = ld
            if n in mine:
                mine[n] = src
            else:
                grads[n] = src
        in_flight.append((s_sem, r_sem, names, mine, src_l, land_l, rows))
        return token

    token = None
    for i in reversed(range(DEPTH)):
        kind, j, W = i % 3, i // 3, weights[i]
        t = f"l{i}b"
        h0, hn, y, h1, fin, gu, act, f, h2, pp, gt, mix = saved[i]
        dpp, dgt = ple_bwd(dh, pp, gt, f"{t}_ple")
        grads["ple_w_proj"] = mm(p[i, 0], dpp, "tn", f"{t}_dwp", out_dtype=BF16, out_dev=LANES, tm=256, tn=LANES,
                                 stack=(grads["ple_w_proj"], DEPTH, i), after=token)
        grads["ple_w_gate"] = by_device(mm(h2, dgt, "tn", f"{t}_dwg", out_dtype=BF16,
                                           stack=(stacked("ple_w_gate"), DEPTH, i)))
        dh2 = mm(dgt, W["w_gate"], "nt", f"{t}_dh2", add=dh)
        df, g_norm[i][3] = rms_bwd(f, W["g"][3], dh2, f"{t}_n3", out_dtype=BF16)
        grads["ffn_w_out"] = by_device(mm(act, df, "tn", f"{t}_dwout", out_dtype=BF16, tm=1408,
                                          stack=(stacked("ffn_w_out"), DEPTH, i)))
        dact = mm(df, W["w_out"], "nt", f"{t}_dact", tn=1408)
        dgu = swiglu_bwd(gu, dact, f"{t}_swiglu")
        grads["ffn_w_in"] = by_device(mm(dgu, fin, "tn", f"{t}_dwin", out_dtype=BF16, tm=512, tn=1024,
                                         stack=(stacked("ffn_w_in"), DEPTH, i)))
        token = start(i, lists[i][-4:], {}, f"scatter_ffn{i}")
        dfin = mm(dgu, W["w_in_t"], "nn", f"{t}_dfin", after=token, tk=1408)
        dh1, g_norm[i][2] = rms_bwd(h1, W["g"][2], dfin, f"{t}_n2", res=dh2)
        dy, g_norm[i][1] = rms_bwd(y, W["g"][1], dh1, f"{t}_n1", out_dtype=BF16)
        dhn, gr = _mixer_bwd(kind, f"{t}_mix", hn, dy, W, aux, mix)
        if kind == 0:
            d_a, d_uq, d_ukv = _mla_unlayout(gr["w_a"], gr["w_uq"], gr["w_ukv"])
            mine = {"mla_w_a": by_device(d_a[None]), "mla_w_uq": d_uq[:, None], "mla_w_ukv": d_ukv[:, None],
                    "mla_w_o": by_device(gr["w_o"][None])}
            g_qn[j], g_kvn[j] = gr["q_norm"], gr["kv_norm"]
        elif kind == 1:
            mine = {"dil_w_qkv": gr["w_qkv"], "dil_w_o": by_device(gr["w_o"][None])}
            g_rel = jnp.concatenate(
                [mm(jnp.asarray(_bucket_onehot(DIL[g][1])), gr["dil_dbias"][g].reshape(HEADS, -1), "nt",
                    f"{t}_drel{g}", precise=True, tk=4096) for g in range(3)], axis=1)
        else:
            wide = gr["w_qkvf"]
            mine = {"fox_w_qkvf": jnp.stack([wide[:, 386 * dev:386 * (dev + 1)] for dev in range(N_DEV)])[:, None],
                    "fox_w_o": by_device(gr["w_o"][None])}
            g_bf = gr["b_f"][:, :HEADS]
        token = start(i, [e for e in lists[i][:-4] if e[0] in mine], mine, f"scatter_mix{i}")
        dh, g_norm[i][0] = rms_bwd(h0, W["g"][0], dhn, f"{t}_n0", res=dh1, after=token)
    grad_x = dh[None]

    updated = {}

    def finish(groups, behind):
        for idx, (s_sem, r_sem, names, mine, src_l, land_l, rows) in groups:
            srcs = [mine[n] if n in mine else grads[n] for n in names]
            srcs, got = scatter_wait(s_sem, r_sem, srcs, src_l, [landed[n] for n in names], land_l, rows, behind,
                                     f"scatter_wait{idx}")
            for n, src, ld in zip(names, srcs, got):
                landed[n] = ld
                if n in mine:
                    mine[n] = src
                else:
                    grads[n] = src
        for _, (_, _, names, mine, src_l, land_l, rows) in groups:
            for n, sl, ll, rw in zip(names, src_l, land_l, rows):
                src = mine[n] if n in mine else grads[n]
                own[n].append((ll, lax.dynamic_index_in_dim(src, me, axis=1 if rw else 0, keepdims=False)[sl]))
        for n in dict.fromkeys(n for _, g in groups for n in g[2]):
            part = landed[n]
            for ll, blk in own[n]:
                part = lax.dynamic_update_slice(part, blk[None, None], (me, ll, 0, 0))
            updated[n] = adamw(given[n], given["m_" + n], given["v_" + n], part, f"adamw_{n}")

    finish([(k, g) for k, g in enumerate(in_flight) if g[2][0] == "ffn_w_in"], dh)
    finish([(k, g) for k, g in enumerate(in_flight) if g[2][0] != "ffn_w_in"], updated["ffn_w_in"][0])
    big_out = [updated[n] for n, _, _ in BIG]

    small_full = [jnp.stack([jnp.concatenate(r, axis=0) for r in g_norm]).reshape(-1),
                  jnp.concatenate(g_qn, axis=0).reshape(-1), jnp.concatenate(g_kvn, axis=0).reshape(-1),
                  g_rel.reshape(-1), g_bf.reshape(-1), loss_lanes.reshape(-1)]
    small_n = sum(a.shape[0] for a in small_full)
    small_rows = _rows(small_n)
    parts, = all_gather([_pack(small_full, small_rows, F32)], [False], "gather_small_grads", in_vmem=True,
                        after=[big_out[idx][0] for idx, (n, _, _) in enumerate(BIG) if n.startswith(("ffn", "ple"))])
    tot = _unpack(sum_parts(parts, "sum_small_grads"), [(4, 4, D), (2, Q_RANK), (2, KV_RANK), (32, 48), (1, 16), (LANES,)])
    loss = jnp.sum(tot[5])
    small_g = [lax.dynamic_slice_in_dim(tot[0], me * 128, 128, axis=2), lax.dynamic_slice_in_dim(tot[1], me * 48, 48, axis=1),
               lax.dynamic_slice_in_dim(tot[2], me * 32, 32, axis=1), tot[3], tot[4]]
    small_names = [n for n, _, _ in SMALL_SHARDED] + [n for n, _ in SMALL_REPL]
    small_shapes = [s for _, s, _ in SMALL_SHARDED] + [s for _, s in SMALL_REPL]
    s_rows = _rows(sum(int(np.prod(s)) for s in small_shapes))
    small_out = adamw(_pack([given[n] for n in small_names], s_rows, F32)[None],
                      _pack([given["m_" + n] for n in small_names], s_rows, F32)[None],
                      _pack([given["v_" + n] for n in small_names], s_rows, F32)[None],
                      _pack(small_g, s_rows, F32)[None, None], "adamw_small")
    small_out = [_unpack(o_, small_shapes) for o_ in small_out]

    res = [{}, {}, {}, {}]
    for k in range(4):
        for idx, (n, _, _) in enumerate(BIG):
            res[k][n] = jnp.swapaxes(big_out[idx][k], 1, 2) if n in TRANSPOSED else big_out[idx][k]
        for idx, n in enumerate(small_names):
            res[k][n] = small_out[k][idx]
    return (loss, grad_x, *[res[0][n] for n in WEIGHTS], *[res[1][n] for n in WEIGHTS],
            *[res[2][n] for n in WEIGHTS], *[res[3][n] for n in WEIGHTS])
```
